```python
import math
import jax, jax.numpy as jnp
from jax import lax
import numpy as np

D_MODEL = 1024
BATCH = 8
SEQ = 8192
DEPTH = 4

N_A_LAYERS = DEPTH // 2
N_B_LAYERS = DEPTH - N_A_LAYERS
HEAD_DIM = 128
GDN_HEADS = 6
GDN_WIDTH = GDN_HEADS * HEAD_DIM
CONV_WIDTH = 4
CHUNK = 64
FOX_HEADS = 6
FOX_WIDTH = FOX_HEADS * HEAD_DIM
Q_BLOCK = 128
MEM_LEN = 256
MEM_HEADS = 4
MEM_HEAD_DIM = 64
MEM_WIDTH = MEM_HEADS * MEM_HEAD_DIM
MIX_WIDTH = GDN_WIDTH + MEM_WIDTH
A_IN_WIDTH = 4 * GDN_WIDTH + 2 * GDN_HEADS + MEM_WIDTH
B_IN_WIDTH = FOX_WIDTH + MEM_WIDTH
KV_WIDTH = 2 * FOX_WIDTH + FOX_HEADS
FFN_HIDDEN = 2816
EPS = 1e-6
NEG_INF = -1e30

kernel_name = "yoco_gdn_fox_macaron_memory"


def rmsnorm(x, gain):
    x32 = x.astype(jnp.float32)
    y = x32 * lax.rsqrt(jnp.mean(x32 * x32, axis=-1, keepdims=True) + EPS)
    return (y * gain.astype(jnp.float32)).astype(x.dtype)


def l2norm(x):
    x32 = x.astype(jnp.float32)
    return x32 * lax.rsqrt(jnp.sum(x32 * x32, axis=-1, keepdims=True) + EPS)


def swiglu(x, w_gate_up, w_down):
    gate, up = jnp.split(x @ w_gate_up, 2, axis=-1)
    return (jax.nn.silu(gate) * up) @ w_down


def causal_depthwise_conv(x, w):
    c = x.shape[-1]
    return lax.conv_general_dilated(
        x, w[:, None, :].astype(x.dtype), window_strides=(1,),
        padding=((CONV_WIDTH - 1, 0),), dimension_numbers=("NWC", "WIO", "NWC"),
        feature_group_count=c)


def chunk_gated_delta_rule(q, k, v, g, beta):
    bsz, s, h, dk = q.shape
    dv = v.shape[-1]
    n = s // CHUNK

    def chunks(t):
        return t.reshape(bsz, n, CHUNK, h, -1).transpose(0, 3, 1, 2, 4)

    q, k, v = chunks(q), chunks(k), chunks(v)
    g = g.reshape(bsz, n, CHUNK, h).transpose(0, 3, 1, 2)
    beta = beta.reshape(bsz, n, CHUNK, h).transpose(0, 3, 1, 2)
    g = jnp.cumsum(g, axis=-1)
    causal = jnp.tril(jnp.ones((CHUNK, CHUNK), dtype=bool))
    strict = jnp.tril(jnp.ones((CHUNK, CHUNK), dtype=bool), k=-1)
    gdiff = g[..., :, None] - g[..., None, :]
    decay = jnp.where(causal, jnp.exp(jnp.where(causal, gdiff, 0.0)), 0.0)
    k_beta = k * beta[..., None]
    lower = jnp.where(strict, jnp.einsum("bhnid,bhnjd->bhnij", k_beta, k) * decay, 0.0)
    system = lower + jnp.eye(CHUNK, dtype=jnp.float32)
    rhs = jnp.concatenate([v * beta[..., None], k_beta * jnp.exp(g)[..., None]], axis=-1)
    sol = lax.linalg.triangular_solve(system, rhs, left_side=True, lower=True,
                                      unit_diagonal=True)
    u_vals, w_keys = sol[..., :dv], sol[..., dv:]
    qk = jnp.where(causal, jnp.einsum("bhnid,bhnjd->bhnij", q, k) * decay, 0.0)
    g_last = g[..., -1]
    k_tail = k * jnp.exp(g_last[..., None] - g)[..., None]
    q_head = q * jnp.exp(g)[..., None]
    xs = tuple(jnp.moveaxis(t, 2, 0) for t in (q_head, qk, u_vals, w_keys, k_tail, g_last))

    def step(state, inp):
        q_c, qk_c, u_c, w_c, kt_c, gl_c = inp
        v_new = u_c - jnp.einsum("bhck,bhkv->bhcv", w_c, state)
        out = (jnp.einsum("bhck,bhkv->bhcv", q_c, state)
               + jnp.einsum("bhij,bhjv->bhiv", qk_c, v_new))
        state = (state * jnp.exp(gl_c)[..., None, None]
                 + jnp.einsum("bhck,bhcv->bhkv", kt_c, v_new))
        return state, out

    state0 = jnp.zeros((bsz, h, dk, dv), jnp.float32)
    _, out = lax.scan(step, state0, xs)
    return out.transpose(1, 0, 3, 2, 4).reshape(bsz, s, h, dv)


def gdn_mixer(u, w_in, conv_w, A_log, dt_bias, out_norm):
    bsz, s, _ = u.shape
    proj = u @ w_in
    o0 = 3 * GDN_WIDTH
    o1 = 4 * GDN_WIDTH
    qkv = proj[..., :o0]
    z = proj[..., o0:o1]
    a = proj[..., o1:o1 + GDN_HEADS]
    b = proj[..., o1 + GDN_HEADS:o1 + 2 * GDN_HEADS]
    q_mem = proj[..., o1 + 2 * GDN_HEADS:]
    qkv = jax.nn.silu(causal_depthwise_conv(qkv, conv_w))
    qkv = qkv.reshape(bsz, s, 3, GDN_HEADS, HEAD_DIM)
    q = l2norm(qkv[:, :, 0]) * (HEAD_DIM ** -0.5)
    k = l2norm(qkv[:, :, 1])
    v = qkv[:, :, 2].astype(jnp.float32)
    beta = jax.nn.sigmoid(b.astype(jnp.float32))
    g = -jnp.exp(A_log.astype(jnp.float32)) * jax.nn.softplus(
        a.astype(jnp.float32) + dt_bias.astype(jnp.float32))
    o = chunk_gated_delta_rule(q, k, v, g, beta)
    z = z.reshape(bsz, s, GDN_HEADS, HEAD_DIM).astype(jnp.float32)
    o = rmsnorm(o, out_norm) * jax.nn.silu(z)
    return o.reshape(bsz, s, GDN_WIDTH).astype(u.dtype), q_mem


def shared_fox_kv(h, kv_norm, kv_w, kv_b_f):
    bsz, s, _ = h.shape
    p = rmsnorm(h, kv_norm) @ kv_w
    k = p[..., :FOX_WIDTH].reshape(bsz, s, FOX_HEADS, HEAD_DIM)
    v = p[..., FOX_WIDTH:2 * FOX_WIDTH].reshape(bsz, s, FOX_HEADS, HEAD_DIM)
    log_f = jax.nn.log_sigmoid(p[..., 2 * FOX_WIDTH:].astype(jnp.float32)
                               + kv_b_f.astype(jnp.float32))
    c = jnp.cumsum(log_f, axis=1).transpose(0, 2, 1)
    return k, v, c


def forgetting_attention(q, k, v, c):
    bsz, s, h, d = q.shape
    nb = s // Q_BLOCK
    scale = d ** -0.5
    q_blocks = q.reshape(bsz, nb, Q_BLOCK, h, d).transpose(1, 0, 3, 2, 4)
    c_blocks = c.reshape(bsz, h, nb, Q_BLOCK).transpose(2, 0, 1, 3)
    key_pos = jnp.arange(s)

    def attend(args):
        q_i, c_i, i = args
        logits = jnp.einsum("bhqd,bshd->bhqs", q_i, k).astype(jnp.float32) * scale
        logits = logits + c_i[..., None] - c[:, :, None, :]
        q_pos = i * Q_BLOCK + jnp.arange(Q_BLOCK)
        causal = key_pos[None, :] <= q_pos[:, None]
        p = jax.nn.softmax(jnp.where(causal, logits, NEG_INF), axis=-1)
        return jnp.einsum("bhqs,bshd->bqhd", p.astype(v.dtype), v)

    out = lax.map(attend, (q_blocks, c_blocks, jnp.arange(nb)))
    return out.transpose(1, 0, 2, 3, 4).reshape(bsz, s, h * d).astype(q.dtype)


def memory_attention(q, mem_n, w_kv):
    bsz, s, _ = q.shape
    m = mem_n.shape[1]
    kv = mem_n @ w_kv
    k = kv[..., :MEM_WIDTH].reshape(bsz, m, MEM_HEADS, MEM_HEAD_DIM)
    v = kv[..., MEM_WIDTH:].reshape(bsz, m, MEM_HEADS, MEM_HEAD_DIM)
    q = q.reshape(bsz, s, MEM_HEADS, MEM_HEAD_DIM)
    logits = jnp.einsum("bqhd,bmhd->bhqm", q, k).astype(jnp.float32) * (MEM_HEAD_DIM ** -0.5)
    p = jax.nn.softmax(logits, axis=-1)
    out = jnp.einsum("bhqm,bmhd->bqhd", p.astype(v.dtype), v)
    return out.reshape(bsz, s, MEM_WIDTH)


def _fwd_setup_inputs(seed: int = 0) -> dict:
    key = jax.random.key(seed)
    ks = iter(jax.random.split(key, 32))

    def nrm(shape, scale):
        return scale * jax.random.normal(next(ks), shape, jnp.float32)

    def gain(shape):
        return 1.0 + nrm(shape, 0.02)

    x = nrm((BATCH, SEQ, D_MODEL), 1.0)
    mem = nrm((BATCH, MEM_LEN, D_MODEL), 1.0)
    ffn1_norm = gain((DEPTH, D_MODEL))
    ffn1_w_gate_up = nrm((DEPTH, D_MODEL, 2 * FFN_HIDDEN), D_MODEL ** -0.5)
    ffn1_w_down = nrm((DEPTH, FFN_HIDDEN, D_MODEL), FFN_HIDDEN ** -0.5)
    mix_norm = gain((DEPTH, D_MODEL))
    ffn2_norm = gain((DEPTH, D_MODEL))
    ffn2_w_gate_up = nrm((DEPTH, D_MODEL, 2 * FFN_HIDDEN), D_MODEL ** -0.5)
    ffn2_w_down = nrm((DEPTH, FFN_HIDDEN, D_MODEL), FFN_HIDDEN ** -0.5)
    gdn_w_in = nrm((N_A_LAYERS, D_MODEL, A_IN_WIDTH), D_MODEL ** -0.5)
    gdn_conv = nrm((N_A_LAYERS, CONV_WIDTH, 3 * GDN_WIDTH), CONV_WIDTH ** -0.5)
    gdn_A_log = jnp.log(jax.random.uniform(next(ks), (N_A_LAYERS, GDN_HEADS),
                                           jnp.float32, 1.0, 16.0))
    dt = jnp.exp(jax.random.uniform(next(ks), (N_A_LAYERS, GDN_HEADS), jnp.float32,
                                    math.log(1e-3), math.log(1e-1)))
    gdn_dt_bias = dt + jnp.log(-jnp.expm1(-dt))
    gdn_out_norm = gain((N_A_LAYERS, HEAD_DIM))
    fox_w_in = nrm((N_B_LAYERS, D_MODEL, B_IN_WIDTH), D_MODEL ** -0.5)
    w_out = nrm((DEPTH, MIX_WIDTH, D_MODEL), MIX_WIDTH ** -0.5)
    mem_norm = gain((D_MODEL,))
    mem_w_kv = nrm((DEPTH, D_MODEL, 2 * MEM_WIDTH), D_MODEL ** -0.5)
    kv_norm = gain((D_MODEL,))
    kv_w = nrm((D_MODEL, KV_WIDTH), D_MODEL ** -0.5)
    kv_b_f = 2.0 + nrm((FOX_HEADS,), 0.1)
    final_norm = gain((D_MODEL,))
    return {"x": x, "mem": mem,
            "ffn1_norm": ffn1_norm, "ffn1_w_gate_up": ffn1_w_gate_up, "ffn1_w_down": ffn1_w_down,
            "mix_norm": mix_norm,
            "ffn2_norm": ffn2_norm, "ffn2_w_gate_up": ffn2_w_gate_up, "ffn2_w_down": ffn2_w_down,
            "gdn_w_in": gdn_w_in, "gdn_conv": gdn_conv, "gdn_A_log": gdn_A_log,
            "gdn_dt_bias": gdn_dt_bias, "gdn_out_norm": gdn_out_norm,
            "fox_w_in": fox_w_in, "w_out": w_out,
            "mem_norm": mem_norm, "mem_w_kv": mem_w_kv,
            "kv_norm": kv_norm, "kv_w": kv_w, "kv_b_f": kv_b_f,
            "final_norm": final_norm}


def _fwd_reference(x, mem, ffn1_norm, ffn1_w_gate_up, ffn1_w_down, mix_norm,
              ffn2_norm, ffn2_w_gate_up, ffn2_w_down,
              gdn_w_in, gdn_conv, gdn_A_log, gdn_dt_bias, gdn_out_norm,
              fox_w_in, w_out, mem_norm, mem_w_kv, kv_norm, kv_w, kv_b_f, final_norm):
    bsz, s, _ = x.shape
    mem_n = rmsnorm(mem, mem_norm)
    h = x
    shared_k = shared_v = shared_c = None
    for l in range(DEPTH):
        h = h + 0.5 * swiglu(rmsnorm(h, ffn1_norm[l]), ffn1_w_gate_up[l], ffn1_w_down[l])
        u = rmsnorm(h, mix_norm[l])
        if l < N_A_LAYERS:
            main, q_mem = gdn_mixer(u, gdn_w_in[l], gdn_conv[l], gdn_A_log[l],
                                    gdn_dt_bias[l], gdn_out_norm[l])
        else:
            proj = u @ fox_w_in[l - N_A_LAYERS]
            q_fox = proj[..., :FOX_WIDTH].reshape(bsz, s, FOX_HEADS, HEAD_DIM)
            q_mem = proj[..., FOX_WIDTH:]
            main = forgetting_attention(q_fox, shared_k, shared_v, shared_c)
        mem_out = memory_attention(q_mem, mem_n, mem_w_kv[l])
        h = h + jnp.concatenate([main, mem_out], axis=-1) @ w_out[l]
        h = h + 0.5 * swiglu(rmsnorm(h, ffn2_norm[l]), ffn2_w_gate_up[l], ffn2_w_down[l])
        if l == N_A_LAYERS - 1:
            shared_k, shared_v, shared_c = shared_fox_kv(h, kv_norm, kv_w, kv_b_f)
    return rmsnorm(h, final_norm)


import jax as _jax
import jax.numpy as _jnp

TWIN_FORMAT = 'train_step'
FWD_PARAMS = ['x', 'mem', 'ffn1_norm', 'ffn1_w_gate_up', 'ffn1_w_down', 'mix_norm', 'ffn2_norm', 'ffn2_w_gate_up', 'ffn2_w_down', 'gdn_w_in', 'gdn_conv', 'gdn_A_log', 'gdn_dt_bias', 'gdn_out_norm', 'fox_w_in', 'w_out', 'mem_norm', 'mem_w_kv', 'kv_norm', 'kv_w', 'kv_b_f', 'final_norm']
TWIN_WEIGHTS = ['ffn1_norm', 'ffn1_w_gate_up', 'ffn1_w_down', 'mix_norm', 'ffn2_norm', 'ffn2_w_gate_up', 'ffn2_w_down', 'gdn_w_in', 'gdn_conv', 'gdn_A_log', 'gdn_dt_bias', 'gdn_out_norm', 'fox_w_in', 'w_out', 'mem_norm', 'mem_w_kv', 'kv_norm', 'kv_w', 'kv_b_f', 'final_norm']
TWIN_DIFF_INPUT = 'x'
TWIN_INPUTS = ['x', 'mem', 'ffn1_norm', 'ffn1_w_gate_up', 'ffn1_w_down', 'mix_norm', 'ffn2_norm', 'ffn2_w_gate_up', 'ffn2_w_down', 'gdn_w_in', 'gdn_conv', 'gdn_A_log', 'gdn_dt_bias', 'gdn_out_norm', 'fox_w_in', 'w_out', 'mem_norm', 'mem_w_kv', 'kv_norm', 'kv_w', 'kv_b_f', 'final_norm', 'loss_target', 'm_ffn1_norm', 'm_ffn1_w_gate_up', 'm_ffn1_w_down', 'm_mix_norm', 'm_ffn2_norm', 'm_ffn2_w_gate_up', 'm_ffn2_w_down', 'm_gdn_w_in', 'm_gdn_conv', 'm_gdn_A_log', 'm_gdn_dt_bias', 'm_gdn_out_norm', 'm_fox_w_in', 'm_w_out', 'm_mem_norm', 'm_mem_w_kv', 'm_kv_norm', 'm_kv_w', 'm_kv_b_f', 'm_final_norm', 'v_ffn1_norm', 'v_ffn1_w_gate_up', 'v_ffn1_w_down', 'v_mix_norm', 'v_ffn2_norm', 'v_ffn2_w_gate_up', 'v_ffn2_w_down', 'v_gdn_w_in', 'v_gdn_conv', 'v_gdn_A_log', 'v_gdn_dt_bias', 'v_gdn_out_norm', 'v_fox_w_in', 'v_w_out', 'v_mem_norm', 'v_mem_w_kv', 'v_kv_norm', 'v_kv_w', 'v_kv_b_f', 'v_final_norm']
TWIN_OUTPUTS = ['loss', 'grad_x', 'grad_ffn1_norm', 'grad_ffn1_w_gate_up', 'grad_ffn1_w_down', 'grad_mix_norm', 'grad_ffn2_norm', 'grad_ffn2_w_gate_up', 'grad_ffn2_w_down', 'grad_gdn_w_in', 'grad_gdn_conv', 'grad_gdn_A_log', 'grad_gdn_dt_bias', 'grad_gdn_out_norm', 'grad_fox_w_in', 'grad_w_out', 'grad_mem_norm', 'grad_mem_w_kv', 'grad_kv_norm', 'grad_kv_w', 'grad_kv_b_f', 'grad_final_norm', 'delta_ffn1_norm', 'delta_ffn1_w_gate_up', 'delta_ffn1_w_down', 'delta_mix_norm', 'delta_ffn2_norm', 'delta_ffn2_w_gate_up', 'delta_ffn2_w_down', 'delta_gdn_w_in', 'delta_gdn_conv', 'delta_gdn_A_log', 'delta_gdn_dt_bias', 'delta_gdn_out_norm', 'delta_fox_w_in', 'delta_w_out', 'delta_mem_norm', 'delta_mem_w_kv', 'delta_kv_norm', 'delta_kv_w', 'delta_kv_b_f', 'delta_final_norm', 'new_m_ffn1_norm', 'new_m_ffn1_w_gate_up', 'new_m_ffn1_w_down', 'new_m_mix_norm', 'new_m_ffn2_norm', 'new_m_ffn2_w_gate_up', 'new_m_ffn2_w_down', 'new_m_gdn_w_in', 'new_m_gdn_conv', 'new_m_gdn_A_log', 'new_m_gdn_dt_bias', 'new_m_gdn_out_norm', 'new_m_fox_w_in', 'new_m_w_out', 'new_m_mem_norm', 'new_m_mem_w_kv', 'new_m_kv_norm', 'new_m_kv_w', 'new_m_kv_b_f', 'new_m_final_norm', 'new_v_ffn1_norm', 'new_v_ffn1_w_gate_up', 'new_v_ffn1_w_down', 'new_v_mix_norm', 'new_v_ffn2_norm', 'new_v_ffn2_w_gate_up', 'new_v_ffn2_w_down', 'new_v_gdn_w_in', 'new_v_gdn_conv', 'new_v_gdn_A_log', 'new_v_gdn_dt_bias', 'new_v_gdn_out_norm', 'new_v_fox_w_in', 'new_v_w_out', 'new_v_mem_norm', 'new_v_mem_w_kv', 'new_v_kv_norm', 'new_v_kv_w', 'new_v_kv_b_f', 'new_v_final_norm']
TWIN_LEAF_KINDS = {'loss': 'loss', 'grad_x': 'grad_x', 'grad_ffn1_norm': 'grad_w', 'grad_ffn1_w_gate_up': 'grad_w', 'grad_ffn1_w_down': 'grad_w', 'grad_mix_norm': 'grad_w', 'grad_ffn2_norm': 'grad_w', 'grad_ffn2_w_gate_up': 'grad_w', 'grad_ffn2_w_down': 'grad_w', 'grad_gdn_w_in': 'grad_w', 'grad_gdn_conv': 'grad_w', 'grad_gdn_A_log': 'grad_w', 'grad_gdn_dt_bias': 'grad_w', 'grad_gdn_out_norm': 'grad_w', 'grad_fox_w_in': 'grad_w', 'grad_w_out': 'grad_w', 'grad_mem_norm': 'grad_w', 'grad_mem_w_kv': 'grad_w', 'grad_kv_norm': 'grad_w', 'grad_kv_w': 'grad_w', 'grad_kv_b_f': 'grad_w', 'grad_final_norm': 'grad_w', 'delta_ffn1_norm': 'delta_w', 'delta_ffn1_w_gate_up': 'delta_w', 'delta_ffn1_w_down': 'delta_w', 'delta_mix_norm': 'delta_w', 'delta_ffn2_norm': 'delta_w', 'delta_ffn2_w_gate_up': 'delta_w', 'delta_ffn2_w_down': 'delta_w', 'delta_gdn_w_in': 'delta_w', 'delta_gdn_conv': 'delta_w', 'delta_gdn_A_log': 'delta_w', 'delta_gdn_dt_bias': 'delta_w', 'delta_gdn_out_norm': 'delta_w', 'delta_fox_w_in': 'delta_w', 'delta_w_out': 'delta_w', 'delta_mem_norm': 'delta_w', 'delta_mem_w_kv': 'delta_w', 'delta_kv_norm': 'delta_w', 'delta_kv_w': 'delta_w', 'delta_kv_b_f': 'delta_w', 'delta_final_norm': 'delta_w', 'new_m_ffn1_norm': 'new_m', 'new_m_ffn1_w_gate_up': 'new_m', 'new_m_ffn1_w_down': 'new_m', 'new_m_mix_norm': 'new_m', 'new_m_ffn2_norm': 'new_m', 'new_m_ffn2_w_gate_up': 'new_m', 'new_m_ffn2_w_down': 'new_m', 'new_m_gdn_w_in': 'new_m', 'new_m_gdn_conv': 'new_m', 'new_m_gdn_A_log': 'new_m', 'new_m_gdn_dt_bias': 'new_m', 'new_m_gdn_out_norm': 'new_m', 'new_m_fox_w_in': 'new_m', 'new_m_w_out': 'new_m', 'new_m_mem_norm': 'new_m', 'new_m_mem_w_kv': 'new_m', 'new_m_kv_norm': 'new_m', 'new_m_kv_w': 'new_m', 'new_m_kv_b_f': 'new_m', 'new_m_final_norm': 'new_m', 'new_v_ffn1_norm': 'new_v', 'new_v_ffn1_w_gate_up': 'new_v', 'new_v_ffn1_w_down': 'new_v', 'new_v_mix_norm': 'new_v', 'new_v_ffn2_norm': 'new_v', 'new_v_ffn2_w_gate_up': 'new_v', 'new_v_ffn2_w_down': 'new_v', 'new_v_gdn_w_in': 'new_v', 'new_v_gdn_conv': 'new_v', 'new_v_gdn_A_log': 'new_v', 'new_v_gdn_dt_bias': 'new_v', 'new_v_gdn_out_norm': 'new_v', 'new_v_fox_w_in': 'new_v', 'new_v_w_out': 'new_v', 'new_v_mem_norm': 'new_v', 'new_v_mem_w_kv': 'new_v', 'new_v_kv_norm': 'new_v', 'new_v_kv_w': 'new_v', 'new_v_kv_b_f': 'new_v', 'new_v_final_norm': 'new_v'}


def _forward(args):
    return _fwd_reference(*[args[k] for k in FWD_PARAMS])


def _output_shape():
    def fwd():
        inp = _fwd_setup_inputs(0)
        return _fwd_reference(*[inp[k] for k in FWD_PARAMS])
    out = _jax.eval_shape(fwd)
    return out.shape, out.dtype

N_MICROBATCH = 1
ADAM_LR = 0.001
ADAM_B1 = 0.9
ADAM_B2 = 0.999
ADAM_EPS = 1e-08
ADAM_WD = 0.01
ADAM_STEP = 10
PER_EXAMPLE_BATCH_AXIS = {'x': 0, 'mem': 0, 'loss_target': 0}
SHARED_INPUTS = []
_WEIGHT_DTYPES = {'ffn1_norm': _jnp.float32, 'ffn1_w_gate_up': _jnp.float32, 'ffn1_w_down': _jnp.float32, 'mix_norm': _jnp.float32, 'ffn2_norm': _jnp.float32, 'ffn2_w_gate_up': _jnp.float32, 'ffn2_w_down': _jnp.float32, 'gdn_w_in': _jnp.float32, 'gdn_conv': _jnp.float32, 'gdn_A_log': _jnp.float32, 'gdn_dt_bias': _jnp.float32, 'gdn_out_norm': _jnp.float32, 'fox_w_in': _jnp.float32, 'w_out': _jnp.float32, 'mem_norm': _jnp.float32, 'mem_w_kv': _jnp.float32, 'kv_norm': _jnp.float32, 'kv_w': _jnp.float32, 'kv_b_f': _jnp.float32, 'final_norm': _jnp.float32}
MOMENT_SCALE = {'ffn1_norm': 1.171245e-01, 'ffn1_w_gate_up': 4.899263e-02, 'ffn1_w_down': 8.007254e-02, 'mix_norm': 1.620564e-01, 'ffn2_norm': 9.152919e-02, 'ffn2_w_gate_up': 3.897075e-02, 'ffn2_w_down': 6.368408e-02, 'gdn_w_in': 1.245829e-01, 'gdn_conv': 1.170142e-01, 'gdn_A_log': 4.699750e-01, 'gdn_dt_bias': 4.628594e-01, 'gdn_out_norm': 3.785963e-01, 'fox_w_in': 4.708819e-02, 'w_out': 1.079282e-01, 'mem_norm': 3.641120e-02, 'mem_w_kv': 2.453863e-02, 'kv_norm': 1.154507e-01, 'kv_w': 9.239880e-02, 'kv_b_f': 6.753608e-01, 'final_norm': 6.406261e+01}


def _to_microbatches(a, axis):
    t = _jnp.moveaxis(a, axis, 0)
    t = t.reshape((N_MICROBATCH, t.shape[0] // N_MICROBATCH) + t.shape[1:])
    return _jnp.moveaxis(t, 1, axis + 1)


def setup_inputs(seed: int = 0) -> dict:
    inp = _fwd_setup_inputs(seed)
    key = _jax.random.fold_in(_jax.random.key(seed), 7919)
    shape, _ = _output_shape()
    out = dict(inp)
    out["loss_target"] = _jax.random.normal(_jax.random.fold_in(key, 0), shape, _jnp.float32)
    for i, name in enumerate(TWIN_WEIGHTS):
        w = inp[name].astype(_jnp.float32)
        if MOMENT_SCALE is None:
            s = _jnp.sqrt(_jnp.mean(_jnp.square(w)) + 1e-30)
        else:
            s = MOMENT_SCALE[name]
        km, kv = _jax.random.split(_jax.random.fold_in(key, i + 1))
        out[name] = w
        out["m_" + name] = s * _jax.random.normal(km, w.shape, _jnp.float32)
        out["v_" + name] = (s * s) * _jax.random.uniform(kv, w.shape, _jnp.float32, 0.5, 1.5)
    if N_MICROBATCH > 1:
        for name, axis in PER_EXAMPLE_BATCH_AXIS.items():
            out[name] = _to_microbatches(out[name], axis)
    return {'x': out['x'], 'mem': out['mem'], 'ffn1_norm': out['ffn1_norm'], 'ffn1_w_gate_up': out['ffn1_w_gate_up'], 'ffn1_w_down': out['ffn1_w_down'], 'mix_norm': out['mix_norm'], 'ffn2_norm': out['ffn2_norm'], 'ffn2_w_gate_up': out['ffn2_w_gate_up'], 'ffn2_w_down': out['ffn2_w_down'], 'gdn_w_in': out['gdn_w_in'], 'gdn_conv': out['gdn_conv'], 'gdn_A_log': out['gdn_A_log'], 'gdn_dt_bias': out['gdn_dt_bias'], 'gdn_out_norm': out['gdn_out_norm'], 'fox_w_in': out['fox_w_in'], 'w_out': out['w_out'], 'mem_norm': out['mem_norm'], 'mem_w_kv': out['mem_w_kv'], 'kv_norm': out['kv_norm'], 'kv_w': out['kv_w'], 'kv_b_f': out['kv_b_f'], 'final_norm': out['final_norm'], 'loss_target': out['loss_target'], 'm_ffn1_norm': out['m_ffn1_norm'], 'm_ffn1_w_gate_up': out['m_ffn1_w_gate_up'], 'm_ffn1_w_down': out['m_ffn1_w_down'], 'm_mix_norm': out['m_mix_norm'], 'm_ffn2_norm': out['m_ffn2_norm'], 'm_ffn2_w_gate_up': out['m_ffn2_w_gate_up'], 'm_ffn2_w_down': out['m_ffn2_w_down'], 'm_gdn_w_in': out['m_gdn_w_in'], 'm_gdn_conv': out['m_gdn_conv'], 'm_gdn_A_log': out['m_gdn_A_log'], 'm_gdn_dt_bias': out['m_gdn_dt_bias'], 'm_gdn_out_norm': out['m_gdn_out_norm'], 'm_fox_w_in': out['m_fox_w_in'], 'm_w_out': out['m_w_out'], 'm_mem_norm': out['m_mem_norm'], 'm_mem_w_kv': out['m_mem_w_kv'], 'm_kv_norm': out['m_kv_norm'], 'm_kv_w': out['m_kv_w'], 'm_kv_b_f': out['m_kv_b_f'], 'm_final_norm': out['m_final_norm'], 'v_ffn1_norm': out['v_ffn1_norm'], 'v_ffn1_w_gate_up': out['v_ffn1_w_gate_up'], 'v_ffn1_w_down': out['v_ffn1_w_down'], 'v_mix_norm': out['v_mix_norm'], 'v_ffn2_norm': out['v_ffn2_norm'], 'v_ffn2_w_gate_up': out['v_ffn2_w_gate_up'], 'v_ffn2_w_down': out['v_ffn2_w_down'], 'v_gdn_w_in': out['v_gdn_w_in'], 'v_gdn_conv': out['v_gdn_conv'], 'v_gdn_A_log': out['v_gdn_A_log'], 'v_gdn_dt_bias': out['v_gdn_dt_bias'], 'v_gdn_out_norm': out['v_gdn_out_norm'], 'v_fox_w_in': out['v_fox_w_in'], 'v_w_out': out['v_w_out'], 'v_mem_norm': out['v_mem_norm'], 'v_mem_w_kv': out['v_mem_w_kv'], 'v_kv_norm': out['v_kv_norm'], 'v_kv_w': out['v_kv_w'], 'v_kv_b_f': out['v_kv_b_f'], 'v_final_norm': out['v_final_norm']}


def _loss(weights, diff, rest, loss_target):
    with _jax.named_scope("forward"):
        args = {**rest, TWIN_DIFF_INPUT: diff, **{k: w.astype(_WEIGHT_DTYPES[k]) for k, w in weights.items()}}
        y = _forward(args)
    with _jax.named_scope("loss_head"):
        err = _jnp.square(y.astype(_jnp.float32) - loss_target)
        return 0.5 * _jnp.sum(_jnp.mean(err, axis=-1)) if err.ndim else 0.5 * err


def _adamw(w, g, m, v):
    m = ADAM_B1 * m + (1.0 - ADAM_B1) * g
    v = ADAM_B2 * v + (1.0 - ADAM_B2) * _jnp.square(g)
    m_hat = m / (1.0 - ADAM_B1 ** ADAM_STEP)
    v_hat = v / (1.0 - ADAM_B2 ** ADAM_STEP)
    delta = -ADAM_LR * (m_hat / (_jnp.sqrt(v_hat) + ADAM_EPS) + ADAM_WD * w)
    return delta, m, v


def reference(x, mem, ffn1_norm, ffn1_w_gate_up, ffn1_w_down, mix_norm, ffn2_norm, ffn2_w_gate_up, ffn2_w_down, gdn_w_in, gdn_conv, gdn_A_log, gdn_dt_bias, gdn_out_norm, fox_w_in, w_out, mem_norm, mem_w_kv, kv_norm, kv_w, kv_b_f, final_norm, loss_target, m_ffn1_norm, m_ffn1_w_gate_up, m_ffn1_w_down, m_mix_norm, m_ffn2_norm, m_ffn2_w_gate_up, m_ffn2_w_down, m_gdn_w_in, m_gdn_conv, m_gdn_A_log, m_gdn_dt_bias, m_gdn_out_norm, m_fox_w_in, m_w_out, m_mem_norm, m_mem_w_kv, m_kv_norm, m_kv_w, m_kv_b_f, m_final_norm, v_ffn1_norm, v_ffn1_w_gate_up, v_ffn1_w_down, v_mix_norm, v_ffn2_norm, v_ffn2_w_gate_up, v_ffn2_w_down, v_gdn_w_in, v_gdn_conv, v_gdn_A_log, v_gdn_dt_bias, v_gdn_out_norm, v_fox_w_in, v_w_out, v_mem_norm, v_mem_w_kv, v_kv_norm, v_kv_w, v_kv_b_f, v_final_norm):
    given = dict(x=x, mem=mem, ffn1_norm=ffn1_norm, ffn1_w_gate_up=ffn1_w_gate_up, ffn1_w_down=ffn1_w_down, mix_norm=mix_norm, ffn2_norm=ffn2_norm, ffn2_w_gate_up=ffn2_w_gate_up, ffn2_w_down=ffn2_w_down, gdn_w_in=gdn_w_in, gdn_conv=gdn_conv, gdn_A_log=gdn_A_log, gdn_dt_bias=gdn_dt_bias, gdn_out_norm=gdn_out_norm, fox_w_in=fox_w_in, w_out=w_out, mem_norm=mem_norm, mem_w_kv=mem_w_kv, kv_norm=kv_norm, kv_w=kv_w, kv_b_f=kv_b_f, final_norm=final_norm, loss_target=loss_target, m_ffn1_norm=m_ffn1_norm, m_ffn1_w_gate_up=m_ffn1_w_gate_up, m_ffn1_w_down=m_ffn1_w_down, m_mix_norm=m_mix_norm, m_ffn2_norm=m_ffn2_norm, m_ffn2_w_gate_up=m_ffn2_w_gate_up, m_ffn2_w_down=m_ffn2_w_down, m_gdn_w_in=m_gdn_w_in, m_gdn_conv=m_gdn_conv, m_gdn_A_log=m_gdn_A_log, m_gdn_dt_bias=m_gdn_dt_bias, m_gdn_out_norm=m_gdn_out_norm, m_fox_w_in=m_fox_w_in, m_w_out=m_w_out, m_mem_norm=m_mem_norm, m_mem_w_kv=m_mem_w_kv, m_kv_norm=m_kv_norm, m_kv_w=m_kv_w, m_kv_b_f=m_kv_b_f, m_final_norm=m_final_norm, v_ffn1_norm=v_ffn1_norm, v_ffn1_w_gate_up=v_ffn1_w_gate_up, v_ffn1_w_down=v_ffn1_w_down, v_mix_norm=v_mix_norm, v_ffn2_norm=v_ffn2_norm, v_ffn2_w_gate_up=v_ffn2_w_gate_up, v_ffn2_w_down=v_ffn2_w_down, v_gdn_w_in=v_gdn_w_in, v_gdn_conv=v_gdn_conv, v_gdn_A_log=v_gdn_A_log, v_gdn_dt_bias=v_gdn_dt_bias, v_gdn_out_norm=v_gdn_out_norm, v_fox_w_in=v_fox_w_in, v_w_out=v_w_out, v_mem_norm=v_mem_norm, v_mem_w_kv=v_mem_w_kv, v_kv_norm=v_kv_norm, v_kv_w=v_kv_w, v_kv_b_f=v_kv_b_f, v_final_norm=v_final_norm)
    weights = {n: given[n] for n in TWIN_WEIGHTS}
    shared = {n: given[n] for n in SHARED_INPUTS}
    per_example = {n: given[n] for n in ['x', 'mem']}
    grad_fn = _jax.value_and_grad(_loss, argnums=(0, 1))

    def one_microbatch(ex, loss_target):
        ex = dict(ex)
        diff = ex.pop(TWIN_DIFF_INPUT)
        return grad_fn(weights, diff, {**shared, **ex}, loss_target)

    if N_MICROBATCH == 1:
        loss, (grad_w, grad_x) = one_microbatch(per_example, given["loss_target"])
    else:
        def body(carry, xs):
            loss_sum, grad_sum = carry
            l_k, (gw_k, gx_k) = one_microbatch(xs[0], xs[1])
            with _jax.named_scope("update"):
                return (loss_sum + l_k, _jax.tree.map(_jnp.add, grad_sum, gw_k)), gx_k

        init = (_jnp.zeros((), _jnp.float32), _jax.tree.map(_jnp.zeros_like, weights))
        (loss, grad_w), grad_x = _jax.lax.scan(body, init, (per_example, given["loss_target"]))
    with _jax.named_scope("update"):
        delta_w, new_m, new_v = {}, {}, {}
        for n in TWIN_WEIGHTS:
            delta_w[n], new_m[n], new_v[n] = _adamw(weights[n], grad_w[n], given["m_" + n], given["v_" + n])
    return (loss, grad_x, *[grad_w[n] for n in TWIN_WEIGHTS], *[delta_w[n] for n in TWIN_WEIGHTS],
            *[new_m[n] for n in TWIN_WEIGHTS], *[new_v[n] for n in TWIN_WEIGHTS])
```

```python
import jax
import jax.numpy as jnp
from jax import lax
from jax.experimental import pallas as pl
from jax.experimental.pallas import tpu as pltpu

F32, BF16 = jnp.float32, jnp.bfloat16
HI = lax.Precision.HIGHEST
MESH = pl.DeviceIdType.MESH

VMEM_LIMIT_BYTES = 48 * 1024 * 1024
LANES = 128
EPS = 1e-6
NEG_INF = -1e30

D_MODEL = 1024
HEAD_DIM = 128
GDN_HEADS = 6
GDN_WIDTH = GDN_HEADS * HEAD_DIM
FOX_HEADS = 6
FOX_WIDTH = FOX_HEADS * HEAD_DIM
MEM_HEADS = 4
MEM_HEAD_DIM = 64
MEM_WIDTH = MEM_HEADS * MEM_HEAD_DIM
FFN_HIDDEN = 2816
CONV_WIDTH = 4
GDN_CHUNK = 128
N_CHIPS = 4
N_DEV = 8

ADAM_LR, ADAM_B1, ADAM_B2, ADAM_EPS, ADAM_WD, ADAM_STEP = 0.001, 0.9, 0.999, 1e-08, 0.01, 10


def _pcall(body, *, name, out_shape, grid=(), in_specs=None, out_specs=None, scratch=(), sem=None):
    params = dict(vmem_limit_bytes=VMEM_LIMIT_BYTES)
    if sem is not None:
        params["dimension_semantics"] = sem
    kw = dict(grid=grid, in_specs=in_specs, out_specs=out_specs) if grid else {}
    return pl.pallas_call(body, name=name, out_shape=out_shape, scratch_shapes=list(scratch),
                          compiler_params=pltpu.CompilerParams(**params), **kw)


def _pick(n, cands):
    for c in cands:
        if n % c == 0:
            return c
    return n


def _make_dot(dtype, precision):
    def raw(a, b, dims):
        return lax.dot_general(a.astype(dtype), b.astype(dtype), (dims, ((), ())),
                               precision=precision, preferred_element_type=F32)

    @jax.custom_vjp
    def dot(a, b):
        return raw(a, b, ((1,), (0,)))

    def fwd(a, b):
        return dot(a, b), (a, b)

    def bwd(resid, ct):
        a, b = resid
        return raw(ct, b, ((1,), (1,))).astype(a.dtype), raw(a, ct, ((0,), (0,))).astype(b.dtype)

    dot.defvjp(fwd, bwd)
    dot.nt = lambda a, b: raw(a, b, ((1,), (1,)))
    dot.tn = lambda a, b: raw(a, b, ((0,), (0,)))
    return dot


bdot = _make_dot(BF16, None)
fdot = _make_dot(F32, HI)


def _sigmoid(x):
    return jax.nn.sigmoid(x)


def _silu(x):
    return x * _sigmoid(x)


def _softplus(x):
    return jnp.maximum(x, 0.0) + jnp.log(1.0 + jnp.exp(-jnp.abs(x)))


def _log_sigmoid(x):
    return -_softplus(-x)


def _iota2(shape, dim):
    return lax.broadcasted_iota(jnp.int32, shape, dim)


def mm(a, b, *, ta=False, tb=False, out_dtype=F32, scale=1.0, res=None, name):
    (K, M) = a.shape if ta else a.shape[::-1]
    (N, Kb) = b.shape if tb else b.shape[::-1]
    assert K == Kb, (a.shape, b.shape, ta, tb)
    tm = _pick(M, (1024, 1408, 512, 256, 128)) if ta else _pick(M, (512, 256, 128))
    tn = _pick(N, (1024, 1408, 768, 512, 384, 256, 128))
    tk = _pick(K, (512, 256, 128)) if ta else _pick(K, (1024, 1408, 512, 256, 128))
    nk = K // tk
    dims = (((0 if ta else 1,), (1 if tb else 0,)), ((), ()))

    def body(a_ref, b_ref, *rest):
        o_ref, acc = rest[-2], rest[-1]
        k = pl.program_id(2)

        @pl.when(k == 0)
        def _():
            acc[...] = jnp.zeros_like(acc)

        acc[...] += lax.dot_general(a_ref[...].astype(BF16), b_ref[...].astype(BF16), dims,
                                    preferred_element_type=F32)

        @pl.when(k == nk - 1)
        def _():
            out = acc[...] * scale
            if res is not None:
                out = out + rest[0][...].astype(F32)
            o_ref[...] = out.astype(o_ref.dtype)

    a_spec = pl.BlockSpec((tk, tm), lambda i, j, k: (k, i)) if ta else pl.BlockSpec((tm, tk), lambda i, j, k: (i, k))
    b_spec = pl.BlockSpec((tn, tk), lambda i, j, k: (j, k)) if tb else pl.BlockSpec((tk, tn), lambda i, j, k: (k, j))
    o_spec = pl.BlockSpec((tm, tn), lambda i, j, k: (i, j))
    ins, specs = [a, b], [a_spec, b_spec]
    if res is not None:
        ins.append(res)
        specs.append(o_spec)
    return _pcall(body, name=name, out_shape=jax.ShapeDtypeStruct((M, N), out_dtype),
                  grid=(M // tm, N // tn, nk), in_specs=specs, out_specs=o_spec,
                  scratch=[pltpu.VMEM((tm, tn), F32)], sem=("parallel", "parallel", "arbitrary"))(*ins)


def _row_spec(r, tile):
    if isinstance(r, tuple):
        arr, width, col = r
        return arr, pl.BlockSpec((tile, width), lambda i, col=col: (i, col))
    return r, pl.BlockSpec((tile, r.shape[1]), lambda i: (i, 0))


def _const_spec(c):
    return pl.BlockSpec(c.shape, lambda i: (0,) * c.ndim)


def rowwise(fn, rows, consts, outs, *, tile, name):
    arrs, specs = zip(*[_row_spec(r, tile) for r in rows])
    n_rows = arrs[0].shape[0]
    tile = min(tile, n_rows)
    n_in = len(rows) + len(consts)

    def body(*refs):
        res = fn(*[r[...] for r in refs[:n_in]])
        for o_ref, o in zip(refs[n_in:], res):
            o_ref[...] = o.astype(o_ref.dtype)

    arrs, specs = zip(*[_row_spec(r, tile) for r in rows])
    return _pcall(body, name=name,
                  out_shape=[jax.ShapeDtypeStruct((n_rows, w), dt) for w, dt in outs],
                  grid=(n_rows // tile,),
                  in_specs=list(specs) + [_const_spec(c) for c in consts],
                  out_specs=[pl.BlockSpec((tile, w), lambda i: (i, 0)) for w, _ in outs],
                  sem=("parallel",))(*arrs, *consts)


def rowwise_bwd(fn, rows, consts, cts, *, tile, name, row_grads, const_grads, add=None):
    arrs, _ = zip(*[_row_spec(r, tile) for r in rows])
    n_rows = arrs[0].shape[0]
    tile = min(tile, n_rows)
    arrs, specs = zip(*[_row_spec(r, tile) for r in rows])
    ct_arrs, ct_specs = zip(*[_row_spec(c, tile) for c in cts])
    add = add or {}
    add_idx = sorted(add)
    add_arrs, add_specs = (zip(*[_row_spec(add[i], tile) for i in add_idx]) if add_idx else ((), ()))
    nr, nc, nct, na = len(rows), len(consts), len(cts), len(add_idx)
    want_rows = [i for i, d in enumerate(row_grads) if d is not None]
    want_consts = [i for i, w in enumerate(const_grads) if w]

    def body(*refs):
        row_v = [r[...] for r in refs[:nr]]
        const_v = [r[...] for r in refs[nr:nr + nc]]
        ct_v = [r[...] for r in refs[nr + nc:nr + nc + nct]]
        add_v = {i: refs[nr + nc + nct + j][...] for j, i in enumerate(add_idx)}
        out_refs = refs[nr + nc + nct + na:]
        res, vjp = jax.vjp(fn, *row_v, *const_v)
        grads = vjp(tuple(c.astype(o.dtype) for c, o in zip(ct_v, res)))
        for o_ref, i in zip(out_refs, want_rows):
            g = grads[i].astype(F32)
            if i in add_v:
                g = g + add_v[i].astype(F32)
            o_ref[...] = g.astype(o_ref.dtype)
        first = pl.program_id(0) == 0
        for o_ref, i in zip(out_refs[len(want_rows):], want_consts):
            g = grads[nr + i].astype(F32)

            @pl.when(first)
            def _(o_ref=o_ref, g=g):
                o_ref[...] = g

            @pl.when(jnp.logical_not(first))
            def _(o_ref=o_ref, g=g):
                o_ref[...] += g

    def width(r):
        return r[1] if isinstance(r, tuple) else r.shape[1]

    out_shape = [jax.ShapeDtypeStruct((n_rows, width(rows[i])), row_grads[i]) for i in want_rows]
    out_shape += [jax.ShapeDtypeStruct(consts[i].shape, F32) for i in want_consts]
    out_specs = [pl.BlockSpec((tile, width(rows[i])), lambda i_: (i_, 0)) for i in want_rows]
    out_specs += [_const_spec(consts[i]) for i in want_consts]
    return _pcall(body, name=name, out_shape=out_shape, grid=(n_rows // tile,),
                  in_specs=list(specs) + [_const_spec(c) for c in consts] + list(ct_specs) + list(add_specs),
                  out_specs=out_specs, sem=("arbitrary",))(*arrs, *consts, *ct_arrs, *add_arrs)


def f_rmsnorm(x, g):
    x = x.astype(F32)
    return (x * lax.rsqrt(jnp.mean(x * x, axis=-1, keepdims=True) + EPS) * g,)


def f_swiglu(gu):
    gu = gu.astype(F32)
    return (_silu(gu[:, :FFN_HIDDEN]) * gu[:, FFN_HIDDEN:],)


def _head_sel(first_lane):
    r, c = _iota2((LANES, GDN_WIDTH), 0), _iota2((LANES, GDN_WIDTH), 1)
    return (r == c // HEAD_DIM + first_lane).astype(F32)


def _tri(n, strict=False):
    r, c = _iota2((n, n), 0), _iota2((n, n), 1)
    return r > c if strict else r >= c


def f_gdn_pre(xc, ab, a_log, dt_bias):
    s = _silu(xc.astype(F32))
    qs, ks = [], []
    for h in range(GDN_HEADS):
        qh = s[:, h * HEAD_DIM:(h + 1) * HEAD_DIM]
        kh = s[:, GDN_WIDTH + h * HEAD_DIM:GDN_WIDTH + (h + 1) * HEAD_DIM]
        qs.append(qh * lax.rsqrt(jnp.sum(qh * qh, axis=-1, keepdims=True) + EPS) * (HEAD_DIM ** -0.5))
        ks.append(kh * lax.rsqrt(jnp.sum(kh * kh, axis=-1, keepdims=True) + EPS))
    q, k = jnp.concatenate(qs, axis=1), jnp.concatenate(ks, axis=1)
    v = s[:, 2 * GDN_WIDTH:]
    ab = ab.astype(F32)
    g = -jnp.exp(a_log) * _softplus(ab + dt_bias)
    gc = fdot(_tri(GDN_CHUNK).astype(F32), fdot(g, _head_sel(0)))
    beta = fdot(_sigmoid(ab), _head_sel(GDN_HEADS))
    return q, k, v, gc, beta


def f_gdn_intra(q, k, v, gc, beta):
    C = GDN_CHUNK
    causal, strict = _tri(C), _tri(C, strict=True)
    last_row = (_iota2((C, C), 1) == C - 1).astype(F32)
    outs = [[] for _ in range(5)]
    for h in range(GDN_HEADS):
        sl = slice(h * HEAD_DIM, (h + 1) * HEAD_DIM)
        qh, kh, vh, gh, bh = q[:, sl], k[:, sl], v[:, sl], gc[:, sl], beta[:, sl]
        gdiff = gh - gh.T
        decay = jnp.where(causal, jnp.exp(jnp.where(causal, gdiff, 0.0)), 0.0)
        kb = kh * bh
        neg_lower = jnp.where(strict, -(fdot(kb, kh.T) * decay), 0.0)
        sol = jnp.concatenate([vh * bh, kb * jnp.exp(gh)], axis=1)
        power = neg_lower
        for step in range(7):
            sol = sol + fdot(power, sol)
            if step < 6:
                power = fdot(power, power)
        qk = jnp.where(causal, fdot(qh, kh.T) * decay, 0.0)
        g_last = fdot(last_row, gh)
        for lst, val in zip(outs, (sol[:, :HEAD_DIM], sol[:, HEAD_DIM:], qk, kh * jnp.exp(g_last - gh), qh * jnp.exp(gh))):
            lst.append(val)
    return tuple(jnp.concatenate(o, axis=1) for o in outs)


def f_gdn_post(o, z, gain):
    z = z.astype(F32)
    parts = []
    for h in range(GDN_HEADS):
        oh = o[:, h * HEAD_DIM:(h + 1) * HEAD_DIM]
        parts.append(oh * lax.rsqrt(jnp.mean(oh * oh, axis=-1, keepdims=True) + EPS) * gain)
    return (jnp.concatenate(parts, axis=1) * _silu(z),)


def f_mem_attn(q, k, v):
    q = q.astype(F32)
    lane_head = _iota2((1, MEM_WIDTH), 1) // MEM_HEAD_DIM
    out = jnp.zeros(q.shape, F32)
    kt = k.astype(F32).T
    for h in range(MEM_HEADS):
        mask = (lane_head == h).astype(F32)
        logits = bdot(q * mask, kt) * (MEM_HEAD_DIM ** -0.5)
        p = jnp.exp(logits - jnp.max(logits, axis=-1, keepdims=True))
        p = p / jnp.sum(p, axis=-1, keepdims=True)
        out = out + bdot(p, v) * mask
    return (out,)


def f_loss(y, t):
    d = y - t
    return (d * d,)


def conv_fwd(proj, w8, *, width, tile=512):
    n_rows = proj.shape[0]
    tile = min(tile, n_rows)

    def body(x_ref, halo_ref, w_ref, o_ref):
        i = pl.program_id(0)
        halo = jnp.where(i > 0, halo_ref[...].astype(F32), 0.0)
        xs = jnp.concatenate([halo, x_ref[...].astype(F32)], axis=0)
        acc = xs[8:] * w_ref[3:4, :]
        for j in range(CONV_WIDTH - 1):
            acc = acc + pltpu.roll(xs, CONV_WIDTH - 1 - j, 0)[8:] * w_ref[j:j + 1, :]
        o_ref[...] = acc

    return _pcall(body, name="gdn_conv_fwd", out_shape=jax.ShapeDtypeStruct((n_rows, width), F32),
                  grid=(n_rows // tile,),
                  in_specs=[pl.BlockSpec((tile, width), lambda i: (i, 0)),
                            pl.BlockSpec((8, width), lambda i: (jnp.maximum(i * (tile // 8) - 1, 0), 0)),
                            pl.BlockSpec((8, width), lambda i: (0, 0))],
                  out_specs=pl.BlockSpec((tile, width), lambda i: (i, 0)), sem=("parallel",))(proj, proj, w8)


def conv_bwd(proj, w8, dy, *, width, tile=512):
    n_rows = proj.shape[0]
    tile = min(tile, n_rows)
    n = n_rows // tile

    def body(x_ref, xhalo_ref, w_ref, dy_ref, dyhalo_ref, dx_ref, dw_ref):
        i = pl.program_id(0)
        dy = dy_ref[...]
        after = jnp.where(i < n - 1, dyhalo_ref[...], 0.0)
        ds = jnp.concatenate([dy, after], axis=0)
        dx = dy * w_ref[3:4, :]
        for j in range(CONV_WIDTH - 1):
            shift = CONV_WIDTH - 1 - j
            dx = dx + pltpu.roll(ds, tile + 8 - shift, 0)[:tile] * w_ref[j:j + 1, :]
        dx_ref[...] = dx.astype(dx_ref.dtype)
        halo = jnp.where(i > 0, xhalo_ref[...].astype(F32), 0.0)
        xs = jnp.concatenate([halo, x_ref[...].astype(F32)], axis=0)
        rows = [jnp.sum(dy * pltpu.roll(xs, CONV_WIDTH - 1 - j, 0)[8:], axis=0, keepdims=True)
                for j in range(CONV_WIDTH - 1)]
        rows.append(jnp.sum(dy * xs[8:], axis=0, keepdims=True))
        dw = jnp.concatenate(rows + [jnp.zeros((8 - CONV_WIDTH, width), F32)], axis=0)

        @pl.when(i == 0)
        def _():
            dw_ref[...] = dw

        @pl.when(i > 0)
        def _():
            dw_ref[...] += dw

    t8 = tile // 8
    return _pcall(body, name="gdn_conv_bwd",
                  out_shape=[jax.ShapeDtypeStruct((n_rows, width), BF16), jax.ShapeDtypeStruct((8, width), F32)],
                  grid=(n,),
                  in_specs=[pl.BlockSpec((tile, width), lambda i: (i, 0)),
                            pl.BlockSpec((8, width), lambda i: (jnp.maximum(i * t8 - 1, 0), 0)),
                            pl.BlockSpec((8, width), lambda i: (0, 0)),
                            pl.BlockSpec((tile, width), lambda i: (i, 0)),
                            pl.BlockSpec((8, width), lambda i: (jnp.minimum((i + 1) * t8, n * t8 - 1), 0))],
                  out_specs=[pl.BlockSpec((tile, width), lambda i: (i, 0)), pl.BlockSpec((8, width), lambda i: (0, 0))],
                  sem=("arbitrary",))(proj, proj, w8, dy, dy)


def gdn_scan_fwd(u, w, qk, kt, qh, gc):
    n_rows = u.shape[0]
    C, n = GDN_CHUNK, u.shape[0] // GDN_CHUNK

    def body(u_ref, w_ref, qk_ref, kt_ref, qh_ref, gc_ref, o_ref, vn_ref, sin_ref, st):
        @pl.when(pl.program_id(0) == 0)
        def _():
            st[...] = jnp.zeros_like(st)

        sin_ref[0] = st[...]
        for h in range(GDN_HEADS):
            sl = slice(h * HEAD_DIM, (h + 1) * HEAD_DIM)
            s = st[sl, :]
            v_new = u_ref[:, sl] - fdot(w_ref[:, sl], s)
            o_ref[:, sl] = fdot(qh_ref[:, sl], s) + fdot(qk_ref[:, sl], v_new)
            vn_ref[:, sl] = v_new
            st[sl, :] = s * jnp.exp(gc_ref[C - 1:C, sl]) + fdot.tn(kt_ref[:, sl], v_new)

    blk = pl.BlockSpec((C, GDN_WIDTH), lambda i: (i, 0))
    return _pcall(body, name="gdn_scan_fwd",
                  out_shape=[jax.ShapeDtypeStruct((n_rows, GDN_WIDTH), F32), jax.ShapeDtypeStruct((n_rows, GDN_WIDTH), F32),
                             jax.ShapeDtypeStruct((n, GDN_WIDTH, HEAD_DIM), F32)],
                  grid=(n,), in_specs=[blk] * 6,
                  out_specs=[blk, blk, pl.BlockSpec((1, GDN_WIDTH, HEAD_DIM), lambda i: (i, 0, 0))],
                  scratch=[pltpu.VMEM((GDN_WIDTH, HEAD_DIM), F32)], sem=("arbitrary",))(u, w, qk, kt, qh, gc)


def gdn_scan_bwd(do, w, qk, kt, qh, gc, vn, sin):
    n_rows = do.shape[0]
    C, n = GDN_CHUNK, do.shape[0] // GDN_CHUNK

    def body(do_ref, w_ref, qk_ref, kt_ref, qh_ref, gc_ref, vn_ref, sin_ref,
             du_ref, dw_ref, dqk_ref, dkt_ref, dqh_ref, dgl_ref, dst):
        @pl.when(pl.program_id(0) == 0)
        def _():
            dst[...] = jnp.zeros_like(dst)

        for h in range(GDN_HEADS):
            sl = slice(h * HEAD_DIM, (h + 1) * HEAD_DIM)
            s, ds_out, d_o, v_new = sin_ref[0, sl, :], dst[sl, :], do_ref[:, sl], vn_ref[:, sl]
            e = jnp.exp(gc_ref[C - 1:C, sl])
            dvn = fdot.tn(qk_ref[:, sl], d_o) + fdot(kt_ref[:, sl], ds_out)
            du_ref[:, sl] = dvn
            dw_ref[:, sl] = -fdot.nt(dvn, s)
            dqk_ref[:, sl] = fdot.nt(d_o, v_new)
            dkt_ref[:, sl] = fdot.nt(v_new, ds_out)
            dqh_ref[:, sl] = fdot.nt(d_o, s)
            dgl = jnp.sum(ds_out * s, axis=0, keepdims=True) * e
            dgl_ref[:, sl] = jnp.broadcast_to(dgl, (8, HEAD_DIM))
            dst[sl, :] = fdot.tn(qh_ref[:, sl], d_o) + e * ds_out - fdot.tn(w_ref[:, sl], dvn)

    blk = pl.BlockSpec((C, GDN_WIDTH), lambda i: (n - 1 - i, 0))
    row = jax.ShapeDtypeStruct((n_rows, GDN_WIDTH), F32)
    return _pcall(body, name="gdn_scan_bwd",
                  out_shape=[row] * 5 + [jax.ShapeDtypeStruct((n * 8, GDN_WIDTH), F32)],
                  grid=(n,), in_specs=[blk] * 7 + [pl.BlockSpec((1, GDN_WIDTH, HEAD_DIM), lambda i: (n - 1 - i, 0, 0))],
                  out_specs=[blk] * 5 + [pl.BlockSpec((8, GDN_WIDTH), lambda i: (n - 1 - i, 0))],
                  scratch=[pltpu.VMEM((GDN_WIDTH, HEAD_DIM), F32)], sem=("arbitrary",))(do, w, qk, kt, qh, gc, vn, sin)


def gdn_intra_bwd(q, k, v, gc, beta, cts, dgl):
    n_rows = q.shape[0]
    C = GDN_CHUNK

    def body(*refs):
        ins = [r[...] for r in refs[:5]]
        ct = tuple(r[...] for r in refs[5:10])
        dgl_v = refs[10][...]
        _, vjp = jax.vjp(f_gdn_intra, *ins)
        grads = list(vjp(ct))
        last = _iota2((C, GDN_WIDTH), 0) == C - 1
        grads[3] = grads[3] + jnp.where(last, jnp.broadcast_to(dgl_v[0:1, :], (C, GDN_WIDTH)), 0.0)
        for o_ref, g in zip(refs[11:], grads):
            o_ref[...] = g

    blk = pl.BlockSpec((C, GDN_WIDTH), lambda i: (i, 0))
    return _pcall(body, name="gdn_intra_bwd", out_shape=[jax.ShapeDtypeStruct((n_rows, GDN_WIDTH), F32)] * 5,
                  grid=(n_rows // C,), in_specs=[blk] * 10 + [pl.BlockSpec((8, GDN_WIDTH), lambda i: (i, 0))],
                  out_specs=[blk] * 5, sem=("parallel",))(q, k, v, gc, beta, *cts, dgl)


def fox_gate_fwd(f, b_f):
    n_rows = f.shape[0]
    T = LANES

    def body(f_ref, b_ref, cb_ref, crow_ref, carry):
        @pl.when(pl.program_id(0) == 0)
        def _():
            carry[...] = jnp.zeros_like(carry)

        c = fdot(_tri(T).astype(F32), _log_sigmoid(f_ref[...] + b_ref[...])) + carry[...]
        carry[...] = c[T - 1:T, :]
        cb_ref[...] = fdot(c, _head_sel(0))
        ct = c.T
        for h in range(FOX_HEADS):
            crow_ref[h] = ct[h:h + 1, :]

    return _pcall(body, name="fox_gate_fwd",
                  out_shape=[jax.ShapeDtypeStruct((n_rows, FOX_WIDTH), F32), jax.ShapeDtypeStruct((FOX_HEADS, 1, n_rows), F32)],
                  grid=(n_rows // T,),
                  in_specs=[pl.BlockSpec((T, LANES), lambda i: (i, 0)), pl.BlockSpec((1, LANES), lambda i: (0, 0))],
                  out_specs=[pl.BlockSpec((T, FOX_WIDTH), lambda i: (i, 0)), pl.BlockSpec((FOX_HEADS, 1, T), lambda i: (0, 0, i))],
                  scratch=[pltpu.VMEM((1, LANES), F32)], sem=("arbitrary",))(f, b_f)


def fox_gate_bwd(f, b_f, dcrow, dcb):
    n_rows = f.shape[0]
    T = LANES
    n = n_rows // T

    def body(f_ref, b_ref, dc_ref, dcb_ref, df_ref, db_ref, carry):
        i = pl.program_id(0)

        @pl.when(i == 0)
        def _():
            carry[...] = jnp.zeros_like(carry)

        rows = [dc_ref[h] for h in range(FOX_HEADS)] + [jnp.zeros((T - FOX_HEADS, T), F32)]
        first_lane = (_iota2((FOX_WIDTH, LANES), 0) == _iota2((FOX_WIDTH, LANES), 1) * HEAD_DIM).astype(F32)
        dc = jnp.concatenate(rows, axis=0).T + fdot(dcb_ref[...], first_lane)
        dlog = fdot.tn(_tri(T).astype(F32), dc) + carry[...]
        carry[...] = dlog[0:1, :]
        df = dlog * (1.0 - _sigmoid(f_ref[...] + b_ref[...]))
        df_ref[...] = df
        db = jnp.sum(df, axis=0, keepdims=True)

        @pl.when(i == 0)
        def _():
            db_ref[...] = db

        @pl.when(i > 0)
        def _():
            db_ref[...] += db

    return _pcall(body, name="fox_gate_bwd",
                  out_shape=[jax.ShapeDtypeStruct((n_rows, LANES), F32), jax.ShapeDtypeStruct((1, LANES), F32)],
                  grid=(n,),
                  in_specs=[pl.BlockSpec((T, LANES), lambda i: (n - 1 - i, 0)), pl.BlockSpec((1, LANES), lambda i: (0, 0)),
                            pl.BlockSpec((FOX_HEADS, 1, T), lambda i: (0, 0, n - 1 - i)),
                            pl.BlockSpec((T, FOX_WIDTH), lambda i: (n - 1 - i, 0))],
                  out_specs=[pl.BlockSpec((T, LANES), lambda i: (n - 1 - i, 0)), pl.BlockSpec((1, LANES), lambda i: (0, 0))],
                  scratch=[pltpu.VMEM((1, LANES), F32)], sem=("arbitrary",))(f, b_f, dcrow, dcb)


def _fox_tile(n_rows):
    return min(512, n_rows)


def _fox_logits(q, k, ccol, crow, diag, tq, tk):
    s = bdot.nt(q, k) * (HEAD_DIM ** -0.5) + jnp.tile(ccol, (1, tk // LANES)) - crow
    causal = _iota2((tq, tk), 0) >= _iota2((tq, tk), 1)
    return jnp.where(jnp.logical_or(jnp.logical_not(diag), causal), s, NEG_INF)


def fox_fwd(q, kv, cb, crow):
    n_rows = kv.shape[0]
    t = _fox_tile(n_rows)
    n = n_rows // t

    def body(q_ref, k_ref, v_ref, cb_ref, crow_ref, o_ref, lse_ref, m_sc, l_sc, acc):
        i, j = pl.program_id(1), pl.program_id(2)

        @pl.when(j == 0)
        def _():
            m_sc[...] = jnp.full_like(m_sc, NEG_INF)
            l_sc[...] = jnp.zeros_like(l_sc)
            acc[...] = jnp.zeros_like(acc)

        @pl.when(j <= i)
        def _():
            s = _fox_logits(q_ref[...], k_ref[...], cb_ref[...], crow_ref[0], j == i, t, t)
            m_old = m_sc[...]
            m_new = jnp.maximum(m_old, jnp.max(s, axis=-1, keepdims=True))
            alpha = jnp.exp(m_old - m_new)
            p = jnp.exp(s - jnp.tile(m_new, (1, t // LANES)))
            l_sc[...] = l_sc[...] * alpha + jnp.sum(p, axis=-1, keepdims=True)
            acc[...] = acc[...] * alpha + bdot(p, v_ref[...])
            m_sc[...] = m_new

        @pl.when(j == i)
        def _():
            o_ref[...] = (acc[...] / l_sc[...]).astype(o_ref.dtype)
            lse_ref[...] = m_sc[...] + jnp.log(l_sc[...])

    qspec = pl.BlockSpec((t, HEAD_DIM), lambda h, i, j: (i, h))
    kspec = pl.BlockSpec((t, HEAD_DIM), lambda h, i, j: (jnp.minimum(j, i), h))
    vspec = pl.BlockSpec((t, HEAD_DIM), lambda h, i, j: (jnp.minimum(j, i), FOX_HEADS + h))
    return _pcall(body, name="fox_fwd",
                  out_shape=[jax.ShapeDtypeStruct((n_rows, FOX_WIDTH), BF16), jax.ShapeDtypeStruct((n_rows, FOX_WIDTH), F32)],
                  grid=(FOX_HEADS, n, n),
                  in_specs=[qspec, kspec, vspec, qspec,
                            pl.BlockSpec((1, 1, t), lambda h, i, j: (h, 0, jnp.minimum(j, i)))],
                  out_specs=[qspec, qspec],
                  scratch=[pltpu.VMEM((t, HEAD_DIM), F32)] * 3,
                  sem=("parallel", "parallel", "arbitrary"))(q, kv, kv, cb, crow)


def fox_bwd_dq(q, kv, cb, crow, o, lse, do, prev=None):
    do, _, do_col = do
    do_col *= FOX_HEADS
    n_rows = kv.shape[0]
    t = _fox_tile(n_rows)
    n = n_rows // t
    n_prev = 0 if prev is None else 1

    def body(q_ref, k_ref, v_ref, cb_ref, crow_ref, o_ref, lse_ref, do_ref, *rest):
        dq_ref, drow_ref, acc, rows = rest[n_prev:]
        i, j = pl.program_id(1), pl.program_id(2)

        @pl.when(j == 0)
        def _():
            acc[...] = jnp.zeros_like(acc)
            rows[...] = jnp.zeros_like(rows)

        @pl.when(j <= i)
        def _():
            s = _fox_logits(q_ref[...], k_ref[...], cb_ref[...], crow_ref[0], j == i, t, t)
            p = jnp.exp(s - jnp.tile(lse_ref[...], (1, t // LANES)))
            d_o = do_ref[...].astype(F32)
            delta = jnp.sum(d_o * o_ref[...].astype(F32), axis=-1, keepdims=True)
            ds = p * (bdot.nt(d_o, v_ref[...]) - delta)
            acc[...] += bdot(ds, k_ref[...])
            rows[...] += jnp.sum(ds, axis=-1, keepdims=True)

        @pl.when(j == i)
        def _():
            dq_ref[...] = (acc[...] * (HEAD_DIM ** -0.5)).astype(dq_ref.dtype)
            drow_ref[...] = rows[...] + rest[0][...] if n_prev else rows[...]

    qspec = pl.BlockSpec((t, HEAD_DIM), lambda h, i, j: (i, h))
    dospec = pl.BlockSpec((t, HEAD_DIM), lambda h, i, j: (i, do_col + h))
    kspec = pl.BlockSpec((t, HEAD_DIM), lambda h, i, j: (jnp.minimum(j, i), h))
    vspec = pl.BlockSpec((t, HEAD_DIM), lambda h, i, j: (jnp.minimum(j, i), FOX_HEADS + h))
    return _pcall(body, name="fox_bwd_dq" + ("_acc" if n_prev else ""),
                  out_shape=[jax.ShapeDtypeStruct((n_rows, FOX_WIDTH), BF16), jax.ShapeDtypeStruct((n_rows, FOX_WIDTH), F32)],
                  grid=(FOX_HEADS, n, n),
                  in_specs=[qspec, kspec, vspec, qspec, pl.BlockSpec((1, 1, t), lambda h, i, j: (h, 0, jnp.minimum(j, i))),
                            qspec, qspec, dospec] + [qspec] * n_prev,
                  out_specs=[qspec, qspec], scratch=[pltpu.VMEM((t, HEAD_DIM), F32)] * 2,
                  sem=("parallel", "parallel", "arbitrary"))(q, kv, kv, cb, crow, o, lse, do, *([prev] if n_prev else []))


def fox_bwd_dkv(q, kv, cb, crow, o, lse, do, prev=None):
    do, _, do_col = do
    do_col *= FOX_HEADS
    n_rows = kv.shape[0]
    t = _fox_tile(n_rows)
    n = n_rows // t
    n_prev = 0 if prev is None else 3

    def body(q_ref, k_ref, v_ref, cb_ref, crow_ref, o_ref, lse_ref, do_ref, *rest):
        prev_refs = rest[:n_prev]
        dk_ref, dv_ref, dc_ref, dk_acc, dv_acc, dc_acc = rest[n_prev:]
        j, i = pl.program_id(1), pl.program_id(2)

        @pl.when(i == 0)
        def _():
            dk_acc[...] = jnp.zeros_like(dk_acc)
            dv_acc[...] = jnp.zeros_like(dv_acc)
            dc_acc[...] = jnp.zeros_like(dc_acc)

        @pl.when(i >= j)
        def _():
            s = _fox_logits(q_ref[...], k_ref[...], cb_ref[...], crow_ref[0], j == i, t, t)
            p = jnp.exp(s - jnp.tile(lse_ref[...], (1, t // LANES)))
            d_o = do_ref[...].astype(F32)
            delta = jnp.sum(d_o * o_ref[...].astype(F32), axis=-1, keepdims=True)
            ds = p * (bdot.nt(d_o, v_ref[...]) - delta)
            dv_acc[...] += bdot.tn(p, d_o)
            dk_acc[...] += bdot.tn(ds, q_ref[...])
            dc_acc[...] -= jnp.sum(ds, axis=0, keepdims=True)

        @pl.when(i == n - 1)
        def _():
            dk, dv, dc = dk_acc[...] * (HEAD_DIM ** -0.5), dv_acc[...], dc_acc[...]
            if n_prev:
                dk, dv, dc = dk + prev_refs[0][...], dv + prev_refs[1][...], dc + prev_refs[2][0]
            dk_ref[...] = dk
            dv_ref[...] = dv
            dc_ref[0] = dc

    qspec = pl.BlockSpec((t, HEAD_DIM), lambda h, j, i: (jnp.maximum(i, j), h))
    dospec = pl.BlockSpec((t, HEAD_DIM), lambda h, j, i: (jnp.maximum(i, j), do_col + h))
    kspec = pl.BlockSpec((t, HEAD_DIM), lambda h, j, i: (j, h))
    vspec = pl.BlockSpec((t, HEAD_DIM), lambda h, j, i: (j, FOX_HEADS + h))
    cspec = pl.BlockSpec((1, 1, t), lambda h, j, i: (h, 0, j))
    return _pcall(body, name="fox_bwd_dkv" + ("_acc" if n_prev else ""),
                  out_shape=[jax.ShapeDtypeStruct((n_rows, FOX_WIDTH), F32), jax.ShapeDtypeStruct((n_rows, FOX_WIDTH), F32),
                             jax.ShapeDtypeStruct((FOX_HEADS, 1, n_rows), F32)],
                  grid=(FOX_HEADS, n, n),
                  in_specs=[qspec, kspec, vspec, qspec, cspec, qspec, qspec, dospec] + [kspec, kspec, cspec][:n_prev],
                  out_specs=[kspec, kspec, cspec],
                  scratch=[pltpu.VMEM((t, HEAD_DIM), F32), pltpu.VMEM((t, HEAD_DIM), F32), pltpu.VMEM((1, t), F32)],
                  sem=("parallel", "parallel", "arbitrary"))(q, kv, kv, cb, crow, o, lse, do, *(prev or ()))


def loss_head(h, gain, target, *, tile=512):
    n_rows, d = h.shape
    tile = min(tile, n_rows)

    def body(h_ref, g_ref, t_ref, part_ref, dy_ref):
        (y,) = f_rmsnorm(h_ref[...], g_ref[...])
        diff = y - t_ref[...]
        dy_ref[...] = diff * (1.0 / d)
        part = jnp.sum(diff * diff, axis=0, keepdims=True)
        first = pl.program_id(0) == 0

        @pl.when(first)
        def _():
            part_ref[...] = part

        @pl.when(jnp.logical_not(first))
        def _():
            part_ref[...] += part

    blk = pl.BlockSpec((tile, d), lambda i: (i, 0))
    one = pl.BlockSpec((1, d), lambda i: (0, 0))
    return _pcall(body, name="loss_head",
                  out_shape=[jax.ShapeDtypeStruct((1, d), F32), jax.ShapeDtypeStruct((n_rows, d), F32)],
                  grid=(n_rows // tile,), in_specs=[blk, one, blk], out_specs=[one, blk], sem=("arbitrary",))(h, gain, target)


def adamw(w, g, m, v, *, name):
    shape = w.shape
    cols = shape[-1] if w.ndim >= 2 else w.size
    rows = w.size // cols
    tile = _pick(rows, (256, 128, 64, 32, 16, 8))
    as2d = lambda a: a.reshape(rows, cols)

    def body(w_ref, g_ref, m_ref, v_ref, d_ref, nm_ref, nv_ref):
        g_ = g_ref[...]
        m_ = ADAM_B1 * m_ref[...] + (1.0 - ADAM_B1) * g_
        v_ = ADAM_B2 * v_ref[...] + (1.0 - ADAM_B2) * (g_ * g_)
        m_hat = m_ / (1.0 - ADAM_B1 ** ADAM_STEP)
        v_hat = v_ / (1.0 - ADAM_B2 ** ADAM_STEP)
        d_ref[...] = -ADAM_LR * (m_hat / (jnp.sqrt(v_hat) + ADAM_EPS) + ADAM_WD * w_ref[...])
        nm_ref[...] = m_
        nv_ref[...] = v_

    blk = pl.BlockSpec((tile, cols), lambda i: (i, 0))
    outs = _pcall(body, name=name, out_shape=[jax.ShapeDtypeStruct((rows, cols), F32)] * 3, grid=(rows // tile,),
                  in_specs=[blk] * 4, out_specs=[blk] * 3, sem=("parallel",))(as2d(w), as2d(g), as2d(m), as2d(v))
    return tuple(o.reshape(shape) for o in outs)


def add_pairs(a, b, *, out_dtype, name):
    p, r, c = a.shape
    tile = _pick(r, (512, 256, 128, 64, 32, 16, 8))

    def body(a_ref, b_ref, o_ref):
        o_ref[...] = (a_ref[...].astype(F32) + b_ref[...].astype(F32)).astype(o_ref.dtype)

    blk = pl.BlockSpec((1, tile, c), lambda k, i: (k, i, 0))
    return _pcall(body, name=name, out_shape=jax.ShapeDtypeStruct((p, r, c), out_dtype), grid=(p, r // tile),
                  in_specs=[blk, blk], out_specs=blk, sem=("parallel", "parallel"))(a, b)


def sum_leading(a, *, name):
    p, r, c = a.shape
    tile = _pick(r, (256, 128, 64, 32, 16, 8))

    def body(a_ref, o_ref):
        total = a_ref[0].astype(F32)
        for k in range(1, p):
            total = total + a_ref[k].astype(F32)
        o_ref[...] = total

    return _pcall(body, name=name, out_shape=jax.ShapeDtypeStruct((r, c), F32), grid=(r // tile,),
                  in_specs=[pl.BlockSpec((p, tile, c), lambda i: (0, i, 0))],
                  out_specs=pl.BlockSpec((tile, c), lambda i: (i, 0)), sem=("parallel",))(a)


_HBM = pl.BlockSpec(memory_space=pltpu.HBM)


def _comm_call(body, *, name, out_shape, n_in, scratch):
    return pl.pallas_call(body, name=name, out_shape=out_shape, in_specs=[_HBM] * n_in, out_specs=_HBM,
                          scratch_shapes=scratch,
                          compiler_params=pltpu.CompilerParams(has_side_effects=True))


def all_gather8(a, *, name):
    m_per, n = a.shape

    def body(x_ref, out_ref, send_sems, recv_sems, local_sem):
        x, y, c = lax.axis_index("x"), lax.axis_index("y"), lax.axis_index("c")
        me, sibling = (x, y, c), (x, y, 1 - c)
        chips = [(1 - x, y), (x, 1 - y), (1 - x, 1 - y)]

        def rows(px, py, pc):
            return out_ref.at[pl.ds((4 * px + 2 * py + pc) * m_per, m_per), :]

        def copy(k, block, to, src=None):
            return pltpu.make_async_remote_copy(
                src_ref=rows(*block) if src is None else src, dst_ref=rows(*block),
                send_sem=send_sems.at[k], recv_sem=recv_sems.at[k], device_id=to, device_id_type=MESH)

        mine = pltpu.make_async_copy(x_ref, rows(*me), local_sem)
        mine.start()
        first = [copy(0, me, sibling, src=x_ref)]
        first += [copy(1 + j, me, (*chip, c), src=x_ref) for j, chip in enumerate(chips)]
        for cp in first:
            cp.start()
        passed = [copy(4 + j, (*chip, c), sibling) for j, chip in enumerate(chips)]
        for j, chip in enumerate(chips):
            copy(1 + j, (*chip, c), me).wait_recv()
            passed[j].start()
        copy(0, sibling, me).wait_recv()
        for j, chip in enumerate(chips):
            copy(4 + j, (*chip, 1 - c), me).wait_recv()
        for cp in first + passed:
            cp.wait_send()
        mine.wait()

    return _comm_call(body, name=name, out_shape=jax.ShapeDtypeStruct((N_DEV * m_per, n), a.dtype), n_in=1,
                      scratch=[pltpu.SemaphoreType.DMA((7,)), pltpu.SemaphoreType.DMA((7,)), pltpu.SemaphoreType.DMA])(a)


def sibling_swap(a, *, name):
    def body(a_ref, out_ref, send_sem, recv_sem):
        x, y, c = lax.axis_index("x"), lax.axis_index("y"), lax.axis_index("c")
        cp = pltpu.make_async_remote_copy(src_ref=a_ref, dst_ref=out_ref, send_sem=send_sem, recv_sem=recv_sem,
                                          device_id=(x, y, 1 - c), device_id_type=MESH)
        cp.start()
        cp.wait()

    return _comm_call(body, name=name, out_shape=jax.ShapeDtypeStruct(a.shape, a.dtype), n_in=1,
                      scratch=[pltpu.SemaphoreType.DMA, pltpu.SemaphoreType.DMA])(a)


def chip_all_to_all(a, *, name):
    def body(a_ref, out_ref, send_sems, recv_sems, local_sem):
        x, y, c = lax.axis_index("x"), lax.axis_index("y"), lax.axis_index("c")
        me = 2 * x + y
        mine = pltpu.make_async_copy(a_ref.at[me], out_ref.at[me], local_sem)
        mine.start()
        copies = []
        for j, (px, py) in enumerate([(1 - x, y), (x, 1 - y), (1 - x, 1 - y)]):
            copies.append(pltpu.make_async_remote_copy(
                src_ref=a_ref.at[2 * px + py], dst_ref=out_ref.at[me], send_sem=send_sems.at[j],
                recv_sem=recv_sems.at[j], device_id=(px, py, c), device_id_type=MESH))
        for cp in copies:
            cp.start()
        for cp in copies:
            cp.wait()
        mine.wait()

    return _comm_call(body, name=name, out_shape=jax.ShapeDtypeStruct(a.shape, a.dtype), n_in=1,
                      scratch=[pltpu.SemaphoreType.DMA((3,)), pltpu.SemaphoreType.DMA((3,)), pltpu.SemaphoreType.DMA])(a)


PACK_COLS = 1024
PACK_ROW_MULTIPLE = 32

WEIGHT_NAMES = ["ffn1_norm", "ffn1_w_gate_up", "ffn1_w_down", "mix_norm", "ffn2_norm", "ffn2_w_gate_up", "ffn2_w_down",
                "gdn_w_in", "gdn_conv", "gdn_A_log", "gdn_dt_bias", "gdn_out_norm", "fox_w_in", "w_out", "mem_norm",
                "mem_w_kv", "kv_norm", "kv_w", "kv_b_f", "final_norm"]
SHARDED = [("ffn1_w_gate_up", 2), ("ffn1_w_down", 1), ("ffn2_w_gate_up", 2), ("ffn2_w_down", 1), ("gdn_w_in", 2),
           ("gdn_conv", 2), ("fox_w_in", 1), ("w_out", 1), ("mem_w_kv", 1), ("kv_w", 0)]
REPLICATED = [n for n in WEIGHT_NAMES if n not in dict(SHARDED)]


PACK_PIECE_ROWS = 16


def _rows_of(size):
    return -(-size // (PACK_COLS * PACK_PIECE_ROWS)) * PACK_PIECE_ROWS


def pack(pieces, dtype, row_multiple=PACK_ROW_MULTIPLE):
    bufs, total = [], 0
    for p in pieces:
        flat = p.astype(dtype).reshape(-1)
        rows = _rows_of(flat.size)
        bufs.append(jnp.pad(flat, (0, rows * PACK_COLS - flat.size)).reshape(rows, PACK_COLS))
        total += rows
    pad = -total % row_multiple
    if pad:
        bufs.append(jnp.zeros((pad, PACK_COLS), dtype))
    return jnp.concatenate(bufs, axis=0)


def unpack(buf, shapes):
    out, row = [], 0
    for shape in shapes:
        size = 1
        for s in shape:
            size *= s
        rows = _rows_of(size)
        out.append(buf[row:row + rows].reshape(-1)[:size].reshape(shape))
        row += rows
    return out


def _row(vec, width=None):
    vec = vec.astype(F32).reshape(1, -1)
    if width is not None and vec.shape[1] < width:
        vec = jnp.pad(vec, ((0, 0), (0, width - vec.shape[1])))
    return vec


ROW_TILE = 512
GDN_PROJ_WIDTH = 4 * GDN_WIDTH + MEM_WIDTH + LANES
GDN_Z_COL, GDN_QMEM_COL, GDN_AB_COL = 3, 4 * GDN_WIDTH // MEM_WIDTH, (4 * GDN_WIDTH + MEM_WIDTH) // LANES
FOX_QMEM_COL = FOX_WIDTH // MEM_WIDTH
KV_PAD_WIDTH = 2 * FOX_WIDTH + LANES


def rms_fwd(x, gain_row, out_dtype=BF16):
    return rowwise(f_rmsnorm, [x], [gain_row], [(x.shape[1], out_dtype)], tile=ROW_TILE, name="rms_fwd")[0]


def rms_bwd(x, gain_row, dy, dres=None):
    return rowwise_bwd(f_rmsnorm, [x], [gain_row], [dy], tile=ROW_TILE, name="rms_bwd", row_grads=[F32],
                       const_grads=[True], add=None if dres is None else {0: dres})


def ffn_fwd(h, gain_row, wgu, wd):
    n = rms_fwd(h, gain_row)
    gu = mm(n, wgu, out_dtype=BF16, name="ffn_up")
    act = rowwise(f_swiglu, [gu], [], [(FFN_HIDDEN, BF16)], tile=256, name="swiglu_fwd")[0]
    return mm(act, wd, scale=0.5, res=h, name="ffn_down"), (h, n, gu, act)


def ffn_bwd(dh, saved, gain_row, wgu, wd):
    h, n, gu, act = saved
    dact = mm(dh, wd, tb=True, scale=0.5, out_dtype=BF16, name="ffn_down_dx")
    dwd = mm(act, dh, ta=True, scale=0.5, name="ffn_down_dw")
    (dgu,) = rowwise_bwd(f_swiglu, [gu], [], [dact], tile=256, name="swiglu_bwd", row_grads=[BF16], const_grads=[])
    dwgu = mm(n, dgu, ta=True, name="ffn_up_dw")
    dn = mm(dgu, wgu, tb=True, out_dtype=BF16, name="ffn_up_dx")
    dh, dgain = rms_bwd(h, gain_row, dn, dh)
    return dh, dwgu, dwd, dgain


def gdn_fwd(proj, w8, a_row, dt_row, onorm_row):
    wide = [(GDN_WIDTH, F32)] * 5
    xc = conv_fwd(proj, w8, width=3 * GDN_WIDTH)
    q, k, v, gc, beta = rowwise(f_gdn_pre, [xc, (proj, LANES, GDN_AB_COL)], [a_row, dt_row], wide, tile=GDN_CHUNK,
                                name="gdn_pre_fwd")
    u, w, qk, kt, qh = rowwise(f_gdn_intra, [q, k, v, gc, beta], [], wide, tile=GDN_CHUNK, name="gdn_intra_fwd")
    o, vn, sin = gdn_scan_fwd(u, w, qk, kt, qh, gc)
    main = rowwise(f_gdn_post, [o, (proj, GDN_WIDTH, GDN_Z_COL)], [onorm_row], [(GDN_WIDTH, BF16)], tile=ROW_TILE,
                   name="gdn_post_fwd")[0]
    return main, (xc, q, k, v, gc, beta, w, qk, kt, qh, vn, sin, o)


def gdn_bwd(dmain, proj, saved, w8, a_row, dt_row, onorm_row):
    xc, q, k, v, gc, beta, w, qk, kt, qh, vn, sin, o = saved
    do, dz, donorm = rowwise_bwd(f_gdn_post, [o, (proj, GDN_WIDTH, GDN_Z_COL)], [onorm_row], [dmain], tile=ROW_TILE,
                                 name="gdn_post_bwd", row_grads=[F32, BF16], const_grads=[True])
    du, dw, dqk, dkt, dqh, dgl = gdn_scan_bwd(do, w, qk, kt, qh, gc, vn, sin)
    dq, dk, dv, dgc, dbeta = gdn_intra_bwd(q, k, v, gc, beta, (du, dw, dqk, dkt, dqh), dgl)
    dxc, dab, da, ddt = rowwise_bwd(f_gdn_pre, [xc, (proj, LANES, GDN_AB_COL)], [a_row, dt_row], [dq, dk, dv, dgc, dbeta],
                                    tile=GDN_CHUNK, name="gdn_pre_bwd", row_grads=[F32, BF16], const_grads=[True, True])
    dqkv, dw8 = conv_bwd(proj, w8, dxc, width=3 * GDN_WIDTH)
    return dqkv, dz, dab, dw8, da, ddt, donorm


def mem_fwd(q, kmem, vmem):
    return rowwise(f_mem_attn, [q], [kmem, vmem], [(MEM_WIDTH, BF16)], tile=ROW_TILE, name="mem_attn_fwd")[0]


def mem_bwd(q, kmem, vmem, dout):
    return rowwise_bwd(f_mem_attn, [q], [kmem, vmem], [dout], tile=ROW_TILE, name="mem_attn_bwd", row_grads=[BF16],
                       const_grads=[True, True])


def forward_backward(xs, mems, target, P):
    depth, n_a = 4, 2
    G = {}
    mem_gain = _row(P["mem_norm"])
    mem_n = rms_fwd(mems, mem_gain)
    h = xs
    saved = []
    shared = None
    for l in range(depth):
        h0 = h
        h1, s1 = ffn_fwd(h0, _row(P["ffn1_norm"][l]), P["ffn1_w_gate_up"][l], P["ffn1_w_down"][l])
        u = rms_fwd(h1, _row(P["mix_norm"][l]))
        kvm = mm(mem_n, P["mem_w_kv"][l], name="mem_kv")
        kmem, vmem = kvm[:, :MEM_WIDTH], kvm[:, MEM_WIDTH:]
        if l < n_a:
            gp = (P["conv8"][l], _row(P["gdn_A_log"][l], LANES), _row(P["gdn_dt_bias"][l], LANES), _row(P["gdn_out_norm"][l]))
            proj = mm(u, P["gdn_w_in_pad"][l], name="gdn_in")
            main, sm = gdn_fwd(proj, *gp)
            qm = (proj, MEM_WIDTH, GDN_QMEM_COL)
        else:
            proj = mm(u, P["fox_w_in"][l - n_a], out_dtype=BF16, name="fox_in")
            main, lse = fox_fwd(proj, *shared)
            sm = (main, lse)
            qm = (proj, MEM_WIDTH, FOX_QMEM_COL)
        mo = mem_fwd(qm, kmem, vmem)
        cat = jnp.concatenate([main, mo], axis=1)
        h2 = mm(cat, P["w_out"][l], res=h1, name="mix_out")
        h3, s2 = ffn_fwd(h2, _row(P["ffn2_norm"][l]), P["ffn2_w_gate_up"][l], P["ffn2_w_down"][l])
        saved.append((s1, h1, u, kmem, vmem, proj, sm, qm, cat, s2))
        h = h3
        if l == n_a - 1:
            nkv = rms_fwd(h, _row(P["kv_norm"]))
            kv = mm(nkv, P["kv_w_pad"][:, :2 * FOX_WIDTH], out_dtype=BF16, name="fox_kv")
            f = mm(nkv, P["kv_w_pad"][:, 2 * FOX_WIDTH:], name="fox_f")
            bf_row = _row(P["kv_b_f"], LANES)
            cb, crow = fox_gate_fwd(f, bf_row)
            shared = (kv, cb, crow)
            kv_saved = (h, nkv, f, bf_row)

    part, dy = loss_head(h, _row(P["final_norm"]), target)
    dh, G["final_norm"] = rms_bwd(h, _row(P["final_norm"]), dy)

    per_layer = {n: [None] * depth for n in ("ffn1_norm", "ffn1_w_gate_up", "ffn1_w_down", "mix_norm", "ffn2_norm",
                                             "ffn2_w_gate_up", "ffn2_w_down", "w_out", "mem_w_kv")}
    gdn_g = {n: [None] * n_a for n in ("gdn_w_in_pad", "conv8", "gdn_A_log", "gdn_dt_bias", "gdn_out_norm")}
    fox_g = [None] * (depth - n_a)
    dmem_n = None
    dkv_acc = dcb_acc = None
    for l in reversed(range(depth)):
        s1, h1, u, kmem, vmem, proj, sm, qm, cat, s2 = saved[l]
        if l == n_a - 1:
            hk, nkv, f, bf_row = kv_saved
            dk, dv, dcrow = dkv_acc
            df, dbf = fox_gate_bwd(f, bf_row, dcrow, dcb_acc)
            dp = jnp.concatenate([dk.astype(BF16), dv.astype(BF16), df.astype(BF16)], axis=1)
            G["kv_w_pad"] = mm(nkv, dp, ta=True, name="fox_kv_dw")
            G["kv_b_f"] = dbf
            dnkv = mm(dp, P["kv_w_pad"], tb=True, out_dtype=BF16, name="fox_kv_dx")
            dh, G["kv_norm"] = rms_bwd(hk, _row(P["kv_norm"]), dnkv, dh)
        dh, per_layer["ffn2_w_gate_up"][l], per_layer["ffn2_w_down"][l], per_layer["ffn2_norm"][l] = ffn_bwd(
            dh, s2, _row(P["ffn2_norm"][l]), P["ffn2_w_gate_up"][l], P["ffn2_w_down"][l])
        dcat = mm(dh, P["w_out"][l], tb=True, out_dtype=BF16, name="mix_out_dx")
        per_layer["w_out"][l] = mm(cat, dh, ta=True, name="mix_out_dw")
        dqm, dkm, dvm = mem_bwd(qm, kmem, vmem, (dcat, MEM_WIDTH, FOX_QMEM_COL))
        dkvm = jnp.concatenate([dkm, dvm], axis=1)
        per_layer["mem_w_kv"][l] = mm(mem_n, dkvm, ta=True, name="mem_kv_dw")
        dmem_n = mm(dkvm, P["mem_w_kv"][l], tb=True, res=dmem_n, name="mem_kv_dx")
        dmain = (dcat, GDN_WIDTH, 0)
        if l < n_a:
            gp = (P["conv8"][l], _row(P["gdn_A_log"][l], LANES), _row(P["gdn_dt_bias"][l], LANES), _row(P["gdn_out_norm"][l]))
            dqkv, dz, dab, gdn_g["conv8"][l], gdn_g["gdn_A_log"][l], gdn_g["gdn_dt_bias"][l], gdn_g["gdn_out_norm"][l] = gdn_bwd(
                dmain, proj, sm, *gp)
            dproj = jnp.concatenate([dqkv, dz, dqm, dab], axis=1)
            gdn_g["gdn_w_in_pad"][l] = mm(u, dproj, ta=True, name="gdn_in_dw")
            du = mm(dproj, P["gdn_w_in_pad"][l], tb=True, out_dtype=BF16, name="gdn_in_dx")
        else:
            o, lse = sm
            dq, dcb_acc = fox_bwd_dq(proj, *shared, o, lse, dmain, dcb_acc)
            dkv_acc = fox_bwd_dkv(proj, *shared, o, lse, dmain, dkv_acc)
            dproj = jnp.concatenate([dq, dqm], axis=1)
            fox_g[l - n_a] = mm(u, dproj, ta=True, name="fox_in_dw")
            du = mm(dproj, P["fox_w_in"][l - n_a], tb=True, out_dtype=BF16, name="fox_in_dx")
        dh, per_layer["mix_norm"][l] = rms_bwd(h1, _row(P["mix_norm"][l]), du, dh)
        dh, per_layer["ffn1_w_gate_up"][l], per_layer["ffn1_w_down"][l], per_layer["ffn1_norm"][l] = ffn_bwd(
            dh, s1, _row(P["ffn1_norm"][l]), P["ffn1_w_gate_up"][l], P["ffn1_w_down"][l])

    (G["mem_norm"],) = rowwise_bwd(f_rmsnorm, [mems], [mem_gain], [dmem_n], tile=ROW_TILE, name="mem_norm_bwd",
                                   row_grads=[None], const_grads=[True])
    for n, v in per_layer.items():
        G[n] = jnp.stack(v)
    for n, v in gdn_g.items():
        G[n] = jnp.stack(v)
    G["fox_w_in"] = jnp.stack(fox_g)
    return part, dh, G


_GDN_O0 = 4 * GDN_WIDTH
_GDN_O1 = _GDN_O0 + 2 * GDN_HEADS
_KV_WIDTH = 2 * FOX_WIDTH + FOX_HEADS


def derived_weights(gdn_w_in, gdn_conv, kv_w):
    zeros = jnp.zeros(gdn_w_in.shape[:-1] + (LANES - 2 * GDN_HEADS,), gdn_w_in.dtype)
    return dict(
        gdn_w_in_pad=jnp.concatenate([gdn_w_in[..., :_GDN_O0], gdn_w_in[..., _GDN_O1:], gdn_w_in[..., _GDN_O0:_GDN_O1], zeros], axis=-1),
        conv8=jnp.pad(gdn_conv.astype(F32), ((0, 0), (0, 8 - CONV_WIDTH), (0, 0))),
        kv_w_pad=jnp.pad(kv_w, ((0, 0), (0, KV_PAD_WIDTH - _KV_WIDTH))))


def reference_layout(G):
    gp = G["gdn_w_in_pad"]
    out = dict(G)
    out["gdn_w_in"] = jnp.concatenate([gp[..., :_GDN_O0], gp[..., _GDN_O0 + MEM_WIDTH:_GDN_O0 + MEM_WIDTH + 2 * GDN_HEADS],
                                       gp[..., _GDN_O0:_GDN_O0 + MEM_WIDTH]], axis=-1)
    out["gdn_conv"] = G["conv8"][:, :CONV_WIDTH]
    out["kv_w"] = G["kv_w_pad"][:, :_KV_WIDTH]
    out["gdn_A_log"] = G["gdn_A_log"][:, 0, :GDN_HEADS]
    out["gdn_dt_bias"] = G["gdn_dt_bias"][:, 0, :GDN_HEADS]
    out["gdn_out_norm"] = G["gdn_out_norm"][:, 0, :]
    out["kv_b_f"] = G["kv_b_f"][0, :FOX_HEADS]
    for n in ("ffn1_norm", "mix_norm", "ffn2_norm"):
        out[n] = G[n][:, 0, :]
    for n in ("mem_norm", "kv_norm", "final_norm"):
        out[n] = G[n][0]
    return {n: out[n] for n in WEIGHT_NAMES}


def _f32_as_bf16(a):
    a = a.astype(F32)
    hi = a.astype(BF16)
    mid = (a - hi.astype(F32)).astype(BF16)
    lo = (a - hi.astype(F32) - mid.astype(F32)).astype(BF16)
    return jnp.stack([hi, mid, lo], axis=-1)


def _bf16_as_f32(a):
    a = a.astype(F32)
    return (a[..., 0] + a[..., 1]) + a[..., 2]


def gather_weights(W):
    c = lax.axis_index("c")
    pieces, shapes = [], []
    for name, _ in SHARDED:
        p = _f32_as_bf16(W[name]) if name == "gdn_conv" else W[name].astype(BF16)
        pieces.append(p)
        shapes.append(p.shape)
    packed = pack(pieces, BF16)
    half = packed.shape[0] // 2
    gathered = all_gather8(lax.dynamic_slice_in_dim(packed, c * half, half, axis=0), name="gather_weights")
    per_chip = [unpack(gathered[k * 2 * half:(k + 1) * 2 * half], shapes) for k in range(N_CHIPS)]
    full = {}
    for i, (name, axis) in enumerate(SHARDED):
        parts = [per_chip[k][i] for k in range(N_CHIPS)]
        if name == "gdn_conv":
            parts = [_bf16_as_f32(p) for p in parts]
        full[name] = jnp.concatenate(parts, axis=axis)
    return full


def reduce_gradients(G, shard_shapes):
    c = lax.axis_index("c")
    per_chip = []
    for k in range(N_CHIPS):
        pieces = []
        for name, axis in SHARDED:
            size = shard_shapes[name][axis]
            pieces.append(lax.slice_in_dim(G[name], k * size, (k + 1) * size, axis=axis))
        per_chip.append(pack(pieces, F32))
    stacked = jnp.stack(per_chip)
    half = stacked.shape[1] // 2
    mine = lax.dynamic_slice_in_dim(stacked, c * half, half, axis=1)
    theirs = lax.dynamic_slice_in_dim(stacked, (1 - c) * half, half, axis=1)
    pair = add_pairs(mine, sibling_swap(theirs, name="grad_pair_swap"), out_dtype=BF16, name="grad_pair_sum")
    total_half = sum_leading(chip_all_to_all(pair, name="grad_all_to_all"), name="grad_chip_sum")
    other_half = sibling_swap(total_half, name="grad_half_swap")
    total = jnp.zeros((2 * half, PACK_COLS), F32)
    total = lax.dynamic_update_slice_in_dim(total, total_half, c * half, axis=0)
    total = lax.dynamic_update_slice_in_dim(total, other_half, (1 - c) * half, axis=0)
    return dict(zip([n for n, _ in SHARDED], unpack(total, [shard_shapes[n] for n, _ in SHARDED])))


def allreduce_replicated(G):
    names = REPLICATED
    packed = pack([G[n] for n in names], F32, row_multiple=8)
    gathered = all_gather8(packed, name="gather_small_grads").reshape(N_DEV, packed.shape[0], PACK_COLS)
    total = sum_leading(gathered, name="sum_small_grads")
    return dict(zip(names, unpack(total, [G[n].shape for n in names])))


def kernel(x, mem, *rest):
    n_w = len(WEIGHT_NAMES)
    W = dict(zip(WEIGHT_NAMES, rest[:n_w]))
    target = rest[n_w]
    M = dict(zip(WEIGHT_NAMES, rest[n_w + 1:2 * n_w + 1]))
    V = dict(zip(WEIGHT_NAMES, rest[2 * n_w + 1:3 * n_w + 1]))

    full = gather_weights(W)
    P = {n: W[n] for n in REPLICATED}
    P.update({n: full[n] for n in ("ffn1_w_gate_up", "ffn1_w_down", "ffn2_w_gate_up", "ffn2_w_down", "fox_w_in", "w_out", "mem_w_kv")})
    P.update(derived_weights(full["gdn_w_in"], full["gdn_conv"], full["kv_w"]))

    part, dx, G = forward_backward(x[0], mem[0], target[0], P)
    G = reference_layout(G)
    loss = lax.psum(0.5 / x.shape[-1] * jnp.sum(part), ("x", "y", "c"))

    grads = reduce_gradients(G, {n: W[n].shape for n, _ in SHARDED})
    grads.update(allreduce_replicated(G))

    outs = {n: adamw(W[n], grads[n], M[n], V[n], name="adamw_" + n) for n in WEIGHT_NAMES}
    return (loss, dx[None], *[grads[n] for n in WEIGHT_NAMES], *[outs[n][0] for n in WEIGHT_NAMES],
            *[outs[n][1] for n in WEIGHT_NAMES], *[outs[n][2] for n in WEIGHT_NAMES])
```

```python
import jax
import jax.numpy as jnp
from jax import lax
from jax.experimental import pallas as pl
from jax.experimental.pallas import tpu as pltpu

F32, BF16 = jnp.float32, jnp.bfloat16
HI = lax.Precision.HIGHEST
MESH = pl.DeviceIdType.MESH

VMEM_LIMIT_BYTES = 48 * 1024 * 1024
LANES = 128
EPS = 1e-6
NEG_INF = -1e30

D_MODEL = 1024
HEAD_DIM = 128
GDN_HEADS = 6
GDN_WIDTH = GDN_HEADS * HEAD_DIM
FOX_HEADS = 6
FOX_WIDTH = FOX_HEADS * HEAD_DIM
MEM_HEADS = 4
MEM_HEAD_DIM = 64
MEM_WIDTH = MEM_HEADS * MEM_HEAD_DIM
FFN_HIDDEN = 2816
CONV_WIDTH = 4
GDN_CHUNK = 128
N_CHIPS = 4
N_DEV = 8

ADAM_LR, ADAM_B1, ADAM_B2, ADAM_EPS, ADAM_WD, ADAM_STEP = 0.001, 0.9, 0.999, 1e-08, 0.01, 10


def _pcall(body, *, name, out_shape, grid=(), in_specs=None, out_specs=None, scratch=(), sem=None):
    params = dict(vmem_limit_bytes=VMEM_LIMIT_BYTES)
    if sem is not None:
        params["dimension_semantics"] = sem
    kw = dict(grid=grid, in_specs=in_specs, out_specs=out_specs) if grid else {}
    return pl.pallas_call(body, name=name, out_shape=out_shape, scratch_shapes=list(scratch),
                          compiler_params=pltpu.CompilerParams(**params), **kw)


def _pick(n, cands):
    for c in cands:
        if n % c == 0:
            return c
    return n


def _make_dot(dtype, precision):
    def raw(a, b, dims):
        return lax.dot_general(a.astype(dtype), b.astype(dtype), (dims, ((), ())),
                               precision=precision, preferred_element_type=F32)

    @jax.custom_vjp
    def dot(a, b):
        return raw(a, b, ((1,), (0,)))

    def fwd(a, b):
        return dot(a, b), (a, b)

    def bwd(resid, ct):
        a, b = resid
        return raw(ct, b, ((1,), (1,))).astype(a.dtype), raw(a, ct, ((0,), (0,))).astype(b.dtype)

    dot.defvjp(fwd, bwd)
    dot.nn = lambda a, b: raw(a, b, ((1,), (0,)))
    dot.nt = lambda a, b: raw(a, b, ((1,), (1,)))
    dot.tn = lambda a, b: raw(a, b, ((0,), (0,)))
    return dot


bdot = _make_dot(BF16, None)
fdot = _make_dot(F32, HI)
idot = fdot
sdot = fdot


def _sigmoid(x):
    return jax.nn.sigmoid(x)


def _silu(x):
    return x * _sigmoid(x)


def _softplus(x):
    return jnp.maximum(x, 0.0) + jnp.log(1.0 + jnp.exp(-jnp.abs(x)))


def _log_sigmoid(x):
    return -_softplus(-x)


def _iota2(shape, dim):
    return lax.broadcasted_iota(jnp.int32, shape, dim)


def mm(a, b, *, ta=False, tb=False, a_split=False, b_split=False, out_dtype=F32, scale=1.0, res=None, name):
    assert not (a_split and ta) and not (b_split and tb)
    (K, M) = a.shape if ta else ((2 * a.shape[2], a.shape[1]) if a_split else a.shape[::-1])
    (N, Kb) = b.shape if tb else ((2 * b.shape[2], b.shape[1]) if b_split else b.shape[::-1])
    assert K == Kb, (a.shape, b.shape, ta, tb)
    tm = _pick(M, (1024, 1408, 512, 256, 128)) if ta else _pick(M, (512, 256, 128))
    tn = _pick(N, (1024, 1408, 768, 512, 384, 256, 128))
    tk = _pick(K, (512, 256, 128)) if ta else _pick(K, (1024, 1408, 512, 256, 128))
    nk = K // tk
    dims = (((0 if ta else 1,), (1 if tb else 0,)), ((), ()))

    def body(a_ref, b_ref, *rest):
        o_ref, acc = rest[-2], rest[-1]
        k = pl.program_id(2)

        @pl.when(k == 0)
        def _():
            acc[...] = jnp.zeros_like(acc)

        acc[...] += lax.dot_general(a_ref[...].astype(BF16), b_ref[...].astype(BF16), dims,
                                    preferred_element_type=F32)

        @pl.when(k == nk - 1)
        def _():
            out = acc[...] * scale
            if res is not None:
                out = out + rest[0][...].astype(F32)
            o_ref[...] = out.astype(o_ref.dtype)

    a_spec = pl.BlockSpec((tk, tm), lambda i, j, k: (k, i)) if ta else pl.BlockSpec((tm, tk), lambda i, j, k: (i, k))
    b_spec = pl.BlockSpec((tn, tk), lambda i, j, k: (j, k)) if tb else pl.BlockSpec((tk, tn), lambda i, j, k: (k, j))
    if a_split:
        per_half = K // 2 // tk
        a_spec = pl.BlockSpec((None, tm, tk), lambda i, j, k: (k // per_half, i, k % per_half))
    if b_split:
        per_half = N // 2 // tn
        b_spec = pl.BlockSpec((None, tk, tn), lambda i, j, k: (j // per_half, k, j % per_half))
    o_spec = pl.BlockSpec((tm, tn), lambda i, j, k: (i, j))
    ins, specs = [a, b], [a_spec, b_spec]
    if res is not None:
        ins.append(res)
        specs.append(o_spec)
    return _pcall(body, name=name, out_shape=jax.ShapeDtypeStruct((M, N), out_dtype),
                  grid=(M // tm, N // tn, nk), in_specs=specs, out_specs=o_spec,
                  scratch=[pltpu.VMEM((tm, tn), F32)], sem=("parallel", "parallel", "arbitrary"))(*ins)


def _row_spec(r, tile):
    if isinstance(r, tuple):
        arr, width, col = r
        return arr, pl.BlockSpec((tile, width), lambda i, col=col: (i, col))
    return r, pl.BlockSpec((tile, r.shape[1]), lambda i: (i, 0))


def _const_spec(c):
    return pl.BlockSpec(c.shape, lambda i: (0,) * c.ndim)


def rowwise(fn, rows, consts, outs, *, tile, name):
    arrs, specs = zip(*[_row_spec(r, tile) for r in rows])
    n_rows = arrs[0].shape[0]
    tile = min(tile, n_rows)
    n_in = len(rows) + len(consts)

    def body(*refs):
        res = fn(*[r[...] for r in refs[:n_in]])
        for o_ref, o in zip(refs[n_in:], res):
            o_ref[...] = o.astype(o_ref.dtype)

    arrs, specs = zip(*[_row_spec(r, tile) for r in rows])
    return _pcall(body, name=name,
                  out_shape=[jax.ShapeDtypeStruct((n_rows, w), dt) for w, dt in outs],
                  grid=(n_rows // tile,),
                  in_specs=list(specs) + [_const_spec(c) for c in consts],
                  out_specs=[pl.BlockSpec((tile, w), lambda i: (i, 0)) for w, _ in outs],
                  sem=("parallel",))(*arrs, *consts)


def rowwise_bwd(fn, rows, consts, cts, *, tile, name, row_grads, const_grads, add=None):
    arrs, _ = zip(*[_row_spec(r, tile) for r in rows])
    n_rows = arrs[0].shape[0]
    tile = min(tile, n_rows)
    arrs, specs = zip(*[_row_spec(r, tile) for r in rows])
    ct_arrs, ct_specs = zip(*[_row_spec(c, tile) for c in cts])
    add = add or {}
    add_idx = sorted(add)
    add_arrs, add_specs = (zip(*[_row_spec(add[i], tile) for i in add_idx]) if add_idx else ((), ()))
    nr, nc, nct, na = len(rows), len(consts), len(cts), len(add_idx)
    want_rows = [i for i, d in enumerate(row_grads) if d is not None]
    want_consts = [i for i, w in enumerate(const_grads) if w]

    def body(*refs):
        row_v = [r[...] for r in refs[:nr]]
        const_v = [r[...] for r in refs[nr:nr + nc]]
        ct_v = [r[...] for r in refs[nr + nc:nr + nc + nct]]
        add_v = {i: refs[nr + nc + nct + j][...] for j, i in enumerate(add_idx)}
        out_refs = refs[nr + nc + nct + na:]
        res, vjp = jax.vjp(fn, *row_v, *const_v)
        grads = vjp(tuple(c.astype(o.dtype) for c, o in zip(ct_v, res)))
        for o_ref, i in zip(out_refs, want_rows):
            g = grads[i].astype(F32)
            if i in add_v:
                g = g + add_v[i].astype(F32)
            o_ref[...] = g.astype(o_ref.dtype)
        first = pl.program_id(0) == 0
        for o_ref, i in zip(out_refs[len(want_rows):], want_consts):
            g = grads[nr + i].astype(F32)

            @pl.when(first)
            def _(o_ref=o_ref, g=g):
                o_ref[...] = g

            @pl.when(jnp.logical_not(first))
            def _(o_ref=o_ref, g=g):
                o_ref[...] += g

    def width(r):
        return r[1] if isinstance(r, tuple) else r.shape[1]

    out_shape = [jax.ShapeDtypeStruct((n_rows, width(rows[i])), row_grads[i]) for i in want_rows]
    out_shape += [jax.ShapeDtypeStruct(consts[i].shape, F32) for i in want_consts]
    out_specs = [pl.BlockSpec((tile, width(rows[i])), lambda i_: (i_, 0)) for i in want_rows]
    out_specs += [_const_spec(consts[i]) for i in want_consts]
    return _pcall(body, name=name, out_shape=out_shape, grid=(n_rows // tile,),
                  in_specs=list(specs) + [_const_spec(c) for c in consts] + list(ct_specs) + list(add_specs),
                  out_specs=out_specs, sem=("arbitrary",))(*arrs, *consts, *ct_arrs, *add_arrs)


def f_rmsnorm(x, g):
    x = x.astype(F32)
    return (x * lax.rsqrt(jnp.mean(x * x, axis=-1, keepdims=True) + EPS) * g,)


def f_swiglu(gu):
    gu = gu.astype(F32)
    return (_silu(gu[:, :FFN_HIDDEN]) * gu[:, FFN_HIDDEN:],)


def _head_sel(first_lane):
    r, c = _iota2((LANES, GDN_WIDTH), 0), _iota2((LANES, GDN_WIDTH), 1)
    return (r == c // HEAD_DIM + first_lane).astype(F32)


def _tri(n, strict=False):
    r, c = _iota2((n, n), 0), _iota2((n, n), 1)
    return r > c if strict else r >= c


def f_gdn_pre(xc, ab, a_log, dt_bias):
    s = _silu(xc.astype(F32))
    qs, ks = [], []
    for h in range(GDN_HEADS):
        qh = s[:, h * HEAD_DIM:(h + 1) * HEAD_DIM]
        kh = s[:, GDN_WIDTH + h * HEAD_DIM:GDN_WIDTH + (h + 1) * HEAD_DIM]
        qs.append(qh * lax.rsqrt(jnp.sum(qh * qh, axis=-1, keepdims=True) + EPS) * (HEAD_DIM ** -0.5))
        ks.append(kh * lax.rsqrt(jnp.sum(kh * kh, axis=-1, keepdims=True) + EPS))
    q, k = jnp.concatenate(qs, axis=1), jnp.concatenate(ks, axis=1)
    v = s[:, 2 * GDN_WIDTH:]
    ab = ab.astype(F32)
    g = -jnp.exp(a_log) * _softplus(ab + dt_bias)
    gc = fdot(_tri(GDN_CHUNK).astype(F32), fdot(g, _head_sel(0)))
    beta = fdot(_sigmoid(ab), _head_sel(GDN_HEADS))
    return q, k, v, gc, beta


def _unit_lower_inverse(neg_lower):
    C = neg_lower.shape[0]
    inv = (_iota2((C, C), 0) == _iota2((C, C), 1)).astype(F32) + neg_lower
    power = neg_lower
    for _ in range(6):
        power = idot.nn(power, power)
        inv = inv + idot.nn(power, inv)
    return inv


@jax.custom_vjp
def _solve_with_inverse(inv, neg_lower, rhs):
    return idot.nn(inv, rhs)


def _solve_fwd(inv, neg_lower, rhs):
    x = idot.nn(inv, rhs)
    return x, (inv, x)


def _solve_bwd(resid, ct):
    inv, x = resid
    d_rhs = idot.tn(inv, ct)
    return jnp.zeros_like(inv), idot.nt(d_rhs, x), d_rhs


_solve_with_inverse.defvjp(_solve_fwd, _solve_bwd)


def f_gdn_intra(q, k, v, gc, beta, inv=None):
    C = GDN_CHUNK
    causal, strict = _tri(C), _tri(C, strict=True)
    is_last = _iota2((C, HEAD_DIM), 0) == C - 1
    outs = [[] for _ in range(6 if inv is None else 5)]
    for h in range(GDN_HEADS):
        sl = slice(h * HEAD_DIM, (h + 1) * HEAD_DIM)
        qh, kh, vh, gh, bh = q[:, sl], k[:, sl], v[:, sl], gc[:, sl], beta[:, sl]
        gdiff = gh - gh.T
        decay = jnp.where(causal, jnp.exp(jnp.where(causal, gdiff, 0.0)), 0.0)
        kb = kh * bh
        neg_lower = jnp.where(strict, -(idot(kb, kh.T) * decay), 0.0)
        rhs = jnp.concatenate([vh * bh, kb * jnp.exp(gh)], axis=1)
        if inv is None:
            inv_h = _unit_lower_inverse(neg_lower)
            sol = idot.nn(inv_h, rhs)
        else:
            sol = _solve_with_inverse(inv[:, sl], neg_lower, rhs)
        qk = jnp.where(causal, idot(qh, kh.T) * decay, 0.0)
        g_last = jnp.sum(jnp.where(is_last, gh, 0.0), axis=0, keepdims=True)
        vals = (sol[:, :HEAD_DIM], sol[:, HEAD_DIM:], qk, kh * jnp.exp(g_last - gh), qh * jnp.exp(gh))
        for lst, val in zip(outs, vals + ((inv_h,) if inv is None else ())):
            lst.append(val)
    return tuple(jnp.concatenate(o, axis=1) for o in outs)


def f_gdn_post(o, z, gain):
    z = z.astype(F32)
    parts = []
    for h in range(GDN_HEADS):
        oh = o[:, h * HEAD_DIM:(h + 1) * HEAD_DIM]
        parts.append(oh * lax.rsqrt(jnp.mean(oh * oh, axis=-1, keepdims=True) + EPS) * gain)
    return (jnp.concatenate(parts, axis=1) * _silu(z),)


def f_mem_attn(q, k, v):
    q = q.astype(F32)
    lane_head = _iota2((1, MEM_WIDTH), 1) // MEM_HEAD_DIM
    out = jnp.zeros(q.shape, F32)
    kt = k.astype(F32).T
    for h in range(MEM_HEADS):
        mask = (lane_head == h).astype(F32)
        logits = bdot(q * mask, kt) * (MEM_HEAD_DIM ** -0.5)
        p = jnp.exp(logits - jnp.max(logits, axis=-1, keepdims=True))
        p = p / jnp.sum(p, axis=-1, keepdims=True)
        out = out + bdot(p, v) * mask
    return (out,)


def f_loss(y, t):
    d = y - t
    return (d * d,)


def conv_fwd(proj, w8, *, width, tile=512):
    n_rows = proj.shape[0]
    tile = min(tile, n_rows)

    def body(x_ref, halo_ref, w_ref, o_ref):
        i = pl.program_id(0)
        halo = jnp.where(i > 0, halo_ref[...].astype(F32), 0.0)
        xs = jnp.concatenate([halo, x_ref[...].astype(F32)], axis=0)
        acc = xs[8:] * w_ref[3:4, :]
        for j in range(CONV_WIDTH - 1):
            acc = acc + pltpu.roll(xs, CONV_WIDTH - 1 - j, 0)[8:] * w_ref[j:j + 1, :]
        o_ref[...] = acc

    return _pcall(body, name="gdn_conv_fwd", out_shape=jax.ShapeDtypeStruct((n_rows, width), F32),
                  grid=(n_rows // tile,),
                  in_specs=[pl.BlockSpec((tile, width), lambda i: (i, 0)),
                            pl.BlockSpec((8, width), lambda i: (jnp.maximum(i * (tile // 8) - 1, 0), 0)),
                            pl.BlockSpec((8, width), lambda i: (0, 0))],
                  out_specs=pl.BlockSpec((tile, width), lambda i: (i, 0)), sem=("parallel",))(proj, proj, w8)


def conv_bwd(proj, w8, dy, *, width, tile=512):
    n_rows = proj.shape[0]
    tile = min(tile, n_rows)
    n = n_rows // tile

    def body(x_ref, xhalo_ref, w_ref, dy_ref, dyhalo_ref, dx_ref, dw_ref):
        i = pl.program_id(0)
        dy = dy_ref[...]
        after = jnp.where(i < n - 1, dyhalo_ref[...], 0.0)
        ds = jnp.concatenate([dy, after], axis=0)
        dx = dy * w_ref[3:4, :]
        for j in range(CONV_WIDTH - 1):
            shift = CONV_WIDTH - 1 - j
            dx = dx + pltpu.roll(ds, tile + 8 - shift, 0)[:tile] * w_ref[j:j + 1, :]
        dx_ref[...] = dx.astype(dx_ref.dtype)
        halo = jnp.where(i > 0, xhalo_ref[...].astype(F32), 0.0)
        xs = jnp.concatenate([halo, x_ref[...].astype(F32)], axis=0)
        rows = [jnp.sum(dy * pltpu.roll(xs, CONV_WIDTH - 1 - j, 0)[8:], axis=0, keepdims=True)
                for j in range(CONV_WIDTH - 1)]
        rows.append(jnp.sum(dy * xs[8:], axis=0, keepdims=True))
        dw = jnp.concatenate(rows + [jnp.zeros((8 - CONV_WIDTH, width), F32)], axis=0)

        @pl.when(i == 0)
        def _():
            dw_ref[...] = dw

        @pl.when(i > 0)
        def _():
            dw_ref[...] += dw

    t8 = tile // 8
    return _pcall(body, name="gdn_conv_bwd",
                  out_shape=[jax.ShapeDtypeStruct((n_rows, width), BF16), jax.ShapeDtypeStruct((8, width), F32)],
                  grid=(n,),
                  in_specs=[pl.BlockSpec((tile, width), lambda i: (i, 0)),
                            pl.BlockSpec((8, width), lambda i: (jnp.maximum(i * t8 - 1, 0), 0)),
                            pl.BlockSpec((8, width), lambda i: (0, 0)),
                            pl.BlockSpec((tile, width), lambda i: (i, 0)),
                            pl.BlockSpec((8, width), lambda i: (jnp.minimum((i + 1) * t8, n * t8 - 1), 0))],
                  out_specs=[pl.BlockSpec((tile, width), lambda i: (i, 0)), pl.BlockSpec((8, width), lambda i: (0, 0))],
                  sem=("arbitrary",))(proj, proj, w8, dy, dy)


def gdn_scan_fwd(u, w, qk, kt, qh, gc):
    n_rows = u.shape[0]
    C, n = GDN_CHUNK, u.shape[0] // GDN_CHUNK

    def body(u_ref, w_ref, qk_ref, kt_ref, qh_ref, gc_ref, o_ref, vn_ref, sin_ref, st):
        @pl.when(pl.program_id(0) == 0)
        def _():
            st[...] = jnp.zeros_like(st)

        sin_ref[0] = st[...]
        for h in range(GDN_HEADS):
            sl = slice(h * HEAD_DIM, (h + 1) * HEAD_DIM)
            s = st[sl, :]
            v_new = u_ref[:, sl] - sdot(w_ref[:, sl], s)
            o_ref[:, sl] = sdot(qh_ref[:, sl], s) + sdot(qk_ref[:, sl], v_new)
            vn_ref[:, sl] = v_new
            st[sl, :] = s * jnp.exp(gc_ref[C - 1:C, sl]) + sdot.tn(kt_ref[:, sl], v_new)

    blk = pl.BlockSpec((C, GDN_WIDTH), lambda i: (i, 0))
    return _pcall(body, name="gdn_scan_fwd",
                  out_shape=[jax.ShapeDtypeStruct((n_rows, GDN_WIDTH), F32), jax.ShapeDtypeStruct((n_rows, GDN_WIDTH), F32),
                             jax.ShapeDtypeStruct((n, GDN_WIDTH, HEAD_DIM), F32)],
                  grid=(n,), in_specs=[blk] * 6,
                  out_specs=[blk, blk, pl.BlockSpec((1, GDN_WIDTH, HEAD_DIM), lambda i: (i, 0, 0))],
                  scratch=[pltpu.VMEM((GDN_WIDTH, HEAD_DIM), F32)], sem=("arbitrary",))(u, w, qk, kt, qh, gc)


def gdn_scan_bwd(do, w, qk, kt, qh, gc, vn, sin):
    n_rows = do.shape[0]
    C, n = GDN_CHUNK, do.shape[0] // GDN_CHUNK

    def body(do_ref, w_ref, qk_ref, kt_ref, qh_ref, gc_ref, vn_ref, sin_ref,
             du_ref, dw_ref, dqk_ref, dkt_ref, dqh_ref, dgl_ref, dst):
        @pl.when(pl.program_id(0) == 0)
        def _():
            dst[...] = jnp.zeros_like(dst)

        for h in range(GDN_HEADS):
            sl = slice(h * HEAD_DIM, (h + 1) * HEAD_DIM)
            s, ds_out, d_o, v_new = sin_ref[0, sl, :], dst[sl, :], do_ref[:, sl], vn_ref[:, sl]
            e = jnp.exp(gc_ref[C - 1:C, sl])
            dvn = sdot.tn(qk_ref[:, sl], d_o) + sdot(kt_ref[:, sl], ds_out)
            du_ref[:, sl] = dvn
            dw_ref[:, sl] = -sdot.nt(dvn, s)
            dqk_ref[:, sl] = sdot.nt(d_o, v_new)
            dkt_ref[:, sl] = sdot.nt(v_new, ds_out)
            dqh_ref[:, sl] = sdot.nt(d_o, s)
            dgl = jnp.sum(ds_out * s, axis=0, keepdims=True) * e
            dgl_ref[:, sl] = jnp.broadcast_to(dgl, (8, HEAD_DIM))
            dst[sl, :] = sdot.tn(qh_ref[:, sl], d_o) + e * ds_out - sdot.tn(w_ref[:, sl], dvn)

    blk = pl.BlockSpec((C, GDN_WIDTH), lambda i: (n - 1 - i, 0))
    row = jax.ShapeDtypeStruct((n_rows, GDN_WIDTH), F32)
    return _pcall(body, name="gdn_scan_bwd",
                  out_shape=[row] * 5 + [jax.ShapeDtypeStruct((n * 8, GDN_WIDTH), F32)],
                  grid=(n,), in_specs=[blk] * 7 + [pl.BlockSpec((1, GDN_WIDTH, HEAD_DIM), lambda i: (n - 1 - i, 0, 0))],
                  out_specs=[blk] * 5 + [pl.BlockSpec((8, GDN_WIDTH), lambda i: (n - 1 - i, 0))],
                  scratch=[pltpu.VMEM((GDN_WIDTH, HEAD_DIM), F32)], sem=("arbitrary",))(do, w, qk, kt, qh, gc, vn, sin)


def gdn_intra_bwd(q, k, v, gc, beta, inv, cts, dgl):
    n_rows = q.shape[0]
    C = GDN_CHUNK

    def body(*refs):
        ins = [r[...] for r in refs[:5]]
        inv_v = refs[5][...]
        ct = tuple(r[...] for r in refs[6:11])
        dgl_v = refs[11][...]
        _, vjp = jax.vjp(lambda *a: f_gdn_intra(*a, inv=inv_v), *ins)
        grads = list(vjp(ct))
        last = _iota2((C, GDN_WIDTH), 0) == C - 1
        grads[3] = grads[3] + jnp.where(last, jnp.broadcast_to(dgl_v[0:1, :], (C, GDN_WIDTH)), 0.0)
        for o_ref, g in zip(refs[12:], grads):
            o_ref[...] = g

    blk = pl.BlockSpec((C, GDN_WIDTH), lambda i: (i, 0))
    return _pcall(body, name="gdn_intra_bwd", out_shape=[jax.ShapeDtypeStruct((n_rows, GDN_WIDTH), F32)] * 5,
                  grid=(n_rows // C,), in_specs=[blk] * 11 + [pl.BlockSpec((8, GDN_WIDTH), lambda i: (i, 0))],
                  out_specs=[blk] * 5, sem=("parallel",))(q, k, v, gc, beta, inv, *cts, dgl)


def fox_gate_fwd(f, b_f):
    n_rows = f.shape[0]
    T = LANES

    def body(f_ref, b_ref, cb_ref, crow_ref, carry):
        @pl.when(pl.program_id(0) == 0)
        def _():
            carry[...] = jnp.zeros_like(carry)

        c = fdot(_tri(T).astype(F32), _log_sigmoid(f_ref[...] + b_ref[...])) + carry[...]
        carry[...] = c[T - 1:T, :]
        cb_ref[...] = fdot(c, _head_sel(0))
        ct = c.T
        for h in range(FOX_HEADS):
            crow_ref[h] = ct[h:h + 1, :]

    return _pcall(body, name="fox_gate_fwd",
                  out_shape=[jax.ShapeDtypeStruct((n_rows, FOX_WIDTH), F32), jax.ShapeDtypeStruct((FOX_HEADS, 1, n_rows), F32)],
                  grid=(n_rows // T,),
                  in_specs=[pl.BlockSpec((T, LANES), lambda i: (i, 0)), pl.BlockSpec((1, LANES), lambda i: (0, 0))],
                  out_specs=[pl.BlockSpec((T, FOX_WIDTH), lambda i: (i, 0)), pl.BlockSpec((FOX_HEADS, 1, T), lambda i: (0, 0, i))],
                  scratch=[pltpu.VMEM((1, LANES), F32)], sem=("arbitrary",))(f, b_f)


def fox_gate_bwd(f, b_f, dcrow, dcb):
    n_rows = f.shape[0]
    T = LANES
    n = n_rows // T

    def body(f_ref, b_ref, dc_ref, dcb_ref, df_ref, db_ref, carry):
        i = pl.program_id(0)

        @pl.when(i == 0)
        def _():
            carry[...] = jnp.zeros_like(carry)

        rows = [dc_ref[h] for h in range(FOX_HEADS)] + [jnp.zeros((T - FOX_HEADS, T), F32)]
        first_lane = (_iota2((FOX_WIDTH, LANES), 0) == _iota2((FOX_WIDTH, LANES), 1) * HEAD_DIM).astype(F32)
        dc = jnp.concatenate(rows, axis=0).T + fdot(dcb_ref[...], first_lane)
        dlog = fdot.tn(_tri(T).astype(F32), dc) + carry[...]
        carry[...] = dlog[0:1, :]
        df = dlog * (1.0 - _sigmoid(f_ref[...] + b_ref[...]))
        df_ref[...] = df
        db = jnp.sum(df, axis=0, keepdims=True)

        @pl.when(i == 0)
        def _():
            db_ref[...] = db

        @pl.when(i > 0)
        def _():
            db_ref[...] += db

    return _pcall(body, name="fox_gate_bwd",
                  out_shape=[jax.ShapeDtypeStruct((n_rows, LANES), F32), jax.ShapeDtypeStruct((1, LANES), F32)],
                  grid=(n,),
                  in_specs=[pl.BlockSpec((T, LANES), lambda i: (n - 1 - i, 0)), pl.BlockSpec((1, LANES), lambda i: (0, 0)),
                            pl.BlockSpec((FOX_HEADS, 1, T), lambda i: (0, 0, n - 1 - i)),
                            pl.BlockSpec((T, FOX_WIDTH), lambda i: (n - 1 - i, 0))],
                  out_specs=[pl.BlockSpec((T, LANES), lambda i: (n - 1 - i, 0)), pl.BlockSpec((1, LANES), lambda i: (0, 0))],
                  scratch=[pltpu.VMEM((1, LANES), F32)], sem=("arbitrary",))(f, b_f, dcrow, dcb)


def _fox_tile(n_rows):
    return min(512, n_rows)


def _fox_pairs(n, query_major):
    pairs = [(i, j) for i in range(n) for j in range(i + 1)] if query_major else [(i, j) for j in range(n) for i in range(j, n)]
    return jnp.asarray([p[0] for p in pairs], jnp.int32), jnp.asarray([p[1] for p in pairs], jnp.int32)


def _pcall_tables(body, *, name, out_shape, grid, tables, in_specs, out_specs, scratch, sem):
    spec = pltpu.PrefetchScalarGridSpec(num_scalar_prefetch=len(tables), grid=grid, in_specs=in_specs, out_specs=out_specs,
                                        scratch_shapes=list(scratch))
    return pl.pallas_call(body, name=name, out_shape=out_shape, grid_spec=spec,
                          compiler_params=pltpu.CompilerParams(vmem_limit_bytes=VMEM_LIMIT_BYTES, dimension_semantics=sem))


def _fox_logits(q, k, ccol, crow, masked, t):
    s = bdot.nt(q, k) * (HEAD_DIM ** -0.5) + jnp.tile(ccol, (1, t // LANES)) - crow
    if masked:
        s = jnp.where(_iota2((t, t), 0) >= _iota2((t, t), 1), s, NEG_INF)
    return s


def _fox_p_ds(masked, t, q_ref, k_ref, v_ref, cb_ref, crow_ref, o_ref, lse_ref, do_ref):
    s = _fox_logits(q_ref[...], k_ref[...], cb_ref[...], crow_ref[0], masked, t)
    p = jnp.exp(s - jnp.tile(lse_ref[...], (1, t // LANES)))
    d_o = do_ref[...].astype(F32)
    delta = jnp.sum(d_o * o_ref[...].astype(F32), axis=-1, keepdims=True)
    return p, p * (bdot.nt(d_o, v_ref[...]) - delta), d_o


def _fox_specs(t, do_col, query_major_tables=True):
    qspec = pl.BlockSpec((t, HEAD_DIM), lambda h, p, it, jt: (it[p], h))
    dospec = pl.BlockSpec((t, HEAD_DIM), lambda h, p, it, jt: (it[p], do_col + h))
    kspec = pl.BlockSpec((t, HEAD_DIM), lambda h, p, it, jt: (jt[p], h))
    vspec = pl.BlockSpec((t, HEAD_DIM), lambda h, p, it, jt: (jt[p], FOX_HEADS + h))
    cspec = pl.BlockSpec((1, 1, t), lambda h, p, it, jt: (h, 0, jt[p]))
    return qspec, dospec, kspec, vspec, cspec


def fox_fwd(q, kv, cb, crow):
    n_rows = kv.shape[0]
    t = _fox_tile(n_rows)
    n = n_rows // t
    tables = _fox_pairs(n, True)

    def body(it, jt, q_ref, k_ref, v_ref, cb_ref, crow_ref, o_ref, lse_ref, m_sc, l_sc, acc):
        i, j = it[pl.program_id(1)], jt[pl.program_id(1)]

        @pl.when(j == 0)
        def _():
            m_sc[...] = jnp.full(m_sc.shape, NEG_INF, F32)
            l_sc[...] = jnp.zeros_like(l_sc)
            acc[...] = jnp.zeros_like(acc)

        def step(masked):
            s = _fox_logits(q_ref[...], k_ref[...], cb_ref[...], crow_ref[0], masked, t)
            m_old = m_sc[...]
            m_new = jnp.maximum(m_old, jnp.max(s, axis=-1, keepdims=True))
            alpha = jnp.exp(m_old - m_new)
            p = jnp.exp(s - jnp.tile(m_new, (1, t // LANES)))
            l_sc[...] = l_sc[...] * alpha + jnp.sum(p, axis=-1, keepdims=True)
            acc[...] = acc[...] * alpha + bdot(p, v_ref[...])
            m_sc[...] = m_new

        @pl.when(j < i)
        def _():
            step(False)

        @pl.when(j == i)
        def _():
            step(True)
            o_ref[...] = (acc[...] / l_sc[...]).astype(o_ref.dtype)
            lse_ref[...] = m_sc[...] + jnp.log(l_sc[...])

    qspec, _, kspec, vspec, cspec = _fox_specs(t, 0)
    return _pcall_tables(body, name="fox_fwd",
                         out_shape=[jax.ShapeDtypeStruct((n_rows, FOX_WIDTH), BF16), jax.ShapeDtypeStruct((n_rows, FOX_WIDTH), F32)],
                         grid=(FOX_HEADS, n * (n + 1) // 2), tables=tables,
                         in_specs=[qspec, kspec, vspec, qspec, cspec], out_specs=[qspec, qspec],
                         scratch=[pltpu.VMEM((t, HEAD_DIM), F32)] * 3, sem=("parallel", "arbitrary"))(*tables, q, kv, kv, cb, crow)


def fox_bwd_dq(q, kv, cb, crow, o, lse, do, prev=None):
    do, _, do_col = do
    do_col *= FOX_HEADS
    n_rows = kv.shape[0]
    t = _fox_tile(n_rows)
    n = n_rows // t
    n_prev = 0 if prev is None else 1
    tables = _fox_pairs(n, True)

    def body(it, jt, q_ref, k_ref, v_ref, cb_ref, crow_ref, o_ref, lse_ref, do_ref, *rest):
        dq_ref, drow_ref, acc, rows = rest[n_prev:]
        i, j = it[pl.program_id(1)], jt[pl.program_id(1)]

        @pl.when(j == 0)
        def _():
            acc[...] = jnp.zeros_like(acc)
            rows[...] = jnp.zeros_like(rows)

        def step(masked):
            _, ds, _ = _fox_p_ds(masked, t, q_ref, k_ref, v_ref, cb_ref, crow_ref, o_ref, lse_ref, do_ref)
            acc[...] += bdot(ds, k_ref[...])
            rows[...] += jnp.sum(ds, axis=-1, keepdims=True)

        @pl.when(j < i)
        def _():
            step(False)

        @pl.when(j == i)
        def _():
            step(True)
            dq_ref[...] = (acc[...] * (HEAD_DIM ** -0.5)).astype(dq_ref.dtype)
            drow_ref[...] = rows[...] + rest[0][...] if n_prev else rows[...]

    qspec, dospec, kspec, vspec, cspec = _fox_specs(t, do_col)
    return _pcall_tables(body, name="fox_bwd_dq" + ("_acc" if n_prev else ""),
                         out_shape=[jax.ShapeDtypeStruct((n_rows, FOX_WIDTH), BF16), jax.ShapeDtypeStruct((n_rows, FOX_WIDTH), F32)],
                         grid=(FOX_HEADS, n * (n + 1) // 2), tables=tables,
                         in_specs=[qspec, kspec, vspec, qspec, cspec, qspec, qspec, dospec] + [qspec] * n_prev,
                         out_specs=[qspec, qspec], scratch=[pltpu.VMEM((t, HEAD_DIM), F32)] * 2,
                         sem=("parallel", "arbitrary"))(*tables, q, kv, kv, cb, crow, o, lse, do, *([prev] if n_prev else []))


def fox_bwd_dkv(q, kv, cb, crow, o, lse, do, prev=None):
    do, _, do_col = do
    do_col *= FOX_HEADS
    n_rows = kv.shape[0]
    t = _fox_tile(n_rows)
    n = n_rows // t
    n_prev = 0 if prev is None else 3
    tables = _fox_pairs(n, False)

    def body(it, jt, q_ref, k_ref, v_ref, cb_ref, crow_ref, o_ref, lse_ref, do_ref, *rest):
        prev_refs = rest[:n_prev]
        dk_ref, dv_ref, dc_ref, dk_acc, dv_acc, dc_acc = rest[n_prev:]
        i, j = it[pl.program_id(1)], jt[pl.program_id(1)]

        def step(masked):
            p, ds, d_o = _fox_p_ds(masked, t, q_ref, k_ref, v_ref, cb_ref, crow_ref, o_ref, lse_ref, do_ref)
            dv_acc[...] += bdot.tn(p, d_o)
            dk_acc[...] += bdot.tn(ds, q_ref[...])
            dc_acc[...] -= jnp.sum(ds, axis=0, keepdims=True)

        @pl.when(i == j)
        def _():
            dk_acc[...] = jnp.zeros_like(dk_acc)
            dv_acc[...] = jnp.zeros_like(dv_acc)
            dc_acc[...] = jnp.zeros_like(dc_acc)
            step(True)

        @pl.when(i > j)
        def _():
            step(False)

        @pl.when(i == n - 1)
        def _():
            dk, dv, dc = dk_acc[...] * (HEAD_DIM ** -0.5), dv_acc[...], dc_acc[...]
            if n_prev:
                dk, dv, dc = dk + prev_refs[0][...], dv + prev_refs[1][...], dc + prev_refs[2][0]
            dk_ref[...] = dk
            dv_ref[...] = dv
            dc_ref[0] = dc

    qspec, dospec, kspec, vspec, cspec = _fox_specs(t, do_col)
    return _pcall_tables(body, name="fox_bwd_dkv" + ("_acc" if n_prev else ""),
                         out_shape=[jax.ShapeDtypeStruct((n_rows, FOX_WIDTH), F32), jax.ShapeDtypeStruct((n_rows, FOX_WIDTH), F32),
                                    jax.ShapeDtypeStruct((FOX_HEADS, 1, n_rows), F32)],
                         grid=(FOX_HEADS, n * (n + 1) // 2), tables=tables,
                         in_specs=[qspec, kspec, vspec, qspec, cspec, qspec, qspec, dospec] + [kspec, kspec, cspec][:n_prev],
                         out_specs=[kspec, kspec, cspec],
                         scratch=[pltpu.VMEM((t, HEAD_DIM), F32), pltpu.VMEM((t, HEAD_DIM), F32), pltpu.VMEM((1, t), F32)],
                         sem=("parallel", "arbitrary"))(*tables, q, kv, kv, cb, crow, o, lse, do, *(prev or ()))


def loss_head(h, gain, target, *, tile=512):
    n_rows, d = h.shape
    tile = min(tile, n_rows)

    def body(h_ref, g_ref, t_ref, part_ref, dy_ref):
        (y,) = f_rmsnorm(h_ref[...], g_ref[...])
        diff = y - t_ref[...]
        dy_ref[...] = diff * (1.0 / d)
        part = jnp.sum(diff * diff, axis=0, keepdims=True)
        first = pl.program_id(0) == 0

        @pl.when(first)
        def _():
            part_ref[...] = part

        @pl.when(jnp.logical_not(first))
        def _():
            part_ref[...] += part

    blk = pl.BlockSpec((tile, d), lambda i: (i, 0))
    one = pl.BlockSpec((1, d), lambda i: (0, 0))
    return _pcall(body, name="loss_head",
                  out_shape=[jax.ShapeDtypeStruct((1, d), F32), jax.ShapeDtypeStruct((n_rows, d), F32)],
                  grid=(n_rows // tile,), in_specs=[blk, one, blk], out_specs=[one, blk], sem=("arbitrary",))(h, gain, target)


def adamw(w, g, m, v, *, name):
    shape = w.shape
    cols = shape[-1] if w.ndim >= 2 else w.size
    rows = w.size // cols
    tile = _pick(rows, (256, 128, 64, 32, 16, 8))
    as2d = lambda a: a.reshape(rows, cols)

    def body(w_ref, g_ref, m_ref, v_ref, d_ref, nm_ref, nv_ref):
        g_ = g_ref[...]
        m_ = ADAM_B1 * m_ref[...] + (1.0 - ADAM_B1) * g_
        v_ = ADAM_B2 * v_ref[...] + (1.0 - ADAM_B2) * (g_ * g_)
        m_hat = m_ / (1.0 - ADAM_B1 ** ADAM_STEP)
        v_hat = v_ / (1.0 - ADAM_B2 ** ADAM_STEP)
        d_ref[...] = -ADAM_LR * (m_hat / (jnp.sqrt(v_hat) + ADAM_EPS) + ADAM_WD * w_ref[...])
        nm_ref[...] = m_
        nv_ref[...] = v_

    blk = pl.BlockSpec((tile, cols), lambda i: (i, 0))
    outs = _pcall(body, name=name, out_shape=[jax.ShapeDtypeStruct((rows, cols), F32)] * 3, grid=(rows // tile,),
                  in_specs=[blk] * 4, out_specs=[blk] * 3, sem=("parallel",))(as2d(w), as2d(g), as2d(m), as2d(v))
    return tuple(o.reshape(shape) for o in outs)


def add_pairs(a, b, *, out_dtype, name):
    p, r, c = a.shape
    tile = _pick(r, (512, 256, 128, 64, 32, 16, 8))

    def body(a_ref, b_ref, o_ref):
        o_ref[...] = (a_ref[...].astype(F32) + b_ref[...].astype(F32)).astype(o_ref.dtype)

    blk = pl.BlockSpec((1, tile, c), lambda k, i: (k, i, 0))
    return _pcall(body, name=name, out_shape=jax.ShapeDtypeStruct((p, r, c), out_dtype), grid=(p, r // tile),
                  in_specs=[blk, blk], out_specs=blk, sem=("parallel", "parallel"))(a, b)


def sum_leading(a, *, name):
    p, r, c = a.shape
    tile = _pick(r, (256, 128, 64, 32, 16, 8))

    def body(a_ref, o_ref):
        total = a_ref[0].astype(F32)
        for k in range(1, p):
            total = total + a_ref[k].astype(F32)
        o_ref[...] = total

    return _pcall(body, name=name, out_shape=jax.ShapeDtypeStruct((r, c), F32), grid=(r // tile,),
                  in_specs=[pl.BlockSpec((p, tile, c), lambda i: (0, i, 0))],
                  out_specs=pl.BlockSpec((tile, c), lambda i: (i, 0)), sem=("parallel",))(a)


_HBM = pl.BlockSpec(memory_space=pltpu.HBM)


def _comm_call(body, *, name, out_shape, n_in, scratch):
    return pl.pallas_call(body, name=name, out_shape=out_shape, in_specs=[_HBM] * n_in, out_specs=_HBM,
                          scratch_shapes=scratch,
                          compiler_params=pltpu.CompilerParams(has_side_effects=True))


def all_gather8(a, *, name):
    m_per, n = a.shape

    def body(x_ref, out_ref, send_sems, recv_sems, local_sem):
        x, y, c = lax.axis_index("x"), lax.axis_index("y"), lax.axis_index("c")
        me, sibling = (x, y, c), (x, y, 1 - c)
        chips = [(1 - x, y), (x, 1 - y), (1 - x, 1 - y)]

        def rows(px, py, pc):
            return out_ref.at[pl.ds((4 * px + 2 * py + pc) * m_per, m_per), :]

        def copy(k, block, to, src=None):
            return pltpu.make_async_remote_copy(
                src_ref=rows(*block) if src is None else src, dst_ref=rows(*block),
                send_sem=send_sems.at[k], recv_sem=recv_sems.at[k], device_id=to, device_id_type=MESH)

        mine = pltpu.make_async_copy(x_ref, rows(*me), local_sem)
        mine.start()
        first = [copy(0, me, sibling, src=x_ref)]
        first += [copy(1 + j, me, (*chip, c), src=x_ref) for j, chip in enumerate(chips)]
        for cp in first:
            cp.start()
        passed = [copy(4 + j, (*chip, c), sibling) for j, chip in enumerate(chips)]
        for j, chip in enumerate(chips):
            copy(1 + j, (*chip, c), me).wait_recv()
            passed[j].start()
        copy(0, sibling, me).wait_recv()
        for j, chip in enumerate(chips):
            copy(4 + j, (*chip, 1 - c), me).wait_recv()
        for cp in first + passed:
            cp.wait_send()
        mine.wait()

    return _comm_call(body, name=name, out_shape=jax.ShapeDtypeStruct((N_DEV * m_per, n), a.dtype), n_in=1,
                      scratch=[pltpu.SemaphoreType.DMA((7,)), pltpu.SemaphoreType.DMA((7,)), pltpu.SemaphoreType.DMA])(a)


def sibling_swap(a, *, name):
    def body(a_ref, out_ref, send_sem, recv_sem):
        x, y, c = lax.axis_index("x"), lax.axis_index("y"), lax.axis_index("c")
        cp = pltpu.make_async_remote_copy(src_ref=a_ref, dst_ref=out_ref, send_sem=send_sem, recv_sem=recv_sem,
                                          device_id=(x, y, 1 - c), device_id_type=MESH)
        cp.start()
        cp.wait()

    return _comm_call(body, name=name, out_shape=jax.ShapeDtypeStruct(a.shape, a.dtype), n_in=1,
                      scratch=[pltpu.SemaphoreType.DMA, pltpu.SemaphoreType.DMA])(a)


def chip_all_to_all(a, *, name):
    def body(a_ref, out_ref, send_sems, recv_sems, local_sem):
        x, y, c = lax.axis_index("x"), lax.axis_index("y"), lax.axis_index("c")
        me = 2 * x + y
        mine = pltpu.make_async_copy(a_ref.at[me], out_ref.at[me], local_sem)
        mine.start()
        copies = []
        for j, (px, py) in enumerate([(1 - x, y), (x, 1 - y), (1 - x, 1 - y)]):
            copies.append(pltpu.make_async_remote_copy(
                src_ref=a_ref.at[2 * px + py], dst_ref=out_ref.at[me], send_sem=send_sems.at[j],
                recv_sem=recv_sems.at[j], device_id=(px, py, c), device_id_type=MESH))
        for cp in copies:
            cp.start()
        for cp in copies:
            cp.wait()
        mine.wait()

    return _comm_call(body, name=name, out_shape=jax.ShapeDtypeStruct(a.shape, a.dtype), n_in=1,
                      scratch=[pltpu.SemaphoreType.DMA((3,)), pltpu.SemaphoreType.DMA((3,)), pltpu.SemaphoreType.DMA])(a)


PACK_COLS = 1024
PACK_ROW_MULTIPLE = 32

WEIGHT_NAMES = ["ffn1_norm", "ffn1_w_gate_up", "ffn1_w_down", "mix_norm", "ffn2_norm", "ffn2_w_gate_up", "ffn2_w_down",
                "gdn_w_in", "gdn_conv", "gdn_A_log", "gdn_dt_bias", "gdn_out_norm", "fox_w_in", "w_out", "mem_norm",
                "mem_w_kv", "kv_norm", "kv_w", "kv_b_f", "final_norm"]
SHARDED = [("ffn1_w_gate_up", 2), ("ffn1_w_down", 1), ("ffn2_w_gate_up", 2), ("ffn2_w_down", 1), ("gdn_w_in", 2),
           ("gdn_conv", 2), ("fox_w_in", 1), ("w_out", 1), ("mem_w_kv", 1), ("kv_w", 0)]
REPLICATED = [n for n in WEIGHT_NAMES if n not in dict(SHARDED)]


PACK_PIECE_ROWS = 16


def _rows_of(size):
    return -(-size // (PACK_COLS * PACK_PIECE_ROWS)) * PACK_PIECE_ROWS


def pack(pieces, dtype, row_multiple=PACK_ROW_MULTIPLE):
    bufs, total = [], 0
    for p in pieces:
        flat = p.astype(dtype).reshape(-1)
        rows = _rows_of(flat.size)
        bufs.append(jnp.pad(flat, (0, rows * PACK_COLS - flat.size)).reshape(rows, PACK_COLS))
        total += rows
    pad = -total % row_multiple
    if pad:
        bufs.append(jnp.zeros((pad, PACK_COLS), dtype))
    return jnp.concatenate(bufs, axis=0)


def unpack(buf, shapes):
    out, row = [], 0
    for shape in shapes:
        size = 1
        for s in shape:
            size *= s
        rows = _rows_of(size)
        out.append(buf[row:row + rows].reshape(-1)[:size].reshape(shape))
        row += rows
    return out


def _row(vec, width=None):
    vec = vec.astype(F32).reshape(1, -1)
    if width is not None and vec.shape[1] < width:
        vec = jnp.pad(vec, ((0, 0), (0, width - vec.shape[1])))
    return vec


ROW_TILE = 512
GDN_PROJ_WIDTH = 4 * GDN_WIDTH + MEM_WIDTH + LANES
GDN_Z_COL, GDN_QMEM_COL, GDN_AB_COL = 3, 4 * GDN_WIDTH // MEM_WIDTH, (4 * GDN_WIDTH + MEM_WIDTH) // LANES
FOX_QMEM_COL = FOX_WIDTH // MEM_WIDTH
KV_PAD_WIDTH = 2 * FOX_WIDTH + LANES


def rms_fwd(x, gain_row, out_dtype=BF16):
    return rowwise(f_rmsnorm, [x], [gain_row], [(x.shape[1], out_dtype)], tile=ROW_TILE, name="rms_fwd")[0]


def rms_bwd(x, gain_row, dy, dres=None):
    return rowwise_bwd(f_rmsnorm, [x], [gain_row], [dy], tile=ROW_TILE, name="rms_bwd", row_grads=[F32],
                       const_grads=[True], add=None if dres is None else {0: dres})


def _ffn_tiles(n_rows):
    return _pick(n_rows, (512, 256, 128)), _pick(FFN_HIDDEN, (1408, 256, 128))


def ffn_up_act(n, wgu):
    n_rows, d = n.shape
    tm, tn = _ffn_tiles(n_rows)
    nj = FFN_HIDDEN // tn

    def body(n_ref, wg_ref, wu_ref, gu_ref, act_ref):
        x = n_ref[...].astype(BF16)
        g = bdot.nn(x, wg_ref[...])
        u = bdot.nn(x, wu_ref[...])
        gu_ref[0] = g.astype(gu_ref.dtype)
        gu_ref[1] = u.astype(gu_ref.dtype)
        act_ref[...] = (_silu(g) * u).astype(act_ref.dtype)

    return _pcall(body, name="ffn_up_act",
                  out_shape=[jax.ShapeDtypeStruct((2, n_rows, FFN_HIDDEN), BF16), jax.ShapeDtypeStruct((n_rows, FFN_HIDDEN), BF16)],
                  grid=(nj, n_rows // tm),
                  in_specs=[pl.BlockSpec((tm, d), lambda j, i: (i, 0)), pl.BlockSpec((d, tn), lambda j, i: (0, j)),
                            pl.BlockSpec((d, tn), lambda j, i: (0, nj + j))],
                  out_specs=[pl.BlockSpec((2, tm, tn), lambda j, i: (0, i, j)), pl.BlockSpec((tm, tn), lambda j, i: (i, j))],
                  sem=("parallel", "parallel"))(n, wgu, wgu)


def ffn_down_dx_act(dh, wd, gu):
    n_rows, d = dh.shape
    tm, tn = _ffn_tiles(n_rows)

    def body(dh_ref, wd_ref, gu_ref, dgu_ref):
        dact = 0.5 * bdot.nt(dh_ref[...], wd_ref[...])
        gate, up = gu_ref[0].astype(F32), gu_ref[1].astype(F32)
        sg = _sigmoid(gate)
        dgu_ref[0] = (dact * up * (sg * (1.0 + gate * (1.0 - sg)))).astype(dgu_ref.dtype)
        dgu_ref[1] = (dact * (gate * sg)).astype(dgu_ref.dtype)

    blk = pl.BlockSpec((2, tm, tn), lambda j, i: (0, i, j))
    return _pcall(body, name="ffn_down_dx_act", out_shape=jax.ShapeDtypeStruct((2, n_rows, FFN_HIDDEN), BF16),
                  grid=(FFN_HIDDEN // tn, n_rows // tm),
                  in_specs=[pl.BlockSpec((tm, d), lambda j, i: (i, 0)), pl.BlockSpec((tn, d), lambda j, i: (j, 0)), blk],
                  out_specs=blk, sem=("parallel", "parallel"))(dh, wd, gu)


def ffn_fwd(h, gain_row, wgu, wd):
    n = rms_fwd(h, gain_row)
    gu, act = ffn_up_act(n, wgu)
    return mm(act, wd, scale=0.5, res=h, name="ffn_down"), (h, n, gu, act)


def ffn_bwd(dh, saved, gain_row, wgu, wd):
    h, n, gu, act = saved
    dgu = ffn_down_dx_act(dh, wd, gu)
    dwd = mm(act, dh, ta=True, scale=0.5, name="ffn_down_dw")
    dwgu = mm(n, dgu, ta=True, b_split=True, name="ffn_up_dw")
    dn = mm(dgu, wgu, tb=True, a_split=True, out_dtype=BF16, name="ffn_up_dx")
    dh, dgain = rms_bwd(h, gain_row, dn, dh)
    return dh, dwgu, dwd, dgain


def gdn_fwd(proj, w8, a_row, dt_row, onorm_row):
    wide = [(GDN_WIDTH, F32)] * 5
    xc = conv_fwd(proj, w8, width=3 * GDN_WIDTH)
    q, k, v, gc, beta = rowwise(f_gdn_pre, [xc, (proj, LANES, GDN_AB_COL)], [a_row, dt_row], wide, tile=GDN_CHUNK,
                                name="gdn_pre_fwd")
    u, w, qk, kt, qh, inv = rowwise(f_gdn_intra, [q, k, v, gc, beta], [], wide + wide[:1], tile=GDN_CHUNK,
                                    name="gdn_intra_fwd")
    o, vn, sin = gdn_scan_fwd(u, w, qk, kt, qh, gc)
    main = rowwise(f_gdn_post, [o, (proj, GDN_WIDTH, GDN_Z_COL)], [onorm_row], [(GDN_WIDTH, BF16)], tile=ROW_TILE,
                   name="gdn_post_fwd")[0]
    return main, (xc, q, k, v, gc, beta, inv, w, qk, kt, qh, vn, sin, o)


def gdn_bwd(dmain, proj, saved, w8, a_row, dt_row, onorm_row):
    xc, q, k, v, gc, beta, inv, w, qk, kt, qh, vn, sin, o = saved
    do, dz, donorm = rowwise_bwd(f_gdn_post, [o, (proj, GDN_WIDTH, GDN_Z_COL)], [onorm_row], [dmain], tile=ROW_TILE,
                                 name="gdn_post_bwd", row_grads=[F32, BF16], const_grads=[True])
    du, dw, dqk, dkt, dqh, dgl = gdn_scan_bwd(do, w, qk, kt, qh, gc, vn, sin)
    dq, dk, dv, dgc, dbeta = gdn_intra_bwd(q, k, v, gc, beta, inv, (du, dw, dqk, dkt, dqh), dgl)
    dxc, dab, da, ddt = rowwise_bwd(f_gdn_pre, [xc, (proj, LANES, GDN_AB_COL)], [a_row, dt_row], [dq, dk, dv, dgc, dbeta],
                                    tile=GDN_CHUNK, name="gdn_pre_bwd", row_grads=[F32, BF16], const_grads=[True, True])
    dqkv, dw8 = conv_bwd(proj, w8, dxc, width=3 * GDN_WIDTH)
    return dqkv, dz, dab, dw8, da, ddt, donorm


def mem_fwd(q, kmem, vmem):
    return rowwise(f_mem_attn, [q], [kmem, vmem], [(MEM_WIDTH, BF16)], tile=ROW_TILE, name="mem_attn_fwd")[0]


def mem_bwd(q, kmem, vmem, dout):
    return rowwise_bwd(f_mem_attn, [q], [kmem, vmem], [dout], tile=ROW_TILE, name="mem_attn_bwd", row_grads=[BF16],
                       const_grads=[True, True])


def forward_backward(xs, mems, target, P):
    depth, n_a = 4, 2
    G = {}
    mem_gain = _row(P["mem_norm"])
    mem_n = rms_fwd(mems, mem_gain)
    h = xs
    saved = []
    shared = None
    for l in range(depth):
        h0 = h
        h1, s1 = ffn_fwd(h0, _row(P["ffn1_norm"][l]), P["ffn1_w_gate_up"][l], P["ffn1_w_down"][l])
        u = rms_fwd(h1, _row(P["mix_norm"][l]))
        kvm = mm(mem_n, P["mem_w_kv"][l], name="mem_kv")
        kmem, vmem = kvm[:, :MEM_WIDTH], kvm[:, MEM_WIDTH:]
        if l < n_a:
            gp = (P["conv8"][l], _row(P["gdn_A_log"][l], LANES), _row(P["gdn_dt_bias"][l], LANES), _row(P["gdn_out_norm"][l]))
            proj = mm(u, P["gdn_w_in_pad"][l], name="gdn_in")
            main, sm = gdn_fwd(proj, *gp)
            qm = (proj, MEM_WIDTH, GDN_QMEM_COL)
        else:
            proj = mm(u, P["fox_w_in"][l - n_a], out_dtype=BF16, name="fox_in")
            main, lse = fox_fwd(proj, *shared)
            sm = (main, lse)
            qm = (proj, MEM_WIDTH, FOX_QMEM_COL)
        mo = mem_fwd(qm, kmem, vmem)
        cat = jnp.concatenate([main, mo], axis=1)
        h2 = mm(cat, P["w_out"][l], res=h1, name="mix_out")
        h3, s2 = ffn_fwd(h2, _row(P["ffn2_norm"][l]), P["ffn2_w_gate_up"][l], P["ffn2_w_down"][l])
        saved.append((s1, h1, u, kmem, vmem, proj, sm, qm, cat, s2))
        h = h3
        if l == n_a - 1:
            nkv = rms_fwd(h, _row(P["kv_norm"]))
            kv = mm(nkv, P["kv_w_pad"][:, :2 * FOX_WIDTH], out_dtype=BF16, name="fox_kv")
            f = mm(nkv, P["kv_w_pad"][:, 2 * FOX_WIDTH:], name="fox_f")
            bf_row = _row(P["kv_b_f"], LANES)
            cb, crow = fox_gate_fwd(f, bf_row)
            shared = (kv, cb, crow)
            kv_saved = (h, nkv, f, bf_row)

    part, dy = loss_head(h, _row(P["final_norm"]), target)
    dh, G["final_norm"] = rms_bwd(h, _row(P["final_norm"]), dy)

    per_layer = {n: [None] * depth for n in ("ffn1_norm", "ffn1_w_gate_up", "ffn1_w_down", "mix_norm", "ffn2_norm",
                                             "ffn2_w_gate_up", "ffn2_w_down", "w_out", "mem_w_kv")}
    gdn_g = {n: [None] * n_a for n in ("gdn_w_in_pad", "conv8", "gdn_A_log", "gdn_dt_bias", "gdn_out_norm")}
    fox_g = [None] * (depth - n_a)
    dmem_n = None
    dkv_acc = dcb_acc = None
    for l in reversed(range(depth)):
        s1, h1, u, kmem, vmem, proj, sm, qm, cat, s2 = saved[l]
        if l == n_a - 1:
            hk, nkv, f, bf_row = kv_saved
            dk, dv, dcrow = dkv_acc
            df, dbf = fox_gate_bwd(f, bf_row, dcrow, dcb_acc)
            dp = jnp.concatenate([dk.astype(BF16), dv.astype(BF16), df.astype(BF16)], axis=1)
            G["kv_w_pad"] = mm(nkv, dp, ta=True, name="fox_kv_dw")
            G["kv_b_f"] = dbf
            dnkv = mm(dp, P["kv_w_pad"], tb=True, out_dtype=BF16, name="fox_kv_dx")
            dh, G["kv_norm"] = rms_bwd(hk, _row(P["kv_norm"]), dnkv, dh)
        dh, per_layer["ffn2_w_gate_up"][l], per_layer["ffn2_w_down"][l], per_layer["ffn2_norm"][l] = ffn_bwd(
            dh, s2, _row(P["ffn2_norm"][l]), P["ffn2_w_gate_up"][l], P["ffn2_w_down"][l])
        dcat = mm(dh, P["w_out"][l], tb=True, out_dtype=BF16, name="mix_out_dx")
        per_layer["w_out"][l] = mm(cat, dh, ta=True, name="mix_out_dw")
        dqm, dkm, dvm = mem_bwd(qm, kmem, vmem, (dcat, MEM_WIDTH, FOX_QMEM_COL))
        dkvm = jnp.concatenate([dkm, dvm], axis=1)
        per_layer["mem_w_kv"][l] = mm(mem_n, dkvm, ta=True, name="mem_kv_dw")
        dmem_n = mm(dkvm, P["mem_w_kv"][l], tb=True, res=dmem_n, name="mem_kv_dx")
        dmain = (dcat, GDN_WIDTH, 0)
        if l < n_a:
            gp = (P["conv8"][l], _row(P["gdn_A_log"][l], LANES), _row(P["gdn_dt_bias"][l], LANES), _row(P["gdn_out_norm"][l]))
            dqkv, dz, dab, gdn_g["conv8"][l], gdn_g["gdn_A_log"][l], gdn_g["gdn_dt_bias"][l], gdn_g["gdn_out_norm"][l] = gdn_bwd(
                dmain, proj, sm, *gp)
            dproj = jnp.concatenate([dqkv, dz, dqm, dab], axis=1)
            gdn_g["gdn_w_in_pad"][l] = mm(u, dproj, ta=True, name="gdn_in_dw")
            du = mm(dproj, P["gdn_w_in_pad"][l], tb=True, out_dtype=BF16, name="gdn_in_dx")
        else:
            o, lse = sm
            dq, dcb_acc = fox_bwd_dq(proj, *shared, o, lse, dmain, dcb_acc)
            dkv_acc = fox_bwd_dkv(proj, *shared, o, lse, dmain, dkv_acc)
            dproj = jnp.concatenate([dq, dqm], axis=1)
            fox_g[l - n_a] = mm(u, dproj, ta=True, name="fox_in_dw")
            du = mm(dproj, P["fox_w_in"][l - n_a], tb=True, out_dtype=BF16, name="fox_in_dx")
        dh, per_layer["mix_norm"][l] = rms_bwd(h1, _row(P["mix_norm"][l]), du, dh)
        dh, per_layer["ffn1_w_gate_up"][l], per_layer["ffn1_w_down"][l], per_layer["ffn1_norm"][l] = ffn_bwd(
            dh, s1, _row(P["ffn1_norm"][l]), P["ffn1_w_gate_up"][l], P["ffn1_w_down"][l])

    (G["mem_norm"],) = rowwise_bwd(f_rmsnorm, [mems], [mem_gain], [dmem_n], tile=ROW_TILE, name="mem_norm_bwd",
                                   row_grads=[None], const_grads=[True])
    for n, v in per_layer.items():
        G[n] = jnp.stack(v)
    for n, v in gdn_g.items():
        G[n] = jnp.stack(v)
    G["fox_w_in"] = jnp.stack(fox_g)
    return part, dh, G


_GDN_O0 = 4 * GDN_WIDTH
_GDN_O1 = _GDN_O0 + 2 * GDN_HEADS
_KV_WIDTH = 2 * FOX_WIDTH + FOX_HEADS


def derived_weights(gdn_w_in, gdn_conv, kv_w):
    zeros = jnp.zeros(gdn_w_in.shape[:-1] + (LANES - 2 * GDN_HEADS,), gdn_w_in.dtype)
    return dict(
        gdn_w_in_pad=jnp.concatenate([gdn_w_in[..., :_GDN_O0], gdn_w_in[..., _GDN_O1:], gdn_w_in[..., _GDN_O0:_GDN_O1], zeros], axis=-1),
        conv8=jnp.pad(gdn_conv.astype(F32), ((0, 0), (0, 8 - CONV_WIDTH), (0, 0))),
        kv_w_pad=jnp.pad(kv_w, ((0, 0), (0, KV_PAD_WIDTH - _KV_WIDTH))))


def reference_layout(G):
    gp = G["gdn_w_in_pad"]
    out = dict(G)
    out["gdn_w_in"] = jnp.concatenate([gp[..., :_GDN_O0], gp[..., _GDN_O0 + MEM_WIDTH:_GDN_O0 + MEM_WIDTH + 2 * GDN_HEADS],
                                       gp[..., _GDN_O0:_GDN_O0 + MEM_WIDTH]], axis=-1)
    out["gdn_conv"] = G["conv8"][:, :CONV_WIDTH]
    out["kv_w"] = G["kv_w_pad"][:, :_KV_WIDTH]
    out["gdn_A_log"] = G["gdn_A_log"][:, 0, :GDN_HEADS]
    out["gdn_dt_bias"] = G["gdn_dt_bias"][:, 0, :GDN_HEADS]
    out["gdn_out_norm"] = G["gdn_out_norm"][:, 0, :]
    out["kv_b_f"] = G["kv_b_f"][0, :FOX_HEADS]
    for n in ("ffn1_norm", "mix_norm", "ffn2_norm"):
        out[n] = G[n][:, 0, :]
    for n in ("mem_norm", "kv_norm", "final_norm"):
        out[n] = G[n][0]
    return {n: out[n] for n in WEIGHT_NAMES}


def _f32_as_bf16(a):
    a = a.astype(F32)
    hi = a.astype(BF16)
    mid = (a - hi.astype(F32)).astype(BF16)
    lo = (a - hi.astype(F32) - mid.astype(F32)).astype(BF16)
    return jnp.stack([hi, mid, lo], axis=-1)


def _bf16_as_f32(a):
    a = a.astype(F32)
    return (a[..., 0] + a[..., 1]) + a[..., 2]


def gather_weights(W):
    c = lax.axis_index("c")
    pieces, shapes = [], []
    for name, _ in SHARDED:
        p = _f32_as_bf16(W[name]) if name == "gdn_conv" else W[name].astype(BF16)
        pieces.append(p)
        shapes.append(p.shape)
    packed = pack(pieces, BF16)
    half = packed.shape[0] // 2
    gathered = all_gather8(lax.dynamic_slice_in_dim(packed, c * half, half, axis=0), name="gather_weights")
    per_chip = [unpack(gathered[k * 2 * half:(k + 1) * 2 * half], shapes) for k in range(N_CHIPS)]
    full = {}
    for i, (name, axis) in enumerate(SHARDED):
        parts = [per_chip[k][i] for k in range(N_CHIPS)]
        if name == "gdn_conv":
            parts = [_bf16_as_f32(p) for p in parts]
        full[name] = jnp.concatenate(parts, axis=axis)
    return full


def reduce_gradients(G, shard_shapes):
    c = lax.axis_index("c")
    per_chip = []
    for k in range(N_CHIPS):
        pieces = []
        for name, axis in SHARDED:
            size = shard_shapes[name][axis]
            pieces.append(lax.slice_in_dim(G[name], k * size, (k + 1) * size, axis=axis))
        per_chip.append(pack(pieces, F32))
    stacked = jnp.stack(per_chip)
    half = stacked.shape[1] // 2
    mine = lax.dynamic_slice_in_dim(stacked, c * half, half, axis=1)
    theirs = lax.dynamic_slice_in_dim(stacked, (1 - c) * half, half, axis=1)
    pair = add_pairs(mine, sibling_swap(theirs, name="grad_pair_swap"), out_dtype=BF16, name="grad_pair_sum")
    total_half = sum_leading(chip_all_to_all(pair, name="grad_all_to_all"), name="grad_chip_sum")
    other_half = sibling_swap(total_half, name="grad_half_swap")
    total = jnp.zeros((2 * half, PACK_COLS), F32)
    total = lax.dynamic_update_slice_in_dim(total, total_half, c * half, axis=0)
    total = lax.dynamic_update_slice_in_dim(total, other_half, (1 - c) * half, axis=0)
    return dict(zip([n for n, _ in SHARDED], unpack(total, [shard_shapes[n] for n, _ in SHARDED])))


def allreduce_replicated(G):
    names = REPLICATED
    packed = pack([G[n] for n in names], F32, row_multiple=8)
    gathered = all_gather8(packed, name="gather_small_grads").reshape(N_DEV, packed.shape[0], PACK_COLS)
    total = sum_leading(gathered, name="sum_small_grads")
    return dict(zip(names, unpack(total, [G[n].shape for n in names])))


def kernel(x, mem, *rest):
    n_w = len(WEIGHT_NAMES)
    W = dict(zip(WEIGHT_NAMES, rest[:n_w]))
    target = rest[n_w]
    M = dict(zip(WEIGHT_NAMES, rest[n_w + 1:2 * n_w + 1]))
    V = dict(zip(WEIGHT_NAMES, rest[2 * n_w + 1:3 * n_w + 1]))

    full = gather_weights(W)
    P = {n: W[n] for n in REPLICATED}
    P.update({n: full[n] for n in ("ffn1_w_gate_up", "ffn1_w_down", "ffn2_w_gate_up", "ffn2_w_down", "fox_w_in", "w_out", "mem_w_kv")})
    P.update(derived_weights(full["gdn_w_in"], full["gdn_conv"], full["kv_w"]))

    part, dx, G = forward_backward(x[0], mem[0], target[0], P)
    G = reference_layout(G)
    loss = lax.psum(0.5 / x.shape[-1] * jnp.sum(part), ("x", "y", "c"))

    grads = reduce_gradients(G, {n: W[n].shape for n, _ in SHARDED})
    grads.update(allreduce_replicated(G))

    outs = {n: adamw(W[n], grads[n], M[n], V[n], name="adamw_" + n) for n in WEIGHT_NAMES}
    return (loss, dx[None], *[grads[n] for n in WEIGHT_NAMES], *[outs[n][0] for n in WEIGHT_NAMES],
            *[outs[n][1] for n in WEIGHT_NAMES], *[outs[n][2] for n in WEIGHT_NAMES])
```

```python
import jax
import jax.numpy as jnp
from jax import lax
from jax.experimental import pallas as pl
from jax.experimental.pallas import tpu as pltpu

F32, BF16 = jnp.float32, jnp.bfloat16
HI = lax.Precision.HIGHEST
MESH = pl.DeviceIdType.MESH

VMEM_LIMIT_BYTES = 48 * 1024 * 1024
LANES = 128
EPS = 1e-6
NEG_INF = -1e30

D_MODEL = 1024
HEAD_DIM = 128
GDN_HEADS = 6
GDN_WIDTH = GDN_HEADS * HEAD_DIM
FOX_HEADS = 6
FOX_WIDTH = FOX_HEADS * HEAD_DIM
MEM_HEADS = 4
MEM_HEAD_DIM = 64
MEM_WIDTH = MEM_HEADS * MEM_HEAD_DIM
FFN_HIDDEN = 2816
CONV_WIDTH = 4
GDN_CHUNK = 128
N_CHIPS = 4
N_DEV = 8

ADAM_LR, ADAM_B1, ADAM_B2, ADAM_EPS, ADAM_WD, ADAM_STEP = 0.001, 0.9, 0.999, 1e-08, 0.01, 10


def _pcall(body, *, name, out_shape, grid=(), in_specs=None, out_specs=None, scratch=(), sem=None):
    params = dict(vmem_limit_bytes=VMEM_LIMIT_BYTES)
    if sem is not None:
        params["dimension_semantics"] = sem
    kw = dict(grid=grid, in_specs=in_specs, out_specs=out_specs) if grid else {}
    return pl.pallas_call(body, name=name, out_shape=out_shape, scratch_shapes=list(scratch),
                          compiler_params=pltpu.CompilerParams(**params), **kw)


def _pick(n, cands):
    for c in cands:
        if n % c == 0:
            return c
    return n


def _make_dot(dtype, precision):
    def raw(a, b, dims):
        return lax.dot_general(a.astype(dtype), b.astype(dtype), (dims, ((), ())),
                               precision=precision, preferred_element_type=F32)

    @jax.custom_vjp
    def dot(a, b):
        return raw(a, b, ((1,), (0,)))

    def fwd(a, b):
        return dot(a, b), (a, b)

    def bwd(resid, ct):
        a, b = resid
        return raw(ct, b, ((1,), (1,))).astype(a.dtype), raw(a, ct, ((0,), (0,))).astype(b.dtype)

    dot.defvjp(fwd, bwd)
    dot.nn = lambda a, b: raw(a, b, ((1,), (0,)))
    dot.nt = lambda a, b: raw(a, b, ((1,), (1,)))
    dot.tn = lambda a, b: raw(a, b, ((0,), (0,)))
    return dot


bdot = _make_dot(BF16, None)
fdot = _make_dot(F32, HI)
idot = _make_dot(F32, lax.Precision.HIGH)
sdot = idot


def _sigmoid(x):
    return jax.nn.sigmoid(x)


def _silu(x):
    return x * _sigmoid(x)


def _softplus(x):
    return jnp.maximum(x, 0.0) + jnp.log(1.0 + jnp.exp(-jnp.abs(x)))


def _log_sigmoid(x):
    return -_softplus(-x)


def _iota2(shape, dim):
    return lax.broadcasted_iota(jnp.int32, shape, dim)


def mm(a, b, *, ta=False, tb=False, a_split=False, b_split=False, out_dtype=F32, scale=1.0, res=None, name):
    assert not (a_split and ta) and not (b_split and tb)
    (K, M) = a.shape if ta else ((2 * a.shape[2], a.shape[1]) if a_split else a.shape[::-1])
    (N, Kb) = b.shape if tb else ((2 * b.shape[2], b.shape[1]) if b_split else b.shape[::-1])
    assert K == Kb, (a.shape, b.shape, ta, tb)
    tm = _pick(M, (1024, 1408, 512, 256, 128)) if ta else _pick(M, (512, 256, 128))
    tn = _pick(N, (1024, 1408, 1152, 768, 512, 384, 256, 128))
    tk = _pick(K, (512, 256, 128)) if ta else _pick(K, (1024, 1408, 1152, 512, 256, 128))
    assert not a_split or (K // 2) % tk == 0
    assert not b_split or (N // 2) % tn == 0
    nk = K // tk
    dims = (((0 if ta else 1,), (1 if tb else 0,)), ((), ()))

    def body(a_ref, b_ref, *rest):
        o_ref, acc = rest[-2], rest[-1]
        k = pl.program_id(2)

        @pl.when(k == 0)
        def _():
            acc[...] = jnp.zeros_like(acc)

        acc[...] += lax.dot_general(a_ref[...].astype(BF16), b_ref[...].astype(BF16), dims,
                                    preferred_element_type=F32)

        @pl.when(k == nk - 1)
        def _():
            out = acc[...] * scale
            if res is not None:
                out = out + rest[0][...].astype(F32)
            o_ref[...] = out.astype(o_ref.dtype)

    a_spec = pl.BlockSpec((tk, tm), lambda i, j, k: (k, i)) if ta else pl.BlockSpec((tm, tk), lambda i, j, k: (i, k))
    b_spec = pl.BlockSpec((tn, tk), lambda i, j, k: (j, k)) if tb else pl.BlockSpec((tk, tn), lambda i, j, k: (k, j))
    if a_split:
        per_half = K // 2 // tk
        a_spec = pl.BlockSpec((None, tm, tk), lambda i, j, k: (k // per_half, i, k % per_half))
    if b_split:
        per_half = N // 2 // tn
        b_spec = pl.BlockSpec((None, tk, tn), lambda i, j, k: (j // per_half, k, j % per_half))
    o_spec = pl.BlockSpec((tm, tn), lambda i, j, k: (i, j))
    ins, specs = [a, b], [a_spec, b_spec]
    if res is not None:
        ins.append(res)
        specs.append(o_spec)
    return _pcall(body, name=name, out_shape=jax.ShapeDtypeStruct((M, N), out_dtype),
                  grid=(M // tm, N // tn, nk), in_specs=specs, out_specs=o_spec,
                  scratch=[pltpu.VMEM((tm, tn), F32)], sem=("parallel", "parallel", "arbitrary"))(*ins)


def _row_spec(r, tile):
    if isinstance(r, tuple):
        arr, width, col = r
        return arr, pl.BlockSpec((tile, width), lambda i, col=col: (i, col))
    return r, pl.BlockSpec((tile, r.shape[1]), lambda i: (i, 0))


def _const_spec(c):
    return pl.BlockSpec(c.shape, lambda i: (0,) * c.ndim)


def rowwise(fn, rows, consts, outs, *, tile, name):
    arrs, specs = zip(*[_row_spec(r, tile) for r in rows])
    n_rows = arrs[0].shape[0]
    tile = min(tile, n_rows)
    n_in = len(rows) + len(consts)

    def body(*refs):
        res = fn(*[r[...] for r in refs[:n_in]])
        for o_ref, o in zip(refs[n_in:], res):
            o_ref[...] = o.astype(o_ref.dtype)

    arrs, specs = zip(*[_row_spec(r, tile) for r in rows])
    return _pcall(body, name=name,
                  out_shape=[jax.ShapeDtypeStruct((n_rows, w), dt) for w, dt in outs],
                  grid=(n_rows // tile,),
                  in_specs=list(specs) + [_const_spec(c) for c in consts],
                  out_specs=[pl.BlockSpec((tile, w), lambda i: (i, 0)) for w, _ in outs],
                  sem=("parallel",))(*arrs, *consts)


def rowwise_bwd(fn, rows, consts, cts, *, tile, name, row_grads, const_grads, add=None):
    arrs, _ = zip(*[_row_spec(r, tile) for r in rows])
    n_rows = arrs[0].shape[0]
    tile = min(tile, n_rows)
    arrs, specs = zip(*[_row_spec(r, tile) for r in rows])
    ct_arrs, ct_specs = zip(*[_row_spec(c, tile) for c in cts])
    add = add or {}
    add_idx = sorted(add)
    add_arrs, add_specs = (zip(*[_row_spec(add[i], tile) for i in add_idx]) if add_idx else ((), ()))
    nr, nc, nct, na = len(rows), len(consts), len(cts), len(add_idx)
    want_rows = [i for i, d in enumerate(row_grads) if d is not None]
    want_consts = [i for i, w in enumerate(const_grads) if w]

    def body(*refs):
        row_v = [r[...] for r in refs[:nr]]
        const_v = [r[...] for r in refs[nr:nr + nc]]
        ct_v = [r[...] for r in refs[nr + nc:nr + nc + nct]]
        add_v = {i: refs[nr + nc + nct + j][...] for j, i in enumerate(add_idx)}
        out_refs = refs[nr + nc + nct + na:]
        res, vjp = jax.vjp(fn, *row_v, *const_v)
        grads = vjp(tuple(c.astype(o.dtype) for c, o in zip(ct_v, res)))
        for o_ref, i in zip(out_refs, want_rows):
            g = grads[i].astype(F32)
            if i in add_v:
                g = g + add_v[i].astype(F32)
            o_ref[...] = g.astype(o_ref.dtype)
        first = pl.program_id(0) == 0
        for o_ref, i in zip(out_refs[len(want_rows):], want_consts):
            g = grads[nr + i].astype(F32)

            @pl.when(first)
            def _(o_ref=o_ref, g=g):
                o_ref[...] = g

            @pl.when(jnp.logical_not(first))
            def _(o_ref=o_ref, g=g):
                o_ref[...] += g

    def width(r):
        return r[1] if isinstance(r, tuple) else r.shape[1]

    out_shape = [jax.ShapeDtypeStruct((n_rows, width(rows[i])), row_grads[i]) for i in want_rows]
    out_shape += [jax.ShapeDtypeStruct(consts[i].shape, F32) for i in want_consts]
    out_specs = [pl.BlockSpec((tile, width(rows[i])), lambda i_: (i_, 0)) for i in want_rows]
    out_specs += [_const_spec(consts[i]) for i in want_consts]
    return _pcall(body, name=name, out_shape=out_shape, grid=(n_rows // tile,),
                  in_specs=list(specs) + [_const_spec(c) for c in consts] + list(ct_specs) + list(add_specs),
                  out_specs=out_specs, sem=("arbitrary",))(*arrs, *consts, *ct_arrs, *add_arrs)


def f_rmsnorm(x, g):
    x = x.astype(F32)
    return (x * lax.rsqrt(jnp.mean(x * x, axis=-1, keepdims=True) + EPS) * g,)


def _head_sel(first_lane):
    r, c = _iota2((LANES, GDN_WIDTH), 0), _iota2((LANES, GDN_WIDTH), 1)
    return (r == c // HEAD_DIM + first_lane).astype(F32)


def _tri(n, strict=False):
    r, c = _iota2((n, n), 0), _iota2((n, n), 1)
    return r > c if strict else r >= c


def f_gdn_pre(xc, ab, a_log, dt_bias):
    s = _silu(xc.astype(F32))
    qs, ks = [], []
    for h in range(GDN_HEADS):
        qh = s[:, h * HEAD_DIM:(h + 1) * HEAD_DIM]
        kh = s[:, GDN_WIDTH + h * HEAD_DIM:GDN_WIDTH + (h + 1) * HEAD_DIM]
        qs.append(qh * lax.rsqrt(jnp.sum(qh * qh, axis=-1, keepdims=True) + EPS) * (HEAD_DIM ** -0.5))
        ks.append(kh * lax.rsqrt(jnp.sum(kh * kh, axis=-1, keepdims=True) + EPS))
    q, k = jnp.concatenate(qs, axis=1), jnp.concatenate(ks, axis=1)
    v = s[:, 2 * GDN_WIDTH:]
    ab = ab.astype(F32)
    g = -jnp.exp(a_log) * _softplus(ab + dt_bias)
    gc = fdot(_tri(GDN_CHUNK).astype(F32), fdot(g, _head_sel(0)))
    beta = fdot(_sigmoid(ab), _head_sel(GDN_HEADS))
    return q, k, v, gc, beta


def _unit_lower_inverse(neg_lower):
    C = neg_lower.shape[0]
    inv = (_iota2((C, C), 0) == _iota2((C, C), 1)).astype(F32) + neg_lower
    power = neg_lower
    for _ in range(6):
        power = idot.nn(power, power)
        inv = inv + idot.nn(power, inv)
    return inv


@jax.custom_vjp
def _solve_with_inverse(inv, neg_lower, rhs):
    return idot.nn(inv, rhs)


def _solve_fwd(inv, neg_lower, rhs):
    x = idot.nn(inv, rhs)
    return x, (inv, x)


def _solve_bwd(resid, ct):
    inv, x = resid
    d_rhs = idot.tn(inv, ct)
    return jnp.zeros_like(inv), idot.nt(d_rhs, x), d_rhs


_solve_with_inverse.defvjp(_solve_fwd, _solve_bwd)


def f_gdn_intra(q, k, v, gc, beta, inv=None):
    C = GDN_CHUNK
    causal, strict = _tri(C), _tri(C, strict=True)
    is_last = _iota2((C, HEAD_DIM), 0) == C - 1
    outs = [[] for _ in range(6 if inv is None else 5)]
    for h in range(GDN_HEADS):
        sl = slice(h * HEAD_DIM, (h + 1) * HEAD_DIM)
        qh, kh, vh, gh, bh = q[:, sl], k[:, sl], v[:, sl], gc[:, sl], beta[:, sl]
        gdiff = gh - gh.T
        decay = jnp.where(causal, jnp.exp(jnp.where(causal, gdiff, 0.0)), 0.0)
        kb = kh * bh
        neg_lower = jnp.where(strict, -(idot(kb, kh.T) * decay), 0.0)
        rhs = jnp.concatenate([vh * bh, kb * jnp.exp(gh)], axis=1)
        if inv is None:
            inv_h = _unit_lower_inverse(neg_lower)
            sol = idot.nn(inv_h, rhs)
        else:
            sol = _solve_with_inverse(inv[:, sl], neg_lower, rhs)
        qk = jnp.where(causal, idot(qh, kh.T) * decay, 0.0)
        g_last = jnp.sum(jnp.where(is_last, gh, 0.0), axis=0, keepdims=True)
        vals = (sol[:, :HEAD_DIM], sol[:, HEAD_DIM:], qk, kh * jnp.exp(g_last - gh), qh * jnp.exp(gh))
        for lst, val in zip(outs, vals + ((inv_h,) if inv is None else ())):
            lst.append(val)
    return tuple(jnp.concatenate(o, axis=1) for o in outs)


def f_gdn_post(o, z, gain):
    z = z.astype(F32)
    parts = []
    for h in range(GDN_HEADS):
        oh = o[:, h * HEAD_DIM:(h + 1) * HEAD_DIM]
        parts.append(oh * lax.rsqrt(jnp.mean(oh * oh, axis=-1, keepdims=True) + EPS) * gain)
    return (jnp.concatenate(parts, axis=1) * _silu(z),)


def f_mem_attn(q, k, v):
    q = q.astype(F32)
    lane_head = _iota2((1, MEM_WIDTH), 1) // MEM_HEAD_DIM
    out = jnp.zeros(q.shape, F32)
    kt = k.astype(F32).T
    for h in range(MEM_HEADS):
        mask = (lane_head == h).astype(F32)
        logits = bdot(q * mask, kt) * (MEM_HEAD_DIM ** -0.5)
        p = jnp.exp(logits - jnp.max(logits, axis=-1, keepdims=True))
        p = p / jnp.sum(p, axis=-1, keepdims=True)
        out = out + bdot(p, v) * mask
    return (out,)


def f_loss(y, t):
    d = y - t
    return (d * d,)


def conv_fwd(proj, w8, *, width, tile=512):
    n_rows = proj.shape[0]
    tile = min(tile, n_rows)

    def body(x_ref, halo_ref, w_ref, o_ref):
        i = pl.program_id(0)
        halo = jnp.where(i > 0, halo_ref[...].astype(F32), 0.0)
        xs = jnp.concatenate([halo, x_ref[...].astype(F32)], axis=0)
        acc = xs[8:] * w_ref[3:4, :]
        for j in range(CONV_WIDTH - 1):
            acc = acc + pltpu.roll(xs, CONV_WIDTH - 1 - j, 0)[8:] * w_ref[j:j + 1, :]
        o_ref[...] = acc

    return _pcall(body, name="gdn_conv_fwd", out_shape=jax.ShapeDtypeStruct((n_rows, width), F32),
                  grid=(n_rows // tile,),
                  in_specs=[pl.BlockSpec((tile, width), lambda i: (i, 0)),
                            pl.BlockSpec((8, width), lambda i: (jnp.maximum(i * (tile // 8) - 1, 0), 0)),
                            pl.BlockSpec((8, width), lambda i: (0, 0))],
                  out_specs=pl.BlockSpec((tile, width), lambda i: (i, 0)), sem=("parallel",))(proj, proj, w8)


def conv_bwd(proj, w8, dy, *, width, tile=512):
    n_rows = proj.shape[0]
    tile = min(tile, n_rows)
    n = n_rows // tile

    def body(x_ref, xhalo_ref, w_ref, dy_ref, dyhalo_ref, dx_ref, dw_ref):
        i = pl.program_id(0)
        dy = dy_ref[...]
        after = jnp.where(i < n - 1, dyhalo_ref[...], 0.0)
        ds = jnp.concatenate([dy, after], axis=0)
        dx = dy * w_ref[3:4, :]
        for j in range(CONV_WIDTH - 1):
            shift = CONV_WIDTH - 1 - j
            dx = dx + pltpu.roll(ds, tile + 8 - shift, 0)[:tile] * w_ref[j:j + 1, :]
        dx_ref[...] = dx.astype(dx_ref.dtype)
        halo = jnp.where(i > 0, xhalo_ref[...].astype(F32), 0.0)
        xs = jnp.concatenate([halo, x_ref[...].astype(F32)], axis=0)
        rows = [jnp.sum(dy * pltpu.roll(xs, CONV_WIDTH - 1 - j, 0)[8:], axis=0, keepdims=True)
                for j in range(CONV_WIDTH - 1)]
        rows.append(jnp.sum(dy * xs[8:], axis=0, keepdims=True))
        dw = jnp.concatenate(rows + [jnp.zeros((8 - CONV_WIDTH, width), F32)], axis=0)

        @pl.when(i == 0)
        def _():
            dw_ref[...] = dw

        @pl.when(i > 0)
        def _():
            dw_ref[...] += dw

    t8 = tile // 8
    return _pcall(body, name="gdn_conv_bwd",
                  out_shape=[jax.ShapeDtypeStruct((n_rows, width), BF16), jax.ShapeDtypeStruct((8, width), F32)],
                  grid=(n,),
                  in_specs=[pl.BlockSpec((tile, width), lambda i: (i, 0)),
                            pl.BlockSpec((8, width), lambda i: (jnp.maximum(i * t8 - 1, 0), 0)),
                            pl.BlockSpec((8, width), lambda i: (0, 0)),
                            pl.BlockSpec((tile, width), lambda i: (i, 0)),
                            pl.BlockSpec((8, width), lambda i: (jnp.minimum((i + 1) * t8, n * t8 - 1), 0))],
                  out_specs=[pl.BlockSpec((tile, width), lambda i: (i, 0)), pl.BlockSpec((8, width), lambda i: (0, 0))],
                  sem=("arbitrary",))(proj, proj, w8, dy, dy)


def gdn_scan_fwd(u, w, qk, kt, qh, gc):
    n_rows = u.shape[0]
    C, n = GDN_CHUNK, u.shape[0] // GDN_CHUNK

    def body(u_ref, w_ref, qk_ref, kt_ref, qh_ref, gc_ref, o_ref, vn_ref, sin_ref, st):
        @pl.when(pl.program_id(0) == 0)
        def _():
            st[...] = jnp.zeros_like(st)

        sin_ref[0] = st[...]
        for h in range(GDN_HEADS):
            sl = slice(h * HEAD_DIM, (h + 1) * HEAD_DIM)
            s = st[sl, :]
            v_new = u_ref[:, sl] - sdot(w_ref[:, sl], s)
            o_ref[:, sl] = sdot(qh_ref[:, sl], s) + sdot(qk_ref[:, sl], v_new)
            vn_ref[:, sl] = v_new
            st[sl, :] = s * jnp.exp(gc_ref[C - 1:C, sl]) + sdot.tn(kt_ref[:, sl], v_new)

    blk = pl.BlockSpec((C, GDN_WIDTH), lambda i: (i, 0))
    return _pcall(body, name="gdn_scan_fwd",
                  out_shape=[jax.ShapeDtypeStruct((n_rows, GDN_WIDTH), F32), jax.ShapeDtypeStruct((n_rows, GDN_WIDTH), F32),
                             jax.ShapeDtypeStruct((n, GDN_WIDTH, HEAD_DIM), F32)],
                  grid=(n,), in_specs=[blk] * 6,
                  out_specs=[blk, blk, pl.BlockSpec((1, GDN_WIDTH, HEAD_DIM), lambda i: (i, 0, 0))],
                  scratch=[pltpu.VMEM((GDN_WIDTH, HEAD_DIM), F32)], sem=("arbitrary",))(u, w, qk, kt, qh, gc)


def gdn_scan_bwd(do, w, qk, kt, qh, gc, vn, sin):
    n_rows = do.shape[0]
    C, n = GDN_CHUNK, do.shape[0] // GDN_CHUNK

    def body(do_ref, w_ref, qk_ref, kt_ref, qh_ref, gc_ref, vn_ref, sin_ref,
             du_ref, dw_ref, dqk_ref, dkt_ref, dqh_ref, dgl_ref, dst):
        @pl.when(pl.program_id(0) == 0)
        def _():
            dst[...] = jnp.zeros_like(dst)

        for h in range(GDN_HEADS):
            sl = slice(h * HEAD_DIM, (h + 1) * HEAD_DIM)
            s, ds_out, d_o, v_new = sin_ref[0, sl, :], dst[sl, :], do_ref[:, sl], vn_ref[:, sl]
            e = jnp.exp(gc_ref[C - 1:C, sl])
            dvn = sdot.tn(qk_ref[:, sl], d_o) + sdot(kt_ref[:, sl], ds_out)
            du_ref[:, sl] = dvn
            dw_ref[:, sl] = -sdot.nt(dvn, s)
            dqk_ref[:, sl] = sdot.nt(d_o, v_new)
            dkt_ref[:, sl] = sdot.nt(v_new, ds_out)
            dqh_ref[:, sl] = sdot.nt(d_o, s)
            dgl = jnp.sum(ds_out * s, axis=0, keepdims=True) * e
            dgl_ref[:, sl] = jnp.broadcast_to(dgl, (8, HEAD_DIM))
            dst[sl, :] = sdot.tn(qh_ref[:, sl], d_o) + e * ds_out - sdot.tn(w_ref[:, sl], dvn)

    blk = pl.BlockSpec((C, GDN_WIDTH), lambda i: (n - 1 - i, 0))
    row = jax.ShapeDtypeStruct((n_rows, GDN_WIDTH), F32)
    return _pcall(body, name="gdn_scan_bwd",
                  out_shape=[row] * 5 + [jax.ShapeDtypeStruct((n * 8, GDN_WIDTH), F32)],
                  grid=(n,), in_specs=[blk] * 7 + [pl.BlockSpec((1, GDN_WIDTH, HEAD_DIM), lambda i: (n - 1 - i, 0, 0))],
                  out_specs=[blk] * 5 + [pl.BlockSpec((8, GDN_WIDTH), lambda i: (n - 1 - i, 0))],
                  scratch=[pltpu.VMEM((GDN_WIDTH, HEAD_DIM), F32)], sem=("arbitrary",))(do, w, qk, kt, qh, gc, vn, sin)


def gdn_intra_bwd(q, k, v, gc, beta, inv, cts, dgl):
    n_rows = q.shape[0]
    C = GDN_CHUNK

    def body(*refs):
        ins = [r[...] for r in refs[:5]]
        inv_v = refs[5][...]
        ct = tuple(r[...] for r in refs[6:11])
        dgl_v = refs[11][...]
        _, vjp = jax.vjp(lambda *a: f_gdn_intra(*a, inv=inv_v), *ins)
        grads = list(vjp(ct))
        last = _iota2((C, GDN_WIDTH), 0) == C - 1
        grads[3] = grads[3] + jnp.where(last, jnp.broadcast_to(dgl_v[0:1, :], (C, GDN_WIDTH)), 0.0)
        for o_ref, g in zip(refs[12:], grads):
            o_ref[...] = g

    blk = pl.BlockSpec((C, GDN_WIDTH), lambda i: (i, 0))
    return _pcall(body, name="gdn_intra_bwd", out_shape=[jax.ShapeDtypeStruct((n_rows, GDN_WIDTH), F32)] * 5,
                  grid=(n_rows // C,), in_specs=[blk] * 11 + [pl.BlockSpec((8, GDN_WIDTH), lambda i: (i, 0))],
                  out_specs=[blk] * 5, sem=("parallel",))(q, k, v, gc, beta, inv, *cts, dgl)


def fox_gate_fwd(f, b_f):
    n_rows = f.shape[0]
    T = LANES

    def body(f_ref, b_ref, cb_ref, crow_ref, carry):
        @pl.when(pl.program_id(0) == 0)
        def _():
            carry[...] = jnp.zeros_like(carry)

        c = fdot(_tri(T).astype(F32), _log_sigmoid(f_ref[...] + b_ref[...])) + carry[...]
        carry[...] = c[T - 1:T, :]
        cb_ref[...] = fdot(c, _head_sel(0))
        ct = c.T
        for h in range(FOX_HEADS):
            crow_ref[h] = ct[h:h + 1, :]

    return _pcall(body, name="fox_gate_fwd",
                  out_shape=[jax.ShapeDtypeStruct((n_rows, FOX_WIDTH), F32), jax.ShapeDtypeStruct((FOX_HEADS, 1, n_rows), F32)],
                  grid=(n_rows // T,),
                  in_specs=[pl.BlockSpec((T, LANES), lambda i: (i, 0)), pl.BlockSpec((1, LANES), lambda i: (0, 0))],
                  out_specs=[pl.BlockSpec((T, FOX_WIDTH), lambda i: (i, 0)), pl.BlockSpec((FOX_HEADS, 1, T), lambda i: (0, 0, i))],
                  scratch=[pltpu.VMEM((1, LANES), F32)], sem=("arbitrary",))(f, b_f)


def fox_gate_bwd(f, b_f, dcrow, dcb):
    n_rows = f.shape[0]
    T = LANES
    n = n_rows // T

    def body(f_ref, b_ref, dc_ref, dcb_ref, df_ref, db_ref, carry):
        i = pl.program_id(0)

        @pl.when(i == 0)
        def _():
            carry[...] = jnp.zeros_like(carry)

        rows = [dc_ref[h] for h in range(FOX_HEADS)] + [jnp.zeros((T - FOX_HEADS, T), F32)]
        first_lane = (_iota2((FOX_WIDTH, LANES), 0) == _iota2((FOX_WIDTH, LANES), 1) * HEAD_DIM).astype(F32)
        dc = jnp.concatenate(rows, axis=0).T + fdot(dcb_ref[...], first_lane)
        dlog = fdot.tn(_tri(T).astype(F32), dc) + carry[...]
        carry[...] = dlog[0:1, :]
        df = dlog * (1.0 - _sigmoid(f_ref[...] + b_ref[...]))
        df_ref[...] = df
        db = jnp.sum(df, axis=0, keepdims=True)

        @pl.when(i == 0)
        def _():
            db_ref[...] = db

        @pl.when(i > 0)
        def _():
            db_ref[...] += db

    return _pcall(body, name="fox_gate_bwd",
                  out_shape=[jax.ShapeDtypeStruct((n_rows, LANES), F32), jax.ShapeDtypeStruct((1, LANES), F32)],
                  grid=(n,),
                  in_specs=[pl.BlockSpec((T, LANES), lambda i: (n - 1 - i, 0)), pl.BlockSpec((1, LANES), lambda i: (0, 0)),
                            pl.BlockSpec((FOX_HEADS, 1, T), lambda i: (0, 0, n - 1 - i)),
                            pl.BlockSpec((T, FOX_WIDTH), lambda i: (n - 1 - i, 0))],
                  out_specs=[pl.BlockSpec((T, LANES), lambda i: (n - 1 - i, 0)), pl.BlockSpec((1, LANES), lambda i: (0, 0))],
                  scratch=[pltpu.VMEM((1, LANES), F32)], sem=("arbitrary",))(f, b_f, dcrow, dcb)


def _fox_tile(n_rows):
    return min(512, n_rows)


def _fox_pairs(n, query_major):
    pairs = [(i, j) for i in range(n) for j in range(i + 1)] if query_major else [(i, j) for j in range(n) for i in range(j, n)]
    return jnp.asarray([p[0] for p in pairs], jnp.int32), jnp.asarray([p[1] for p in pairs], jnp.int32)


def _pcall_tables(body, *, name, out_shape, grid, tables, in_specs, out_specs, scratch, sem):
    spec = pltpu.PrefetchScalarGridSpec(num_scalar_prefetch=len(tables), grid=grid, in_specs=in_specs, out_specs=out_specs,
                                        scratch_shapes=list(scratch))
    return pl.pallas_call(body, name=name, out_shape=out_shape, grid_spec=spec,
                          compiler_params=pltpu.CompilerParams(vmem_limit_bytes=VMEM_LIMIT_BYTES, dimension_semantics=sem))


def _fox_logits(q, k, ccol, crow, masked, t):
    s = bdot.nt(q, k) * (HEAD_DIM ** -0.5) + jnp.tile(ccol, (1, t // LANES)) - crow
    if masked:
        s = jnp.where(_iota2((t, t), 0) >= _iota2((t, t), 1), s, NEG_INF)
    return s


def _fox_p_ds(masked, t, q_ref, k_ref, v_ref, cb_ref, crow_ref, o_ref, lse_ref, do_ref):
    s = _fox_logits(q_ref[...], k_ref[...], cb_ref[...], crow_ref[0], masked, t)
    p = jnp.exp(s - jnp.tile(lse_ref[...], (1, t // LANES)))
    d_o = do_ref[...].astype(F32)
    delta = jnp.sum(d_o * o_ref[...].astype(F32), axis=-1, keepdims=True)
    return p, p * (bdot.nt(d_o, v_ref[...]) - delta), d_o


def _fox_specs(t, do_col):
    qspec = pl.BlockSpec((t, HEAD_DIM), lambda h, p, it, jt: (it[p], h))
    dospec = pl.BlockSpec((t, HEAD_DIM), lambda h, p, it, jt: (it[p], do_col + h))
    kspec = pl.BlockSpec((t, HEAD_DIM), lambda h, p, it, jt: (jt[p], h))
    vspec = pl.BlockSpec((t, HEAD_DIM), lambda h, p, it, jt: (jt[p], FOX_HEADS + h))
    cspec = pl.BlockSpec((1, 1, t), lambda h, p, it, jt: (h, 0, jt[p]))
    return qspec, dospec, kspec, vspec, cspec


def fox_fwd(q, kv, cb, crow):
    n_rows = kv.shape[0]
    t = _fox_tile(n_rows)
    n = n_rows // t
    tables = _fox_pairs(n, True)

    def body(it, jt, q_ref, k_ref, v_ref, cb_ref, crow_ref, o_ref, lse_ref, m_sc, l_sc, acc):
        i, j = it[pl.program_id(1)], jt[pl.program_id(1)]

        @pl.when(j == 0)
        def _():
            m_sc[...] = jnp.full(m_sc.shape, NEG_INF, F32)
            l_sc[...] = jnp.zeros_like(l_sc)
            acc[...] = jnp.zeros_like(acc)

        def step(masked):
            s = _fox_logits(q_ref[...], k_ref[...], cb_ref[...], crow_ref[0], masked, t)
            m_old = m_sc[...]
            m_new = jnp.maximum(m_old, jnp.max(s, axis=-1, keepdims=True))
            alpha = jnp.exp(m_old - m_new)
            p = jnp.exp(s - jnp.tile(m_new, (1, t // LANES)))
            l_sc[...] = l_sc[...] * alpha + jnp.sum(p, axis=-1, keepdims=True)
            acc[...] = acc[...] * alpha + bdot(p, v_ref[...])
            m_sc[...] = m_new

        @pl.when(j < i)
        def _():
            step(False)

        @pl.when(j == i)
        def _():
            step(True)
            o_ref[...] = (acc[...] / l_sc[...]).astype(o_ref.dtype)
            lse_ref[...] = m_sc[...] + jnp.log(l_sc[...])

    qspec, _, kspec, vspec, cspec = _fox_specs(t, 0)
    return _pcall_tables(body, name="fox_fwd",
                         out_shape=[jax.ShapeDtypeStruct((n_rows, FOX_WIDTH), BF16), jax.ShapeDtypeStruct((n_rows, FOX_WIDTH), F32)],
                         grid=(FOX_HEADS, n * (n + 1) // 2), tables=tables,
                         in_specs=[qspec, kspec, vspec, qspec, cspec], out_specs=[qspec, qspec],
                         scratch=[pltpu.VMEM((t, HEAD_DIM), F32)] * 3, sem=("parallel", "arbitrary"))(*tables, q, kv, kv, cb, crow)


def fox_bwd_dq(q, kv, cb, crow, o, lse, do, prev=None):
    do, _, do_col = do
    do_col *= FOX_HEADS
    n_rows = kv.shape[0]
    t = _fox_tile(n_rows)
    n = n_rows // t
    n_prev = 0 if prev is None else 1
    tables = _fox_pairs(n, True)

    def body(it, jt, q_ref, k_ref, v_ref, cb_ref, crow_ref, o_ref, lse_ref, do_ref, *rest):
        dq_ref, drow_ref, acc, rows = rest[n_prev:]
        i, j = it[pl.program_id(1)], jt[pl.program_id(1)]

        @pl.when(j == 0)
        def _():
            acc[...] = jnp.zeros_like(acc)
            rows[...] = jnp.zeros_like(rows)

        def step(masked):
            _, ds, _ = _fox_p_ds(masked, t, q_ref, k_ref, v_ref, cb_ref, crow_ref, o_ref, lse_ref, do_ref)
            acc[...] += bdot(ds, k_ref[...])
            rows[...] += jnp.sum(ds, axis=-1, keepdims=True)

        @pl.when(j < i)
        def _():
            step(False)

        @pl.when(j == i)
        def _():
            step(True)
            dq_ref[...] = (acc[...] * (HEAD_DIM ** -0.5)).astype(dq_ref.dtype)
            drow_ref[...] = rows[...] + rest[0][...] if n_prev else rows[...]

    qspec, dospec, kspec, vspec, cspec = _fox_specs(t, do_col)
    return _pcall_tables(body, name="fox_bwd_dq" + ("_acc" if n_prev else ""),
                         out_shape=[jax.ShapeDtypeStruct((n_rows, FOX_WIDTH), BF16), jax.ShapeDtypeStruct((n_rows, FOX_WIDTH), F32)],
                         grid=(FOX_HEADS, n * (n + 1) // 2), tables=tables,
                         in_specs=[qspec, kspec, vspec, qspec, cspec, qspec, qspec, dospec] + [qspec] * n_prev,
                         out_specs=[qspec, qspec], scratch=[pltpu.VMEM((t, HEAD_DIM), F32)] * 2,
                         sem=("parallel", "arbitrary"))(*tables, q, kv, kv, cb, crow, o, lse, do, *([prev] if n_prev else []))


def fox_bwd_dkv(q, kv, cb, crow, o, lse, do, prev=None):
    do, _, do_col = do
    do_col *= FOX_HEADS
    n_rows = kv.shape[0]
    t = _fox_tile(n_rows)
    n = n_rows // t
    n_prev = 0 if prev is None else 3
    tables = _fox_pairs(n, False)

    def body(it, jt, q_ref, k_ref, v_ref, cb_ref, crow_ref, o_ref, lse_ref, do_ref, *rest):
        prev_refs = rest[:n_prev]
        dk_ref, dv_ref, dc_ref, dk_acc, dv_acc, dc_acc = rest[n_prev:]
        i, j = it[pl.program_id(1)], jt[pl.program_id(1)]

        def step(masked):
            p, ds, d_o = _fox_p_ds(masked, t, q_ref, k_ref, v_ref, cb_ref, crow_ref, o_ref, lse_ref, do_ref)
            dv_acc[...] += bdot.tn(p, d_o)
            dk_acc[...] += bdot.tn(ds, q_ref[...])
            dc_acc[...] -= jnp.sum(ds, axis=0, keepdims=True)

        @pl.when(i == j)
        def _():
            dk_acc[...] = jnp.zeros_like(dk_acc)
            dv_acc[...] = jnp.zeros_like(dv_acc)
            dc_acc[...] = jnp.zeros_like(dc_acc)
            step(True)

        @pl.when(i > j)
        def _():
            step(False)

        @pl.when(i == n - 1)
        def _():
            dk, dv, dc = dk_acc[...] * (HEAD_DIM ** -0.5), dv_acc[...], dc_acc[...]
            if n_prev:
                dk, dv, dc = dk + prev_refs[0][...], dv + prev_refs[1][...], dc + prev_refs[2][0]
            dk_ref[...] = dk
            dv_ref[...] = dv
            dc_ref[0] = dc

    qspec, dospec, kspec, vspec, cspec = _fox_specs(t, do_col)
    return _pcall_tables(body, name="fox_bwd_dkv" + ("_acc" if n_prev else ""),
                         out_shape=[jax.ShapeDtypeStruct((n_rows, FOX_WIDTH), F32), jax.ShapeDtypeStruct((n_rows, FOX_WIDTH), F32),
                                    jax.ShapeDtypeStruct((FOX_HEADS, 1, n_rows), F32)],
                         grid=(FOX_HEADS, n * (n + 1) // 2), tables=tables,
                         in_specs=[qspec, kspec, vspec, qspec, cspec, qspec, qspec, dospec] + [kspec, kspec, cspec][:n_prev],
                         out_specs=[kspec, kspec, cspec],
                         scratch=[pltpu.VMEM((t, HEAD_DIM), F32), pltpu.VMEM((t, HEAD_DIM), F32), pltpu.VMEM((1, t), F32)],
                         sem=("parallel", "arbitrary"))(*tables, q, kv, kv, cb, crow, o, lse, do, *(prev or ()))


def loss_head(h, gain, target, *, tile=512):
    n_rows, d = h.shape
    tile = min(tile, n_rows)

    def body(h_ref, g_ref, t_ref, part_ref, dy_ref):
        (y,) = f_rmsnorm(h_ref[...], g_ref[...])
        diff = y - t_ref[...]
        dy_ref[...] = diff * (1.0 / d)
        part = jnp.sum(diff * diff, axis=0, keepdims=True)
        first = pl.program_id(0) == 0

        @pl.when(first)
        def _():
            part_ref[...] = part

        @pl.when(jnp.logical_not(first))
        def _():
            part_ref[...] += part

    blk = pl.BlockSpec((tile, d), lambda i: (i, 0))
    one = pl.BlockSpec((1, d), lambda i: (0, 0))
    return _pcall(body, name="loss_head",
                  out_shape=[jax.ShapeDtypeStruct((1, d), F32), jax.ShapeDtypeStruct((n_rows, d), F32)],
                  grid=(n_rows // tile,), in_specs=[blk, one, blk], out_specs=[one, blk], sem=("arbitrary",))(h, gain, target)


def adamw(w, g, m, v, *, name):
    shape = w.shape
    cols = shape[-1] if w.ndim >= 2 else w.size
    rows = w.size // cols
    tile = _pick(rows, (256, 128, 64, 32, 16, 8))
    as2d = lambda a: a.reshape(rows, cols)

    def body(w_ref, g_ref, m_ref, v_ref, d_ref, nm_ref, nv_ref):
        g_ = g_ref[...]
        m_ = ADAM_B1 * m_ref[...] + (1.0 - ADAM_B1) * g_
        v_ = ADAM_B2 * v_ref[...] + (1.0 - ADAM_B2) * (g_ * g_)
        m_hat = m_ / (1.0 - ADAM_B1 ** ADAM_STEP)
        v_hat = v_ / (1.0 - ADAM_B2 ** ADAM_STEP)
        d_ref[...] = -ADAM_LR * (m_hat / (jnp.sqrt(v_hat) + ADAM_EPS) + ADAM_WD * w_ref[...])
        nm_ref[...] = m_
        nv_ref[...] = v_

    blk = pl.BlockSpec((tile, cols), lambda i: (i, 0))
    outs = _pcall(body, name=name, out_shape=[jax.ShapeDtypeStruct((rows, cols), F32)] * 3, grid=(rows // tile,),
                  in_specs=[blk] * 4, out_specs=[blk] * 3, sem=("parallel",))(as2d(w), as2d(g), as2d(m), as2d(v))
    return tuple(o.reshape(shape) for o in outs)


def add_pairs(a, b, *, out_dtype, name):
    p, r, c = a.shape
    tile = _pick(r, (512, 256, 128, 64, 32, 16, 8))

    def body(a_ref, b_ref, o_ref):
        o_ref[...] = (a_ref[...].astype(F32) + b_ref[...].astype(F32)).astype(o_ref.dtype)

    blk = pl.BlockSpec((1, tile, c), lambda k, i: (k, i, 0))
    return _pcall(body, name=name, out_shape=jax.ShapeDtypeStruct((p, r, c), out_dtype), grid=(p, r // tile),
                  in_specs=[blk, blk], out_specs=blk, sem=("parallel", "parallel"))(a, b)


def sum_leading(a, *, name):
    p, r, c = a.shape
    tile = _pick(r, (256, 128, 64, 32, 16, 8))

    def body(a_ref, o_ref):
        total = a_ref[0].astype(F32)
        for k in range(1, p):
            total = total + a_ref[k].astype(F32)
        o_ref[...] = total

    return _pcall(body, name=name, out_shape=jax.ShapeDtypeStruct((r, c), F32), grid=(r // tile,),
                  in_specs=[pl.BlockSpec((p, tile, c), lambda i: (0, i, 0))],
                  out_specs=pl.BlockSpec((tile, c), lambda i: (i, 0)), sem=("parallel",))(a)


_HBM = pl.BlockSpec(memory_space=pltpu.HBM)


def _comm_call(body, *, name, out_shape, n_in, scratch):
    return pl.pallas_call(body, name=name, out_shape=out_shape, in_specs=[_HBM] * n_in, out_specs=_HBM,
                          scratch_shapes=scratch,
                          compiler_params=pltpu.CompilerParams(has_side_effects=True))


def all_gather8(a, *, name):
    m_per, n = a.shape

    def body(x_ref, out_ref, send_sems, recv_sems, local_sem):
        x, y, c = lax.axis_index("x"), lax.axis_index("y"), lax.axis_index("c")
        me, sibling = (x, y, c), (x, y, 1 - c)
        chips = [(1 - x, y), (x, 1 - y), (1 - x, 1 - y)]

        def rows(px, py, pc):
            return out_ref.at[pl.ds((4 * px + 2 * py + pc) * m_per, m_per), :]

        def copy(k, block, to, src=None):
            return pltpu.make_async_remote_copy(
                src_ref=rows(*block) if src is None else src, dst_ref=rows(*block),
                send_sem=send_sems.at[k], recv_sem=recv_sems.at[k], device_id=to, device_id_type=MESH)

        mine = pltpu.make_async_copy(x_ref, rows(*me), local_sem)
        mine.start()
        first = [copy(0, me, sibling, src=x_ref)]
        first += [copy(1 + j, me, (*chip, c), src=x_ref) for j, chip in enumerate(chips)]
        for cp in first:
            cp.start()
        passed = [copy(4 + j, (*chip, c), sibling) for j, chip in enumerate(chips)]
        for j, chip in enumerate(chips):
            copy(1 + j, (*chip, c), me).wait_recv()
            passed[j].start()
        copy(0, sibling, me).wait_recv()
        for j, chip in enumerate(chips):
            copy(4 + j, (*chip, 1 - c), me).wait_recv()
        for cp in first + passed:
            cp.wait_send()
        mine.wait()

    return _comm_call(body, name=name, out_shape=jax.ShapeDtypeStruct((N_DEV * m_per, n), a.dtype), n_in=1,
                      scratch=[pltpu.SemaphoreType.DMA((7,)), pltpu.SemaphoreType.DMA((7,)), pltpu.SemaphoreType.DMA])(a)


def sibling_swap(a, *, name):
    def body(a_ref, out_ref, send_sem, recv_sem):
        x, y, c = lax.axis_index("x"), lax.axis_index("y"), lax.axis_index("c")
        cp = pltpu.make_async_remote_copy(src_ref=a_ref, dst_ref=out_ref, send_sem=send_sem, recv_sem=recv_sem,
                                          device_id=(x, y, 1 - c), device_id_type=MESH)
        cp.start()
        cp.wait()

    return _comm_call(body, name=name, out_shape=jax.ShapeDtypeStruct(a.shape, a.dtype), n_in=1,
                      scratch=[pltpu.SemaphoreType.DMA, pltpu.SemaphoreType.DMA])(a)


def chip_all_to_all(a, *, name):
    def body(a_ref, out_ref, send_sems, recv_sems, local_sem):
        x, y, c = lax.axis_index("x"), lax.axis_index("y"), lax.axis_index("c")
        me = 2 * x + y
        mine = pltpu.make_async_copy(a_ref.at[me], out_ref.at[me], local_sem)
        mine.start()
        copies = []
        for j, (px, py) in enumerate([(1 - x, y), (x, 1 - y), (1 - x, 1 - y)]):
            copies.append(pltpu.make_async_remote_copy(
                src_ref=a_ref.at[2 * px + py], dst_ref=out_ref.at[me], send_sem=send_sems.at[j],
                recv_sem=recv_sems.at[j], device_id=(px, py, c), device_id_type=MESH))
        for cp in copies:
            cp.start()
        for cp in copies:
            cp.wait()
        mine.wait()

    return _comm_call(body, name=name, out_shape=jax.ShapeDtypeStruct(a.shape, a.dtype), n_in=1,
                      scratch=[pltpu.SemaphoreType.DMA((3,)), pltpu.SemaphoreType.DMA((3,)), pltpu.SemaphoreType.DMA])(a)


PACK_COLS = 1024
PACK_ROW_MULTIPLE = 32

WEIGHT_NAMES = ["ffn1_norm", "ffn1_w_gate_up", "ffn1_w_down", "mix_norm", "ffn2_norm", "ffn2_w_gate_up", "ffn2_w_down",
                "gdn_w_in", "gdn_conv", "gdn_A_log", "gdn_dt_bias", "gdn_out_norm", "fox_w_in", "w_out", "mem_norm",
                "mem_w_kv", "kv_norm", "kv_w", "kv_b_f", "final_norm"]
SHARDED = [("ffn1_w_gate_up", 2), ("ffn1_w_down", 1), ("ffn2_w_gate_up", 2), ("ffn2_w_down", 1), ("gdn_w_in", 2),
           ("gdn_conv", 2), ("fox_w_in", 1), ("w_out", 1), ("mem_w_kv", 1), ("kv_w", 0)]
REPLICATED = [n for n in WEIGHT_NAMES if n not in dict(SHARDED)]


PACK_PIECE_ROWS = 16


def _rows_of(size):
    return -(-size // (PACK_COLS * PACK_PIECE_ROWS)) * PACK_PIECE_ROWS


def pack(pieces, dtype, row_multiple=PACK_ROW_MULTIPLE):
    bufs, total = [], 0
    for p in pieces:
        flat = p.astype(dtype).reshape(-1)
        rows = _rows_of(flat.size)
        bufs.append(jnp.pad(flat, (0, rows * PACK_COLS - flat.size)).reshape(rows, PACK_COLS))
        total += rows
    pad = -total % row_multiple
    if pad:
        bufs.append(jnp.zeros((pad, PACK_COLS), dtype))
    return jnp.concatenate(bufs, axis=0)


def unpack(buf, shapes):
    out, row = [], 0
    for shape in shapes:
        size = 1
        for s in shape:
            size *= s
        rows = _rows_of(size)
        out.append(buf[row:row + rows].reshape(-1)[:size].reshape(shape))
        row += rows
    return out


def _row(vec, width=None):
    vec = vec.astype(F32).reshape(1, -1)
    if width is not None and vec.shape[1] < width:
        vec = jnp.pad(vec, ((0, 0), (0, width - vec.shape[1])))
    return vec


ROW_TILE = 512
GDN_PROJ_WIDTH = 4 * GDN_WIDTH + MEM_WIDTH + LANES
GDN_Z_COL, GDN_QMEM_COL, GDN_AB_COL = 3, 4 * GDN_WIDTH // MEM_WIDTH, (4 * GDN_WIDTH + MEM_WIDTH) // LANES
FOX_QMEM_COL = FOX_WIDTH // MEM_WIDTH
KV_PAD_WIDTH = 2 * FOX_WIDTH + LANES


def rms_fwd(x, gain_row, out_dtype=BF16):
    return rowwise(f_rmsnorm, [x], [gain_row], [(x.shape[1], out_dtype)], tile=ROW_TILE, name="rms_fwd")[0]


def rms_bwd(x, gain_row, dy, dres=None):
    return rowwise_bwd(f_rmsnorm, [x], [gain_row], [dy], tile=ROW_TILE, name="rms_bwd", row_grads=[F32],
                       const_grads=[True], add=None if dres is None else {0: dres})


def _ffn_tiles(n_rows):
    return _pick(n_rows, (512, 256, 128)), _pick(FFN_HIDDEN, (1408, 256, 128))


def ffn_up_act(n, wgu):
    n_rows, d = n.shape
    tm, tn = _ffn_tiles(n_rows)
    nj = FFN_HIDDEN // tn

    def body(n_ref, wg_ref, wu_ref, gu_ref, act_ref):
        x = n_ref[...].astype(BF16)
        g = bdot.nn(x, wg_ref[...])
        u = bdot.nn(x, wu_ref[...])
        gu_ref[0] = g.astype(gu_ref.dtype)
        gu_ref[1] = u.astype(gu_ref.dtype)
        act_ref[...] = (_silu(g) * u).astype(act_ref.dtype)

    return _pcall(body, name="ffn_up_act",
                  out_shape=[jax.ShapeDtypeStruct((2, n_rows, FFN_HIDDEN), BF16), jax.ShapeDtypeStruct((n_rows, FFN_HIDDEN), BF16)],
                  grid=(nj, n_rows // tm),
                  in_specs=[pl.BlockSpec((tm, d), lambda j, i: (i, 0)), pl.BlockSpec((d, tn), lambda j, i: (0, j)),
                            pl.BlockSpec((d, tn), lambda j, i: (0, nj + j))],
                  out_specs=[pl.BlockSpec((2, tm, tn), lambda j, i: (0, i, j)), pl.BlockSpec((tm, tn), lambda j, i: (i, j))],
                  sem=("parallel", "parallel"))(n, wgu, wgu)


def ffn_down_dx_act(dh, wd, gu):
    n_rows, d = dh.shape
    tm, tn = _ffn_tiles(n_rows)

    def body(dh_ref, wd_ref, gu_ref, dgu_ref):
        dact = 0.5 * bdot.nt(dh_ref[...], wd_ref[...])
        gate, up = gu_ref[0].astype(F32), gu_ref[1].astype(F32)
        sg = _sigmoid(gate)
        dgu_ref[0] = (dact * up * (sg * (1.0 + gate * (1.0 - sg)))).astype(dgu_ref.dtype)
        dgu_ref[1] = (dact * (gate * sg)).astype(dgu_ref.dtype)

    blk = pl.BlockSpec((2, tm, tn), lambda j, i: (0, i, j))
    return _pcall(body, name="ffn_down_dx_act", out_shape=jax.ShapeDtypeStruct((2, n_rows, FFN_HIDDEN), BF16),
                  grid=(FFN_HIDDEN // tn, n_rows // tm),
                  in_specs=[pl.BlockSpec((tm, d), lambda j, i: (i, 0)), pl.BlockSpec((tn, d), lambda j, i: (j, 0)), blk],
                  out_specs=blk, sem=("parallel", "parallel"))(dh, wd, gu)


def ffn_fwd(h, gain_row, wgu, wd):
    n = rms_fwd(h, gain_row)
    gu, act = ffn_up_act(n, wgu)
    return mm(act, wd, scale=0.5, res=h, name="ffn_down"), (h, n, gu, act)


def ffn_bwd(dh, saved, gain_row, wgu, wd):
    h, n, gu, act = saved
    dgu = ffn_down_dx_act(dh, wd, gu)
    dwd = mm(act, dh, ta=True, scale=0.5, name="ffn_down_dw")
    dwgu = mm(n, dgu, ta=True, b_split=True, name="ffn_up_dw")
    dn = mm(dgu, wgu, tb=True, a_split=True, out_dtype=BF16, name="ffn_up_dx")
    dh, dgain = rms_bwd(h, gain_row, dn, dh)
    return dh, dwgu, dwd, dgain


def gdn_fwd(proj, w8, a_row, dt_row, onorm_row):
    wide = [(GDN_WIDTH, F32)] * 5
    xc = conv_fwd(proj, w8, width=3 * GDN_WIDTH)
    q, k, v, gc, beta = rowwise(f_gdn_pre, [xc, (proj, LANES, GDN_AB_COL)], [a_row, dt_row], wide, tile=GDN_CHUNK,
                                name="gdn_pre_fwd")
    u, w, qk, kt, qh, inv = rowwise(f_gdn_intra, [q, k, v, gc, beta], [], wide + wide[:1], tile=GDN_CHUNK,
                                    name="gdn_intra_fwd")
    o, vn, sin = gdn_scan_fwd(u, w, qk, kt, qh, gc)
    main = rowwise(f_gdn_post, [o, (proj, GDN_WIDTH, GDN_Z_COL)], [onorm_row], [(GDN_WIDTH, BF16)], tile=ROW_TILE,
                   name="gdn_post_fwd")[0]
    return main, (xc, q, k, v, gc, beta, inv, w, qk, kt, qh, vn, sin, o)


def gdn_bwd(dmain, proj, saved, w8, a_row, dt_row, onorm_row):
    xc, q, k, v, gc, beta, inv, w, qk, kt, qh, vn, sin, o = saved
    do, dz, donorm = rowwise_bwd(f_gdn_post, [o, (proj, GDN_WIDTH, GDN_Z_COL)], [onorm_row], [dmain], tile=ROW_TILE,
                                 name="gdn_post_bwd", row_grads=[F32, BF16], const_grads=[True])
    du, dw, dqk, dkt, dqh, dgl = gdn_scan_bwd(do, w, qk, kt, qh, gc, vn, sin)
    dq, dk, dv, dgc, dbeta = gdn_intra_bwd(q, k, v, gc, beta, inv, (du, dw, dqk, dkt, dqh), dgl)
    dxc, dab, da, ddt = rowwise_bwd(f_gdn_pre, [xc, (proj, LANES, GDN_AB_COL)], [a_row, dt_row], [dq, dk, dv, dgc, dbeta],
                                    tile=GDN_CHUNK, name="gdn_pre_bwd", row_grads=[F32, BF16], const_grads=[True, True])
    dqkv, dw8 = conv_bwd(proj, w8, dxc, width=3 * GDN_WIDTH)
    return dqkv, dz, dab, dw8, da, ddt, donorm


def mem_fwd(q, kmem, vmem):
    return rowwise(f_mem_attn, [q], [kmem, vmem], [(MEM_WIDTH, BF16)], tile=ROW_TILE, name="mem_attn_fwd")[0]


def mem_bwd(q, kmem, vmem, dout):
    return rowwise_bwd(f_mem_attn, [q], [kmem, vmem], [dout], tile=ROW_TILE, name="mem_attn_bwd", row_grads=[BF16],
                       const_grads=[True, True])


def forward_backward(xs, mems, target, P):
    depth, n_a = 4, 2
    G = {}
    mem_gain = _row(P["mem_norm"])
    mem_n = rms_fwd(mems, mem_gain)
    h = xs
    saved = []
    shared = None
    for l in range(depth):
        h0 = h
        h1, s1 = ffn_fwd(h0, _row(P["ffn1_norm"][l]), P["ffn1_w_gate_up"][l], P["ffn1_w_down"][l])
        u = rms_fwd(h1, _row(P["mix_norm"][l]))
        kvm = mm(mem_n, P["mem_w_kv"][l], name="mem_kv")
        kmem, vmem = kvm[:, :MEM_WIDTH], kvm[:, MEM_WIDTH:]
        if l < n_a:
            gp = (P["conv8"][l], _row(P["gdn_A_log"][l], LANES), _row(P["gdn_dt_bias"][l], LANES), _row(P["gdn_out_norm"][l]))
            proj = mm(u, P["gdn_w_in_pad"][l], name="gdn_in")
            main, sm = gdn_fwd(proj, *gp)
            qm = (proj, MEM_WIDTH, GDN_QMEM_COL)
        else:
            proj = mm(u, P["fox_w_in"][l - n_a], out_dtype=BF16, name="fox_in")
            main, lse = fox_fwd(proj, *shared)
            sm = (main, lse)
            qm = (proj, MEM_WIDTH, FOX_QMEM_COL)
        mo = mem_fwd(qm, kmem, vmem)
        cat = jnp.concatenate([main, mo], axis=1)
        h2 = mm(cat, P["w_out"][l], res=h1, name="mix_out")
        h3, s2 = ffn_fwd(h2, _row(P["ffn2_norm"][l]), P["ffn2_w_gate_up"][l], P["ffn2_w_down"][l])
        saved.append((s1, h1, u, kmem, vmem, proj, sm, qm, cat, s2))
        h = h3
        if l == n_a - 1:
            nkv = rms_fwd(h, _row(P["kv_norm"]))
            kv = mm(nkv, P["kv_w_pad"][:, :2 * FOX_WIDTH], out_dtype=BF16, name="fox_kv")
            f = mm(nkv, P["kv_w_pad"][:, 2 * FOX_WIDTH:], name="fox_f")
            bf_row = _row(P["kv_b_f"], LANES)
            cb, crow = fox_gate_fwd(f, bf_row)
            shared = (kv, cb, crow)
            kv_saved = (h, nkv, f, bf_row)

    part, dy = loss_head(h, _row(P["final_norm"]), target)
    dh, G["final_norm"] = rms_bwd(h, _row(P["final_norm"]), dy)

    per_layer = {n: [None] * depth for n in ("ffn1_norm", "ffn1_w_gate_up", "ffn1_w_down", "mix_norm", "ffn2_norm",
                                             "ffn2_w_gate_up", "ffn2_w_down", "w_out", "mem_w_kv")}
    gdn_g = {n: [None] * n_a for n in ("gdn_w_in_pad", "conv8", "gdn_A_log", "gdn_dt_bias", "gdn_out_norm")}
    fox_g = [None] * (depth - n_a)
    dmem_n = None
    dkv_acc = dcb_acc = None
    for l in reversed(range(depth)):
        s1, h1, u, kmem, vmem, proj, sm, qm, cat, s2 = saved[l]
        if l == n_a - 1:
            hk, nkv, f, bf_row = kv_saved
            dk, dv, dcrow = dkv_acc
            df, dbf = fox_gate_bwd(f, bf_row, dcrow, dcb_acc)
            dp = jnp.concatenate([dk.astype(BF16), dv.astype(BF16), df.astype(BF16)], axis=1)
            G["kv_w_pad"] = mm(nkv, dp, ta=True, name="fox_kv_dw")
            G["kv_b_f"] = dbf
            dnkv = mm(dp, P["kv_w_pad"], tb=True, out_dtype=BF16, name="fox_kv_dx")
            dh, G["kv_norm"] = rms_bwd(hk, _row(P["kv_norm"]), dnkv, dh)
        dh, per_layer["ffn2_w_gate_up"][l], per_layer["ffn2_w_down"][l], per_layer["ffn2_norm"][l] = ffn_bwd(
            dh, s2, _row(P["ffn2_norm"][l]), P["ffn2_w_gate_up"][l], P["ffn2_w_down"][l])
        dcat = mm(dh, P["w_out"][l], tb=True, out_dtype=BF16, name="mix_out_dx")
        per_layer["w_out"][l] = mm(cat, dh, ta=True, name="mix_out_dw")
        dqm, dkm, dvm = mem_bwd(qm, kmem, vmem, (dcat, MEM_WIDTH, FOX_QMEM_COL))
        dkvm = jnp.concatenate([dkm, dvm], axis=1)
        per_layer["mem_w_kv"][l] = mm(mem_n, dkvm, ta=True, name="mem_kv_dw")
        dmem_n = mm(dkvm, P["mem_w_kv"][l], tb=True, res=dmem_n, name="mem_kv_dx")
        dmain = (dcat, GDN_WIDTH, 0)
        if l < n_a:
            gp = (P["conv8"][l], _row(P["gdn_A_log"][l], LANES), _row(P["gdn_dt_bias"][l], LANES), _row(P["gdn_out_norm"][l]))
            dqkv, dz, dab, gdn_g["conv8"][l], gdn_g["gdn_A_log"][l], gdn_g["gdn_dt_bias"][l], gdn_g["gdn_out_norm"][l] = gdn_bwd(
                dmain, proj, sm, *gp)
            dproj = jnp.concatenate([dqkv, dz, dqm, dab], axis=1)
            gdn_g["gdn_w_in_pad"][l] = mm(u, dproj, ta=True, name="gdn_in_dw")
            du = mm(dproj, P["gdn_w_in_pad"][l], tb=True, out_dtype=BF16, name="gdn_in_dx")
        else:
            o, lse = sm
            dq, dcb_acc = fox_bwd_dq(proj, *shared, o, lse, dmain, dcb_acc)
            dkv_acc = fox_bwd_dkv(proj, *shared, o, lse, dmain, dkv_acc)
            dproj = jnp.concatenate([dq, dqm], axis=1)
            fox_g[l - n_a] = mm(u, dproj, ta=True, name="fox_in_dw")
            du = mm(dproj, P["fox_w_in"][l - n_a], tb=True, out_dtype=BF16, name="fox_in_dx")
        dh, per_layer["mix_norm"][l] = rms_bwd(h1, _row(P["mix_norm"][l]), du, dh)
        dh, per_layer["ffn1_w_gate_up"][l], per_layer["ffn1_w_down"][l], per_layer["ffn1_norm"][l] = ffn_bwd(
            dh, s1, _row(P["ffn1_norm"][l]), P["ffn1_w_gate_up"][l], P["ffn1_w_down"][l])

    (G["mem_norm"],) = rowwise_bwd(f_rmsnorm, [mems], [mem_gain], [dmem_n], tile=ROW_TILE, name="mem_norm_bwd",
                                   row_grads=[None], const_grads=[True])
    for n, v in per_layer.items():
        G[n] = jnp.stack(v)
    for n, v in gdn_g.items():
        G[n] = jnp.stack(v)
    G["fox_w_in"] = jnp.stack(fox_g)
    return part, dh, G


_GDN_O0 = 4 * GDN_WIDTH
_GDN_O1 = _GDN_O0 + 2 * GDN_HEADS
_KV_WIDTH = 2 * FOX_WIDTH + FOX_HEADS


def derived_weights(gdn_w_in, gdn_conv, kv_w):
    zeros = jnp.zeros(gdn_w_in.shape[:-1] + (LANES - 2 * GDN_HEADS,), gdn_w_in.dtype)
    return dict(
        gdn_w_in_pad=jnp.concatenate([gdn_w_in[..., :_GDN_O0], gdn_w_in[..., _GDN_O1:], gdn_w_in[..., _GDN_O0:_GDN_O1], zeros], axis=-1),
        conv8=jnp.pad(gdn_conv.astype(F32), ((0, 0), (0, 8 - CONV_WIDTH), (0, 0))),
        kv_w_pad=jnp.pad(kv_w, ((0, 0), (0, KV_PAD_WIDTH - _KV_WIDTH))))


def reference_layout(G):
    gp = G["gdn_w_in_pad"]
    out = dict(G)
    out["gdn_w_in"] = jnp.concatenate([gp[..., :_GDN_O0], gp[..., _GDN_O0 + MEM_WIDTH:_GDN_O0 + MEM_WIDTH + 2 * GDN_HEADS],
                                       gp[..., _GDN_O0:_GDN_O0 + MEM_WIDTH]], axis=-1)
    out["gdn_conv"] = G["conv8"][:, :CONV_WIDTH]
    out["kv_w"] = G["kv_w_pad"][:, :_KV_WIDTH]
    out["gdn_A_log"] = G["gdn_A_log"][:, 0, :GDN_HEADS]
    out["gdn_dt_bias"] = G["gdn_dt_bias"][:, 0, :GDN_HEADS]
    out["gdn_out_norm"] = G["gdn_out_norm"][:, 0, :]
    out["kv_b_f"] = G["kv_b_f"][0, :FOX_HEADS]
    for n in ("ffn1_norm", "mix_norm", "ffn2_norm"):
        out[n] = G[n][:, 0, :]
    for n in ("mem_norm", "kv_norm", "final_norm"):
        out[n] = G[n][0]
    return {n: out[n] for n in WEIGHT_NAMES}


def _f32_as_bf16(a):
    a = a.astype(F32)
    hi = a.astype(BF16)
    mid = (a - hi.astype(F32)).astype(BF16)
    lo = (a - hi.astype(F32) - mid.astype(F32)).astype(BF16)
    return jnp.stack([hi, mid, lo], axis=-1)


def _bf16_as_f32(a):
    a = a.astype(F32)
    return (a[..., 0] + a[..., 1]) + a[..., 2]


def gather_weights(W):
    c = lax.axis_index("c")
    pieces, shapes = [], []
    for name, _ in SHARDED:
        p = _f32_as_bf16(W[name]) if name == "gdn_conv" else W[name].astype(BF16)
        pieces.append(p)
        shapes.append(p.shape)
    packed = pack(pieces, BF16)
    half = packed.shape[0] // 2
    gathered = all_gather8(lax.dynamic_slice_in_dim(packed, c * half, half, axis=0), name="gather_weights")
    per_chip = [unpack(gathered[k * 2 * half:(k + 1) * 2 * half], shapes) for k in range(N_CHIPS)]
    full = {}
    for i, (name, axis) in enumerate(SHARDED):
        parts = [per_chip[k][i] for k in range(N_CHIPS)]
        if name == "gdn_conv":
            parts = [_bf16_as_f32(p) for p in parts]
        full[name] = jnp.concatenate(parts, axis=axis)
    return full


def reduce_gradients(G, shard_shapes):
    c = lax.axis_index("c")
    per_chip = []
    for k in range(N_CHIPS):
        pieces = []
        for name, axis in SHARDED:
            size = shard_shapes[name][axis]
            pieces.append(lax.slice_in_dim(G[name], k * size, (k + 1) * size, axis=axis))
        per_chip.append(pack(pieces, F32))
    stacked = jnp.stack(per_chip)
    half = stacked.shape[1] // 2
    mine = lax.dynamic_slice_in_dim(stacked, c * half, half, axis=1)
    theirs = lax.dynamic_slice_in_dim(stacked, (1 - c) * half, half, axis=1)
    pair = add_pairs(mine, sibling_swap(theirs, name="grad_pair_swap"), out_dtype=BF16, name="grad_pair_sum")
    total_half = sum_leading(chip_all_to_all(pair, name="grad_all_to_all"), name="grad_chip_sum")
    other_half = sibling_swap(total_half, name="grad_half_swap")
    total = jnp.zeros((2 * half, PACK_COLS), F32)
    total = lax.dynamic_update_slice_in_dim(total, total_half, c * half, axis=0)
    total = lax.dynamic_update_slice_in_dim(total, other_half, (1 - c) * half, axis=0)
    return dict(zip([n for n, _ in SHARDED], unpack(total, [shard_shapes[n] for n, _ in SHARDED])))


def allreduce_replicated(G):
    names = REPLICATED
    packed = pack([G[n] for n in names], F32, row_multiple=8)
    gathered = all_gather8(packed, name="gather_small_grads").reshape(N_DEV, packed.shape[0], PACK_COLS)
    total = sum_leading(gathered, name="sum_small_grads")
    return dict(zip(names, unpack(total, [G[n].shape for n in names])))


def kernel(x, mem, *rest):
    n_w = len(WEIGHT_NAMES)
    W = dict(zip(WEIGHT_NAMES, rest[:n_w]))
    target = rest[n_w]
    M = dict(zip(WEIGHT_NAMES, rest[n_w + 1:2 * n_w + 1]))
    V = dict(zip(WEIGHT_NAMES, rest[2 * n_w + 1:3 * n_w + 1]))

    full = gather_weights(W)
    P = {n: W[n] for n in REPLICATED}
    P.update({n: full[n] for n in ("ffn1_w_gate_up", "ffn1_w_down", "ffn2_w_gate_up", "ffn2_w_down", "fox_w_in", "w_out", "mem_w_kv")})
    P.update(derived_weights(full["gdn_w_in"], full["gdn_conv"], full["kv_w"]))

    part, dx, G = forward_backward(x[0], mem[0], target[0], P)
    G = reference_layout(G)
    loss = lax.psum(0.5 / x.shape[-1] * jnp.sum(part), ("x", "y", "c"))

    grads = reduce_gradients(G, {n: W[n].shape for n, _ in SHARDED})
    grads.update(allreduce_replicated(G))

    outs = {n: adamw(W[n], grads[n], M[n], V[n], name="adamw_" + n) for n in WEIGHT_NAMES}
    return (loss, dx[None], *[grads[n] for n in WEIGHT_NAMES], *[outs[n][0] for n in WEIGHT_NAMES],
            *[outs[n][1] for n in WEIGHT_NAMES], *[outs[n][2] for n in WEIGHT_NAMES])
```

```python
import jax
import jax.numpy as jnp
from jax import lax
from jax.experimental import pallas as pl
from jax.experimental.pallas import tpu as pltpu

F32, BF16 = jnp.float32, jnp.bfloat16
HI = lax.Precision.HIGHEST
MESH = pl.DeviceIdType.MESH

VMEM_LIMIT_BYTES = 48 * 1024 * 1024
LANES = 128
EPS = 1e-6
NEG_INF = -1e30

D_MODEL = 1024
HEAD_DIM = 128
GDN_HEADS = 6
GDN_WIDTH = GDN_HEADS * HEAD_DIM
FOX_HEADS = 6
FOX_WIDTH = FOX_HEADS * HEAD_DIM
MEM_HEADS = 4
MEM_HEAD_DIM = 64
MEM_WIDTH = MEM_HEADS * MEM_HEAD_DIM
FFN_HIDDEN = 2816
CONV_WIDTH = 4
GDN_CHUNK = 128
N_CHIPS = 4
N_DEV = 8

ADAM_LR, ADAM_B1, ADAM_B2, ADAM_EPS, ADAM_WD, ADAM_STEP = 0.001, 0.9, 0.999, 1e-08, 0.01, 10


def _pcall(body, *, name, out_shape, grid=(), in_specs=None, out_specs=None, scratch=(), sem=None):
    params = dict(vmem_limit_bytes=VMEM_LIMIT_BYTES)
    if sem is not None:
        params["dimension_semantics"] = sem
    kw = dict(grid=grid, in_specs=in_specs, out_specs=out_specs) if grid else {}
    return pl.pallas_call(body, name=name, out_shape=out_shape, scratch_shapes=list(scratch),
                          compiler_params=pltpu.CompilerParams(**params), **kw)


def _pick(n, cands):
    for c in cands:
        if n % c == 0:
            return c
    return n


def _make_dot(dtype, precision):
    def raw(a, b, dims):
        return lax.dot_general(a.astype(dtype), b.astype(dtype), (dims, ((), ())),
                               precision=precision, preferred_element_type=F32)

    @jax.custom_vjp
    def dot(a, b):
        return raw(a, b, ((1,), (0,)))

    def fwd(a, b):
        return dot(a, b), (a, b)

    def bwd(resid, ct):
        a, b = resid
        return raw(ct, b, ((1,), (1,))).astype(a.dtype), raw(a, ct, ((0,), (0,))).astype(b.dtype)

    dot.defvjp(fwd, bwd)
    dot.nn = lambda a, b: raw(a, b, ((1,), (0,)))
    dot.nt = lambda a, b: raw(a, b, ((1,), (1,)))
    dot.tn = lambda a, b: raw(a, b, ((0,), (0,)))
    return dot


bdot = _make_dot(BF16, None)
fdot = _make_dot(F32, HI)
idot = _make_dot(F32, lax.Precision.HIGH)
sdot = idot


def _sigmoid(x):
    return jax.nn.sigmoid(x)


def _silu(x):
    return x * _sigmoid(x)


def _softplus(x):
    return jnp.maximum(x, 0.0) + jnp.log(1.0 + jnp.exp(-jnp.abs(x)))


def _log_sigmoid(x):
    return -_softplus(-x)


def _iota2(shape, dim):
    return lax.broadcasted_iota(jnp.int32, shape, dim)


def mm(a, b, *, ta=False, tb=False, a_split=False, b_split=False, out_dtype=F32, scale=1.0, res=None, name):
    assert not (a_split and ta) and not (b_split and tb)
    (K, M) = a.shape if ta else ((2 * a.shape[2], a.shape[1]) if a_split else a.shape[::-1])
    (N, Kb) = b.shape if tb else ((2 * b.shape[2], b.shape[1]) if b_split else b.shape[::-1])
    assert K == Kb, (a.shape, b.shape, ta, tb)
    tm = _pick(M, (1024, 1408, 512, 256, 128)) if ta else _pick(M, (512, 256, 128))
    tn = _pick(N, (1024, 1408, 1152, 768, 512, 384, 256, 128))
    tk = _pick(K, (512, 256, 128)) if ta else _pick(K, (1024, 1408, 1152, 512, 256, 128))
    assert not a_split or (K // 2) % tk == 0
    assert not b_split or (N // 2) % tn == 0
    nk = K // tk
    dims = (((0 if ta else 1,), (1 if tb else 0,)), ((), ()))

    def body(a_ref, b_ref, *rest):
        o_ref, acc = rest[-2], rest[-1]
        k = pl.program_id(2)

        @pl.when(k == 0)
        def _():
            acc[...] = jnp.zeros_like(acc)

        acc[...] += lax.dot_general(a_ref[...].astype(BF16), b_ref[...].astype(BF16), dims,
                                    preferred_element_type=F32)

        @pl.when(k == nk - 1)
        def _():
            out = acc[...] * scale
            if res is not None:
                out = out + rest[0][...].astype(F32)
            o_ref[...] = out.astype(o_ref.dtype)

    a_spec = pl.BlockSpec((tk, tm), lambda i, j, k: (k, i)) if ta else pl.BlockSpec((tm, tk), lambda i, j, k: (i, k))
    b_spec = pl.BlockSpec((tn, tk), lambda i, j, k: (j, k)) if tb else pl.BlockSpec((tk, tn), lambda i, j, k: (k, j))
    if a_split:
        per_half = K // 2 // tk
        a_spec = pl.BlockSpec((None, tm, tk), lambda i, j, k: (k // per_half, i, k % per_half))
    if b_split:
        per_half = N // 2 // tn
        b_spec = pl.BlockSpec((None, tk, tn), lambda i, j, k: (j // per_half, k, j % per_half))
    o_spec = pl.BlockSpec((tm, tn), lambda i, j, k: (i, j))
    ins, specs = [a, b], [a_spec, b_spec]
    if res is not None:
        ins.append(res)
        specs.append(o_spec)
    return _pcall(body, name=name, out_shape=jax.ShapeDtypeStruct((M, N), out_dtype),
                  grid=(M // tm, N // tn, nk), in_specs=specs, out_specs=o_spec,
                  scratch=[pltpu.VMEM((tm, tn), F32)], sem=("parallel", "parallel", "arbitrary"))(*ins)


def _row_spec(r, tile):
    if isinstance(r, tuple):
        arr, width, col = r
        return arr, pl.BlockSpec((tile, width), lambda i, col=col: (i, col))
    return r, pl.BlockSpec((tile, r.shape[1]), lambda i: (i, 0))


def _const_spec(c):
    return pl.BlockSpec(c.shape, lambda i: (0,) * c.ndim)


def rowwise(fn, rows, consts, outs, *, tile, name):
    arrs, specs = zip(*[_row_spec(r, tile) for r in rows])
    n_rows = arrs[0].shape[0]
    tile = min(tile, n_rows)
    n_in = len(rows) + len(consts)

    def body(*refs):
        res = fn(*[r[...] for r in refs[:n_in]])
        for o_ref, o in zip(refs[n_in:], res):
            o_ref[...] = o.astype(o_ref.dtype)

    arrs, specs = zip(*[_row_spec(r, tile) for r in rows])
    return _pcall(body, name=name,
                  out_shape=[jax.ShapeDtypeStruct((n_rows, w), dt) for w, dt in outs],
                  grid=(n_rows // tile,),
                  in_specs=list(specs) + [_const_spec(c) for c in consts],
                  out_specs=[pl.BlockSpec((tile, w), lambda i: (i, 0)) for w, _ in outs],
                  sem=("parallel",))(*arrs, *consts)


def rowwise_bwd(fn, rows, consts, cts, *, tile, name, row_grads, const_grads, add=None):
    arrs, _ = zip(*[_row_spec(r, tile) for r in rows])
    n_rows = arrs[0].shape[0]
    tile = min(tile, n_rows)
    arrs, specs = zip(*[_row_spec(r, tile) for r in rows])
    ct_arrs, ct_specs = zip(*[_row_spec(c, tile) for c in cts])
    add = add or {}
    add_idx = sorted(add)
    add_arrs, add_specs = (zip(*[_row_spec(add[i], tile) for i in add_idx]) if add_idx else ((), ()))
    nr, nc, nct, na = len(rows), len(consts), len(cts), len(add_idx)
    want_rows = [i for i, d in enumerate(row_grads) if d is not None]
    want_consts = [i for i, w in enumerate(const_grads) if w]

    def body(*refs):
        row_v = [r[...] for r in refs[:nr]]
        const_v = [r[...] for r in refs[nr:nr + nc]]
        ct_v = [r[...] for r in refs[nr + nc:nr + nc + nct]]
        add_v = {i: refs[nr + nc + nct + j][...] for j, i in enumerate(add_idx)}
        out_refs = refs[nr + nc + nct + na:]
        res, vjp = jax.vjp(fn, *row_v, *const_v)
        grads = vjp(tuple(c.astype(o.dtype) for c, o in zip(ct_v, res)))
        for o_ref, i in zip(out_refs, want_rows):
            g = grads[i].astype(F32)
            if i in add_v:
                g = g + add_v[i].astype(F32)
            o_ref[...] = g.astype(o_ref.dtype)
        first = pl.program_id(0) == 0
        for o_ref, i in zip(out_refs[len(want_rows):], want_consts):
            g = grads[nr + i].astype(F32)

            @pl.when(first)
            def _(o_ref=o_ref, g=g):
                o_ref[...] = g

            @pl.when(jnp.logical_not(first))
            def _(o_ref=o_ref, g=g):
                o_ref[...] += g

    def width(r):
        return r[1] if isinstance(r, tuple) else r.shape[1]

    out_shape = [jax.ShapeDtypeStruct((n_rows, width(rows[i])), row_grads[i]) for i in want_rows]
    out_shape += [jax.ShapeDtypeStruct(consts[i].shape, F32) for i in want_consts]
    out_specs = [pl.BlockSpec((tile, width(rows[i])), lambda i_: (i_, 0)) for i in want_rows]
    out_specs += [_const_spec(consts[i]) for i in want_consts]
    return _pcall(body, name=name, out_shape=out_shape, grid=(n_rows // tile,),
                  in_specs=list(specs) + [_const_spec(c) for c in consts] + list(ct_specs) + list(add_specs),
                  out_specs=out_specs, sem=("arbitrary",))(*arrs, *consts, *ct_arrs, *add_arrs)


def f_rmsnorm(x, g):
    x = x.astype(F32)
    return (x * lax.rsqrt(jnp.mean(x * x, axis=-1, keepdims=True) + EPS) * g,)


def _head_sel(first_lane):
    r, c = _iota2((LANES, GDN_WIDTH), 0), _iota2((LANES, GDN_WIDTH), 1)
    return (r == c // HEAD_DIM + first_lane).astype(F32)


def _tri(n, strict=False):
    r, c = _iota2((n, n), 0), _iota2((n, n), 1)
    return r > c if strict else r >= c


def f_gdn_pre(xc, ab, a_log, dt_bias):
    s = _silu(xc.astype(F32))
    qs, ks = [], []
    for h in range(GDN_HEADS):
        qh = s[:, h * HEAD_DIM:(h + 1) * HEAD_DIM]
        kh = s[:, GDN_WIDTH + h * HEAD_DIM:GDN_WIDTH + (h + 1) * HEAD_DIM]
        qs.append(qh * lax.rsqrt(jnp.sum(qh * qh, axis=-1, keepdims=True) + EPS) * (HEAD_DIM ** -0.5))
        ks.append(kh * lax.rsqrt(jnp.sum(kh * kh, axis=-1, keepdims=True) + EPS))
    q, k = jnp.concatenate(qs, axis=1), jnp.concatenate(ks, axis=1)
    v = s[:, 2 * GDN_WIDTH:]
    ab = ab.astype(F32)
    g = -jnp.exp(a_log) * _softplus(ab + dt_bias)
    gc = fdot(_tri(GDN_CHUNK).astype(F32), fdot(g, _head_sel(0)))
    beta = fdot(_sigmoid(ab), _head_sel(GDN_HEADS))
    return q, k, v, gc, beta


def _unit_lower_inverse(neg_lower):
    C = neg_lower.shape[0]
    inv = (_iota2((C, C), 0) == _iota2((C, C), 1)).astype(F32) + neg_lower
    power = neg_lower
    for _ in range(6):
        power = idot.nn(power, power)
        inv = inv + idot.nn(power, inv)
    return inv


@jax.custom_vjp
def _solve_with_inverse(inv, neg_lower, rhs):
    return idot.nn(inv, rhs)


def _solve_fwd(inv, neg_lower, rhs):
    x = idot.nn(inv, rhs)
    return x, (inv, x)


def _solve_bwd(resid, ct):
    inv, x = resid
    d_rhs = idot.tn(inv, ct)
    return jnp.zeros_like(inv), idot.nt(d_rhs, x), d_rhs


_solve_with_inverse.defvjp(_solve_fwd, _solve_bwd)


def f_gdn_intra(q, k, v, gc, beta, inv=None):
    C = GDN_CHUNK
    causal, strict = _tri(C), _tri(C, strict=True)
    is_last = _iota2((C, HEAD_DIM), 0) == C - 1
    outs = [[] for _ in range(6 if inv is None else 5)]
    for h in range(GDN_HEADS):
        sl = slice(h * HEAD_DIM, (h + 1) * HEAD_DIM)
        qh, kh, vh, gh, bh = q[:, sl], k[:, sl], v[:, sl], gc[:, sl], beta[:, sl]
        gdiff = gh - gh.T
        decay = jnp.where(causal, jnp.exp(jnp.where(causal, gdiff, 0.0)), 0.0)
        kb = kh * bh
        neg_lower = jnp.where(strict, -(idot(kb, kh.T) * decay), 0.0)
        rhs = jnp.concatenate([vh * bh, kb * jnp.exp(gh)], axis=1)
        if inv is None:
            inv_h = _unit_lower_inverse(neg_lower)
            sol = idot.nn(inv_h, rhs)
        else:
            sol = _solve_with_inverse(inv[:, sl], neg_lower, rhs)
        qk = jnp.where(causal, idot(qh, kh.T) * decay, 0.0)
        g_last = jnp.sum(jnp.where(is_last, gh, 0.0), axis=0, keepdims=True)
        vals = (sol[:, :HEAD_DIM], sol[:, HEAD_DIM:], qk, kh * jnp.exp(g_last - gh), qh * jnp.exp(gh))
        for lst, val in zip(outs, vals + ((inv_h,) if inv is None else ())):
            lst.append(val)
    return tuple(jnp.concatenate(o, axis=1) for o in outs)


def f_gdn_post(o, z, gain):
    z = z.astype(F32)
    parts = []
    for h in range(GDN_HEADS):
        oh = o[:, h * HEAD_DIM:(h + 1) * HEAD_DIM]
        parts.append(oh * lax.rsqrt(jnp.mean(oh * oh, axis=-1, keepdims=True) + EPS) * gain)
    return (jnp.concatenate(parts, axis=1) * _silu(z),)


def f_mem_attn(q, k, v):
    q = q.astype(F32)
    lane_head = _iota2((1, MEM_WIDTH), 1) // MEM_HEAD_DIM
    out = jnp.zeros(q.shape, F32)
    kt = k.astype(F32).T
    for h in range(MEM_HEADS):
        mask = (lane_head == h).astype(F32)
        logits = bdot(q * mask, kt) * (MEM_HEAD_DIM ** -0.5)
        p = jnp.exp(logits - jnp.max(logits, axis=-1, keepdims=True))
        p = p / jnp.sum(p, axis=-1, keepdims=True)
        out = out + bdot(p, v) * mask
    return (out,)


def f_loss(y, t):
    d = y - t
    return (d * d,)


def conv_fwd(proj, w8, *, width, tile=512):
    n_rows = proj.shape[0]
    tile = min(tile, n_rows)

    def body(x_ref, halo_ref, w_ref, o_ref):
        i = pl.program_id(0)
        halo = jnp.where(i > 0, halo_ref[...].astype(F32), 0.0)
        xs = jnp.concatenate([halo, x_ref[...].astype(F32)], axis=0)
        acc = xs[8:] * w_ref[3:4, :]
        for j in range(CONV_WIDTH - 1):
            acc = acc + pltpu.roll(xs, CONV_WIDTH - 1 - j, 0)[8:] * w_ref[j:j + 1, :]
        o_ref[...] = acc

    return _pcall(body, name="gdn_conv_fwd", out_shape=jax.ShapeDtypeStruct((n_rows, width), F32),
                  grid=(n_rows // tile,),
                  in_specs=[pl.BlockSpec((tile, width), lambda i: (i, 0)),
                            pl.BlockSpec((8, width), lambda i: (jnp.maximum(i * (tile // 8) - 1, 0), 0)),
                            pl.BlockSpec((8, width), lambda i: (0, 0))],
                  out_specs=pl.BlockSpec((tile, width), lambda i: (i, 0)), sem=("parallel",))(proj, proj, w8)


def conv_bwd(proj, w8, dy, *, width, tile=512):
    n_rows = proj.shape[0]
    tile = min(tile, n_rows)
    n = n_rows // tile

    def body(x_ref, xhalo_ref, w_ref, dy_ref, dyhalo_ref, dx_ref, dw_ref):
        i = pl.program_id(0)
        dy = dy_ref[...]
        after = jnp.where(i < n - 1, dyhalo_ref[...], 0.0)
        ds = jnp.concatenate([dy, after], axis=0)
        dx = dy * w_ref[3:4, :]
        for j in range(CONV_WIDTH - 1):
            shift = CONV_WIDTH - 1 - j
            dx = dx + pltpu.roll(ds, tile + 8 - shift, 0)[:tile] * w_ref[j:j + 1, :]
        dx_ref[...] = dx.astype(dx_ref.dtype)
        halo = jnp.where(i > 0, xhalo_ref[...].astype(F32), 0.0)
        xs = jnp.concatenate([halo, x_ref[...].astype(F32)], axis=0)
        rows = [jnp.sum(dy * pltpu.roll(xs, CONV_WIDTH - 1 - j, 0)[8:], axis=0, keepdims=True)
                for j in range(CONV_WIDTH - 1)]
        rows.append(jnp.sum(dy * xs[8:], axis=0, keepdims=True))
        dw = jnp.concatenate(rows + [jnp.zeros((8 - CONV_WIDTH, width), F32)], axis=0)

        @pl.when(i == 0)
        def _():
            dw_ref[...] = dw

        @pl.when(i > 0)
        def _():
            dw_ref[...] += dw

    t8 = tile // 8
    return _pcall(body, name="gdn_conv_bwd",
                  out_shape=[jax.ShapeDtypeStruct((n_rows, width), BF16), jax.ShapeDtypeStruct((8, width), F32)],
                  grid=(n,),
                  in_specs=[pl.BlockSpec((tile, width), lambda i: (i, 0)),
                            pl.BlockSpec((8, width), lambda i: (jnp.maximum(i * t8 - 1, 0), 0)),
                            pl.BlockSpec((8, width), lambda i: (0, 0)),
                            pl.BlockSpec((tile, width), lambda i: (i, 0)),
                            pl.BlockSpec((8, width), lambda i: (jnp.minimum((i + 1) * t8, n * t8 - 1), 0))],
                  out_specs=[pl.BlockSpec((tile, width), lambda i: (i, 0)), pl.BlockSpec((8, width), lambda i: (0, 0))],
                  sem=("arbitrary",))(proj, proj, w8, dy, dy)


def gdn_scan_fwd(u, w, qk, kt, qh, gc):
    n_rows = u.shape[0]
    C, n = GDN_CHUNK, u.shape[0] // GDN_CHUNK

    def body(u_ref, w_ref, qk_ref, kt_ref, qh_ref, gc_ref, o_ref, vn_ref, sin_ref, st):
        @pl.when(pl.program_id(0) == 0)
        def _():
            st[...] = jnp.zeros_like(st)

        sin_ref[0] = st[...]
        for h in range(GDN_HEADS):
            sl = slice(h * HEAD_DIM, (h + 1) * HEAD_DIM)
            s = st[sl, :]
            v_new = u_ref[:, sl] - sdot(w_ref[:, sl], s)
            o_ref[:, sl] = sdot(qh_ref[:, sl], s) + sdot(qk_ref[:, sl], v_new)
            vn_ref[:, sl] = v_new
            st[sl, :] = s * jnp.exp(gc_ref[C - 1:C, sl]) + sdot.tn(kt_ref[:, sl], v_new)

    blk = pl.BlockSpec((C, GDN_WIDTH), lambda i: (i, 0))
    return _pcall(body, name="gdn_scan_fwd",
                  out_shape=[jax.ShapeDtypeStruct((n_rows, GDN_WIDTH), F32), jax.ShapeDtypeStruct((n_rows, GDN_WIDTH), F32),
                             jax.ShapeDtypeStruct((n, GDN_WIDTH, HEAD_DIM), F32)],
                  grid=(n,), in_specs=[blk] * 6,
                  out_specs=[blk, blk, pl.BlockSpec((1, GDN_WIDTH, HEAD_DIM), lambda i: (i, 0, 0))],
                  scratch=[pltpu.VMEM((GDN_WIDTH, HEAD_DIM), F32)], sem=("arbitrary",))(u, w, qk, kt, qh, gc)


def gdn_scan_bwd(do, w, qk, kt, qh, gc, vn, sin):
    n_rows = do.shape[0]
    C, n = GDN_CHUNK, do.shape[0] // GDN_CHUNK

    def body(do_ref, w_ref, qk_ref, kt_ref, qh_ref, gc_ref, vn_ref, sin_ref,
             du_ref, dw_ref, dqk_ref, dkt_ref, dqh_ref, dgl_ref, dst):
        @pl.when(pl.program_id(0) == 0)
        def _():
            dst[...] = jnp.zeros_like(dst)

        for h in range(GDN_HEADS):
            sl = slice(h * HEAD_DIM, (h + 1) * HEAD_DIM)
            s, ds_out, d_o, v_new = sin_ref[0, sl, :], dst[sl, :], do_ref[:, sl], vn_ref[:, sl]
            e = jnp.exp(gc_ref[C - 1:C, sl])
            dvn = sdot.tn(qk_ref[:, sl], d_o) + sdot(kt_ref[:, sl], ds_out)
            du_ref[:, sl] = dvn
            dw_ref[:, sl] = -sdot.nt(dvn, s)
            dqk_ref[:, sl] = sdot.nt(d_o, v_new)
            dkt_ref[:, sl] = sdot.nt(v_new, ds_out)
            dqh_ref[:, sl] = sdot.nt(d_o, s)
            dgl = jnp.sum(ds_out * s, axis=0, keepdims=True) * e
            dgl_ref[:, sl] = jnp.broadcast_to(dgl, (8, HEAD_DIM))
            dst[sl, :] = sdot.tn(qh_ref[:, sl], d_o) + e * ds_out - sdot.tn(w_ref[:, sl], dvn)

    blk = pl.BlockSpec((C, GDN_WIDTH), lambda i: (n - 1 - i, 0))
    row = jax.ShapeDtypeStruct((n_rows, GDN_WIDTH), F32)
    return _pcall(body, name="gdn_scan_bwd",
                  out_shape=[row] * 5 + [jax.ShapeDtypeStruct((n * 8, GDN_WIDTH), F32)],
                  grid=(n,), in_specs=[blk] * 7 + [pl.BlockSpec((1, GDN_WIDTH, HEAD_DIM), lambda i: (n - 1 - i, 0, 0))],
                  out_specs=[blk] * 5 + [pl.BlockSpec((8, GDN_WIDTH), lambda i: (n - 1 - i, 0))],
                  scratch=[pltpu.VMEM((GDN_WIDTH, HEAD_DIM), F32)], sem=("arbitrary",))(do, w, qk, kt, qh, gc, vn, sin)


def gdn_intra_bwd(q, k, v, gc, beta, inv, cts, dgl):
    n_rows = q.shape[0]
    C = GDN_CHUNK

    def body(*refs):
        ins = [r[...] for r in refs[:5]]
        inv_v = refs[5][...]
        ct = tuple(r[...] for r in refs[6:11])
        dgl_v = refs[11][...]
        _, vjp = jax.vjp(lambda *a: f_gdn_intra(*a, inv=inv_v), *ins)
        grads = list(vjp(ct))
        last = _iota2((C, GDN_WIDTH), 0) == C - 1
        grads[3] = grads[3] + jnp.where(last, jnp.broadcast_to(dgl_v[0:1, :], (C, GDN_WIDTH)), 0.0)
        for o_ref, g in zip(refs[12:], grads):
            o_ref[...] = g

    blk = pl.BlockSpec((C, GDN_WIDTH), lambda i: (i, 0))
    return _pcall(body, name="gdn_intra_bwd", out_shape=[jax.ShapeDtypeStruct((n_rows, GDN_WIDTH), F32)] * 5,
                  grid=(n_rows // C,), in_specs=[blk] * 11 + [pl.BlockSpec((8, GDN_WIDTH), lambda i: (i, 0))],
                  out_specs=[blk] * 5, sem=("parallel",))(q, k, v, gc, beta, inv, *cts, dgl)


def fox_gate_fwd(f, b_f):
    n_rows = f.shape[0]
    T = LANES

    def body(f_ref, b_ref, cb_ref, crow_ref, carry):
        @pl.when(pl.program_id(0) == 0)
        def _():
            carry[...] = jnp.zeros_like(carry)

        c = fdot(_tri(T).astype(F32), _log_sigmoid(f_ref[...] + b_ref[...])) + carry[...]
        carry[...] = c[T - 1:T, :]
        cb_ref[...] = fdot(c, _head_sel(0))
        ct = c.T
        for h in range(FOX_HEADS):
            crow_ref[h] = ct[h:h + 1, :]

    return _pcall(body, name="fox_gate_fwd",
                  out_shape=[jax.ShapeDtypeStruct((n_rows, FOX_WIDTH), F32), jax.ShapeDtypeStruct((FOX_HEADS, 1, n_rows), F32)],
                  grid=(n_rows // T,),
                  in_specs=[pl.BlockSpec((T, LANES), lambda i: (i, 0)), pl.BlockSpec((1, LANES), lambda i: (0, 0))],
                  out_specs=[pl.BlockSpec((T, FOX_WIDTH), lambda i: (i, 0)), pl.BlockSpec((FOX_HEADS, 1, T), lambda i: (0, 0, i))],
                  scratch=[pltpu.VMEM((1, LANES), F32)], sem=("arbitrary",))(f, b_f)


def fox_gate_bwd(f, b_f, dcrow, dcb):
    n_rows = f.shape[0]
    T = LANES
    n = n_rows // T

    def body(f_ref, b_ref, dc_ref, dcb_ref, df_ref, db_ref, carry):
        i = pl.program_id(0)

        @pl.when(i == 0)
        def _():
            carry[...] = jnp.zeros_like(carry)

        rows = [dc_ref[h] for h in range(FOX_HEADS)] + [jnp.zeros((T - FOX_HEADS, T), F32)]
        first_lane = (_iota2((FOX_WIDTH, LANES), 0) == _iota2((FOX_WIDTH, LANES), 1) * HEAD_DIM).astype(F32)
        dc = jnp.concatenate(rows, axis=0).T + fdot(dcb_ref[...], first_lane)
        dlog = fdot.tn(_tri(T).astype(F32), dc) + carry[...]
        carry[...] = dlog[0:1, :]
        df = dlog * (1.0 - _sigmoid(f_ref[...] + b_ref[...]))
        df_ref[...] = df
        db = jnp.sum(df, axis=0, keepdims=True)

        @pl.when(i == 0)
        def _():
            db_ref[...] = db

        @pl.when(i > 0)
        def _():
            db_ref[...] += db

    return _pcall(body, name="fox_gate_bwd",
                  out_shape=[jax.ShapeDtypeStruct((n_rows, LANES), F32), jax.ShapeDtypeStruct((1, LANES), F32)],
                  grid=(n,),
                  in_specs=[pl.BlockSpec((T, LANES), lambda i: (n - 1 - i, 0)), pl.BlockSpec((1, LANES), lambda i: (0, 0)),
                            pl.BlockSpec((FOX_HEADS, 1, T), lambda i: (0, 0, n - 1 - i)),
                            pl.BlockSpec((T, FOX_WIDTH), lambda i: (n - 1 - i, 0))],
                  out_specs=[pl.BlockSpec((T, LANES), lambda i: (n - 1 - i, 0)), pl.BlockSpec((1, LANES), lambda i: (0, 0))],
                  scratch=[pltpu.VMEM((1, LANES), F32)], sem=("arbitrary",))(f, b_f, dcrow, dcb)


def _fox_tile(n_rows):
    return min(512, n_rows)


def _fox_pairs(n, query_major):
    pairs = [(i, j) for i in range(n) for j in range(i + 1)] if query_major else [(i, j) for j in range(n) for i in range(j, n)]
    return jnp.asarray([p[0] for p in pairs], jnp.int32), jnp.asarray([p[1] for p in pairs], jnp.int32)


def _pcall_tables(body, *, name, out_shape, grid, tables, in_specs, out_specs, scratch, sem):
    spec = pltpu.PrefetchScalarGridSpec(num_scalar_prefetch=len(tables), grid=grid, in_specs=in_specs, out_specs=out_specs,
                                        scratch_shapes=list(scratch))
    return pl.pallas_call(body, name=name, out_shape=out_shape, grid_spec=spec,
                          compiler_params=pltpu.CompilerParams(vmem_limit_bytes=VMEM_LIMIT_BYTES, dimension_semantics=sem))


def _fox_logits(q, k, ccol, crow, masked, t):
    s = bdot.nt(q, k) * (HEAD_DIM ** -0.5) + jnp.tile(ccol, (1, t // LANES)) - crow
    if masked:
        s = jnp.where(_iota2((t, t), 0) >= _iota2((t, t), 1), s, NEG_INF)
    return s


def _fox_p_ds(masked, t, q_ref, k_ref, v_ref, cb_ref, crow_ref, o_ref, lse_ref, do_ref):
    s = _fox_logits(q_ref[...], k_ref[...], cb_ref[...], crow_ref[0], masked, t)
    p = jnp.exp(s - jnp.tile(lse_ref[...], (1, t // LANES)))
    d_o = do_ref[...].astype(F32)
    delta = jnp.sum(d_o * o_ref[...].astype(F32), axis=-1, keepdims=True)
    return p, p * (bdot.nt(d_o, v_ref[...]) - delta), d_o


def _fox_specs(t, do_col):
    qspec = pl.BlockSpec((t, HEAD_DIM), lambda h, p, it, jt: (it[p], h))
    dospec = pl.BlockSpec((t, HEAD_DIM), lambda h, p, it, jt: (it[p], do_col + h))
    kspec = pl.BlockSpec((t, HEAD_DIM), lambda h, p, it, jt: (jt[p], h))
    vspec = pl.BlockSpec((t, HEAD_DIM), lambda h, p, it, jt: (jt[p], FOX_HEADS + h))
    cspec = pl.BlockSpec((1, 1, t), lambda h, p, it, jt: (h, 0, jt[p]))
    return qspec, dospec, kspec, vspec, cspec


def fox_fwd(q, kv, cb, crow):
    n_rows = kv.shape[0]
    t = _fox_tile(n_rows)
    n = n_rows // t
    tables = _fox_pairs(n, True)

    def body(it, jt, q_ref, k_ref, v_ref, cb_ref, crow_ref, o_ref, lse_ref, m_sc, l_sc, acc):
        i, j = it[pl.program_id(1)], jt[pl.program_id(1)]

        @pl.when(j == 0)
        def _():
            m_sc[...] = jnp.full(m_sc.shape, NEG_INF, F32)
            l_sc[...] = jnp.zeros_like(l_sc)
            acc[...] = jnp.zeros_like(acc)

        def step(masked):
            s = _fox_logits(q_ref[...], k_ref[...], cb_ref[...], crow_ref[0], masked, t)
            m_old = m_sc[...]
            m_new = jnp.maximum(m_old, jnp.max(s, axis=-1, keepdims=True))
            alpha = jnp.exp(m_old - m_new)
            p = jnp.exp(s - jnp.tile(m_new, (1, t // LANES)))
            l_sc[...] = l_sc[...] * alpha + jnp.sum(p, axis=-1, keepdims=True)
            acc[...] = acc[...] * alpha + bdot(p, v_ref[...])
            m_sc[...] = m_new

        @pl.when(j < i)
        def _():
            step(False)

        @pl.when(j == i)
        def _():
            step(True)
            o_ref[...] = (acc[...] / l_sc[...]).astype(o_ref.dtype)
            lse_ref[...] = m_sc[...] + jnp.log(l_sc[...])

    qspec, _, kspec, vspec, cspec = _fox_specs(t, 0)
    return _pcall_tables(body, name="fox_fwd",
                         out_shape=[jax.ShapeDtypeStruct((n_rows, FOX_WIDTH), BF16), jax.ShapeDtypeStruct((n_rows, FOX_WIDTH), F32)],
                         grid=(FOX_HEADS, n * (n + 1) // 2), tables=tables,
                         in_specs=[qspec, kspec, vspec, qspec, cspec], out_specs=[qspec, qspec],
                         scratch=[pltpu.VMEM((t, HEAD_DIM), F32)] * 3, sem=("parallel", "arbitrary"))(*tables, q, kv, kv, cb, crow)


def fox_bwd_dq(q, kv, cb, crow, o, lse, do, prev=None):
    do, _, do_col = do
    do_col *= FOX_HEADS
    n_rows = kv.shape[0]
    t = _fox_tile(n_rows)
    n = n_rows // t
    n_prev = 0 if prev is None else 1
    tables = _fox_pairs(n, True)

    def body(it, jt, q_ref, k_ref, v_ref, cb_ref, crow_ref, o_ref, lse_ref, do_ref, *rest):
        dq_ref, drow_ref, acc, rows = rest[n_prev:]
        i, j = it[pl.program_id(1)], jt[pl.program_id(1)]

        @pl.when(j == 0)
        def _():
            acc[...] = jnp.zeros_like(acc)
            rows[...] = jnp.zeros_like(rows)

        def step(masked):
            _, ds, _ = _fox_p_ds(masked, t, q_ref, k_ref, v_ref, cb_ref, crow_ref, o_ref, lse_ref, do_ref)
            acc[...] += bdot(ds, k_ref[...])
            rows[...] += jnp.sum(ds, axis=-1, keepdims=True)

        @pl.when(j < i)
        def _():
            step(False)

        @pl.when(j == i)
        def _():
            step(True)
            dq_ref[...] = (acc[...] * (HEAD_DIM ** -0.5)).astype(dq_ref.dtype)
            drow_ref[...] = rows[...] + rest[0][...] if n_prev else rows[...]

    qspec, dospec, kspec, vspec, cspec = _fox_specs(t, do_col)
    return _pcall_tables(body, name="fox_bwd_dq" + ("_acc" if n_prev else ""),
                         out_shape=[jax.ShapeDtypeStruct((n_rows, FOX_WIDTH), BF16), jax.ShapeDtypeStruct((n_rows, FOX_WIDTH), F32)],
                         grid=(FOX_HEADS, n * (n + 1) // 2), tables=tables,
                         in_specs=[qspec, kspec, vspec, qspec, cspec, qspec, qspec, dospec] + [qspec] * n_prev,
                         out_specs=[qspec, qspec], scratch=[pltpu.VMEM((t, HEAD_DIM), F32)] * 2,
                         sem=("parallel", "arbitrary"))(*tables, q, kv, kv, cb, crow, o, lse, do, *([prev] if n_prev else []))


def fox_bwd_dkv(q, kv, cb, crow, o, lse, do, prev=None):
    do, _, do_col = do
    do_col *= FOX_HEADS
    n_rows = kv.shape[0]
    t = _fox_tile(n_rows)
    n = n_rows // t
    n_prev = 0 if prev is None else 3
    tables = _fox_pairs(n, False)

    def body(it, jt, q_ref, k_ref, v_ref, cb_ref, crow_ref, o_ref, lse_ref, do_ref, *rest):
        prev_refs = rest[:n_prev]
        dk_ref, dv_ref, dc_ref, dk_acc, dv_acc, dc_acc = rest[n_prev:]
        i, j = it[pl.program_id(1)], jt[pl.program_id(1)]

        def step(masked):
            p, ds, d_o = _fox_p_ds(masked, t, q_ref, k_ref, v_ref, cb_ref, crow_ref, o_ref, lse_ref, do_ref)
            dv_acc[...] += bdot.tn(p, d_o)
            dk_acc[...] += bdot.tn(ds, q_ref[...])
            dc_acc[...] -= jnp.sum(ds, axis=0, keepdims=True)

        @pl.when(i == j)
        def _():
            dk_acc[...] = jnp.zeros_like(dk_acc)
            dv_acc[...] = jnp.zeros_like(dv_acc)
            dc_acc[...] = jnp.zeros_like(dc_acc)
            step(True)

        @pl.when(i > j)
        def _():
            step(False)

        @pl.when(i == n - 1)
        def _():
            dk, dv, dc = dk_acc[...] * (HEAD_DIM ** -0.5), dv_acc[...], dc_acc[...]
            if n_prev:
                dk, dv, dc = dk + prev_refs[0][...], dv + prev_refs[1][...], dc + prev_refs[2][0]
            dk_ref[...] = dk
            dv_ref[...] = dv
            dc_ref[0] = dc

    qspec, dospec, kspec, vspec, cspec = _fox_specs(t, do_col)
    return _pcall_tables(body, name="fox_bwd_dkv" + ("_acc" if n_prev else ""),
                         out_shape=[jax.ShapeDtypeStruct((n_rows, FOX_WIDTH), F32), jax.ShapeDtypeStruct((n_rows, FOX_WIDTH), F32),
                                    jax.ShapeDtypeStruct((FOX_HEADS, 1, n_rows), F32)],
                         grid=(FOX_HEADS, n * (n + 1) // 2), tables=tables,
                         in_specs=[qspec, kspec, vspec, qspec, cspec, qspec, qspec, dospec] + [kspec, kspec, cspec][:n_prev],
                         out_specs=[kspec, kspec, cspec],
                         scratch=[pltpu.VMEM((t, HEAD_DIM), F32), pltpu.VMEM((t, HEAD_DIM), F32), pltpu.VMEM((1, t), F32)],
                         sem=("parallel", "arbitrary"))(*tables, q, kv, kv, cb, crow, o, lse, do, *(prev or ()))


def loss_head(h, gain, target, *, tile=512):
    n_rows, d = h.shape
    tile = min(tile, n_rows)

    def body(h_ref, g_ref, t_ref, part_ref, dy_ref):
        (y,) = f_rmsnorm(h_ref[...], g_ref[...])
        diff = y - t_ref[...]
        dy_ref[...] = diff * (1.0 / d)
        part = jnp.sum(diff * diff, axis=0, keepdims=True)
        first = pl.program_id(0) == 0

        @pl.when(first)
        def _():
            part_ref[...] = part

        @pl.when(jnp.logical_not(first))
        def _():
            part_ref[...] += part

    blk = pl.BlockSpec((tile, d), lambda i: (i, 0))
    one = pl.BlockSpec((1, d), lambda i: (0, 0))
    return _pcall(body, name="loss_head",
                  out_shape=[jax.ShapeDtypeStruct((1, d), F32), jax.ShapeDtypeStruct((n_rows, d), F32)],
                  grid=(n_rows // tile,), in_specs=[blk, one, blk], out_specs=[one, blk], sem=("arbitrary",))(h, gain, target)


def adamw(w, g, m, v, *, name):
    shape = w.shape
    cols = shape[-1] if w.ndim >= 2 else w.size
    rows = w.size // cols
    tile = _pick(rows, (256, 128, 64, 32, 16, 8))
    as2d = lambda a: a.reshape(rows, cols)

    def body(w_ref, g_ref, m_ref, v_ref, d_ref, nm_ref, nv_ref):
        g_ = g_ref[...]
        m_ = ADAM_B1 * m_ref[...] + (1.0 - ADAM_B1) * g_
        v_ = ADAM_B2 * v_ref[...] + (1.0 - ADAM_B2) * (g_ * g_)
        m_hat = m_ / (1.0 - ADAM_B1 ** ADAM_STEP)
        v_hat = v_ / (1.0 - ADAM_B2 ** ADAM_STEP)
        d_ref[...] = -ADAM_LR * (m_hat / (jnp.sqrt(v_hat) + ADAM_EPS) + ADAM_WD * w_ref[...])
        nm_ref[...] = m_
        nv_ref[...] = v_

    blk = pl.BlockSpec((tile, cols), lambda i: (i, 0))
    outs = _pcall(body, name=name, out_shape=[jax.ShapeDtypeStruct((rows, cols), F32)] * 3, grid=(rows // tile,),
                  in_specs=[blk] * 4, out_specs=[blk] * 3, sem=("parallel",))(as2d(w), as2d(g), as2d(m), as2d(v))
    return tuple(o.reshape(shape) for o in outs)


def sum_leading(a, *, name):
    p, r, c = a.shape
    tile = _pick(r, (256, 128, 64, 32, 16, 8))

    def body(a_ref, o_ref):
        total = a_ref[0].astype(F32)
        for k in range(1, p):
            total = total + a_ref[k].astype(F32)
        o_ref[...] = total

    return _pcall(body, name=name, out_shape=jax.ShapeDtypeStruct((r, c), F32), grid=(r // tile,),
                  in_specs=[pl.BlockSpec((p, tile, c), lambda i: (0, i, 0))],
                  out_specs=pl.BlockSpec((tile, c), lambda i: (i, 0)), sem=("parallel",))(a)


_HBM = pl.BlockSpec(memory_space=pltpu.HBM)


def _comm_call(body, *, name, out_shape, n_in, scratch):
    return pl.pallas_call(body, name=name, out_shape=out_shape, in_specs=[_HBM] * n_in, out_specs=_HBM,
                          scratch_shapes=scratch,
                          compiler_params=pltpu.CompilerParams(has_side_effects=True))


def all_gather8(a, *, name):
    m_per, n = a.shape

    def body(x_ref, out_ref, send_sems, recv_sems, local_sem):
        x, y, c = lax.axis_index("x"), lax.axis_index("y"), lax.axis_index("c")
        me, sibling = (x, y, c), (x, y, 1 - c)
        chips = [(1 - x, y), (x, 1 - y), (1 - x, 1 - y)]

        def rows(px, py, pc):
            return out_ref.at[pl.ds((4 * px + 2 * py + pc) * m_per, m_per), :]

        def copy(k, block, to, src=None):
            return pltpu.make_async_remote_copy(
                src_ref=rows(*block) if src is None else src, dst_ref=rows(*block),
                send_sem=send_sems.at[k], recv_sem=recv_sems.at[k], device_id=to, device_id_type=MESH)

        mine = pltpu.make_async_copy(x_ref, rows(*me), local_sem)
        mine.start()
        first = [copy(0, me, sibling, src=x_ref)]
        first += [copy(1 + j, me, (*chip, c), src=x_ref) for j, chip in enumerate(chips)]
        for cp in first:
            cp.start()
        passed = [copy(4 + j, (*chip, c), sibling) for j, chip in enumerate(chips)]
        for j, chip in enumerate(chips):
            copy(1 + j, (*chip, c), me).wait_recv()
            passed[j].start()
        copy(0, sibling, me).wait_recv()
        for j, chip in enumerate(chips):
            copy(4 + j, (*chip, 1 - c), me).wait_recv()
        for cp in first + passed:
            cp.wait_send()
        mine.wait()

    return _comm_call(body, name=name, out_shape=jax.ShapeDtypeStruct((N_DEV * m_per, n), a.dtype), n_in=1,
                      scratch=[pltpu.SemaphoreType.DMA((7,)), pltpu.SemaphoreType.DMA((7,)), pltpu.SemaphoreType.DMA])(a)


PACK_COLS = 1024
PACK_ROW_MULTIPLE = 32

WEIGHT_NAMES = ["ffn1_norm", "ffn1_w_gate_up", "ffn1_w_down", "mix_norm", "ffn2_norm", "ffn2_w_gate_up", "ffn2_w_down",
                "gdn_w_in", "gdn_conv", "gdn_A_log", "gdn_dt_bias", "gdn_out_norm", "fox_w_in", "w_out", "mem_norm",
                "mem_w_kv", "kv_norm", "kv_w", "kv_b_f", "final_norm"]
SHARDED = [("ffn1_w_gate_up", 2), ("ffn1_w_down", 1), ("ffn2_w_gate_up", 2), ("ffn2_w_down", 1), ("gdn_w_in", 2),
           ("gdn_conv", 2), ("fox_w_in", 1), ("w_out", 1), ("mem_w_kv", 1), ("kv_w", 0)]
REPLICATED = [n for n in WEIGHT_NAMES if n not in dict(SHARDED)]


PACK_PIECE_ROWS = 16


def _rows_of(size):
    return -(-size // (PACK_COLS * PACK_PIECE_ROWS)) * PACK_PIECE_ROWS


def pack(pieces, dtype, row_multiple=PACK_ROW_MULTIPLE):
    bufs, total = [], 0
    for p in pieces:
        flat = p.astype(dtype).reshape(-1)
        rows = _rows_of(flat.size)
        bufs.append(jnp.pad(flat, (0, rows * PACK_COLS - flat.size)).reshape(rows, PACK_COLS))
        total += rows
    pad = -total % row_multiple
    if pad:
        bufs.append(jnp.zeros((pad, PACK_COLS), dtype))
    return jnp.concatenate(bufs, axis=0)


def unpack(buf, shapes):
    out, row = [], 0
    for shape in shapes:
        size = 1
        for s in shape:
            size *= s
        rows = _rows_of(size)
        out.append(buf[row:row + rows].reshape(-1)[:size].reshape(shape))
        row += rows
    return out


def _row(vec, width=None):
    vec = vec.astype(F32).reshape(1, -1)
    if width is not None and vec.shape[1] < width:
        vec = jnp.pad(vec, ((0, 0), (0, width - vec.shape[1])))
    return vec


ROW_TILE = 512
GDN_PROJ_WIDTH = 4 * GDN_WIDTH + MEM_WIDTH + LANES
GDN_Z_COL, GDN_QMEM_COL, GDN_AB_COL = 3, 4 * GDN_WIDTH // MEM_WIDTH, (4 * GDN_WIDTH + MEM_WIDTH) // LANES
FOX_QMEM_COL = FOX_WIDTH // MEM_WIDTH
KV_PAD_WIDTH = 2 * FOX_WIDTH + LANES


def rms_fwd(x, gain_row, out_dtype=BF16):
    return rowwise(f_rmsnorm, [x], [gain_row], [(x.shape[1], out_dtype)], tile=ROW_TILE, name="rms_fwd")[0]


def rms_bwd(x, gain_row, dy, dres=None):
    return rowwise_bwd(f_rmsnorm, [x], [gain_row], [dy], tile=ROW_TILE, name="rms_bwd", row_grads=[F32],
                       const_grads=[True], add=None if dres is None else {0: dres})


def _ffn_tiles(n_rows):
    return _pick(n_rows, (512, 256, 128)), _pick(FFN_HIDDEN, (1408, 256, 128))


def ffn_up_act(n, wgu):
    n_rows, d = n.shape
    tm, tn = _ffn_tiles(n_rows)
    nj = FFN_HIDDEN // tn

    def body(n_ref, wg_ref, wu_ref, gu_ref, act_ref):
        x = n_ref[...].astype(BF16)
        g = bdot.nn(x, wg_ref[...])
        u = bdot.nn(x, wu_ref[...])
        gu_ref[0] = g.astype(gu_ref.dtype)
        gu_ref[1] = u.astype(gu_ref.dtype)
        act_ref[...] = (_silu(g) * u).astype(act_ref.dtype)

    return _pcall(body, name="ffn_up_act",
                  out_shape=[jax.ShapeDtypeStruct((2, n_rows, FFN_HIDDEN), BF16), jax.ShapeDtypeStruct((n_rows, FFN_HIDDEN), BF16)],
                  grid=(nj, n_rows // tm),
                  in_specs=[pl.BlockSpec((tm, d), lambda j, i: (i, 0)), pl.BlockSpec((d, tn), lambda j, i: (0, j)),
                            pl.BlockSpec((d, tn), lambda j, i: (0, nj + j))],
                  out_specs=[pl.BlockSpec((2, tm, tn), lambda j, i: (0, i, j)), pl.BlockSpec((tm, tn), lambda j, i: (i, j))],
                  sem=("parallel", "parallel"))(n, wgu, wgu)


def ffn_down_dx_act(dh, wd, gu):
    n_rows, d = dh.shape
    tm, tn = _ffn_tiles(n_rows)

    def body(dh_ref, wd_ref, gu_ref, dgu_ref):
        dact = 0.5 * bdot.nt(dh_ref[...], wd_ref[...])
        gate, up = gu_ref[0].astype(F32), gu_ref[1].astype(F32)
        sg = _sigmoid(gate)
        dgu_ref[0] = (dact * up * (sg * (1.0 + gate * (1.0 - sg)))).astype(dgu_ref.dtype)
        dgu_ref[1] = (dact * (gate * sg)).astype(dgu_ref.dtype)

    blk = pl.BlockSpec((2, tm, tn), lambda j, i: (0, i, j))
    return _pcall(body, name="ffn_down_dx_act", out_shape=jax.ShapeDtypeStruct((2, n_rows, FFN_HIDDEN), BF16),
                  grid=(FFN_HIDDEN // tn, n_rows // tm),
                  in_specs=[pl.BlockSpec((tm, d), lambda j, i: (i, 0)), pl.BlockSpec((tn, d), lambda j, i: (j, 0)), blk],
                  out_specs=blk, sem=("parallel", "parallel"))(dh, wd, gu)


def ffn_fwd(h, gain_row, wgu, wd):
    n = rms_fwd(h, gain_row)
    gu, act = ffn_up_act(n, wgu)
    return mm(act, wd, scale=0.5, res=h, name="ffn_down"), (h, n, gu, act)


def ffn_bwd(dh, saved, gain_row, wgu, wd):
    h, n, gu, act = saved
    dgu = ffn_down_dx_act(dh, wd, gu)
    dwd = mm(act, dh, ta=True, scale=0.5, name="ffn_down_dw")
    dwgu = mm(n, dgu, ta=True, b_split=True, name="ffn_up_dw")
    dn = mm(dgu, wgu, tb=True, a_split=True, out_dtype=BF16, name="ffn_up_dx")
    dh, dgain = rms_bwd(h, gain_row, dn, dh)
    return dh, dwgu, dwd, dgain


def gdn_fwd(proj, w8, a_row, dt_row, onorm_row):
    wide = [(GDN_WIDTH, F32)] * 5
    xc = conv_fwd(proj, w8, width=3 * GDN_WIDTH)
    q, k, v, gc, beta = rowwise(f_gdn_pre, [xc, (proj, LANES, GDN_AB_COL)], [a_row, dt_row], wide, tile=GDN_CHUNK,
                                name="gdn_pre_fwd")
    u, w, qk, kt, qh, inv = rowwise(f_gdn_intra, [q, k, v, gc, beta], [], wide + wide[:1], tile=GDN_CHUNK,
                                    name="gdn_intra_fwd")
    o, vn, sin = gdn_scan_fwd(u, w, qk, kt, qh, gc)
    main = rowwise(f_gdn_post, [o, (proj, GDN_WIDTH, GDN_Z_COL)], [onorm_row], [(GDN_WIDTH, BF16)], tile=ROW_TILE,
                   name="gdn_post_fwd")[0]
    return main, (xc, q, k, v, gc, beta, inv, w, qk, kt, qh, vn, sin, o)


def gdn_bwd(dmain, proj, saved, w8, a_row, dt_row, onorm_row):
    xc, q, k, v, gc, beta, inv, w, qk, kt, qh, vn, sin, o = saved
    do, dz, donorm = rowwise_bwd(f_gdn_post, [o, (proj, GDN_WIDTH, GDN_Z_COL)], [onorm_row], [dmain], tile=ROW_TILE,
                                 name="gdn_post_bwd", row_grads=[F32, BF16], const_grads=[True])
    du, dw, dqk, dkt, dqh, dgl = gdn_scan_bwd(do, w, qk, kt, qh, gc, vn, sin)
    dq, dk, dv, dgc, dbeta = gdn_intra_bwd(q, k, v, gc, beta, inv, (du, dw, dqk, dkt, dqh), dgl)
    dxc, dab, da, ddt = rowwise_bwd(f_gdn_pre, [xc, (proj, LANES, GDN_AB_COL)], [a_row, dt_row], [dq, dk, dv, dgc, dbeta],
                                    tile=GDN_CHUNK, name="gdn_pre_bwd", row_grads=[F32, BF16], const_grads=[True, True])
    dqkv, dw8 = conv_bwd(proj, w8, dxc, width=3 * GDN_WIDTH)
    return dqkv, dz, dab, dw8, da, ddt, donorm


def mem_fwd(q, kmem, vmem):
    return rowwise(f_mem_attn, [q], [kmem, vmem], [(MEM_WIDTH, BF16)], tile=ROW_TILE, name="mem_attn_fwd")[0]


def mem_bwd(q, kmem, vmem, dout):
    return rowwise_bwd(f_mem_attn, [q], [kmem, vmem], [dout], tile=ROW_TILE, name="mem_attn_bwd", row_grads=[BF16],
                       const_grads=[True, True])


def forward_backward(xs, mems, target, P):
    depth, n_a = 4, 2
    G = {}
    mem_gain = _row(P["mem_norm"])
    mem_n = rms_fwd(mems, mem_gain)
    h = xs
    saved = []
    shared = None
    for l in range(depth):
        h0 = h
        h1, s1 = ffn_fwd(h0, _row(P["ffn1_norm"][l]), P["ffn1_w_gate_up"][l], P["ffn1_w_down"][l])
        u = rms_fwd(h1, _row(P["mix_norm"][l]))
        kvm = mm(mem_n, P["mem_w_kv"][l], name="mem_kv")
        kmem, vmem = kvm[:, :MEM_WIDTH], kvm[:, MEM_WIDTH:]
        if l < n_a:
            gp = (P["conv8"][l], _row(P["gdn_A_log"][l], LANES), _row(P["gdn_dt_bias"][l], LANES), _row(P["gdn_out_norm"][l]))
            proj = mm(u, P["gdn_w_in_pad"][l], name="gdn_in")
            main, sm = gdn_fwd(proj, *gp)
            qm = (proj, MEM_WIDTH, GDN_QMEM_COL)
        else:
            proj = mm(u, P["fox_w_in"][l - n_a], out_dtype=BF16, name="fox_in")
            main, lse = fox_fwd(proj, *shared)
            sm = (main, lse)
            qm = (proj, MEM_WIDTH, FOX_QMEM_COL)
        mo = mem_fwd(qm, kmem, vmem)
        cat = jnp.concatenate([main, mo], axis=1)
        h2 = mm(cat, P["w_out"][l], res=h1, name="mix_out")
        h3, s2 = ffn_fwd(h2, _row(P["ffn2_norm"][l]), P["ffn2_w_gate_up"][l], P["ffn2_w_down"][l])
        saved.append((s1, h1, u, kmem, vmem, proj, sm, qm, cat, s2))
        h = h3
        if l == n_a - 1:
            nkv = rms_fwd(h, _row(P["kv_norm"]))
            kv = mm(nkv, P["kv_w_pad"][:, :2 * FOX_WIDTH], out_dtype=BF16, name="fox_kv")
            f = mm(nkv, P["kv_w_pad"][:, 2 * FOX_WIDTH:], name="fox_f")
            bf_row = _row(P["kv_b_f"], LANES)
            cb, crow = fox_gate_fwd(f, bf_row)
            shared = (kv, cb, crow)
            kv_saved = (h, nkv, f, bf_row)

    part, dy = loss_head(h, _row(P["final_norm"]), target)
    dh, G["final_norm"] = rms_bwd(h, _row(P["final_norm"]), dy)

    per_layer = {n: [None] * depth for n in ("ffn1_norm", "ffn1_w_gate_up", "ffn1_w_down", "mix_norm", "ffn2_norm",
                                             "ffn2_w_gate_up", "ffn2_w_down", "w_out", "mem_w_kv")}
    gdn_g = {n: [None] * n_a for n in ("gdn_w_in_pad", "conv8", "gdn_A_log", "gdn_dt_bias", "gdn_out_norm")}
    fox_g = [None] * (depth - n_a)
    dmem_n = None
    dkv_acc = dcb_acc = None
    for l in reversed(range(depth)):
        s1, h1, u, kmem, vmem, proj, sm, qm, cat, s2 = saved[l]
        if l == n_a - 1:
            hk, nkv, f, bf_row = kv_saved
            dk, dv, dcrow = dkv_acc
            df, dbf = fox_gate_bwd(f, bf_row, dcrow, dcb_acc)
            dp = jnp.concatenate([dk.astype(BF16), dv.astype(BF16), df.astype(BF16)], axis=1)
            G["kv_w_pad"] = mm(nkv, dp, ta=True, name="fox_kv_dw")
            G["kv_b_f"] = dbf
            dnkv = mm(dp, P["kv_w_pad"], tb=True, out_dtype=BF16, name="fox_kv_dx")
            dh, G["kv_norm"] = rms_bwd(hk, _row(P["kv_norm"]), dnkv, dh)
        dh, per_layer["ffn2_w_gate_up"][l], per_layer["ffn2_w_down"][l], per_layer["ffn2_norm"][l] = ffn_bwd(
            dh, s2, _row(P["ffn2_norm"][l]), P["ffn2_w_gate_up"][l], P["ffn2_w_down"][l])
        dcat = mm(dh, P["w_out"][l], tb=True, out_dtype=BF16, name="mix_out_dx")
        per_layer["w_out"][l] = mm(cat, dh, ta=True, name="mix_out_dw")
        dqm, dkm, dvm = mem_bwd(qm, kmem, vmem, (dcat, MEM_WIDTH, FOX_QMEM_COL))
        dkvm = jnp.concatenate([dkm, dvm], axis=1)
        per_layer["mem_w_kv"][l] = mm(mem_n, dkvm, ta=True, name="mem_kv_dw")
        dmem_n = mm(dkvm, P["mem_w_kv"][l], tb=True, res=dmem_n, name="mem_kv_dx")
        dmain = (dcat, GDN_WIDTH, 0)
        if l < n_a:
            gp = (P["conv8"][l], _row(P["gdn_A_log"][l], LANES), _row(P["gdn_dt_bias"][l], LANES), _row(P["gdn_out_norm"][l]))
            dqkv, dz, dab, gdn_g["conv8"][l], gdn_g["gdn_A_log"][l], gdn_g["gdn_dt_bias"][l], gdn_g["gdn_out_norm"][l] = gdn_bwd(
                dmain, proj, sm, *gp)
            dproj = jnp.concatenate([dqkv, dz, dqm, dab], axis=1)
            gdn_g["gdn_w_in_pad"][l] = mm(u, dproj, ta=True, name="gdn_in_dw")
            du = mm(dproj, P["gdn_w_in_pad"][l], tb=True, out_dtype=BF16, name="gdn_in_dx")
        else:
            o, lse = sm
            dq, dcb_acc = fox_bwd_dq(proj, *shared, o, lse, dmain, dcb_acc)
            dkv_acc = fox_bwd_dkv(proj, *shared, o, lse, dmain, dkv_acc)
            dproj = jnp.concatenate([dq, dqm], axis=1)
            fox_g[l - n_a] = mm(u, dproj, ta=True, name="fox_in_dw")
            du = mm(dproj, P["fox_w_in"][l - n_a], tb=True, out_dtype=BF16, name="fox_in_dx")
        dh, per_layer["mix_norm"][l] = rms_bwd(h1, _row(P["mix_norm"][l]), du, dh)
        dh, per_layer["ffn1_w_gate_up"][l], per_layer["ffn1_w_down"][l], per_layer["ffn1_norm"][l] = ffn_bwd(
            dh, s1, _row(P["ffn1_norm"][l]), P["ffn1_w_gate_up"][l], P["ffn1_w_down"][l])

    (G["mem_norm"],) = rowwise_bwd(f_rmsnorm, [mems], [mem_gain], [dmem_n], tile=ROW_TILE, name="mem_norm_bwd",
                                   row_grads=[None], const_grads=[True])
    for n, v in per_layer.items():
        G[n] = jnp.stack(v)
    for n, v in gdn_g.items():
        G[n] = jnp.stack(v)
    G["fox_w_in"] = jnp.stack(fox_g)
    return part, dh, G


_GDN_O0 = 4 * GDN_WIDTH
_GDN_O1 = _GDN_O0 + 2 * GDN_HEADS
_KV_WIDTH = 2 * FOX_WIDTH + FOX_HEADS


def derived_weights(gdn_w_in, gdn_conv, kv_w=None):
    zeros = jnp.zeros(gdn_w_in.shape[:-1] + (LANES - 2 * GDN_HEADS,), gdn_w_in.dtype)
    out = dict(
        gdn_w_in_pad=jnp.concatenate([gdn_w_in[..., :_GDN_O0], gdn_w_in[..., _GDN_O1:], gdn_w_in[..., _GDN_O0:_GDN_O1], zeros], axis=-1),
        conv8=jnp.pad(gdn_conv.astype(F32), ((0, 0), (0, 8 - CONV_WIDTH), (0, 0))))
    if kv_w is not None:
        out["kv_w_pad"] = jnp.pad(kv_w, ((0, 0), (0, KV_PAD_WIDTH - _KV_WIDTH)))
    return out


def reference_layout(G):
    gp = G["gdn_w_in_pad"]
    out = dict(G)
    out["gdn_w_in"] = jnp.concatenate([gp[..., :_GDN_O0], gp[..., _GDN_O0 + MEM_WIDTH:_GDN_O0 + MEM_WIDTH + 2 * GDN_HEADS],
                                       gp[..., _GDN_O0:_GDN_O0 + MEM_WIDTH]], axis=-1)
    out["gdn_conv"] = G["conv8"][:, :CONV_WIDTH]
    out["kv_w"] = G["kv_w_pad"][:, :_KV_WIDTH]
    out["gdn_A_log"] = G["gdn_A_log"][:, 0, :GDN_HEADS]
    out["gdn_dt_bias"] = G["gdn_dt_bias"][:, 0, :GDN_HEADS]
    out["gdn_out_norm"] = G["gdn_out_norm"][:, 0, :]
    out["kv_b_f"] = G["kv_b_f"][0, :FOX_HEADS]
    for n in ("ffn1_norm", "mix_norm", "ffn2_norm"):
        out[n] = G[n][:, 0, :]
    for n in ("mem_norm", "kv_norm", "final_norm"):
        out[n] = G[n][0]
    return {n: out[n] for n in WEIGHT_NAMES}


EXCHANGED = [("ffn1_w_gate_up", 2, 0), ("ffn1_w_down", 1, 0), ("ffn2_w_gate_up", 2, 0), ("ffn2_w_down", 1, 0),
             ("gdn_w_in", 2, 0), ("fox_w_in", 1, 0), ("w_out", 1, 0), ("mem_w_kv", 1, 0), ("kv_w", 0, 1)]
GDN_IN_SHARD = (4 * GDN_WIDTH + 2 * GDN_HEADS + MEM_WIDTH) // N_CHIPS
GDN_IN_SLOT = 896


def _slab(ref, axis_slices):
    idx = [slice(None)] * len(ref.shape)
    for axis, (start, size) in axis_slices.items():
        idx[axis] = pl.ds(start, size)
    return ref.at[tuple(idx)]


def _comm_multi(body, *, name, n_in, out_shapes, scratch):
    return pl.pallas_call(body, name=name, out_shape=out_shapes, in_specs=[_HBM] * n_in, out_specs=[_HBM] * len(out_shapes),
                          scratch_shapes=scratch, compiler_params=pltpu.CompilerParams(has_side_effects=True))


def gather_shards(shards, layout):
    n = len(shards)
    fulls = [tuple(d * (N_CHIPS if a == sa else 1) for a, d in enumerate(s.shape)) for s, (sa, _) in zip(shards, layout)]

    def body(*refs):
        ins, outs = refs[:n], refs[n:2 * n]
        send_sems, recv_sems, local_sems = refs[2 * n:]
        x, y, c = lax.axis_index("x"), lax.axis_index("y"), lax.axis_index("c")
        me, sibling = (x, y, c), (x, y, 1 - c)
        chips = [(1 - x, y), (x, 1 - y), (1 - x, 1 - y)]

        def region(w, px, py, pc):
            (sa, ha), shard = layout[w], shards[w].shape
            return _slab(outs[w], {sa: ((2 * px + py) * shard[sa], shard[sa]), ha: (pc * (shard[ha] // 2), shard[ha] // 2)})

        def my_half(w):
            ha, shard = layout[w][1], shards[w].shape
            return _slab(ins[w], {ha: (c * (shard[ha] // 2), shard[ha] // 2)})

        def copy(w, k, block, to, src=None):
            return pltpu.make_async_remote_copy(
                src_ref=region(w, *block) if src is None else src, dst_ref=region(w, *block),
                send_sem=send_sems.at[7 * w + k], recv_sem=recv_sems.at[7 * w + k], device_id=to, device_id_type=MESH)

        mine, first, passed = [], [], []
        for w in range(n):
            mine.append(pltpu.make_async_copy(my_half(w), region(w, *me), local_sems.at[w]))
            mine[w].start()
            first.append([copy(w, 0, me, sibling, src=my_half(w))]
                         + [copy(w, 1 + j, me, (*chip, c), src=my_half(w)) for j, chip in enumerate(chips)])
            for cp in first[w]:
                cp.start()
            passed.append([copy(w, 4 + j, (*chip, c), sibling) for j, chip in enumerate(chips)])
        for w in range(n):
            for j, chip in enumerate(chips):
                copy(w, 1 + j, (*chip, c), me).wait_recv()
                passed[w][j].start()
        for w in range(n):
            copy(w, 0, sibling, me).wait_recv()
            for j, chip in enumerate(chips):
                copy(w, 4 + j, (*chip, 1 - c), me).wait_recv()
        for w in range(n):
            for cp in first[w] + passed[w]:
                cp.wait_send()
            mine[w].wait()

    return _comm_multi(body, name="gather_shards", n_in=n,
                       out_shapes=[jax.ShapeDtypeStruct(f, s.dtype) for f, s in zip(fulls, shards)],
                       scratch=[pltpu.SemaphoreType.DMA((7 * n,)), pltpu.SemaphoreType.DMA((7 * n,)),
                                pltpu.SemaphoreType.DMA((n,))])(*shards)


def swap_other_halves(arrays, layout):
    n = len(arrays)
    halves = [tuple(d // 2 if a == ha else d for a, d in enumerate(g.shape)) for g, (_, ha) in zip(arrays, layout)]

    def body(*refs):
        ins, outs = refs[:n], refs[n:2 * n]
        send_sems, recv_sems = refs[2 * n:]
        x, y, c = lax.axis_index("x"), lax.axis_index("y"), lax.axis_index("c")
        copies = []
        for w in range(n):
            ha, size = layout[w][1], halves[w][layout[w][1]]
            copies.append(pltpu.make_async_remote_copy(
                src_ref=_slab(ins[w], {ha: ((1 - c) * size, size)}), dst_ref=outs[w], send_sem=send_sems.at[w],
                recv_sem=recv_sems.at[w], device_id=(x, y, 1 - c), device_id_type=MESH))
            copies[w].start()
        for cp in copies:
            cp.wait()

    return _comm_multi(body, name="grad_pair_swap", n_in=n,
                       out_shapes=[jax.ShapeDtypeStruct(h, g.dtype) for h, g in zip(halves, arrays)],
                       scratch=[pltpu.SemaphoreType.DMA((n,)), pltpu.SemaphoreType.DMA((n,))])(*arrays)


def scatter_to_chips(arrays, layout):
    n = len(arrays)
    slabs = [tuple(d // N_CHIPS if a == sa else d for a, d in enumerate(p.shape)) for p, (sa, _) in zip(arrays, layout)]

    def body(*refs):
        ins, outs = refs[:n], refs[n:2 * n]
        send_sems, recv_sems, local_sems = refs[2 * n:]
        x, y, c = lax.axis_index("x"), lax.axis_index("y"), lax.axis_index("c")
        me = 2 * x + y

        def slab(w, k):
            sa, size = layout[w][0], slabs[w][layout[w][0]]
            return _slab(ins[w], {sa: (k * size, size)})

        local, copies = [], []
        for w in range(n):
            local.append(pltpu.make_async_copy(slab(w, me), outs[w].at[me], local_sems.at[w]))
            local[w].start()
            for j, (px, py) in enumerate([(1 - x, y), (x, 1 - y), (1 - x, 1 - y)]):
                copies.append(pltpu.make_async_remote_copy(
                    src_ref=slab(w, 2 * px + py), dst_ref=outs[w].at[me], send_sem=send_sems.at[3 * w + j],
                    recv_sem=recv_sems.at[3 * w + j], device_id=(px, py, c), device_id_type=MESH))
                copies[-1].start()
        for cp in copies:
            cp.wait()
        for cp in local:
            cp.wait()

    return _comm_multi(body, name="grad_all_to_all", n_in=n,
                       out_shapes=[jax.ShapeDtypeStruct((N_CHIPS,) + s, p.dtype) for s, p in zip(slabs, arrays)],
                       scratch=[pltpu.SemaphoreType.DMA((3 * n,)), pltpu.SemaphoreType.DMA((3 * n,)),
                                pltpu.SemaphoreType.DMA((n,))])(*arrays)


def join_halves(halves, layout):
    n = len(halves)
    shards = [tuple(d * 2 if a == ha else d for a, d in enumerate(h.shape)) for h, (_, ha) in zip(halves, layout)]

    def body(*refs):
        ins, outs = refs[:n], refs[n:2 * n]
        send_sems, recv_sems, local_sems = refs[2 * n:]
        x, y, c = lax.axis_index("x"), lax.axis_index("y"), lax.axis_index("c")
        local, copies = [], []
        for w in range(n):
            ha, size = layout[w][1], halves[w].shape[layout[w][1]]
            place = _slab(outs[w], {ha: (c * size, size)})
            local.append(pltpu.make_async_copy(ins[w], place, local_sems.at[w]))
            local[w].start()
            copies.append(pltpu.make_async_remote_copy(src_ref=ins[w], dst_ref=place, send_sem=send_sems.at[w],
                                                       recv_sem=recv_sems.at[w], device_id=(x, y, 1 - c), device_id_type=MESH))
            copies[w].start()
        for w in range(n):
            ha, size = layout[w][1], halves[w].shape[layout[w][1]]
            landed = _slab(outs[w], {ha: ((1 - c) * size, size)})
            pltpu.make_async_remote_copy(src_ref=ins[w], dst_ref=landed, send_sem=send_sems.at[w], recv_sem=recv_sems.at[w],
                                         device_id=(x, y, 1 - c), device_id_type=MESH).wait_recv()
            copies[w].wait_send()
            local[w].wait()

    return _comm_multi(body, name="grad_half_swap", n_in=n,
                       out_shapes=[jax.ShapeDtypeStruct(s, h.dtype) for s, h in zip(shards, halves)],
                       scratch=[pltpu.SemaphoreType.DMA((n,)), pltpu.SemaphoreType.DMA((n,)), pltpu.SemaphoreType.DMA((n,))])(*halves)


def _row_tile(rows, cols, itemsize=4, budget=2 * 1024 * 1024):
    for t in (1024, 512, 256, 128, 64, 32, 16):
        if rows % t == 0 and t * cols * itemsize <= budget:
            return t
    return rows


def add_own_half(full, recv, half_axis, c_arr, *, name):
    a0, a1, a2 = recv.shape
    tile = _row_tile(a1, a2)

    def body(c_ref, f_ref, r_ref, o_ref):
        o_ref[...] = (f_ref[...] + r_ref[...]).astype(o_ref.dtype)

    if half_axis == 0:
        f_spec = pl.BlockSpec((None, tile, a2), lambda i, j, c_ref: (c_ref[0] * a0 + i, j, 0))
    else:
        f_spec = pl.BlockSpec((None, tile, a2), lambda i, j, c_ref: (i, c_ref[0] * (a1 // tile) + j, 0))
    blk = pl.BlockSpec((None, tile, a2), lambda i, j, c_ref: (i, j, 0))
    spec = pltpu.PrefetchScalarGridSpec(num_scalar_prefetch=1, grid=(a0, a1 // tile), in_specs=[f_spec, blk], out_specs=blk)
    return pl.pallas_call(body, name=name, out_shape=jax.ShapeDtypeStruct(recv.shape, BF16), grid_spec=spec,
                          compiler_params=pltpu.CompilerParams(vmem_limit_bytes=VMEM_LIMIT_BYTES,
                                                               dimension_semantics=("parallel", "parallel")))(c_arr, full, recv)


def sum_slots(q, *, name):
    _, a0, a1, a2 = q.shape
    tile = _row_tile(a1, a2, budget=1024 * 1024)

    def body(q_ref, o_ref):
        total = q_ref[0].astype(F32)
        for k in range(1, N_CHIPS):
            total = total + q_ref[k].astype(F32)
        o_ref[...] = total

    return _pcall(body, name=name, out_shape=jax.ShapeDtypeStruct((a0, a1, a2), F32), grid=(a0, a1 // tile),
                  in_specs=[pl.BlockSpec((N_CHIPS, None, tile, a2), lambda i, j: (0, i, j, 0))],
                  out_specs=pl.BlockSpec((None, tile, a2), lambda i, j: (i, j, 0)), sem=("parallel", "parallel"))(q)


def gather_weights(W):
    shards = []
    for name, _, _ in EXCHANGED:
        w = W[name].astype(BF16)
        if name == "gdn_w_in":
            w = jnp.pad(w, ((0, 0), (0, 0), (0, GDN_IN_SLOT - GDN_IN_SHARD)))
        if name == "kv_w":
            w = jnp.pad(w, ((0, 0), (0, KV_PAD_WIDTH - _KV_WIDTH)))[None]
        shards.append(w)
    fulls = dict(zip([n for n, _, _ in EXCHANGED], gather_shards(shards, [(sa, ha) for _, sa, ha in EXCHANGED])))
    slots = fulls["gdn_w_in"]
    fulls["gdn_w_in"] = jnp.concatenate([slots[..., k * GDN_IN_SLOT:k * GDN_IN_SLOT + GDN_IN_SHARD] for k in range(N_CHIPS)], axis=-1)
    fulls["kv_w_pad"] = fulls.pop("kv_w").reshape(-1, KV_PAD_WIDTH)
    conv = pack([W["gdn_conv"]], F32, row_multiple=8)
    conv_all = all_gather8(conv, name="gather_conv").reshape(N_DEV, conv.shape[0], PACK_COLS)
    fulls["gdn_conv"] = jnp.concatenate([unpack(conv_all[2 * k], [W["gdn_conv"].shape])[0] for k in range(N_CHIPS)], axis=-1)
    return fulls


def reduce_gradients(G):
    layout = [(sa, ha) for _, sa, ha in EXCHANGED]
    c_arr = lax.axis_index("c").astype(jnp.int32).reshape(1)
    received = swap_other_halves(G, layout)
    pairs = [add_own_half(g, r, ha, c_arr, name="grad_pair_sum") for g, r, (_, ha) in zip(G, received, layout)]
    slots = scatter_to_chips(pairs, layout)
    halves = [sum_slots(q, name="grad_chip_sum") for q in slots]
    return join_halves(halves, layout)


def allreduce_small(G, names):
    packed = pack([G[n] for n in names], F32, row_multiple=8)
    gathered = all_gather8(packed, name="gather_small_grads").reshape(N_DEV, packed.shape[0], PACK_COLS)
    total = sum_leading(gathered, name="sum_small_grads")
    return dict(zip(names, unpack(total, [G[n].shape for n in names])))


def kernel(x, mem, *rest):
    n_w = len(WEIGHT_NAMES)
    W = dict(zip(WEIGHT_NAMES, rest[:n_w]))
    target = rest[n_w]
    M = dict(zip(WEIGHT_NAMES, rest[n_w + 1:2 * n_w + 1]))
    V = dict(zip(WEIGHT_NAMES, rest[2 * n_w + 1:3 * n_w + 1]))

    full = gather_weights(W)
    P = {n: W[n] for n in REPLICATED}
    P.update({n: full[n] for n in ("ffn1_w_gate_up", "ffn1_w_down", "ffn2_w_gate_up", "ffn2_w_down", "fox_w_in", "w_out",
                                   "mem_w_kv", "kv_w_pad")})
    derived = derived_weights(full["gdn_w_in"], full["gdn_conv"])
    P.update(gdn_w_in_pad=derived["gdn_w_in_pad"], conv8=derived["conv8"])

    part, dx, G = forward_backward(x[0], mem[0], target[0], P)
    loss = lax.psum(0.5 / x.shape[-1] * jnp.sum(part), ("x", "y", "c"))

    ref = reference_layout(G)
    exchange = {n: ref[n] for n, _, _ in EXCHANGED}
    exchange["gdn_w_in"] = jnp.concatenate(
        [jnp.pad(ref["gdn_w_in"][..., k * GDN_IN_SHARD:(k + 1) * GDN_IN_SHARD], ((0, 0), (0, 0), (0, GDN_IN_SLOT - GDN_IN_SHARD)))
         for k in range(N_CHIPS)], axis=-1)
    exchange["kv_w"] = G["kv_w_pad"].reshape(N_CHIPS, -1, KV_PAD_WIDTH)
    shards = dict(zip([n for n, _, _ in EXCHANGED], reduce_gradients([exchange[n] for n, _, _ in EXCHANGED])))
    shards["gdn_w_in"] = shards["gdn_w_in"][..., :GDN_IN_SHARD]
    shards["kv_w"] = shards["kv_w"][0, :, :_KV_WIDTH]
    grads = allreduce_small(ref, REPLICATED + ["gdn_conv"])
    conv_cols = W["gdn_conv"].shape[-1]
    chip = 2 * lax.axis_index("x") + lax.axis_index("y")
    grads["gdn_conv"] = lax.dynamic_slice_in_dim(grads["gdn_conv"], chip * conv_cols, conv_cols, axis=2)
    grads.update(shards)

    outs = {n: adamw(W[n], grads[n], M[n], V[n], name="adamw_" + n) for n in WEIGHT_NAMES}
    return (loss, dx[None], *[grads[n] for n in WEIGHT_NAMES], *[outs[n][0] for n in WEIGHT_NAMES],
            *[outs[n][1] for n in WEIGHT_NAMES], *[outs[n][2] for n in WEIGHT_NAMES])
```

```python
import jax
import jax.numpy as jnp
from jax import lax
from jax.experimental import pallas as pl
from jax.experimental.pallas import tpu as pltpu

F32, BF16 = jnp.float32, jnp.bfloat16
HI = lax.Precision.HIGHEST
MESH = pl.DeviceIdType.MESH

VMEM_LIMIT_BYTES = 48 * 1024 * 1024
LANES = 128
EPS = 1e-6
NEG_INF = -1e30

D_MODEL = 1024
HEAD_DIM = 128
GDN_HEADS = 6
GDN_WIDTH = GDN_HEADS * HEAD_DIM
FOX_HEADS = 6
FOX_WIDTH = FOX_HEADS * HEAD_DIM
MEM_HEADS = 4
MEM_HEAD_DIM = 64
MEM_WIDTH = MEM_HEADS * MEM_HEAD_DIM
FFN_HIDDEN = 2816
CONV_WIDTH = 4
GDN_CHUNK = 128
N_CHIPS = 4
N_DEV = 8

ADAM_LR, ADAM_B1, ADAM_B2, ADAM_EPS, ADAM_WD, ADAM_STEP = 0.001, 0.9, 0.999, 1e-08, 0.01, 10


def _pcall(body, *, name, out_shape, grid=(), in_specs=None, out_specs=None, scratch=(), sem=None):
    params = dict(vmem_limit_bytes=VMEM_LIMIT_BYTES)
    if sem is not None:
        params["dimension_semantics"] = sem
    kw = dict(grid=grid, in_specs=in_specs, out_specs=out_specs) if grid else {}
    return pl.pallas_call(body, name=name, out_shape=out_shape, scratch_shapes=list(scratch),
                          compiler_params=pltpu.CompilerParams(**params), **kw)


def _pick(n, cands):
    for c in cands:
        if n % c == 0:
            return c
    return n


def _make_dot(dtype, precision):
    def raw(a, b, dims):
        return lax.dot_general(a.astype(dtype), b.astype(dtype), (dims, ((), ())),
                               precision=precision, preferred_element_type=F32)

    @jax.custom_vjp
    def dot(a, b):
        return raw(a, b, ((1,), (0,)))

    def fwd(a, b):
        return dot(a, b), (a, b)

    def bwd(resid, ct):
        a, b = resid
        return raw(ct, b, ((1,), (1,))).astype(a.dtype), raw(a, ct, ((0,), (0,))).astype(b.dtype)

    dot.defvjp(fwd, bwd)
    dot.nn = lambda a, b: raw(a, b, ((1,), (0,)))
    dot.nt = lambda a, b: raw(a, b, ((1,), (1,)))
    dot.tn = lambda a, b: raw(a, b, ((0,), (0,)))
    return dot


bdot = _make_dot(BF16, None)
fdot = _make_dot(F32, HI)
idot = _make_dot(F32, lax.Precision.HIGH)
sdot = idot


def _sigmoid(x):
    return jax.nn.sigmoid(x)


def _silu(x):
    return x * _sigmoid(x)


def _softplus(x):
    return jnp.maximum(x, 0.0) + jnp.log(1.0 + jnp.exp(-jnp.abs(x)))


def _log_sigmoid(x):
    return -_softplus(-x)


def _iota2(shape, dim):
    return lax.broadcasted_iota(jnp.int32, shape, dim)


def mm(a, b, *, ta=False, tb=False, a_split=False, b_split=False, out_dtype=F32, scale=1.0, res=None, name):
    assert not (a_split and ta) and not (b_split and tb)
    (K, M) = a.shape if ta else ((2 * a.shape[2], a.shape[1]) if a_split else a.shape[::-1])
    (N, Kb) = b.shape if tb else ((2 * b.shape[2], b.shape[1]) if b_split else b.shape[::-1])
    assert K == Kb, (a.shape, b.shape, ta, tb)
    tm = _pick(M, (1024, 1408, 512, 256, 128))
    tn = _pick(N, (1024, 1408, 1152, 1664, 768, 512, 384, 256, 128))
    tk = _pick(K, (1024, 512, 256, 128)) if ta else _pick(K, (1024, 1408, 1152, 1664, 512, 256, 128))
    assert not a_split or (K // 2) % tk == 0
    assert not b_split or (N // 2) % tn == 0
    nk = K // tk
    dims = (((0 if ta else 1,), (1 if tb else 0,)), ((), ()))

    def body(a_ref, b_ref, *rest):
        o_ref, acc = rest[-2], rest[-1]
        k = pl.program_id(2)

        @pl.when(k == 0)
        def _():
            acc[...] = jnp.zeros_like(acc)

        acc[...] += lax.dot_general(a_ref[...].astype(BF16), b_ref[...].astype(BF16), dims,
                                    preferred_element_type=F32)

        @pl.when(k == nk - 1)
        def _():
            out = acc[...] * scale
            if res is not None:
                out = out + rest[0][...].astype(F32)
            o_ref[...] = out.astype(o_ref.dtype)

    a_spec = pl.BlockSpec((tk, tm), lambda i, j, k: (k, i)) if ta else pl.BlockSpec((tm, tk), lambda i, j, k: (i, k))
    b_spec = pl.BlockSpec((tn, tk), lambda i, j, k: (j, k)) if tb else pl.BlockSpec((tk, tn), lambda i, j, k: (k, j))
    if a_split:
        per_half = K // 2 // tk
        a_spec = pl.BlockSpec((None, tm, tk), lambda i, j, k: (k // per_half, i, k % per_half))
    if b_split:
        per_half = N // 2 // tn
        b_spec = pl.BlockSpec((None, tk, tn), lambda i, j, k: (j // per_half, k, j % per_half))
    o_spec = pl.BlockSpec((tm, tn), lambda i, j, k: (i, j))
    ins, specs = [a, b], [a_spec, b_spec]
    if res is not None:
        ins.append(res)
        specs.append(o_spec)
    return _pcall(body, name=name, out_shape=jax.ShapeDtypeStruct((M, N), out_dtype),
                  grid=(M // tm, N // tn, nk), in_specs=specs, out_specs=o_spec,
                  scratch=[pltpu.VMEM((tm, tn), F32)], sem=("parallel", "parallel", "arbitrary"))(*ins)


def _row_spec(r, tile):
    if isinstance(r, tuple):
        arr, width, col = r
        return arr, pl.BlockSpec((tile, width), lambda i, col=col: (i, col))
    return r, pl.BlockSpec((tile, r.shape[1]), lambda i: (i, 0))


def _const_spec(c):
    return pl.BlockSpec(c.shape, lambda i: (0,) * c.ndim)


def rowwise(fn, rows, consts, outs, *, tile, name):
    arrs, specs = zip(*[_row_spec(r, tile) for r in rows])
    n_rows = arrs[0].shape[0]
    tile = min(tile, n_rows)
    n_in = len(rows) + len(consts)

    def body(*refs):
        res = fn(*[r[...] for r in refs[:n_in]])
        for o_ref, o in zip(refs[n_in:], res):
            o_ref[...] = o.astype(o_ref.dtype)

    arrs, specs = zip(*[_row_spec(r, tile) for r in rows])
    return _pcall(body, name=name,
                  out_shape=[jax.ShapeDtypeStruct((n_rows, w), dt) for w, dt in outs],
                  grid=(n_rows // tile,),
                  in_specs=list(specs) + [_const_spec(c) for c in consts],
                  out_specs=[pl.BlockSpec((tile, w), lambda i: (i, 0)) for w, _ in outs],
                  sem=("parallel",))(*arrs, *consts)


def rowwise_bwd(fn, rows, consts, cts, *, tile, name, row_grads, const_grads, add=None):
    arrs, _ = zip(*[_row_spec(r, tile) for r in rows])
    n_rows = arrs[0].shape[0]
    tile = min(tile, n_rows)
    arrs, specs = zip(*[_row_spec(r, tile) for r in rows])
    ct_arrs, ct_specs = zip(*[_row_spec(c, tile) for c in cts])
    add = add or {}
    add_idx = sorted(add)
    add_arrs, add_specs = (zip(*[_row_spec(add[i], tile) for i in add_idx]) if add_idx else ((), ()))
    nr, nc, nct, na = len(rows), len(consts), len(cts), len(add_idx)
    want_rows = [i for i, d in enumerate(row_grads) if d is not None]
    want_consts = [i for i, w in enumerate(const_grads) if w]

    def body(*refs):
        row_v = [r[...] for r in refs[:nr]]
        const_v = [r[...] for r in refs[nr:nr + nc]]
        ct_v = [r[...] for r in refs[nr + nc:nr + nc + nct]]
        add_v = {i: refs[nr + nc + nct + j][...] for j, i in enumerate(add_idx)}
        out_refs = refs[nr + nc + nct + na:]
        res, vjp = jax.vjp(fn, *row_v, *const_v)
        grads = vjp(tuple(c.astype(o.dtype) for c, o in zip(ct_v, res)))
        for o_ref, i in zip(out_refs, want_rows):
            g = grads[i].astype(F32)
            if i in add_v:
                g = g + add_v[i].astype(F32)
            o_ref[...] = g.astype(o_ref.dtype)
        first = pl.program_id(0) == 0
        for o_ref, i in zip(out_refs[len(want_rows):], want_consts):
            g = grads[nr + i].astype(F32)

            @pl.when(first)
            def _(o_ref=o_ref, g=g):
                o_ref[...] = g

            @pl.when(jnp.logical_not(first))
            def _(o_ref=o_ref, g=g):
                o_ref[...] += g

    def width(r):
        return r[1] if isinstance(r, tuple) else r.shape[1]

    out_shape = [jax.ShapeDtypeStruct((n_rows, width(rows[i])), row_grads[i]) for i in want_rows]
    out_shape += [jax.ShapeDtypeStruct(consts[i].shape, F32) for i in want_consts]
    out_specs = [pl.BlockSpec((tile, width(rows[i])), lambda i_: (i_, 0)) for i in want_rows]
    out_specs += [_const_spec(consts[i]) for i in want_consts]
    return _pcall(body, name=name, out_shape=out_shape, grid=(n_rows // tile,),
                  in_specs=list(specs) + [_const_spec(c) for c in consts] + list(ct_specs) + list(add_specs),
                  out_specs=out_specs, sem=("arbitrary",))(*arrs, *consts, *ct_arrs, *add_arrs)


def f_rmsnorm(x, g):
    x = x.astype(F32)
    return (x * lax.rsqrt(jnp.mean(x * x, axis=-1, keepdims=True) + EPS) * g,)


def _head_sel(first_lane):
    r, c = _iota2((LANES, GDN_WIDTH), 0), _iota2((LANES, GDN_WIDTH), 1)
    return (r == c // HEAD_DIM + first_lane).astype(F32)


def _tri(n, strict=False):
    r, c = _iota2((n, n), 0), _iota2((n, n), 1)
    return r > c if strict else r >= c


def f_gdn_pre(xc, ab, a_log, dt_bias):
    s = _silu(xc.astype(F32))
    qs, ks = [], []
    for h in range(GDN_HEADS):
        qh = s[:, h * HEAD_DIM:(h + 1) * HEAD_DIM]
        kh = s[:, GDN_WIDTH + h * HEAD_DIM:GDN_WIDTH + (h + 1) * HEAD_DIM]
        qs.append(qh * lax.rsqrt(jnp.sum(qh * qh, axis=-1, keepdims=True) + EPS) * (HEAD_DIM ** -0.5))
        ks.append(kh * lax.rsqrt(jnp.sum(kh * kh, axis=-1, keepdims=True) + EPS))
    q, k = jnp.concatenate(qs, axis=1), jnp.concatenate(ks, axis=1)
    v = s[:, 2 * GDN_WIDTH:]
    ab = ab.astype(F32)
    g = -jnp.exp(a_log) * _softplus(ab + dt_bias)
    gc = fdot(_tri(GDN_CHUNK).astype(F32), fdot(g, _head_sel(0)))
    beta = fdot(_sigmoid(ab), _head_sel(GDN_HEADS))
    return q, k, v, gc, beta


def _unit_lower_inverse(neg_lower):
    C = neg_lower.shape[0]
    inv = (_iota2((C, C), 0) == _iota2((C, C), 1)).astype(F32) + neg_lower
    power = neg_lower
    for _ in range(6):
        power = idot.nn(power, power)
        inv = inv + idot.nn(power, inv)
    return inv


@jax.custom_vjp
def _solve_with_inverse(inv, neg_lower, rhs):
    return idot.nn(inv, rhs)


def _solve_fwd(inv, neg_lower, rhs):
    x = idot.nn(inv, rhs)
    return x, (inv, x)


def _solve_bwd(resid, ct):
    inv, x = resid
    d_rhs = idot.tn(inv, ct)
    return jnp.zeros_like(inv), idot.nt(d_rhs, x), d_rhs


_solve_with_inverse.defvjp(_solve_fwd, _solve_bwd)


def f_gdn_intra(q, k, v, gc, beta, inv=None):
    C = GDN_CHUNK
    causal, strict = _tri(C), _tri(C, strict=True)
    is_last = _iota2((C, HEAD_DIM), 0) == C - 1
    outs = [[] for _ in range(6 if inv is None else 5)]
    for h in range(GDN_HEADS):
        sl = slice(h * HEAD_DIM, (h + 1) * HEAD_DIM)
        qh, kh, vh, gh, bh = q[:, sl], k[:, sl], v[:, sl], gc[:, sl], beta[:, sl]
        gdiff = gh - gh.T
        decay = jnp.where(causal, jnp.exp(jnp.where(causal, gdiff, 0.0)), 0.0)
        kb = kh * bh
        neg_lower = jnp.where(strict, -(idot(kb, kh.T) * decay), 0.0)
        rhs = jnp.concatenate([vh * bh, kb * jnp.exp(gh)], axis=1)
        if inv is None:
            inv_h = _unit_lower_inverse(neg_lower)
            sol = idot.nn(inv_h, rhs)
        else:
            sol = _solve_with_inverse(inv[:, sl], neg_lower, rhs)
        qk = jnp.where(causal, idot(qh, kh.T) * decay, 0.0)
        g_last = jnp.sum(jnp.where(is_last, gh, 0.0), axis=0, keepdims=True)
        vals = (sol[:, :HEAD_DIM], sol[:, HEAD_DIM:], qk, kh * jnp.exp(g_last - gh), qh * jnp.exp(gh))
        for lst, val in zip(outs, vals + ((inv_h,) if inv is None else ())):
            lst.append(val)
    return tuple(jnp.concatenate(o, axis=1) for o in outs)


def f_gdn_post(o, z, gain):
    z = z.astype(F32)
    parts = []
    for h in range(GDN_HEADS):
        oh = o[:, h * HEAD_DIM:(h + 1) * HEAD_DIM]
        parts.append(oh * lax.rsqrt(jnp.mean(oh * oh, axis=-1, keepdims=True) + EPS) * gain)
    return (jnp.concatenate(parts, axis=1) * _silu(z),)


def f_mem_attn(q, k, v):
    q = q.astype(F32)
    lane_head = _iota2((1, MEM_WIDTH), 1) // MEM_HEAD_DIM
    out = jnp.zeros(q.shape, F32)
    kt = k.astype(F32).T
    for h in range(MEM_HEADS):
        mask = (lane_head == h).astype(F32)
        logits = bdot(q * mask, kt) * (MEM_HEAD_DIM ** -0.5)
        p = jnp.exp(logits - jnp.max(logits, axis=-1, keepdims=True))
        p = p / jnp.sum(p, axis=-1, keepdims=True)
        out = out + bdot(p, v) * mask
    return (out,)


def f_loss(y, t):
    d = y - t
    return (d * d,)


def conv_fwd(proj, w8, *, width, tile=512):
    n_rows = proj.shape[0]
    tile = min(tile, n_rows)

    def body(x_ref, halo_ref, w_ref, o_ref):
        i = pl.program_id(0)
        halo = jnp.where(i > 0, halo_ref[...].astype(F32), 0.0)
        xs = jnp.concatenate([halo, x_ref[...].astype(F32)], axis=0)
        acc = xs[8:] * w_ref[3:4, :]
        for j in range(CONV_WIDTH - 1):
            acc = acc + pltpu.roll(xs, CONV_WIDTH - 1 - j, 0)[8:] * w_ref[j:j + 1, :]
        o_ref[...] = acc

    return _pcall(body, name="gdn_conv_fwd", out_shape=jax.ShapeDtypeStruct((n_rows, width), F32),
                  grid=(n_rows // tile,),
                  in_specs=[pl.BlockSpec((tile, width), lambda i: (i, 0)),
                            pl.BlockSpec((8, width), lambda i: (jnp.maximum(i * (tile // 8) - 1, 0), 0)),
                            pl.BlockSpec((8, width), lambda i: (0, 0))],
                  out_specs=pl.BlockSpec((tile, width), lambda i: (i, 0)), sem=("parallel",))(proj, proj, w8)


def conv_bwd(proj, w8, dy, *, width, tile=512):
    n_rows = proj.shape[0]
    tile = min(tile, n_rows)
    n = n_rows // tile

    def body(x_ref, xhalo_ref, w_ref, dy_ref, dyhalo_ref, dx_ref, dw_ref):
        i = pl.program_id(0)
        dy = dy_ref[...]
        after = jnp.where(i < n - 1, dyhalo_ref[...], 0.0)
        ds = jnp.concatenate([dy, after], axis=0)
        dx = dy * w_ref[3:4, :]
        for j in range(CONV_WIDTH - 1):
            shift = CONV_WIDTH - 1 - j
            dx = dx + pltpu.roll(ds, tile + 8 - shift, 0)[:tile] * w_ref[j:j + 1, :]
        dx_ref[...] = dx.astype(dx_ref.dtype)
        halo = jnp.where(i > 0, xhalo_ref[...].astype(F32), 0.0)
        xs = jnp.concatenate([halo, x_ref[...].astype(F32)], axis=0)
        rows = [jnp.sum(dy * pltpu.roll(xs, CONV_WIDTH - 1 - j, 0)[8:], axis=0, keepdims=True)
                for j in range(CONV_WIDTH - 1)]
        rows.append(jnp.sum(dy * xs[8:], axis=0, keepdims=True))
        dw = jnp.concatenate(rows + [jnp.zeros((8 - CONV_WIDTH, width), F32)], axis=0)

        @pl.when(i == 0)
        def _():
            dw_ref[...] = dw

        @pl.when(i > 0)
        def _():
            dw_ref[...] += dw

    t8 = tile // 8
    return _pcall(body, name="gdn_conv_bwd",
                  out_shape=[jax.ShapeDtypeStruct((n_rows, width), BF16), jax.ShapeDtypeStruct((8, width), F32)],
                  grid=(n,),
                  in_specs=[pl.BlockSpec((tile, width), lambda i: (i, 0)),
                            pl.BlockSpec((8, width), lambda i: (jnp.maximum(i * t8 - 1, 0), 0)),
                            pl.BlockSpec((8, width), lambda i: (0, 0)),
                            pl.BlockSpec((tile, width), lambda i: (i, 0)),
                            pl.BlockSpec((8, width), lambda i: (jnp.minimum((i + 1) * t8, n * t8 - 1), 0))],
                  out_specs=[pl.BlockSpec((tile, width), lambda i: (i, 0)), pl.BlockSpec((8, width), lambda i: (0, 0))],
                  sem=("arbitrary",))(proj, proj, w8, dy, dy)


def gdn_scan_fwd(u, w, qk, kt, qh, gc):
    n_rows = u.shape[0]
    C, n = GDN_CHUNK, u.shape[0] // GDN_CHUNK

    def body(u_ref, w_ref, qk_ref, kt_ref, qh_ref, gc_ref, o_ref, vn_ref, sin_ref, st):
        @pl.when(pl.program_id(0) == 0)
        def _():
            st[...] = jnp.zeros_like(st)

        sin_ref[0] = st[...]
        for h in range(GDN_HEADS):
            sl = slice(h * HEAD_DIM, (h + 1) * HEAD_DIM)
            s = st[sl, :]
            v_new = u_ref[:, sl] - sdot(w_ref[:, sl], s)
            o_ref[:, sl] = sdot(qh_ref[:, sl], s) + sdot(qk_ref[:, sl], v_new)
            vn_ref[:, sl] = v_new
            st[sl, :] = s * jnp.exp(gc_ref[C - 1:C, sl]) + sdot.tn(kt_ref[:, sl], v_new)

    blk = pl.BlockSpec((C, GDN_WIDTH), lambda i: (i, 0))
    return _pcall(body, name="gdn_scan_fwd",
                  out_shape=[jax.ShapeDtypeStruct((n_rows, GDN_WIDTH), F32), jax.ShapeDtypeStruct((n_rows, GDN_WIDTH), F32),
                             jax.ShapeDtypeStruct((n, GDN_WIDTH, HEAD_DIM), F32)],
                  grid=(n,), in_specs=[blk] * 6,
                  out_specs=[blk, blk, pl.BlockSpec((1, GDN_WIDTH, HEAD_DIM), lambda i: (i, 0, 0))],
                  scratch=[pltpu.VMEM((GDN_WIDTH, HEAD_DIM), F32)], sem=("arbitrary",))(u, w, qk, kt, qh, gc)


def gdn_scan_bwd(do, w, qk, kt, qh, gc, vn, sin):
    n_rows = do.shape[0]
    C, n = GDN_CHUNK, do.shape[0] // GDN_CHUNK

    def body(do_ref, w_ref, qk_ref, kt_ref, qh_ref, gc_ref, vn_ref, sin_ref,
             du_ref, dw_ref, dqk_ref, dkt_ref, dqh_ref, dgl_ref, dst):
        @pl.when(pl.program_id(0) == 0)
        def _():
            dst[...] = jnp.zeros_like(dst)

        for h in range(GDN_HEADS):
            sl = slice(h * HEAD_DIM, (h + 1) * HEAD_DIM)
            s, ds_out, d_o, v_new = sin_ref[0, sl, :], dst[sl, :], do_ref[:, sl], vn_ref[:, sl]
            e = jnp.exp(gc_ref[C - 1:C, sl])
            dvn = sdot.tn(qk_ref[:, sl], d_o) + sdot(kt_ref[:, sl], ds_out)
            du_ref[:, sl] = dvn
            dw_ref[:, sl] = -sdot.nt(dvn, s)
            dqk_ref[:, sl] = sdot.nt(d_o, v_new)
            dkt_ref[:, sl] = sdot.nt(v_new, ds_out)
            dqh_ref[:, sl] = sdot.nt(d_o, s)
            dgl = jnp.sum(ds_out * s, axis=0, keepdims=True) * e
            dgl_ref[:, sl] = jnp.broadcast_to(dgl, (8, HEAD_DIM))
            dst[sl, :] = sdot.tn(qh_ref[:, sl], d_o) + e * ds_out - sdot.tn(w_ref[:, sl], dvn)

    blk = pl.BlockSpec((C, GDN_WIDTH), lambda i: (n - 1 - i, 0))
    row = jax.ShapeDtypeStruct((n_rows, GDN_WIDTH), F32)
    return _pcall(body, name="gdn_scan_bwd",
                  out_shape=[row] * 5 + [jax.ShapeDtypeStruct((n * 8, GDN_WIDTH), F32)],
                  grid=(n,), in_specs=[blk] * 7 + [pl.BlockSpec((1, GDN_WIDTH, HEAD_DIM), lambda i: (n - 1 - i, 0, 0))],
                  out_specs=[blk] * 5 + [pl.BlockSpec((8, GDN_WIDTH), lambda i: (n - 1 - i, 0))],
                  scratch=[pltpu.VMEM((GDN_WIDTH, HEAD_DIM), F32)], sem=("arbitrary",))(do, w, qk, kt, qh, gc, vn, sin)


def gdn_intra_bwd(q, k, v, gc, beta, inv, cts, dgl):
    n_rows = q.shape[0]
    C = GDN_CHUNK

    def body(*refs):
        ins = [r[...] for r in refs[:5]]
        inv_v = refs[5][...]
        ct = tuple(r[...] for r in refs[6:11])
        dgl_v = refs[11][...]
        _, vjp = jax.vjp(lambda *a: f_gdn_intra(*a, inv=inv_v), *ins)
        grads = list(vjp(ct))
        last = _iota2((C, GDN_WIDTH), 0) == C - 1
        grads[3] = grads[3] + jnp.where(last, jnp.broadcast_to(dgl_v[0:1, :], (C, GDN_WIDTH)), 0.0)
        for o_ref, g in zip(refs[12:], grads):
            o_ref[...] = g

    blk = pl.BlockSpec((C, GDN_WIDTH), lambda i: (i, 0))
    return _pcall(body, name="gdn_intra_bwd", out_shape=[jax.ShapeDtypeStruct((n_rows, GDN_WIDTH), F32)] * 5,
                  grid=(n_rows // C,), in_specs=[blk] * 11 + [pl.BlockSpec((8, GDN_WIDTH), lambda i: (i, 0))],
                  out_specs=[blk] * 5, sem=("parallel",))(q, k, v, gc, beta, inv, *cts, dgl)


def fox_gate_fwd(f, b_f):
    n_rows = f.shape[0]
    T = LANES

    def body(f_ref, b_ref, cb_ref, crow_ref, carry):
        @pl.when(pl.program_id(0) == 0)
        def _():
            carry[...] = jnp.zeros_like(carry)

        c = fdot(_tri(T).astype(F32), _log_sigmoid(f_ref[...] + b_ref[...])) + carry[...]
        carry[...] = c[T - 1:T, :]
        cb_ref[...] = fdot(c, _head_sel(0))
        ct = c.T
        for h in range(FOX_HEADS):
            crow_ref[h] = ct[h:h + 1, :]

    return _pcall(body, name="fox_gate_fwd",
                  out_shape=[jax.ShapeDtypeStruct((n_rows, FOX_WIDTH), F32), jax.ShapeDtypeStruct((FOX_HEADS, 1, n_rows), F32)],
                  grid=(n_rows // T,),
                  in_specs=[pl.BlockSpec((T, LANES), lambda i: (i, 0)), pl.BlockSpec((1, LANES), lambda i: (0, 0))],
                  out_specs=[pl.BlockSpec((T, FOX_WIDTH), lambda i: (i, 0)), pl.BlockSpec((FOX_HEADS, 1, T), lambda i: (0, 0, i))],
                  scratch=[pltpu.VMEM((1, LANES), F32)], sem=("arbitrary",))(f, b_f)


def fox_gate_bwd(f, b_f, dcrow, dcb):
    n_rows = f.shape[0]
    T = LANES
    n = n_rows // T

    def body(f_ref, b_ref, dc_ref, dcb_ref, df_ref, db_ref, carry):
        i = pl.program_id(0)

        @pl.when(i == 0)
        def _():
            carry[...] = jnp.zeros_like(carry)

        rows = [dc_ref[h] for h in range(FOX_HEADS)] + [jnp.zeros((T - FOX_HEADS, T), F32)]
        first_lane = (_iota2((FOX_WIDTH, LANES), 0) == _iota2((FOX_WIDTH, LANES), 1) * HEAD_DIM).astype(F32)
        dc = jnp.concatenate(rows, axis=0).T + fdot(dcb_ref[...], first_lane)
        dlog = fdot.tn(_tri(T).astype(F32), dc) + carry[...]
        carry[...] = dlog[0:1, :]
        df = dlog * (1.0 - _sigmoid(f_ref[...] + b_ref[...]))
        df_ref[...] = df
        db = jnp.sum(df, axis=0, keepdims=True)

        @pl.when(i == 0)
        def _():
            db_ref[...] = db

        @pl.when(i > 0)
        def _():
            db_ref[...] += db

    return _pcall(body, name="fox_gate_bwd",
                  out_shape=[jax.ShapeDtypeStruct((n_rows, LANES), F32), jax.ShapeDtypeStruct((1, LANES), F32)],
                  grid=(n,),
                  in_specs=[pl.BlockSpec((T, LANES), lambda i: (n - 1 - i, 0)), pl.BlockSpec((1, LANES), lambda i: (0, 0)),
                            pl.BlockSpec((FOX_HEADS, 1, T), lambda i: (0, 0, n - 1 - i)),
                            pl.BlockSpec((T, FOX_WIDTH), lambda i: (n - 1 - i, 0))],
                  out_specs=[pl.BlockSpec((T, LANES), lambda i: (n - 1 - i, 0)), pl.BlockSpec((1, LANES), lambda i: (0, 0))],
                  scratch=[pltpu.VMEM((1, LANES), F32)], sem=("arbitrary",))(f, b_f, dcrow, dcb)


def _fox_tile(n_rows):
    return min(512, n_rows)


def _fox_pairs(n, query_major):
    pairs = [(i, j) for i in range(n) for j in range(i + 1)] if query_major else [(i, j) for j in range(n) for i in range(j, n)]
    return jnp.asarray([p[0] for p in pairs], jnp.int32), jnp.asarray([p[1] for p in pairs], jnp.int32)


def _pcall_tables(body, *, name, out_shape, grid, tables, in_specs, out_specs, scratch, sem):
    spec = pltpu.PrefetchScalarGridSpec(num_scalar_prefetch=len(tables), grid=grid, in_specs=in_specs, out_specs=out_specs,
                                        scratch_shapes=list(scratch))
    return pl.pallas_call(body, name=name, out_shape=out_shape, grid_spec=spec,
                          compiler_params=pltpu.CompilerParams(vmem_limit_bytes=VMEM_LIMIT_BYTES, dimension_semantics=sem))


def _fox_logits(q, k, ccol, crow, masked, t):
    s = bdot.nt(q, k) * (HEAD_DIM ** -0.5) + jnp.tile(ccol, (1, t // LANES)) - crow
    if masked:
        s = jnp.where(_iota2((t, t), 0) >= _iota2((t, t), 1), s, NEG_INF)
    return s


def _fox_p_ds(masked, t, q_ref, k_ref, v_ref, cb_ref, crow_ref, o_ref, lse_ref, do_ref):
    s = _fox_logits(q_ref[...], k_ref[...], cb_ref[...], crow_ref[0], masked, t)
    p = jnp.exp(s - jnp.tile(lse_ref[...], (1, t // LANES)))
    d_o = do_ref[...].astype(F32)
    delta = jnp.sum(d_o * o_ref[...].astype(F32), axis=-1, keepdims=True)
    return p, p * (bdot.nt(d_o, v_ref[...]) - delta), d_o


def _fox_specs(t, do_col):
    qspec = pl.BlockSpec((t, HEAD_DIM), lambda h, p, it, jt: (it[p], h))
    dospec = pl.BlockSpec((t, HEAD_DIM), lambda h, p, it, jt: (it[p], do_col + h))
    kspec = pl.BlockSpec((t, HEAD_DIM), lambda h, p, it, jt: (jt[p], h))
    vspec = pl.BlockSpec((t, HEAD_DIM), lambda h, p, it, jt: (jt[p], FOX_HEADS + h))
    cspec = pl.BlockSpec((1, 1, t), lambda h, p, it, jt: (h, 0, jt[p]))
    return qspec, dospec, kspec, vspec, cspec


def fox_fwd(q, kv, cb, crow):
    n_rows = kv.shape[0]
    t = _fox_tile(n_rows)
    n = n_rows // t
    tables = _fox_pairs(n, True)

    def body(it, jt, q_ref, k_ref, v_ref, cb_ref, crow_ref, o_ref, lse_ref, m_sc, l_sc, acc):
        i, j = it[pl.program_id(1)], jt[pl.program_id(1)]

        @pl.when(j == 0)
        def _():
            m_sc[...] = jnp.full(m_sc.shape, NEG_INF, F32)
            l_sc[...] = jnp.zeros_like(l_sc)
            acc[...] = jnp.zeros_like(acc)

        def step(masked):
            s = _fox_logits(q_ref[...], k_ref[...], cb_ref[...], crow_ref[0], masked, t)
            m_old = m_sc[...]
            m_new = jnp.maximum(m_old, jnp.max(s, axis=-1, keepdims=True))
            alpha = jnp.exp(m_old - m_new)
            p = jnp.exp(s - jnp.tile(m_new, (1, t // LANES)))
            l_sc[...] = l_sc[...] * alpha + jnp.sum(p, axis=-1, keepdims=True)
            acc[...] = acc[...] * alpha + bdot(p, v_ref[...])
            m_sc[...] = m_new

        @pl.when(j < i)
        def _():
            step(False)

        @pl.when(j == i)
        def _():
            step(True)
            o_ref[...] = (acc[...] / l_sc[...]).astype(o_ref.dtype)
            lse_ref[...] = m_sc[...] + jnp.log(l_sc[...])

    qspec, _, kspec, vspec, cspec = _fox_specs(t, 0)
    return _pcall_tables(body, name="fox_fwd",
                         out_shape=[jax.ShapeDtypeStruct((n_rows, FOX_WIDTH), BF16), jax.ShapeDtypeStruct((n_rows, FOX_WIDTH), F32)],
                         grid=(FOX_HEADS, n * (n + 1) // 2), tables=tables,
                         in_specs=[qspec, kspec, vspec, qspec, cspec], out_specs=[qspec, qspec],
                         scratch=[pltpu.VMEM((t, HEAD_DIM), F32)] * 3, sem=("parallel", "arbitrary"))(*tables, q, kv, kv, cb, crow)


def fox_bwd_dq(q, kv, cb, crow, o, lse, do, prev=None):
    do, _, do_col = do
    do_col *= FOX_HEADS
    n_rows = kv.shape[0]
    t = _fox_tile(n_rows)
    n = n_rows // t
    n_prev = 0 if prev is None else 1
    tables = _fox_pairs(n, True)

    def body(it, jt, q_ref, k_ref, v_ref, cb_ref, crow_ref, o_ref, lse_ref, do_ref, *rest):
        dq_ref, drow_ref, acc, rows = rest[n_prev:]
        i, j = it[pl.program_id(1)], jt[pl.program_id(1)]

        @pl.when(j == 0)
        def _():
            acc[...] = jnp.zeros_like(acc)
            rows[...] = jnp.zeros_like(rows)

        def step(masked):
            _, ds, _ = _fox_p_ds(masked, t, q_ref, k_ref, v_ref, cb_ref, crow_ref, o_ref, lse_ref, do_ref)
            acc[...] += bdot(ds, k_ref[...])
            rows[...] += jnp.sum(ds, axis=-1, keepdims=True)

        @pl.when(j < i)
        def _():
            step(False)

        @pl.when(j == i)
        def _():
            step(True)
            dq_ref[...] = (acc[...] * (HEAD_DIM ** -0.5)).astype(dq_ref.dtype)
            drow_ref[...] = rows[...] + rest[0][...] if n_prev else rows[...]

    qspec, dospec, kspec, vspec, cspec = _fox_specs(t, do_col)
    return _pcall_tables(body, name="fox_bwd_dq" + ("_acc" if n_prev else ""),
                         out_shape=[jax.ShapeDtypeStruct((n_rows, FOX_WIDTH), BF16), jax.ShapeDtypeStruct((n_rows, FOX_WIDTH), F32)],
                         grid=(FOX_HEADS, n * (n + 1) // 2), tables=tables,
                         in_specs=[qspec, kspec, vspec, qspec, cspec, qspec, qspec, dospec] + [qspec] * n_prev,
                         out_specs=[qspec, qspec], scratch=[pltpu.VMEM((t, HEAD_DIM), F32)] * 2,
                         sem=("parallel", "arbitrary"))(*tables, q, kv, kv, cb, crow, o, lse, do, *([prev] if n_prev else []))


def fox_bwd_dkv(q, kv, cb, crow, o, lse, do, prev=None):
    do, _, do_col = do
    do_col *= FOX_HEADS
    n_rows = kv.shape[0]
    t = _fox_tile(n_rows)
    n = n_rows // t
    n_prev = 0 if prev is None else 3
    tables = _fox_pairs(n, False)

    def body(it, jt, q_ref, k_ref, v_ref, cb_ref, crow_ref, o_ref, lse_ref, do_ref, *rest):
        prev_refs = rest[:n_prev]
        dk_ref, dv_ref, dc_ref, dk_acc, dv_acc, dc_acc = rest[n_prev:]
        i, j = it[pl.program_id(1)], jt[pl.program_id(1)]

        def step(masked):
            p, ds, d_o = _fox_p_ds(masked, t, q_ref, k_ref, v_ref, cb_ref, crow_ref, o_ref, lse_ref, do_ref)
            dv_acc[...] += bdot.tn(p, d_o)
            dk_acc[...] += bdot.tn(ds, q_ref[...])
            dc_acc[...] -= jnp.sum(ds, axis=0, keepdims=True)

        @pl.when(i == j)
        def _():
            dk_acc[...] = jnp.zeros_like(dk_acc)
            dv_acc[...] = jnp.zeros_like(dv_acc)
            dc_acc[...] = jnp.zeros_like(dc_acc)
            step(True)

        @pl.when(i > j)
        def _():
            step(False)

        @pl.when(i == n - 1)
        def _():
            dk, dv, dc = dk_acc[...] * (HEAD_DIM ** -0.5), dv_acc[...], dc_acc[...]
            if n_prev:
                dk, dv, dc = dk + prev_refs[0][...], dv + prev_refs[1][...], dc + prev_refs[2][0]
            dk_ref[...] = dk
            dv_ref[...] = dv
            dc_ref[0] = dc

    qspec, dospec, kspec, vspec, cspec = _fox_specs(t, do_col)
    return _pcall_tables(body, name="fox_bwd_dkv" + ("_acc" if n_prev else ""),
                         out_shape=[jax.ShapeDtypeStruct((n_rows, FOX_WIDTH), F32), jax.ShapeDtypeStruct((n_rows, FOX_WIDTH), F32),
                                    jax.ShapeDtypeStruct((FOX_HEADS, 1, n_rows), F32)],
                         grid=(FOX_HEADS, n * (n + 1) // 2), tables=tables,
                         in_specs=[qspec, kspec, vspec, qspec, cspec, qspec, qspec, dospec] + [kspec, kspec, cspec][:n_prev],
                         out_specs=[kspec, kspec, cspec],
                         scratch=[pltpu.VMEM((t, HEAD_DIM), F32), pltpu.VMEM((t, HEAD_DIM), F32), pltpu.VMEM((1, t), F32)],
                         sem=("parallel", "arbitrary"))(*tables, q, kv, kv, cb, crow, o, lse, do, *(prev or ()))


def loss_head(h, gain, target, *, tile=512):
    n_rows, d = h.shape
    tile = min(tile, n_rows)

    def body(h_ref, g_ref, t_ref, part_ref, dy_ref):
        (y,) = f_rmsnorm(h_ref[...], g_ref[...])
        diff = y - t_ref[...]
        dy_ref[...] = diff * (1.0 / d)
        part = jnp.sum(diff * diff, axis=0, keepdims=True)
        first = pl.program_id(0) == 0

        @pl.when(first)
        def _():
            part_ref[...] = part

        @pl.when(jnp.logical_not(first))
        def _():
            part_ref[...] += part

    blk = pl.BlockSpec((tile, d), lambda i: (i, 0))
    one = pl.BlockSpec((1, d), lambda i: (0, 0))
    return _pcall(body, name="loss_head",
                  out_shape=[jax.ShapeDtypeStruct((1, d), F32), jax.ShapeDtypeStruct((n_rows, d), F32)],
                  grid=(n_rows // tile,), in_specs=[blk, one, blk], out_specs=[one, blk], sem=("arbitrary",))(h, gain, target)


def adamw(w, g, m, v, *, name):
    shape = w.shape
    cols = shape[-1] if w.ndim >= 2 else w.size
    rows = w.size // cols
    tile = _pick(rows, (256, 128, 64, 32, 16, 8))
    as2d = lambda a: a.reshape(rows, cols)

    def body(w_ref, g_ref, m_ref, v_ref, d_ref, nm_ref, nv_ref):
        g_ = g_ref[...]
        m_ = ADAM_B1 * m_ref[...] + (1.0 - ADAM_B1) * g_
        v_ = ADAM_B2 * v_ref[...] + (1.0 - ADAM_B2) * (g_ * g_)
        m_hat = m_ / (1.0 - ADAM_B1 ** ADAM_STEP)
        v_hat = v_ / (1.0 - ADAM_B2 ** ADAM_STEP)
        d_ref[...] = -ADAM_LR * (m_hat / (jnp.sqrt(v_hat) + ADAM_EPS) + ADAM_WD * w_ref[...])
        nm_ref[...] = m_
        nv_ref[...] = v_

    blk = pl.BlockSpec((tile, cols), lambda i: (i, 0))
    outs = _pcall(body, name=name, out_shape=[jax.ShapeDtypeStruct((rows, cols), F32)] * 3, grid=(rows // tile,),
                  in_specs=[blk] * 4, out_specs=[blk] * 3, sem=("parallel",))(as2d(w), as2d(g), as2d(m), as2d(v))
    return tuple(o.reshape(shape) for o in outs)


def sum_leading(a, *, name):
    p, r, c = a.shape
    tile = _pick(r, (256, 128, 64, 32, 16, 8))

    def body(a_ref, o_ref):
        total = a_ref[0].astype(F32)
        for k in range(1, p):
            total = total + a_ref[k].astype(F32)
        o_ref[...] = total

    return _pcall(body, name=name, out_shape=jax.ShapeDtypeStruct((r, c), F32), grid=(r // tile,),
                  in_specs=[pl.BlockSpec((p, tile, c), lambda i: (0, i, 0))],
                  out_specs=pl.BlockSpec((tile, c), lambda i: (i, 0)), sem=("parallel",))(a)


_HBM = pl.BlockSpec(memory_space=pltpu.HBM)


def _comm_call(body, *, name, out_shape, n_in, scratch):
    return pl.pallas_call(body, name=name, out_shape=out_shape, in_specs=[_HBM] * n_in, out_specs=_HBM,
                          scratch_shapes=scratch,
                          compiler_params=pltpu.CompilerParams(has_side_effects=True))


def all_gather8(a, *, name):
    m_per, n = a.shape

    def body(x_ref, out_ref, send_sems, recv_sems, local_sem):
        x, y, c = lax.axis_index("x"), lax.axis_index("y"), lax.axis_index("c")
        me, sibling = (x, y, c), (x, y, 1 - c)
        chips = [(1 - x, y), (x, 1 - y), (1 - x, 1 - y)]

        def rows(px, py, pc):
            return out_ref.at[pl.ds((4 * px + 2 * py + pc) * m_per, m_per), :]

        def copy(k, block, to, src=None):
            return pltpu.make_async_remote_copy(
                src_ref=rows(*block) if src is None else src, dst_ref=rows(*block),
                send_sem=send_sems.at[k], recv_sem=recv_sems.at[k], device_id=to, device_id_type=MESH)

        mine = pltpu.make_async_copy(x_ref, rows(*me), local_sem)
        mine.start()
        first = [copy(0, me, sibling, src=x_ref)]
        first += [copy(1 + j, me, (*chip, c), src=x_ref) for j, chip in enumerate(chips)]
        for cp in first:
            cp.start()
        passed = [copy(4 + j, (*chip, c), sibling) for j, chip in enumerate(chips)]
        for j, chip in enumerate(chips):
            copy(1 + j, (*chip, c), me).wait_recv()
            passed[j].start()
        copy(0, sibling, me).wait_recv()
        for j, chip in enumerate(chips):
            copy(4 + j, (*chip, 1 - c), me).wait_recv()
        for cp in first + passed:
            cp.wait_send()
        mine.wait()

    return _comm_call(body, name=name, out_shape=jax.ShapeDtypeStruct((N_DEV * m_per, n), a.dtype), n_in=1,
                      scratch=[pltpu.SemaphoreType.DMA((7,)), pltpu.SemaphoreType.DMA((7,)), pltpu.SemaphoreType.DMA])(a)


PACK_COLS = 1024
PACK_ROW_MULTIPLE = 32

WEIGHT_NAMES = ["ffn1_norm", "ffn1_w_gate_up", "ffn1_w_down", "mix_norm", "ffn2_norm", "ffn2_w_gate_up", "ffn2_w_down",
                "gdn_w_in", "gdn_conv", "gdn_A_log", "gdn_dt_bias", "gdn_out_norm", "fox_w_in", "w_out", "mem_norm",
                "mem_w_kv", "kv_norm", "kv_w", "kv_b_f", "final_norm"]
SHARDED = [("ffn1_w_gate_up", 2), ("ffn1_w_down", 1), ("ffn2_w_gate_up", 2), ("ffn2_w_down", 1), ("gdn_w_in", 2),
           ("gdn_conv", 2), ("fox_w_in", 1), ("w_out", 1), ("mem_w_kv", 1), ("kv_w", 0)]
REPLICATED = [n for n in WEIGHT_NAMES if n not in dict(SHARDED)]


PACK_PIECE_ROWS = 16


def _rows_of(size):
    return -(-size // (PACK_COLS * PACK_PIECE_ROWS)) * PACK_PIECE_ROWS


def pack(pieces, dtype, row_multiple=PACK_ROW_MULTIPLE):
    bufs, total = [], 0
    for p in pieces:
        flat = p.astype(dtype).reshape(-1)
        rows = _rows_of(flat.size)
        bufs.append(jnp.pad(flat, (0, rows * PACK_COLS - flat.size)).reshape(rows, PACK_COLS))
        total += rows
    pad = -total % row_multiple
    if pad:
        bufs.append(jnp.zeros((pad, PACK_COLS), dtype))
    return jnp.concatenate(bufs, axis=0)


def unpack(buf, shapes):
    out, row = [], 0
    for shape in shapes:
        size = 1
        for s in shape:
            size *= s
        rows = _rows_of(size)
        out.append(buf[row:row + rows].reshape(-1)[:size].reshape(shape))
        row += rows
    return out


def _row(vec, width=None):
    vec = vec.astype(F32).reshape(1, -1)
    if width is not None and vec.shape[1] < width:
        vec = jnp.pad(vec, ((0, 0), (0, width - vec.shape[1])))
    return vec


ROW_TILE = 512
GDN_PROJ_WIDTH = 4 * GDN_WIDTH + MEM_WIDTH + LANES
GDN_Z_COL, GDN_QMEM_COL, GDN_AB_COL = 3, 4 * GDN_WIDTH // MEM_WIDTH, (4 * GDN_WIDTH + MEM_WIDTH) // LANES
FOX_QMEM_COL = FOX_WIDTH // MEM_WIDTH
KV_PAD_WIDTH = 2 * FOX_WIDTH + LANES


def rms_fwd(x, gain_row, out_dtype=BF16):
    return rowwise(f_rmsnorm, [x], [gain_row], [(x.shape[1], out_dtype)], tile=ROW_TILE, name="rms_fwd")[0]


def rms_bwd(x, gain_row, dy, dres=None):
    return rowwise_bwd(f_rmsnorm, [x], [gain_row], [dy], tile=ROW_TILE, name="rms_bwd", row_grads=[F32],
                       const_grads=[True], add=None if dres is None else {0: dres})


def _ffn_tiles(n_rows):
    return _pick(n_rows, (512, 256, 128)), _pick(FFN_HIDDEN, (1408, 256, 128))


def ffn_up_act(n, wgu):
    n_rows, d = n.shape
    tm, tn = _ffn_tiles(n_rows)
    nj = FFN_HIDDEN // tn

    def body(n_ref, wg_ref, wu_ref, gu_ref, act_ref):
        x = n_ref[...].astype(BF16)
        g = bdot.nn(x, wg_ref[...])
        u = bdot.nn(x, wu_ref[...])
        gu_ref[0] = g.astype(gu_ref.dtype)
        gu_ref[1] = u.astype(gu_ref.dtype)
        act_ref[...] = (_silu(g) * u).astype(act_ref.dtype)

    return _pcall(body, name="ffn_up_act",
                  out_shape=[jax.ShapeDtypeStruct((2, n_rows, FFN_HIDDEN), BF16), jax.ShapeDtypeStruct((n_rows, FFN_HIDDEN), BF16)],
                  grid=(nj, n_rows // tm),
                  in_specs=[pl.BlockSpec((tm, d), lambda j, i: (i, 0)), pl.BlockSpec((d, tn), lambda j, i: (0, j)),
                            pl.BlockSpec((d, tn), lambda j, i: (0, nj + j))],
                  out_specs=[pl.BlockSpec((2, tm, tn), lambda j, i: (0, i, j)), pl.BlockSpec((tm, tn), lambda j, i: (i, j))],
                  sem=("parallel", "parallel"))(n, wgu, wgu)


def ffn_down_dx_act(dh, wd, gu):
    n_rows, d = dh.shape
    tm, tn = _ffn_tiles(n_rows)

    def body(dh_ref, wd_ref, gu_ref, dgu_ref):
        dact = 0.5 * bdot.nt(dh_ref[...], wd_ref[...])
        gate, up = gu_ref[0].astype(F32), gu_ref[1].astype(F32)
        sg = _sigmoid(gate)
        dgu_ref[0] = (dact * up * (sg * (1.0 + gate * (1.0 - sg)))).astype(dgu_ref.dtype)
        dgu_ref[1] = (dact * (gate * sg)).astype(dgu_ref.dtype)

    blk = pl.BlockSpec((2, tm, tn), lambda j, i: (0, i, j))
    return _pcall(body, name="ffn_down_dx_act", out_shape=jax.ShapeDtypeStruct((2, n_rows, FFN_HIDDEN), BF16),
                  grid=(FFN_HIDDEN // tn, n_rows // tm),
                  in_specs=[pl.BlockSpec((tm, d), lambda j, i: (i, 0)), pl.BlockSpec((tn, d), lambda j, i: (j, 0)), blk],
                  out_specs=blk, sem=("parallel", "parallel"))(dh, wd, gu)


def ffn_fwd(h, gain_row, wgu, wd):
    n = rms_fwd(h, gain_row)
    gu, act = ffn_up_act(n, wgu)
    return mm(act, wd, scale=0.5, res=h, name="ffn_down"), (h, n, gu, act)


def ffn_bwd(dh, saved, gain_row, wgu, wd):
    h, n, gu, act = saved
    dgu = ffn_down_dx_act(dh, wd, gu)
    dwd = mm(act, dh, ta=True, scale=0.5, name="ffn_down_dw")
    dwgu = mm(n, dgu, ta=True, b_split=True, name="ffn_up_dw")
    dn = mm(dgu, wgu, tb=True, a_split=True, out_dtype=BF16, name="ffn_up_dx")
    dh, dgain = rms_bwd(h, gain_row, dn, dh)
    return dh, dwgu, dwd, dgain


def gdn_fwd(proj, w8, a_row, dt_row, onorm_row):
    wide = [(GDN_WIDTH, F32)] * 5
    xc = conv_fwd(proj, w8, width=3 * GDN_WIDTH)
    q, k, v, gc, beta = rowwise(f_gdn_pre, [xc, (proj, LANES, GDN_AB_COL)], [a_row, dt_row], wide, tile=GDN_CHUNK,
                                name="gdn_pre_fwd")
    u, w, qk, kt, qh, inv = rowwise(f_gdn_intra, [q, k, v, gc, beta], [], wide + wide[:1], tile=GDN_CHUNK,
                                    name="gdn_intra_fwd")
    o, vn, sin = gdn_scan_fwd(u, w, qk, kt, qh, gc)
    main = rowwise(f_gdn_post, [o, (proj, GDN_WIDTH, GDN_Z_COL)], [onorm_row], [(GDN_WIDTH, BF16)], tile=ROW_TILE,
                   name="gdn_post_fwd")[0]
    return main, (xc, q, k, v, gc, beta, inv, w, qk, kt, qh, vn, sin, o)


def gdn_bwd(dmain, proj, saved, w8, a_row, dt_row, onorm_row):
    xc, q, k, v, gc, beta, inv, w, qk, kt, qh, vn, sin, o = saved
    do, dz, donorm = rowwise_bwd(f_gdn_post, [o, (proj, GDN_WIDTH, GDN_Z_COL)], [onorm_row], [dmain], tile=ROW_TILE,
                                 name="gdn_post_bwd", row_grads=[F32, BF16], const_grads=[True])
    du, dw, dqk, dkt, dqh, dgl = gdn_scan_bwd(do, w, qk, kt, qh, gc, vn, sin)
    dq, dk, dv, dgc, dbeta = gdn_intra_bwd(q, k, v, gc, beta, inv, (du, dw, dqk, dkt, dqh), dgl)
    dxc, dab, da, ddt = rowwise_bwd(f_gdn_pre, [xc, (proj, LANES, GDN_AB_COL)], [a_row, dt_row], [dq, dk, dv, dgc, dbeta],
                                    tile=GDN_CHUNK, name="gdn_pre_bwd", row_grads=[F32, BF16], const_grads=[True, True])
    dqkv, dw8 = conv_bwd(proj, w8, dxc, width=3 * GDN_WIDTH)
    return dqkv, dz, dab, dw8, da, ddt, donorm


def mem_fwd(q, kmem, vmem):
    return rowwise(f_mem_attn, [q], [kmem, vmem], [(MEM_WIDTH, BF16)], tile=ROW_TILE, name="mem_attn_fwd")[0]


def mem_bwd(q, kmem, vmem, dout):
    return rowwise_bwd(f_mem_attn, [q], [kmem, vmem], [dout], tile=ROW_TILE, name="mem_attn_bwd", row_grads=[BF16],
                       const_grads=[True, True])


def forward_backward(xs, mems, target, P):
    depth, n_a = 4, 2
    G = {}
    mem_gain = _row(P["mem_norm"])
    mem_n = rms_fwd(mems, mem_gain)
    h = xs
    saved = []
    shared = None
    for l in range(depth):
        h0 = h
        h1, s1 = ffn_fwd(h0, _row(P["ffn1_norm"][l]), P["ffn1_w_gate_up"][l], P["ffn1_w_down"][l])
        u = rms_fwd(h1, _row(P["mix_norm"][l]))
        kvm = mm(mem_n, P["mem_w_kv"][l], name="mem_kv")
        kmem, vmem = kvm[:, :MEM_WIDTH], kvm[:, MEM_WIDTH:]
        if l < n_a:
            gp = (P["conv8"][l], _row(P["gdn_A_log"][l], LANES), _row(P["gdn_dt_bias"][l], LANES), _row(P["gdn_out_norm"][l]))
            proj = mm(u, P["gdn_w_in_pad"][l], name="gdn_in")
            main, sm = gdn_fwd(proj, *gp)
            qm = (proj, MEM_WIDTH, GDN_QMEM_COL)
        else:
            proj = mm(u, P["fox_w_in"][l - n_a], out_dtype=BF16, name="fox_in")
            main, lse = fox_fwd(proj, *shared)
            sm = (main, lse)
            qm = (proj, MEM_WIDTH, FOX_QMEM_COL)
        mo = mem_fwd(qm, kmem, vmem)
        cat = jnp.concatenate([main, mo], axis=1)
        h2 = mm(cat, P["w_out"][l], res=h1, name="mix_out")
        h3, s2 = ffn_fwd(h2, _row(P["ffn2_norm"][l]), P["ffn2_w_gate_up"][l], P["ffn2_w_down"][l])
        saved.append((s1, h1, u, kmem, vmem, proj, sm, qm, cat, s2))
        h = h3
        if l == n_a - 1:
            nkv = rms_fwd(h, _row(P["kv_norm"]))
            kv = mm(nkv, P["kv_w_pad"][:, :2 * FOX_WIDTH], out_dtype=BF16, name="fox_kv")
            f = mm(nkv, P["kv_w_pad"][:, 2 * FOX_WIDTH:], name="fox_f")
            bf_row = _row(P["kv_b_f"], LANES)
            cb, crow = fox_gate_fwd(f, bf_row)
            shared = (kv, cb, crow)
            kv_saved = (h, nkv, f, bf_row)

    part, dy = loss_head(h, _row(P["final_norm"]), target)
    dh, G["final_norm"] = rms_bwd(h, _row(P["final_norm"]), dy)

    per_layer = {n: [None] * depth for n in ("ffn1_norm", "ffn1_w_gate_up", "ffn1_w_down", "mix_norm", "ffn2_norm",
                                             "ffn2_w_gate_up", "ffn2_w_down", "w_out", "mem_w_kv")}
    gdn_g = {n: [None] * n_a for n in ("gdn_w_in_pad", "conv8", "gdn_A_log", "gdn_dt_bias", "gdn_out_norm")}
    fox_g = [None] * (depth - n_a)
    dmem_n = None
    dkv_acc = dcb_acc = None
    for l in reversed(range(depth)):
        s1, h1, u, kmem, vmem, proj, sm, qm, cat, s2 = saved[l]
        if l == n_a - 1:
            hk, nkv, f, bf_row = kv_saved
            dk, dv, dcrow = dkv_acc
            df, dbf = fox_gate_bwd(f, bf_row, dcrow, dcb_acc)
            dp = jnp.concatenate([dk.astype(BF16), dv.astype(BF16), df.astype(BF16)], axis=1)
            G["kv_w_pad"] = mm(nkv, dp, ta=True, name="fox_kv_dw")
            G["kv_b_f"] = dbf
            dnkv = mm(dp, P["kv_w_pad"], tb=True, out_dtype=BF16, name="fox_kv_dx")
            dh, G["kv_norm"] = rms_bwd(hk, _row(P["kv_norm"]), dnkv, dh)
        dh, per_layer["ffn2_w_gate_up"][l], per_layer["ffn2_w_down"][l], per_layer["ffn2_norm"][l] = ffn_bwd(
            dh, s2, _row(P["ffn2_norm"][l]), P["ffn2_w_gate_up"][l], P["ffn2_w_down"][l])
        dcat = mm(dh, P["w_out"][l], tb=True, out_dtype=BF16, name="mix_out_dx")
        per_layer["w_out"][l] = mm(cat, dh, ta=True, name="mix_out_dw")
        dqm, dkm, dvm = mem_bwd(qm, kmem, vmem, (dcat, MEM_WIDTH, FOX_QMEM_COL))
        dkvm = jnp.concatenate([dkm, dvm], axis=1)
        per_layer["mem_w_kv"][l] = mm(mem_n, dkvm, ta=True, name="mem_kv_dw")
        dmem_n = mm(dkvm, P["mem_w_kv"][l], tb=True, res=dmem_n, name="mem_kv_dx")
        dmain = (dcat, GDN_WIDTH, 0)
        if l < n_a:
            gp = (P["conv8"][l], _row(P["gdn_A_log"][l], LANES), _row(P["gdn_dt_bias"][l], LANES), _row(P["gdn_out_norm"][l]))
            dqkv, dz, dab, gdn_g["conv8"][l], gdn_g["gdn_A_log"][l], gdn_g["gdn_dt_bias"][l], gdn_g["gdn_out_norm"][l] = gdn_bwd(
                dmain, proj, sm, *gp)
            dproj = jnp.concatenate([dqkv, dz, dqm, dab], axis=1)
            gdn_g["gdn_w_in_pad"][l] = mm(u, dproj, ta=True, name="gdn_in_dw")
            du = mm(dproj, P["gdn_w_in_pad"][l], tb=True, out_dtype=BF16, name="gdn_in_dx")
        else:
            o, lse = sm
            dq, dcb_acc = fox_bwd_dq(proj, *shared, o, lse, dmain, dcb_acc)
            dkv_acc = fox_bwd_dkv(proj, *shared, o, lse, dmain, dkv_acc)
            dproj = jnp.concatenate([dq, dqm], axis=1)
            fox_g[l - n_a] = mm(u, dproj, ta=True, name="fox_in_dw")
            du = mm(dproj, P["fox_w_in"][l - n_a], tb=True, out_dtype=BF16, name="fox_in_dx")
        dh, per_layer["mix_norm"][l] = rms_bwd(h1, _row(P["mix_norm"][l]), du, dh)
        dh, per_layer["ffn1_w_gate_up"][l], per_layer["ffn1_w_down"][l], per_layer["ffn1_norm"][l] = ffn_bwd(
            dh, s1, _row(P["ffn1_norm"][l]), P["ffn1_w_gate_up"][l], P["ffn1_w_down"][l])

    (G["mem_norm"],) = rowwise_bwd(f_rmsnorm, [mems], [mem_gain], [dmem_n], tile=ROW_TILE, name="mem_norm_bwd",
                                   row_grads=[None], const_grads=[True])
    for n, v in per_layer.items():
        G[n] = jnp.stack(v)
    for n, v in gdn_g.items():
        G[n] = jnp.stack(v)
    G["fox_w_in"] = jnp.stack(fox_g)
    return part, dh, G


_GDN_O0 = 4 * GDN_WIDTH
_GDN_O1 = _GDN_O0 + 2 * GDN_HEADS
_KV_WIDTH = 2 * FOX_WIDTH + FOX_HEADS


def derived_weights(gdn_w_in, gdn_conv, kv_w=None):
    zeros = jnp.zeros(gdn_w_in.shape[:-1] + (LANES - 2 * GDN_HEADS,), gdn_w_in.dtype)
    out = dict(
        gdn_w_in_pad=jnp.concatenate([gdn_w_in[..., :_GDN_O0], gdn_w_in[..., _GDN_O1:], gdn_w_in[..., _GDN_O0:_GDN_O1], zeros], axis=-1),
        conv8=jnp.pad(gdn_conv.astype(F32), ((0, 0), (0, 8 - CONV_WIDTH), (0, 0))))
    if kv_w is not None:
        out["kv_w_pad"] = jnp.pad(kv_w, ((0, 0), (0, KV_PAD_WIDTH - _KV_WIDTH)))
    return out


def reference_layout(G):
    gp = G["gdn_w_in_pad"]
    out = dict(G)
    out["gdn_w_in"] = jnp.concatenate([gp[..., :_GDN_O0], gp[..., _GDN_O0 + MEM_WIDTH:_GDN_O0 + MEM_WIDTH + 2 * GDN_HEADS],
                                       gp[..., _GDN_O0:_GDN_O0 + MEM_WIDTH]], axis=-1)
    out["gdn_conv"] = G["conv8"][:, :CONV_WIDTH]
    out["kv_w"] = G["kv_w_pad"][:, :_KV_WIDTH]
    out["gdn_A_log"] = G["gdn_A_log"][:, 0, :GDN_HEADS]
    out["gdn_dt_bias"] = G["gdn_dt_bias"][:, 0, :GDN_HEADS]
    out["gdn_out_norm"] = G["gdn_out_norm"][:, 0, :]
    out["kv_b_f"] = G["kv_b_f"][0, :FOX_HEADS]
    for n in ("ffn1_norm", "mix_norm", "ffn2_norm"):
        out[n] = G[n][:, 0, :]
    for n in ("mem_norm", "kv_norm", "final_norm"):
        out[n] = G[n][0]
    return {n: out[n] for n in WEIGHT_NAMES}


EXCHANGED = [("ffn1_w_gate_up", 2, 0), ("ffn1_w_down", 1, 0), ("ffn2_w_gate_up", 2, 0), ("ffn2_w_down", 1, 0),
             ("gdn_w_in", 2, 0), ("fox_w_in", 1, 0), ("w_out", 1, 0), ("mem_w_kv", 1, 0), ("kv_w", 0, 1)]
GDN_IN_SHARD = (4 * GDN_WIDTH + 2 * GDN_HEADS + MEM_WIDTH) // N_CHIPS
GDN_IN_SLOT = 896


def _slab(ref, axis_slices):
    idx = [slice(None)] * len(ref.shape)
    for axis, (start, size) in axis_slices.items():
        idx[axis] = pl.ds(start, size)
    return ref.at[tuple(idx)]


def _comm_multi(body, *, name, n_in, out_shapes, scratch):
    return pl.pallas_call(body, name=name, out_shape=out_shapes, in_specs=[_HBM] * n_in, out_specs=[_HBM] * len(out_shapes),
                          scratch_shapes=scratch, compiler_params=pltpu.CompilerParams(has_side_effects=True))


def gather_shards(shards, layout):
    n = len(shards)
    fulls = [tuple(d * (N_CHIPS if a == sa else 1) for a, d in enumerate(s.shape)) for s, (sa, _) in zip(shards, layout)]

    def body(*refs):
        ins, outs = refs[:n], refs[n:2 * n]
        send_sems, recv_sems, local_sems = refs[2 * n:]
        x, y, c = lax.axis_index("x"), lax.axis_index("y"), lax.axis_index("c")
        me, sibling = (x, y, c), (x, y, 1 - c)
        chips = [(1 - x, y), (x, 1 - y), (1 - x, 1 - y)]

        def region(w, px, py, pc):
            (sa, ha), shard = layout[w], shards[w].shape
            return _slab(outs[w], {sa: ((2 * px + py) * shard[sa], shard[sa]), ha: (pc * (shard[ha] // 2), shard[ha] // 2)})

        def my_half(w):
            ha, shard = layout[w][1], shards[w].shape
            return _slab(ins[w], {ha: (c * (shard[ha] // 2), shard[ha] // 2)})

        def copy(w, k, block, to, src=None):
            return pltpu.make_async_remote_copy(
                src_ref=region(w, *block) if src is None else src, dst_ref=region(w, *block),
                send_sem=send_sems.at[7 * w + k], recv_sem=recv_sems.at[7 * w + k], device_id=to, device_id_type=MESH)

        mine, first, passed = [], [], []
        for w in range(n):
            mine.append(pltpu.make_async_copy(my_half(w), region(w, *me), local_sems.at[w]))
            mine[w].start()
            first.append([copy(w, 0, me, sibling, src=my_half(w))]
                         + [copy(w, 1 + j, me, (*chip, c), src=my_half(w)) for j, chip in enumerate(chips)])
            for cp in first[w]:
                cp.start()
            passed.append([copy(w, 4 + j, (*chip, c), sibling) for j, chip in enumerate(chips)])
        for w in range(n):
            for j, chip in enumerate(chips):
                copy(w, 1 + j, (*chip, c), me).wait_recv()
                passed[w][j].start()
        for w in range(n):
            copy(w, 0, sibling, me).wait_recv()
            for j, chip in enumerate(chips):
                copy(w, 4 + j, (*chip, 1 - c), me).wait_recv()
        for w in range(n):
            for cp in first[w] + passed[w]:
                cp.wait_send()
            mine[w].wait()

    return _comm_multi(body, name="gather_shards", n_in=n,
                       out_shapes=[jax.ShapeDtypeStruct(f, s.dtype) for f, s in zip(fulls, shards)],
                       scratch=[pltpu.SemaphoreType.DMA((7 * n,)), pltpu.SemaphoreType.DMA((7 * n,)),
                                pltpu.SemaphoreType.DMA((n,))])(*shards)


def swap_other_halves(arrays, layout):
    n = len(arrays)
    halves = [tuple(d // 2 if a == ha else d for a, d in enumerate(g.shape)) for g, (_, ha) in zip(arrays, layout)]

    def body(*refs):
        ins, outs = refs[:n], refs[n:2 * n]
        send_sems, recv_sems = refs[2 * n:]
        x, y, c = lax.axis_index("x"), lax.axis_index("y"), lax.axis_index("c")
        copies = []
        for w in range(n):
            ha, size = layout[w][1], halves[w][layout[w][1]]
            copies.append(pltpu.make_async_remote_copy(
                src_ref=_slab(ins[w], {ha: ((1 - c) * size, size)}), dst_ref=outs[w], send_sem=send_sems.at[w],
                recv_sem=recv_sems.at[w], device_id=(x, y, 1 - c), device_id_type=MESH))
            copies[w].start()
        for cp in copies:
            cp.wait()

    return _comm_multi(body, name="grad_pair_swap", n_in=n,
                       out_shapes=[jax.ShapeDtypeStruct(h, g.dtype) for h, g in zip(halves, arrays)],
                       scratch=[pltpu.SemaphoreType.DMA((n,)), pltpu.SemaphoreType.DMA((n,))])(*arrays)


def scatter_to_chips(arrays, layout):
    n = len(arrays)
    slabs = [tuple(d // N_CHIPS if a == sa else d for a, d in enumerate(p.shape)) for p, (sa, _) in zip(arrays, layout)]

    def body(*refs):
        ins, outs = refs[:n], refs[n:2 * n]
        send_sems, recv_sems, local_sems = refs[2 * n:]
        x, y, c = lax.axis_index("x"), lax.axis_index("y"), lax.axis_index("c")
        me = 2 * x + y

        def slab(w, k):
            sa, size = layout[w][0], slabs[w][layout[w][0]]
            return _slab(ins[w], {sa: (k * size, size)})

        local, copies = [], []
        for w in range(n):
            local.append(pltpu.make_async_copy(slab(w, me), outs[w].at[me], local_sems.at[w]))
            local[w].start()
            for j, (px, py) in enumerate([(1 - x, y), (x, 1 - y), (1 - x, 1 - y)]):
                copies.append(pltpu.make_async_remote_copy(
                    src_ref=slab(w, 2 * px + py), dst_ref=outs[w].at[me], send_sem=send_sems.at[3 * w + j],
                    recv_sem=recv_sems.at[3 * w + j], device_id=(px, py, c), device_id_type=MESH))
                copies[-1].start()
        for cp in copies:
            cp.wait()
        for cp in local:
            cp.wait()

    return _comm_multi(body, name="grad_all_to_all", n_in=n,
                       out_shapes=[jax.ShapeDtypeStruct((N_CHIPS,) + s, p.dtype) for s, p in zip(slabs, arrays)],
                       scratch=[pltpu.SemaphoreType.DMA((3 * n,)), pltpu.SemaphoreType.DMA((3 * n,)),
                                pltpu.SemaphoreType.DMA((n,))])(*arrays)


def swap_with_sibling(arrays):
    n = len(arrays)

    def body(*refs):
        ins, outs = refs[:n], refs[n:2 * n]
        send_sems, recv_sems = refs[2 * n:]
        x, y, c = lax.axis_index("x"), lax.axis_index("y"), lax.axis_index("c")
        copies = [pltpu.make_async_remote_copy(src_ref=ins[w], dst_ref=outs[w], send_sem=send_sems.at[w], recv_sem=recv_sems.at[w],
                                               device_id=(x, y, 1 - c), device_id_type=MESH) for w in range(n)]
        for cp in copies:
            cp.start()
        for cp in copies:
            cp.wait()

    return _comm_multi(body, name="grad_half_swap", n_in=n, out_shapes=[jax.ShapeDtypeStruct(a.shape, a.dtype) for a in arrays],
                       scratch=[pltpu.SemaphoreType.DMA((n,)), pltpu.SemaphoreType.DMA((n,))])(*arrays)


def join_halves(mine, other, half_axis, c_arr, *, name):
    a0, a1, a2 = mine.shape
    tile = _row_tile(a1, a2)

    def body(c_ref, m_ref, o_ref, out_ref):
        for half in range(2):
            @pl.when(c_ref[0] == half)
            def _(half=half):
                out_ref[half] = m_ref[...]
                out_ref[1 - half] = o_ref[...]

    blk = pl.BlockSpec((None, tile, a2), lambda i, j, c_ref: (i, j, 0))
    if half_axis == 0:
        out_shape, out_blk = (2, a0, a1, a2), pl.BlockSpec((2, None, tile, a2), lambda i, j, c_ref: (0, i, j, 0))
    else:
        out_shape, out_blk = (a0, 2, a1, a2), pl.BlockSpec((None, 2, tile, a2), lambda i, j, c_ref: (i, 0, j, 0))
    spec = pltpu.PrefetchScalarGridSpec(num_scalar_prefetch=1, grid=(a0, a1 // tile), in_specs=[blk, blk], out_specs=out_blk)
    out = pl.pallas_call(body, name=name, out_shape=jax.ShapeDtypeStruct(out_shape, mine.dtype), grid_spec=spec,
                         compiler_params=pltpu.CompilerParams(vmem_limit_bytes=VMEM_LIMIT_BYTES,
                                                              dimension_semantics=("parallel", "parallel")))(c_arr, mine, other)
    return out.reshape((2 * a0, a1, a2) if half_axis == 0 else (a0, 2 * a1, a2))


def _row_tile(rows, cols, itemsize=4, budget=2 * 1024 * 1024):
    for t in (1024, 512, 256, 128, 64, 32, 16):
        if rows % t == 0 and t * cols * itemsize <= budget:
            return t
    return rows


def add_own_half(full, recv, half_axis, c_arr, *, name):
    a0, a1, a2 = recv.shape
    tile = _row_tile(a1, a2)

    def body(c_ref, f_ref, r_ref, o_ref):
        o_ref[...] = (f_ref[...] + r_ref[...]).astype(o_ref.dtype)

    if half_axis == 0:
        f_spec = pl.BlockSpec((None, tile, a2), lambda i, j, c_ref: (c_ref[0] * a0 + i, j, 0))
    else:
        f_spec = pl.BlockSpec((None, tile, a2), lambda i, j, c_ref: (i, c_ref[0] * (a1 // tile) + j, 0))
    blk = pl.BlockSpec((None, tile, a2), lambda i, j, c_ref: (i, j, 0))
    spec = pltpu.PrefetchScalarGridSpec(num_scalar_prefetch=1, grid=(a0, a1 // tile), in_specs=[f_spec, blk], out_specs=blk)
    return pl.pallas_call(body, name=name, out_shape=jax.ShapeDtypeStruct(recv.shape, BF16), grid_spec=spec,
                          compiler_params=pltpu.CompilerParams(vmem_limit_bytes=VMEM_LIMIT_BYTES,
                                                               dimension_semantics=("parallel", "parallel")))(c_arr, full, recv)


def sum_slots(q, *, name):
    _, a0, a1, a2 = q.shape
    tile = _row_tile(a1, a2, budget=1024 * 1024)

    def body(q_ref, o_ref):
        total = q_ref[0].astype(F32)
        for k in range(1, N_CHIPS):
            total = total + q_ref[k].astype(F32)
        o_ref[...] = total

    return _pcall(body, name=name, out_shape=jax.ShapeDtypeStruct((a0, a1, a2), F32), grid=(a0, a1 // tile),
                  in_specs=[pl.BlockSpec((N_CHIPS, None, tile, a2), lambda i, j: (0, i, j, 0))],
                  out_specs=pl.BlockSpec((None, tile, a2), lambda i, j: (i, j, 0)), sem=("parallel", "parallel"))(q)


def gather_weights(W):
    shards = []
    for name, _, _ in EXCHANGED:
        w = W[name].astype(BF16)
        if name == "gdn_w_in":
            w = jnp.pad(w, ((0, 0), (0, 0), (0, GDN_IN_SLOT - GDN_IN_SHARD)))
        if name == "kv_w":
            w = jnp.pad(w, ((0, 0), (0, KV_PAD_WIDTH - _KV_WIDTH)))[None]
        shards.append(w)
    fulls = dict(zip([n for n, _, _ in EXCHANGED], gather_shards(shards, [(sa, ha) for _, sa, ha in EXCHANGED])))
    slots = fulls["gdn_w_in"]
    fulls["gdn_w_in"] = jnp.concatenate([slots[..., k * GDN_IN_SLOT:k * GDN_IN_SLOT + GDN_IN_SHARD] for k in range(N_CHIPS)], axis=-1)
    fulls["kv_w_pad"] = fulls.pop("kv_w").reshape(-1, KV_PAD_WIDTH)
    conv = pack([W["gdn_conv"]], F32, row_multiple=8)
    conv_all = all_gather8(conv, name="gather_conv").reshape(N_DEV, conv.shape[0], PACK_COLS)
    fulls["gdn_conv"] = jnp.concatenate([unpack(conv_all[2 * k], [W["gdn_conv"].shape])[0] for k in range(N_CHIPS)], axis=-1)
    return fulls


def reduce_gradients(G):
    layout = [(sa, ha) for _, sa, ha in EXCHANGED]
    c_arr = lax.axis_index("c").astype(jnp.int32).reshape(1)
    received = swap_other_halves(G, layout)
    pairs = [add_own_half(g, r, ha, c_arr, name="grad_pair_sum") for g, r, (_, ha) in zip(G, received, layout)]
    slots = scatter_to_chips(pairs, layout)
    halves = [sum_slots(q, name="grad_chip_sum") for q in slots]
    others = swap_with_sibling(halves)
    return [join_halves(h, o, ha, c_arr, name="grad_join_halves") for h, o, (_, ha) in zip(halves, others, layout)]


def allreduce_small(G, names):
    packed = pack([G[n] for n in names], F32, row_multiple=8)
    gathered = all_gather8(packed, name="gather_small_grads").reshape(N_DEV, packed.shape[0], PACK_COLS)
    total = sum_leading(gathered, name="sum_small_grads")
    return dict(zip(names, unpack(total, [G[n].shape for n in names])))


def kernel(x, mem, *rest):
    n_w = len(WEIGHT_NAMES)
    W = dict(zip(WEIGHT_NAMES, rest[:n_w]))
    target = rest[n_w]
    M = dict(zip(WEIGHT_NAMES, rest[n_w + 1:2 * n_w + 1]))
    V = dict(zip(WEIGHT_NAMES, rest[2 * n_w + 1:3 * n_w + 1]))

    full = gather_weights(W)
    P = {n: W[n] for n in REPLICATED}
    P.update({n: full[n] for n in ("ffn1_w_gate_up", "ffn1_w_down", "ffn2_w_gate_up", "ffn2_w_down", "fox_w_in", "w_out",
                                   "mem_w_kv", "kv_w_pad")})
    derived = derived_weights(full["gdn_w_in"], full["gdn_conv"])
    P.update(gdn_w_in_pad=derived["gdn_w_in_pad"], conv8=derived["conv8"])

    part, dx, G = forward_backward(x[0], mem[0], target[0], P)
    loss = lax.psum(0.5 / x.shape[-1] * jnp.sum(part), ("x", "y", "c"))

    ref = reference_layout(G)
    exchange = {n: ref[n] for n, _, _ in EXCHANGED}
    exchange["gdn_w_in"] = jnp.concatenate(
        [jnp.pad(ref["gdn_w_in"][..., k * GDN_IN_SHARD:(k + 1) * GDN_IN_SHARD], ((0, 0), (0, 0), (0, GDN_IN_SLOT - GDN_IN_SHARD)))
         for k in range(N_CHIPS)], axis=-1)
    exchange["kv_w"] = G["kv_w_pad"].reshape(N_CHIPS, -1, KV_PAD_WIDTH)
    shards = dict(zip([n for n, _, _ in EXCHANGED], reduce_gradients([exchange[n] for n, _, _ in EXCHANGED])))
    shards["gdn_w_in"] = shards["gdn_w_in"][..., :GDN_IN_SHARD]
    shards["kv_w"] = shards["kv_w"][0, :, :_KV_WIDTH]
    grads = allreduce_small(ref, REPLICATED + ["gdn_conv"])
    conv_cols = W["gdn_conv"].shape[-1]
    chip = 2 * lax.axis_index("x") + lax.axis_index("y")
    grads["gdn_conv"] = lax.dynamic_slice_in_dim(grads["gdn_conv"], chip * conv_cols, conv_cols, axis=2)
    grads.update(shards)

    outs = {n: adamw(W[n], grads[n], M[n], V[n], name="adamw_" + n) for n in WEIGHT_NAMES}
    return (loss, dx[None], *[grads[n] for n in WEIGHT_NAMES], *[outs[n][0] for n in WEIGHT_NAMES],
            *[outs[n][1] for n in WEIGHT_NAMES], *[outs[n][2] for n in WEIGHT_NAMES])
```

```python
import jax
import jax.numpy as jnp
from jax import lax
from jax.experimental import pallas as pl
from jax.experimental.pallas import tpu as pltpu

F32, BF16 = jnp.float32, jnp.bfloat16
HI = lax.Precision.HIGHEST
MESH = pl.DeviceIdType.MESH

VMEM_LIMIT_BYTES = 48 * 1024 * 1024
LANES = 128
EPS = 1e-6
NEG_INF = -1e30

D_MODEL = 1024
HEAD_DIM = 128
GDN_HEADS = 6
GDN_WIDTH = GDN_HEADS * HEAD_DIM
FOX_HEADS = 6
FOX_WIDTH = FOX_HEADS * HEAD_DIM
MEM_HEADS = 4
MEM_HEAD_DIM = 64
MEM_WIDTH = MEM_HEADS * MEM_HEAD_DIM
FFN_HIDDEN = 2816
CONV_WIDTH = 4
GDN_CHUNK = 128
N_CHIPS = 4
N_DEV = 8

ADAM_LR, ADAM_B1, ADAM_B2, ADAM_EPS, ADAM_WD, ADAM_STEP = 0.001, 0.9, 0.999, 1e-08, 0.01, 10


def _pcall(body, *, name, out_shape, grid=(), in_specs=None, out_specs=None, scratch=(), sem=None):
    params = dict(vmem_limit_bytes=VMEM_LIMIT_BYTES)
    if sem is not None:
        params["dimension_semantics"] = sem
    kw = dict(grid=grid, in_specs=in_specs, out_specs=out_specs) if grid else {}
    return pl.pallas_call(body, name=name, out_shape=out_shape, scratch_shapes=list(scratch),
                          compiler_params=pltpu.CompilerParams(**params), **kw)


def _pick(n, cands):
    for c in cands:
        if n % c == 0:
            return c
    return n


def _make_dot(dtype, precision):
    def raw(a, b, dims):
        return lax.dot_general(a.astype(dtype), b.astype(dtype), (dims, ((), ())),
                               precision=precision, preferred_element_type=F32)

    @jax.custom_vjp
    def dot(a, b):
        return raw(a, b, ((1,), (0,)))

    def fwd(a, b):
        return dot(a, b), (a, b)

    def bwd(resid, ct):
        a, b = resid
        return raw(ct, b, ((1,), (1,))).astype(a.dtype), raw(a, ct, ((0,), (0,))).astype(b.dtype)

    dot.defvjp(fwd, bwd)
    dot.nn = lambda a, b: raw(a, b, ((1,), (0,)))
    dot.nt = lambda a, b: raw(a, b, ((1,), (1,)))
    dot.tn = lambda a, b: raw(a, b, ((0,), (0,)))
    return dot


bdot = _make_dot(BF16, None)
fdot = _make_dot(F32, HI)
idot = _make_dot(F32, lax.Precision.HIGH)
sdot = idot


def _sigmoid(x):
    return jax.nn.sigmoid(x)


def _silu(x):
    return x * _sigmoid(x)


def _softplus(x):
    return jnp.maximum(x, 0.0) + jnp.log(1.0 + jnp.exp(-jnp.abs(x)))


def _log_sigmoid(x):
    return -_softplus(-x)


def _iota2(shape, dim):
    return lax.broadcasted_iota(jnp.int32, shape, dim)


def mm(a, b, *, ta=False, tb=False, a_split=False, b_split=False, out_dtype=F32, scale=1.0, res=None, name):
    assert not (a_split and ta) and not (b_split and tb)
    (K, M) = a.shape if ta else ((2 * a.shape[2], a.shape[1]) if a_split else a.shape[::-1])
    (N, Kb) = b.shape if tb else ((2 * b.shape[2], b.shape[1]) if b_split else b.shape[::-1])
    assert K == Kb, (a.shape, b.shape, ta, tb)
    tm = _pick(M, (1024, 1408, 512, 256, 128))
    tn = _pick(N, (1024, 1408, 1152, 1664, 768, 512, 384, 256, 128))
    tk = _pick(K, (1024, 512, 256, 128)) if ta else _pick(K, (1024, 1408, 1152, 1664, 512, 256, 128))
    assert not a_split or (K // 2) % tk == 0
    assert not b_split or (N // 2) % tn == 0
    nk = K // tk
    dims = (((0 if ta else 1,), (1 if tb else 0,)), ((), ()))

    def body(a_ref, b_ref, *rest):
        o_ref, acc = rest[-2], rest[-1]
        k = pl.program_id(2)

        @pl.when(k == 0)
        def _():
            acc[...] = jnp.zeros_like(acc)

        acc[...] += lax.dot_general(a_ref[...].astype(BF16), b_ref[...].astype(BF16), dims,
                                    preferred_element_type=F32)

        @pl.when(k == nk - 1)
        def _():
            out = acc[...] * scale
            if res is not None:
                out = out + rest[0][...].astype(F32)
            o_ref[...] = out.astype(o_ref.dtype)

    a_spec = pl.BlockSpec((tk, tm), lambda i, j, k: (k, i)) if ta else pl.BlockSpec((tm, tk), lambda i, j, k: (i, k))
    b_spec = pl.BlockSpec((tn, tk), lambda i, j, k: (j, k)) if tb else pl.BlockSpec((tk, tn), lambda i, j, k: (k, j))
    if a_split:
        per_half = K // 2 // tk
        a_spec = pl.BlockSpec((None, tm, tk), lambda i, j, k: (k // per_half, i, k % per_half))
    if b_split:
        per_half = N // 2 // tn
        b_spec = pl.BlockSpec((None, tk, tn), lambda i, j, k: (j // per_half, k, j % per_half))
    o_spec = pl.BlockSpec((tm, tn), lambda i, j, k: (i, j))
    ins, specs = [a, b], [a_spec, b_spec]
    if res is not None:
        ins.append(res)
        specs.append(o_spec)
    return _pcall(body, name=name, out_shape=jax.ShapeDtypeStruct((M, N), out_dtype),
                  grid=(M // tm, N // tn, nk), in_specs=specs, out_specs=o_spec,
                  scratch=[pltpu.VMEM((tm, tn), F32)], sem=("parallel", "parallel", "arbitrary"))(*ins)


def _row_spec(r, tile):
    if isinstance(r, tuple):
        arr, width, col = r
        return arr, pl.BlockSpec((tile, width), lambda i, col=col: (i, col))
    return r, pl.BlockSpec((tile, r.shape[1]), lambda i: (i, 0))


def _const_spec(c):
    return pl.BlockSpec(c.shape, lambda i: (0,) * c.ndim)


def rowwise(fn, rows, consts, outs, *, tile, name):
    arrs, specs = zip(*[_row_spec(r, tile) for r in rows])
    n_rows = arrs[0].shape[0]
    tile = min(tile, n_rows)
    n_in = len(rows) + len(consts)

    def body(*refs):
        res = fn(*[r[...] for r in refs[:n_in]])
        for o_ref, o in zip(refs[n_in:], res):
            o_ref[...] = o.astype(o_ref.dtype)

    arrs, specs = zip(*[_row_spec(r, tile) for r in rows])
    return _pcall(body, name=name,
                  out_shape=[jax.ShapeDtypeStruct((n_rows, w), dt) for w, dt in outs],
                  grid=(n_rows // tile,),
                  in_specs=list(specs) + [_const_spec(c) for c in consts],
                  out_specs=[pl.BlockSpec((tile, w), lambda i: (i, 0)) for w, _ in outs],
                  sem=("parallel",))(*arrs, *consts)


def rowwise_bwd(fn, rows, consts, cts, *, tile, name, row_grads, const_grads, add=None):
    arrs, _ = zip(*[_row_spec(r, tile) for r in rows])
    n_rows = arrs[0].shape[0]
    tile = min(tile, n_rows)
    arrs, specs = zip(*[_row_spec(r, tile) for r in rows])
    ct_arrs, ct_specs = zip(*[_row_spec(c, tile) for c in cts])
    add = add or {}
    add_idx = sorted(add)
    add_arrs, add_specs = (zip(*[_row_spec(add[i], tile) for i in add_idx]) if add_idx else ((), ()))
    nr, nc, nct, na = len(rows), len(consts), len(cts), len(add_idx)
    want_rows = [i for i, d in enumerate(row_grads) if d is not None]
    want_consts = [i for i, w in enumerate(const_grads) if w]

    def body(*refs):
        row_v = [r[...] for r in refs[:nr]]
        const_v = [r[...] for r in refs[nr:nr + nc]]
        ct_v = [r[...] for r in refs[nr + nc:nr + nc + nct]]
        add_v = {i: refs[nr + nc + nct + j][...] for j, i in enumerate(add_idx)}
        out_refs = refs[nr + nc + nct + na:]
        res, vjp = jax.vjp(fn, *row_v, *const_v)
        grads = vjp(tuple(c.astype(o.dtype) for c, o in zip(ct_v, res)))
        for o_ref, i in zip(out_refs, want_rows):
            g = grads[i].astype(F32)
            if i in add_v:
                g = g + add_v[i].astype(F32)
            o_ref[...] = g.astype(o_ref.dtype)
        first = pl.program_id(0) == 0
        for o_ref, i in zip(out_refs[len(want_rows):], want_consts):
            g = grads[nr + i].astype(F32)

            @pl.when(first)
            def _(o_ref=o_ref, g=g):
                o_ref[...] = g

            @pl.when(jnp.logical_not(first))
            def _(o_ref=o_ref, g=g):
                o_ref[...] += g

    def width(r):
        return r[1] if isinstance(r, tuple) else r.shape[1]

    out_shape = [jax.ShapeDtypeStruct((n_rows, width(rows[i])), row_grads[i]) for i in want_rows]
    out_shape += [jax.ShapeDtypeStruct(consts[i].shape, F32) for i in want_consts]
    out_specs = [pl.BlockSpec((tile, width(rows[i])), lambda i_: (i_, 0)) for i in want_rows]
    out_specs += [_const_spec(consts[i]) for i in want_consts]
    return _pcall(body, name=name, out_shape=out_shape, grid=(n_rows // tile,),
                  in_specs=list(specs) + [_const_spec(c) for c in consts] + list(ct_specs) + list(add_specs),
                  out_specs=out_specs, sem=("arbitrary",))(*arrs, *consts, *ct_arrs, *add_arrs)


def f_rmsnorm(x, g):
    x = x.astype(F32)
    return (x * lax.rsqrt(jnp.mean(x * x, axis=-1, keepdims=True) + EPS) * g,)


def _head_sel(first_lane):
    r, c = _iota2((LANES, GDN_WIDTH), 0), _iota2((LANES, GDN_WIDTH), 1)
    return (r == c // HEAD_DIM + first_lane).astype(F32)


def _tri(n, strict=False):
    r, c = _iota2((n, n), 0), _iota2((n, n), 1)
    return r > c if strict else r >= c


def f_gdn_pre(xc, ab, a_log, dt_bias):
    s = _silu(xc.astype(F32))
    qs, ks = [], []
    for h in range(GDN_HEADS):
        qh = s[:, h * HEAD_DIM:(h + 1) * HEAD_DIM]
        kh = s[:, GDN_WIDTH + h * HEAD_DIM:GDN_WIDTH + (h + 1) * HEAD_DIM]
        qs.append(qh * lax.rsqrt(jnp.sum(qh * qh, axis=-1, keepdims=True) + EPS) * (HEAD_DIM ** -0.5))
        ks.append(kh * lax.rsqrt(jnp.sum(kh * kh, axis=-1, keepdims=True) + EPS))
    q, k = jnp.concatenate(qs, axis=1), jnp.concatenate(ks, axis=1)
    v = s[:, 2 * GDN_WIDTH:]
    ab = ab.astype(F32)
    g = -jnp.exp(a_log) * _softplus(ab + dt_bias)
    gc = fdot(_tri(GDN_CHUNK).astype(F32), fdot(g, _head_sel(0)))
    beta = fdot(_sigmoid(ab), _head_sel(GDN_HEADS))
    return q, k, v, gc, beta


def _unit_lower_inverse(neg_lower):
    C = neg_lower.shape[0]
    inv = (_iota2((C, C), 0) == _iota2((C, C), 1)).astype(F32) + neg_lower
    power = neg_lower
    for _ in range(6):
        power = idot.nn(power, power)
        inv = inv + idot.nn(power, inv)
    return inv


@jax.custom_vjp
def _solve_with_inverse(inv, neg_lower, rhs):
    return idot.nn(inv, rhs)


def _solve_fwd(inv, neg_lower, rhs):
    x = idot.nn(inv, rhs)
    return x, (inv, x)


def _solve_bwd(resid, ct):
    inv, x = resid
    d_rhs = idot.tn(inv, ct)
    return jnp.zeros_like(inv), idot.nt(d_rhs, x), d_rhs


_solve_with_inverse.defvjp(_solve_fwd, _solve_bwd)


def f_gdn_intra(q, k, v, gc, beta, inv=None):
    C = GDN_CHUNK
    causal, strict = _tri(C), _tri(C, strict=True)
    is_last = _iota2((C, HEAD_DIM), 0) == C - 1
    outs = [[] for _ in range(6 if inv is None else 5)]
    for h in range(GDN_HEADS):
        sl = slice(h * HEAD_DIM, (h + 1) * HEAD_DIM)
        qh, kh, vh, gh, bh = q[:, sl], k[:, sl], v[:, sl], gc[:, sl], beta[:, sl]
        gdiff = gh - gh.T
        decay = jnp.where(causal, jnp.exp(jnp.where(causal, gdiff, 0.0)), 0.0)
        kb = kh * bh
        neg_lower = jnp.where(strict, -(idot(kb, kh.T) * decay), 0.0)
        rhs = jnp.concatenate([vh * bh, kb * jnp.exp(gh)], axis=1)
        if inv is None:
            inv_h = _unit_lower_inverse(neg_lower)
            sol = idot.nn(inv_h, rhs)
        else:
            sol = _solve_with_inverse(inv[:, sl], neg_lower, rhs)
        qk = jnp.where(causal, idot(qh, kh.T) * decay, 0.0)
        g_last = jnp.sum(jnp.where(is_last, gh, 0.0), axis=0, keepdims=True)
        vals = (sol[:, :HEAD_DIM], sol[:, HEAD_DIM:], qk, kh * jnp.exp(g_last - gh), qh * jnp.exp(gh))
        for lst, val in zip(outs, vals + ((inv_h,) if inv is None else ())):
            lst.append(val)
    return tuple(jnp.concatenate(o, axis=1) for o in outs)


def f_gdn_post(o, z, gain):
    z = z.astype(F32)
    parts = []
    for h in range(GDN_HEADS):
        oh = o[:, h * HEAD_DIM:(h + 1) * HEAD_DIM]
        parts.append(oh * lax.rsqrt(jnp.mean(oh * oh, axis=-1, keepdims=True) + EPS) * gain)
    return (jnp.concatenate(parts, axis=1) * _silu(z),)


def f_mem_attn(q, k, v):
    q = q.astype(F32)
    lane_head = _iota2((1, MEM_WIDTH), 1) // MEM_HEAD_DIM
    out = jnp.zeros(q.shape, F32)
    kt = k.astype(F32).T
    for h in range(MEM_HEADS):
        mask = (lane_head == h).astype(F32)
        logits = bdot(q * mask, kt) * (MEM_HEAD_DIM ** -0.5)
        p = jnp.exp(logits - jnp.max(logits, axis=-1, keepdims=True))
        p = p / jnp.sum(p, axis=-1, keepdims=True)
        out = out + bdot(p, v) * mask
    return (out,)


def f_loss(y, t):
    d = y - t
    return (d * d,)


def conv_fwd(proj, w8, *, width, tile=512):
    n_rows = proj.shape[0]
    tile = min(tile, n_rows)

    def body(x_ref, halo_ref, w_ref, o_ref):
        i = pl.program_id(0)
        halo = jnp.where(i > 0, halo_ref[...].astype(F32), 0.0)
        xs = jnp.concatenate([halo, x_ref[...].astype(F32)], axis=0)
        acc = xs[8:] * w_ref[3:4, :]
        for j in range(CONV_WIDTH - 1):
            acc = acc + pltpu.roll(xs, CONV_WIDTH - 1 - j, 0)[8:] * w_ref[j:j + 1, :]
        o_ref[...] = acc

    return _pcall(body, name="gdn_conv_fwd", out_shape=jax.ShapeDtypeStruct((n_rows, width), F32),
                  grid=(n_rows // tile,),
                  in_specs=[pl.BlockSpec((tile, width), lambda i: (i, 0)),
                            pl.BlockSpec((8, width), lambda i: (jnp.maximum(i * (tile // 8) - 1, 0), 0)),
                            pl.BlockSpec((8, width), lambda i: (0, 0))],
                  out_specs=pl.BlockSpec((tile, width), lambda i: (i, 0)), sem=("parallel",))(proj, proj, w8)


def conv_bwd(proj, w8, dy, *, width, tile=512):
    n_rows = proj.shape[0]
    tile = min(tile, n_rows)
    n = n_rows // tile

    def body(x_ref, xhalo_ref, w_ref, dy_ref, dyhalo_ref, dx_ref, dw_ref):
        i = pl.program_id(0)
        dy = dy_ref[...]
        after = jnp.where(i < n - 1, dyhalo_ref[...], 0.0)
        ds = jnp.concatenate([dy, after], axis=0)
        dx = dy * w_ref[3:4, :]
        for j in range(CONV_WIDTH - 1):
            shift = CONV_WIDTH - 1 - j
            dx = dx + pltpu.roll(ds, tile + 8 - shift, 0)[:tile] * w_ref[j:j + 1, :]
        dx_ref[...] = dx.astype(dx_ref.dtype)
        halo = jnp.where(i > 0, xhalo_ref[...].astype(F32), 0.0)
        xs = jnp.concatenate([halo, x_ref[...].astype(F32)], axis=0)
        rows = [jnp.sum(dy * pltpu.roll(xs, CONV_WIDTH - 1 - j, 0)[8:], axis=0, keepdims=True)
                for j in range(CONV_WIDTH - 1)]
        rows.append(jnp.sum(dy * xs[8:], axis=0, keepdims=True))
        dw = jnp.concatenate(rows + [jnp.zeros((8 - CONV_WIDTH, width), F32)], axis=0)

        @pl.when(i == 0)
        def _():
            dw_ref[...] = dw

        @pl.when(i > 0)
        def _():
            dw_ref[...] += dw

    t8 = tile // 8
    return _pcall(body, name="gdn_conv_bwd",
                  out_shape=[jax.ShapeDtypeStruct((n_rows, width), BF16), jax.ShapeDtypeStruct((8, width), F32)],
                  grid=(n,),
                  in_specs=[pl.BlockSpec((tile, width), lambda i: (i, 0)),
                            pl.BlockSpec((8, width), lambda i: (jnp.maximum(i * t8 - 1, 0), 0)),
                            pl.BlockSpec((8, width), lambda i: (0, 0)),
                            pl.BlockSpec((tile, width), lambda i: (i, 0)),
                            pl.BlockSpec((8, width), lambda i: (jnp.minimum((i + 1) * t8, n * t8 - 1), 0))],
                  out_specs=[pl.BlockSpec((tile, width), lambda i: (i, 0)), pl.BlockSpec((8, width), lambda i: (0, 0))],
                  sem=("arbitrary",))(proj, proj, w8, dy, dy)


def gdn_scan_fwd(u, w, qk, kt, qh, gc):
    n_rows = u.shape[0]
    C, n = GDN_CHUNK, u.shape[0] // GDN_CHUNK

    def body(u_ref, w_ref, qk_ref, kt_ref, qh_ref, gc_ref, o_ref, vn_ref, sin_ref, st):
        @pl.when(pl.program_id(0) == 0)
        def _():
            st[...] = jnp.zeros_like(st)

        sin_ref[0] = st[...]
        for h in range(GDN_HEADS):
            sl = slice(h * HEAD_DIM, (h + 1) * HEAD_DIM)
            s = st[sl, :]
            v_new = u_ref[:, sl] - sdot(w_ref[:, sl], s)
            o_ref[:, sl] = sdot(qh_ref[:, sl], s) + sdot(qk_ref[:, sl], v_new)
            vn_ref[:, sl] = v_new
            st[sl, :] = s * jnp.exp(gc_ref[C - 1:C, sl]) + sdot.tn(kt_ref[:, sl], v_new)

    blk = pl.BlockSpec((C, GDN_WIDTH), lambda i: (i, 0))
    return _pcall(body, name="gdn_scan_fwd",
                  out_shape=[jax.ShapeDtypeStruct((n_rows, GDN_WIDTH), F32), jax.ShapeDtypeStruct((n_rows, GDN_WIDTH), F32),
                             jax.ShapeDtypeStruct((n, GDN_WIDTH, HEAD_DIM), F32)],
                  grid=(n,), in_specs=[blk] * 6,
                  out_specs=[blk, blk, pl.BlockSpec((1, GDN_WIDTH, HEAD_DIM), lambda i: (i, 0, 0))],
                  scratch=[pltpu.VMEM((GDN_WIDTH, HEAD_DIM), F32)], sem=("arbitrary",))(u, w, qk, kt, qh, gc)


def gdn_scan_bwd(do, w, qk, kt, qh, gc, vn, sin):
    n_rows = do.shape[0]
    C, n = GDN_CHUNK, do.shape[0] // GDN_CHUNK

    def body(do_ref, w_ref, qk_ref, kt_ref, qh_ref, gc_ref, vn_ref, sin_ref,
             du_ref, dw_ref, dqk_ref, dkt_ref, dqh_ref, dgl_ref, dst):
        @pl.when(pl.program_id(0) == 0)
        def _():
            dst[...] = jnp.zeros_like(dst)

        for h in range(GDN_HEADS):
            sl = slice(h * HEAD_DIM, (h + 1) * HEAD_DIM)
            s, ds_out, d_o, v_new = sin_ref[0, sl, :], dst[sl, :], do_ref[:, sl], vn_ref[:, sl]
            e = jnp.exp(gc_ref[C - 1:C, sl])
            dvn = sdot.tn(qk_ref[:, sl], d_o) + sdot(kt_ref[:, sl], ds_out)
            du_ref[:, sl] = dvn
            dw_ref[:, sl] = -sdot.nt(dvn, s)
            dqk_ref[:, sl] = sdot.nt(d_o, v_new)
            dkt_ref[:, sl] = sdot.nt(v_new, ds_out)
            dqh_ref[:, sl] = sdot.nt(d_o, s)
            dgl = jnp.sum(ds_out * s, axis=0, keepdims=True) * e
            dgl_ref[:, sl] = jnp.broadcast_to(dgl, (8, HEAD_DIM))
            dst[sl, :] = sdot.tn(qh_ref[:, sl], d_o) + e * ds_out - sdot.tn(w_ref[:, sl], dvn)

    blk = pl.BlockSpec((C, GDN_WIDTH), lambda i: (n - 1 - i, 0))
    row = jax.ShapeDtypeStruct((n_rows, GDN_WIDTH), F32)
    return _pcall(body, name="gdn_scan_bwd",
                  out_shape=[row] * 5 + [jax.ShapeDtypeStruct((n * 8, GDN_WIDTH), F32)],
                  grid=(n,), in_specs=[blk] * 7 + [pl.BlockSpec((1, GDN_WIDTH, HEAD_DIM), lambda i: (n - 1 - i, 0, 0))],
                  out_specs=[blk] * 5 + [pl.BlockSpec((8, GDN_WIDTH), lambda i: (n - 1 - i, 0))],
                  scratch=[pltpu.VMEM((GDN_WIDTH, HEAD_DIM), F32)], sem=("arbitrary",))(do, w, qk, kt, qh, gc, vn, sin)


def gdn_intra_bwd(q, k, v, gc, beta, inv, cts, dgl):
    n_rows = q.shape[0]
    C = GDN_CHUNK

    def body(*refs):
        ins = [r[...] for r in refs[:5]]
        inv_v = refs[5][...]
        ct = tuple(r[...] for r in refs[6:11])
        dgl_v = refs[11][...]
        _, vjp = jax.vjp(lambda *a: f_gdn_intra(*a, inv=inv_v), *ins)
        grads = list(vjp(ct))
        last = _iota2((C, GDN_WIDTH), 0) == C - 1
        grads[3] = grads[3] + jnp.where(last, jnp.broadcast_to(dgl_v[0:1, :], (C, GDN_WIDTH)), 0.0)
        for o_ref, g in zip(refs[12:], grads):
            o_ref[...] = g

    blk = pl.BlockSpec((C, GDN_WIDTH), lambda i: (i, 0))
    return _pcall(body, name="gdn_intra_bwd", out_shape=[jax.ShapeDtypeStruct((n_rows, GDN_WIDTH), F32)] * 5,
                  grid=(n_rows // C,), in_specs=[blk] * 11 + [pl.BlockSpec((8, GDN_WIDTH), lambda i: (i, 0))],
                  out_specs=[blk] * 5, sem=("parallel",))(q, k, v, gc, beta, inv, *cts, dgl)


def fox_gate_fwd(f, b_f):
    n_rows = f.shape[0]
    T = LANES

    def body(f_ref, b_ref, cb_ref, carry):
        @pl.when(pl.program_id(0) == 0)
        def _():
            carry[...] = jnp.zeros_like(carry)

        c = fdot(_tri(T).astype(F32), _log_sigmoid(f_ref[...] + b_ref[...])) + carry[...]
        carry[...] = c[T - 1:T, :]
        cb_ref[...] = fdot(c, _head_sel(0))

    return _pcall(body, name="fox_gate_fwd", out_shape=jax.ShapeDtypeStruct((n_rows, FOX_WIDTH), F32),
                  grid=(n_rows // T,),
                  in_specs=[pl.BlockSpec((T, LANES), lambda i: (i, 0)), pl.BlockSpec((1, LANES), lambda i: (0, 0))],
                  out_specs=pl.BlockSpec((T, FOX_WIDTH), lambda i: (i, 0)),
                  scratch=[pltpu.VMEM((1, LANES), F32)], sem=("arbitrary",))(f, b_f)


def fox_gate_bwd(f, b_f, dcrow, dcb):
    n_rows = f.shape[0]
    T = LANES
    n = n_rows // T

    def body(f_ref, b_ref, dc_ref, dcb_ref, df_ref, db_ref, carry):
        i = pl.program_id(0)

        @pl.when(i == 0)
        def _():
            carry[...] = jnp.zeros_like(carry)

        rows = [dc_ref[h] for h in range(FOX_HEADS)] + [jnp.zeros((T - FOX_HEADS, T), F32)]
        first_lane = (_iota2((FOX_WIDTH, LANES), 0) == _iota2((FOX_WIDTH, LANES), 1) * HEAD_DIM).astype(F32)
        dc = jnp.concatenate(rows, axis=0).T + fdot(dcb_ref[...], first_lane)
        dlog = fdot.tn(_tri(T).astype(F32), dc) + carry[...]
        carry[...] = dlog[0:1, :]
        df = dlog * (1.0 - _sigmoid(f_ref[...] + b_ref[...]))
        df_ref[...] = df
        db = jnp.sum(df, axis=0, keepdims=True)

        @pl.when(i == 0)
        def _():
            db_ref[...] = db

        @pl.when(i > 0)
        def _():
            db_ref[...] += db

    return _pcall(body, name="fox_gate_bwd",
                  out_shape=[jax.ShapeDtypeStruct((n_rows, LANES), F32), jax.ShapeDtypeStruct((1, LANES), F32)],
                  grid=(n,),
                  in_specs=[pl.BlockSpec((T, LANES), lambda i: (n - 1 - i, 0)), pl.BlockSpec((1, LANES), lambda i: (0, 0)),
                            pl.BlockSpec((FOX_HEADS, 1, T), lambda i: (0, 0, n - 1 - i)),
                            pl.BlockSpec((T, FOX_WIDTH), lambda i: (n - 1 - i, 0))],
                  out_specs=[pl.BlockSpec((T, LANES), lambda i: (n - 1 - i, 0)), pl.BlockSpec((1, LANES), lambda i: (0, 0))],
                  scratch=[pltpu.VMEM((1, LANES), F32)], sem=("arbitrary",))(f, b_f, dcrow, dcb)


FOX_AUG = 2 * HEAD_DIM


def _fox_tiles(n_rows):
    return min(1024, n_rows), min(512, n_rows)


def _fox_pairs(n_rows, query_major):
    tq, tk = _fox_tiles(n_rows)
    nq, r = n_rows // tq, tq // tk
    if query_major:
        pairs = [(i, j) for i in range(nq) for j in range(r * (i + 1))]
    else:
        pairs = [(i, j) for j in range(nq * r) for i in range(j // r, nq)]
    return jnp.asarray([p[0] for p in pairs], jnp.int32), jnp.asarray([p[1] for p in pairs], jnp.int32)


def fox_augment(x, cb, query_side):
    def fn(xt, ct):
        xt = xt.astype(F32)
        lane = _iota2((xt.shape[0], HEAD_DIM), 1)
        parts = []
        for h in range(FOX_HEADS):
            sl = slice(h * HEAD_DIM, (h + 1) * HEAD_DIM)
            c = ct[:, sl]
            hi = c.astype(BF16).astype(F32)
            mid = (c - hi).astype(BF16).astype(F32)
            lo = (c - hi - mid).astype(BF16).astype(F32)
            terms = jnp.where(lane % 3 == 0, hi, jnp.where(lane % 3 == 1, mid, lo))
            if query_side:
                extra = jnp.where(lane < 3, terms, jnp.where(lane < 6, 1.0, 0.0))
                parts += [xt[:, sl] * (HEAD_DIM ** -0.5), extra]
            else:
                extra = jnp.where(lane < 3, 1.0, jnp.where(lane < 6, -terms, 0.0))
                parts += [xt[:, sl], extra]
        return (jnp.concatenate(parts, axis=1),)

    return rowwise(fn, [(x, FOX_WIDTH, 0), cb], [], [(FOX_HEADS * FOX_AUG, BF16)], tile=ROW_TILE,
                   name="fox_augment_q" if query_side else "fox_augment_k")[0]


def _pcall_tables(body, *, name, out_shape, grid, tables, in_specs, out_specs, scratch, sem):
    spec = pltpu.PrefetchScalarGridSpec(num_scalar_prefetch=len(tables), grid=grid, in_specs=in_specs, out_specs=out_specs,
                                        scratch_shapes=list(scratch))
    return pl.pallas_call(body, name=name, out_shape=out_shape, grid_spec=spec,
                          compiler_params=pltpu.CompilerParams(vmem_limit_bytes=VMEM_LIMIT_BYTES, dimension_semantics=sem))


def _fox_logits(qa, ka, offset):
    s = bdot.nt(qa, ka)
    if offset is not None:
        s = jnp.where(_iota2(s.shape, 0) + offset >= _iota2(s.shape, 1), s, NEG_INF)
    return s


def _fox_p_ds(offset, qa_ref, ka_ref, v_ref, o_ref, lse_ref, do_ref):
    s = _fox_logits(qa_ref[...], ka_ref[...], offset)
    p = jnp.exp(s - jnp.tile(lse_ref[...], (1, s.shape[1] // LANES)))
    d_o = do_ref[...].astype(F32)
    delta = jnp.sum(d_o * o_ref[...].astype(F32), axis=-1, keepdims=True)
    return p, p * (bdot.nt(d_o, v_ref[...]) - delta), d_o


def _fox_on_diagonal(i, j, r, tk, step):
    @pl.when(j < r * i)
    def _():
        step(None)

    for m in range(r):
        @pl.when(j == r * i + m)
        def _(m=m):
            step(-m * tk)


def _fox_specs(tq, tk, do_col):
    qaspec = pl.BlockSpec((tq, FOX_AUG), lambda h, p, it, jt: (it[p], h))
    qspec = pl.BlockSpec((tq, HEAD_DIM), lambda h, p, it, jt: (it[p], h))
    dospec = pl.BlockSpec((tq, HEAD_DIM), lambda h, p, it, jt: (it[p], do_col + h))
    kaspec = pl.BlockSpec((tk, FOX_AUG), lambda h, p, it, jt: (jt[p], h))
    kspec = pl.BlockSpec((tk, HEAD_DIM), lambda h, p, it, jt: (jt[p], h))
    vspec = pl.BlockSpec((tk, HEAD_DIM), lambda h, p, it, jt: (jt[p], FOX_HEADS + h))
    cspec = pl.BlockSpec((1, 1, tk), lambda h, p, it, jt: (h, 0, jt[p]))
    return qaspec, qspec, dospec, kaspec, kspec, vspec, cspec


def fox_fwd(qa, kv, ka):
    n_rows = kv.shape[0]
    tq, tk = _fox_tiles(n_rows)
    r = tq // tk
    tables = _fox_pairs(n_rows, True)

    def body(it, jt, qa_ref, ka_ref, v_ref, o_ref, lse_ref, m_sc, l_sc, acc):
        i, j = it[pl.program_id(1)], jt[pl.program_id(1)]

        @pl.when(j == 0)
        def _():
            m_sc[...] = jnp.full(m_sc.shape, NEG_INF, F32)
            l_sc[...] = jnp.zeros_like(l_sc)
            acc[...] = jnp.zeros_like(acc)

        def step(offset):
            s = _fox_logits(qa_ref[...], ka_ref[...], offset)
            m_old = m_sc[...]
            m_new = jnp.maximum(m_old, jnp.max(s, axis=-1, keepdims=True))
            alpha = jnp.exp(m_old - m_new)
            p = jnp.exp(s - jnp.tile(m_new, (1, tk // LANES)))
            l_sc[...] = l_sc[...] * alpha + jnp.sum(p, axis=-1, keepdims=True)
            acc[...] = acc[...] * alpha + bdot(p, v_ref[...])
            m_sc[...] = m_new

        _fox_on_diagonal(i, j, r, tk, step)

        @pl.when(j == r * i + r - 1)
        def _():
            o_ref[...] = (acc[...] / l_sc[...]).astype(o_ref.dtype)
            lse_ref[...] = m_sc[...] + jnp.log(l_sc[...])

    qaspec, qspec, _, kaspec, _, vspec, _ = _fox_specs(tq, tk, 0)
    return _pcall_tables(body, name="fox_fwd",
                         out_shape=[jax.ShapeDtypeStruct((n_rows, FOX_WIDTH), BF16), jax.ShapeDtypeStruct((n_rows, FOX_WIDTH), F32)],
                         grid=(FOX_HEADS, tables[0].shape[0]), tables=tables,
                         in_specs=[qaspec, kaspec, vspec], out_specs=[qspec, qspec],
                         scratch=[pltpu.VMEM((tq, HEAD_DIM), F32)] * 3, sem=("parallel", "arbitrary"))(*tables, qa, ka, kv)


def fox_bwd_dq(qa, kv, ka, o, lse, do, prev=None):
    do, _, do_col = do
    do_col *= FOX_HEADS
    n_rows = kv.shape[0]
    tq, tk = _fox_tiles(n_rows)
    r = tq // tk
    n_prev = 0 if prev is None else 1
    tables = _fox_pairs(n_rows, True)

    def body(it, jt, qa_ref, ka_ref, k_ref, v_ref, o_ref, lse_ref, do_ref, *rest):
        dq_ref, drow_ref, acc, rows = rest[n_prev:]
        i, j = it[pl.program_id(1)], jt[pl.program_id(1)]

        @pl.when(j == 0)
        def _():
            acc[...] = jnp.zeros_like(acc)
            rows[...] = jnp.zeros_like(rows)

        def step(offset):
            _, ds, _ = _fox_p_ds(offset, qa_ref, ka_ref, v_ref, o_ref, lse_ref, do_ref)
            acc[...] += bdot(ds, k_ref[...])
            rows[...] += jnp.sum(ds, axis=-1, keepdims=True)

        _fox_on_diagonal(i, j, r, tk, step)

        @pl.when(j == r * i + r - 1)
        def _():
            dq_ref[...] = (acc[...] * (HEAD_DIM ** -0.5)).astype(dq_ref.dtype)
            drow_ref[...] = rows[...] + rest[0][...] if n_prev else rows[...]

    qaspec, qspec, dospec, kaspec, kspec, vspec, _ = _fox_specs(tq, tk, do_col)
    return _pcall_tables(body, name="fox_bwd_dq" + ("_acc" if n_prev else ""),
                         out_shape=[jax.ShapeDtypeStruct((n_rows, FOX_WIDTH), BF16), jax.ShapeDtypeStruct((n_rows, FOX_WIDTH), F32)],
                         grid=(FOX_HEADS, tables[0].shape[0]), tables=tables,
                         in_specs=[qaspec, kaspec, kspec, vspec, qspec, qspec, dospec] + [qspec] * n_prev,
                         out_specs=[qspec, qspec], scratch=[pltpu.VMEM((tq, HEAD_DIM), F32)] * 2,
                         sem=("parallel", "arbitrary"))(*tables, qa, ka, kv, kv, o, lse, do, *([prev] if n_prev else []))


def fox_bwd_dkv(qa, kv, ka, o, lse, do, prev=None):
    do, _, do_col = do
    do_col *= FOX_HEADS
    n_rows = kv.shape[0]
    tq, tk = _fox_tiles(n_rows)
    nq, r = n_rows // tq, tq // tk
    n_prev = 0 if prev is None else 3
    tables = _fox_pairs(n_rows, False)

    def body(it, jt, qa_ref, ka_ref, v_ref, o_ref, lse_ref, do_ref, *rest):
        prev_refs = rest[:n_prev]
        dk_ref, dv_ref, dc_ref, dk_acc, dv_acc, dc_acc = rest[n_prev:]
        i, j = it[pl.program_id(1)], jt[pl.program_id(1)]

        @pl.when(j >= r * i)
        def _():
            dk_acc[...] = jnp.zeros_like(dk_acc)
            dv_acc[...] = jnp.zeros_like(dv_acc)
            dc_acc[...] = jnp.zeros_like(dc_acc)

        def step(offset):
            p, ds, d_o = _fox_p_ds(offset, qa_ref, ka_ref, v_ref, o_ref, lse_ref, do_ref)
            dv_acc[...] += bdot.tn(p, d_o)
            dk_acc[...] += bdot.tn(ds, qa_ref[:, :HEAD_DIM])
            dc_acc[...] -= jnp.sum(ds, axis=0, keepdims=True)

        _fox_on_diagonal(i, j, r, tk, step)

        @pl.when(i == nq - 1)
        def _():
            dk, dv, dc = dk_acc[...], dv_acc[...], dc_acc[...]
            if n_prev:
                dk, dv, dc = dk + prev_refs[0][...], dv + prev_refs[1][...], dc + prev_refs[2][0]
            dk_ref[...] = dk
            dv_ref[...] = dv
            dc_ref[0] = dc

    qaspec, qspec, dospec, kaspec, kspec, vspec, cspec = _fox_specs(tq, tk, do_col)
    return _pcall_tables(body, name="fox_bwd_dkv" + ("_acc" if n_prev else ""),
                         out_shape=[jax.ShapeDtypeStruct((n_rows, FOX_WIDTH), F32), jax.ShapeDtypeStruct((n_rows, FOX_WIDTH), F32),
                                    jax.ShapeDtypeStruct((FOX_HEADS, 1, n_rows), F32)],
                         grid=(FOX_HEADS, tables[0].shape[0]), tables=tables,
                         in_specs=[qaspec, kaspec, vspec, qspec, qspec, dospec] + [kspec, kspec, cspec][:n_prev],
                         out_specs=[kspec, kspec, cspec],
                         scratch=[pltpu.VMEM((tk, HEAD_DIM), F32), pltpu.VMEM((tk, HEAD_DIM), F32), pltpu.VMEM((1, tk), F32)],
                         sem=("parallel", "arbitrary"))(*tables, qa, ka, kv, o, lse, do, *(prev or ()))


def loss_head(h, gain, target, *, tile=512):
    n_rows, d = h.shape
    tile = min(tile, n_rows)

    def body(h_ref, g_ref, t_ref, part_ref, dy_ref):
        (y,) = f_rmsnorm(h_ref[...], g_ref[...])
        diff = y - t_ref[...]
        dy_ref[...] = diff * (1.0 / d)
        part = jnp.sum(diff * diff, axis=0, keepdims=True)
        first = pl.program_id(0) == 0

        @pl.when(first)
        def _():
            part_ref[...] = part

        @pl.when(jnp.logical_not(first))
        def _():
            part_ref[...] += part

    blk = pl.BlockSpec((tile, d), lambda i: (i, 0))
    one = pl.BlockSpec((1, d), lambda i: (0, 0))
    return _pcall(body, name="loss_head",
                  out_shape=[jax.ShapeDtypeStruct((1, d), F32), jax.ShapeDtypeStruct((n_rows, d), F32)],
                  grid=(n_rows // tile,), in_specs=[blk, one, blk], out_specs=[one, blk], sem=("arbitrary",))(h, gain, target)


def adamw(w, g, m, v, *, name):
    shape = w.shape
    cols = shape[-1] if w.ndim >= 2 else w.size
    rows = w.size // cols
    tile = _pick(rows, (256, 128, 64, 32, 16, 8))
    as2d = lambda a: a.reshape(rows, cols)

    def body(w_ref, g_ref, m_ref, v_ref, d_ref, nm_ref, nv_ref):
        g_ = g_ref[...]
        m_ = ADAM_B1 * m_ref[...] + (1.0 - ADAM_B1) * g_
        v_ = ADAM_B2 * v_ref[...] + (1.0 - ADAM_B2) * (g_ * g_)
        m_hat = m_ / (1.0 - ADAM_B1 ** ADAM_STEP)
        v_hat = v_ / (1.0 - ADAM_B2 ** ADAM_STEP)
        d_ref[...] = -ADAM_LR * (m_hat / (jnp.sqrt(v_hat) + ADAM_EPS) + ADAM_WD * w_ref[...])
        nm_ref[...] = m_
        nv_ref[...] = v_

    blk = pl.BlockSpec((tile, cols), lambda i: (i, 0))
    outs = _pcall(body, name=name, out_shape=[jax.ShapeDtypeStruct((rows, cols), F32)] * 3, grid=(rows // tile,),
                  in_specs=[blk] * 4, out_specs=[blk] * 3, sem=("parallel",))(as2d(w), as2d(g), as2d(m), as2d(v))
    return tuple(o.reshape(shape) for o in outs)


def sum_leading(a, *, name):
    p, r, c = a.shape
    tile = _pick(r, (256, 128, 64, 32, 16, 8))

    def body(a_ref, o_ref):
        total = a_ref[0].astype(F32)
        for k in range(1, p):
            total = total + a_ref[k].astype(F32)
        o_ref[...] = total

    return _pcall(body, name=name, out_shape=jax.ShapeDtypeStruct((r, c), F32), grid=(r // tile,),
                  in_specs=[pl.BlockSpec((p, tile, c), lambda i: (0, i, 0))],
                  out_specs=pl.BlockSpec((tile, c), lambda i: (i, 0)), sem=("parallel",))(a)


_HBM = pl.BlockSpec(memory_space=pltpu.HBM)


def _comm_call(body, *, name, out_shape, n_in, scratch):
    return pl.pallas_call(body, name=name, out_shape=out_shape, in_specs=[_HBM] * n_in, out_specs=_HBM,
                          scratch_shapes=scratch,
                          compiler_params=pltpu.CompilerParams(has_side_effects=True))


def all_gather8(a, *, name):
    m_per, n = a.shape

    def body(x_ref, out_ref, send_sems, recv_sems, local_sem):
        x, y, c = lax.axis_index("x"), lax.axis_index("y"), lax.axis_index("c")
        me, sibling = (x, y, c), (x, y, 1 - c)
        chips = [(1 - x, y), (x, 1 - y), (1 - x, 1 - y)]

        def rows(px, py, pc):
            return out_ref.at[pl.ds((4 * px + 2 * py + pc) * m_per, m_per), :]

        def copy(k, block, to, src=None):
            return pltpu.make_async_remote_copy(
                src_ref=rows(*block) if src is None else src, dst_ref=rows(*block),
                send_sem=send_sems.at[k], recv_sem=recv_sems.at[k], device_id=to, device_id_type=MESH)

        mine = pltpu.make_async_copy(x_ref, rows(*me), local_sem)
        mine.start()
        first = [copy(0, me, sibling, src=x_ref)]
        first += [copy(1 + j, me, (*chip, c), src=x_ref) for j, chip in enumerate(chips)]
        for cp in first:
            cp.start()
        passed = [copy(4 + j, (*chip, c), sibling) for j, chip in enumerate(chips)]
        for j, chip in enumerate(chips):
            copy(1 + j, (*chip, c), me).wait_recv()
            passed[j].start()
        copy(0, sibling, me).wait_recv()
        for j, chip in enumerate(chips):
            copy(4 + j, (*chip, 1 - c), me).wait_recv()
        for cp in first + passed:
            cp.wait_send()
        mine.wait()

    return _comm_call(body, name=name, out_shape=jax.ShapeDtypeStruct((N_DEV * m_per, n), a.dtype), n_in=1,
                      scratch=[pltpu.SemaphoreType.DMA((7,)), pltpu.SemaphoreType.DMA((7,)), pltpu.SemaphoreType.DMA])(a)


PACK_COLS = 1024
PACK_ROW_MULTIPLE = 32

WEIGHT_NAMES = ["ffn1_norm", "ffn1_w_gate_up", "ffn1_w_down", "mix_norm", "ffn2_norm", "ffn2_w_gate_up", "ffn2_w_down",
                "gdn_w_in", "gdn_conv", "gdn_A_log", "gdn_dt_bias", "gdn_out_norm", "fox_w_in", "w_out", "mem_norm",
                "mem_w_kv", "kv_norm", "kv_w", "kv_b_f", "final_norm"]
SHARDED = [("ffn1_w_gate_up", 2), ("ffn1_w_down", 1), ("ffn2_w_gate_up", 2), ("ffn2_w_down", 1), ("gdn_w_in", 2),
           ("gdn_conv", 2), ("fox_w_in", 1), ("w_out", 1), ("mem_w_kv", 1), ("kv_w", 0)]
REPLICATED = [n for n in WEIGHT_NAMES if n not in dict(SHARDED)]


PACK_PIECE_ROWS = 16


def _rows_of(size):
    return -(-size // (PACK_COLS * PACK_PIECE_ROWS)) * PACK_PIECE_ROWS


def pack(pieces, dtype, row_multiple=PACK_ROW_MULTIPLE):
    bufs, total = [], 0
    for p in pieces:
        flat = p.astype(dtype).reshape(-1)
        rows = _rows_of(flat.size)
        bufs.append(jnp.pad(flat, (0, rows * PACK_COLS - flat.size)).reshape(rows, PACK_COLS))
        total += rows
    pad = -total % row_multiple
    if pad:
        bufs.append(jnp.zeros((pad, PACK_COLS), dtype))
    return jnp.concatenate(bufs, axis=0)


def unpack(buf, shapes):
    out, row = [], 0
    for shape in shapes:
        size = 1
        for s in shape:
            size *= s
        rows = _rows_of(size)
        out.append(buf[row:row + rows].reshape(-1)[:size].reshape(shape))
        row += rows
    return out


def _row(vec, width=None):
    vec = vec.astype(F32).reshape(1, -1)
    if width is not None and vec.shape[1] < width:
        vec = jnp.pad(vec, ((0, 0), (0, width - vec.shape[1])))
    return vec


ROW_TILE = 512
GDN_PROJ_WIDTH = 4 * GDN_WIDTH + MEM_WIDTH + LANES
GDN_Z_COL, GDN_QMEM_COL, GDN_AB_COL = 3, 4 * GDN_WIDTH // MEM_WIDTH, (4 * GDN_WIDTH + MEM_WIDTH) // LANES
FOX_QMEM_COL = FOX_WIDTH // MEM_WIDTH
KV_PAD_WIDTH = 2 * FOX_WIDTH + LANES


def rms_fwd(x, gain_row, out_dtype=BF16):
    return rowwise(f_rmsnorm, [x], [gain_row], [(x.shape[1], out_dtype)], tile=ROW_TILE, name="rms_fwd")[0]


def rms_bwd(x, gain_row, dy, dres=None):
    return rowwise_bwd(f_rmsnorm, [x], [gain_row], [dy], tile=ROW_TILE, name="rms_bwd", row_grads=[F32],
                       const_grads=[True], add=None if dres is None else {0: dres})


def _ffn_tiles(n_rows):
    return _pick(n_rows, (512, 256, 128)), _pick(FFN_HIDDEN, (1408, 256, 128))


def ffn_up_act(n, wgu):
    n_rows, d = n.shape
    tm, tn = _ffn_tiles(n_rows)
    nj = FFN_HIDDEN // tn

    def body(n_ref, wg_ref, wu_ref, gu_ref, act_ref):
        x = n_ref[...].astype(BF16)
        g = bdot.nn(x, wg_ref[...])
        u = bdot.nn(x, wu_ref[...])
        gu_ref[0] = g.astype(gu_ref.dtype)
        gu_ref[1] = u.astype(gu_ref.dtype)
        act_ref[...] = (_silu(g) * u).astype(act_ref.dtype)

    return _pcall(body, name="ffn_up_act",
                  out_shape=[jax.ShapeDtypeStruct((2, n_rows, FFN_HIDDEN), BF16), jax.ShapeDtypeStruct((n_rows, FFN_HIDDEN), BF16)],
                  grid=(nj, n_rows // tm),
                  in_specs=[pl.BlockSpec((tm, d), lambda j, i: (i, 0)), pl.BlockSpec((d, tn), lambda j, i: (0, j)),
                            pl.BlockSpec((d, tn), lambda j, i: (0, nj + j))],
                  out_specs=[pl.BlockSpec((2, tm, tn), lambda j, i: (0, i, j)), pl.BlockSpec((tm, tn), lambda j, i: (i, j))],
                  sem=("parallel", "parallel"))(n, wgu, wgu)


def ffn_down_dx_act(dh, wd, gu):
    n_rows, d = dh.shape
    tm, tn = _ffn_tiles(n_rows)

    def body(dh_ref, wd_ref, gu_ref, dgu_ref):
        dact = 0.5 * bdot.nt(dh_ref[...], wd_ref[...])
        gate, up = gu_ref[0].astype(F32), gu_ref[1].astype(F32)
        sg = _sigmoid(gate)
        dgu_ref[0] = (dact * up * (sg * (1.0 + gate * (1.0 - sg)))).astype(dgu_ref.dtype)
        dgu_ref[1] = (dact * (gate * sg)).astype(dgu_ref.dtype)

    blk = pl.BlockSpec((2, tm, tn), lambda j, i: (0, i, j))
    return _pcall(body, name="ffn_down_dx_act", out_shape=jax.ShapeDtypeStruct((2, n_rows, FFN_HIDDEN), BF16),
                  grid=(FFN_HIDDEN // tn, n_rows // tm),
                  in_specs=[pl.BlockSpec((tm, d), lambda j, i: (i, 0)), pl.BlockSpec((tn, d), lambda j, i: (j, 0)), blk],
                  out_specs=blk, sem=("parallel", "parallel"))(dh, wd, gu)


def ffn_fwd(h, gain_row, wgu, wd):
    n = rms_fwd(h, gain_row)
    gu, act = ffn_up_act(n, wgu)
    return mm(act, wd, scale=0.5, res=h, name="ffn_down"), (h, n, gu, act)


def ffn_bwd(dh, saved, gain_row, wgu, wd):
    h, n, gu, act = saved
    dgu = ffn_down_dx_act(dh, wd, gu)
    dwd = mm(act, dh, ta=True, scale=0.5, name="ffn_down_dw")
    dwgu = mm(n, dgu, ta=True, b_split=True, name="ffn_up_dw")
    dn = mm(dgu, wgu, tb=True, a_split=True, out_dtype=BF16, name="ffn_up_dx")
    dh, dgain = rms_bwd(h, gain_row, dn, dh)
    return dh, dwgu, dwd, dgain


def gdn_fwd(proj, w8, a_row, dt_row, onorm_row):
    wide = [(GDN_WIDTH, F32)] * 5
    xc = conv_fwd(proj, w8, width=3 * GDN_WIDTH)
    q, k, v, gc, beta = rowwise(f_gdn_pre, [xc, (proj, LANES, GDN_AB_COL)], [a_row, dt_row], wide, tile=GDN_CHUNK,
                                name="gdn_pre_fwd")
    u, w, qk, kt, qh, inv = rowwise(f_gdn_intra, [q, k, v, gc, beta], [], wide + wide[:1], tile=GDN_CHUNK,
                                    name="gdn_intra_fwd")
    o, vn, sin = gdn_scan_fwd(u, w, qk, kt, qh, gc)
    main = rowwise(f_gdn_post, [o, (proj, GDN_WIDTH, GDN_Z_COL)], [onorm_row], [(GDN_WIDTH, BF16)], tile=ROW_TILE,
                   name="gdn_post_fwd")[0]
    return main, (xc, q, k, v, gc, beta, inv, w, qk, kt, qh, vn, sin, o)


def gdn_bwd(dmain, proj, saved, w8, a_row, dt_row, onorm_row):
    xc, q, k, v, gc, beta, inv, w, qk, kt, qh, vn, sin, o = saved
    do, dz, donorm = rowwise_bwd(f_gdn_post, [o, (proj, GDN_WIDTH, GDN_Z_COL)], [onorm_row], [dmain], tile=ROW_TILE,
                                 name="gdn_post_bwd", row_grads=[F32, BF16], const_grads=[True])
    du, dw, dqk, dkt, dqh, dgl = gdn_scan_bwd(do, w, qk, kt, qh, gc, vn, sin)
    dq, dk, dv, dgc, dbeta = gdn_intra_bwd(q, k, v, gc, beta, inv, (du, dw, dqk, dkt, dqh), dgl)
    dxc, dab, da, ddt = rowwise_bwd(f_gdn_pre, [xc, (proj, LANES, GDN_AB_COL)], [a_row, dt_row], [dq, dk, dv, dgc, dbeta],
                                    tile=GDN_CHUNK, name="gdn_pre_bwd", row_grads=[F32, BF16], const_grads=[True, True])
    dqkv, dw8 = conv_bwd(proj, w8, dxc, width=3 * GDN_WIDTH)
    return dqkv, dz, dab, dw8, da, ddt, donorm


def mem_fwd(q, kmem, vmem):
    return rowwise(f_mem_attn, [q], [kmem, vmem], [(MEM_WIDTH, BF16)], tile=ROW_TILE, name="mem_attn_fwd")[0]


def mem_bwd(q, kmem, vmem, dout):
    return rowwise_bwd(f_mem_attn, [q], [kmem, vmem], [dout], tile=ROW_TILE, name="mem_attn_bwd", row_grads=[BF16],
                       const_grads=[True, True])


def forward_backward(xs, mems, target, P):
    depth, n_a = 4, 2
    G = {}
    mem_gain = _row(P["mem_norm"])
    mem_n = rms_fwd(mems, mem_gain)
    h = xs
    saved = []
    shared = None
    for l in range(depth):
        h0 = h
        h1, s1 = ffn_fwd(h0, _row(P["ffn1_norm"][l]), P["ffn1_w_gate_up"][l], P["ffn1_w_down"][l])
        u = rms_fwd(h1, _row(P["mix_norm"][l]))
        kvm = mm(mem_n, P["mem_w_kv"][l], name="mem_kv")
        kmem, vmem = kvm[:, :MEM_WIDTH], kvm[:, MEM_WIDTH:]
        if l < n_a:
            gp = (P["conv8"][l], _row(P["gdn_A_log"][l], LANES), _row(P["gdn_dt_bias"][l], LANES), _row(P["gdn_out_norm"][l]))
            proj = mm(u, P["gdn_w_in_pad"][l], name="gdn_in")
            main, sm = gdn_fwd(proj, *gp)
            qm = (proj, MEM_WIDTH, GDN_QMEM_COL)
        else:
            proj = mm(u, P["fox_w_in"][l - n_a], out_dtype=BF16, name="fox_in")
            kv, ka, cb = shared
            qa = fox_augment(proj, cb, True)
            main, lse = fox_fwd(qa, kv, ka)
            sm = (main, lse, qa)
            qm = (proj, MEM_WIDTH, FOX_QMEM_COL)
        mo = mem_fwd(qm, kmem, vmem)
        cat = jnp.concatenate([main, mo], axis=1)
        h2 = mm(cat, P["w_out"][l], res=h1, name="mix_out")
        h3, s2 = ffn_fwd(h2, _row(P["ffn2_norm"][l]), P["ffn2_w_gate_up"][l], P["ffn2_w_down"][l])
        saved.append((s1, h1, u, kmem, vmem, proj, sm, qm, cat, s2))
        h = h3
        if l == n_a - 1:
            nkv = rms_fwd(h, _row(P["kv_norm"]))
            kv = mm(nkv, P["kv_w_pad"][:, :2 * FOX_WIDTH], out_dtype=BF16, name="fox_kv")
            f = mm(nkv, P["kv_w_pad"][:, 2 * FOX_WIDTH:], name="fox_f")
            bf_row = _row(P["kv_b_f"], LANES)
            cb = fox_gate_fwd(f, bf_row)
            shared = (kv, fox_augment(kv, cb, False), cb)
            kv_saved = (h, nkv, f, bf_row)

    part, dy = loss_head(h, _row(P["final_norm"]), target)
    dh, G["final_norm"] = rms_bwd(h, _row(P["final_norm"]), dy)

    per_layer = {n: [None] * depth for n in ("ffn1_norm", "ffn1_w_gate_up", "ffn1_w_down", "mix_norm", "ffn2_norm",
                                             "ffn2_w_gate_up", "ffn2_w_down", "w_out", "mem_w_kv")}
    gdn_g = {n: [None] * n_a for n in ("gdn_w_in_pad", "conv8", "gdn_A_log", "gdn_dt_bias", "gdn_out_norm")}
    fox_g = [None] * (depth - n_a)
    dmem_n = None
    dkv_acc = dcb_acc = None
    for l in reversed(range(depth)):
        s1, h1, u, kmem, vmem, proj, sm, qm, cat, s2 = saved[l]
        if l == n_a - 1:
            hk, nkv, f, bf_row = kv_saved
            dk, dv, dcrow = dkv_acc
            df, dbf = fox_gate_bwd(f, bf_row, dcrow, dcb_acc)
            dp = jnp.concatenate([dk.astype(BF16), dv.astype(BF16), df.astype(BF16)], axis=1)
            G["kv_w_pad"] = mm(nkv, dp, ta=True, name="fox_kv_dw")
            G["kv_b_f"] = dbf
            dnkv = mm(dp, P["kv_w_pad"], tb=True, out_dtype=BF16, name="fox_kv_dx")
            dh, G["kv_norm"] = rms_bwd(hk, _row(P["kv_norm"]), dnkv, dh)
        dh, per_layer["ffn2_w_gate_up"][l], per_layer["ffn2_w_down"][l], per_layer["ffn2_norm"][l] = ffn_bwd(
            dh, s2, _row(P["ffn2_norm"][l]), P["ffn2_w_gate_up"][l], P["ffn2_w_down"][l])
        dcat = mm(dh, P["w_out"][l], tb=True, out_dtype=BF16, name="mix_out_dx")
        per_layer["w_out"][l] = mm(cat, dh, ta=True, name="mix_out_dw")
        dqm, dkm, dvm = mem_bwd(qm, kmem, vmem, (dcat, MEM_WIDTH, FOX_QMEM_COL))
        dkvm = jnp.concatenate([dkm, dvm], axis=1)
        per_layer["mem_w_kv"][l] = mm(mem_n, dkvm, ta=True, name="mem_kv_dw")
        dmem_n = mm(dkvm, P["mem_w_kv"][l], tb=True, res=dmem_n, name="mem_kv_dx")
        dmain = (dcat, GDN_WIDTH, 0)
        if l < n_a:
            gp = (P["conv8"][l], _row(P["gdn_A_log"][l], LANES), _row(P["gdn_dt_bias"][l], LANES), _row(P["gdn_out_norm"][l]))
            dqkv, dz, dab, gdn_g["conv8"][l], gdn_g["gdn_A_log"][l], gdn_g["gdn_dt_bias"][l], gdn_g["gdn_out_norm"][l] = gdn_bwd(
                dmain, proj, sm, *gp)
            dproj = jnp.concatenate([dqkv, dz, dqm, dab], axis=1)
            gdn_g["gdn_w_in_pad"][l] = mm(u, dproj, ta=True, name="gdn_in_dw")
            du = mm(dproj, P["gdn_w_in_pad"][l], tb=True, out_dtype=BF16, name="gdn_in_dx")
        else:
            o, lse, qa = sm
            kv, ka, _ = shared
            dq, dcb_acc = fox_bwd_dq(qa, kv, ka, o, lse, dmain, dcb_acc)
            dkv_acc = fox_bwd_dkv(qa, kv, ka, o, lse, dmain, dkv_acc)
            dproj = jnp.concatenate([dq, dqm], axis=1)
            fox_g[l - n_a] = mm(u, dproj, ta=True, name="fox_in_dw")
            du = mm(dproj, P["fox_w_in"][l - n_a], tb=True, out_dtype=BF16, name="fox_in_dx")
        dh, per_layer["mix_norm"][l] = rms_bwd(h1, _row(P["mix_norm"][l]), du, dh)
        dh, per_layer["ffn1_w_gate_up"][l], per_layer["ffn1_w_down"][l], per_layer["ffn1_norm"][l] = ffn_bwd(
            dh, s1, _row(P["ffn1_norm"][l]), P["ffn1_w_gate_up"][l], P["ffn1_w_down"][l])

    (G["mem_norm"],) = rowwise_bwd(f_rmsnorm, [mems], [mem_gain], [dmem_n], tile=ROW_TILE, name="mem_norm_bwd",
                                   row_grads=[None], const_grads=[True])
    for n, v in per_layer.items():
        G[n] = jnp.stack(v)
    for n, v in gdn_g.items():
        G[n] = jnp.stack(v)
    G["fox_w_in"] = jnp.stack(fox_g)
    return part, dh, G


_GDN_O0 = 4 * GDN_WIDTH
_GDN_O1 = _GDN_O0 + 2 * GDN_HEADS
_KV_WIDTH = 2 * FOX_WIDTH + FOX_HEADS


def derived_weights(gdn_w_in, gdn_conv, kv_w=None):
    zeros = jnp.zeros(gdn_w_in.shape[:-1] + (LANES - 2 * GDN_HEADS,), gdn_w_in.dtype)
    out = dict(
        gdn_w_in_pad=jnp.concatenate([gdn_w_in[..., :_GDN_O0], gdn_w_in[..., _GDN_O1:], gdn_w_in[..., _GDN_O0:_GDN_O1], zeros], axis=-1),
        conv8=jnp.pad(gdn_conv.astype(F32), ((0, 0), (0, 8 - CONV_WIDTH), (0, 0))))
    if kv_w is not None:
        out["kv_w_pad"] = jnp.pad(kv_w, ((0, 0), (0, KV_PAD_WIDTH - _KV_WIDTH)))
    return out


def reference_layout(G):
    gp = G["gdn_w_in_pad"]
    out = dict(G)
    out["gdn_w_in"] = jnp.concatenate([gp[..., :_GDN_O0], gp[..., _GDN_O0 + MEM_WIDTH:_GDN_O0 + MEM_WIDTH + 2 * GDN_HEADS],
                                       gp[..., _GDN_O0:_GDN_O0 + MEM_WIDTH]], axis=-1)
    out["gdn_conv"] = G["conv8"][:, :CONV_WIDTH]
    out["kv_w"] = G["kv_w_pad"][:, :_KV_WIDTH]
    out["gdn_A_log"] = G["gdn_A_log"][:, 0, :GDN_HEADS]
    out["gdn_dt_bias"] = G["gdn_dt_bias"][:, 0, :GDN_HEADS]
    out["gdn_out_norm"] = G["gdn_out_norm"][:, 0, :]
    out["kv_b_f"] = G["kv_b_f"][0, :FOX_HEADS]
    for n in ("ffn1_norm", "mix_norm", "ffn2_norm"):
        out[n] = G[n][:, 0, :]
    for n in ("mem_norm", "kv_norm", "final_norm"):
        out[n] = G[n][0]
    return {n: out[n] for n in WEIGHT_NAMES}


EXCHANGED = [("ffn1_w_gate_up", 2, 0), ("ffn1_w_down", 1, 0), ("ffn2_w_gate_up", 2, 0), ("ffn2_w_down", 1, 0),
             ("gdn_w_in", 2, 0), ("fox_w_in", 1, 0), ("w_out", 1, 0), ("mem_w_kv", 1, 0), ("kv_w", 0, 1)]
GDN_IN_SHARD = (4 * GDN_WIDTH + 2 * GDN_HEADS + MEM_WIDTH) // N_CHIPS
GDN_IN_SLOT = 896


def _slab(ref, axis_slices):
    idx = [slice(None)] * len(ref.shape)
    for axis, (start, size) in axis_slices.items():
        idx[axis] = pl.ds(start, size)
    return ref.at[tuple(idx)]


def _comm_multi(body, *, name, n_in, out_shapes, scratch):
    return pl.pallas_call(body, name=name, out_shape=out_shapes, in_specs=[_HBM] * n_in, out_specs=[_HBM] * len(out_shapes),
                          scratch_shapes=scratch, compiler_params=pltpu.CompilerParams(has_side_effects=True))


def gather_shards(shards, layout):
    n = len(shards)
    fulls = [tuple(d * (N_CHIPS if a == sa else 1) for a, d in enumerate(s.shape)) for s, (sa, _) in zip(shards, layout)]

    def body(*refs):
        ins, outs = refs[:n], refs[n:2 * n]
        send_sems, recv_sems, local_sems = refs[2 * n:]
        x, y, c = lax.axis_index("x"), lax.axis_index("y"), lax.axis_index("c")
        me, sibling = (x, y, c), (x, y, 1 - c)
        chips = [(1 - x, y), (x, 1 - y), (1 - x, 1 - y)]

        def region(w, px, py, pc):
            (sa, ha), shard = layout[w], shards[w].shape
            return _slab(outs[w], {sa: ((2 * px + py) * shard[sa], shard[sa]), ha: (pc * (shard[ha] // 2), shard[ha] // 2)})

        def my_half(w):
            ha, shard = layout[w][1], shards[w].shape
            return _slab(ins[w], {ha: (c * (shard[ha] // 2), shard[ha] // 2)})

        def copy(w, k, block, to, src=None):
            return pltpu.make_async_remote_copy(
                src_ref=region(w, *block) if src is None else src, dst_ref=region(w, *block),
                send_sem=send_sems.at[7 * w + k], recv_sem=recv_sems.at[7 * w + k], device_id=to, device_id_type=MESH)

        mine, first, passed = [], [], []
        for w in range(n):
            mine.append(pltpu.make_async_copy(my_half(w), region(w, *me), local_sems.at[w]))
            mine[w].start()
            first.append([copy(w, 0, me, sibling, src=my_half(w))]
                         + [copy(w, 1 + j, me, (*chip, c), src=my_half(w)) for j, chip in enumerate(chips)])
            for cp in first[w]:
                cp.start()
            passed.append([copy(w, 4 + j, (*chip, c), sibling) for j, chip in enumerate(chips)])
        for w in range(n):
            for j, chip in enumerate(chips):
                copy(w, 1 + j, (*chip, c), me).wait_recv()
                passed[w][j].start()
        for w in range(n):
            copy(w, 0, sibling, me).wait_recv()
            for j, chip in enumerate(chips):
                copy(w, 4 + j, (*chip, 1 - c), me).wait_recv()
        for w in range(n):
            for cp in first[w] + passed[w]:
                cp.wait_send()
            mine[w].wait()

    return _comm_multi(body, name="gather_shards", n_in=n,
                       out_shapes=[jax.ShapeDtypeStruct(f, s.dtype) for f, s in zip(fulls, shards)],
                       scratch=[pltpu.SemaphoreType.DMA((7 * n,)), pltpu.SemaphoreType.DMA((7 * n,)),
                                pltpu.SemaphoreType.DMA((n,))])(*shards)


def swap_other_halves(arrays, layout):
    n = len(arrays)
    halves = [tuple(d // 2 if a == ha else d for a, d in enumerate(g.shape)) for g, (_, ha) in zip(arrays, layout)]

    def body(*refs):
        ins, outs = refs[:n], refs[n:2 * n]
        send_sems, recv_sems = refs[2 * n:]
        x, y, c = lax.axis_index("x"), lax.axis_index("y"), lax.axis_index("c")
        copies = []
        for w in range(n):
            ha, size = layout[w][1], halves[w][layout[w][1]]
            copies.append(pltpu.make_async_remote_copy(
                src_ref=_slab(ins[w], {ha: ((1 - c) * size, size)}), dst_ref=outs[w], send_sem=send_sems.at[w],
                recv_sem=recv_sems.at[w], device_id=(x, y, 1 - c), device_id_type=MESH))
            copies[w].start()
        for cp in copies:
            cp.wait()

    return _comm_multi(body, name="grad_pair_swap", n_in=n,
                       out_shapes=[jax.ShapeDtypeStruct(h, g.dtype) for h, g in zip(halves, arrays)],
                       scratch=[pltpu.SemaphoreType.DMA((n,)), pltpu.SemaphoreType.DMA((n,))])(*arrays)


def scatter_to_chips(arrays, layout):
    n = len(arrays)
    slabs = [tuple(d // N_CHIPS if a == sa else d for a, d in enumerate(p.shape)) for p, (sa, _) in zip(arrays, layout)]

    def body(*refs):
        ins, outs = refs[:n], refs[n:2 * n]
        send_sems, recv_sems, local_sems = refs[2 * n:]
        x, y, c = lax.axis_index("x"), lax.axis_index("y"), lax.axis_index("c")
        me = 2 * x + y

        def slab(w, k):
            sa, size = layout[w][0], slabs[w][layout[w][0]]
            return _slab(ins[w], {sa: (k * size, size)})

        local, copies = [], []
        for w in range(n):
            local.append(pltpu.make_async_copy(slab(w, me), outs[w].at[me], local_sems.at[w]))
            local[w].start()
            for j, (px, py) in enumerate([(1 - x, y), (x, 1 - y), (1 - x, 1 - y)]):
                copies.append(pltpu.make_async_remote_copy(
                    src_ref=slab(w, 2 * px + py), dst_ref=outs[w].at[me], send_sem=send_sems.at[3 * w + j],
                    recv_sem=recv_sems.at[3 * w + j], device_id=(px, py, c), device_id_type=MESH))
                copies[-1].start()
        for cp in copies:
            cp.wait()
        for cp in local:
            cp.wait()

    return _comm_multi(body, name="grad_all_to_all", n_in=n,
                       out_shapes=[jax.ShapeDtypeStruct((N_CHIPS,) + s, p.dtype) for s, p in zip(slabs, arrays)],
                       scratch=[pltpu.SemaphoreType.DMA((3 * n,)), pltpu.SemaphoreType.DMA((3 * n,)),
                                pltpu.SemaphoreType.DMA((n,))])(*arrays)


def swap_with_sibling(arrays):
    n = len(arrays)

    def body(*refs):
        ins, outs = refs[:n], refs[n:2 * n]
        send_sems, recv_sems = refs[2 * n:]
        x, y, c = lax.axis_index("x"), lax.axis_index("y"), lax.axis_index("c")
        copies = [pltpu.make_async_remote_copy(src_ref=ins[w], dst_ref=outs[w], send_sem=send_sems.at[w], recv_sem=recv_sems.at[w],
                                               device_id=(x, y, 1 - c), device_id_type=MESH) for w in range(n)]
        for cp in copies:
            cp.start()
        for cp in copies:
            cp.wait()

    return _comm_multi(body, name="grad_half_swap", n_in=n, out_shapes=[jax.ShapeDtypeStruct(a.shape, a.dtype) for a in arrays],
                       scratch=[pltpu.SemaphoreType.DMA((n,)), pltpu.SemaphoreType.DMA((n,))])(*arrays)


def join_halves(mine, other, half_axis, c_arr, *, name):
    a0, a1, a2 = mine.shape
    tile = _row_tile(a1, a2)

    def body(c_ref, m_ref, o_ref, out_ref):
        for half in range(2):
            @pl.when(c_ref[0] == half)
            def _(half=half):
                out_ref[half] = m_ref[...]
                out_ref[1 - half] = o_ref[...]

    blk = pl.BlockSpec((None, tile, a2), lambda i, j, c_ref: (i, j, 0))
    if half_axis == 0:
        out_shape, out_blk = (2, a0, a1, a2), pl.BlockSpec((2, None, tile, a2), lambda i, j, c_ref: (0, i, j, 0))
    else:
        out_shape, out_blk = (a0, 2, a1, a2), pl.BlockSpec((None, 2, tile, a2), lambda i, j, c_ref: (i, 0, j, 0))
    spec = pltpu.PrefetchScalarGridSpec(num_scalar_prefetch=1, grid=(a0, a1 // tile), in_specs=[blk, blk], out_specs=out_blk)
    out = pl.pallas_call(body, name=name, out_shape=jax.ShapeDtypeStruct(out_shape, mine.dtype), grid_spec=spec,
                         compiler_params=pltpu.CompilerParams(vmem_limit_bytes=VMEM_LIMIT_BYTES,
                                                              dimension_semantics=("parallel", "parallel")))(c_arr, mine, other)
    return out.reshape((2 * a0, a1, a2) if half_axis == 0 else (a0, 2 * a1, a2))


def _row_tile(rows, cols, itemsize=4, budget=2 * 1024 * 1024):
    for t in (1024, 512, 256, 128, 64, 32, 16):
        if rows % t == 0 and t * cols * itemsize <= budget:
            return t
    return rows


def add_own_half(full, recv, half_axis, c_arr, *, name):
    a0, a1, a2 = recv.shape
    tile = _row_tile(a1, a2)

    def body(c_ref, f_ref, r_ref, o_ref):
        o_ref[...] = (f_ref[...] + r_ref[...]).astype(o_ref.dtype)

    if half_axis == 0:
        f_spec = pl.BlockSpec((None, tile, a2), lambda i, j, c_ref: (c_ref[0] * a0 + i, j, 0))
    else:
        f_spec = pl.BlockSpec((None, tile, a2), lambda i, j, c_ref: (i, c_ref[0] * (a1 // tile) + j, 0))
    blk = pl.BlockSpec((None, tile, a2), lambda i, j, c_ref: (i, j, 0))
    spec = pltpu.PrefetchScalarGridSpec(num_scalar_prefetch=1, grid=(a0, a1 // tile), in_specs=[f_spec, blk], out_specs=blk)
    return pl.pallas_call(body, name=name, out_shape=jax.ShapeDtypeStruct(recv.shape, BF16), grid_spec=spec,
                          compiler_params=pltpu.CompilerParams(vmem_limit_bytes=VMEM_LIMIT_BYTES,
                                                               dimension_semantics=("parallel", "parallel")))(c_arr, full, recv)


def sum_slots(q, *, name):
    _, a0, a1, a2 = q.shape
    tile = _row_tile(a1, a2, budget=1024 * 1024)

    def body(q_ref, o_ref):
        total = q_ref[0].astype(F32)
        for k in range(1, N_CHIPS):
            total = total + q_ref[k].astype(F32)
        o_ref[...] = total

    return _pcall(body, name=name, out_shape=jax.ShapeDtypeStruct((a0, a1, a2), F32), grid=(a0, a1 // tile),
                  in_specs=[pl.BlockSpec((N_CHIPS, None, tile, a2), lambda i, j: (0, i, j, 0))],
                  out_specs=pl.BlockSpec((None, tile, a2), lambda i, j: (i, j, 0)), sem=("parallel", "parallel"))(q)


def gather_weights(W):
    shards = []
    for name, _, _ in EXCHANGED:
        w = W[name].astype(BF16)
        if name == "gdn_w_in":
            w = jnp.pad(w, ((0, 0), (0, 0), (0, GDN_IN_SLOT - GDN_IN_SHARD)))
        if name == "kv_w":
            w = jnp.pad(w, ((0, 0), (0, KV_PAD_WIDTH - _KV_WIDTH)))[None]
        shards.append(w)
    fulls = dict(zip([n for n, _, _ in EXCHANGED], gather_shards(shards, [(sa, ha) for _, sa, ha in EXCHANGED])))
    slots = fulls["gdn_w_in"]
    fulls["gdn_w_in"] = jnp.concatenate([slots[..., k * GDN_IN_SLOT:k * GDN_IN_SLOT + GDN_IN_SHARD] for k in range(N_CHIPS)], axis=-1)
    fulls["kv_w_pad"] = fulls.pop("kv_w").reshape(-1, KV_PAD_WIDTH)
    conv = pack([W["gdn_conv"]], F32, row_multiple=8)
    conv_all = all_gather8(conv, name="gather_conv").reshape(N_DEV, conv.shape[0], PACK_COLS)
    fulls["gdn_conv"] = jnp.concatenate([unpack(conv_all[2 * k], [W["gdn_conv"].shape])[0] for k in range(N_CHIPS)], axis=-1)
    return fulls


def reduce_gradients(G):
    layout = [(sa, ha) for _, sa, ha in EXCHANGED]
    c_arr = lax.axis_index("c").astype(jnp.int32).reshape(1)
    received = swap_other_halves(G, layout)
    pairs = [add_own_half(g, r, ha, c_arr, name="grad_pair_sum") for g, r, (_, ha) in zip(G, received, layout)]
    slots = scatter_to_chips(pairs, layout)
    halves = [sum_slots(q, name="grad_chip_sum") for q in slots]
    others = swap_with_sibling(halves)
    return [join_halves(h, o, ha, c_arr, name="grad_join_halves") for h, o, (_, ha) in zip(halves, others, layout)]


def allreduce_small(G, names):
    packed = pack([G[n] for n in names], F32, row_multiple=8)
    gathered = all_gather8(packed, name="gather_small_grads").reshape(N_DEV, packed.shape[0], PACK_COLS)
    total = sum_leading(gathered, name="sum_small_grads")
    return dict(zip(names, unpack(total, [G[n].shape for n in names])))


def kernel(x, mem, *rest):
    n_w = len(WEIGHT_NAMES)
    W = dict(zip(WEIGHT_NAMES, rest[:n_w]))
    target = rest[n_w]
    M = dict(zip(WEIGHT_NAMES, rest[n_w + 1:2 * n_w + 1]))
    V = dict(zip(WEIGHT_NAMES, rest[2 * n_w + 1:3 * n_w + 1]))

    full = gather_weights(W)
    P = {n: W[n] for n in REPLICATED}
    P.update({n: full[n] for n in ("ffn1_w_gate_up", "ffn1_w_down", "ffn2_w_gate_up", "ffn2_w_down", "fox_w_in", "w_out",
                                   "mem_w_kv", "kv_w_pad")})
    derived = derived_weights(full["gdn_w_in"], full["gdn_conv"])
    P.update(gdn_w_in_pad=derived["gdn_w_in_pad"], conv8=derived["conv8"])

    part, dx, G = forward_backward(x[0], mem[0], target[0], P)
    loss = lax.psum(0.5 / x.shape[-1] * jnp.sum(part), ("x", "y", "c"))

    ref = reference_layout(G)
    exchange = {n: ref[n] for n, _, _ in EXCHANGED}
    exchange["gdn_w_in"] = jnp.concatenate(
        [jnp.pad(ref["gdn_w_in"][..., k * GDN_IN_SHARD:(k + 1) * GDN_IN_SHARD], ((0, 0), (0, 0), (0, GDN_IN_SLOT - GDN_IN_SHARD)))
         for k in range(N_CHIPS)], axis=-1)
    exchange["kv_w"] = G["kv_w_pad"].reshape(N_CHIPS, -1, KV_PAD_WIDTH)
    shards = dict(zip([n for n, _, _ in EXCHANGED], reduce_gradients([exchange[n] for n, _, _ in EXCHANGED])))
    shards["gdn_w_in"] = shards["gdn_w_in"][..., :GDN_IN_SHARD]
    shards["kv_w"] = shards["kv_w"][0, :, :_KV_WIDTH]
    grads = allreduce_small(ref, REPLICATED + ["gdn_conv"])
    conv_cols = W["gdn_conv"].shape[-1]
    chip = 2 * lax.axis_index("x") + lax.axis_index("y")
    grads["gdn_conv"] = lax.dynamic_slice_in_dim(grads["gdn_conv"], chip * conv_cols, conv_cols, axis=2)
    grads.update(shards)

    outs = {n: adamw(W[n], grads[n], M[n], V[n], name="adamw_" + n) for n in WEIGHT_NAMES}
    return (loss, dx[None], *[grads[n] for n in WEIGHT_NAMES], *[outs[n][0] for n in WEIGHT_NAMES],
            *[outs[n][1] for n in WEIGHT_NAMES], *[outs[n][2] for n in WEIGHT_NAMES])
```

```python
import jax
import jax.numpy as jnp
from jax import lax
from jax.experimental import pallas as pl
from jax.experimental.pallas import tpu as pltpu

F32, BF16 = jnp.float32, jnp.bfloat16
HI = lax.Precision.HIGHEST
MESH = pl.DeviceIdType.MESH

VMEM_LIMIT_BYTES = 48 * 1024 * 1024
LANES = 128
EPS = 1e-6
NEG_INF = -1e30

D_MODEL = 1024
HEAD_DIM = 128
GDN_HEADS = 6
GDN_WIDTH = GDN_HEADS * HEAD_DIM
FOX_HEADS = 6
FOX_WIDTH = FOX_HEADS * HEAD_DIM
MEM_HEADS = 4
MEM_HEAD_DIM = 64
MEM_WIDTH = MEM_HEADS * MEM_HEAD_DIM
FFN_HIDDEN = 2816
CONV_WIDTH = 4
GDN_CHUNK = 128
N_CHIPS = 4
N_DEV = 8

ADAM_LR, ADAM_B1, ADAM_B2, ADAM_EPS, ADAM_WD, ADAM_STEP = 0.001, 0.9, 0.999, 1e-08, 0.01, 10


def _pcall(body, *, name, out_shape, grid=(), in_specs=None, out_specs=None, scratch=(), sem=None):
    params = dict(vmem_limit_bytes=VMEM_LIMIT_BYTES)
    if sem is not None:
        params["dimension_semantics"] = sem
    kw = dict(grid=grid, in_specs=in_specs, out_specs=out_specs) if grid else {}
    return pl.pallas_call(body, name=name, out_shape=out_shape, scratch_shapes=list(scratch),
                          compiler_params=pltpu.CompilerParams(**params), **kw)


def _pick(n, cands):
    for c in cands:
        if n % c == 0:
            return c
    return n


def _make_dot(dtype, precision):
    def raw(a, b, dims):
        return lax.dot_general(a.astype(dtype), b.astype(dtype), (dims, ((), ())),
                               precision=precision, preferred_element_type=F32)

    @jax.custom_vjp
    def dot(a, b):
        return raw(a, b, ((1,), (0,)))

    def fwd(a, b):
        return dot(a, b), (a, b)

    def bwd(resid, ct):
        a, b = resid
        return raw(ct, b, ((1,), (1,))).astype(a.dtype), raw(a, ct, ((0,), (0,))).astype(b.dtype)

    dot.defvjp(fwd, bwd)
    dot.nn = lambda a, b: raw(a, b, ((1,), (0,)))
    dot.nt = lambda a, b: raw(a, b, ((1,), (1,)))
    dot.tn = lambda a, b: raw(a, b, ((0,), (0,)))
    return dot


bdot = _make_dot(BF16, None)
fdot = _make_dot(F32, HI)
idot = _make_dot(F32, lax.Precision.HIGH)
sdot = idot


def _sigmoid(x):
    return 0.5 * jnp.tanh(0.5 * x) + 0.5


def _silu(x):
    return x * _sigmoid(x)


def _softplus(x):
    return jnp.maximum(x, 0.0) + jnp.log(1.0 + jnp.exp(-jnp.abs(x)))


def _log_sigmoid(x):
    return -_softplus(-x)


def _iota2(shape, dim):
    return lax.broadcasted_iota(jnp.int32, shape, dim)


def mm(a, b, *, ta=False, tb=False, a_split=False, b_split=False, out_dtype=F32, scale=1.0, res=None, name):
    assert not (a_split and ta) and not (b_split and tb)
    (K, M) = a.shape if ta else ((2 * a.shape[2], a.shape[1]) if a_split else a.shape[::-1])
    (N, Kb) = b.shape if tb else ((2 * b.shape[2], b.shape[1]) if b_split else b.shape[::-1])
    assert K == Kb, (a.shape, b.shape, ta, tb)
    tm = _pick(M, (1024, 1408, 512, 256, 128))
    tn = _pick(N, (1024, 1408, 1152, 1664, 768, 512, 384, 256, 128))
    tk = _pick(K, (1024, 512, 256, 128)) if ta else _pick(K, (1024, 1408, 1152, 1664, 512, 256, 128))
    assert not a_split or (K // 2) % tk == 0
    assert not b_split or (N // 2) % tn == 0
    nk = K // tk
    dims = (((0 if ta else 1,), (1 if tb else 0,)), ((), ()))

    def body(a_ref, b_ref, *rest):
        o_ref, acc = rest[-2], rest[-1]
        k = pl.program_id(2)

        @pl.when(k == 0)
        def _():
            acc[...] = jnp.zeros_like(acc)

        acc[...] += lax.dot_general(a_ref[...].astype(BF16), b_ref[...].astype(BF16), dims,
                                    preferred_element_type=F32)

        @pl.when(k == nk - 1)
        def _():
            out = acc[...] * scale
            if res is not None:
                out = out + rest[0][...].astype(F32)
            o_ref[...] = out.astype(o_ref.dtype)

    a_spec = pl.BlockSpec((tk, tm), lambda i, j, k: (k, i)) if ta else pl.BlockSpec((tm, tk), lambda i, j, k: (i, k))
    b_spec = pl.BlockSpec((tn, tk), lambda i, j, k: (j, k)) if tb else pl.BlockSpec((tk, tn), lambda i, j, k: (k, j))
    if a_split:
        per_half = K // 2 // tk
        a_spec = pl.BlockSpec((None, tm, tk), lambda i, j, k: (k // per_half, i, k % per_half))
    if b_split:
        per_half = N // 2 // tn
        b_spec = pl.BlockSpec((None, tk, tn), lambda i, j, k: (j // per_half, k, j % per_half))
    o_spec = pl.BlockSpec((tm, tn), lambda i, j, k: (i, j))
    ins, specs = [a, b], [a_spec, b_spec]
    if res is not None:
        ins.append(res)
        specs.append(o_spec)
    return _pcall(body, name=name, out_shape=jax.ShapeDtypeStruct((M, N), out_dtype),
                  grid=(M // tm, N // tn, nk), in_specs=specs, out_specs=o_spec,
                  scratch=[pltpu.VMEM((tm, tn), F32)], sem=("parallel", "parallel", "arbitrary"))(*ins)


def _row_spec(r, tile):
    if isinstance(r, tuple):
        arr, width, col = r
        return arr, pl.BlockSpec((tile, width), lambda i, col=col: (i, col))
    return r, pl.BlockSpec((tile, r.shape[1]), lambda i: (i, 0))


def _const_spec(c):
    return pl.BlockSpec(c.shape, lambda i: (0,) * c.ndim)


def rowwise(fn, rows, consts, outs, *, tile, name):
    arrs, specs = zip(*[_row_spec(r, tile) for r in rows])
    n_rows = arrs[0].shape[0]
    tile = min(tile, n_rows)
    n_in = len(rows) + len(consts)

    def body(*refs):
        res = fn(*[r[...] for r in refs[:n_in]])
        for o_ref, o in zip(refs[n_in:], res):
            o_ref[...] = o.astype(o_ref.dtype)

    arrs, specs = zip(*[_row_spec(r, tile) for r in rows])
    return _pcall(body, name=name,
                  out_shape=[jax.ShapeDtypeStruct((n_rows, w), dt) for w, dt in outs],
                  grid=(n_rows // tile,),
                  in_specs=list(specs) + [_const_spec(c) for c in consts],
                  out_specs=[pl.BlockSpec((tile, w), lambda i: (i, 0)) for w, _ in outs],
                  sem=("parallel",))(*arrs, *consts)


def rowwise_bwd(fn, rows, consts, cts, *, tile, name, row_grads, const_grads, add=None):
    arrs, _ = zip(*[_row_spec(r, tile) for r in rows])
    n_rows = arrs[0].shape[0]
    tile = min(tile, n_rows)
    arrs, specs = zip(*[_row_spec(r, tile) for r in rows])
    ct_arrs, ct_specs = zip(*[_row_spec(c, tile) for c in cts])
    add = add or {}
    add_idx = sorted(add)
    add_arrs, add_specs = (zip(*[_row_spec(add[i], tile) for i in add_idx]) if add_idx else ((), ()))
    nr, nc, nct, na = len(rows), len(consts), len(cts), len(add_idx)
    want_rows = [i for i, d in enumerate(row_grads) if d is not None]
    want_consts = [i for i, w in enumerate(const_grads) if w]

    def body(*refs):
        row_v = [r[...] for r in refs[:nr]]
        const_v = [r[...] for r in refs[nr:nr + nc]]
        ct_v = [r[...] for r in refs[nr + nc:nr + nc + nct]]
        add_v = {i: refs[nr + nc + nct + j][...] for j, i in enumerate(add_idx)}
        out_refs = refs[nr + nc + nct + na:]
        res, vjp = jax.vjp(fn, *row_v, *const_v)
        grads = vjp(tuple(c.astype(o.dtype) for c, o in zip(ct_v, res)))
        for o_ref, i in zip(out_refs, want_rows):
            g = grads[i].astype(F32)
            if i in add_v:
                g = g + add_v[i].astype(F32)
            o_ref[...] = g.astype(o_ref.dtype)
        first = pl.program_id(0) == 0
        for o_ref, i in zip(out_refs[len(want_rows):], want_consts):
            g = grads[nr + i].astype(F32)

            @pl.when(first)
            def _(o_ref=o_ref, g=g):
                o_ref[...] = g

            @pl.when(jnp.logical_not(first))
            def _(o_ref=o_ref, g=g):
                o_ref[...] += g

    def width(r):
        return r[1] if isinstance(r, tuple) else r.shape[1]

    out_shape = [jax.ShapeDtypeStruct((n_rows, width(rows[i])), row_grads[i]) for i in want_rows]
    out_shape += [jax.ShapeDtypeStruct(consts[i].shape, F32) for i in want_consts]
    out_specs = [pl.BlockSpec((tile, width(rows[i])), lambda i_: (i_, 0)) for i in want_rows]
    out_specs += [_const_spec(consts[i]) for i in want_consts]
    return _pcall(body, name=name, out_shape=out_shape, grid=(n_rows // tile,),
                  in_specs=list(specs) + [_const_spec(c) for c in consts] + list(ct_specs) + list(add_specs),
                  out_specs=out_specs, sem=("arbitrary",))(*arrs, *consts, *ct_arrs, *add_arrs)


def f_rmsnorm(x, g):
    x = x.astype(F32)
    return (x * lax.rsqrt(jnp.mean(x * x, axis=-1, keepdims=True) + EPS) * g,)


def _head_sel(first_lane):
    r, c = _iota2((LANES, GDN_WIDTH), 0), _iota2((LANES, GDN_WIDTH), 1)
    return (r == c // HEAD_DIM + first_lane).astype(F32)


def _tri(n, strict=False):
    r, c = _iota2((n, n), 0), _iota2((n, n), 1)
    return r > c if strict else r >= c


def f_gdn_pre(xc, ab, a_log, dt_bias):
    s = _silu(xc.astype(F32))
    qs, ks = [], []
    for h in range(GDN_HEADS):
        qh = s[:, h * HEAD_DIM:(h + 1) * HEAD_DIM]
        kh = s[:, GDN_WIDTH + h * HEAD_DIM:GDN_WIDTH + (h + 1) * HEAD_DIM]
        qs.append(qh * lax.rsqrt(jnp.sum(qh * qh, axis=-1, keepdims=True) + EPS) * (HEAD_DIM ** -0.5))
        ks.append(kh * lax.rsqrt(jnp.sum(kh * kh, axis=-1, keepdims=True) + EPS))
    q, k = jnp.concatenate(qs, axis=1), jnp.concatenate(ks, axis=1)
    v = s[:, 2 * GDN_WIDTH:]
    ab = ab.astype(F32)
    g = -jnp.exp(a_log) * _softplus(ab + dt_bias)
    gc = fdot(_tri(GDN_CHUNK).astype(F32), fdot(g, _head_sel(0)))
    beta = fdot(_sigmoid(ab), _head_sel(GDN_HEADS))
    return q, k, v, gc, beta


def _unit_lower_inverse(neg_lower):
    C = neg_lower.shape[0]
    inv = (_iota2((C, C), 0) == _iota2((C, C), 1)).astype(F32) + neg_lower
    power = idot.nn(neg_lower, neg_lower)
    for level in range(6):
        both = idot.nn(power, jnp.concatenate([power, inv], axis=1) if level < 5 else inv)
        if level < 5:
            power, inv = both[:, :C], inv + both[:, C:]
        else:
            inv = inv + both
    return inv


@jax.custom_vjp
def _solve_with_inverse(inv, neg_lower, rhs):
    return idot.nn(inv, rhs)


def _solve_fwd(inv, neg_lower, rhs):
    x = idot.nn(inv, rhs)
    return x, (inv, x)


def _solve_bwd(resid, ct):
    inv, x = resid
    d_rhs = idot.tn(inv, ct)
    return jnp.zeros_like(inv), idot.nt(d_rhs, x), d_rhs


_solve_with_inverse.defvjp(_solve_fwd, _solve_bwd)


def f_gdn_intra(q, k, v, gc, beta, inv=None):
    C = GDN_CHUNK
    causal, strict = _tri(C), _tri(C, strict=True)
    is_last = _iota2((C, HEAD_DIM), 0) == C - 1
    outs = [[] for _ in range(6 if inv is None else 5)]
    for h in range(GDN_HEADS):
        sl = slice(h * HEAD_DIM, (h + 1) * HEAD_DIM)
        qh, kh, vh, gh, bh = q[:, sl], k[:, sl], v[:, sl], gc[:, sl], beta[:, sl]
        gdiff = gh - gh.T
        decay = jnp.where(causal, jnp.exp(jnp.where(causal, gdiff, 0.0)), 0.0)
        kb = kh * bh
        neg_lower = jnp.where(strict, -(idot(kb, kh.T) * decay), 0.0)
        rhs = jnp.concatenate([vh * bh, kb * jnp.exp(gh)], axis=1)
        if inv is None:
            inv_h = _unit_lower_inverse(neg_lower)
            sol = idot.nn(inv_h, rhs)
        else:
            sol = _solve_with_inverse(inv[:, sl], neg_lower, rhs)
        qk = jnp.where(causal, idot(qh, kh.T) * decay, 0.0)
        g_last = jnp.sum(jnp.where(is_last, gh, 0.0), axis=0, keepdims=True)
        vals = (sol[:, :HEAD_DIM], sol[:, HEAD_DIM:], qk, kh * jnp.exp(g_last - gh), qh * jnp.exp(gh))
        for lst, val in zip(outs, vals + ((inv_h,) if inv is None else ())):
            lst.append(val)
    return tuple(jnp.concatenate(o, axis=1) for o in outs)


def f_gdn_post(o, z, gain):
    z = z.astype(F32)
    parts = []
    for h in range(GDN_HEADS):
        oh = o[:, h * HEAD_DIM:(h + 1) * HEAD_DIM]
        parts.append(oh * lax.rsqrt(jnp.mean(oh * oh, axis=-1, keepdims=True) + EPS) * gain)
    return (jnp.concatenate(parts, axis=1) * _silu(z),)


def f_mem_attn(q, k, v):
    q = q.astype(F32)
    lane_head = _iota2((1, MEM_WIDTH), 1) // MEM_HEAD_DIM
    out = jnp.zeros(q.shape, F32)
    kt = k.astype(F32).T
    for h in range(MEM_HEADS):
        mask = (lane_head == h).astype(F32)
        logits = bdot(q * mask, kt) * (MEM_HEAD_DIM ** -0.5)
        p = jnp.exp(logits - jnp.max(logits, axis=-1, keepdims=True))
        p = p / jnp.sum(p, axis=-1, keepdims=True)
        out = out + bdot(p, v) * mask
    return (out,)


def f_loss(y, t):
    d = y - t
    return (d * d,)


def conv_fwd(proj, w8, *, width, tile=512):
    n_rows = proj.shape[0]
    tile = min(tile, n_rows)

    def body(x_ref, halo_ref, w_ref, o_ref):
        i = pl.program_id(0)
        halo = jnp.where(i > 0, halo_ref[...].astype(F32), 0.0)
        xs = jnp.concatenate([halo, x_ref[...].astype(F32)], axis=0)
        acc = xs[8:] * w_ref[3:4, :]
        for j in range(CONV_WIDTH - 1):
            acc = acc + pltpu.roll(xs, CONV_WIDTH - 1 - j, 0)[8:] * w_ref[j:j + 1, :]
        o_ref[...] = acc

    return _pcall(body, name="gdn_conv_fwd", out_shape=jax.ShapeDtypeStruct((n_rows, width), F32),
                  grid=(n_rows // tile,),
                  in_specs=[pl.BlockSpec((tile, width), lambda i: (i, 0)),
                            pl.BlockSpec((8, width), lambda i: (jnp.maximum(i * (tile // 8) - 1, 0), 0)),
                            pl.BlockSpec((8, width), lambda i: (0, 0))],
                  out_specs=pl.BlockSpec((tile, width), lambda i: (i, 0)), sem=("parallel",))(proj, proj, w8)


def conv_bwd(proj, w8, dy, *, width, tile=512):
    n_rows = proj.shape[0]
    tile = min(tile, n_rows)
    n = n_rows // tile

    def body(x_ref, xhalo_ref, w_ref, dy_ref, dyhalo_ref, dx_ref, dw_ref):
        i = pl.program_id(0)
        dy = dy_ref[...]
        after = jnp.where(i < n - 1, dyhalo_ref[...], 0.0)
        ds = jnp.concatenate([dy, after], axis=0)
        dx = dy * w_ref[3:4, :]
        for j in range(CONV_WIDTH - 1):
            shift = CONV_WIDTH - 1 - j
            dx = dx + pltpu.roll(ds, tile + 8 - shift, 0)[:tile] * w_ref[j:j + 1, :]
        dx_ref[...] = dx.astype(dx_ref.dtype)
        halo = jnp.where(i > 0, xhalo_ref[...].astype(F32), 0.0)
        xs = jnp.concatenate([halo, x_ref[...].astype(F32)], axis=0)
        rows = [jnp.sum(dy * pltpu.roll(xs, CONV_WIDTH - 1 - j, 0)[8:], axis=0, keepdims=True)
                for j in range(CONV_WIDTH - 1)]
        rows.append(jnp.sum(dy * xs[8:], axis=0, keepdims=True))
        dw = jnp.concatenate(rows + [jnp.zeros((8 - CONV_WIDTH, width), F32)], axis=0)

        @pl.when(i == 0)
        def _():
            dw_ref[...] = dw

        @pl.when(i > 0)
        def _():
            dw_ref[...] += dw

    t8 = tile // 8
    return _pcall(body, name="gdn_conv_bwd",
                  out_shape=[jax.ShapeDtypeStruct((n_rows, width), BF16), jax.ShapeDtypeStruct((8, width), F32)],
                  grid=(n,),
                  in_specs=[pl.BlockSpec((tile, width), lambda i: (i, 0)),
                            pl.BlockSpec((8, width), lambda i: (jnp.maximum(i * t8 - 1, 0), 0)),
                            pl.BlockSpec((8, width), lambda i: (0, 0)),
                            pl.BlockSpec((tile, width), lambda i: (i, 0)),
                            pl.BlockSpec((8, width), lambda i: (jnp.minimum((i + 1) * t8, n * t8 - 1), 0))],
                  out_specs=[pl.BlockSpec((tile, width), lambda i: (i, 0)), pl.BlockSpec((8, width), lambda i: (0, 0))],
                  sem=("arbitrary",))(proj, proj, w8, dy, dy)


def gdn_scan_fwd(u, w, qk, kt, qh, gc):
    n_rows = u.shape[0]
    C, n = GDN_CHUNK, u.shape[0] // GDN_CHUNK

    def body(u_ref, w_ref, qk_ref, kt_ref, qh_ref, gc_ref, o_ref, vn_ref, sin_ref, st):
        @pl.when(pl.program_id(0) == 0)
        def _():
            st[...] = jnp.zeros_like(st)

        sin_ref[0] = st[...]
        for h in range(GDN_HEADS):
            sl = slice(h * HEAD_DIM, (h + 1) * HEAD_DIM)
            s = st[sl, :]
            v_new = u_ref[:, sl] - sdot(w_ref[:, sl], s)
            o_ref[:, sl] = sdot(qh_ref[:, sl], s) + sdot(qk_ref[:, sl], v_new)
            vn_ref[:, sl] = v_new
            st[sl, :] = s * jnp.exp(gc_ref[C - 1:C, sl]) + sdot.tn(kt_ref[:, sl], v_new)

    blk = pl.BlockSpec((C, GDN_WIDTH), lambda i: (i, 0))
    return _pcall(body, name="gdn_scan_fwd",
                  out_shape=[jax.ShapeDtypeStruct((n_rows, GDN_WIDTH), F32), jax.ShapeDtypeStruct((n_rows, GDN_WIDTH), F32),
                             jax.ShapeDtypeStruct((n, GDN_WIDTH, HEAD_DIM), F32)],
                  grid=(n,), in_specs=[blk] * 6,
                  out_specs=[blk, blk, pl.BlockSpec((1, GDN_WIDTH, HEAD_DIM), lambda i: (i, 0, 0))],
                  scratch=[pltpu.VMEM((GDN_WIDTH, HEAD_DIM), F32)], sem=("arbitrary",))(u, w, qk, kt, qh, gc)


def gdn_scan_bwd(do, w, qk, kt, qh, gc, vn, sin):
    n_rows = do.shape[0]
    C, n = GDN_CHUNK, do.shape[0] // GDN_CHUNK

    def body(do_ref, w_ref, qk_ref, kt_ref, qh_ref, gc_ref, vn_ref, sin_ref,
             du_ref, dw_ref, dqk_ref, dkt_ref, dqh_ref, dgl_ref, dst):
        @pl.when(pl.program_id(0) == 0)
        def _():
            dst[...] = jnp.zeros_like(dst)

        for h in range(GDN_HEADS):
            sl = slice(h * HEAD_DIM, (h + 1) * HEAD_DIM)
            s, ds_out, d_o, v_new = sin_ref[0, sl, :], dst[sl, :], do_ref[:, sl], vn_ref[:, sl]
            e = jnp.exp(gc_ref[C - 1:C, sl])
            dvn = sdot.tn(qk_ref[:, sl], d_o) + sdot(kt_ref[:, sl], ds_out)
            du_ref[:, sl] = dvn
            dw_ref[:, sl] = -sdot.nt(dvn, s)
            dqk_ref[:, sl] = sdot.nt(d_o, v_new)
            dkt_ref[:, sl] = sdot.nt(v_new, ds_out)
            dqh_ref[:, sl] = sdot.nt(d_o, s)
            dgl = jnp.sum(ds_out * s, axis=0, keepdims=True) * e
            dgl_ref[:, sl] = jnp.broadcast_to(dgl, (8, HEAD_DIM))
            dst[sl, :] = sdot.tn(qh_ref[:, sl], d_o) + e * ds_out - sdot.tn(w_ref[:, sl], dvn)

    blk = pl.BlockSpec((C, GDN_WIDTH), lambda i: (n - 1 - i, 0))
    row = jax.ShapeDtypeStruct((n_rows, GDN_WIDTH), F32)
    return _pcall(body, name="gdn_scan_bwd",
                  out_shape=[row] * 5 + [jax.ShapeDtypeStruct((n * 8, GDN_WIDTH), F32)],
                  grid=(n,), in_specs=[blk] * 7 + [pl.BlockSpec((1, GDN_WIDTH, HEAD_DIM), lambda i: (n - 1 - i, 0, 0))],
                  out_specs=[blk] * 5 + [pl.BlockSpec((8, GDN_WIDTH), lambda i: (n - 1 - i, 0))],
                  scratch=[pltpu.VMEM((GDN_WIDTH, HEAD_DIM), F32)], sem=("arbitrary",))(do, w, qk, kt, qh, gc, vn, sin)


def gdn_intra_bwd(q, k, v, gc, beta, inv, cts, dgl):
    n_rows = q.shape[0]
    C = GDN_CHUNK

    def body(*refs):
        ins = [r[...] for r in refs[:5]]
        inv_v = refs[5][...]
        ct = tuple(r[...] for r in refs[6:11])
        dgl_v = refs[11][...]
        _, vjp = jax.vjp(lambda *a: f_gdn_intra(*a, inv=inv_v), *ins)
        grads = list(vjp(ct))
        last = _iota2((C, GDN_WIDTH), 0) == C - 1
        grads[3] = grads[3] + jnp.where(last, jnp.broadcast_to(dgl_v[0:1, :], (C, GDN_WIDTH)), 0.0)
        for o_ref, g in zip(refs[12:], grads):
            o_ref[...] = g

    blk = pl.BlockSpec((C, GDN_WIDTH), lambda i: (i, 0))
    return _pcall(body, name="gdn_intra_bwd", out_shape=[jax.ShapeDtypeStruct((n_rows, GDN_WIDTH), F32)] * 5,
                  grid=(n_rows // C,), in_specs=[blk] * 11 + [pl.BlockSpec((8, GDN_WIDTH), lambda i: (i, 0))],
                  out_specs=[blk] * 5, sem=("parallel",))(q, k, v, gc, beta, inv, *cts, dgl)


def fox_gate_fwd(f, b_f):
    n_rows = f.shape[0]
    T = LANES

    def body(f_ref, b_ref, cb_ref, carry):
        @pl.when(pl.program_id(0) == 0)
        def _():
            carry[...] = jnp.zeros_like(carry)

        c = fdot(_tri(T).astype(F32), _log_sigmoid(f_ref[...] + b_ref[...])) + carry[...]
        carry[...] = c[T - 1:T, :]
        cb_ref[...] = fdot(c, _head_sel(0))

    return _pcall(body, name="fox_gate_fwd", out_shape=jax.ShapeDtypeStruct((n_rows, FOX_WIDTH), F32),
                  grid=(n_rows // T,),
                  in_specs=[pl.BlockSpec((T, LANES), lambda i: (i, 0)), pl.BlockSpec((1, LANES), lambda i: (0, 0))],
                  out_specs=pl.BlockSpec((T, FOX_WIDTH), lambda i: (i, 0)),
                  scratch=[pltpu.VMEM((1, LANES), F32)], sem=("arbitrary",))(f, b_f)


def fox_gate_bwd(f, b_f, dcrow, dcb):
    n_rows = f.shape[0]
    T = LANES
    n = n_rows // T

    def body(f_ref, b_ref, dc_ref, dcb_ref, df_ref, db_ref, carry):
        i = pl.program_id(0)

        @pl.when(i == 0)
        def _():
            carry[...] = jnp.zeros_like(carry)

        rows = [dc_ref[h] for h in range(FOX_HEADS)] + [jnp.zeros((T - FOX_HEADS, T), F32)]
        first_lane = (_iota2((FOX_WIDTH, LANES), 0) == _iota2((FOX_WIDTH, LANES), 1) * HEAD_DIM).astype(F32)
        dc = jnp.concatenate(rows, axis=0).T + fdot(dcb_ref[...], first_lane)
        dlog = fdot.tn(_tri(T).astype(F32), dc) + carry[...]
        carry[...] = dlog[0:1, :]
        df = dlog * (1.0 - _sigmoid(f_ref[...] + b_ref[...]))
        df_ref[...] = df
        db = jnp.sum(df, axis=0, keepdims=True)

        @pl.when(i == 0)
        def _():
            db_ref[...] = db

        @pl.when(i > 0)
        def _():
            db_ref[...] += db

    return _pcall(body, name="fox_gate_bwd",
                  out_shape=[jax.ShapeDtypeStruct((n_rows, LANES), F32), jax.ShapeDtypeStruct((1, LANES), F32)],
                  grid=(n,),
                  in_specs=[pl.BlockSpec((T, LANES), lambda i: (n - 1 - i, 0)), pl.BlockSpec((1, LANES), lambda i: (0, 0)),
                            pl.BlockSpec((FOX_HEADS, 1, T), lambda i: (0, 0, n - 1 - i)),
                            pl.BlockSpec((T, FOX_WIDTH), lambda i: (n - 1 - i, 0))],
                  out_specs=[pl.BlockSpec((T, LANES), lambda i: (n - 1 - i, 0)), pl.BlockSpec((1, LANES), lambda i: (0, 0))],
                  scratch=[pltpu.VMEM((1, LANES), F32)], sem=("arbitrary",))(f, b_f, dcrow, dcb)


FOX_AUG = 2 * HEAD_DIM


def _fox_tiles(n_rows):
    return min(1024, n_rows), min(512, n_rows)


def _fox_pairs(n_rows, query_major):
    tq, tk = _fox_tiles(n_rows)
    nq, r = n_rows // tq, tq // tk
    if query_major:
        pairs = [(i, j) for i in range(nq) for j in range(r * (i + 1))]
    else:
        pairs = [(i, j) for j in range(nq * r) for i in range(j // r, nq)]
    return jnp.asarray([p[0] for p in pairs], jnp.int32), jnp.asarray([p[1] for p in pairs], jnp.int32)


def fox_augment(x, cb, query_side):
    def fn(xt, ct):
        xt = xt.astype(F32)
        lane = _iota2((xt.shape[0], HEAD_DIM), 1)
        parts = []
        for h in range(FOX_HEADS):
            sl = slice(h * HEAD_DIM, (h + 1) * HEAD_DIM)
            c = ct[:, sl]
            hi = c.astype(BF16).astype(F32)
            mid = (c - hi).astype(BF16).astype(F32)
            lo = (c - hi - mid).astype(BF16).astype(F32)
            terms = jnp.where(lane % 3 == 0, hi, jnp.where(lane % 3 == 1, mid, lo))
            if query_side:
                extra = jnp.where(lane < 3, terms, jnp.where(lane < 6, 1.0, 0.0))
                parts += [xt[:, sl] * (HEAD_DIM ** -0.5), extra]
            else:
                extra = jnp.where(lane < 3, 1.0, jnp.where(lane < 6, -terms, 0.0))
                parts += [xt[:, sl], extra]
        return (jnp.concatenate(parts, axis=1),)

    return rowwise(fn, [(x, FOX_WIDTH, 0), cb], [], [(FOX_HEADS * FOX_AUG, BF16)], tile=ROW_TILE,
                   name="fox_augment_q" if query_side else "fox_augment_k")[0]


def _pcall_tables(body, *, name, out_shape, grid, tables, in_specs, out_specs, scratch, sem):
    spec = pltpu.PrefetchScalarGridSpec(num_scalar_prefetch=len(tables), grid=grid, in_specs=in_specs, out_specs=out_specs,
                                        scratch_shapes=list(scratch))
    return pl.pallas_call(body, name=name, out_shape=out_shape, grid_spec=spec,
                          compiler_params=pltpu.CompilerParams(vmem_limit_bytes=VMEM_LIMIT_BYTES, dimension_semantics=sem))


def _fox_logits(qa, ka, offset):
    s = bdot.nt(qa, ka)
    if offset is not None:
        s = jnp.where(_iota2(s.shape, 0) + offset >= _iota2(s.shape, 1), s, NEG_INF)
    return s


def _fox_p_ds(offset, qa_ref, ka_ref, v_ref, o_ref, lse_ref, do_ref):
    s = _fox_logits(qa_ref[...], ka_ref[...], offset)
    p = jnp.exp(s - jnp.tile(lse_ref[...], (1, s.shape[1] // LANES)))
    d_o = do_ref[...].astype(F32)
    delta = jnp.sum(d_o * o_ref[...].astype(F32), axis=-1, keepdims=True)
    return p, p * (bdot.nt(d_o, v_ref[...]) - delta), d_o


def _fox_on_diagonal(i, j, r, tk, step):
    @pl.when(j < r * i)
    def _():
        step(None)

    for m in range(r):
        @pl.when(j == r * i + m)
        def _(m=m):
            step(-m * tk)


def _fox_specs(tq, tk, do_col):
    qaspec = pl.BlockSpec((tq, FOX_AUG), lambda h, p, it, jt: (it[p], h))
    qspec = pl.BlockSpec((tq, HEAD_DIM), lambda h, p, it, jt: (it[p], h))
    dospec = pl.BlockSpec((tq, HEAD_DIM), lambda h, p, it, jt: (it[p], do_col + h))
    kaspec = pl.BlockSpec((tk, FOX_AUG), lambda h, p, it, jt: (jt[p], h))
    kspec = pl.BlockSpec((tk, HEAD_DIM), lambda h, p, it, jt: (jt[p], h))
    vspec = pl.BlockSpec((tk, HEAD_DIM), lambda h, p, it, jt: (jt[p], FOX_HEADS + h))
    cspec = pl.BlockSpec((1, 1, tk), lambda h, p, it, jt: (h, 0, jt[p]))
    return qaspec, qspec, dospec, kaspec, kspec, vspec, cspec


def fox_fwd(qa, kv, ka):
    n_rows = kv.shape[0]
    tq, tk = _fox_tiles(n_rows)
    r = tq // tk
    tables = _fox_pairs(n_rows, True)

    def body(it, jt, qa_ref, ka_ref, v_ref, o_ref, lse_ref, m_sc, l_sc, acc):
        i, j = it[pl.program_id(1)], jt[pl.program_id(1)]

        @pl.when(j == 0)
        def _():
            m_sc[...] = jnp.full(m_sc.shape, NEG_INF, F32)
            l_sc[...] = jnp.zeros_like(l_sc)
            acc[...] = jnp.zeros_like(acc)

        def step(offset):
            s = _fox_logits(qa_ref[...], ka_ref[...], offset)
            m_old = m_sc[...]
            m_new = jnp.maximum(m_old, jnp.max(s, axis=-1, keepdims=True))
            alpha = jnp.exp(m_old - m_new)
            p = jnp.exp(s - jnp.tile(m_new, (1, tk // LANES)))
            l_sc[...] = l_sc[...] * alpha + jnp.sum(p, axis=-1, keepdims=True)
            acc[...] = acc[...] * alpha + bdot(p, v_ref[...])
            m_sc[...] = m_new

        _fox_on_diagonal(i, j, r, tk, step)

        @pl.when(j == r * i + r - 1)
        def _():
            o_ref[...] = (acc[...] / l_sc[...]).astype(o_ref.dtype)
            lse_ref[...] = m_sc[...] + jnp.log(l_sc[...])

    qaspec, qspec, _, kaspec, _, vspec, _ = _fox_specs(tq, tk, 0)
    return _pcall_tables(body, name="fox_fwd",
                         out_shape=[jax.ShapeDtypeStruct((n_rows, FOX_WIDTH), BF16), jax.ShapeDtypeStruct((n_rows, FOX_WIDTH), F32)],
                         grid=(FOX_HEADS, tables[0].shape[0]), tables=tables,
                         in_specs=[qaspec, kaspec, vspec], out_specs=[qspec, qspec],
                         scratch=[pltpu.VMEM((tq, HEAD_DIM), F32)] * 3, sem=("parallel", "arbitrary"))(*tables, qa, ka, kv)


def fox_bwd_dq(qa, kv, ka, o, lse, do, prev=None):
    do, _, do_col = do
    do_col *= FOX_HEADS
    n_rows = kv.shape[0]
    tq, tk = _fox_tiles(n_rows)
    r = tq // tk
    n_prev = 0 if prev is None else 1
    tables = _fox_pairs(n_rows, True)

    def body(it, jt, qa_ref, ka_ref, k_ref, v_ref, o_ref, lse_ref, do_ref, *rest):
        dq_ref, drow_ref, acc, rows = rest[n_prev:]
        i, j = it[pl.program_id(1)], jt[pl.program_id(1)]

        @pl.when(j == 0)
        def _():
            acc[...] = jnp.zeros_like(acc)
            rows[...] = jnp.zeros_like(rows)

        def step(offset):
            _, ds, _ = _fox_p_ds(offset, qa_ref, ka_ref, v_ref, o_ref, lse_ref, do_ref)
            acc[...] += bdot(ds, k_ref[...])
            rows[...] += jnp.sum(ds, axis=-1, keepdims=True)

        _fox_on_diagonal(i, j, r, tk, step)

        @pl.when(j == r * i + r - 1)
        def _():
            dq_ref[...] = (acc[...] * (HEAD_DIM ** -0.5)).astype(dq_ref.dtype)
            drow_ref[...] = rows[...] + rest[0][...] if n_prev else rows[...]

    qaspec, qspec, dospec, kaspec, kspec, vspec, _ = _fox_specs(tq, tk, do_col)
    return _pcall_tables(body, name="fox_bwd_dq" + ("_acc" if n_prev else ""),
                         out_shape=[jax.ShapeDtypeStruct((n_rows, FOX_WIDTH), BF16), jax.ShapeDtypeStruct((n_rows, FOX_WIDTH), F32)],
                         grid=(FOX_HEADS, tables[0].shape[0]), tables=tables,
                         in_specs=[qaspec, kaspec, kspec, vspec, qspec, qspec, dospec] + [qspec] * n_prev,
                         out_specs=[qspec, qspec], scratch=[pltpu.VMEM((tq, HEAD_DIM), F32)] * 2,
                         sem=("parallel", "arbitrary"))(*tables, qa, ka, kv, kv, o, lse, do, *([prev] if n_prev else []))


def fox_bwd_dkv(qa, kv, ka, o, lse, do, prev=None):
    do, _, do_col = do
    do_col *= FOX_HEADS
    n_rows = kv.shape[0]
    tq, tk = _fox_tiles(n_rows)
    nq, r = n_rows // tq, tq // tk
    n_prev = 0 if prev is None else 3
    tables = _fox_pairs(n_rows, False)

    def body(it, jt, qa_ref, ka_ref, v_ref, o_ref, lse_ref, do_ref, *rest):
        prev_refs = rest[:n_prev]
        dk_ref, dv_ref, dc_ref, dk_acc, dv_acc, dc_acc = rest[n_prev:]
        i, j = it[pl.program_id(1)], jt[pl.program_id(1)]

        @pl.when(j >= r * i)
        def _():
            dk_acc[...] = jnp.zeros_like(dk_acc)
            dv_acc[...] = jnp.zeros_like(dv_acc)
            dc_acc[...] = jnp.zeros_like(dc_acc)

        def step(offset):
            p, ds, d_o = _fox_p_ds(offset, qa_ref, ka_ref, v_ref, o_ref, lse_ref, do_ref)
            dv_acc[...] += bdot.tn(p, d_o)
            dk_acc[...] += bdot.tn(ds, qa_ref[:, :HEAD_DIM])
            dc_acc[...] -= jnp.sum(ds, axis=0, keepdims=True)

        _fox_on_diagonal(i, j, r, tk, step)

        @pl.when(i == nq - 1)
        def _():
            dk, dv, dc = dk_acc[...], dv_acc[...], dc_acc[...]
            if n_prev:
                dk, dv, dc = dk + prev_refs[0][...], dv + prev_refs[1][...], dc + prev_refs[2][0]
            dk_ref[...] = dk
            dv_ref[...] = dv
            dc_ref[0] = dc

    qaspec, qspec, dospec, kaspec, kspec, vspec, cspec = _fox_specs(tq, tk, do_col)
    return _pcall_tables(body, name="fox_bwd_dkv" + ("_acc" if n_prev else ""),
                         out_shape=[jax.ShapeDtypeStruct((n_rows, FOX_WIDTH), F32), jax.ShapeDtypeStruct((n_rows, FOX_WIDTH), F32),
                                    jax.ShapeDtypeStruct((FOX_HEADS, 1, n_rows), F32)],
                         grid=(FOX_HEADS, tables[0].shape[0]), tables=tables,
                         in_specs=[qaspec, kaspec, vspec, qspec, qspec, dospec] + [kspec, kspec, cspec][:n_prev],
                         out_specs=[kspec, kspec, cspec],
                         scratch=[pltpu.VMEM((tk, HEAD_DIM), F32), pltpu.VMEM((tk, HEAD_DIM), F32), pltpu.VMEM((1, tk), F32)],
                         sem=("parallel", "arbitrary"))(*tables, qa, ka, kv, o, lse, do, *(prev or ()))


def loss_head(h, gain, target, *, tile=512):
    n_rows, d = h.shape
    tile = min(tile, n_rows)

    def body(h_ref, g_ref, t_ref, part_ref, dy_ref):
        (y,) = f_rmsnorm(h_ref[...], g_ref[...])
        diff = y - t_ref[...]
        dy_ref[...] = diff * (1.0 / d)
        part = jnp.sum(diff * diff, axis=0, keepdims=True)
        first = pl.program_id(0) == 0

        @pl.when(first)
        def _():
            part_ref[...] = part

        @pl.when(jnp.logical_not(first))
        def _():
            part_ref[...] += part

    blk = pl.BlockSpec((tile, d), lambda i: (i, 0))
    one = pl.BlockSpec((1, d), lambda i: (0, 0))
    return _pcall(body, name="loss_head",
                  out_shape=[jax.ShapeDtypeStruct((1, d), F32), jax.ShapeDtypeStruct((n_rows, d), F32)],
                  grid=(n_rows // tile,), in_specs=[blk, one, blk], out_specs=[one, blk], sem=("arbitrary",))(h, gain, target)


def adamw(w, g, m, v, *, name):
    shape = w.shape
    cols = shape[-1] if w.ndim >= 2 else w.size
    rows = w.size // cols
    tile = _pick(rows, (256, 128, 64, 32, 16, 8))
    as2d = lambda a: a.reshape(rows, cols)

    def body(w_ref, g_ref, m_ref, v_ref, d_ref, nm_ref, nv_ref):
        g_ = g_ref[...]
        m_ = ADAM_B1 * m_ref[...] + (1.0 - ADAM_B1) * g_
        v_ = ADAM_B2 * v_ref[...] + (1.0 - ADAM_B2) * (g_ * g_)
        m_hat = m_ / (1.0 - ADAM_B1 ** ADAM_STEP)
        v_hat = v_ / (1.0 - ADAM_B2 ** ADAM_STEP)
        d_ref[...] = -ADAM_LR * (m_hat / (jnp.sqrt(v_hat) + ADAM_EPS) + ADAM_WD * w_ref[...])
        nm_ref[...] = m_
        nv_ref[...] = v_

    blk = pl.BlockSpec((tile, cols), lambda i: (i, 0))
    outs = _pcall(body, name=name, out_shape=[jax.ShapeDtypeStruct((rows, cols), F32)] * 3, grid=(rows // tile,),
                  in_specs=[blk] * 4, out_specs=[blk] * 3, sem=("parallel",))(as2d(w), as2d(g), as2d(m), as2d(v))
    return tuple(o.reshape(shape) for o in outs)


def sum_leading(a, *, name):
    p, r, c = a.shape
    tile = _pick(r, (256, 128, 64, 32, 16, 8))

    def body(a_ref, o_ref):
        total = a_ref[0].astype(F32)
        for k in range(1, p):
            total = total + a_ref[k].astype(F32)
        o_ref[...] = total

    return _pcall(body, name=name, out_shape=jax.ShapeDtypeStruct((r, c), F32), grid=(r // tile,),
                  in_specs=[pl.BlockSpec((p, tile, c), lambda i: (0, i, 0))],
                  out_specs=pl.BlockSpec((tile, c), lambda i: (i, 0)), sem=("parallel",))(a)


_HBM = pl.BlockSpec(memory_space=pltpu.HBM)


def _comm_call(body, *, name, out_shape, n_in, scratch):
    return pl.pallas_call(body, name=name, out_shape=out_shape, in_specs=[_HBM] * n_in, out_specs=_HBM,
                          scratch_shapes=scratch,
                          compiler_params=pltpu.CompilerParams(has_side_effects=True))


def all_gather8(a, *, name):
    m_per, n = a.shape

    def body(x_ref, out_ref, send_sems, recv_sems, local_sem):
        x, y, c = lax.axis_index("x"), lax.axis_index("y"), lax.axis_index("c")
        me, sibling = (x, y, c), (x, y, 1 - c)
        chips = [(1 - x, y), (x, 1 - y), (1 - x, 1 - y)]

        def rows(px, py, pc):
            return out_ref.at[pl.ds((4 * px + 2 * py + pc) * m_per, m_per), :]

        def copy(k, block, to, src=None):
            return pltpu.make_async_remote_copy(
                src_ref=rows(*block) if src is None else src, dst_ref=rows(*block),
                send_sem=send_sems.at[k], recv_sem=recv_sems.at[k], device_id=to, device_id_type=MESH)

        mine = pltpu.make_async_copy(x_ref, rows(*me), local_sem)
        mine.start()
        first = [copy(0, me, sibling, src=x_ref)]
        first += [copy(1 + j, me, (*chip, c), src=x_ref) for j, chip in enumerate(chips)]
        for cp in first:
            cp.start()
        passed = [copy(4 + j, (*chip, c), sibling) for j, chip in enumerate(chips)]
        for j, chip in enumerate(chips):
            copy(1 + j, (*chip, c), me).wait_recv()
            passed[j].start()
        copy(0, sibling, me).wait_recv()
        for j, chip in enumerate(chips):
            copy(4 + j, (*chip, 1 - c), me).wait_recv()
        for cp in first + passed:
            cp.wait_send()
        mine.wait()

    return _comm_call(body, name=name, out_shape=jax.ShapeDtypeStruct((N_DEV * m_per, n), a.dtype), n_in=1,
                      scratch=[pltpu.SemaphoreType.DMA((7,)), pltpu.SemaphoreType.DMA((7,)), pltpu.SemaphoreType.DMA])(a)


PACK_COLS = 1024
PACK_ROW_MULTIPLE = 32

WEIGHT_NAMES = ["ffn1_norm", "ffn1_w_gate_up", "ffn1_w_down", "mix_norm", "ffn2_norm", "ffn2_w_gate_up", "ffn2_w_down",
                "gdn_w_in", "gdn_conv", "gdn_A_log", "gdn_dt_bias", "gdn_out_norm", "fox_w_in", "w_out", "mem_norm",
                "mem_w_kv", "kv_norm", "kv_w", "kv_b_f", "final_norm"]
SHARDED = [("ffn1_w_gate_up", 2), ("ffn1_w_down", 1), ("ffn2_w_gate_up", 2), ("ffn2_w_down", 1), ("gdn_w_in", 2),
           ("gdn_conv", 2), ("fox_w_in", 1), ("w_out", 1), ("mem_w_kv", 1), ("kv_w", 0)]
REPLICATED = [n for n in WEIGHT_NAMES if n not in dict(SHARDED)]


PACK_PIECE_ROWS = 16


def _rows_of(size):
    return -(-size // (PACK_COLS * PACK_PIECE_ROWS)) * PACK_PIECE_ROWS


def pack(pieces, dtype, row_multiple=PACK_ROW_MULTIPLE):
    bufs, total = [], 0
    for p in pieces:
        flat = p.astype(dtype).reshape(-1)
        rows = _rows_of(flat.size)
        bufs.append(jnp.pad(flat, (0, rows * PACK_COLS - flat.size)).reshape(rows, PACK_COLS))
        total += rows
    pad = -total % row_multiple
    if pad:
        bufs.append(jnp.zeros((pad, PACK_COLS), dtype))
    return jnp.concatenate(bufs, axis=0)


def unpack(buf, shapes):
    out, row = [], 0
    for shape in shapes:
        size = 1
        for s in shape:
            size *= s
        rows = _rows_of(size)
        out.append(buf[row:row + rows].reshape(-1)[:size].reshape(shape))
        row += rows
    return out


def _row(vec, width=None):
    vec = vec.astype(F32).reshape(1, -1)
    if width is not None and vec.shape[1] < width:
        vec = jnp.pad(vec, ((0, 0), (0, width - vec.shape[1])))
    return vec


ROW_TILE = 512
GDN_PROJ_WIDTH = 4 * GDN_WIDTH + MEM_WIDTH + LANES
GDN_Z_COL, GDN_QMEM_COL, GDN_AB_COL = 3, 4 * GDN_WIDTH // MEM_WIDTH, (4 * GDN_WIDTH + MEM_WIDTH) // LANES
FOX_QMEM_COL = FOX_WIDTH // MEM_WIDTH
KV_PAD_WIDTH = 2 * FOX_WIDTH + LANES


def rms_fwd(x, gain_row, out_dtype=BF16):
    return rowwise(f_rmsnorm, [x], [gain_row], [(x.shape[1], out_dtype)], tile=ROW_TILE, name="rms_fwd")[0]


def rms_bwd(x, gain_row, dy, dres=None):
    return rowwise_bwd(f_rmsnorm, [x], [gain_row], [dy], tile=ROW_TILE, name="rms_bwd", row_grads=[F32],
                       const_grads=[True], add=None if dres is None else {0: dres})


def _ffn_tiles(n_rows):
    return _pick(n_rows, (512, 256, 128)), _pick(FFN_HIDDEN, (1408, 256, 128))


def ffn_up_act(n, wgu):
    n_rows, d = n.shape
    tm, tn = _ffn_tiles(n_rows)
    nj = FFN_HIDDEN // tn

    def body(n_ref, wg_ref, wu_ref, gu_ref, act_ref):
        x = n_ref[...].astype(BF16)
        g = bdot.nn(x, wg_ref[...])
        u = bdot.nn(x, wu_ref[...])
        gu_ref[0] = g.astype(gu_ref.dtype)
        gu_ref[1] = u.astype(gu_ref.dtype)
        act_ref[...] = (_silu(g) * u).astype(act_ref.dtype)

    return _pcall(body, name="ffn_up_act",
                  out_shape=[jax.ShapeDtypeStruct((2, n_rows, FFN_HIDDEN), BF16), jax.ShapeDtypeStruct((n_rows, FFN_HIDDEN), BF16)],
                  grid=(nj, n_rows // tm),
                  in_specs=[pl.BlockSpec((tm, d), lambda j, i: (i, 0)), pl.BlockSpec((d, tn), lambda j, i: (0, j)),
                            pl.BlockSpec((d, tn), lambda j, i: (0, nj + j))],
                  out_specs=[pl.BlockSpec((2, tm, tn), lambda j, i: (0, i, j)), pl.BlockSpec((tm, tn), lambda j, i: (i, j))],
                  sem=("parallel", "parallel"))(n, wgu, wgu)


def ffn_down_dx_act(dh, wd, gu):
    n_rows, d = dh.shape
    tm, tn = _ffn_tiles(n_rows)

    def body(dh_ref, wd_ref, gu_ref, dgu_ref):
        dact = 0.5 * bdot.nt(dh_ref[...], wd_ref[...])
        gate, up = gu_ref[0].astype(F32), gu_ref[1].astype(F32)
        sg = _sigmoid(gate)
        dgu_ref[0] = (dact * up * (sg * (1.0 + gate * (1.0 - sg)))).astype(dgu_ref.dtype)
        dgu_ref[1] = (dact * (gate * sg)).astype(dgu_ref.dtype)

    blk = pl.BlockSpec((2, tm, tn), lambda j, i: (0, i, j))
    return _pcall(body, name="ffn_down_dx_act", out_shape=jax.ShapeDtypeStruct((2, n_rows, FFN_HIDDEN), BF16),
                  grid=(FFN_HIDDEN // tn, n_rows // tm),
                  in_specs=[pl.BlockSpec((tm, d), lambda j, i: (i, 0)), pl.BlockSpec((tn, d), lambda j, i: (j, 0)), blk],
                  out_specs=blk, sem=("parallel", "parallel"))(dh, wd, gu)


def ffn_fwd(h, gain_row, wgu, wd):
    n = rms_fwd(h, gain_row)
    gu, act = ffn_up_act(n, wgu)
    return mm(act, wd, scale=0.5, res=h, name="ffn_down"), (h, n, gu, act)


def ffn_bwd(dh, saved, gain_row, wgu, wd):
    h, n, gu, act = saved
    dgu = ffn_down_dx_act(dh, wd, gu)
    dwd = mm(act, dh, ta=True, scale=0.5, name="ffn_down_dw")
    dwgu = mm(n, dgu, ta=True, b_split=True, name="ffn_up_dw")
    dn = mm(dgu, wgu, tb=True, a_split=True, out_dtype=BF16, name="ffn_up_dx")
    dh, dgain = rms_bwd(h, gain_row, dn, dh)
    return dh, dwgu, dwd, dgain


def gdn_fwd(proj, w8, a_row, dt_row, onorm_row):
    wide = [(GDN_WIDTH, F32)] * 5
    xc = conv_fwd(proj, w8, width=3 * GDN_WIDTH)
    q, k, v, gc, beta = rowwise(f_gdn_pre, [xc, (proj, LANES, GDN_AB_COL)], [a_row, dt_row], wide, tile=GDN_CHUNK,
                                name="gdn_pre_fwd")
    u, w, qk, kt, qh, inv = rowwise(f_gdn_intra, [q, k, v, gc, beta], [], wide + wide[:1], tile=GDN_CHUNK,
                                    name="gdn_intra_fwd")
    o, vn, sin = gdn_scan_fwd(u, w, qk, kt, qh, gc)
    main = rowwise(f_gdn_post, [o, (proj, GDN_WIDTH, GDN_Z_COL)], [onorm_row], [(GDN_WIDTH, BF16)], tile=ROW_TILE,
                   name="gdn_post_fwd")[0]
    return main, (xc, q, k, v, gc, beta, inv, w, qk, kt, qh, vn, sin, o)


def gdn_bwd(dmain, proj, saved, w8, a_row, dt_row, onorm_row):
    xc, q, k, v, gc, beta, inv, w, qk, kt, qh, vn, sin, o = saved
    do, dz, donorm = rowwise_bwd(f_gdn_post, [o, (proj, GDN_WIDTH, GDN_Z_COL)], [onorm_row], [dmain], tile=ROW_TILE,
                                 name="gdn_post_bwd", row_grads=[F32, BF16], const_grads=[True])
    du, dw, dqk, dkt, dqh, dgl = gdn_scan_bwd(do, w, qk, kt, qh, gc, vn, sin)
    dq, dk, dv, dgc, dbeta = gdn_intra_bwd(q, k, v, gc, beta, inv, (du, dw, dqk, dkt, dqh), dgl)
    dxc, dab, da, ddt = rowwise_bwd(f_gdn_pre, [xc, (proj, LANES, GDN_AB_COL)], [a_row, dt_row], [dq, dk, dv, dgc, dbeta],
                                    tile=GDN_CHUNK, name="gdn_pre_bwd", row_grads=[F32, BF16], const_grads=[True, True])
    dqkv, dw8 = conv_bwd(proj, w8, dxc, width=3 * GDN_WIDTH)
    return dqkv, dz, dab, dw8, da, ddt, donorm


def mem_fwd(q, kmem, vmem):
    return rowwise(f_mem_attn, [q], [kmem, vmem], [(MEM_WIDTH, BF16)], tile=ROW_TILE, name="mem_attn_fwd")[0]


def mem_bwd(q, kmem, vmem, dout):
    return rowwise_bwd(f_mem_attn, [q], [kmem, vmem], [dout], tile=ROW_TILE, name="mem_attn_bwd", row_grads=[BF16],
                       const_grads=[True, True])


def forward_backward(xs, mems, target, P):
    depth, n_a = 4, 2
    G = {}
    mem_gain = _row(P["mem_norm"])
    mem_n = rms_fwd(mems, mem_gain)
    h = xs
    saved = []
    shared = None
    for l in range(depth):
        h0 = h
        h1, s1 = ffn_fwd(h0, _row(P["ffn1_norm"][l]), P["ffn1_w_gate_up"][l], P["ffn1_w_down"][l])
        u = rms_fwd(h1, _row(P["mix_norm"][l]))
        kvm = mm(mem_n, P["mem_w_kv"][l], name="mem_kv")
        kmem, vmem = kvm[:, :MEM_WIDTH], kvm[:, MEM_WIDTH:]
        if l < n_a:
            gp = (P["conv8"][l], _row(P["gdn_A_log"][l], LANES), _row(P["gdn_dt_bias"][l], LANES), _row(P["gdn_out_norm"][l]))
            proj = mm(u, P["gdn_w_in_pad"][l], name="gdn_in")
            main, sm = gdn_fwd(proj, *gp)
            qm = (proj, MEM_WIDTH, GDN_QMEM_COL)
        else:
            proj = mm(u, P["fox_w_in"][l - n_a], out_dtype=BF16, name="fox_in")
            kv, ka, cb = shared
            qa = fox_augment(proj, cb, True)
            main, lse = fox_fwd(qa, kv, ka)
            sm = (main, lse, qa)
            qm = (proj, MEM_WIDTH, FOX_QMEM_COL)
        mo = mem_fwd(qm, kmem, vmem)
        cat = jnp.concatenate([main, mo], axis=1)
        h2 = mm(cat, P["w_out"][l], res=h1, name="mix_out")
        h3, s2 = ffn_fwd(h2, _row(P["ffn2_norm"][l]), P["ffn2_w_gate_up"][l], P["ffn2_w_down"][l])
        saved.append((s1, h1, u, kmem, vmem, proj, sm, qm, cat, s2))
        h = h3
        if l == n_a - 1:
            nkv = rms_fwd(h, _row(P["kv_norm"]))
            kv = mm(nkv, P["kv_w_pad"][:, :2 * FOX_WIDTH], out_dtype=BF16, name="fox_kv")
            f = mm(nkv, P["kv_w_pad"][:, 2 * FOX_WIDTH:], name="fox_f")
            bf_row = _row(P["kv_b_f"], LANES)
            cb = fox_gate_fwd(f, bf_row)
            shared = (kv, fox_augment(kv, cb, False), cb)
            kv_saved = (h, nkv, f, bf_row)

    part, dy = loss_head(h, _row(P["final_norm"]), target)
    dh, G["final_norm"] = rms_bwd(h, _row(P["final_norm"]), dy)

    per_layer = {n: [None] * depth for n in ("ffn1_norm", "ffn1_w_gate_up", "ffn1_w_down", "mix_norm", "ffn2_norm",
                                             "ffn2_w_gate_up", "ffn2_w_down", "w_out", "mem_w_kv")}
    gdn_g = {n: [None] * n_a for n in ("gdn_w_in_pad", "conv8", "gdn_A_log", "gdn_dt_bias", "gdn_out_norm")}
    fox_g = [None] * (depth - n_a)
    dmem_n = None
    dkv_acc = dcb_acc = None
    for l in reversed(range(depth)):
        s1, h1, u, kmem, vmem, proj, sm, qm, cat, s2 = saved[l]
        if l == n_a - 1:
            hk, nkv, f, bf_row = kv_saved
            dk, dv, dcrow = dkv_acc
            df, dbf = fox_gate_bwd(f, bf_row, dcrow, dcb_acc)
            dp = jnp.concatenate([dk.astype(BF16), dv.astype(BF16), df.astype(BF16)], axis=1)
            G["kv_w_pad"] = mm(nkv, dp, ta=True, name="fox_kv_dw")
            G["kv_b_f"] = dbf
            dnkv = mm(dp, P["kv_w_pad"], tb=True, out_dtype=BF16, name="fox_kv_dx")
            dh, G["kv_norm"] = rms_bwd(hk, _row(P["kv_norm"]), dnkv, dh)
        dh, per_layer["ffn2_w_gate_up"][l], per_layer["ffn2_w_down"][l], per_layer["ffn2_norm"][l] = ffn_bwd(
            dh, s2, _row(P["ffn2_norm"][l]), P["ffn2_w_gate_up"][l], P["ffn2_w_down"][l])
        dcat = mm(dh, P["w_out"][l], tb=True, out_dtype=BF16, name="mix_out_dx")
        per_layer["w_out"][l] = mm(cat, dh, ta=True, name="mix_out_dw")
        dqm, dkm, dvm = mem_bwd(qm, kmem, vmem, (dcat, MEM_WIDTH, FOX_QMEM_COL))
        dkvm = jnp.concatenate([dkm, dvm], axis=1)
        per_layer["mem_w_kv"][l] = mm(mem_n, dkvm, ta=True, name="mem_kv_dw")
        dmem_n = mm(dkvm, P["mem_w_kv"][l], tb=True, res=dmem_n, name="mem_kv_dx")
        dmain = (dcat, GDN_WIDTH, 0)
        if l < n_a:
            gp = (P["conv8"][l], _row(P["gdn_A_log"][l], LANES), _row(P["gdn_dt_bias"][l], LANES), _row(P["gdn_out_norm"][l]))
            dqkv, dz, dab, gdn_g["conv8"][l], gdn_g["gdn_A_log"][l], gdn_g["gdn_dt_bias"][l], gdn_g["gdn_out_norm"][l] = gdn_bwd(
                dmain, proj, sm, *gp)
            dproj = jnp.concatenate([dqkv, dz, dqm, dab], axis=1)
            gdn_g["gdn_w_in_pad"][l] = mm(u, dproj, ta=True, name="gdn_in_dw")
            du = mm(dproj, P["gdn_w_in_pad"][l], tb=True, out_dtype=BF16, name="gdn_in_dx")
        else:
            o, lse, qa = sm
            kv, ka, _ = shared
            dq, dcb_acc = fox_bwd_dq(qa, kv, ka, o, lse, dmain, dcb_acc)
            dkv_acc = fox_bwd_dkv(qa, kv, ka, o, lse, dmain, dkv_acc)
            dproj = jnp.concatenate([dq, dqm], axis=1)
            fox_g[l - n_a] = mm(u, dproj, ta=True, name="fox_in_dw")
            du = mm(dproj, P["fox_w_in"][l - n_a], tb=True, out_dtype=BF16, name="fox_in_dx")
        dh, per_layer["mix_norm"][l] = rms_bwd(h1, _row(P["mix_norm"][l]), du, dh)
        dh, per_layer["ffn1_w_gate_up"][l], per_layer["ffn1_w_down"][l], per_layer["ffn1_norm"][l] = ffn_bwd(
            dh, s1, _row(P["ffn1_norm"][l]), P["ffn1_w_gate_up"][l], P["ffn1_w_down"][l])

    (G["mem_norm"],) = rowwise_bwd(f_rmsnorm, [mems], [mem_gain], [dmem_n], tile=ROW_TILE, name="mem_norm_bwd",
                                   row_grads=[None], const_grads=[True])
    for n, v in per_layer.items():
        G[n] = jnp.stack(v)
    for n, v in gdn_g.items():
        G[n] = jnp.stack(v)
    G["fox_w_in"] = jnp.stack(fox_g)
    return part, dh, G


_GDN_O0 = 4 * GDN_WIDTH
_GDN_O1 = _GDN_O0 + 2 * GDN_HEADS
_KV_WIDTH = 2 * FOX_WIDTH + FOX_HEADS


def derived_weights(gdn_w_in, gdn_conv, kv_w=None):
    zeros = jnp.zeros(gdn_w_in.shape[:-1] + (LANES - 2 * GDN_HEADS,), gdn_w_in.dtype)
    out = dict(
        gdn_w_in_pad=jnp.concatenate([gdn_w_in[..., :_GDN_O0], gdn_w_in[..., _GDN_O1:], gdn_w_in[..., _GDN_O0:_GDN_O1], zeros], axis=-1),
        conv8=jnp.pad(gdn_conv.astype(F32), ((0, 0), (0, 8 - CONV_WIDTH), (0, 0))))
    if kv_w is not None:
        out["kv_w_pad"] = jnp.pad(kv_w, ((0, 0), (0, KV_PAD_WIDTH - _KV_WIDTH)))
    return out


def reference_layout(G):
    gp = G["gdn_w_in_pad"]
    out = dict(G)
    out["gdn_w_in"] = jnp.concatenate([gp[..., :_GDN_O0], gp[..., _GDN_O0 + MEM_WIDTH:_GDN_O0 + MEM_WIDTH + 2 * GDN_HEADS],
                                       gp[..., _GDN_O0:_GDN_O0 + MEM_WIDTH]], axis=-1)
    out["gdn_conv"] = G["conv8"][:, :CONV_WIDTH]
    out["kv_w"] = G["kv_w_pad"][:, :_KV_WIDTH]
    out["gdn_A_log"] = G["gdn_A_log"][:, 0, :GDN_HEADS]
    out["gdn_dt_bias"] = G["gdn_dt_bias"][:, 0, :GDN_HEADS]
    out["gdn_out_norm"] = G["gdn_out_norm"][:, 0, :]
    out["kv_b_f"] = G["kv_b_f"][0, :FOX_HEADS]
    for n in ("ffn1_norm", "mix_norm", "ffn2_norm"):
        out[n] = G[n][:, 0, :]
    for n in ("mem_norm", "kv_norm", "final_norm"):
        out[n] = G[n][0]
    return {n: out[n] for n in WEIGHT_NAMES}


EXCHANGED = [("ffn1_w_gate_up", 2, 0), ("ffn1_w_down", 1, 0), ("ffn2_w_gate_up", 2, 0), ("ffn2_w_down", 1, 0),
             ("gdn_w_in", 2, 0), ("fox_w_in", 1, 0), ("w_out", 1, 0), ("mem_w_kv", 1, 0), ("kv_w", 0, 1)]
GDN_IN_SHARD = (4 * GDN_WIDTH + 2 * GDN_HEADS + MEM_WIDTH) // N_CHIPS
GDN_IN_SLOT = 896


def _slab(ref, axis_slices):
    idx = [slice(None)] * len(ref.shape)
    for axis, (start, size) in axis_slices.items():
        idx[axis] = pl.ds(start, size)
    return ref.at[tuple(idx)]


def _comm_multi(body, *, name, n_in, out_shapes, scratch):
    return pl.pallas_call(body, name=name, out_shape=out_shapes, in_specs=[_HBM] * n_in, out_specs=[_HBM] * len(out_shapes),
                          scratch_shapes=scratch, compiler_params=pltpu.CompilerParams(has_side_effects=True))


def gather_shards(shards, layout):
    n = len(shards)
    fulls = [tuple(d * (N_CHIPS if a == sa else 1) for a, d in enumerate(s.shape)) for s, (sa, _) in zip(shards, layout)]

    def body(*refs):
        ins, outs = refs[:n], refs[n:2 * n]
        send_sems, recv_sems, local_sems = refs[2 * n:]
        x, y, c = lax.axis_index("x"), lax.axis_index("y"), lax.axis_index("c")
        me, sibling = (x, y, c), (x, y, 1 - c)
        chips = [(1 - x, y), (x, 1 - y), (1 - x, 1 - y)]

        def region(w, px, py, pc):
            (sa, ha), shard = layout[w], shards[w].shape
            return _slab(outs[w], {sa: ((2 * px + py) * shard[sa], shard[sa]), ha: (pc * (shard[ha] // 2), shard[ha] // 2)})

        def my_half(w):
            ha, shard = layout[w][1], shards[w].shape
            return _slab(ins[w], {ha: (c * (shard[ha] // 2), shard[ha] // 2)})

        def copy(w, k, block, to, src=None):
            return pltpu.make_async_remote_copy(
                src_ref=region(w, *block) if src is None else src, dst_ref=region(w, *block),
                send_sem=send_sems.at[7 * w + k], recv_sem=recv_sems.at[7 * w + k], device_id=to, device_id_type=MESH)

        mine, first, passed = [], [], []
        for w in range(n):
            mine.append(pltpu.make_async_copy(my_half(w), region(w, *me), local_sems.at[w]))
            mine[w].start()
            first.append([copy(w, 0, me, sibling, src=my_half(w))]
                         + [copy(w, 1 + j, me, (*chip, c), src=my_half(w)) for j, chip in enumerate(chips)])
            for cp in first[w]:
                cp.start()
            passed.append([copy(w, 4 + j, (*chip, c), sibling) for j, chip in enumerate(chips)])
        for w in range(n):
            for j, chip in enumerate(chips):
                copy(w, 1 + j, (*chip, c), me).wait_recv()
                passed[w][j].start()
        for w in range(n):
            copy(w, 0, sibling, me).wait_recv()
            for j, chip in enumerate(chips):
                copy(w, 4 + j, (*chip, 1 - c), me).wait_recv()
        for w in range(n):
            for cp in first[w] + passed[w]:
                cp.wait_send()
            mine[w].wait()

    return _comm_multi(body, name="gather_shards", n_in=n,
                       out_shapes=[jax.ShapeDtypeStruct(f, s.dtype) for f, s in zip(fulls, shards)],
                       scratch=[pltpu.SemaphoreType.DMA((7 * n,)), pltpu.SemaphoreType.DMA((7 * n,)),
                                pltpu.SemaphoreType.DMA((n,))])(*shards)


def swap_other_halves(arrays, layout):
    n = len(arrays)
    halves = [tuple(d // 2 if a == ha else d for a, d in enumerate(g.shape)) for g, (_, ha) in zip(arrays, layout)]

    def body(*refs):
        ins, outs = refs[:n], refs[n:2 * n]
        send_sems, recv_sems = refs[2 * n:]
        x, y, c = lax.axis_index("x"), lax.axis_index("y"), lax.axis_index("c")
        copies = []
        for w in range(n):
            ha, size = layout[w][1], halves[w][layout[w][1]]
            copies.append(pltpu.make_async_remote_copy(
                src_ref=_slab(ins[w], {ha: ((1 - c) * size, size)}), dst_ref=outs[w], send_sem=send_sems.at[w],
                recv_sem=recv_sems.at[w], device_id=(x, y, 1 - c), device_id_type=MESH))
            copies[w].start()
        for cp in copies:
            cp.wait()

    return _comm_multi(body, name="grad_pair_swap", n_in=n,
                       out_shapes=[jax.ShapeDtypeStruct(h, g.dtype) for h, g in zip(halves, arrays)],
                       scratch=[pltpu.SemaphoreType.DMA((n,)), pltpu.SemaphoreType.DMA((n,))])(*arrays)


def scatter_to_chips(arrays, layout):
    n = len(arrays)
    slabs = [tuple(d // N_CHIPS if a == sa else d for a, d in enumerate(p.shape)) for p, (sa, _) in zip(arrays, layout)]

    def body(*refs):
        ins, outs = refs[:n], refs[n:2 * n]
        send_sems, recv_sems, local_sems = refs[2 * n:]
        x, y, c = lax.axis_index("x"), lax.axis_index("y"), lax.axis_index("c")
        me = 2 * x + y

        def slab(w, k):
            sa, size = layout[w][0], slabs[w][layout[w][0]]
            return _slab(ins[w], {sa: (k * size, size)})

        local, copies = [], []
        for w in range(n):
            local.append(pltpu.make_async_copy(slab(w, me), outs[w].at[me], local_sems.at[w]))
            local[w].start()
            for j, (px, py) in enumerate([(1 - x, y), (x, 1 - y), (1 - x, 1 - y)]):
                copies.append(pltpu.make_async_remote_copy(
                    src_ref=slab(w, 2 * px + py), dst_ref=outs[w].at[me], send_sem=send_sems.at[3 * w + j],
                    recv_sem=recv_sems.at[3 * w + j], device_id=(px, py, c), device_id_type=MESH))
                copies[-1].start()
        for cp in copies:
            cp.wait()
        for cp in local:
            cp.wait()

    return _comm_multi(body, name="grad_all_to_all", n_in=n,
                       out_shapes=[jax.ShapeDtypeStruct((N_CHIPS,) + s, p.dtype) for s, p in zip(slabs, arrays)],
                       scratch=[pltpu.SemaphoreType.DMA((3 * n,)), pltpu.SemaphoreType.DMA((3 * n,)),
                                pltpu.SemaphoreType.DMA((n,))])(*arrays)


def swap_with_sibling(arrays):
    n = len(arrays)

    def body(*refs):
        ins, outs = refs[:n], refs[n:2 * n]
        send_sems, recv_sems = refs[2 * n:]
        x, y, c = lax.axis_index("x"), lax.axis_index("y"), lax.axis_index("c")
        copies = [pltpu.make_async_remote_copy(src_ref=ins[w], dst_ref=outs[w], send_sem=send_sems.at[w], recv_sem=recv_sems.at[w],
                                               device_id=(x, y, 1 - c), device_id_type=MESH) for w in range(n)]
        for cp in copies:
            cp.start()
        for cp in copies:
            cp.wait()

    return _comm_multi(body, name="grad_half_swap", n_in=n, out_shapes=[jax.ShapeDtypeStruct(a.shape, a.dtype) for a in arrays],
                       scratch=[pltpu.SemaphoreType.DMA((n,)), pltpu.SemaphoreType.DMA((n,))])(*arrays)


def join_halves(mine, other, half_axis, c_arr, *, name):
    a0, a1, a2 = mine.shape
    tile = _row_tile(a1, a2)

    def body(c_ref, m_ref, o_ref, out_ref):
        for half in range(2):
            @pl.when(c_ref[0] == half)
            def _(half=half):
                out_ref[half] = m_ref[...]
                out_ref[1 - half] = o_ref[...]

    blk = pl.BlockSpec((None, tile, a2), lambda i, j, c_ref: (i, j, 0))
    if half_axis == 0:
        out_shape, out_blk = (2, a0, a1, a2), pl.BlockSpec((2, None, tile, a2), lambda i, j, c_ref: (0, i, j, 0))
    else:
        out_shape, out_blk = (a0, 2, a1, a2), pl.BlockSpec((None, 2, tile, a2), lambda i, j, c_ref: (i, 0, j, 0))
    spec = pltpu.PrefetchScalarGridSpec(num_scalar_prefetch=1, grid=(a0, a1 // tile), in_specs=[blk, blk], out_specs=out_blk)
    out = pl.pallas_call(body, name=name, out_shape=jax.ShapeDtypeStruct(out_shape, mine.dtype), grid_spec=spec,
                         compiler_params=pltpu.CompilerParams(vmem_limit_bytes=VMEM_LIMIT_BYTES,
                                                              dimension_semantics=("parallel", "parallel")))(c_arr, mine, other)
    return out.reshape((2 * a0, a1, a2) if half_axis == 0 else (a0, 2 * a1, a2))


def _row_tile(rows, cols, itemsize=4, budget=2 * 1024 * 1024):
    for t in (1024, 512, 256, 128, 64, 32, 16):
        if rows % t == 0 and t * cols * itemsize <= budget:
            return t
    return rows


def add_own_half(full, recv, half_axis, c_arr, *, name):
    a0, a1, a2 = recv.shape
    tile = _row_tile(a1, a2)

    def body(c_ref, f_ref, r_ref, o_ref):
        o_ref[...] = (f_ref[...] + r_ref[...]).astype(o_ref.dtype)

    if half_axis == 0:
        f_spec = pl.BlockSpec((None, tile, a2), lambda i, j, c_ref: (c_ref[0] * a0 + i, j, 0))
    else:
        f_spec = pl.BlockSpec((None, tile, a2), lambda i, j, c_ref: (i, c_ref[0] * (a1 // tile) + j, 0))
    blk = pl.BlockSpec((None, tile, a2), lambda i, j, c_ref: (i, j, 0))
    spec = pltpu.PrefetchScalarGridSpec(num_scalar_prefetch=1, grid=(a0, a1 // tile), in_specs=[f_spec, blk], out_specs=blk)
    return pl.pallas_call(body, name=name, out_shape=jax.ShapeDtypeStruct(recv.shape, BF16), grid_spec=spec,
                          compiler_params=pltpu.CompilerParams(vmem_limit_bytes=VMEM_LIMIT_BYTES,
                                                               dimension_semantics=("parallel", "parallel")))(c_arr, full, recv)


def sum_slots(q, *, name):
    _, a0, a1, a2 = q.shape
    tile = _row_tile(a1, a2, budget=1024 * 1024)

    def body(q_ref, o_ref):
        total = q_ref[0].astype(F32)
        for k in range(1, N_CHIPS):
            total = total + q_ref[k].astype(F32)
        o_ref[...] = total

    return _pcall(body, name=name, out_shape=jax.ShapeDtypeStruct((a0, a1, a2), F32), grid=(a0, a1 // tile),
                  in_specs=[pl.BlockSpec((N_CHIPS, None, tile, a2), lambda i, j: (0, i, j, 0))],
                  out_specs=pl.BlockSpec((None, tile, a2), lambda i, j: (i, j, 0)), sem=("parallel", "parallel"))(q)


def gather_weights(W):
    shards = []
    for name, _, _ in EXCHANGED:
        w = W[name].astype(BF16)
        if name == "gdn_w_in":
            w = jnp.pad(w, ((0, 0), (0, 0), (0, GDN_IN_SLOT - GDN_IN_SHARD)))
        if name == "kv_w":
            w = jnp.pad(w, ((0, 0), (0, KV_PAD_WIDTH - _KV_WIDTH)))[None]
        shards.append(w)
    fulls = dict(zip([n for n, _, _ in EXCHANGED], gather_shards(shards, [(sa, ha) for _, sa, ha in EXCHANGED])))
    slots = fulls["gdn_w_in"]
    fulls["gdn_w_in"] = jnp.concatenate([slots[..., k * GDN_IN_SLOT:k * GDN_IN_SLOT + GDN_IN_SHARD] for k in range(N_CHIPS)], axis=-1)
    fulls["kv_w_pad"] = fulls.pop("kv_w").reshape(-1, KV_PAD_WIDTH)
    conv = pack([W["gdn_conv"]], F32, row_multiple=8)
    conv_all = all_gather8(conv, name="gather_conv").reshape(N_DEV, conv.shape[0], PACK_COLS)
    fulls["gdn_conv"] = jnp.concatenate([unpack(conv_all[2 * k], [W["gdn_conv"].shape])[0] for k in range(N_CHIPS)], axis=-1)
    return fulls


def reduce_gradients(G):
    layout = [(sa, ha) for _, sa, ha in EXCHANGED]
    c_arr = lax.axis_index("c").astype(jnp.int32).reshape(1)
    received = swap_other_halves(G, layout)
    pairs = [add_own_half(g, r, ha, c_arr, name="grad_pair_sum") for g, r, (_, ha) in zip(G, received, layout)]
    slots = scatter_to_chips(pairs, layout)
    halves = [sum_slots(q, name="grad_chip_sum") for q in slots]
    others = swap_with_sibling(halves)
    return [join_halves(h, o, ha, c_arr, name="grad_join_halves") for h, o, (_, ha) in zip(halves, others, layout)]


def allreduce_small(G, names):
    packed = pack([G[n] for n in names], F32, row_multiple=8)
    gathered = all_gather8(packed, name="gather_small_grads").reshape(N_DEV, packed.shape[0], PACK_COLS)
    total = sum_leading(gathered, name="sum_small_grads")
    return dict(zip(names, unpack(total, [G[n].shape for n in names])))


def kernel(x, mem, *rest):
    n_w = len(WEIGHT_NAMES)
    W = dict(zip(WEIGHT_NAMES, rest[:n_w]))
    target = rest[n_w]
    M = dict(zip(WEIGHT_NAMES, rest[n_w + 1:2 * n_w + 1]))
    V = dict(zip(WEIGHT_NAMES, rest[2 * n_w + 1:3 * n_w + 1]))

    full = gather_weights(W)
    P = {n: W[n] for n in REPLICATED}
    P.update({n: full[n] for n in ("ffn1_w_gate_up", "ffn1_w_down", "ffn2_w_gate_up", "ffn2_w_down", "fox_w_in", "w_out",
                                   "mem_w_kv", "kv_w_pad")})
    derived = derived_weights(full["gdn_w_in"], full["gdn_conv"])
    P.update(gdn_w_in_pad=derived["gdn_w_in_pad"], conv8=derived["conv8"])

    part, dx, G = forward_backward(x[0], mem[0], target[0], P)
    loss = lax.psum(0.5 / x.shape[-1] * jnp.sum(part), ("x", "y", "c"))

    ref = reference_layout(G)
    exchange = {n: ref[n] for n, _, _ in EXCHANGED}
    exchange["gdn_w_in"] = jnp.concatenate(
        [jnp.pad(ref["gdn_w_in"][..., k * GDN_IN_SHARD:(k + 1) * GDN_IN_SHARD], ((0, 0), (0, 0), (0, GDN_IN_SLOT - GDN_IN_SHARD)))
         for k in range(N_CHIPS)], axis=-1)
    exchange["kv_w"] = G["kv_w_pad"].reshape(N_CHIPS, -1, KV_PAD_WIDTH)
    shards = dict(zip([n for n, _, _ in EXCHANGED], reduce_gradients([exchange[n] for n, _, _ in EXCHANGED])))
    shards["gdn_w_in"] = shards["gdn_w_in"][..., :GDN_IN_SHARD]
    shards["kv_w"] = shards["kv_w"][0, :, :_KV_WIDTH]
    grads = allreduce_small(ref, REPLICATED + ["gdn_conv"])
    conv_cols = W["gdn_conv"].shape[-1]
    chip = 2 * lax.axis_index("x") + lax.axis_index("y")
    grads["gdn_conv"] = lax.dynamic_slice_in_dim(grads["gdn_conv"], chip * conv_cols, conv_cols, axis=2)
    grads.update(shards)

    outs = {n: adamw(W[n], grads[n], M[n], V[n], name="adamw_" + n) for n in WEIGHT_NAMES}
    return (loss, dx[None], *[grads[n] for n in WEIGHT_NAMES], *[outs[n][0] for n in WEIGHT_NAMES],
            *[outs[n][1] for n in WEIGHT_NAMES], *[outs[n][2] for n in WEIGHT_NAMES])
```

```python
import jax
import jax.numpy as jnp
from jax import lax
from jax.experimental import pallas as pl
from jax.experimental.pallas import tpu as pltpu

F32, BF16 = jnp.float32, jnp.bfloat16
HI = lax.Precision.HIGHEST
MESH = pl.DeviceIdType.MESH

VMEM_LIMIT_BYTES = 48 * 1024 * 1024
LANES = 128
EPS = 1e-6
NEG_INF = -1e30

D_MODEL = 1024
HEAD_DIM = 128
GDN_HEADS = 6
GDN_WIDTH = GDN_HEADS * HEAD_DIM
FOX_HEADS = 6
FOX_WIDTH = FOX_HEADS * HEAD_DIM
MEM_HEADS = 4
MEM_HEAD_DIM = 64
MEM_WIDTH = MEM_HEADS * MEM_HEAD_DIM
FFN_HIDDEN = 2816
CONV_WIDTH = 4
GDN_CHUNK = 128
N_CHIPS = 4
N_DEV = 8

ADAM_LR, ADAM_B1, ADAM_B2, ADAM_EPS, ADAM_WD, ADAM_STEP = 0.001, 0.9, 0.999, 1e-08, 0.01, 10


def _pcall(body, *, name, out_shape, grid=(), in_specs=None, out_specs=None, scratch=(), sem=None):
    params = dict(vmem_limit_bytes=VMEM_LIMIT_BYTES)
    if sem is not None:
        params["dimension_semantics"] = sem
    kw = dict(grid=grid, in_specs=in_specs, out_specs=out_specs) if grid else {}
    return pl.pallas_call(body, name=name, out_shape=out_shape, scratch_shapes=list(scratch),
                          compiler_params=pltpu.CompilerParams(**params), **kw)


def _pick(n, cands):
    for c in cands:
        if n % c == 0:
            return c
    return n


def _make_dot(dtype, precision):
    def raw(a, b, dims):
        return lax.dot_general(a.astype(dtype), b.astype(dtype), (dims, ((), ())),
                               precision=precision, preferred_element_type=F32)

    @jax.custom_vjp
    def dot(a, b):
        return raw(a, b, ((1,), (0,)))

    def fwd(a, b):
        return dot(a, b), (a, b)

    def bwd(resid, ct):
        a, b = resid
        return raw(ct, b, ((1,), (1,))).astype(a.dtype), raw(a, ct, ((0,), (0,))).astype(b.dtype)

    dot.defvjp(fwd, bwd)
    dot.nn = lambda a, b: raw(a, b, ((1,), (0,)))
    dot.nt = lambda a, b: raw(a, b, ((1,), (1,)))
    dot.tn = lambda a, b: raw(a, b, ((0,), (0,)))
    return dot


bdot = _make_dot(BF16, None)
fdot = _make_dot(F32, HI)
idot = _make_dot(F32, lax.Precision.HIGH)
sdot = idot


def _sigmoid(x):
    return 0.5 * jnp.tanh(0.5 * x) + 0.5


def _silu(x):
    return x * _sigmoid(x)


def _softplus(x):
    return jnp.maximum(x, 0.0) + jnp.log(1.0 + jnp.exp(-jnp.abs(x)))


def _log_sigmoid(x):
    return -_softplus(-x)


def _iota2(shape, dim):
    return lax.broadcasted_iota(jnp.int32, shape, dim)


def mm(a, b, *, ta=False, tb=False, a_split=False, b_split=False, out_dtype=F32, scale=1.0, res=None, name):
    assert not (a_split and ta) and not (b_split and tb)
    (K, M) = a.shape if ta else ((2 * a.shape[2], a.shape[1]) if a_split else a.shape[::-1])
    (N, Kb) = b.shape if tb else ((2 * b.shape[2], b.shape[1]) if b_split else b.shape[::-1])
    assert K == Kb, (a.shape, b.shape, ta, tb)
    tm = _pick(M, (1024, 1408, 512, 256, 128))
    tn = _pick(N, (1024, 1408, 1152, 1664, 768, 512, 384, 256, 128))
    tk = _pick(K, (1024, 512, 256, 128)) if ta else _pick(K, (1024, 1408, 1152, 1664, 512, 256, 128))
    assert not a_split or (K // 2) % tk == 0
    assert not b_split or (N // 2) % tn == 0
    nk = K // tk
    dims = (((0 if ta else 1,), (1 if tb else 0,)), ((), ()))

    def body(a_ref, b_ref, *rest):
        o_ref, acc = rest[-2], rest[-1]
        k = pl.program_id(2)

        @pl.when(k == 0)
        def _():
            acc[...] = jnp.zeros_like(acc)

        acc[...] += lax.dot_general(a_ref[...].astype(BF16), b_ref[...].astype(BF16), dims,
                                    preferred_element_type=F32)

        @pl.when(k == nk - 1)
        def _():
            out = acc[...] * scale
            if res is not None:
                out = out + rest[0][...].astype(F32)
            o_ref[...] = out.astype(o_ref.dtype)

    a_spec = pl.BlockSpec((tk, tm), lambda i, j, k: (k, i)) if ta else pl.BlockSpec((tm, tk), lambda i, j, k: (i, k))
    b_spec = pl.BlockSpec((tn, tk), lambda i, j, k: (j, k)) if tb else pl.BlockSpec((tk, tn), lambda i, j, k: (k, j))
    if a_split:
        per_half = K // 2 // tk
        a_spec = pl.BlockSpec((None, tm, tk), lambda i, j, k: (k // per_half, i, k % per_half))
    if b_split:
        per_half = N // 2 // tn
        b_spec = pl.BlockSpec((None, tk, tn), lambda i, j, k: (j // per_half, k, j % per_half))
    o_spec = pl.BlockSpec((tm, tn), lambda i, j, k: (i, j))
    ins, specs = [a, b], [a_spec, b_spec]
    if res is not None:
        ins.append(res)
        specs.append(o_spec)
    return _pcall(body, name=name, out_shape=jax.ShapeDtypeStruct((M, N), out_dtype),
                  grid=(M // tm, N // tn, nk), in_specs=specs, out_specs=o_spec,
                  scratch=[pltpu.VMEM((tm, tn), F32)], sem=("parallel", "parallel", "arbitrary"))(*ins)


def _row_spec(r, tile):
    if isinstance(r, tuple):
        arr, width, col = r
        return arr, pl.BlockSpec((tile, width), lambda i, col=col: (i, col))
    return r, pl.BlockSpec((tile, r.shape[1]), lambda i: (i, 0))


def _const_spec(c):
    return pl.BlockSpec(c.shape, lambda i: (0,) * c.ndim)


def rowwise(fn, rows, consts, outs, *, tile, name):
    arrs, specs = zip(*[_row_spec(r, tile) for r in rows])
    n_rows = arrs[0].shape[0]
    tile = min(tile, n_rows)
    n_in = len(rows) + len(consts)

    def body(*refs):
        res = fn(*[r[...] for r in refs[:n_in]])
        for o_ref, o in zip(refs[n_in:], res):
            o_ref[...] = o.astype(o_ref.dtype)

    arrs, specs = zip(*[_row_spec(r, tile) for r in rows])
    return _pcall(body, name=name,
                  out_shape=[jax.ShapeDtypeStruct((n_rows, w), dt) for w, dt in outs],
                  grid=(n_rows // tile,),
                  in_specs=list(specs) + [_const_spec(c) for c in consts],
                  out_specs=[pl.BlockSpec((tile, w), lambda i: (i, 0)) for w, _ in outs],
                  sem=("parallel",))(*arrs, *consts)


def rowwise_bwd(fn, rows, consts, cts, *, tile, name, row_grads, const_grads, add=None):
    arrs, _ = zip(*[_row_spec(r, tile) for r in rows])
    n_rows = arrs[0].shape[0]
    tile = min(tile, n_rows)
    arrs, specs = zip(*[_row_spec(r, tile) for r in rows])
    ct_arrs, ct_specs = zip(*[_row_spec(c, tile) for c in cts])
    add = add or {}
    add_idx = sorted(add)
    add_arrs, add_specs = (zip(*[_row_spec(add[i], tile) for i in add_idx]) if add_idx else ((), ()))
    nr, nc, nct, na = len(rows), len(consts), len(cts), len(add_idx)
    want_rows = [i for i, d in enumerate(row_grads) if d is not None]
    want_consts = [i for i, w in enumerate(const_grads) if w]

    def body(*refs):
        row_v = [r[...] for r in refs[:nr]]
        const_v = [r[...] for r in refs[nr:nr + nc]]
        ct_v = [r[...] for r in refs[nr + nc:nr + nc + nct]]
        add_v = {i: refs[nr + nc + nct + j][...] for j, i in enumerate(add_idx)}
        out_refs = refs[nr + nc + nct + na:]
        res, vjp = jax.vjp(fn, *row_v, *const_v)
        grads = vjp(tuple(c.astype(o.dtype) for c, o in zip(ct_v, res)))
        for o_ref, i in zip(out_refs, want_rows):
            g = grads[i].astype(F32)
            if i in add_v:
                g = g + add_v[i].astype(F32)
            o_ref[...] = g.astype(o_ref.dtype)
        first = pl.program_id(0) == 0
        for o_ref, i in zip(out_refs[len(want_rows):], want_consts):
            g = grads[nr + i].astype(F32)

            @pl.when(first)
            def _(o_ref=o_ref, g=g):
                o_ref[...] = g

            @pl.when(jnp.logical_not(first))
            def _(o_ref=o_ref, g=g):
                o_ref[...] += g

    def width(r):
        return r[1] if isinstance(r, tuple) else r.shape[1]

    out_shape = [jax.ShapeDtypeStruct((n_rows, width(rows[i])), row_grads[i]) for i in want_rows]
    out_shape += [jax.ShapeDtypeStruct(consts[i].shape, F32) for i in want_consts]
    out_specs = [pl.BlockSpec((tile, width(rows[i])), lambda i_: (i_, 0)) for i in want_rows]
    out_specs += [_const_spec(consts[i]) for i in want_consts]
    return _pcall(body, name=name, out_shape=out_shape, grid=(n_rows // tile,),
                  in_specs=list(specs) + [_const_spec(c) for c in consts] + list(ct_specs) + list(add_specs),
                  out_specs=out_specs, sem=("arbitrary",))(*arrs, *consts, *ct_arrs, *add_arrs)


def f_rmsnorm(x, g):
    x = x.astype(F32)
    return (x * lax.rsqrt(jnp.mean(x * x, axis=-1, keepdims=True) + EPS) * g,)


def _head_sel(first_lane):
    r, c = _iota2((LANES, GDN_WIDTH), 0), _iota2((LANES, GDN_WIDTH), 1)
    return (r == c // HEAD_DIM + first_lane).astype(F32)


def _tri(n, strict=False):
    r, c = _iota2((n, n), 0), _iota2((n, n), 1)
    return r > c if strict else r >= c


def f_gdn_pre(xc, ab, a_log, dt_bias):
    s = _silu(xc.astype(F32))
    qs, ks = [], []
    for h in range(GDN_HEADS):
        qh = s[:, h * HEAD_DIM:(h + 1) * HEAD_DIM]
        kh = s[:, GDN_WIDTH + h * HEAD_DIM:GDN_WIDTH + (h + 1) * HEAD_DIM]
        qs.append(qh * lax.rsqrt(jnp.sum(qh * qh, axis=-1, keepdims=True) + EPS) * (HEAD_DIM ** -0.5))
        ks.append(kh * lax.rsqrt(jnp.sum(kh * kh, axis=-1, keepdims=True) + EPS))
    q, k = jnp.concatenate(qs, axis=1), jnp.concatenate(ks, axis=1)
    v = s[:, 2 * GDN_WIDTH:]
    ab = ab.astype(F32)
    g = -jnp.exp(a_log) * _softplus(ab + dt_bias)
    gc = fdot(_tri(GDN_CHUNK).astype(F32), fdot(g, _head_sel(0)))
    beta = fdot(_sigmoid(ab), _head_sel(GDN_HEADS))
    return q, k, v, gc, beta


def _unit_lower_inverses(neg_lowers):
    C = neg_lowers[0].shape[0]
    eye = (_iota2((C, C), 0) == _iota2((C, C), 1)).astype(F32)
    invs = [eye + n for n in neg_lowers]
    powers = list(neg_lowers)
    for _ in range(6):
        powers = [idot.nn(p, p) for p in powers]
        invs = [inv + idot.nn(p, inv) for p, inv in zip(powers, invs)]
    return invs


@jax.custom_vjp
def _solve_with_inverse(inv, neg_lower, rhs):
    return idot.nn(inv, rhs)


def _solve_fwd(inv, neg_lower, rhs):
    x = idot.nn(inv, rhs)
    return x, (inv, x)


def _solve_bwd(resid, ct):
    inv, x = resid
    d_rhs = idot.tn(inv, ct)
    return jnp.zeros_like(inv), idot.nt(d_rhs, x), d_rhs


_solve_with_inverse.defvjp(_solve_fwd, _solve_bwd)


def f_gdn_intra(q, k, v, gc, beta, inv=None):
    C = GDN_CHUNK
    causal, strict = _tri(C), _tri(C, strict=True)
    is_last = _iota2((C, HEAD_DIM), 0) == C - 1
    heads = range(GDN_HEADS)
    sls = [slice(h * HEAD_DIM, (h + 1) * HEAD_DIM) for h in heads]
    qs, ks, vs, gs, bs = ([a[:, sl] for sl in sls] for a in (q, k, v, gc, beta))
    decays = [jnp.where(causal, jnp.exp(jnp.where(causal, g - g.T, 0.0)), 0.0) for g in gs]
    kbs = [kh * bh for kh, bh in zip(ks, bs)]
    kts = [kh.T for kh in ks]
    neg_lowers = [jnp.where(strict, -(idot(kb, kt) * d), 0.0) for kb, kt, d in zip(kbs, kts, decays)]
    qks = [jnp.where(causal, idot(qh, kt) * d, 0.0) for qh, kt, d in zip(qs, kts, decays)]
    rhss = [jnp.concatenate([vh * bh, kb * jnp.exp(g)], axis=1) for vh, bh, kb, g in zip(vs, bs, kbs, gs)]
    if inv is None:
        invs = _unit_lower_inverses(neg_lowers)
        sols = [idot.nn(m, r) for m, r in zip(invs, rhss)]
    else:
        sols = [_solve_with_inverse(inv[:, sl], n, r) for sl, n, r in zip(sls, neg_lowers, rhss)]
    g_lasts = [jnp.sum(jnp.where(is_last, g, 0.0), axis=0, keepdims=True) for g in gs]
    outs = [[s[:, :HEAD_DIM] for s in sols], [s[:, HEAD_DIM:] for s in sols], qks,
            [kh * jnp.exp(gl - g) for kh, gl, g in zip(ks, g_lasts, gs)], [qh * jnp.exp(g) for qh, g in zip(qs, gs)]]
    if inv is None:
        outs.append(invs)
    return tuple(jnp.concatenate(o, axis=1) for o in outs)


def f_gdn_post(o, z, gain):
    z = z.astype(F32)
    parts = []
    for h in range(GDN_HEADS):
        oh = o[:, h * HEAD_DIM:(h + 1) * HEAD_DIM]
        parts.append(oh * lax.rsqrt(jnp.mean(oh * oh, axis=-1, keepdims=True) + EPS) * gain)
    return (jnp.concatenate(parts, axis=1) * _silu(z),)


def f_mem_attn(q, k, v):
    q = q.astype(F32)
    lane_head = _iota2((1, MEM_WIDTH), 1) // MEM_HEAD_DIM
    kt = k.astype(F32).T
    masks = [(lane_head == h).astype(F32) for h in range(MEM_HEADS)]
    logits = [bdot(q * mask, kt) * (MEM_HEAD_DIM ** -0.5) for mask in masks]
    ps = [jnp.exp(s - jnp.max(s, axis=-1, keepdims=True)) for s in logits]
    ps = [p / jnp.sum(p, axis=-1, keepdims=True) for p in ps]
    outs = [bdot(p, v) * mask for p, mask in zip(ps, masks)]
    return ((outs[0] + outs[1]) + (outs[2] + outs[3]),)


def f_loss(y, t):
    d = y - t
    return (d * d,)


def conv_fwd(proj, w8, *, width, tile=512):
    n_rows = proj.shape[0]
    tile = min(tile, n_rows)

    def body(x_ref, halo_ref, w_ref, o_ref):
        i = pl.program_id(0)
        halo = jnp.where(i > 0, halo_ref[...].astype(F32), 0.0)
        xs = jnp.concatenate([halo, x_ref[...].astype(F32)], axis=0)
        acc = xs[8:] * w_ref[3:4, :]
        for j in range(CONV_WIDTH - 1):
            acc = acc + pltpu.roll(xs, CONV_WIDTH - 1 - j, 0)[8:] * w_ref[j:j + 1, :]
        o_ref[...] = acc

    return _pcall(body, name="gdn_conv_fwd", out_shape=jax.ShapeDtypeStruct((n_rows, width), F32),
                  grid=(n_rows // tile,),
                  in_specs=[pl.BlockSpec((tile, width), lambda i: (i, 0)),
                            pl.BlockSpec((8, width), lambda i: (jnp.maximum(i * (tile // 8) - 1, 0), 0)),
                            pl.BlockSpec((8, width), lambda i: (0, 0))],
                  out_specs=pl.BlockSpec((tile, width), lambda i: (i, 0)), sem=("parallel",))(proj, proj, w8)


def conv_bwd(proj, w8, dy, *, width, tile=512):
    n_rows = proj.shape[0]
    tile = min(tile, n_rows)
    n = n_rows // tile

    def body(x_ref, xhalo_ref, w_ref, dy_ref, dyhalo_ref, dx_ref, dw_ref):
        i = pl.program_id(0)
        dy = dy_ref[...]
        after = jnp.where(i < n - 1, dyhalo_ref[...], 0.0)
        ds = jnp.concatenate([dy, after], axis=0)
        dx = dy * w_ref[3:4, :]
        for j in range(CONV_WIDTH - 1):
            shift = CONV_WIDTH - 1 - j
            dx = dx + pltpu.roll(ds, tile + 8 - shift, 0)[:tile] * w_ref[j:j + 1, :]
        dx_ref[...] = dx.astype(dx_ref.dtype)
        halo = jnp.where(i > 0, xhalo_ref[...].astype(F32), 0.0)
        xs = jnp.concatenate([halo, x_ref[...].astype(F32)], axis=0)
        rows = [jnp.sum(dy * pltpu.roll(xs, CONV_WIDTH - 1 - j, 0)[8:], axis=0, keepdims=True)
                for j in range(CONV_WIDTH - 1)]
        rows.append(jnp.sum(dy * xs[8:], axis=0, keepdims=True))
        dw = jnp.concatenate(rows + [jnp.zeros((8 - CONV_WIDTH, width), F32)], axis=0)

        @pl.when(i == 0)
        def _():
            dw_ref[...] = dw

        @pl.when(i > 0)
        def _():
            dw_ref[...] += dw

    t8 = tile // 8
    return _pcall(body, name="gdn_conv_bwd",
                  out_shape=[jax.ShapeDtypeStruct((n_rows, width), BF16), jax.ShapeDtypeStruct((8, width), F32)],
                  grid=(n,),
                  in_specs=[pl.BlockSpec((tile, width), lambda i: (i, 0)),
                            pl.BlockSpec((8, width), lambda i: (jnp.maximum(i * t8 - 1, 0), 0)),
                            pl.BlockSpec((8, width), lambda i: (0, 0)),
                            pl.BlockSpec((tile, width), lambda i: (i, 0)),
                            pl.BlockSpec((8, width), lambda i: (jnp.minimum((i + 1) * t8, n * t8 - 1), 0))],
                  out_specs=[pl.BlockSpec((tile, width), lambda i: (i, 0)), pl.BlockSpec((8, width), lambda i: (0, 0))],
                  sem=("arbitrary",))(proj, proj, w8, dy, dy)


def gdn_scan_fwd(u, w, qk, kt, qh, gc):
    n_rows = u.shape[0]
    C, n = GDN_CHUNK, u.shape[0] // GDN_CHUNK

    def body(u_ref, w_ref, qk_ref, kt_ref, qh_ref, gc_ref, o_ref, vn_ref, sin_ref, st):
        @pl.when(pl.program_id(0) == 0)
        def _():
            st[...] = jnp.zeros_like(st)

        sin_ref[0] = st[...]
        sls = [slice(h * HEAD_DIM, (h + 1) * HEAD_DIM) for h in range(GDN_HEADS)]
        states = [st[sl, :] for sl in sls]
        v_news = [u_ref[:, sl] - sdot(w_ref[:, sl], s) for sl, s in zip(sls, states)]
        from_state = [sdot(qh_ref[:, sl], s) for sl, s in zip(sls, states)]
        for sl, a, v_new in zip(sls, from_state, v_news):
            o_ref[:, sl] = a + sdot(qk_ref[:, sl], v_new)
            vn_ref[:, sl] = v_new
        for sl, s, v_new in zip(sls, states, v_news):
            st[sl, :] = s * jnp.exp(gc_ref[C - 1:C, sl]) + sdot.tn(kt_ref[:, sl], v_new)

    blk = pl.BlockSpec((C, GDN_WIDTH), lambda i: (i, 0))
    return _pcall(body, name="gdn_scan_fwd",
                  out_shape=[jax.ShapeDtypeStruct((n_rows, GDN_WIDTH), F32), jax.ShapeDtypeStruct((n_rows, GDN_WIDTH), F32),
                             jax.ShapeDtypeStruct((n, GDN_WIDTH, HEAD_DIM), F32)],
                  grid=(n,), in_specs=[blk] * 6,
                  out_specs=[blk, blk, pl.BlockSpec((1, GDN_WIDTH, HEAD_DIM), lambda i: (i, 0, 0))],
                  scratch=[pltpu.VMEM((GDN_WIDTH, HEAD_DIM), F32)], sem=("arbitrary",))(u, w, qk, kt, qh, gc)


def gdn_scan_bwd(do, w, qk, kt, qh, gc, vn, sin):
    n_rows = do.shape[0]
    C, n = GDN_CHUNK, do.shape[0] // GDN_CHUNK

    def body(do_ref, w_ref, qk_ref, kt_ref, qh_ref, gc_ref, vn_ref, sin_ref,
             du_ref, dw_ref, dqk_ref, dkt_ref, dqh_ref, dgl_ref, dst):
        @pl.when(pl.program_id(0) == 0)
        def _():
            dst[...] = jnp.zeros_like(dst)

        sls = [slice(h * HEAD_DIM, (h + 1) * HEAD_DIM) for h in range(GDN_HEADS)]
        dvns = [sdot.tn(qk_ref[:, sl], do_ref[:, sl]) + sdot(kt_ref[:, sl], dst[sl, :]) for sl in sls]
        for sl, dvn in zip(sls, dvns):
            du_ref[:, sl] = dvn
            dw_ref[:, sl] = -sdot.nt(dvn, sin_ref[0, sl, :])
        for sl in sls:
            dqk_ref[:, sl] = sdot.nt(do_ref[:, sl], vn_ref[:, sl])
            dkt_ref[:, sl] = sdot.nt(vn_ref[:, sl], dst[sl, :])
            dqh_ref[:, sl] = sdot.nt(do_ref[:, sl], sin_ref[0, sl, :])
        for sl, dvn in zip(sls, dvns):
            ds_out = dst[sl, :]
            e = jnp.exp(gc_ref[C - 1:C, sl])
            dgl = jnp.sum(ds_out * sin_ref[0, sl, :], axis=0, keepdims=True) * e
            dgl_ref[:, sl] = jnp.broadcast_to(dgl, (8, HEAD_DIM))
            dst[sl, :] = sdot.tn(qh_ref[:, sl], do_ref[:, sl]) + e * ds_out - sdot.tn(w_ref[:, sl], dvn)

    blk = pl.BlockSpec((C, GDN_WIDTH), lambda i: (n - 1 - i, 0))
    row = jax.ShapeDtypeStruct((n_rows, GDN_WIDTH), F32)
    return _pcall(body, name="gdn_scan_bwd",
                  out_shape=[row] * 5 + [jax.ShapeDtypeStruct((n * 8, GDN_WIDTH), F32)],
                  grid=(n,), in_specs=[blk] * 7 + [pl.BlockSpec((1, GDN_WIDTH, HEAD_DIM), lambda i: (n - 1 - i, 0, 0))],
                  out_specs=[blk] * 5 + [pl.BlockSpec((8, GDN_WIDTH), lambda i: (n - 1 - i, 0))],
                  scratch=[pltpu.VMEM((GDN_WIDTH, HEAD_DIM), F32)], sem=("arbitrary",))(do, w, qk, kt, qh, gc, vn, sin)


def gdn_intra_bwd(q, k, v, gc, beta, inv, cts, dgl):
    n_rows = q.shape[0]
    C = GDN_CHUNK

    def body(*refs):
        ins = [r[...] for r in refs[:5]]
        inv_v = refs[5][...]
        ct = tuple(r[...] for r in refs[6:11])
        dgl_v = refs[11][...]
        _, vjp = jax.vjp(lambda *a: f_gdn_intra(*a, inv=inv_v), *ins)
        grads = list(vjp(ct))
        last = _iota2((C, GDN_WIDTH), 0) == C - 1
        grads[3] = grads[3] + jnp.where(last, jnp.broadcast_to(dgl_v[0:1, :], (C, GDN_WIDTH)), 0.0)
        for o_ref, g in zip(refs[12:], grads):
            o_ref[...] = g

    blk = pl.BlockSpec((C, GDN_WIDTH), lambda i: (i, 0))
    return _pcall(body, name="gdn_intra_bwd", out_shape=[jax.ShapeDtypeStruct((n_rows, GDN_WIDTH), F32)] * 5,
                  grid=(n_rows // C,), in_specs=[blk] * 11 + [pl.BlockSpec((8, GDN_WIDTH), lambda i: (i, 0))],
                  out_specs=[blk] * 5, sem=("parallel",))(q, k, v, gc, beta, inv, *cts, dgl)


def fox_gate_fwd(f, b_f):
    n_rows = f.shape[0]
    T = LANES

    def body(f_ref, b_ref, cb_ref, carry):
        @pl.when(pl.program_id(0) == 0)
        def _():
            carry[...] = jnp.zeros_like(carry)

        c = fdot(_tri(T).astype(F32), _log_sigmoid(f_ref[...] + b_ref[...])) + carry[...]
        carry[...] = c[T - 1:T, :]
        cb_ref[...] = fdot(c, _head_sel(0))

    return _pcall(body, name="fox_gate_fwd", out_shape=jax.ShapeDtypeStruct((n_rows, FOX_WIDTH), F32),
                  grid=(n_rows // T,),
                  in_specs=[pl.BlockSpec((T, LANES), lambda i: (i, 0)), pl.BlockSpec((1, LANES), lambda i: (0, 0))],
                  out_specs=pl.BlockSpec((T, FOX_WIDTH), lambda i: (i, 0)),
                  scratch=[pltpu.VMEM((1, LANES), F32)], sem=("arbitrary",))(f, b_f)


def fox_gate_bwd(f, b_f, dcrow, dcb):
    n_rows = f.shape[0]
    T = LANES
    n = n_rows // T

    def body(f_ref, b_ref, dc_ref, dcb_ref, df_ref, db_ref, carry):
        i = pl.program_id(0)

        @pl.when(i == 0)
        def _():
            carry[...] = jnp.zeros_like(carry)

        rows = [dc_ref[h] for h in range(FOX_HEADS)] + [jnp.zeros((T - FOX_HEADS, T), F32)]
        first_lane = (_iota2((FOX_WIDTH, LANES), 0) == _iota2((FOX_WIDTH, LANES), 1) * HEAD_DIM).astype(F32)
        dc = jnp.concatenate(rows, axis=0).T + fdot(dcb_ref[...], first_lane)
        dlog = fdot.tn(_tri(T).astype(F32), dc) + carry[...]
        carry[...] = dlog[0:1, :]
        df = dlog * (1.0 - _sigmoid(f_ref[...] + b_ref[...]))
        df_ref[...] = df
        db = jnp.sum(df, axis=0, keepdims=True)

        @pl.when(i == 0)
        def _():
            db_ref[...] = db

        @pl.when(i > 0)
        def _():
            db_ref[...] += db

    return _pcall(body, name="fox_gate_bwd",
                  out_shape=[jax.ShapeDtypeStruct((n_rows, LANES), F32), jax.ShapeDtypeStruct((1, LANES), F32)],
                  grid=(n,),
                  in_specs=[pl.BlockSpec((T, LANES), lambda i: (n - 1 - i, 0)), pl.BlockSpec((1, LANES), lambda i: (0, 0)),
                            pl.BlockSpec((FOX_HEADS, 1, T), lambda i: (0, 0, n - 1 - i)),
                            pl.BlockSpec((T, FOX_WIDTH), lambda i: (n - 1 - i, 0))],
                  out_specs=[pl.BlockSpec((T, LANES), lambda i: (n - 1 - i, 0)), pl.BlockSpec((1, LANES), lambda i: (0, 0))],
                  scratch=[pltpu.VMEM((1, LANES), F32)], sem=("arbitrary",))(f, b_f, dcrow, dcb)


FOX_AUG = 2 * HEAD_DIM


def _fox_tiles(n_rows):
    return min(1024, n_rows), min(512, n_rows)


def _fox_pairs(n_rows, query_major):
    tq, tk = _fox_tiles(n_rows)
    nq, r = n_rows // tq, tq // tk
    if query_major:
        pairs = [(i, j) for i in range(nq) for j in range(r * (i + 1))]
    else:
        pairs = [(i, j) for j in range(nq * r) for i in range(j // r, nq)]
    return jnp.asarray([p[0] for p in pairs], jnp.int32), jnp.asarray([p[1] for p in pairs], jnp.int32)


def fox_augment(x, cb, query_side):
    def fn(xt, ct):
        xt = xt.astype(F32)
        lane = _iota2((xt.shape[0], HEAD_DIM), 1)
        parts = []
        for h in range(FOX_HEADS):
            sl = slice(h * HEAD_DIM, (h + 1) * HEAD_DIM)
            c = ct[:, sl]
            hi = c.astype(BF16).astype(F32)
            mid = (c - hi).astype(BF16).astype(F32)
            lo = (c - hi - mid).astype(BF16).astype(F32)
            terms = jnp.where(lane % 3 == 0, hi, jnp.where(lane % 3 == 1, mid, lo))
            if query_side:
                extra = jnp.where(lane < 3, terms, jnp.where(lane < 6, 1.0, 0.0))
                parts += [xt[:, sl] * (HEAD_DIM ** -0.5), extra]
            else:
                extra = jnp.where(lane < 3, 1.0, jnp.where(lane < 6, -terms, 0.0))
                parts += [xt[:, sl], extra]
        return (jnp.concatenate(parts, axis=1),)

    return rowwise(fn, [(x, FOX_WIDTH, 0), cb], [], [(FOX_HEADS * FOX_AUG, BF16)], tile=ROW_TILE,
                   name="fox_augment_q" if query_side else "fox_augment_k")[0]


def _pcall_tables(body, *, name, out_shape, grid, tables, in_specs, out_specs, scratch, sem):
    spec = pltpu.PrefetchScalarGridSpec(num_scalar_prefetch=len(tables), grid=grid, in_specs=in_specs, out_specs=out_specs,
                                        scratch_shapes=list(scratch))
    return pl.pallas_call(body, name=name, out_shape=out_shape, grid_spec=spec,
                          compiler_params=pltpu.CompilerParams(vmem_limit_bytes=VMEM_LIMIT_BYTES, dimension_semantics=sem))


def _fox_logits(qa, ka, offset):
    s = bdot.nt(qa, ka)
    if offset is not None:
        s = jnp.where(_iota2(s.shape, 0) + offset >= _iota2(s.shape, 1), s, NEG_INF)
    return s


def _fox_p_ds(offset, qa_ref, ka_ref, v_ref, o_ref, lse_ref, do_ref):
    s = _fox_logits(qa_ref[...], ka_ref[...], offset)
    p = jnp.exp(s - jnp.tile(lse_ref[...], (1, s.shape[1] // LANES)))
    d_o = do_ref[...].astype(F32)
    delta = jnp.sum(d_o * o_ref[...].astype(F32), axis=-1, keepdims=True)
    return p, p * (bdot.nt(d_o, v_ref[...]) - delta), d_o


def _fox_on_diagonal(i, j, r, tk, step):
    @pl.when(j < r * i)
    def _():
        step(None)

    for m in range(r):
        @pl.when(j == r * i + m)
        def _(m=m):
            step(-m * tk)


def _fox_specs(tq, tk, do_col):
    qaspec = pl.BlockSpec((tq, FOX_AUG), lambda h, p, it, jt: (it[p], h))
    qspec = pl.BlockSpec((tq, HEAD_DIM), lambda h, p, it, jt: (it[p], h))
    dospec = pl.BlockSpec((tq, HEAD_DIM), lambda h, p, it, jt: (it[p], do_col + h))
    kaspec = pl.BlockSpec((tk, FOX_AUG), lambda h, p, it, jt: (jt[p], h))
    kspec = pl.BlockSpec((tk, HEAD_DIM), lambda h, p, it, jt: (jt[p], h))
    vspec = pl.BlockSpec((tk, HEAD_DIM), lambda h, p, it, jt: (jt[p], FOX_HEADS + h))
    cspec = pl.BlockSpec((1, 1, tk), lambda h, p, it, jt: (h, 0, jt[p]))
    return qaspec, qspec, dospec, kaspec, kspec, vspec, cspec


def fox_fwd(qa, kv, ka):
    n_rows = kv.shape[0]
    tq, tk = _fox_tiles(n_rows)
    r = tq // tk
    tables = _fox_pairs(n_rows, True)

    def body(it, jt, qa_ref, ka_ref, v_ref, o_ref, lse_ref, m_sc, l_sc, acc):
        i, j = it[pl.program_id(1)], jt[pl.program_id(1)]

        @pl.when(j == 0)
        def _():
            m_sc[...] = jnp.full(m_sc.shape, NEG_INF, F32)
            l_sc[...] = jnp.zeros_like(l_sc)
            acc[...] = jnp.zeros_like(acc)

        def step(offset):
            s = _fox_logits(qa_ref[...], ka_ref[...], offset)
            m_old = m_sc[...]
            m_new = jnp.maximum(m_old, jnp.max(s, axis=-1, keepdims=True))
            alpha = jnp.exp(m_old - m_new)
            p = jnp.exp(s - jnp.tile(m_new, (1, tk // LANES)))
            l_sc[...] = l_sc[...] * alpha + jnp.sum(p, axis=-1, keepdims=True)
            acc[...] = acc[...] * alpha + bdot(p, v_ref[...])
            m_sc[...] = m_new

        _fox_on_diagonal(i, j, r, tk, step)

        @pl.when(j == r * i + r - 1)
        def _():
            o_ref[...] = (acc[...] / l_sc[...]).astype(o_ref.dtype)
            lse_ref[...] = m_sc[...] + jnp.log(l_sc[...])

    qaspec, qspec, _, kaspec, _, vspec, _ = _fox_specs(tq, tk, 0)
    return _pcall_tables(body, name="fox_fwd",
                         out_shape=[jax.ShapeDtypeStruct((n_rows, FOX_WIDTH), BF16), jax.ShapeDtypeStruct((n_rows, FOX_WIDTH), F32)],
                         grid=(FOX_HEADS, tables[0].shape[0]), tables=tables,
                         in_specs=[qaspec, kaspec, vspec], out_specs=[qspec, qspec],
                         scratch=[pltpu.VMEM((tq, HEAD_DIM), F32)] * 3, sem=("parallel", "arbitrary"))(*tables, qa, ka, kv)


def fox_bwd_dq(qa, kv, ka, o, lse, do, prev=None):
    do, _, do_col = do
    do_col *= FOX_HEADS
    n_rows = kv.shape[0]
    tq, tk = _fox_tiles(n_rows)
    r = tq // tk
    n_prev = 0 if prev is None else 1
    tables = _fox_pairs(n_rows, True)

    def body(it, jt, qa_ref, ka_ref, k_ref, v_ref, o_ref, lse_ref, do_ref, *rest):
        dq_ref, drow_ref, acc, rows = rest[n_prev:]
        i, j = it[pl.program_id(1)], jt[pl.program_id(1)]

        @pl.when(j == 0)
        def _():
            acc[...] = jnp.zeros_like(acc)
            rows[...] = jnp.zeros_like(rows)

        def step(offset):
            _, ds, _ = _fox_p_ds(offset, qa_ref, ka_ref, v_ref, o_ref, lse_ref, do_ref)
            acc[...] += bdot(ds, k_ref[...])
            rows[...] += jnp.sum(ds, axis=-1, keepdims=True)

        _fox_on_diagonal(i, j, r, tk, step)

        @pl.when(j == r * i + r - 1)
        def _():
            dq_ref[...] = (acc[...] * (HEAD_DIM ** -0.5)).astype(dq_ref.dtype)
            drow_ref[...] = rows[...] + rest[0][...] if n_prev else rows[...]

    qaspec, qspec, dospec, kaspec, kspec, vspec, _ = _fox_specs(tq, tk, do_col)
    return _pcall_tables(body, name="fox_bwd_dq" + ("_acc" if n_prev else ""),
                         out_shape=[jax.ShapeDtypeStruct((n_rows, FOX_WIDTH), BF16), jax.ShapeDtypeStruct((n_rows, FOX_WIDTH), F32)],
                         grid=(FOX_HEADS, tables[0].shape[0]), tables=tables,
                         in_specs=[qaspec, kaspec, kspec, vspec, qspec, qspec, dospec] + [qspec] * n_prev,
                         out_specs=[qspec, qspec], scratch=[pltpu.VMEM((tq, HEAD_DIM), F32)] * 2,
                         sem=("parallel", "arbitrary"))(*tables, qa, ka, kv, kv, o, lse, do, *([prev] if n_prev else []))


def fox_bwd_dkv(qa, kv, ka, o, lse, do, prev=None):
    do, _, do_col = do
    do_col *= FOX_HEADS
    n_rows = kv.shape[0]
    tq, tk = _fox_tiles(n_rows)
    nq, r = n_rows // tq, tq // tk
    n_prev = 0 if prev is None else 3
    tables = _fox_pairs(n_rows, False)

    def body(it, jt, qa_ref, ka_ref, v_ref, o_ref, lse_ref, do_ref, *rest):
        prev_refs = rest[:n_prev]
        dk_ref, dv_ref, dc_ref, dk_acc, dv_acc, dc_acc = rest[n_prev:]
        i, j = it[pl.program_id(1)], jt[pl.program_id(1)]

        @pl.when(j >= r * i)
        def _():
            dk_acc[...] = jnp.zeros_like(dk_acc)
            dv_acc[...] = jnp.zeros_like(dv_acc)
            dc_acc[...] = jnp.zeros_like(dc_acc)

        def step(offset):
            p, ds, d_o = _fox_p_ds(offset, qa_ref, ka_ref, v_ref, o_ref, lse_ref, do_ref)
            dv_acc[...] += bdot.tn(p, d_o)
            dk_acc[...] += bdot.tn(ds, qa_ref[:, :HEAD_DIM])
            dc_acc[...] -= jnp.sum(ds, axis=0, keepdims=True)

        _fox_on_diagonal(i, j, r, tk, step)

        @pl.when(i == nq - 1)
        def _():
            dk, dv, dc = dk_acc[...], dv_acc[...], dc_acc[...]
            if n_prev:
                dk, dv, dc = dk + prev_refs[0][...], dv + prev_refs[1][...], dc + prev_refs[2][0]
            dk_ref[...] = dk
            dv_ref[...] = dv
            dc_ref[0] = dc

    qaspec, qspec, dospec, kaspec, kspec, vspec, cspec = _fox_specs(tq, tk, do_col)
    return _pcall_tables(body, name="fox_bwd_dkv" + ("_acc" if n_prev else ""),
                         out_shape=[jax.ShapeDtypeStruct((n_rows, FOX_WIDTH), F32), jax.ShapeDtypeStruct((n_rows, FOX_WIDTH), F32),
                                    jax.ShapeDtypeStruct((FOX_HEADS, 1, n_rows), F32)],
                         grid=(FOX_HEADS, tables[0].shape[0]), tables=tables,
                         in_specs=[qaspec, kaspec, vspec, qspec, qspec, dospec] + [kspec, kspec, cspec][:n_prev],
                         out_specs=[kspec, kspec, cspec],
                         scratch=[pltpu.VMEM((tk, HEAD_DIM), F32), pltpu.VMEM((tk, HEAD_DIM), F32), pltpu.VMEM((1, tk), F32)],
                         sem=("parallel", "arbitrary"))(*tables, qa, ka, kv, o, lse, do, *(prev or ()))


def loss_head(h, gain, target, *, tile=512):
    n_rows, d = h.shape
    tile = min(tile, n_rows)

    def body(h_ref, g_ref, t_ref, part_ref, dy_ref):
        (y,) = f_rmsnorm(h_ref[...], g_ref[...])
        diff = y - t_ref[...]
        dy_ref[...] = diff * (1.0 / d)
        part = jnp.sum(diff * diff, axis=0, keepdims=True)
        first = pl.program_id(0) == 0

        @pl.when(first)
        def _():
            part_ref[...] = part

        @pl.when(jnp.logical_not(first))
        def _():
            part_ref[...] += part

    blk = pl.BlockSpec((tile, d), lambda i: (i, 0))
    one = pl.BlockSpec((1, d), lambda i: (0, 0))
    return _pcall(body, name="loss_head",
                  out_shape=[jax.ShapeDtypeStruct((1, d), F32), jax.ShapeDtypeStruct((n_rows, d), F32)],
                  grid=(n_rows // tile,), in_specs=[blk, one, blk], out_specs=[one, blk], sem=("arbitrary",))(h, gain, target)


def adamw(w, g, m, v, *, name):
    shape = w.shape
    cols = shape[-1] if w.ndim >= 2 else w.size
    rows = w.size // cols
    tile = _pick(rows, (256, 128, 64, 32, 16, 8))
    as2d = lambda a: a.reshape(rows, cols)

    def body(w_ref, g_ref, m_ref, v_ref, d_ref, nm_ref, nv_ref):
        g_ = g_ref[...]
        m_ = ADAM_B1 * m_ref[...] + (1.0 - ADAM_B1) * g_
        v_ = ADAM_B2 * v_ref[...] + (1.0 - ADAM_B2) * (g_ * g_)
        m_hat = m_ / (1.0 - ADAM_B1 ** ADAM_STEP)
        v_hat = v_ / (1.0 - ADAM_B2 ** ADAM_STEP)
        d_ref[...] = -ADAM_LR * (m_hat / (jnp.sqrt(v_hat) + ADAM_EPS) + ADAM_WD * w_ref[...])
        nm_ref[...] = m_
        nv_ref[...] = v_

    blk = pl.BlockSpec((tile, cols), lambda i: (i, 0))
    outs = _pcall(body, name=name, out_shape=[jax.ShapeDtypeStruct((rows, cols), F32)] * 3, grid=(rows // tile,),
                  in_specs=[blk] * 4, out_specs=[blk] * 3, sem=("parallel",))(as2d(w), as2d(g), as2d(m), as2d(v))
    return tuple(o.reshape(shape) for o in outs)


def sum_leading(a, *, name):
    p, r, c = a.shape
    tile = _pick(r, (256, 128, 64, 32, 16, 8))

    def body(a_ref, o_ref):
        total = a_ref[0].astype(F32)
        for k in range(1, p):
            total = total + a_ref[k].astype(F32)
        o_ref[...] = total

    return _pcall(body, name=name, out_shape=jax.ShapeDtypeStruct((r, c), F32), grid=(r // tile,),
                  in_specs=[pl.BlockSpec((p, tile, c), lambda i: (0, i, 0))],
                  out_specs=pl.BlockSpec((tile, c), lambda i: (i, 0)), sem=("parallel",))(a)


_HBM = pl.BlockSpec(memory_space=pltpu.HBM)


def _comm_call(body, *, name, out_shape, n_in, scratch):
    return pl.pallas_call(body, name=name, out_shape=out_shape, in_specs=[_HBM] * n_in, out_specs=_HBM,
                          scratch_shapes=scratch,
                          compiler_params=pltpu.CompilerParams(has_side_effects=True))


def all_gather8(a, *, name):
    m_per, n = a.shape

    def body(x_ref, out_ref, send_sems, recv_sems, local_sem):
        x, y, c = lax.axis_index("x"), lax.axis_index("y"), lax.axis_index("c")
        me, sibling = (x, y, c), (x, y, 1 - c)
        chips = [(1 - x, y), (x, 1 - y), (1 - x, 1 - y)]

        def rows(px, py, pc):
            return out_ref.at[pl.ds((4 * px + 2 * py + pc) * m_per, m_per), :]

        def copy(k, block, to, src=None):
            return pltpu.make_async_remote_copy(
                src_ref=rows(*block) if src is None else src, dst_ref=rows(*block),
                send_sem=send_sems.at[k], recv_sem=recv_sems.at[k], device_id=to, device_id_type=MESH)

        mine = pltpu.make_async_copy(x_ref, rows(*me), local_sem)
        mine.start()
        first = [copy(0, me, sibling, src=x_ref)]
        first += [copy(1 + j, me, (*chip, c), src=x_ref) for j, chip in enumerate(chips)]
        for cp in first:
            cp.start()
        passed = [copy(4 + j, (*chip, c), sibling) for j, chip in enumerate(chips)]
        for j, chip in enumerate(chips):
            copy(1 + j, (*chip, c), me).wait_recv()
            passed[j].start()
        copy(0, sibling, me).wait_recv()
        for j, chip in enumerate(chips):
            copy(4 + j, (*chip, 1 - c), me).wait_recv()
        for cp in first + passed:
            cp.wait_send()
        mine.wait()

    return _comm_call(body, name=name, out_shape=jax.ShapeDtypeStruct((N_DEV * m_per, n), a.dtype), n_in=1,
                      scratch=[pltpu.SemaphoreType.DMA((7,)), pltpu.SemaphoreType.DMA((7,)), pltpu.SemaphoreType.DMA])(a)


PACK_COLS = 1024
PACK_ROW_MULTIPLE = 32

WEIGHT_NAMES = ["ffn1_norm", "ffn1_w_gate_up", "ffn1_w_down", "mix_norm", "ffn2_norm", "ffn2_w_gate_up", "ffn2_w_down",
                "gdn_w_in", "gdn_conv", "gdn_A_log", "gdn_dt_bias", "gdn_out_norm", "fox_w_in", "w_out", "mem_norm",
                "mem_w_kv", "kv_norm", "kv_w", "kv_b_f", "final_norm"]
SHARDED = [("ffn1_w_gate_up", 2), ("ffn1_w_down", 1), ("ffn2_w_gate_up", 2), ("ffn2_w_down", 1), ("gdn_w_in", 2),
           ("gdn_conv", 2), ("fox_w_in", 1), ("w_out", 1), ("mem_w_kv", 1), ("kv_w", 0)]
REPLICATED = [n for n in WEIGHT_NAMES if n not in dict(SHARDED)]


PACK_PIECE_ROWS = 16


def _rows_of(size):
    return -(-size // (PACK_COLS * PACK_PIECE_ROWS)) * PACK_PIECE_ROWS


def pack(pieces, dtype, row_multiple=PACK_ROW_MULTIPLE):
    bufs, total = [], 0
    for p in pieces:
        flat = p.astype(dtype).reshape(-1)
        rows = _rows_of(flat.size)
        bufs.append(jnp.pad(flat, (0, rows * PACK_COLS - flat.size)).reshape(rows, PACK_COLS))
        total += rows
    pad = -total % row_multiple
    if pad:
        bufs.append(jnp.zeros((pad, PACK_COLS), dtype))
    return jnp.concatenate(bufs, axis=0)


def unpack(buf, shapes):
    out, row = [], 0
    for shape in shapes:
        size = 1
        for s in shape:
            size *= s
        rows = _rows_of(size)
        out.append(buf[row:row + rows].reshape(-1)[:size].reshape(shape))
        row += rows
    return out


def _row(vec, width=None):
    vec = vec.astype(F32).reshape(1, -1)
    if width is not None and vec.shape[1] < width:
        vec = jnp.pad(vec, ((0, 0), (0, width - vec.shape[1])))
    return vec


ROW_TILE = 512
GDN_PROJ_WIDTH = 4 * GDN_WIDTH + MEM_WIDTH + LANES
GDN_Z_COL, GDN_QMEM_COL, GDN_AB_COL = 3, 4 * GDN_WIDTH // MEM_WIDTH, (4 * GDN_WIDTH + MEM_WIDTH) // LANES
FOX_QMEM_COL = FOX_WIDTH // MEM_WIDTH
KV_PAD_WIDTH = 2 * FOX_WIDTH + LANES


def rms_fwd(x, gain_row, out_dtype=BF16):
    return rowwise(f_rmsnorm, [x], [gain_row], [(x.shape[1], out_dtype)], tile=ROW_TILE, name="rms_fwd")[0]


def rms_bwd(x, gain_row, dy, dres=None):
    return rowwise_bwd(f_rmsnorm, [x], [gain_row], [dy], tile=ROW_TILE, name="rms_bwd", row_grads=[F32],
                       const_grads=[True], add=None if dres is None else {0: dres})


def _ffn_tiles(n_rows):
    return _pick(n_rows, (512, 256, 128)), _pick(FFN_HIDDEN, (1408, 256, 128))


def ffn_up_act(n, wgu):
    n_rows, d = n.shape
    tm, tn = _ffn_tiles(n_rows)
    nj = FFN_HIDDEN // tn

    def body(n_ref, wg_ref, wu_ref, gu_ref, act_ref):
        x = n_ref[...].astype(BF16)
        g = bdot.nn(x, wg_ref[...])
        u = bdot.nn(x, wu_ref[...])
        gu_ref[0] = g.astype(gu_ref.dtype)
        gu_ref[1] = u.astype(gu_ref.dtype)
        act_ref[...] = (_silu(g) * u).astype(act_ref.dtype)

    return _pcall(body, name="ffn_up_act",
                  out_shape=[jax.ShapeDtypeStruct((2, n_rows, FFN_HIDDEN), BF16), jax.ShapeDtypeStruct((n_rows, FFN_HIDDEN), BF16)],
                  grid=(nj, n_rows // tm),
                  in_specs=[pl.BlockSpec((tm, d), lambda j, i: (i, 0)), pl.BlockSpec((d, tn), lambda j, i: (0, j)),
                            pl.BlockSpec((d, tn), lambda j, i: (0, nj + j))],
                  out_specs=[pl.BlockSpec((2, tm, tn), lambda j, i: (0, i, j)), pl.BlockSpec((tm, tn), lambda j, i: (i, j))],
                  sem=("parallel", "parallel"))(n, wgu, wgu)


def ffn_down_dx_act(dh, wd, gu):
    n_rows, d = dh.shape
    tm, tn = _ffn_tiles(n_rows)

    def body(dh_ref, wd_ref, gu_ref, dgu_ref):
        dact = 0.5 * bdot.nt(dh_ref[...], wd_ref[...])
        gate, up = gu_ref[0].astype(F32), gu_ref[1].astype(F32)
        sg = _sigmoid(gate)
        dgu_ref[0] = (dact * up * (sg * (1.0 + gate * (1.0 - sg)))).astype(dgu_ref.dtype)
        dgu_ref[1] = (dact * (gate * sg)).astype(dgu_ref.dtype)

    blk = pl.BlockSpec((2, tm, tn), lambda j, i: (0, i, j))
    return _pcall(body, name="ffn_down_dx_act", out_shape=jax.ShapeDtypeStruct((2, n_rows, FFN_HIDDEN), BF16),
                  grid=(FFN_HIDDEN // tn, n_rows // tm),
                  in_specs=[pl.BlockSpec((tm, d), lambda j, i: (i, 0)), pl.BlockSpec((tn, d), lambda j, i: (j, 0)), blk],
                  out_specs=blk, sem=("parallel", "parallel"))(dh, wd, gu)


def ffn_fwd(h, gain_row, wgu, wd):
    n = rms_fwd(h, gain_row)
    gu, act = ffn_up_act(n, wgu)
    return mm(act, wd, scale=0.5, res=h, name="ffn_down"), (h, n, gu, act)


def ffn_bwd(dh, saved, gain_row, wgu, wd):
    h, n, gu, act = saved
    dgu = ffn_down_dx_act(dh, wd, gu)
    dwd = mm(act, dh, ta=True, scale=0.5, name="ffn_down_dw")
    dwgu = mm(n, dgu, ta=True, b_split=True, name="ffn_up_dw")
    dn = mm(dgu, wgu, tb=True, a_split=True, out_dtype=BF16, name="ffn_up_dx")
    dh, dgain = rms_bwd(h, gain_row, dn, dh)
    return dh, dwgu, dwd, dgain


def gdn_fwd(proj, w8, a_row, dt_row, onorm_row):
    wide = [(GDN_WIDTH, F32)] * 5
    xc = conv_fwd(proj, w8, width=3 * GDN_WIDTH)
    q, k, v, gc, beta = rowwise(f_gdn_pre, [xc, (proj, LANES, GDN_AB_COL)], [a_row, dt_row], wide, tile=GDN_CHUNK,
                                name="gdn_pre_fwd")
    u, w, qk, kt, qh, inv = rowwise(f_gdn_intra, [q, k, v, gc, beta], [], wide + wide[:1], tile=GDN_CHUNK,
                                    name="gdn_intra_fwd")
    o, vn, sin = gdn_scan_fwd(u, w, qk, kt, qh, gc)
    main = rowwise(f_gdn_post, [o, (proj, GDN_WIDTH, GDN_Z_COL)], [onorm_row], [(GDN_WIDTH, BF16)], tile=ROW_TILE,
                   name="gdn_post_fwd")[0]
    return main, (xc, q, k, v, gc, beta, inv, w, qk, kt, qh, vn, sin, o)


def gdn_bwd(dmain, proj, saved, w8, a_row, dt_row, onorm_row):
    xc, q, k, v, gc, beta, inv, w, qk, kt, qh, vn, sin, o = saved
    do, dz, donorm = rowwise_bwd(f_gdn_post, [o, (proj, GDN_WIDTH, GDN_Z_COL)], [onorm_row], [dmain], tile=ROW_TILE,
                                 name="gdn_post_bwd", row_grads=[F32, BF16], const_grads=[True])
    du, dw, dqk, dkt, dqh, dgl = gdn_scan_bwd(do, w, qk, kt, qh, gc, vn, sin)
    dq, dk, dv, dgc, dbeta = gdn_intra_bwd(q, k, v, gc, beta, inv, (du, dw, dqk, dkt, dqh), dgl)
    dxc, dab, da, ddt = rowwise_bwd(f_gdn_pre, [xc, (proj, LANES, GDN_AB_COL)], [a_row, dt_row], [dq, dk, dv, dgc, dbeta],
                                    tile=GDN_CHUNK, name="gdn_pre_bwd", row_grads=[F32, BF16], const_grads=[True, True])
    dqkv, dw8 = conv_bwd(proj, w8, dxc, width=3 * GDN_WIDTH)
    return dqkv, dz, dab, dw8, da, ddt, donorm


def mem_fwd(q, kmem, vmem):
    return rowwise(f_mem_attn, [q], [kmem, vmem], [(MEM_WIDTH, BF16)], tile=ROW_TILE, name="mem_attn_fwd")[0]


def mem_bwd(q, kmem, vmem, dout):
    return rowwise_bwd(f_mem_attn, [q], [kmem, vmem], [dout], tile=ROW_TILE, name="mem_attn_bwd", row_grads=[BF16],
                       const_grads=[True, True])


def forward_backward(xs, mems, target, P):
    depth, n_a = 4, 2
    G = {}
    mem_gain = _row(P["mem_norm"])
    mem_n = rms_fwd(mems, mem_gain)
    h = xs
    saved = []
    shared = None
    for l in range(depth):
        h0 = h
        h1, s1 = ffn_fwd(h0, _row(P["ffn1_norm"][l]), P["ffn1_w_gate_up"][l], P["ffn1_w_down"][l])
        u = rms_fwd(h1, _row(P["mix_norm"][l]))
        kvm = mm(mem_n, P["mem_w_kv"][l], name="mem_kv")
        kmem, vmem = kvm[:, :MEM_WIDTH], kvm[:, MEM_WIDTH:]
        if l < n_a:
            gp = (P["conv8"][l], _row(P["gdn_A_log"][l], LANES), _row(P["gdn_dt_bias"][l], LANES), _row(P["gdn_out_norm"][l]))
            proj = mm(u, P["gdn_w_in_pad"][l], name="gdn_in")
            main, sm = gdn_fwd(proj, *gp)
            qm = (proj, MEM_WIDTH, GDN_QMEM_COL)
        else:
            proj = mm(u, P["fox_w_in"][l - n_a], out_dtype=BF16, name="fox_in")
            kv, ka, cb = shared
            qa = fox_augment(proj, cb, True)
            main, lse = fox_fwd(qa, kv, ka)
            sm = (main, lse, qa)
            qm = (proj, MEM_WIDTH, FOX_QMEM_COL)
        mo = mem_fwd(qm, kmem, vmem)
        cat = jnp.concatenate([main, mo], axis=1)
        h2 = mm(cat, P["w_out"][l], res=h1, name="mix_out")
        h3, s2 = ffn_fwd(h2, _row(P["ffn2_norm"][l]), P["ffn2_w_gate_up"][l], P["ffn2_w_down"][l])
        saved.append((s1, h1, u, kmem, vmem, proj, sm, qm, cat, s2))
        h = h3
        if l == n_a - 1:
            nkv = rms_fwd(h, _row(P["kv_norm"]))
            kv = mm(nkv, P["kv_w_pad"][:, :2 * FOX_WIDTH], out_dtype=BF16, name="fox_kv")
            f = mm(nkv, P["kv_w_pad"][:, 2 * FOX_WIDTH:], name="fox_f")
            bf_row = _row(P["kv_b_f"], LANES)
            cb = fox_gate_fwd(f, bf_row)
            shared = (kv, fox_augment(kv, cb, False), cb)
            kv_saved = (h, nkv, f, bf_row)

    part, dy = loss_head(h, _row(P["final_norm"]), target)
    dh, G["final_norm"] = rms_bwd(h, _row(P["final_norm"]), dy)

    per_layer = {n: [None] * depth for n in ("ffn1_norm", "ffn1_w_gate_up", "ffn1_w_down", "mix_norm", "ffn2_norm",
                                             "ffn2_w_gate_up", "ffn2_w_down", "w_out", "mem_w_kv")}
    gdn_g = {n: [None] * n_a for n in ("gdn_w_in_pad", "conv8", "gdn_A_log", "gdn_dt_bias", "gdn_out_norm")}
    fox_g = [None] * (depth - n_a)
    dmem_n = None
    dkv_acc = dcb_acc = None
    for l in reversed(range(depth)):
        s1, h1, u, kmem, vmem, proj, sm, qm, cat, s2 = saved[l]
        if l == n_a - 1:
            hk, nkv, f, bf_row = kv_saved
            dk, dv, dcrow = dkv_acc
            df, dbf = fox_gate_bwd(f, bf_row, dcrow, dcb_acc)
            dp = jnp.concatenate([dk.astype(BF16), dv.astype(BF16), df.astype(BF16)], axis=1)
            G["kv_w_pad"] = mm(nkv, dp, ta=True, name="fox_kv_dw")
            G["kv_b_f"] = dbf
            dnkv = mm(dp, P["kv_w_pad"], tb=True, out_dtype=BF16, name="fox_kv_dx")
            dh, G["kv_norm"] = rms_bwd(hk, _row(P["kv_norm"]), dnkv, dh)
        dh, per_layer["ffn2_w_gate_up"][l], per_layer["ffn2_w_down"][l], per_layer["ffn2_norm"][l] = ffn_bwd(
            dh, s2, _row(P["ffn2_norm"][l]), P["ffn2_w_gate_up"][l], P["ffn2_w_down"][l])
        dcat = mm(dh, P["w_out"][l], tb=True, out_dtype=BF16, name="mix_out_dx")
        per_layer["w_out"][l] = mm(cat, dh, ta=True, name="mix_out_dw")
        dqm, dkm, dvm = mem_bwd(qm, kmem, vmem, (dcat, MEM_WIDTH, FOX_QMEM_COL))
        dkvm = jnp.concatenate([dkm, dvm], axis=1)
        per_layer["mem_w_kv"][l] = mm(mem_n, dkvm, ta=True, name="mem_kv_dw")
        dmem_n = mm(dkvm, P["mem_w_kv"][l], tb=True, res=dmem_n, name="mem_kv_dx")
        dmain = (dcat, GDN_WIDTH, 0)
        if l < n_a:
            gp = (P["conv8"][l], _row(P["gdn_A_log"][l], LANES), _row(P["gdn_dt_bias"][l], LANES), _row(P["gdn_out_norm"][l]))
            dqkv, dz, dab, gdn_g["conv8"][l], gdn_g["gdn_A_log"][l], gdn_g["gdn_dt_bias"][l], gdn_g["gdn_out_norm"][l] = gdn_bwd(
                dmain, proj, sm, *gp)
            dproj = jnp.concatenate([dqkv, dz, dqm, dab], axis=1)
            gdn_g["gdn_w_in_pad"][l] = mm(u, dproj, ta=True, name="gdn_in_dw")
            du = mm(dproj, P["gdn_w_in_pad"][l], tb=True, out_dtype=BF16, name="gdn_in_dx")
        else:
            o, lse, qa = sm
            kv, ka, _ = shared
            dq, dcb_acc = fox_bwd_dq(qa, kv, ka, o, lse, dmain, dcb_acc)
            dkv_acc = fox_bwd_dkv(qa, kv, ka, o, lse, dmain, dkv_acc)
            dproj = jnp.concatenate([dq, dqm], axis=1)
            fox_g[l - n_a] = mm(u, dproj, ta=True, name="fox_in_dw")
            du = mm(dproj, P["fox_w_in"][l - n_a], tb=True, out_dtype=BF16, name="fox_in_dx")
        dh, per_layer["mix_norm"][l] = rms_bwd(h1, _row(P["mix_norm"][l]), du, dh)
        dh, per_layer["ffn1_w_gate_up"][l], per_layer["ffn1_w_down"][l], per_layer["ffn1_norm"][l] = ffn_bwd(
            dh, s1, _row(P["ffn1_norm"][l]), P["ffn1_w_gate_up"][l], P["ffn1_w_down"][l])

    (G["mem_norm"],) = rowwise_bwd(f_rmsnorm, [mems], [mem_gain], [dmem_n], tile=ROW_TILE, name="mem_norm_bwd",
                                   row_grads=[None], const_grads=[True])
    for n, v in per_layer.items():
        G[n] = jnp.stack(v)
    for n, v in gdn_g.items():
        G[n] = jnp.stack(v)
    G["fox_w_in"] = jnp.stack(fox_g)
    return part, dh, G


_GDN_O0 = 4 * GDN_WIDTH
_GDN_O1 = _GDN_O0 + 2 * GDN_HEADS
_KV_WIDTH = 2 * FOX_WIDTH + FOX_HEADS


def derived_weights(gdn_w_in, gdn_conv, kv_w=None):
    zeros = jnp.zeros(gdn_w_in.shape[:-1] + (LANES - 2 * GDN_HEADS,), gdn_w_in.dtype)
    out = dict(
        gdn_w_in_pad=jnp.concatenate([gdn_w_in[..., :_GDN_O0], gdn_w_in[..., _GDN_O1:], gdn_w_in[..., _GDN_O0:_GDN_O1], zeros], axis=-1),
        conv8=jnp.pad(gdn_conv.astype(F32), ((0, 0), (0, 8 - CONV_WIDTH), (0, 0))))
    if kv_w is not None:
        out["kv_w_pad"] = jnp.pad(kv_w, ((0, 0), (0, KV_PAD_WIDTH - _KV_WIDTH)))
    return out


def reference_layout(G):
    gp = G["gdn_w_in_pad"]
    out = dict(G)
    out["gdn_w_in"] = jnp.concatenate([gp[..., :_GDN_O0], gp[..., _GDN_O0 + MEM_WIDTH:_GDN_O0 + MEM_WIDTH + 2 * GDN_HEADS],
                                       gp[..., _GDN_O0:_GDN_O0 + MEM_WIDTH]], axis=-1)
    out["gdn_conv"] = G["conv8"][:, :CONV_WIDTH]
    out["kv_w"] = G["kv_w_pad"][:, :_KV_WIDTH]
    out["gdn_A_log"] = G["gdn_A_log"][:, 0, :GDN_HEADS]
    out["gdn_dt_bias"] = G["gdn_dt_bias"][:, 0, :GDN_HEADS]
    out["gdn_out_norm"] = G["gdn_out_norm"][:, 0, :]
    out["kv_b_f"] = G["kv_b_f"][0, :FOX_HEADS]
    for n in ("ffn1_norm", "mix_norm", "ffn2_norm"):
        out[n] = G[n][:, 0, :]
    for n in ("mem_norm", "kv_norm", "final_norm"):
        out[n] = G[n][0]
    return {n: out[n] for n in WEIGHT_NAMES}


EXCHANGED = [("ffn1_w_gate_up", 2, 0), ("ffn1_w_down", 1, 0), ("ffn2_w_gate_up", 2, 0), ("ffn2_w_down", 1, 0),
             ("gdn_w_in", 2, 0), ("fox_w_in", 1, 0), ("w_out", 1, 0), ("mem_w_kv", 1, 0), ("kv_w", 0, 1)]
GDN_IN_SHARD = (4 * GDN_WIDTH + 2 * GDN_HEADS + MEM_WIDTH) // N_CHIPS
GDN_IN_SLOT = 896


def _slab(ref, axis_slices):
    idx = [slice(None)] * len(ref.shape)
    for axis, (start, size) in axis_slices.items():
        idx[axis] = pl.ds(start, size)
    return ref.at[tuple(idx)]


def _comm_multi(body, *, name, n_in, out_shapes, scratch):
    return pl.pallas_call(body, name=name, out_shape=out_shapes, in_specs=[_HBM] * n_in, out_specs=[_HBM] * len(out_shapes),
                          scratch_shapes=scratch, compiler_params=pltpu.CompilerParams(has_side_effects=True))


def gather_shards(shards, layout):
    n = len(shards)
    fulls = [tuple(d * (N_CHIPS if a == sa else 1) for a, d in enumerate(s.shape)) for s, (sa, _) in zip(shards, layout)]

    def body(*refs):
        ins, outs = refs[:n], refs[n:2 * n]
        send_sems, recv_sems, local_sems = refs[2 * n:]
        x, y, c = lax.axis_index("x"), lax.axis_index("y"), lax.axis_index("c")
        me, sibling = (x, y, c), (x, y, 1 - c)
        chips = [(1 - x, y), (x, 1 - y), (1 - x, 1 - y)]

        def region(w, px, py, pc):
            (sa, ha), shard = layout[w], shards[w].shape
            return _slab(outs[w], {sa: ((2 * px + py) * shard[sa], shard[sa]), ha: (pc * (shard[ha] // 2), shard[ha] // 2)})

        def my_half(w):
            ha, shard = layout[w][1], shards[w].shape
            return _slab(ins[w], {ha: (c * (shard[ha] // 2), shard[ha] // 2)})

        def copy(w, k, block, to, src=None):
            return pltpu.make_async_remote_copy(
                src_ref=region(w, *block) if src is None else src, dst_ref=region(w, *block),
                send_sem=send_sems.at[7 * w + k], recv_sem=recv_sems.at[7 * w + k], device_id=to, device_id_type=MESH)

        mine, first, passed = [], [], []
        for w in range(n):
            mine.append(pltpu.make_async_copy(my_half(w), region(w, *me), local_sems.at[w]))
            mine[w].start()
            first.append([copy(w, 0, me, sibling, src=my_half(w))]
                         + [copy(w, 1 + j, me, (*chip, c), src=my_half(w)) for j, chip in enumerate(chips)])
            for cp in first[w]:
                cp.start()
            passed.append([copy(w, 4 + j, (*chip, c), sibling) for j, chip in enumerate(chips)])
        for w in range(n):
            for j, chip in enumerate(chips):
                copy(w, 1 + j, (*chip, c), me).wait_recv()
                passed[w][j].start()
        for w in range(n):
            copy(w, 0, sibling, me).wait_recv()
            for j, chip in enumerate(chips):
                copy(w, 4 + j, (*chip, 1 - c), me).wait_recv()
        for w in range(n):
            for cp in first[w] + passed[w]:
                cp.wait_send()
            mine[w].wait()

    return _comm_multi(body, name="gather_shards", n_in=n,
                       out_shapes=[jax.ShapeDtypeStruct(f, s.dtype) for f, s in zip(fulls, shards)],
                       scratch=[pltpu.SemaphoreType.DMA((7 * n,)), pltpu.SemaphoreType.DMA((7 * n,)),
                                pltpu.SemaphoreType.DMA((n,))])(*shards)


def swap_other_halves(arrays, layout):
    n = len(arrays)
    halves = [tuple(d // 2 if a == ha else d for a, d in enumerate(g.shape)) for g, (_, ha) in zip(arrays, layout)]

    def body(*refs):
        ins, outs = refs[:n], refs[n:2 * n]
        send_sems, recv_sems = refs[2 * n:]
        x, y, c = lax.axis_index("x"), lax.axis_index("y"), lax.axis_index("c")
        copies = []
        for w in range(n):
            ha, size = layout[w][1], halves[w][layout[w][1]]
            copies.append(pltpu.make_async_remote_copy(
                src_ref=_slab(ins[w], {ha: ((1 - c) * size, size)}), dst_ref=outs[w], send_sem=send_sems.at[w],
                recv_sem=recv_sems.at[w], device_id=(x, y, 1 - c), device_id_type=MESH))
            copies[w].start()
        for cp in copies:
            cp.wait()

    return _comm_multi(body, name="grad_pair_swap", n_in=n,
                       out_shapes=[jax.ShapeDtypeStruct(h, g.dtype) for h, g in zip(halves, arrays)],
                       scratch=[pltpu.SemaphoreType.DMA((n,)), pltpu.SemaphoreType.DMA((n,))])(*arrays)


def scatter_to_chips(arrays, layout):
    n = len(arrays)
    slabs = [tuple(d // N_CHIPS if a == sa else d for a, d in enumerate(p.shape)) for p, (sa, _) in zip(arrays, layout)]

    def body(*refs):
        ins, outs = refs[:n], refs[n:2 * n]
        send_sems, recv_sems, local_sems = refs[2 * n:]
        x, y, c = lax.axis_index("x"), lax.axis_index("y"), lax.axis_index("c")
        me = 2 * x + y

        def slab(w, k):
            sa, size = layout[w][0], slabs[w][layout[w][0]]
            return _slab(ins[w], {sa: (k * size, size)})

        local, copies = [], []
        for w in range(n):
            local.append(pltpu.make_async_copy(slab(w, me), outs[w].at[me], local_sems.at[w]))
            local[w].start()
            for j, (px, py) in enumerate([(1 - x, y), (x, 1 - y), (1 - x, 1 - y)]):
                copies.append(pltpu.make_async_remote_copy(
                    src_ref=slab(w, 2 * px + py), dst_ref=outs[w].at[me], send_sem=send_sems.at[3 * w + j],
                    recv_sem=recv_sems.at[3 * w + j], device_id=(px, py, c), device_id_type=MESH))
                copies[-1].start()
        for cp in copies:
            cp.wait()
        for cp in local:
            cp.wait()

    return _comm_multi(body, name="grad_all_to_all", n_in=n,
                       out_shapes=[jax.ShapeDtypeStruct((N_CHIPS,) + s, p.dtype) for s, p in zip(slabs, arrays)],
                       scratch=[pltpu.SemaphoreType.DMA((3 * n,)), pltpu.SemaphoreType.DMA((3 * n,)),
                                pltpu.SemaphoreType.DMA((n,))])(*arrays)


def swap_with_sibling(arrays):
    n = len(arrays)

    def body(*refs):
        ins, outs = refs[:n], refs[n:2 * n]
        send_sems, recv_sems = refs[2 * n:]
        x, y, c = lax.axis_index("x"), lax.axis_index("y"), lax.axis_index("c")
        copies = [pltpu.make_async_remote_copy(src_ref=ins[w], dst_ref=outs[w], send_sem=send_sems.at[w], recv_sem=recv_sems.at[w],
                                               device_id=(x, y, 1 - c), device_id_type=MESH) for w in range(n)]
        for cp in copies:
            cp.start()
        for cp in copies:
            cp.wait()

    return _comm_multi(body, name="grad_half_swap", n_in=n, out_shapes=[jax.ShapeDtypeStruct(a.shape, a.dtype) for a in arrays],
                       scratch=[pltpu.SemaphoreType.DMA((n,)), pltpu.SemaphoreType.DMA((n,))])(*arrays)


def join_halves(mine, other, half_axis, c_arr, *, name):
    a0, a1, a2 = mine.shape
    tile = _row_tile(a1, a2)

    def body(c_ref, m_ref, o_ref, out_ref):
        for half in range(2):
            @pl.when(c_ref[0] == half)
            def _(half=half):
                out_ref[half] = m_ref[...]
                out_ref[1 - half] = o_ref[...]

    blk = pl.BlockSpec((None, tile, a2), lambda i, j, c_ref: (i, j, 0))
    if half_axis == 0:
        out_shape, out_blk = (2, a0, a1, a2), pl.BlockSpec((2, None, tile, a2), lambda i, j, c_ref: (0, i, j, 0))
    else:
        out_shape, out_blk = (a0, 2, a1, a2), pl.BlockSpec((None, 2, tile, a2), lambda i, j, c_ref: (i, 0, j, 0))
    spec = pltpu.PrefetchScalarGridSpec(num_scalar_prefetch=1, grid=(a0, a1 // tile), in_specs=[blk, blk], out_specs=out_blk)
    out = pl.pallas_call(body, name=name, out_shape=jax.ShapeDtypeStruct(out_shape, mine.dtype), grid_spec=spec,
                         compiler_params=pltpu.CompilerParams(vmem_limit_bytes=VMEM_LIMIT_BYTES,
                                                              dimension_semantics=("parallel", "parallel")))(c_arr, mine, other)
    return out.reshape((2 * a0, a1, a2) if half_axis == 0 else (a0, 2 * a1, a2))


def _row_tile(rows, cols, itemsize=4, budget=2 * 1024 * 1024):
    for t in (1024, 512, 256, 128, 64, 32, 16):
        if rows % t == 0 and t * cols * itemsize <= budget:
            return t
    return rows


def add_own_half(full, recv, half_axis, c_arr, *, name):
    a0, a1, a2 = recv.shape
    tile = _row_tile(a1, a2)

    def body(c_ref, f_ref, r_ref, o_ref):
        o_ref[...] = (f_ref[...] + r_ref[...]).astype(o_ref.dtype)

    if half_axis == 0:
        f_spec = pl.BlockSpec((None, tile, a2), lambda i, j, c_ref: (c_ref[0] * a0 + i, j, 0))
    else:
        f_spec = pl.BlockSpec((None, tile, a2), lambda i, j, c_ref: (i, c_ref[0] * (a1 // tile) + j, 0))
    blk = pl.BlockSpec((None, tile, a2), lambda i, j, c_ref: (i, j, 0))
    spec = pltpu.PrefetchScalarGridSpec(num_scalar_prefetch=1, grid=(a0, a1 // tile), in_specs=[f_spec, blk], out_specs=blk)
    return pl.pallas_call(body, name=name, out_shape=jax.ShapeDtypeStruct(recv.shape, BF16), grid_spec=spec,
                          compiler_params=pltpu.CompilerParams(vmem_limit_bytes=VMEM_LIMIT_BYTES,
                                                               dimension_semantics=("parallel", "parallel")))(c_arr, full, recv)


def sum_slots(q, *, name):
    _, a0, a1, a2 = q.shape
    tile = _row_tile(a1, a2, budget=1024 * 1024)

    def body(q_ref, o_ref):
        total = q_ref[0].astype(F32)
        for k in range(1, N_CHIPS):
            total = total + q_ref[k].astype(F32)
        o_ref[...] = total

    return _pcall(body, name=name, out_shape=jax.ShapeDtypeStruct((a0, a1, a2), F32), grid=(a0, a1 // tile),
                  in_specs=[pl.BlockSpec((N_CHIPS, None, tile, a2), lambda i, j: (0, i, j, 0))],
                  out_specs=pl.BlockSpec((None, tile, a2), lambda i, j: (i, j, 0)), sem=("parallel", "parallel"))(q)


def gather_weights(W):
    shards = []
    for name, _, _ in EXCHANGED:
        w = W[name].astype(BF16)
        if name == "gdn_w_in":
            w = jnp.pad(w, ((0, 0), (0, 0), (0, GDN_IN_SLOT - GDN_IN_SHARD)))
        if name == "kv_w":
            w = jnp.pad(w, ((0, 0), (0, KV_PAD_WIDTH - _KV_WIDTH)))[None]
        shards.append(w)
    fulls = dict(zip([n for n, _, _ in EXCHANGED], gather_shards(shards, [(sa, ha) for _, sa, ha in EXCHANGED])))
    slots = fulls["gdn_w_in"]
    fulls["gdn_w_in"] = jnp.concatenate([slots[..., k * GDN_IN_SLOT:k * GDN_IN_SLOT + GDN_IN_SHARD] for k in range(N_CHIPS)], axis=-1)
    fulls["kv_w_pad"] = fulls.pop("kv_w").reshape(-1, KV_PAD_WIDTH)
    conv = pack([W["gdn_conv"]], F32, row_multiple=8)
    conv_all = all_gather8(conv, name="gather_conv").reshape(N_DEV, conv.shape[0], PACK_COLS)
    fulls["gdn_conv"] = jnp.concatenate([unpack(conv_all[2 * k], [W["gdn_conv"].shape])[0] for k in range(N_CHIPS)], axis=-1)
    return fulls


def reduce_gradients(G):
    layout = [(sa, ha) for _, sa, ha in EXCHANGED]
    c_arr = lax.axis_index("c").astype(jnp.int32).reshape(1)
    received = swap_other_halves(G, layout)
    pairs = [add_own_half(g, r, ha, c_arr, name="grad_pair_sum") for g, r, (_, ha) in zip(G, received, layout)]
    slots = scatter_to_chips(pairs, layout)
    halves = [sum_slots(q, name="grad_chip_sum") for q in slots]
    others = swap_with_sibling(halves)
    return [join_halves(h, o, ha, c_arr, name="grad_join_halves") for h, o, (_, ha) in zip(halves, others, layout)]


def allreduce_small(G, names):
    packed = pack([G[n] for n in names], F32, row_multiple=8)
    gathered = all_gather8(packed, name="gather_small_grads").reshape(N_DEV, packed.shape[0], PACK_COLS)
    total = sum_leading(gathered, name="sum_small_grads")
    return dict(zip(names, unpack(total, [G[n].shape for n in names])))


def kernel(x, mem, *rest):
    n_w = len(WEIGHT_NAMES)
    W = dict(zip(WEIGHT_NAMES, rest[:n_w]))
    target = rest[n_w]
    M = dict(zip(WEIGHT_NAMES, rest[n_w + 1:2 * n_w + 1]))
    V = dict(zip(WEIGHT_NAMES, rest[2 * n_w + 1:3 * n_w + 1]))

    full = gather_weights(W)
    P = {n: W[n] for n in REPLICATED}
    P.update({n: full[n] for n in ("ffn1_w_gate_up", "ffn1_w_down", "ffn2_w_gate_up", "ffn2_w_down", "fox_w_in", "w_out",
                                   "mem_w_kv", "kv_w_pad")})
    derived = derived_weights(full["gdn_w_in"], full["gdn_conv"])
    P.update(gdn_w_in_pad=derived["gdn_w_in_pad"], conv8=derived["conv8"])

    part, dx, G = forward_backward(x[0], mem[0], target[0], P)
    loss = lax.psum(0.5 / x.shape[-1] * jnp.sum(part), ("x", "y", "c"))

    ref = reference_layout(G)
    exchange = {n: ref[n] for n, _, _ in EXCHANGED}
    exchange["gdn_w_in"] = jnp.concatenate(
        [jnp.pad(ref["gdn_w_in"][..., k * GDN_IN_SHARD:(k + 1) * GDN_IN_SHARD], ((0, 0), (0, 0), (0, GDN_IN_SLOT - GDN_IN_SHARD)))
         for k in range(N_CHIPS)], axis=-1)
    exchange["kv_w"] = G["kv_w_pad"].reshape(N_CHIPS, -1, KV_PAD_WIDTH)
    shards = dict(zip([n for n, _, _ in EXCHANGED], reduce_gradients([exchange[n] for n, _, _ in EXCHANGED])))
    shards["gdn_w_in"] = shards["gdn_w_in"][..., :GDN_IN_SHARD]
    shards["kv_w"] = shards["kv_w"][0, :, :_KV_WIDTH]
    grads = allreduce_small(ref, REPLICATED + ["gdn_conv"])
    conv_cols = W["gdn_conv"].shape[-1]
    chip = 2 * lax.axis_index("x") + lax.axis_index("y")
    grads["gdn_conv"] = lax.dynamic_slice_in_dim(grads["gdn_conv"], chip * conv_cols, conv_cols, axis=2)
    grads.update(shards)

    outs = {n: adamw(W[n], grads[n], M[n], V[n], name="adamw_" + n) for n in WEIGHT_NAMES}
    return (loss, dx[None], *[grads[n] for n in WEIGHT_NAMES], *[outs[n][0] for n in WEIGHT_NAMES],
            *[outs[n][1] for n in WEIGHT_NAMES], *[outs[n][2] for n in WEIGHT_NAMES])
```

```python
import jax
import jax.numpy as jnp
from jax import lax
from jax.experimental import pallas as pl
from jax.experimental.pallas import tpu as pltpu

F32, BF16 = jnp.float32, jnp.bfloat16
HI = lax.Precision.HIGHEST
MESH = pl.DeviceIdType.MESH

VMEM_LIMIT_BYTES = 48 * 1024 * 1024
LANES = 128
EPS = 1e-6
NEG_INF = -1e30

D_MODEL = 1024
HEAD_DIM = 128
GDN_HEADS = 6
GDN_WIDTH = GDN_HEADS * HEAD_DIM
FOX_HEADS = 6
FOX_WIDTH = FOX_HEADS * HEAD_DIM
MEM_HEADS = 4
MEM_HEAD_DIM = 64
MEM_WIDTH = MEM_HEADS * MEM_HEAD_DIM
FFN_HIDDEN = 2816
CONV_WIDTH = 4
GDN_CHUNK = 128
N_CHIPS = 4
N_DEV = 8

ADAM_LR, ADAM_B1, ADAM_B2, ADAM_EPS, ADAM_WD, ADAM_STEP = 0.001, 0.9, 0.999, 1e-08, 0.01, 10


def _pcall(body, *, name, out_shape, grid=(), in_specs=None, out_specs=None, scratch=(), sem=None):
    params = dict(vmem_limit_bytes=VMEM_LIMIT_BYTES)
    if sem is not None:
        params["dimension_semantics"] = sem
    kw = dict(grid=grid, in_specs=in_specs, out_specs=out_specs) if grid else {}
    return pl.pallas_call(body, name=name, out_shape=out_shape, scratch_shapes=list(scratch),
                          compiler_params=pltpu.CompilerParams(**params), **kw)


def _pick(n, cands):
    for c in cands:
        if n % c == 0:
            return c
    return n


def _make_dot(dtype, precision):
    def raw(a, b, dims):
        return lax.dot_general(a.astype(dtype), b.astype(dtype), (dims, ((), ())),
                               precision=precision, preferred_element_type=F32)

    @jax.custom_vjp
    def dot(a, b):
        return raw(a, b, ((1,), (0,)))

    def fwd(a, b):
        return dot(a, b), (a, b)

    def bwd(resid, ct):
        a, b = resid
        return raw(ct, b, ((1,), (1,))).astype(a.dtype), raw(a, ct, ((0,), (0,))).astype(b.dtype)

    dot.defvjp(fwd, bwd)
    dot.nn = lambda a, b: raw(a, b, ((1,), (0,)))
    dot.nt = lambda a, b: raw(a, b, ((1,), (1,)))
    dot.tn = lambda a, b: raw(a, b, ((0,), (0,)))
    return dot


bdot = _make_dot(BF16, None)
fdot = _make_dot(F32, HI)
idot = _make_dot(F32, lax.Precision.HIGH)
sdot = idot


def _sigmoid(x):
    return 0.5 * jnp.tanh(0.5 * x) + 0.5


def _silu(x):
    return x * _sigmoid(x)


def _softplus(x):
    return jnp.maximum(x, 0.0) + jnp.log(1.0 + jnp.exp(-jnp.abs(x)))


def _log_sigmoid(x):
    return -_softplus(-x)


def _iota2(shape, dim):
    return lax.broadcasted_iota(jnp.int32, shape, dim)


def mm(a, b, *, ta=False, tb=False, a_split=False, b_split=False, out_dtype=F32, scale=1.0, res=None, name):
    assert not (a_split and ta) and not (b_split and tb)
    (K, M) = a.shape if ta else ((2 * a.shape[2], a.shape[1]) if a_split else a.shape[::-1])
    (N, Kb) = b.shape if tb else ((2 * b.shape[2], b.shape[1]) if b_split else b.shape[::-1])
    assert K == Kb, (a.shape, b.shape, ta, tb)
    tm = _pick(M, (1024, 1408, 512, 256, 128))
    tn = _pick(N, (1024, 1408, 1152, 1664, 768, 512, 384, 256, 128))
    tk = _pick(K, (1024, 512, 256, 128)) if ta else _pick(K, (1024, 1408, 1152, 1664, 512, 256, 128))
    assert not a_split or (K // 2) % tk == 0
    assert not b_split or (N // 2) % tn == 0
    nk = K // tk
    dims = (((0 if ta else 1,), (1 if tb else 0,)), ((), ()))

    def body(a_ref, b_ref, *rest):
        o_ref, acc = rest[-2], rest[-1]
        k = pl.program_id(2)

        @pl.when(k == 0)
        def _():
            acc[...] = jnp.zeros_like(acc)

        acc[...] += lax.dot_general(a_ref[...].astype(BF16), b_ref[...].astype(BF16), dims,
                                    preferred_element_type=F32)

        @pl.when(k == nk - 1)
        def _():
            out = acc[...] * scale
            if res is not None:
                out = out + rest[0][...].astype(F32)
            o_ref[...] = out.astype(o_ref.dtype)

    a_spec = pl.BlockSpec((tk, tm), lambda i, j, k: (k, i)) if ta else pl.BlockSpec((tm, tk), lambda i, j, k: (i, k))
    b_spec = pl.BlockSpec((tn, tk), lambda i, j, k: (j, k)) if tb else pl.BlockSpec((tk, tn), lambda i, j, k: (k, j))
    if a_split:
        per_half = K // 2 // tk
        a_spec = pl.BlockSpec((None, tm, tk), lambda i, j, k: (k // per_half, i, k % per_half))
    if b_split:
        per_half = N // 2 // tn
        b_spec = pl.BlockSpec((None, tk, tn), lambda i, j, k: (j // per_half, k, j % per_half))
    o_spec = pl.BlockSpec((tm, tn), lambda i, j, k: (i, j))
    ins, specs = [a, b], [a_spec, b_spec]
    if res is not None:
        ins.append(res)
        specs.append(o_spec)
    return _pcall(body, name=name, out_shape=jax.ShapeDtypeStruct((M, N), out_dtype),
                  grid=(M // tm, N // tn, nk), in_specs=specs, out_specs=o_spec,
                  scratch=[pltpu.VMEM((tm, tn), F32)], sem=("parallel", "parallel", "arbitrary"))(*ins)


def mm_rms_bwd(a, b, x, gain_row, dres, *, a_split=False, name):
    (K, M) = (2 * a.shape[2], a.shape[1]) if a_split else a.shape[::-1]
    D = b.shape[0]
    assert b.shape[1] == K and x.shape == (M, D)
    tm = _pick(M, (512, 256, 128))
    tk = _pick(K, (1024, 1408, 1152, 1664, 512, 256, 128))
    assert not a_split or (K // 2) % tk == 0
    nk = K // tk

    def body(a_ref, b_ref, x_ref, g_ref, r_ref, o_ref, dg_ref, acc):
        i, k = pl.program_id(0), pl.program_id(1)

        @pl.when(k == 0)
        def _():
            acc[...] = jnp.zeros_like(acc)

        acc[...] += bdot.nt(a_ref[...], b_ref[...])

        @pl.when(k == nk - 1)
        def _():
            xv = x_ref[...]
            xh = xv * lax.rsqrt(jnp.mean(xv * xv, axis=-1, keepdims=True) + EPS)
            dn = acc[...]
            dy = dn * g_ref[...]
            dx = (dy - xh * jnp.mean(dy * xh, axis=-1, keepdims=True)) * lax.rsqrt(jnp.mean(xv * xv, axis=-1, keepdims=True) + EPS)
            o_ref[...] = dx + r_ref[...]
            dg = jnp.sum(dn * xh, axis=0, keepdims=True)

            @pl.when(i == 0)
            def _():
                dg_ref[...] = dg

            @pl.when(i > 0)
            def _():
                dg_ref[...] += dg

    a_spec = pl.BlockSpec((tm, tk), lambda i, k: (i, k))
    if a_split:
        per_half = K // 2 // tk
        a_spec = pl.BlockSpec((None, tm, tk), lambda i, k: (k // per_half, i, k % per_half))
    row = pl.BlockSpec((tm, D), lambda i, k: (i, 0))
    one = pl.BlockSpec((1, D), lambda i, k: (0, 0))
    return _pcall(body, name=name, out_shape=[jax.ShapeDtypeStruct((M, D), F32), jax.ShapeDtypeStruct((1, D), F32)],
                  grid=(M // tm, nk), in_specs=[a_spec, pl.BlockSpec((D, tk), lambda i, k: (0, k)), row, one, row],
                  out_specs=[row, one], scratch=[pltpu.VMEM((tm, D), F32)], sem=("arbitrary", "arbitrary"))(a, b, x, gain_row, dres)


def _row_spec(r, tile):
    if isinstance(r, tuple):
        arr, width, col = r
        return arr, pl.BlockSpec((tile, width), lambda i, col=col: (i, col))
    return r, pl.BlockSpec((tile, r.shape[1]), lambda i: (i, 0))


def _const_spec(c):
    return pl.BlockSpec(c.shape, lambda i: (0,) * c.ndim)


def rowwise(fn, rows, consts, outs, *, tile, name):
    arrs, specs = zip(*[_row_spec(r, tile) for r in rows])
    n_rows = arrs[0].shape[0]
    tile = min(tile, n_rows)
    n_in = len(rows) + len(consts)

    def body(*refs):
        res = fn(*[r[...] for r in refs[:n_in]])
        for o_ref, o in zip(refs[n_in:], res):
            o_ref[...] = o.astype(o_ref.dtype)

    arrs, specs = zip(*[_row_spec(r, tile) for r in rows])
    return _pcall(body, name=name,
                  out_shape=[jax.ShapeDtypeStruct((n_rows, w), dt) for w, dt in outs],
                  grid=(n_rows // tile,),
                  in_specs=list(specs) + [_const_spec(c) for c in consts],
                  out_specs=[pl.BlockSpec((tile, w), lambda i: (i, 0)) for w, _ in outs],
                  sem=("parallel",))(*arrs, *consts)


def rowwise_bwd(fn, rows, consts, cts, *, tile, name, row_grads, const_grads, add=None):
    arrs, _ = zip(*[_row_spec(r, tile) for r in rows])
    n_rows = arrs[0].shape[0]
    tile = min(tile, n_rows)
    arrs, specs = zip(*[_row_spec(r, tile) for r in rows])
    ct_arrs, ct_specs = zip(*[_row_spec(c, tile) for c in cts])
    add = add or {}
    add_idx = sorted(add)
    add_arrs, add_specs = (zip(*[_row_spec(add[i], tile) for i in add_idx]) if add_idx else ((), ()))
    nr, nc, nct, na = len(rows), len(consts), len(cts), len(add_idx)
    want_rows = [i for i, d in enumerate(row_grads) if d is not None]
    want_consts = [i for i, w in enumerate(const_grads) if w]

    def body(*refs):
        row_v = [r[...] for r in refs[:nr]]
        const_v = [r[...] for r in refs[nr:nr + nc]]
        ct_v = [r[...] for r in refs[nr + nc:nr + nc + nct]]
        add_v = {i: refs[nr + nc + nct + j][...] for j, i in enumerate(add_idx)}
        out_refs = refs[nr + nc + nct + na:]
        res, vjp = jax.vjp(fn, *row_v, *const_v)
        grads = vjp(tuple(c.astype(o.dtype) for c, o in zip(ct_v, res)))
        for o_ref, i in zip(out_refs, want_rows):
            g = grads[i].astype(F32)
            if i in add_v:
                g = g + add_v[i].astype(F32)
            o_ref[...] = g.astype(o_ref.dtype)
        first = pl.program_id(0) == 0
        for o_ref, i in zip(out_refs[len(want_rows):], want_consts):
            g = grads[nr + i].astype(F32)

            @pl.when(first)
            def _(o_ref=o_ref, g=g):
                o_ref[...] = g

            @pl.when(jnp.logical_not(first))
            def _(o_ref=o_ref, g=g):
                o_ref[...] += g

    def width(r):
        return r[1] if isinstance(r, tuple) else r.shape[1]

    out_shape = [jax.ShapeDtypeStruct((n_rows, width(rows[i])), row_grads[i]) for i in want_rows]
    out_shape += [jax.ShapeDtypeStruct(consts[i].shape, F32) for i in want_consts]
    out_specs = [pl.BlockSpec((tile, width(rows[i])), lambda i_: (i_, 0)) for i in want_rows]
    out_specs += [_const_spec(consts[i]) for i in want_consts]
    return _pcall(body, name=name, out_shape=out_shape, grid=(n_rows // tile,),
                  in_specs=list(specs) + [_const_spec(c) for c in consts] + list(ct_specs) + list(add_specs),
                  out_specs=out_specs, sem=("arbitrary",))(*arrs, *consts, *ct_arrs, *add_arrs)


def f_rmsnorm(x, g):
    x = x.astype(F32)
    return (x * lax.rsqrt(jnp.mean(x * x, axis=-1, keepdims=True) + EPS) * g,)


def _head_sel(first_lane):
    r, c = _iota2((LANES, GDN_WIDTH), 0), _iota2((LANES, GDN_WIDTH), 1)
    return (r == c // HEAD_DIM + first_lane).astype(F32)


def _tri(n, strict=False):
    r, c = _iota2((n, n), 0), _iota2((n, n), 1)
    return r > c if strict else r >= c


def f_gdn_pre(xc, ab, a_log, dt_bias):
    s = _silu(xc.astype(F32))
    qs, ks = [], []
    for h in range(GDN_HEADS):
        qh = s[:, h * HEAD_DIM:(h + 1) * HEAD_DIM]
        kh = s[:, GDN_WIDTH + h * HEAD_DIM:GDN_WIDTH + (h + 1) * HEAD_DIM]
        qs.append(qh * lax.rsqrt(jnp.sum(qh * qh, axis=-1, keepdims=True) + EPS) * (HEAD_DIM ** -0.5))
        ks.append(kh * lax.rsqrt(jnp.sum(kh * kh, axis=-1, keepdims=True) + EPS))
    q, k = jnp.concatenate(qs, axis=1), jnp.concatenate(ks, axis=1)
    v = s[:, 2 * GDN_WIDTH:]
    ab = ab.astype(F32)
    g = -jnp.exp(a_log) * _softplus(ab + dt_bias)
    gc = fdot(_tri(GDN_CHUNK).astype(F32), fdot(g, _head_sel(0)))
    beta = fdot(_sigmoid(ab), _head_sel(GDN_HEADS))
    return q, k, v, gc, beta


def _unit_lower_inverses(neg_lowers):
    C = neg_lowers[0].shape[0]
    eye = (_iota2((C, C), 0) == _iota2((C, C), 1)).astype(F32)
    invs = [eye + n for n in neg_lowers]
    powers = list(neg_lowers)
    for _ in range(6):
        powers = [idot.nn(p, p) for p in powers]
        invs = [inv + idot.nn(p, inv) for p, inv in zip(powers, invs)]
    return invs


@jax.custom_vjp
def _solve_with_inverse(inv, neg_lower, rhs):
    return idot.nn(inv, rhs)


def _solve_fwd(inv, neg_lower, rhs):
    x = idot.nn(inv, rhs)
    return x, (inv, x)


def _solve_bwd(resid, ct):
    inv, x = resid
    d_rhs = idot.tn(inv, ct)
    return jnp.zeros_like(inv), idot.nt(d_rhs, x), d_rhs


_solve_with_inverse.defvjp(_solve_fwd, _solve_bwd)


def f_gdn_intra(q, k, v, gc, beta, inv=None):
    C = GDN_CHUNK
    causal, strict = _tri(C), _tri(C, strict=True)
    is_last = _iota2((C, HEAD_DIM), 0) == C - 1
    heads = range(GDN_HEADS)
    sls = [slice(h * HEAD_DIM, (h + 1) * HEAD_DIM) for h in heads]
    qs, ks, vs, gs, bs = ([a[:, sl] for sl in sls] for a in (q, k, v, gc, beta))
    decays = [jnp.where(causal, jnp.exp(jnp.where(causal, g - g.T, 0.0)), 0.0) for g in gs]
    kbs = [kh * bh for kh, bh in zip(ks, bs)]
    kts = [kh.T for kh in ks]
    neg_lowers = [jnp.where(strict, -(idot(kb, kt) * d), 0.0) for kb, kt, d in zip(kbs, kts, decays)]
    qks = [jnp.where(causal, idot(qh, kt) * d, 0.0) for qh, kt, d in zip(qs, kts, decays)]
    rhss = [jnp.concatenate([vh * bh, kb * jnp.exp(g)], axis=1) for vh, bh, kb, g in zip(vs, bs, kbs, gs)]
    if inv is None:
        invs = _unit_lower_inverses(neg_lowers)
        sols = [idot.nn(m, r) for m, r in zip(invs, rhss)]
    else:
        sols = [_solve_with_inverse(inv[:, sl], n, r) for sl, n, r in zip(sls, neg_lowers, rhss)]
    g_lasts = [jnp.sum(jnp.where(is_last, g, 0.0), axis=0, keepdims=True) for g in gs]
    outs = [[s[:, :HEAD_DIM] for s in sols], [s[:, HEAD_DIM:] for s in sols], qks,
            [kh * jnp.exp(gl - g) for kh, gl, g in zip(ks, g_lasts, gs)], [qh * jnp.exp(g) for qh, g in zip(qs, gs)]]
    if inv is None:
        outs.append(invs)
    return tuple(jnp.concatenate(o, axis=1) for o in outs)


def f_gdn_post(o, z, gain):
    z = z.astype(F32)
    parts = []
    for h in range(GDN_HEADS):
        oh = o[:, h * HEAD_DIM:(h + 1) * HEAD_DIM]
        parts.append(oh * lax.rsqrt(jnp.mean(oh * oh, axis=-1, keepdims=True) + EPS) * gain)
    return (jnp.concatenate(parts, axis=1) * _silu(z),)


def f_mem_attn(q, k, v):
    q = q.astype(F32)
    lane_head = _iota2((1, MEM_WIDTH), 1) // MEM_HEAD_DIM
    kt = k.astype(F32).T
    masks = [(lane_head == h).astype(F32) for h in range(MEM_HEADS)]
    logits = [bdot(q * mask, kt) * (MEM_HEAD_DIM ** -0.5) for mask in masks]
    ps = [jnp.exp(s - jnp.max(s, axis=-1, keepdims=True)) for s in logits]
    ps = [p / jnp.sum(p, axis=-1, keepdims=True) for p in ps]
    outs = [bdot(p, v) * mask for p, mask in zip(ps, masks)]
    return ((outs[0] + outs[1]) + (outs[2] + outs[3]),)


def f_loss(y, t):
    d = y - t
    return (d * d,)


def conv_fwd(proj, w8, *, width, tile=512):
    n_rows = proj.shape[0]
    tile = min(tile, n_rows)

    def body(x_ref, halo_ref, w_ref, o_ref):
        i = pl.program_id(0)
        halo = jnp.where(i > 0, halo_ref[...].astype(F32), 0.0)
        xs = jnp.concatenate([halo, x_ref[...].astype(F32)], axis=0)
        acc = xs[8:] * w_ref[3:4, :]
        for j in range(CONV_WIDTH - 1):
            acc = acc + pltpu.roll(xs, CONV_WIDTH - 1 - j, 0)[8:] * w_ref[j:j + 1, :]
        o_ref[...] = acc

    return _pcall(body, name="gdn_conv_fwd", out_shape=jax.ShapeDtypeStruct((n_rows, width), F32),
                  grid=(n_rows // tile,),
                  in_specs=[pl.BlockSpec((tile, width), lambda i: (i, 0)),
                            pl.BlockSpec((8, width), lambda i: (jnp.maximum(i * (tile // 8) - 1, 0), 0)),
                            pl.BlockSpec((8, width), lambda i: (0, 0))],
                  out_specs=pl.BlockSpec((tile, width), lambda i: (i, 0)), sem=("parallel",))(proj, proj, w8)


def conv_bwd(proj, w8, dy, *, width, tile=512):
    n_rows = proj.shape[0]
    tile = min(tile, n_rows)
    n = n_rows // tile

    def body(x_ref, xhalo_ref, w_ref, dy_ref, dyhalo_ref, dx_ref, dw_ref):
        i = pl.program_id(0)
        dy = dy_ref[...]
        after = jnp.where(i < n - 1, dyhalo_ref[...], 0.0)
        ds = jnp.concatenate([dy, after], axis=0)
        dx = dy * w_ref[3:4, :]
        for j in range(CONV_WIDTH - 1):
            shift = CONV_WIDTH - 1 - j
            dx = dx + pltpu.roll(ds, tile + 8 - shift, 0)[:tile] * w_ref[j:j + 1, :]
        dx_ref[...] = dx.astype(dx_ref.dtype)
        halo = jnp.where(i > 0, xhalo_ref[...].astype(F32), 0.0)
        xs = jnp.concatenate([halo, x_ref[...].astype(F32)], axis=0)
        rows = [jnp.sum(dy * pltpu.roll(xs, CONV_WIDTH - 1 - j, 0)[8:], axis=0, keepdims=True)
                for j in range(CONV_WIDTH - 1)]
        rows.append(jnp.sum(dy * xs[8:], axis=0, keepdims=True))
        dw = jnp.concatenate(rows + [jnp.zeros((8 - CONV_WIDTH, width), F32)], axis=0)

        @pl.when(i == 0)
        def _():
            dw_ref[...] = dw

        @pl.when(i > 0)
        def _():
            dw_ref[...] += dw

    t8 = tile // 8
    return _pcall(body, name="gdn_conv_bwd",
                  out_shape=[jax.ShapeDtypeStruct((n_rows, width), BF16), jax.ShapeDtypeStruct((8, width), F32)],
                  grid=(n,),
                  in_specs=[pl.BlockSpec((tile, width), lambda i: (i, 0)),
                            pl.BlockSpec((8, width), lambda i: (jnp.maximum(i * t8 - 1, 0), 0)),
                            pl.BlockSpec((8, width), lambda i: (0, 0)),
                            pl.BlockSpec((tile, width), lambda i: (i, 0)),
                            pl.BlockSpec((8, width), lambda i: (jnp.minimum((i + 1) * t8, n * t8 - 1), 0))],
                  out_specs=[pl.BlockSpec((tile, width), lambda i: (i, 0)), pl.BlockSpec((8, width), lambda i: (0, 0))],
                  sem=("arbitrary",))(proj, proj, w8, dy, dy)


def gdn_scan_fwd(u, w, qk, kt, qh, gc):
    n_rows = u.shape[0]
    C, n = GDN_CHUNK, u.shape[0] // GDN_CHUNK

    def body(u_ref, w_ref, qk_ref, kt_ref, qh_ref, gc_ref, o_ref, vn_ref, sin_ref, st):
        @pl.when(pl.program_id(0) == 0)
        def _():
            st[...] = jnp.zeros_like(st)

        sin_ref[0] = st[...]
        sls = [slice(h * HEAD_DIM, (h + 1) * HEAD_DIM) for h in range(GDN_HEADS)]
        states = [st[sl, :] for sl in sls]
        v_news = [u_ref[:, sl] - sdot(w_ref[:, sl], s) for sl, s in zip(sls, states)]
        from_state = [sdot(qh_ref[:, sl], s) for sl, s in zip(sls, states)]
        for sl, a, v_new in zip(sls, from_state, v_news):
            o_ref[:, sl] = a + sdot(qk_ref[:, sl], v_new)
            vn_ref[:, sl] = v_new
        for sl, s, v_new in zip(sls, states, v_news):
            st[sl, :] = s * jnp.exp(gc_ref[C - 1:C, sl]) + sdot.tn(kt_ref[:, sl], v_new)

    blk = pl.BlockSpec((C, GDN_WIDTH), lambda i: (i, 0))
    return _pcall(body, name="gdn_scan_fwd",
                  out_shape=[jax.ShapeDtypeStruct((n_rows, GDN_WIDTH), F32), jax.ShapeDtypeStruct((n_rows, GDN_WIDTH), F32),
                             jax.ShapeDtypeStruct((n, GDN_WIDTH, HEAD_DIM), F32)],
                  grid=(n,), in_specs=[blk] * 6,
                  out_specs=[blk, blk, pl.BlockSpec((1, GDN_WIDTH, HEAD_DIM), lambda i: (i, 0, 0))],
                  scratch=[pltpu.VMEM((GDN_WIDTH, HEAD_DIM), F32)], sem=("arbitrary",))(u, w, qk, kt, qh, gc)


def gdn_scan_bwd(do, w, qk, kt, qh, gc, vn, sin):
    n_rows = do.shape[0]
    C, n = GDN_CHUNK, do.shape[0] // GDN_CHUNK

    def body(do_ref, w_ref, qk_ref, kt_ref, qh_ref, gc_ref, vn_ref, sin_ref,
             du_ref, dw_ref, dqk_ref, dkt_ref, dqh_ref, dgl_ref, dst):
        @pl.when(pl.program_id(0) == 0)
        def _():
            dst[...] = jnp.zeros_like(dst)

        sls = [slice(h * HEAD_DIM, (h + 1) * HEAD_DIM) for h in range(GDN_HEADS)]
        dvns = [sdot.tn(qk_ref[:, sl], do_ref[:, sl]) + sdot(kt_ref[:, sl], dst[sl, :]) for sl in sls]
        for sl, dvn in zip(sls, dvns):
            du_ref[:, sl] = dvn
            dw_ref[:, sl] = -sdot.nt(dvn, sin_ref[0, sl, :])
        for sl in sls:
            dqk_ref[:, sl] = sdot.nt(do_ref[:, sl], vn_ref[:, sl])
            dkt_ref[:, sl] = sdot.nt(vn_ref[:, sl], dst[sl, :])
            dqh_ref[:, sl] = sdot.nt(do_ref[:, sl], sin_ref[0, sl, :])
        for sl, dvn in zip(sls, dvns):
            ds_out = dst[sl, :]
            e = jnp.exp(gc_ref[C - 1:C, sl])
            dgl = jnp.sum(ds_out * sin_ref[0, sl, :], axis=0, keepdims=True) * e
            dgl_ref[:, sl] = jnp.broadcast_to(dgl, (8, HEAD_DIM))
            dst[sl, :] = sdot.tn(qh_ref[:, sl], do_ref[:, sl]) + e * ds_out - sdot.tn(w_ref[:, sl], dvn)

    blk = pl.BlockSpec((C, GDN_WIDTH), lambda i: (n - 1 - i, 0))
    row = jax.ShapeDtypeStruct((n_rows, GDN_WIDTH), F32)
    return _pcall(body, name="gdn_scan_bwd",
                  out_shape=[row] * 5 + [jax.ShapeDtypeStruct((n * 8, GDN_WIDTH), F32)],
                  grid=(n,), in_specs=[blk] * 7 + [pl.BlockSpec((1, GDN_WIDTH, HEAD_DIM), lambda i: (n - 1 - i, 0, 0))],
                  out_specs=[blk] * 5 + [pl.BlockSpec((8, GDN_WIDTH), lambda i: (n - 1 - i, 0))],
                  scratch=[pltpu.VMEM((GDN_WIDTH, HEAD_DIM), F32)], sem=("arbitrary",))(do, w, qk, kt, qh, gc, vn, sin)


def gdn_intra_bwd(q, k, v, gc, beta, inv, cts, dgl):
    n_rows = q.shape[0]
    C = GDN_CHUNK

    def body(*refs):
        ins = [r[...] for r in refs[:5]]
        inv_v = refs[5][...]
        ct = tuple(r[...] for r in refs[6:11])
        dgl_v = refs[11][...]
        _, vjp = jax.vjp(lambda *a: f_gdn_intra(*a, inv=inv_v), *ins)
        grads = list(vjp(ct))
        last = _iota2((C, GDN_WIDTH), 0) == C - 1
        grads[3] = grads[3] + jnp.where(last, jnp.broadcast_to(dgl_v[0:1, :], (C, GDN_WIDTH)), 0.0)
        for o_ref, g in zip(refs[12:], grads):
            o_ref[...] = g

    blk = pl.BlockSpec((C, GDN_WIDTH), lambda i: (i, 0))
    return _pcall(body, name="gdn_intra_bwd", out_shape=[jax.ShapeDtypeStruct((n_rows, GDN_WIDTH), F32)] * 5,
                  grid=(n_rows // C,), in_specs=[blk] * 11 + [pl.BlockSpec((8, GDN_WIDTH), lambda i: (i, 0))],
                  out_specs=[blk] * 5, sem=("parallel",))(q, k, v, gc, beta, inv, *cts, dgl)


def fox_gate_fwd(f, b_f):
    n_rows = f.shape[0]
    T = LANES

    def body(f_ref, b_ref, cb_ref, carry):
        @pl.when(pl.program_id(0) == 0)
        def _():
            carry[...] = jnp.zeros_like(carry)

        c = fdot(_tri(T).astype(F32), _log_sigmoid(f_ref[...] + b_ref[...])) + carry[...]
        carry[...] = c[T - 1:T, :]
        cb_ref[...] = fdot(c, _head_sel(0))

    return _pcall(body, name="fox_gate_fwd", out_shape=jax.ShapeDtypeStruct((n_rows, FOX_WIDTH), F32),
                  grid=(n_rows // T,),
                  in_specs=[pl.BlockSpec((T, LANES), lambda i: (i, 0)), pl.BlockSpec((1, LANES), lambda i: (0, 0))],
                  out_specs=pl.BlockSpec((T, FOX_WIDTH), lambda i: (i, 0)),
                  scratch=[pltpu.VMEM((1, LANES), F32)], sem=("arbitrary",))(f, b_f)


def fox_gate_bwd(f, b_f, dcrow, dcb):
    n_rows = f.shape[0]
    T = LANES
    n = n_rows // T

    def body(f_ref, b_ref, dc_ref, dcb_ref, df_ref, db_ref, carry):
        i = pl.program_id(0)

        @pl.when(i == 0)
        def _():
            carry[...] = jnp.zeros_like(carry)

        rows = [dc_ref[h] for h in range(FOX_HEADS)] + [jnp.zeros((T - FOX_HEADS, T), F32)]
        first_lane = (_iota2((FOX_WIDTH, LANES), 0) == _iota2((FOX_WIDTH, LANES), 1) * HEAD_DIM).astype(F32)
        dc = jnp.concatenate(rows, axis=0).T + fdot(dcb_ref[...], first_lane)
        dlog = fdot.tn(_tri(T).astype(F32), dc) + carry[...]
        carry[...] = dlog[0:1, :]
        df = dlog * (1.0 - _sigmoid(f_ref[...] + b_ref[...]))
        df_ref[...] = df
        db = jnp.sum(df, axis=0, keepdims=True)

        @pl.when(i == 0)
        def _():
            db_ref[...] = db

        @pl.when(i > 0)
        def _():
            db_ref[...] += db

    return _pcall(body, name="fox_gate_bwd",
                  out_shape=[jax.ShapeDtypeStruct((n_rows, LANES), F32), jax.ShapeDtypeStruct((1, LANES), F32)],
                  grid=(n,),
                  in_specs=[pl.BlockSpec((T, LANES), lambda i: (n - 1 - i, 0)), pl.BlockSpec((1, LANES), lambda i: (0, 0)),
                            pl.BlockSpec((FOX_HEADS, 1, T), lambda i: (0, 0, n - 1 - i)),
                            pl.BlockSpec((T, FOX_WIDTH), lambda i: (n - 1 - i, 0))],
                  out_specs=[pl.BlockSpec((T, LANES), lambda i: (n - 1 - i, 0)), pl.BlockSpec((1, LANES), lambda i: (0, 0))],
                  scratch=[pltpu.VMEM((1, LANES), F32)], sem=("arbitrary",))(f, b_f, dcrow, dcb)


FOX_AUG = 2 * HEAD_DIM


def _fox_tiles(n_rows):
    return min(1024, n_rows), min(512, n_rows)


def _fox_pairs(n_rows, query_major):
    tq, tk = _fox_tiles(n_rows)
    nq, r = n_rows // tq, tq // tk
    if query_major:
        pairs = [(i, j) for i in range(nq) for j in range(r * (i + 1))]
    else:
        pairs = [(i, j) for j in range(nq * r) for i in range(j // r, nq)]
    return jnp.asarray([p[0] for p in pairs], jnp.int32), jnp.asarray([p[1] for p in pairs], jnp.int32)


def fox_augment(x, cb, query_side):
    def fn(xt, ct):
        xt = xt.astype(F32)
        lane = _iota2((xt.shape[0], HEAD_DIM), 1)
        parts = []
        for h in range(FOX_HEADS):
            sl = slice(h * HEAD_DIM, (h + 1) * HEAD_DIM)
            c = ct[:, sl]
            hi = c.astype(BF16).astype(F32)
            mid = (c - hi).astype(BF16).astype(F32)
            lo = (c - hi - mid).astype(BF16).astype(F32)
            terms = jnp.where(lane % 3 == 0, hi, jnp.where(lane % 3 == 1, mid, lo))
            if query_side:
                extra = jnp.where(lane < 3, terms, jnp.where(lane < 6, 1.0, 0.0))
                parts += [xt[:, sl] * (HEAD_DIM ** -0.5), extra]
            else:
                extra = jnp.where(lane < 3, 1.0, jnp.where(lane < 6, -terms, 0.0))
                parts += [xt[:, sl], extra]
        return (jnp.concatenate(parts, axis=1),)

    return rowwise(fn, [(x, FOX_WIDTH, 0), cb], [], [(FOX_HEADS * FOX_AUG, BF16)], tile=ROW_TILE,
                   name="fox_augment_q" if query_side else "fox_augment_k")[0]


def _pcall_tables(body, *, name, out_shape, grid, tables, in_specs, out_specs, scratch, sem):
    spec = pltpu.PrefetchScalarGridSpec(num_scalar_prefetch=len(tables), grid=grid, in_specs=in_specs, out_specs=out_specs,
                                        scratch_shapes=list(scratch))
    return pl.pallas_call(body, name=name, out_shape=out_shape, grid_spec=spec,
                          compiler_params=pltpu.CompilerParams(vmem_limit_bytes=VMEM_LIMIT_BYTES, dimension_semantics=sem))


def _fox_logits(qa, ka, offset):
    s = bdot.nt(qa, ka)
    if offset is not None:
        s = jnp.where(_iota2(s.shape, 0) + offset >= _iota2(s.shape, 1), s, NEG_INF)
    return s


def _fox_p_ds(offset, qa_ref, ka_ref, v_ref, o_ref, lse_ref, do_ref):
    s = _fox_logits(qa_ref[...], ka_ref[...], offset)
    p = jnp.exp(s - jnp.tile(lse_ref[...], (1, s.shape[1] // LANES)))
    d_o = do_ref[...].astype(F32)
    delta = jnp.sum(d_o * o_ref[...].astype(F32), axis=-1, keepdims=True)
    return p, p * (bdot.nt(d_o, v_ref[...]) - delta), d_o


def _fox_on_diagonal(i, j, r, tk, step):
    @pl.when(j < r * i)
    def _():
        step(None)

    for m in range(r):
        @pl.when(j == r * i + m)
        def _(m=m):
            step(-m * tk)


def _fox_specs(tq, tk, do_col):
    qaspec = pl.BlockSpec((tq, FOX_AUG), lambda h, p, it, jt: (it[p], h))
    qspec = pl.BlockSpec((tq, HEAD_DIM), lambda h, p, it, jt: (it[p], h))
    dospec = pl.BlockSpec((tq, HEAD_DIM), lambda h, p, it, jt: (it[p], do_col + h))
    kaspec = pl.BlockSpec((tk, FOX_AUG), lambda h, p, it, jt: (jt[p], h))
    kspec = pl.BlockSpec((tk, HEAD_DIM), lambda h, p, it, jt: (jt[p], h))
    vspec = pl.BlockSpec((tk, HEAD_DIM), lambda h, p, it, jt: (jt[p], FOX_HEADS + h))
    cspec = pl.BlockSpec((1, 1, tk), lambda h, p, it, jt: (h, 0, jt[p]))
    return qaspec, qspec, dospec, kaspec, kspec, vspec, cspec


def fox_fwd(qa, kv, ka):
    n_rows = kv.shape[0]
    tq, tk = _fox_tiles(n_rows)
    r = tq // tk
    tables = _fox_pairs(n_rows, True)

    def body(it, jt, qa_ref, ka_ref, v_ref, o_ref, lse_ref, m_sc, l_sc, acc):
        i, j = it[pl.program_id(1)], jt[pl.program_id(1)]

        @pl.when(j == 0)
        def _():
            m_sc[...] = jnp.full(m_sc.shape, NEG_INF, F32)
            l_sc[...] = jnp.zeros_like(l_sc)
            acc[...] = jnp.zeros_like(acc)

        def step(offset):
            s = _fox_logits(qa_ref[...], ka_ref[...], offset)
            m_old = m_sc[...]
            m_new = jnp.maximum(m_old, jnp.max(s, axis=-1, keepdims=True))
            alpha = jnp.exp(m_old - m_new)
            p = jnp.exp(s - jnp.tile(m_new, (1, tk // LANES)))
            l_sc[...] = l_sc[...] * alpha + jnp.sum(p, axis=-1, keepdims=True)
            acc[...] = acc[...] * alpha + bdot(p, v_ref[...])
            m_sc[...] = m_new

        _fox_on_diagonal(i, j, r, tk, step)

        @pl.when(j == r * i + r - 1)
        def _():
            o_ref[...] = (acc[...] / l_sc[...]).astype(o_ref.dtype)
            lse_ref[...] = m_sc[...] + jnp.log(l_sc[...])

    qaspec, qspec, _, kaspec, _, vspec, _ = _fox_specs(tq, tk, 0)
    return _pcall_tables(body, name="fox_fwd",
                         out_shape=[jax.ShapeDtypeStruct((n_rows, FOX_WIDTH), BF16), jax.ShapeDtypeStruct((n_rows, FOX_WIDTH), F32)],
                         grid=(FOX_HEADS, tables[0].shape[0]), tables=tables,
                         in_specs=[qaspec, kaspec, vspec], out_specs=[qspec, qspec],
                         scratch=[pltpu.VMEM((tq, HEAD_DIM), F32)] * 3, sem=("parallel", "arbitrary"))(*tables, qa, ka, kv)


def fox_bwd_dq(qa, kv, ka, o, lse, do, prev=None):
    do, _, do_col = do
    do_col *= FOX_HEADS
    n_rows = kv.shape[0]
    tq, tk = _fox_tiles(n_rows)
    r = tq // tk
    n_prev = 0 if prev is None else 1
    tables = _fox_pairs(n_rows, True)

    def body(it, jt, qa_ref, ka_ref, k_ref, v_ref, o_ref, lse_ref, do_ref, *rest):
        dq_ref, drow_ref, acc, rows = rest[n_prev:]
        i, j = it[pl.program_id(1)], jt[pl.program_id(1)]

        @pl.when(j == 0)
        def _():
            acc[...] = jnp.zeros_like(acc)
            rows[...] = jnp.zeros_like(rows)

        def step(offset):
            _, ds, _ = _fox_p_ds(offset, qa_ref, ka_ref, v_ref, o_ref, lse_ref, do_ref)
            acc[...] += bdot(ds, k_ref[...])
            rows[...] += jnp.sum(ds, axis=-1, keepdims=True)

        _fox_on_diagonal(i, j, r, tk, step)

        @pl.when(j == r * i + r - 1)
        def _():
            dq_ref[...] = (acc[...] * (HEAD_DIM ** -0.5)).astype(dq_ref.dtype)
            drow_ref[...] = rows[...] + rest[0][...] if n_prev else rows[...]

    qaspec, qspec, dospec, kaspec, kspec, vspec, _ = _fox_specs(tq, tk, do_col)
    return _pcall_tables(body, name="fox_bwd_dq" + ("_acc" if n_prev else ""),
                         out_shape=[jax.ShapeDtypeStruct((n_rows, FOX_WIDTH), BF16), jax.ShapeDtypeStruct((n_rows, FOX_WIDTH), F32)],
                         grid=(FOX_HEADS, tables[0].shape[0]), tables=tables,
                         in_specs=[qaspec, kaspec, kspec, vspec, qspec, qspec, dospec] + [qspec] * n_prev,
                         out_specs=[qspec, qspec], scratch=[pltpu.VMEM((tq, HEAD_DIM), F32)] * 2,
                         sem=("parallel", "arbitrary"))(*tables, qa, ka, kv, kv, o, lse, do, *([prev] if n_prev else []))


def fox_bwd_dkv(qa, kv, ka, o, lse, do, prev=None):
    do, _, do_col = do
    do_col *= FOX_HEADS
    n_rows = kv.shape[0]
    tq, tk = _fox_tiles(n_rows)
    nq, r = n_rows // tq, tq // tk
    n_prev = 0 if prev is None else 3
    tables = _fox_pairs(n_rows, False)

    def body(it, jt, qa_ref, ka_ref, v_ref, o_ref, lse_ref, do_ref, *rest):
        prev_refs = rest[:n_prev]
        dk_ref, dv_ref, dc_ref, dk_acc, dv_acc, dc_acc = rest[n_prev:]
        i, j = it[pl.program_id(1)], jt[pl.program_id(1)]

        @pl.when(j >= r * i)
        def _():
            dk_acc[...] = jnp.zeros_like(dk_acc)
            dv_acc[...] = jnp.zeros_like(dv_acc)
            dc_acc[...] = jnp.zeros_like(dc_acc)

        def step(offset):
            p, ds, d_o = _fox_p_ds(offset, qa_ref, ka_ref, v_ref, o_ref, lse_ref, do_ref)
            dv_acc[...] += bdot.tn(p, d_o)
            dk_acc[...] += bdot.tn(ds, qa_ref[:, :HEAD_DIM])
            dc_acc[...] -= jnp.sum(ds, axis=0, keepdims=True)

        _fox_on_diagonal(i, j, r, tk, step)

        @pl.when(i == nq - 1)
        def _():
            dk, dv, dc = dk_acc[...], dv_acc[...], dc_acc[...]
            if n_prev:
                dk, dv, dc = dk + prev_refs[0][...], dv + prev_refs[1][...], dc + prev_refs[2][0]
            dk_ref[...] = dk
            dv_ref[...] = dv
            dc_ref[0] = dc

    qaspec, qspec, dospec, kaspec, kspec, vspec, cspec = _fox_specs(tq, tk, do_col)
    return _pcall_tables(body, name="fox_bwd_dkv" + ("_acc" if n_prev else ""),
                         out_shape=[jax.ShapeDtypeStruct((n_rows, FOX_WIDTH), F32), jax.ShapeDtypeStruct((n_rows, FOX_WIDTH), F32),
                                    jax.ShapeDtypeStruct((FOX_HEADS, 1, n_rows), F32)],
                         grid=(FOX_HEADS, tables[0].shape[0]), tables=tables,
                         in_specs=[qaspec, kaspec, vspec, qspec, qspec, dospec] + [kspec, kspec, cspec][:n_prev],
                         out_specs=[kspec, kspec, cspec],
                         scratch=[pltpu.VMEM((tk, HEAD_DIM), F32), pltpu.VMEM((tk, HEAD_DIM), F32), pltpu.VMEM((1, tk), F32)],
                         sem=("parallel", "arbitrary"))(*tables, qa, ka, kv, o, lse, do, *(prev or ()))


def loss_head(h, gain, target, *, tile=512):
    n_rows, d = h.shape
    tile = min(tile, n_rows)

    def body(h_ref, g_ref, t_ref, part_ref, dy_ref):
        (y,) = f_rmsnorm(h_ref[...], g_ref[...])
        diff = y - t_ref[...]
        dy_ref[...] = diff * (1.0 / d)
        part = jnp.sum(diff * diff, axis=0, keepdims=True)
        first = pl.program_id(0) == 0

        @pl.when(first)
        def _():
            part_ref[...] = part

        @pl.when(jnp.logical_not(first))
        def _():
            part_ref[...] += part

    blk = pl.BlockSpec((tile, d), lambda i: (i, 0))
    one = pl.BlockSpec((1, d), lambda i: (0, 0))
    return _pcall(body, name="loss_head",
                  out_shape=[jax.ShapeDtypeStruct((1, d), F32), jax.ShapeDtypeStruct((n_rows, d), F32)],
                  grid=(n_rows // tile,), in_specs=[blk, one, blk], out_specs=[one, blk], sem=("arbitrary",))(h, gain, target)


def adamw(w, g, m, v, *, name):
    shape = w.shape
    cols = shape[-1] if w.ndim >= 2 else w.size
    rows = w.size // cols
    tile = _pick(rows, (256, 128, 64, 32, 16, 8))
    as2d = lambda a: a.reshape(rows, cols)

    def body(w_ref, g_ref, m_ref, v_ref, d_ref, nm_ref, nv_ref):
        g_ = g_ref[...]
        m_ = ADAM_B1 * m_ref[...] + (1.0 - ADAM_B1) * g_
        v_ = ADAM_B2 * v_ref[...] + (1.0 - ADAM_B2) * (g_ * g_)
        m_hat = m_ / (1.0 - ADAM_B1 ** ADAM_STEP)
        v_hat = v_ / (1.0 - ADAM_B2 ** ADAM_STEP)
        d_ref[...] = -ADAM_LR * (m_hat / (jnp.sqrt(v_hat) + ADAM_EPS) + ADAM_WD * w_ref[...])
        nm_ref[...] = m_
        nv_ref[...] = v_

    blk = pl.BlockSpec((tile, cols), lambda i: (i, 0))
    outs = _pcall(body, name=name, out_shape=[jax.ShapeDtypeStruct((rows, cols), F32)] * 3, grid=(rows // tile,),
                  in_specs=[blk] * 4, out_specs=[blk] * 3, sem=("parallel",))(as2d(w), as2d(g), as2d(m), as2d(v))
    return tuple(o.reshape(shape) for o in outs)


def sum_leading(a, *, name):
    p, r, c = a.shape
    tile = _pick(r, (256, 128, 64, 32, 16, 8))

    def body(a_ref, o_ref):
        total = a_ref[0].astype(F32)
        for k in range(1, p):
            total = total + a_ref[k].astype(F32)
        o_ref[...] = total

    return _pcall(body, name=name, out_shape=jax.ShapeDtypeStruct((r, c), F32), grid=(r // tile,),
                  in_specs=[pl.BlockSpec((p, tile, c), lambda i: (0, i, 0))],
                  out_specs=pl.BlockSpec((tile, c), lambda i: (i, 0)), sem=("parallel",))(a)


_HBM = pl.BlockSpec(memory_space=pltpu.HBM)


def _comm_call(body, *, name, out_shape, n_in, scratch):
    return pl.pallas_call(body, name=name, out_shape=out_shape, in_specs=[_HBM] * n_in, out_specs=_HBM,
                          scratch_shapes=scratch,
                          compiler_params=pltpu.CompilerParams(has_side_effects=True))


def all_gather8(a, *, name):
    m_per, n = a.shape

    def body(x_ref, out_ref, send_sems, recv_sems, local_sem):
        x, y, c = lax.axis_index("x"), lax.axis_index("y"), lax.axis_index("c")
        me, sibling = (x, y, c), (x, y, 1 - c)
        chips = [(1 - x, y), (x, 1 - y), (1 - x, 1 - y)]

        def rows(px, py, pc):
            return out_ref.at[pl.ds((4 * px + 2 * py + pc) * m_per, m_per), :]

        def copy(k, block, to, src=None):
            return pltpu.make_async_remote_copy(
                src_ref=rows(*block) if src is None else src, dst_ref=rows(*block),
                send_sem=send_sems.at[k], recv_sem=recv_sems.at[k], device_id=to, device_id_type=MESH)

        mine = pltpu.make_async_copy(x_ref, rows(*me), local_sem)
        mine.start()
        first = [copy(0, me, sibling, src=x_ref)]
        first += [copy(1 + j, me, (*chip, c), src=x_ref) for j, chip in enumerate(chips)]
        for cp in first:
            cp.start()
        passed = [copy(4 + j, (*chip, c), sibling) for j, chip in enumerate(chips)]
        for j, chip in enumerate(chips):
            copy(1 + j, (*chip, c), me).wait_recv()
            passed[j].start()
        copy(0, sibling, me).wait_recv()
        for j, chip in enumerate(chips):
            copy(4 + j, (*chip, 1 - c), me).wait_recv()
        for cp in first + passed:
            cp.wait_send()
        mine.wait()

    return _comm_call(body, name=name, out_shape=jax.ShapeDtypeStruct((N_DEV * m_per, n), a.dtype), n_in=1,
                      scratch=[pltpu.SemaphoreType.DMA((7,)), pltpu.SemaphoreType.DMA((7,)), pltpu.SemaphoreType.DMA])(a)


PACK_COLS = 1024
PACK_ROW_MULTIPLE = 32

WEIGHT_NAMES = ["ffn1_norm", "ffn1_w_gate_up", "ffn1_w_down", "mix_norm", "ffn2_norm", "ffn2_w_gate_up", "ffn2_w_down",
                "gdn_w_in", "gdn_conv", "gdn_A_log", "gdn_dt_bias", "gdn_out_norm", "fox_w_in", "w_out", "mem_norm",
                "mem_w_kv", "kv_norm", "kv_w", "kv_b_f", "final_norm"]
SHARDED = [("ffn1_w_gate_up", 2), ("ffn1_w_down", 1), ("ffn2_w_gate_up", 2), ("ffn2_w_down", 1), ("gdn_w_in", 2),
           ("gdn_conv", 2), ("fox_w_in", 1), ("w_out", 1), ("mem_w_kv", 1), ("kv_w", 0)]
REPLICATED = [n for n in WEIGHT_NAMES if n not in dict(SHARDED)]


PACK_PIECE_ROWS = 16


def _rows_of(size):
    return -(-size // (PACK_COLS * PACK_PIECE_ROWS)) * PACK_PIECE_ROWS


def pack(pieces, dtype, row_multiple=PACK_ROW_MULTIPLE):
    bufs, total = [], 0
    for p in pieces:
        flat = p.astype(dtype).reshape(-1)
        rows = _rows_of(flat.size)
        bufs.append(jnp.pad(flat, (0, rows * PACK_COLS - flat.size)).reshape(rows, PACK_COLS))
        total += rows
    pad = -total % row_multiple
    if pad:
        bufs.append(jnp.zeros((pad, PACK_COLS), dtype))
    return jnp.concatenate(bufs, axis=0)


def unpack(buf, shapes):
    out, row = [], 0
    for shape in shapes:
        size = 1
        for s in shape:
            size *= s
        rows = _rows_of(size)
        out.append(buf[row:row + rows].reshape(-1)[:size].reshape(shape))
        row += rows
    return out


def _row(vec, width=None):
    vec = vec.astype(F32).reshape(1, -1)
    if width is not None and vec.shape[1] < width:
        vec = jnp.pad(vec, ((0, 0), (0, width - vec.shape[1])))
    return vec


ROW_TILE = 512
GDN_PROJ_WIDTH = 4 * GDN_WIDTH + MEM_WIDTH + LANES
GDN_Z_COL, GDN_QMEM_COL, GDN_AB_COL = 3, 4 * GDN_WIDTH // MEM_WIDTH, (4 * GDN_WIDTH + MEM_WIDTH) // LANES
FOX_QMEM_COL = FOX_WIDTH // MEM_WIDTH
KV_PAD_WIDTH = 2 * FOX_WIDTH + LANES


def rms_fwd(x, gain_row, out_dtype=BF16):
    return rowwise(f_rmsnorm, [x], [gain_row], [(x.shape[1], out_dtype)], tile=ROW_TILE, name="rms_fwd")[0]


def rms_bwd(x, gain_row, dy, dres=None):
    return rowwise_bwd(f_rmsnorm, [x], [gain_row], [dy], tile=ROW_TILE, name="rms_bwd", row_grads=[F32],
                       const_grads=[True], add=None if dres is None else {0: dres})


def _ffn_tiles(n_rows):
    return _pick(n_rows, (512, 256, 128)), _pick(FFN_HIDDEN, (1408, 256, 128))


def ffn_up_act(n, wgu):
    n_rows, d = n.shape
    tm, tn = _ffn_tiles(n_rows)
    nj = FFN_HIDDEN // tn

    def body(n_ref, wg_ref, wu_ref, gu_ref, act_ref):
        x = n_ref[...].astype(BF16)
        g = bdot.nn(x, wg_ref[...])
        u = bdot.nn(x, wu_ref[...])
        gu_ref[0] = g.astype(gu_ref.dtype)
        gu_ref[1] = u.astype(gu_ref.dtype)
        act_ref[...] = (_silu(g) * u).astype(act_ref.dtype)

    return _pcall(body, name="ffn_up_act",
                  out_shape=[jax.ShapeDtypeStruct((2, n_rows, FFN_HIDDEN), BF16), jax.ShapeDtypeStruct((n_rows, FFN_HIDDEN), BF16)],
                  grid=(nj, n_rows // tm),
                  in_specs=[pl.BlockSpec((tm, d), lambda j, i: (i, 0)), pl.BlockSpec((d, tn), lambda j, i: (0, j)),
                            pl.BlockSpec((d, tn), lambda j, i: (0, nj + j))],
                  out_specs=[pl.BlockSpec((2, tm, tn), lambda j, i: (0, i, j)), pl.BlockSpec((tm, tn), lambda j, i: (i, j))],
                  sem=("parallel", "parallel"))(n, wgu, wgu)


def ffn_down_dx_act(dh, wd, gu):
    n_rows, d = dh.shape
    tm, tn = _ffn_tiles(n_rows)

    def body(dh_ref, wd_ref, gu_ref, dgu_ref):
        dact = 0.5 * bdot.nt(dh_ref[...], wd_ref[...])
        gate, up = gu_ref[0].astype(F32), gu_ref[1].astype(F32)
        sg = _sigmoid(gate)
        dgu_ref[0] = (dact * up * (sg * (1.0 + gate * (1.0 - sg)))).astype(dgu_ref.dtype)
        dgu_ref[1] = (dact * (gate * sg)).astype(dgu_ref.dtype)

    blk = pl.BlockSpec((2, tm, tn), lambda j, i: (0, i, j))
    return _pcall(body, name="ffn_down_dx_act", out_shape=jax.ShapeDtypeStruct((2, n_rows, FFN_HIDDEN), BF16),
                  grid=(FFN_HIDDEN // tn, n_rows // tm),
                  in_specs=[pl.BlockSpec((tm, d), lambda j, i: (i, 0)), pl.BlockSpec((tn, d), lambda j, i: (j, 0)), blk],
                  out_specs=blk, sem=("parallel", "parallel"))(dh, wd, gu)


def ffn_fwd(h, gain_row, wgu, wd):
    n = rms_fwd(h, gain_row)
    gu, act = ffn_up_act(n, wgu)
    return mm(act, wd, scale=0.5, res=h, name="ffn_down"), (h, n, gu, act)


def ffn_bwd(dh, saved, gain_row, wgu, wd):
    h, n, gu, act = saved
    dgu = ffn_down_dx_act(dh, wd, gu)
    dwd = mm(act, dh, ta=True, scale=0.5, name="ffn_down_dw")
    dwgu = mm(n, dgu, ta=True, b_split=True, name="ffn_up_dw")
    dh, dgain = mm_rms_bwd(dgu, wgu, h, gain_row, dh, a_split=True, name="ffn_up_dx")
    return dh, dwgu, dwd, dgain


def gdn_fwd(proj, w8, a_row, dt_row, onorm_row):
    wide = [(GDN_WIDTH, F32)] * 5
    xc = conv_fwd(proj, w8, width=3 * GDN_WIDTH)
    q, k, v, gc, beta = rowwise(f_gdn_pre, [xc, (proj, LANES, GDN_AB_COL)], [a_row, dt_row], wide, tile=GDN_CHUNK,
                                name="gdn_pre_fwd")
    u, w, qk, kt, qh, inv = rowwise(f_gdn_intra, [q, k, v, gc, beta], [], wide + wide[:1], tile=GDN_CHUNK,
                                    name="gdn_intra_fwd")
    o, vn, sin = gdn_scan_fwd(u, w, qk, kt, qh, gc)
    main = rowwise(f_gdn_post, [o, (proj, GDN_WIDTH, GDN_Z_COL)], [onorm_row], [(GDN_WIDTH, BF16)], tile=ROW_TILE,
                   name="gdn_post_fwd")[0]
    return main, (xc, q, k, v, gc, beta, inv, w, qk, kt, qh, vn, sin, o)


def gdn_bwd(dmain, proj, saved, w8, a_row, dt_row, onorm_row):
    xc, q, k, v, gc, beta, inv, w, qk, kt, qh, vn, sin, o = saved
    do, dz, donorm = rowwise_bwd(f_gdn_post, [o, (proj, GDN_WIDTH, GDN_Z_COL)], [onorm_row], [dmain], tile=ROW_TILE,
                                 name="gdn_post_bwd", row_grads=[F32, BF16], const_grads=[True])
    du, dw, dqk, dkt, dqh, dgl = gdn_scan_bwd(do, w, qk, kt, qh, gc, vn, sin)
    dq, dk, dv, dgc, dbeta = gdn_intra_bwd(q, k, v, gc, beta, inv, (du, dw, dqk, dkt, dqh), dgl)
    dxc, dab, da, ddt = rowwise_bwd(f_gdn_pre, [xc, (proj, LANES, GDN_AB_COL)], [a_row, dt_row], [dq, dk, dv, dgc, dbeta],
                                    tile=GDN_CHUNK, name="gdn_pre_bwd", row_grads=[F32, BF16], const_grads=[True, True])
    dqkv, dw8 = conv_bwd(proj, w8, dxc, width=3 * GDN_WIDTH)
    return dqkv, dz, dab, dw8, da, ddt, donorm


def mem_fwd(q, kmem, vmem):
    return rowwise(f_mem_attn, [q], [kmem, vmem], [(MEM_WIDTH, BF16)], tile=ROW_TILE, name="mem_attn_fwd")[0]


def mem_bwd(q, kmem, vmem, dout):
    return rowwise_bwd(f_mem_attn, [q], [kmem, vmem], [dout], tile=ROW_TILE, name="mem_attn_bwd", row_grads=[BF16],
                       const_grads=[True, True])


def forward_backward(xs, mems, target, P):
    depth, n_a = 4, 2
    G = {}
    mem_gain = _row(P["mem_norm"])
    mem_n = rms_fwd(mems, mem_gain)
    h = xs
    saved = []
    shared = None
    for l in range(depth):
        h0 = h
        h1, s1 = ffn_fwd(h0, _row(P["ffn1_norm"][l]), P["ffn1_w_gate_up"][l], P["ffn1_w_down"][l])
        u = rms_fwd(h1, _row(P["mix_norm"][l]))
        kvm = mm(mem_n, P["mem_w_kv"][l], name="mem_kv")
        kmem, vmem = kvm[:, :MEM_WIDTH], kvm[:, MEM_WIDTH:]
        if l < n_a:
            gp = (P["conv8"][l], _row(P["gdn_A_log"][l], LANES), _row(P["gdn_dt_bias"][l], LANES), _row(P["gdn_out_norm"][l]))
            proj = mm(u, P["gdn_w_in_pad"][l], name="gdn_in")
            main, sm = gdn_fwd(proj, *gp)
            qm = (proj, MEM_WIDTH, GDN_QMEM_COL)
        else:
            proj = mm(u, P["fox_w_in"][l - n_a], out_dtype=BF16, name="fox_in")
            kv, ka, cb = shared
            qa = fox_augment(proj, cb, True)
            main, lse = fox_fwd(qa, kv, ka)
            sm = (main, lse, qa)
            qm = (proj, MEM_WIDTH, FOX_QMEM_COL)
        mo = mem_fwd(qm, kmem, vmem)
        cat = jnp.concatenate([main, mo], axis=1)
        h2 = mm(cat, P["w_out"][l], res=h1, name="mix_out")
        h3, s2 = ffn_fwd(h2, _row(P["ffn2_norm"][l]), P["ffn2_w_gate_up"][l], P["ffn2_w_down"][l])
        saved.append((s1, h1, u, kmem, vmem, proj, sm, qm, cat, s2))
        h = h3
        if l == n_a - 1:
            nkv = rms_fwd(h, _row(P["kv_norm"]))
            kv = mm(nkv, P["kv_w_pad"][:, :2 * FOX_WIDTH], out_dtype=BF16, name="fox_kv")
            f = mm(nkv, P["kv_w_pad"][:, 2 * FOX_WIDTH:], name="fox_f")
            bf_row = _row(P["kv_b_f"], LANES)
            cb = fox_gate_fwd(f, bf_row)
            shared = (kv, fox_augment(kv, cb, False), cb)
            kv_saved = (h, nkv, f, bf_row)

    part, dy = loss_head(h, _row(P["final_norm"]), target)
    dh, G["final_norm"] = rms_bwd(h, _row(P["final_norm"]), dy)

    per_layer = {n: [None] * depth for n in ("ffn1_norm", "ffn1_w_gate_up", "ffn1_w_down", "mix_norm", "ffn2_norm",
                                             "ffn2_w_gate_up", "ffn2_w_down", "w_out", "mem_w_kv")}
    gdn_g = {n: [None] * n_a for n in ("gdn_w_in_pad", "conv8", "gdn_A_log", "gdn_dt_bias", "gdn_out_norm")}
    fox_g = [None] * (depth - n_a)
    dmem_n = None
    dkv_acc = dcb_acc = None
    for l in reversed(range(depth)):
        s1, h1, u, kmem, vmem, proj, sm, qm, cat, s2 = saved[l]
        if l == n_a - 1:
            hk, nkv, f, bf_row = kv_saved
            dk, dv, dcrow = dkv_acc
            df, dbf = fox_gate_bwd(f, bf_row, dcrow, dcb_acc)
            dp = jnp.concatenate([dk.astype(BF16), dv.astype(BF16), df.astype(BF16)], axis=1)
            G["kv_w_pad"] = mm(nkv, dp, ta=True, name="fox_kv_dw")
            G["kv_b_f"] = dbf
            dh, G["kv_norm"] = mm_rms_bwd(dp, P["kv_w_pad"], hk, _row(P["kv_norm"]), dh, name="fox_kv_dx")
        dh, per_layer["ffn2_w_gate_up"][l], per_layer["ffn2_w_down"][l], per_layer["ffn2_norm"][l] = ffn_bwd(
            dh, s2, _row(P["ffn2_norm"][l]), P["ffn2_w_gate_up"][l], P["ffn2_w_down"][l])
        dcat = mm(dh, P["w_out"][l], tb=True, out_dtype=BF16, name="mix_out_dx")
        per_layer["w_out"][l] = mm(cat, dh, ta=True, name="mix_out_dw")
        dqm, dkm, dvm = mem_bwd(qm, kmem, vmem, (dcat, MEM_WIDTH, FOX_QMEM_COL))
        dkvm = jnp.concatenate([dkm, dvm], axis=1)
        per_layer["mem_w_kv"][l] = mm(mem_n, dkvm, ta=True, name="mem_kv_dw")
        dmem_n = mm(dkvm, P["mem_w_kv"][l], tb=True, res=dmem_n, name="mem_kv_dx")
        dmain = (dcat, GDN_WIDTH, 0)
        if l < n_a:
            gp = (P["conv8"][l], _row(P["gdn_A_log"][l], LANES), _row(P["gdn_dt_bias"][l], LANES), _row(P["gdn_out_norm"][l]))
            dqkv, dz, dab, gdn_g["conv8"][l], gdn_g["gdn_A_log"][l], gdn_g["gdn_dt_bias"][l], gdn_g["gdn_out_norm"][l] = gdn_bwd(
                dmain, proj, sm, *gp)
            dproj = jnp.concatenate([dqkv, dz, dqm, dab], axis=1)
            gdn_g["gdn_w_in_pad"][l] = mm(u, dproj, ta=True, name="gdn_in_dw")
            w_in = P["gdn_w_in_pad"][l]
        else:
            o, lse, qa = sm
            kv, ka, _ = shared
            dq, dcb_acc = fox_bwd_dq(qa, kv, ka, o, lse, dmain, dcb_acc)
            dkv_acc = fox_bwd_dkv(qa, kv, ka, o, lse, dmain, dkv_acc)
            dproj = jnp.concatenate([dq, dqm], axis=1)
            fox_g[l - n_a] = mm(u, dproj, ta=True, name="fox_in_dw")
            w_in = P["fox_w_in"][l - n_a]
        dh, per_layer["mix_norm"][l] = mm_rms_bwd(dproj, w_in, h1, _row(P["mix_norm"][l]), dh, name="mix_in_dx")
        dh, per_layer["ffn1_w_gate_up"][l], per_layer["ffn1_w_down"][l], per_layer["ffn1_norm"][l] = ffn_bwd(
            dh, s1, _row(P["ffn1_norm"][l]), P["ffn1_w_gate_up"][l], P["ffn1_w_down"][l])

    (G["mem_norm"],) = rowwise_bwd(f_rmsnorm, [mems], [mem_gain], [dmem_n], tile=ROW_TILE, name="mem_norm_bwd",
                                   row_grads=[None], const_grads=[True])
    for n, v in per_layer.items():
        G[n] = jnp.stack(v)
    for n, v in gdn_g.items():
        G[n] = jnp.stack(v)
    G["fox_w_in"] = jnp.stack(fox_g)
    return part, dh, G


_GDN_O0 = 4 * GDN_WIDTH
_GDN_O1 = _GDN_O0 + 2 * GDN_HEADS
_KV_WIDTH = 2 * FOX_WIDTH + FOX_HEADS


def derived_weights(gdn_w_in, gdn_conv, kv_w=None):
    zeros = jnp.zeros(gdn_w_in.shape[:-1] + (LANES - 2 * GDN_HEADS,), gdn_w_in.dtype)
    out = dict(
        gdn_w_in_pad=jnp.concatenate([gdn_w_in[..., :_GDN_O0], gdn_w_in[..., _GDN_O1:], gdn_w_in[..., _GDN_O0:_GDN_O1], zeros], axis=-1),
        conv8=jnp.pad(gdn_conv.astype(F32), ((0, 0), (0, 8 - CONV_WIDTH), (0, 0))))
    if kv_w is not None:
        out["kv_w_pad"] = jnp.pad(kv_w, ((0, 0), (0, KV_PAD_WIDTH - _KV_WIDTH)))
    return out


def reference_layout(G):
    gp = G["gdn_w_in_pad"]
    out = dict(G)
    out["gdn_w_in"] = jnp.concatenate([gp[..., :_GDN_O0], gp[..., _GDN_O0 + MEM_WIDTH:_GDN_O0 + MEM_WIDTH + 2 * GDN_HEADS],
                                       gp[..., _GDN_O0:_GDN_O0 + MEM_WIDTH]], axis=-1)
    out["gdn_conv"] = G["conv8"][:, :CONV_WIDTH]
    out["kv_w"] = G["kv_w_pad"][:, :_KV_WIDTH]
    out["gdn_A_log"] = G["gdn_A_log"][:, 0, :GDN_HEADS]
    out["gdn_dt_bias"] = G["gdn_dt_bias"][:, 0, :GDN_HEADS]
    out["gdn_out_norm"] = G["gdn_out_norm"][:, 0, :]
    out["kv_b_f"] = G["kv_b_f"][0, :FOX_HEADS]
    for n in ("ffn1_norm", "mix_norm", "ffn2_norm"):
        out[n] = G[n][:, 0, :]
    for n in ("mem_norm", "kv_norm", "final_norm"):
        out[n] = G[n][0]
    return {n: out[n] for n in WEIGHT_NAMES}


EXCHANGED = [("ffn1_w_gate_up", 2, 0), ("ffn1_w_down", 1, 0), ("ffn2_w_gate_up", 2, 0), ("ffn2_w_down", 1, 0),
             ("gdn_w_in", 2, 0), ("fox_w_in", 1, 0), ("w_out", 1, 0), ("mem_w_kv", 1, 0), ("kv_w", 0, 1)]
GDN_IN_SHARD = (4 * GDN_WIDTH + 2 * GDN_HEADS + MEM_WIDTH) // N_CHIPS
GDN_IN_SLOT = 896


def _slab(ref, axis_slices):
    idx = [slice(None)] * len(ref.shape)
    for axis, (start, size) in axis_slices.items():
        idx[axis] = pl.ds(start, size)
    return ref.at[tuple(idx)]


def _comm_multi(body, *, name, n_in, out_shapes, scratch):
    return pl.pallas_call(body, name=name, out_shape=out_shapes, in_specs=[_HBM] * n_in, out_specs=[_HBM] * len(out_shapes),
                          scratch_shapes=scratch, compiler_params=pltpu.CompilerParams(has_side_effects=True))


def gather_shards(shards, layout):
    n = len(shards)
    fulls = [tuple(d * (N_CHIPS if a == sa else 1) for a, d in enumerate(s.shape)) for s, (sa, _) in zip(shards, layout)]

    def body(*refs):
        ins, outs = refs[:n], refs[n:2 * n]
        send_sems, recv_sems, local_sems = refs[2 * n:]
        x, y, c = lax.axis_index("x"), lax.axis_index("y"), lax.axis_index("c")
        me, sibling = (x, y, c), (x, y, 1 - c)
        chips = [(1 - x, y), (x, 1 - y), (1 - x, 1 - y)]

        def region(w, px, py, pc):
            (sa, ha), shard = layout[w], shards[w].shape
            return _slab(outs[w], {sa: ((2 * px + py) * shard[sa], shard[sa]), ha: (pc * (shard[ha] // 2), shard[ha] // 2)})

        def my_half(w):
            ha, shard = layout[w][1], shards[w].shape
            return _slab(ins[w], {ha: (c * (shard[ha] // 2), shard[ha] // 2)})

        def copy(w, k, block, to, src=None):
            return pltpu.make_async_remote_copy(
                src_ref=region(w, *block) if src is None else src, dst_ref=region(w, *block),
                send_sem=send_sems.at[7 * w + k], recv_sem=recv_sems.at[7 * w + k], device_id=to, device_id_type=MESH)

        mine, first, passed = [], [], []
        for w in range(n):
            mine.append(pltpu.make_async_copy(my_half(w), region(w, *me), local_sems.at[w]))
            mine[w].start()
            first.append([copy(w, 0, me, sibling, src=my_half(w))]
                         + [copy(w, 1 + j, me, (*chip, c), src=my_half(w)) for j, chip in enumerate(chips)])
            for cp in first[w]:
                cp.start()
            passed.append([copy(w, 4 + j, (*chip, c), sibling) for j, chip in enumerate(chips)])
        for w in range(n):
            for j, chip in enumerate(chips):
                copy(w, 1 + j, (*chip, c), me).wait_recv()
                passed[w][j].start()
        for w in range(n):
            copy(w, 0, sibling, me).wait_recv()
            for j, chip in enumerate(chips):
                copy(w, 4 + j, (*chip, 1 - c), me).wait_recv()
        for w in range(n):
            for cp in first[w] + passed[w]:
                cp.wait_send()
            mine[w].wait()

    return _comm_multi(body, name="gather_shards", n_in=n,
                       out_shapes=[jax.ShapeDtypeStruct(f, s.dtype) for f, s in zip(fulls, shards)],
                       scratch=[pltpu.SemaphoreType.DMA((7 * n,)), pltpu.SemaphoreType.DMA((7 * n,)),
                                pltpu.SemaphoreType.DMA((n,))])(*shards)


def swap_other_halves(arrays, layout):
    n = len(arrays)
    halves = [tuple(d // 2 if a == ha else d for a, d in enumerate(g.shape)) for g, (_, ha) in zip(arrays, layout)]

    def body(*refs):
        ins, outs = refs[:n], refs[n:2 * n]
        send_sems, recv_sems = refs[2 * n:]
        x, y, c = lax.axis_index("x"), lax.axis_index("y"), lax.axis_index("c")
        copies = []
        for w in range(n):
            ha, size = layout[w][1], halves[w][layout[w][1]]
            copies.append(pltpu.make_async_remote_copy(
                src_ref=_slab(ins[w], {ha: ((1 - c) * size, size)}), dst_ref=outs[w], send_sem=send_sems.at[w],
                recv_sem=recv_sems.at[w], device_id=(x, y, 1 - c), device_id_type=MESH))
            copies[w].start()
        for cp in copies:
            cp.wait()

    return _comm_multi(body, name="grad_pair_swap", n_in=n,
                       out_shapes=[jax.ShapeDtypeStruct(h, g.dtype) for h, g in zip(halves, arrays)],
                       scratch=[pltpu.SemaphoreType.DMA((n,)), pltpu.SemaphoreType.DMA((n,))])(*arrays)


def scatter_to_chips(arrays, layout):
    n = len(arrays)
    slabs = [tuple(d // N_CHIPS if a == sa else d for a, d in enumerate(p.shape)) for p, (sa, _) in zip(arrays, layout)]

    def body(*refs):
        ins, outs = refs[:n], refs[n:2 * n]
        send_sems, recv_sems, local_sems = refs[2 * n:]
        x, y, c = lax.axis_index("x"), lax.axis_index("y"), lax.axis_index("c")
        me = 2 * x + y

        def slab(w, k):
            sa, size = layout[w][0], slabs[w][layout[w][0]]
            return _slab(ins[w], {sa: (k * size, size)})

        local, copies = [], []
        for w in range(n):
            local.append(pltpu.make_async_copy(slab(w, me), outs[w].at[me], local_sems.at[w]))
            local[w].start()
            for j, (px, py) in enumerate([(1 - x, y), (x, 1 - y), (1 - x, 1 - y)]):
                copies.append(pltpu.make_async_remote_copy(
                    src_ref=slab(w, 2 * px + py), dst_ref=outs[w].at[me], send_sem=send_sems.at[3 * w + j],
                    recv_sem=recv_sems.at[3 * w + j], device_id=(px, py, c), device_id_type=MESH))
                copies[-1].start()
        for cp in copies:
            cp.wait()
        for cp in local:
            cp.wait()

    return _comm_multi(body, name="grad_all_to_all", n_in=n,
                       out_shapes=[jax.ShapeDtypeStruct((N_CHIPS,) + s, p.dtype) for s, p in zip(slabs, arrays)],
                       scratch=[pltpu.SemaphoreType.DMA((3 * n,)), pltpu.SemaphoreType.DMA((3 * n,)),
                                pltpu.SemaphoreType.DMA((n,))])(*arrays)


def swap_with_sibling(arrays):
    n = len(arrays)

    def body(*refs):
        ins, outs = refs[:n], refs[n:2 * n]
        send_sems, recv_sems = refs[2 * n:]
        x, y, c = lax.axis_index("x"), lax.axis_index("y"), lax.axis_index("c")
        copies = [pltpu.make_async_remote_copy(src_ref=ins[w], dst_ref=outs[w], send_sem=send_sems.at[w], recv_sem=recv_sems.at[w],
                                               device_id=(x, y, 1 - c), device_id_type=MESH) for w in range(n)]
        for cp in copies:
            cp.start()
        for cp in copies:
            cp.wait()

    return _comm_multi(body, name="grad_half_swap", n_in=n, out_shapes=[jax.ShapeDtypeStruct(a.shape, a.dtype) for a in arrays],
                       scratch=[pltpu.SemaphoreType.DMA((n,)), pltpu.SemaphoreType.DMA((n,))])(*arrays)


def join_halves(mine, other, half_axis, c_arr, *, name):
    a0, a1, a2 = mine.shape
    tile = _row_tile(a1, a2)

    def body(c_ref, m_ref, o_ref, out_ref):
        for half in range(2):
            @pl.when(c_ref[0] == half)
            def _(half=half):
                out_ref[half] = m_ref[...]
                out_ref[1 - half] = o_ref[...]

    blk = pl.BlockSpec((None, tile, a2), lambda i, j, c_ref: (i, j, 0))
    if half_axis == 0:
        out_shape, out_blk = (2, a0, a1, a2), pl.BlockSpec((2, None, tile, a2), lambda i, j, c_ref: (0, i, j, 0))
    else:
        out_shape, out_blk = (a0, 2, a1, a2), pl.BlockSpec((None, 2, tile, a2), lambda i, j, c_ref: (i, 0, j, 0))
    spec = pltpu.PrefetchScalarGridSpec(num_scalar_prefetch=1, grid=(a0, a1 // tile), in_specs=[blk, blk], out_specs=out_blk)
    out = pl.pallas_call(body, name=name, out_shape=jax.ShapeDtypeStruct(out_shape, mine.dtype), grid_spec=spec,
                         compiler_params=pltpu.CompilerParams(vmem_limit_bytes=VMEM_LIMIT_BYTES,
                                                              dimension_semantics=("parallel", "parallel")))(c_arr, mine, other)
    return out.reshape((2 * a0, a1, a2) if half_axis == 0 else (a0, 2 * a1, a2))


def _row_tile(rows, cols, itemsize=4, budget=2 * 1024 * 1024):
    for t in (1024, 512, 256, 128, 64, 32, 16):
        if rows % t == 0 and t * cols * itemsize <= budget:
            return t
    return rows


def add_own_half(full, recv, half_axis, c_arr, *, name):
    a0, a1, a2 = recv.shape
    tile = _row_tile(a1, a2)

    def body(c_ref, f_ref, r_ref, o_ref):
        o_ref[...] = (f_ref[...] + r_ref[...]).astype(o_ref.dtype)

    if half_axis == 0:
        f_spec = pl.BlockSpec((None, tile, a2), lambda i, j, c_ref: (c_ref[0] * a0 + i, j, 0))
    else:
        f_spec = pl.BlockSpec((None, tile, a2), lambda i, j, c_ref: (i, c_ref[0] * (a1 // tile) + j, 0))
    blk = pl.BlockSpec((None, tile, a2), lambda i, j, c_ref: (i, j, 0))
    spec = pltpu.PrefetchScalarGridSpec(num_scalar_prefetch=1, grid=(a0, a1 // tile), in_specs=[f_spec, blk], out_specs=blk)
    return pl.pallas_call(body, name=name, out_shape=jax.ShapeDtypeStruct(recv.shape, BF16), grid_spec=spec,
                          compiler_params=pltpu.CompilerParams(vmem_limit_bytes=VMEM_LIMIT_BYTES,
                                                               dimension_semantics=("parallel", "parallel")))(c_arr, full, recv)


def sum_slots(q, *, name):
    _, a0, a1, a2 = q.shape
    tile = _row_tile(a1, a2, budget=1024 * 1024)

    def body(q_ref, o_ref):
        total = q_ref[0].astype(F32)
        for k in range(1, N_CHIPS):
            total = total + q_ref[k].astype(F32)
        o_ref[...] = total

    return _pcall(body, name=name, out_shape=jax.ShapeDtypeStruct((a0, a1, a2), F32), grid=(a0, a1 // tile),
                  in_specs=[pl.BlockSpec((N_CHIPS, None, tile, a2), lambda i, j: (0, i, j, 0))],
                  out_specs=pl.BlockSpec((None, tile, a2), lambda i, j: (i, j, 0)), sem=("parallel", "parallel"))(q)


def gather_weights(W):
    shards = []
    for name, _, _ in EXCHANGED:
        w = W[name].astype(BF16)
        if name == "gdn_w_in":
            w = jnp.pad(w, ((0, 0), (0, 0), (0, GDN_IN_SLOT - GDN_IN_SHARD)))
        if name == "kv_w":
            w = jnp.pad(w, ((0, 0), (0, KV_PAD_WIDTH - _KV_WIDTH)))[None]
        shards.append(w)
    fulls = dict(zip([n for n, _, _ in EXCHANGED], gather_shards(shards, [(sa, ha) for _, sa, ha in EXCHANGED])))
    slots = fulls["gdn_w_in"]
    fulls["gdn_w_in"] = jnp.concatenate([slots[..., k * GDN_IN_SLOT:k * GDN_IN_SLOT + GDN_IN_SHARD] for k in range(N_CHIPS)], axis=-1)
    fulls["kv_w_pad"] = fulls.pop("kv_w").reshape(-1, KV_PAD_WIDTH)
    conv = pack([W["gdn_conv"]], F32, row_multiple=8)
    conv_all = all_gather8(conv, name="gather_conv").reshape(N_DEV, conv.shape[0], PACK_COLS)
    fulls["gdn_conv"] = jnp.concatenate([unpack(conv_all[2 * k], [W["gdn_conv"].shape])[0] for k in range(N_CHIPS)], axis=-1)
    return fulls


def reduce_gradients(G):
    layout = [(sa, ha) for _, sa, ha in EXCHANGED]
    c_arr = lax.axis_index("c").astype(jnp.int32).reshape(1)
    received = swap_other_halves(G, layout)
    pairs = [add_own_half(g, r, ha, c_arr, name="grad_pair_sum") for g, r, (_, ha) in zip(G, received, layout)]
    slots = scatter_to_chips(pairs, layout)
    halves = [sum_slots(q, name="grad_chip_sum") for q in slots]
    others = swap_with_sibling(halves)
    return [join_halves(h, o, ha, c_arr, name="grad_join_halves") for h, o, (_, ha) in zip(halves, others, layout)]


def allreduce_small(G, names):
    packed = pack([G[n] for n in names], F32, row_multiple=8)
    gathered = all_gather8(packed, name="gather_small_grads").reshape(N_DEV, packed.shape[0], PACK_COLS)
    total = sum_leading(gathered, name="sum_small_grads")
    return dict(zip(names, unpack(total, [G[n].shape for n in names])))


def kernel(x, mem, *rest):
    n_w = len(WEIGHT_NAMES)
    W = dict(zip(WEIGHT_NAMES, rest[:n_w]))
    target = rest[n_w]
    M = dict(zip(WEIGHT_NAMES, rest[n_w + 1:2 * n_w + 1]))
    V = dict(zip(WEIGHT_NAMES, rest[2 * n_w + 1:3 * n_w + 1]))

    full = gather_weights(W)
    P = {n: W[n] for n in REPLICATED}
    P.update({n: full[n] for n in ("ffn1_w_gate_up", "ffn1_w_down", "ffn2_w_gate_up", "ffn2_w_down", "fox_w_in", "w_out",
                                   "mem_w_kv", "kv_w_pad")})
    derived = derived_weights(full["gdn_w_in"], full["gdn_conv"])
    P.update(gdn_w_in_pad=derived["gdn_w_in_pad"], conv8=derived["conv8"])

    part, dx, G = forward_backward(x[0], mem[0], target[0], P)
    loss = lax.psum(0.5 / x.shape[-1] * jnp.sum(part), ("x", "y", "c"))

    ref = reference_layout(G)
    exchange = {n: ref[n] for n, _, _ in EXCHANGED}
    exchange["gdn_w_in"] = jnp.concatenate(
        [jnp.pad(ref["gdn_w_in"][..., k * GDN_IN_SHARD:(k + 1) * GDN_IN_SHARD], ((0, 0), (0, 0), (0, GDN_IN_SLOT - GDN_IN_SHARD)))
         for k in range(N_CHIPS)], axis=-1)
    exchange["kv_w"] = G["kv_w_pad"].reshape(N_CHIPS, -1, KV_PAD_WIDTH)
    shards = dict(zip([n for n, _, _ in EXCHANGED], reduce_gradients([exchange[n] for n, _, _ in EXCHANGED])))
    shards["gdn_w_in"] = shards["gdn_w_in"][..., :GDN_IN_SHARD]
    shards["kv_w"] = shards["kv_w"][0, :, :_KV_WIDTH]
    grads = allreduce_small(ref, REPLICATED + ["gdn_conv"])
    conv_cols = W["gdn_conv"].shape[-1]
    chip = 2 * lax.axis_index("x") + lax.axis_index("y")
    grads["gdn_conv"] = lax.dynamic_slice_in_dim(grads["gdn_conv"], chip * conv_cols, conv_cols, axis=2)
    grads.update(shards)

    outs = {n: adamw(W[n], grads[n], M[n], V[n], name="adamw_" + n) for n in WEIGHT_NAMES}
    return (loss, dx[None], *[grads[n] for n in WEIGHT_NAMES], *[outs[n][0] for n in WEIGHT_NAMES],
            *[outs[n][1] for n in WEIGHT_NAMES], *[outs[n][2] for n in WEIGHT_NAMES])
```

```python
import jax
import jax.numpy as jnp
from jax import lax
from jax.experimental import pallas as pl
from jax.experimental.pallas import tpu as pltpu

F32, BF16 = jnp.float32, jnp.bfloat16
HI = lax.Precision.HIGHEST
MESH = pl.DeviceIdType.MESH

VMEM_LIMIT_BYTES = 48 * 1024 * 1024
LANES = 128
EPS = 1e-6
NEG_INF = -1e30

D_MODEL = 1024
HEAD_DIM = 128
GDN_HEADS = 6
GDN_WIDTH = GDN_HEADS * HEAD_DIM
FOX_HEADS = 6
FOX_WIDTH = FOX_HEADS * HEAD_DIM
MEM_HEADS = 4
MEM_HEAD_DIM = 64
MEM_WIDTH = MEM_HEADS * MEM_HEAD_DIM
FFN_HIDDEN = 2816
CONV_WIDTH = 4
GDN_CHUNK = 128
N_CHIPS = 4
N_DEV = 8

ADAM_LR, ADAM_B1, ADAM_B2, ADAM_EPS, ADAM_WD, ADAM_STEP = 0.001, 0.9, 0.999, 1e-08, 0.01, 10


def _pcall(body, *, name, out_shape, grid=(), in_specs=None, out_specs=None, scratch=(), sem=None):
    params = dict(vmem_limit_bytes=VMEM_LIMIT_BYTES)
    if sem is not None:
        params["dimension_semantics"] = sem
    kw = dict(grid=grid, in_specs=in_specs, out_specs=out_specs) if grid else {}
    return pl.pallas_call(body, name=name, out_shape=out_shape, scratch_shapes=list(scratch),
                          compiler_params=pltpu.CompilerParams(**params), **kw)


def _pick(n, cands):
    for c in cands:
        if n % c == 0:
            return c
    return n


def _make_dot(dtype, precision):
    def raw(a, b, dims):
        return lax.dot_general(a.astype(dtype), b.astype(dtype), (dims, ((), ())),
                               precision=precision, preferred_element_type=F32)

    @jax.custom_vjp
    def dot(a, b):
        return raw(a, b, ((1,), (0,)))

    def fwd(a, b):
        return dot(a, b), (a, b)

    def bwd(resid, ct):
        a, b = resid
        return raw(ct, b, ((1,), (1,))).astype(a.dtype), raw(a, ct, ((0,), (0,))).astype(b.dtype)

    dot.defvjp(fwd, bwd)
    dot.nn = lambda a, b: raw(a, b, ((1,), (0,)))
    dot.nt = lambda a, b: raw(a, b, ((1,), (1,)))
    dot.tn = lambda a, b: raw(a, b, ((0,), (0,)))
    return dot


bdot = _make_dot(BF16, None)
fdot = _make_dot(F32, HI)
idot = _make_dot(F32, lax.Precision.HIGH)
sdot = idot


def _sigmoid(x):
    return 0.5 * jnp.tanh(0.5 * x) + 0.5


def _silu(x):
    return x * _sigmoid(x)


def _softplus(x):
    return jnp.maximum(x, 0.0) + jnp.log(1.0 + jnp.exp(-jnp.abs(x)))


def _log_sigmoid(x):
    return -_softplus(-x)


def _iota2(shape, dim):
    return lax.broadcasted_iota(jnp.int32, shape, dim)


def mm(a, b, *, ta=False, tb=False, a_split=False, b_split=False, out_dtype=F32, scale=1.0, res=None, name):
    assert not (a_split and ta) and not (b_split and tb)
    (K, M) = a.shape if ta else ((2 * a.shape[2], a.shape[1]) if a_split else a.shape[::-1])
    (N, Kb) = b.shape if tb else ((2 * b.shape[2], b.shape[1]) if b_split else b.shape[::-1])
    assert K == Kb, (a.shape, b.shape, ta, tb)
    tm = _pick(M, (1024, 1408, 512, 256, 128))
    tn = _pick(N, (1024, 1408, 1152, 1664, 768, 512, 384, 256, 128))
    tk = _pick(K, (1024, 512, 256, 128)) if ta else _pick(K, (1024, 1408, 1152, 1664, 512, 256, 128))
    assert not a_split or (K // 2) % tk == 0
    assert not b_split or (N // 2) % tn == 0
    nk = K // tk
    dims = (((0 if ta else 1,), (1 if tb else 0,)), ((), ()))

    def body(a_ref, b_ref, *rest):
        o_ref, acc = rest[-2], rest[-1]
        k = pl.program_id(2)

        @pl.when(k == 0)
        def _():
            acc[...] = jnp.zeros_like(acc)

        acc[...] += lax.dot_general(a_ref[...].astype(BF16), b_ref[...].astype(BF16), dims,
                                    preferred_element_type=F32)

        @pl.when(k == nk - 1)
        def _():
            out = acc[...] * scale
            if res is not None:
                out = out + rest[0][...].astype(F32)
            o_ref[...] = out.astype(o_ref.dtype)

    a_spec = pl.BlockSpec((tk, tm), lambda i, j, k: (k, i)) if ta else pl.BlockSpec((tm, tk), lambda i, j, k: (i, k))
    b_spec = pl.BlockSpec((tn, tk), lambda i, j, k: (j, k)) if tb else pl.BlockSpec((tk, tn), lambda i, j, k: (k, j))
    if a_split:
        per_half = K // 2 // tk
        a_spec = pl.BlockSpec((None, tm, tk), lambda i, j, k: (k // per_half, i, k % per_half))
    if b_split:
        per_half = N // 2 // tn
        b_spec = pl.BlockSpec((None, tk, tn), lambda i, j, k: (j // per_half, k, j % per_half))
    o_spec = pl.BlockSpec((tm, tn), lambda i, j, k: (i, j))
    ins, specs = [a, b], [a_spec, b_spec]
    if res is not None:
        ins.append(res)
        specs.append(o_spec)
    return _pcall(body, name=name, out_shape=jax.ShapeDtypeStruct((M, N), out_dtype),
                  grid=(M // tm, N // tn, nk), in_specs=specs, out_specs=o_spec,
                  scratch=[pltpu.VMEM((tm, tn), F32)], sem=("parallel", "parallel", "arbitrary"))(*ins)


def mm_rms_bwd(a, b, x, gain_row, dres, *, a_split=False, name):
    (K, M) = (2 * a.shape[2], a.shape[1]) if a_split else a.shape[::-1]
    D = b.shape[0]
    assert b.shape[1] == K and x.shape == (M, D)
    tm = _pick(M, (1024, 512, 256, 128))
    rows = _pick(tm, (256, 128))
    tk = _pick(K, (1024, 1408, 1152, 1664, 512, 256, 128))
    assert not a_split or (K // 2) % tk == 0
    nk = K // tk

    def body(a_ref, b_ref, x_ref, g_ref, r_ref, o_ref, dg_ref, acc):
        i, k = pl.program_id(0), pl.program_id(1)

        @pl.when(k == 0)
        def _():
            acc[...] = jnp.zeros_like(acc)

        acc[...] += bdot.nt(a_ref[...], b_ref[...])

        @pl.when(k == nk - 1)
        def _():
            dg = jnp.zeros((1, D), F32)
            for s in range(tm // rows):
                sl = slice(s * rows, (s + 1) * rows)
                xv = x_ref[sl, :]
                rstd = lax.rsqrt(jnp.mean(xv * xv, axis=-1, keepdims=True) + EPS)
                xh = xv * rstd
                dn = acc[sl, :]
                dy = dn * g_ref[...]
                o_ref[sl, :] = (dy - xh * jnp.mean(dy * xh, axis=-1, keepdims=True)) * rstd + r_ref[sl, :]
                dg = dg + jnp.sum(dn * xh, axis=0, keepdims=True)

            @pl.when(i == 0)
            def _():
                dg_ref[...] = dg

            @pl.when(i > 0)
            def _():
                dg_ref[...] += dg

    a_spec = pl.BlockSpec((tm, tk), lambda i, k: (i, k))
    if a_split:
        per_half = K // 2 // tk
        a_spec = pl.BlockSpec((None, tm, tk), lambda i, k: (k // per_half, i, k % per_half))
    row = pl.BlockSpec((tm, D), lambda i, k: (i, 0))
    one = pl.BlockSpec((1, D), lambda i, k: (0, 0))
    return _pcall(body, name=name, out_shape=[jax.ShapeDtypeStruct((M, D), F32), jax.ShapeDtypeStruct((1, D), F32)],
                  grid=(M // tm, nk), in_specs=[a_spec, pl.BlockSpec((D, tk), lambda i, k: (0, k)), row, one, row],
                  out_specs=[row, one], scratch=[pltpu.VMEM((tm, D), F32)], sem=("arbitrary", "arbitrary"))(a, b, x, gain_row, dres)


def _row_spec(r, tile):
    if isinstance(r, tuple):
        arr, width, col = r
        return arr, pl.BlockSpec((tile, width), lambda i, col=col: (i, col))
    return r, pl.BlockSpec((tile, r.shape[1]), lambda i: (i, 0))


def _const_spec(c):
    return pl.BlockSpec(c.shape, lambda i: (0,) * c.ndim)


def rowwise(fn, rows, consts, outs, *, tile, name):
    arrs, specs = zip(*[_row_spec(r, tile) for r in rows])
    n_rows = arrs[0].shape[0]
    tile = min(tile, n_rows)
    n_in = len(rows) + len(consts)

    def body(*refs):
        res = fn(*[r[...] for r in refs[:n_in]])
        for o_ref, o in zip(refs[n_in:], res):
            o_ref[...] = o.astype(o_ref.dtype)

    arrs, specs = zip(*[_row_spec(r, tile) for r in rows])
    return _pcall(body, name=name,
                  out_shape=[jax.ShapeDtypeStruct((n_rows, w), dt) for w, dt in outs],
                  grid=(n_rows // tile,),
                  in_specs=list(specs) + [_const_spec(c) for c in consts],
                  out_specs=[pl.BlockSpec((tile, w), lambda i: (i, 0)) for w, _ in outs],
                  sem=("parallel",))(*arrs, *consts)


def rowwise_bwd(fn, rows, consts, cts, *, tile, name, row_grads, const_grads, add=None):
    arrs, _ = zip(*[_row_spec(r, tile) for r in rows])
    n_rows = arrs[0].shape[0]
    tile = min(tile, n_rows)
    arrs, specs = zip(*[_row_spec(r, tile) for r in rows])
    ct_arrs, ct_specs = zip(*[_row_spec(c, tile) for c in cts])
    add = add or {}
    add_idx = sorted(add)
    add_arrs, add_specs = (zip(*[_row_spec(add[i], tile) for i in add_idx]) if add_idx else ((), ()))
    nr, nc, nct, na = len(rows), len(consts), len(cts), len(add_idx)
    want_rows = [i for i, d in enumerate(row_grads) if d is not None]
    want_consts = [i for i, w in enumerate(const_grads) if w]

    def body(*refs):
        row_v = [r[...] for r in refs[:nr]]
        const_v = [r[...] for r in refs[nr:nr + nc]]
        ct_v = [r[...] for r in refs[nr + nc:nr + nc + nct]]
        add_v = {i: refs[nr + nc + nct + j][...] for j, i in enumerate(add_idx)}
        out_refs = refs[nr + nc + nct + na:]
        res, vjp = jax.vjp(fn, *row_v, *const_v)
        grads = vjp(tuple(c.astype(o.dtype) for c, o in zip(ct_v, res)))
        for o_ref, i in zip(out_refs, want_rows):
            g = grads[i].astype(F32)
            if i in add_v:
                g = g + add_v[i].astype(F32)
            o_ref[...] = g.astype(o_ref.dtype)
        first = pl.program_id(0) == 0
        for o_ref, i in zip(out_refs[len(want_rows):], want_consts):
            g = grads[nr + i].astype(F32)

            @pl.when(first)
            def _(o_ref=o_ref, g=g):
                o_ref[...] = g

            @pl.when(jnp.logical_not(first))
            def _(o_ref=o_ref, g=g):
                o_ref[...] += g

    def width(r):
        return r[1] if isinstance(r, tuple) else r.shape[1]

    out_shape = [jax.ShapeDtypeStruct((n_rows, width(rows[i])), row_grads[i]) for i in want_rows]
    out_shape += [jax.ShapeDtypeStruct(consts[i].shape, F32) for i in want_consts]
    out_specs = [pl.BlockSpec((tile, width(rows[i])), lambda i_: (i_, 0)) for i in want_rows]
    out_specs += [_const_spec(consts[i]) for i in want_consts]
    return _pcall(body, name=name, out_shape=out_shape, grid=(n_rows // tile,),
                  in_specs=list(specs) + [_const_spec(c) for c in consts] + list(ct_specs) + list(add_specs),
                  out_specs=out_specs, sem=("arbitrary",))(*arrs, *consts, *ct_arrs, *add_arrs)


def f_rmsnorm(x, g):
    x = x.astype(F32)
    return (x * lax.rsqrt(jnp.mean(x * x, axis=-1, keepdims=True) + EPS) * g,)


def _head_sel(first_lane):
    r, c = _iota2((LANES, GDN_WIDTH), 0), _iota2((LANES, GDN_WIDTH), 1)
    return (r == c // HEAD_DIM + first_lane).astype(F32)


def _tri(n, strict=False):
    r, c = _iota2((n, n), 0), _iota2((n, n), 1)
    return r > c if strict else r >= c


def f_gdn_pre(xc, ab, a_log, dt_bias):
    s = _silu(xc.astype(F32))
    qs, ks = [], []
    for h in range(GDN_HEADS):
        qh = s[:, h * HEAD_DIM:(h + 1) * HEAD_DIM]
        kh = s[:, GDN_WIDTH + h * HEAD_DIM:GDN_WIDTH + (h + 1) * HEAD_DIM]
        qs.append(qh * lax.rsqrt(jnp.sum(qh * qh, axis=-1, keepdims=True) + EPS) * (HEAD_DIM ** -0.5))
        ks.append(kh * lax.rsqrt(jnp.sum(kh * kh, axis=-1, keepdims=True) + EPS))
    q, k = jnp.concatenate(qs, axis=1), jnp.concatenate(ks, axis=1)
    v = s[:, 2 * GDN_WIDTH:]
    ab = ab.astype(F32)
    g = -jnp.exp(a_log) * _softplus(ab + dt_bias)
    gc = fdot(_tri(GDN_CHUNK).astype(F32), fdot(g, _head_sel(0)))
    beta = fdot(_sigmoid(ab), _head_sel(GDN_HEADS))
    return q, k, v, gc, beta


def _unit_lower_inverses(neg_lowers):
    C = neg_lowers[0].shape[0]
    eye = (_iota2((C, C), 0) == _iota2((C, C), 1)).astype(F32)
    invs = [eye + n for n in neg_lowers]
    powers = list(neg_lowers)
    for _ in range(6):
        powers = [idot.nn(p, p) for p in powers]
        invs = [inv + idot.nn(p, inv) for p, inv in zip(powers, invs)]
    return invs


@jax.custom_vjp
def _solve_with_inverse(inv, neg_lower, rhs):
    return idot.nn(inv, rhs)


def _solve_fwd(inv, neg_lower, rhs):
    x = idot.nn(inv, rhs)
    return x, (inv, x)


def _solve_bwd(resid, ct):
    inv, x = resid
    d_rhs = idot.tn(inv, ct)
    return jnp.zeros_like(inv), idot.nt(d_rhs, x), d_rhs


_solve_with_inverse.defvjp(_solve_fwd, _solve_bwd)


def f_gdn_intra(q, k, v, gc, beta, inv=None):
    C = GDN_CHUNK
    causal, strict = _tri(C), _tri(C, strict=True)
    is_last = _iota2((C, HEAD_DIM), 0) == C - 1
    heads = range(GDN_HEADS)
    sls = [slice(h * HEAD_DIM, (h + 1) * HEAD_DIM) for h in heads]
    qs, ks, vs, gs, bs = ([a[:, sl] for sl in sls] for a in (q, k, v, gc, beta))
    decays = [jnp.where(causal, jnp.exp(jnp.where(causal, g - g.T, 0.0)), 0.0) for g in gs]
    kbs = [kh * bh for kh, bh in zip(ks, bs)]
    kts = [kh.T for kh in ks]
    neg_lowers = [jnp.where(strict, -(idot(kb, kt) * d), 0.0) for kb, kt, d in zip(kbs, kts, decays)]
    qks = [jnp.where(causal, idot(qh, kt) * d, 0.0) for qh, kt, d in zip(qs, kts, decays)]
    rhss = [jnp.concatenate([vh * bh, kb * jnp.exp(g)], axis=1) for vh, bh, kb, g in zip(vs, bs, kbs, gs)]
    if inv is None:
        invs = _unit_lower_inverses(neg_lowers)
        sols = [idot.nn(m, r) for m, r in zip(invs, rhss)]
    else:
        sols = [_solve_with_inverse(inv[:, sl], n, r) for sl, n, r in zip(sls, neg_lowers, rhss)]
    g_lasts = [jnp.sum(jnp.where(is_last, g, 0.0), axis=0, keepdims=True) for g in gs]
    outs = [[s[:, :HEAD_DIM] for s in sols], [s[:, HEAD_DIM:] for s in sols], qks,
            [kh * jnp.exp(gl - g) for kh, gl, g in zip(ks, g_lasts, gs)], [qh * jnp.exp(g) for qh, g in zip(qs, gs)]]
    if inv is None:
        outs.append(invs)
    return tuple(jnp.concatenate(o, axis=1) for o in outs)


def f_gdn_post(o, z, gain):
    z = z.astype(F32)
    parts = []
    for h in range(GDN_HEADS):
        oh = o[:, h * HEAD_DIM:(h + 1) * HEAD_DIM]
        parts.append(oh * lax.rsqrt(jnp.mean(oh * oh, axis=-1, keepdims=True) + EPS) * gain)
    return (jnp.concatenate(parts, axis=1) * _silu(z),)


def f_mem_attn(q, k, v):
    q = q.astype(F32)
    lane_head = _iota2((1, MEM_WIDTH), 1) // MEM_HEAD_DIM
    kt = k.astype(F32).T
    masks = [(lane_head == h).astype(F32) for h in range(MEM_HEADS)]
    logits = [bdot(q * mask, kt) * (MEM_HEAD_DIM ** -0.5) for mask in masks]
    ps = [jnp.exp(s - jnp.max(s, axis=-1, keepdims=True)) for s in logits]
    ps = [p / jnp.sum(p, axis=-1, keepdims=True) for p in ps]
    outs = [bdot(p, v) * mask for p, mask in zip(ps, masks)]
    return ((outs[0] + outs[1]) + (outs[2] + outs[3]),)


def f_loss(y, t):
    d = y - t
    return (d * d,)


def conv_fwd(proj, w8, *, width, tile=512):
    n_rows = proj.shape[0]
    tile = min(tile, n_rows)

    def body(x_ref, halo_ref, w_ref, o_ref):
        i = pl.program_id(0)
        halo = jnp.where(i > 0, halo_ref[...].astype(F32), 0.0)
        xs = jnp.concatenate([halo, x_ref[...].astype(F32)], axis=0)
        acc = xs[8:] * w_ref[3:4, :]
        for j in range(CONV_WIDTH - 1):
            acc = acc + pltpu.roll(xs, CONV_WIDTH - 1 - j, 0)[8:] * w_ref[j:j + 1, :]
        o_ref[...] = acc

    return _pcall(body, name="gdn_conv_fwd", out_shape=jax.ShapeDtypeStruct((n_rows, width), F32),
                  grid=(n_rows // tile,),
                  in_specs=[pl.BlockSpec((tile, width), lambda i: (i, 0)),
                            pl.BlockSpec((8, width), lambda i: (jnp.maximum(i * (tile // 8) - 1, 0), 0)),
                            pl.BlockSpec((8, width), lambda i: (0, 0))],
                  out_specs=pl.BlockSpec((tile, width), lambda i: (i, 0)), sem=("parallel",))(proj, proj, w8)


def conv_bwd(proj, w8, dy, *, width, tile=512):
    n_rows = proj.shape[0]
    tile = min(tile, n_rows)
    n = n_rows // tile

    def body(x_ref, xhalo_ref, w_ref, dy_ref, dyhalo_ref, dx_ref, dw_ref):
        i = pl.program_id(0)
        dy = dy_ref[...]
        after = jnp.where(i < n - 1, dyhalo_ref[...], 0.0)
        ds = jnp.concatenate([dy, after], axis=0)
        dx = dy * w_ref[3:4, :]
        for j in range(CONV_WIDTH - 1):
            shift = CONV_WIDTH - 1 - j
            dx = dx + pltpu.roll(ds, tile + 8 - shift, 0)[:tile] * w_ref[j:j + 1, :]
        dx_ref[...] = dx.astype(dx_ref.dtype)
        halo = jnp.where(i > 0, xhalo_ref[...].astype(F32), 0.0)
        xs = jnp.concatenate([halo, x_ref[...].astype(F32)], axis=0)
        rows = [jnp.sum(dy * pltpu.roll(xs, CONV_WIDTH - 1 - j, 0)[8:], axis=0, keepdims=True)
                for j in range(CONV_WIDTH - 1)]
        rows.append(jnp.sum(dy * xs[8:], axis=0, keepdims=True))
        dw = jnp.concatenate(rows + [jnp.zeros((8 - CONV_WIDTH, width), F32)], axis=0)

        @pl.when(i == 0)
        def _():
            dw_ref[...] = dw

        @pl.when(i > 0)
        def _():
            dw_ref[...] += dw

    t8 = tile // 8
    return _pcall(body, name="gdn_conv_bwd",
                  out_shape=[jax.ShapeDtypeStruct((n_rows, width), BF16), jax.ShapeDtypeStruct((8, width), F32)],
                  grid=(n,),
                  in_specs=[pl.BlockSpec((tile, width), lambda i: (i, 0)),
                            pl.BlockSpec((8, width), lambda i: (jnp.maximum(i * t8 - 1, 0), 0)),
                            pl.BlockSpec((8, width), lambda i: (0, 0)),
                            pl.BlockSpec((tile, width), lambda i: (i, 0)),
                            pl.BlockSpec((8, width), lambda i: (jnp.minimum((i + 1) * t8, n * t8 - 1), 0))],
                  out_specs=[pl.BlockSpec((tile, width), lambda i: (i, 0)), pl.BlockSpec((8, width), lambda i: (0, 0))],
                  sem=("arbitrary",))(proj, proj, w8, dy, dy)


def gdn_scan_fwd(u, w, qk, kt, qh, gc):
    n_rows = u.shape[0]
    C, n = GDN_CHUNK, u.shape[0] // GDN_CHUNK

    def body(u_ref, w_ref, qk_ref, kt_ref, qh_ref, gc_ref, o_ref, vn_ref, sin_ref, st):
        @pl.when(pl.program_id(0) == 0)
        def _():
            st[...] = jnp.zeros_like(st)

        sin_ref[0] = st[...]
        sls = [slice(h * HEAD_DIM, (h + 1) * HEAD_DIM) for h in range(GDN_HEADS)]
        states = [st[sl, :] for sl in sls]
        v_news = [u_ref[:, sl] - sdot(w_ref[:, sl], s) for sl, s in zip(sls, states)]
        from_state = [sdot(qh_ref[:, sl], s) for sl, s in zip(sls, states)]
        for sl, a, v_new in zip(sls, from_state, v_news):
            o_ref[:, sl] = a + sdot(qk_ref[:, sl], v_new)
            vn_ref[:, sl] = v_new
        for sl, s, v_new in zip(sls, states, v_news):
            st[sl, :] = s * jnp.exp(gc_ref[C - 1:C, sl]) + sdot.tn(kt_ref[:, sl], v_new)

    blk = pl.BlockSpec((C, GDN_WIDTH), lambda i: (i, 0))
    return _pcall(body, name="gdn_scan_fwd",
                  out_shape=[jax.ShapeDtypeStruct((n_rows, GDN_WIDTH), F32), jax.ShapeDtypeStruct((n_rows, GDN_WIDTH), F32),
                             jax.ShapeDtypeStruct((n, GDN_WIDTH, HEAD_DIM), F32)],
                  grid=(n,), in_specs=[blk] * 6,
                  out_specs=[blk, blk, pl.BlockSpec((1, GDN_WIDTH, HEAD_DIM), lambda i: (i, 0, 0))],
                  scratch=[pltpu.VMEM((GDN_WIDTH, HEAD_DIM), F32)], sem=("arbitrary",))(u, w, qk, kt, qh, gc)


def gdn_scan_bwd(do, w, qk, kt, qh, gc, vn, sin):
    n_rows = do.shape[0]
    C, n = GDN_CHUNK, do.shape[0] // GDN_CHUNK

    def body(do_ref, w_ref, qk_ref, kt_ref, qh_ref, gc_ref, vn_ref, sin_ref,
             du_ref, dw_ref, dqk_ref, dkt_ref, dqh_ref, dgl_ref, dst):
        @pl.when(pl.program_id(0) == 0)
        def _():
            dst[...] = jnp.zeros_like(dst)

        sls = [slice(h * HEAD_DIM, (h + 1) * HEAD_DIM) for h in range(GDN_HEADS)]
        dvns = [sdot.tn(qk_ref[:, sl], do_ref[:, sl]) + sdot(kt_ref[:, sl], dst[sl, :]) for sl in sls]
        for sl, dvn in zip(sls, dvns):
            du_ref[:, sl] = dvn
            dw_ref[:, sl] = -sdot.nt(dvn, sin_ref[0, sl, :])
        for sl in sls:
            dqk_ref[:, sl] = sdot.nt(do_ref[:, sl], vn_ref[:, sl])
            dkt_ref[:, sl] = sdot.nt(vn_ref[:, sl], dst[sl, :])
            dqh_ref[:, sl] = sdot.nt(do_ref[:, sl], sin_ref[0, sl, :])
        for sl, dvn in zip(sls, dvns):
            ds_out = dst[sl, :]
            e = jnp.exp(gc_ref[C - 1:C, sl])
            dgl = jnp.sum(ds_out * sin_ref[0, sl, :], axis=0, keepdims=True) * e
            dgl_ref[:, sl] = jnp.broadcast_to(dgl, (8, HEAD_DIM))
            dst[sl, :] = sdot.tn(qh_ref[:, sl], do_ref[:, sl]) + e * ds_out - sdot.tn(w_ref[:, sl], dvn)

    blk = pl.BlockSpec((C, GDN_WIDTH), lambda i: (n - 1 - i, 0))
    row = jax.ShapeDtypeStruct((n_rows, GDN_WIDTH), F32)
    return _pcall(body, name="gdn_scan_bwd",
                  out_shape=[row] * 5 + [jax.ShapeDtypeStruct((n * 8, GDN_WIDTH), F32)],
                  grid=(n,), in_specs=[blk] * 7 + [pl.BlockSpec((1, GDN_WIDTH, HEAD_DIM), lambda i: (n - 1 - i, 0, 0))],
                  out_specs=[blk] * 5 + [pl.BlockSpec((8, GDN_WIDTH), lambda i: (n - 1 - i, 0))],
                  scratch=[pltpu.VMEM((GDN_WIDTH, HEAD_DIM), F32)], sem=("arbitrary",))(do, w, qk, kt, qh, gc, vn, sin)


def gdn_intra_bwd(q, k, v, gc, beta, inv, cts, dgl):
    n_rows = q.shape[0]
    C = GDN_CHUNK

    def body(*refs):
        ins = [r[...] for r in refs[:5]]
        inv_v = refs[5][...]
        ct = tuple(r[...] for r in refs[6:11])
        dgl_v = refs[11][...]
        _, vjp = jax.vjp(lambda *a: f_gdn_intra(*a, inv=inv_v), *ins)
        grads = list(vjp(ct))
        last = _iota2((C, GDN_WIDTH), 0) == C - 1
        grads[3] = grads[3] + jnp.where(last, jnp.broadcast_to(dgl_v[0:1, :], (C, GDN_WIDTH)), 0.0)
        for o_ref, g in zip(refs[12:], grads):
            o_ref[...] = g

    blk = pl.BlockSpec((C, GDN_WIDTH), lambda i: (i, 0))
    return _pcall(body, name="gdn_intra_bwd", out_shape=[jax.ShapeDtypeStruct((n_rows, GDN_WIDTH), F32)] * 5,
                  grid=(n_rows // C,), in_specs=[blk] * 11 + [pl.BlockSpec((8, GDN_WIDTH), lambda i: (i, 0))],
                  out_specs=[blk] * 5, sem=("parallel",))(q, k, v, gc, beta, inv, *cts, dgl)


def fox_gate_fwd(f, b_f):
    n_rows = f.shape[0]
    T = LANES

    def body(f_ref, b_ref, cb_ref, carry):
        @pl.when(pl.program_id(0) == 0)
        def _():
            carry[...] = jnp.zeros_like(carry)

        c = fdot(_tri(T).astype(F32), _log_sigmoid(f_ref[...] + b_ref[...])) + carry[...]
        carry[...] = c[T - 1:T, :]
        cb_ref[...] = fdot(c, _head_sel(0))

    return _pcall(body, name="fox_gate_fwd", out_shape=jax.ShapeDtypeStruct((n_rows, FOX_WIDTH), F32),
                  grid=(n_rows // T,),
                  in_specs=[pl.BlockSpec((T, LANES), lambda i: (i, 0)), pl.BlockSpec((1, LANES), lambda i: (0, 0))],
                  out_specs=pl.BlockSpec((T, FOX_WIDTH), lambda i: (i, 0)),
                  scratch=[pltpu.VMEM((1, LANES), F32)], sem=("arbitrary",))(f, b_f)


def fox_gate_bwd(f, b_f, dcrow, dcb):
    n_rows = f.shape[0]
    T = LANES
    n = n_rows // T

    def body(f_ref, b_ref, dc_ref, dcb_ref, df_ref, db_ref, carry):
        i = pl.program_id(0)

        @pl.when(i == 0)
        def _():
            carry[...] = jnp.zeros_like(carry)

        rows = [dc_ref[h] for h in range(FOX_HEADS)] + [jnp.zeros((T - FOX_HEADS, T), F32)]
        first_lane = (_iota2((FOX_WIDTH, LANES), 0) == _iota2((FOX_WIDTH, LANES), 1) * HEAD_DIM).astype(F32)
        dc = jnp.concatenate(rows, axis=0).T + fdot(dcb_ref[...], first_lane)
        dlog = fdot.tn(_tri(T).astype(F32), dc) + carry[...]
        carry[...] = dlog[0:1, :]
        df = dlog * (1.0 - _sigmoid(f_ref[...] + b_ref[...]))
        df_ref[...] = df
        db = jnp.sum(df, axis=0, keepdims=True)

        @pl.when(i == 0)
        def _():
            db_ref[...] = db

        @pl.when(i > 0)
        def _():
            db_ref[...] += db

    return _pcall(body, name="fox_gate_bwd",
                  out_shape=[jax.ShapeDtypeStruct((n_rows, LANES), F32), jax.ShapeDtypeStruct((1, LANES), F32)],
                  grid=(n,),
                  in_specs=[pl.BlockSpec((T, LANES), lambda i: (n - 1 - i, 0)), pl.BlockSpec((1, LANES), lambda i: (0, 0)),
                            pl.BlockSpec((FOX_HEADS, 1, T), lambda i: (0, 0, n - 1 - i)),
                            pl.BlockSpec((T, FOX_WIDTH), lambda i: (n - 1 - i, 0))],
                  out_specs=[pl.BlockSpec((T, LANES), lambda i: (n - 1 - i, 0)), pl.BlockSpec((1, LANES), lambda i: (0, 0))],
                  scratch=[pltpu.VMEM((1, LANES), F32)], sem=("arbitrary",))(f, b_f, dcrow, dcb)


FOX_AUG = 2 * HEAD_DIM


def _fox_tiles(n_rows):
    return min(1024, n_rows), min(512, n_rows)


def _fox_pairs(n_rows, query_major):
    tq, tk = _fox_tiles(n_rows)
    nq, r = n_rows // tq, tq // tk
    if query_major:
        pairs = [(i, j) for i in range(nq) for j in range(r * (i + 1))]
    else:
        pairs = [(i, j) for j in range(nq * r) for i in range(j // r, nq)]
    return jnp.asarray([p[0] for p in pairs], jnp.int32), jnp.asarray([p[1] for p in pairs], jnp.int32)


def fox_augment(x, cb, query_side):
    def fn(xt, ct):
        xt = xt.astype(F32)
        lane = _iota2((xt.shape[0], HEAD_DIM), 1)
        parts = []
        for h in range(FOX_HEADS):
            sl = slice(h * HEAD_DIM, (h + 1) * HEAD_DIM)
            c = ct[:, sl]
            hi = c.astype(BF16).astype(F32)
            mid = (c - hi).astype(BF16).astype(F32)
            lo = (c - hi - mid).astype(BF16).astype(F32)
            terms = jnp.where(lane % 3 == 0, hi, jnp.where(lane % 3 == 1, mid, lo))
            if query_side:
                extra = jnp.where(lane < 3, terms, jnp.where(lane < 6, 1.0, 0.0))
                parts += [xt[:, sl] * (HEAD_DIM ** -0.5), extra]
            else:
                extra = jnp.where(lane < 3, 1.0, jnp.where(lane < 6, -terms, 0.0))
                parts += [xt[:, sl], extra]
        return (jnp.concatenate(parts, axis=1),)

    return rowwise(fn, [(x, FOX_WIDTH, 0), cb], [], [(FOX_HEADS * FOX_AUG, BF16)], tile=ROW_TILE,
                   name="fox_augment_q" if query_side else "fox_augment_k")[0]


def _pcall_tables(body, *, name, out_shape, grid, tables, in_specs, out_specs, scratch, sem):
    spec = pltpu.PrefetchScalarGridSpec(num_scalar_prefetch=len(tables), grid=grid, in_specs=in_specs, out_specs=out_specs,
                                        scratch_shapes=list(scratch))
    return pl.pallas_call(body, name=name, out_shape=out_shape, grid_spec=spec,
                          compiler_params=pltpu.CompilerParams(vmem_limit_bytes=VMEM_LIMIT_BYTES, dimension_semantics=sem))


def _fox_logits(qa, ka, offset):
    s = bdot.nt(qa, ka)
    if offset is not None:
        s = jnp.where(_iota2(s.shape, 0) + offset >= _iota2(s.shape, 1), s, NEG_INF)
    return s


def _fox_p_ds(offset, qa_ref, ka_ref, v_ref, o_ref, lse_ref, do_ref):
    s = _fox_logits(qa_ref[...], ka_ref[...], offset)
    p = jnp.exp(s - jnp.tile(lse_ref[...], (1, s.shape[1] // LANES)))
    d_o = do_ref[...].astype(F32)
    delta = jnp.sum(d_o * o_ref[...].astype(F32), axis=-1, keepdims=True)
    return p, p * (bdot.nt(d_o, v_ref[...]) - delta), d_o


def _fox_on_diagonal(i, j, r, tk, step):
    @pl.when(j < r * i)
    def _():
        step(None)

    for m in range(r):
        @pl.when(j == r * i + m)
        def _(m=m):
            step(-m * tk)


def _fox_specs(tq, tk, do_col):
    qaspec = pl.BlockSpec((tq, FOX_AUG), lambda h, p, it, jt: (it[p], h))
    qspec = pl.BlockSpec((tq, HEAD_DIM), lambda h, p, it, jt: (it[p], h))
    dospec = pl.BlockSpec((tq, HEAD_DIM), lambda h, p, it, jt: (it[p], do_col + h))
    kaspec = pl.BlockSpec((tk, FOX_AUG), lambda h, p, it, jt: (jt[p], h))
    kspec = pl.BlockSpec((tk, HEAD_DIM), lambda h, p, it, jt: (jt[p], h))
    vspec = pl.BlockSpec((tk, HEAD_DIM), lambda h, p, it, jt: (jt[p], FOX_HEADS + h))
    cspec = pl.BlockSpec((1, 1, tk), lambda h, p, it, jt: (h, 0, jt[p]))
    return qaspec, qspec, dospec, kaspec, kspec, vspec, cspec


def fox_fwd(qa, kv, ka):
    n_rows = kv.shape[0]
    tq, tk = _fox_tiles(n_rows)
    r = tq // tk
    tables = _fox_pairs(n_rows, True)

    def body(it, jt, qa_ref, ka_ref, v_ref, o_ref, lse_ref, m_sc, l_sc, acc):
        i, j = it[pl.program_id(1)], jt[pl.program_id(1)]

        @pl.when(j == 0)
        def _():
            m_sc[...] = jnp.full(m_sc.shape, NEG_INF, F32)
            l_sc[...] = jnp.zeros_like(l_sc)
            acc[...] = jnp.zeros_like(acc)

        def step(offset):
            s = _fox_logits(qa_ref[...], ka_ref[...], offset)
            m_old = m_sc[...]
            m_new = jnp.maximum(m_old, jnp.max(s, axis=-1, keepdims=True))
            alpha = jnp.exp(m_old - m_new)
            p = jnp.exp(s - jnp.tile(m_new, (1, tk // LANES)))
            l_sc[...] = l_sc[...] * alpha + jnp.sum(p, axis=-1, keepdims=True)
            acc[...] = acc[...] * alpha + bdot(p, v_ref[...])
            m_sc[...] = m_new

        _fox_on_diagonal(i, j, r, tk, step)

        @pl.when(j == r * i + r - 1)
        def _():
            o_ref[...] = (acc[...] / l_sc[...]).astype(o_ref.dtype)
            lse_ref[...] = m_sc[...] + jnp.log(l_sc[...])

    qaspec, qspec, _, kaspec, _, vspec, _ = _fox_specs(tq, tk, 0)
    return _pcall_tables(body, name="fox_fwd",
                         out_shape=[jax.ShapeDtypeStruct((n_rows, FOX_WIDTH), BF16), jax.ShapeDtypeStruct((n_rows, FOX_WIDTH), F32)],
                         grid=(FOX_HEADS, tables[0].shape[0]), tables=tables,
                         in_specs=[qaspec, kaspec, vspec], out_specs=[qspec, qspec],
                         scratch=[pltpu.VMEM((tq, HEAD_DIM), F32)] * 3, sem=("parallel", "arbitrary"))(*tables, qa, ka, kv)


def fox_bwd_dq(qa, kv, ka, o, lse, do, prev=None):
    do, _, do_col = do
    do_col *= FOX_HEADS
    n_rows = kv.shape[0]
    tq, tk = _fox_tiles(n_rows)
    r = tq // tk
    n_prev = 0 if prev is None else 1
    tables = _fox_pairs(n_rows, True)

    def body(it, jt, qa_ref, ka_ref, k_ref, v_ref, o_ref, lse_ref, do_ref, *rest):
        dq_ref, drow_ref, acc, rows = rest[n_prev:]
        i, j = it[pl.program_id(1)], jt[pl.program_id(1)]

        @pl.when(j == 0)
        def _():
            acc[...] = jnp.zeros_like(acc)
            rows[...] = jnp.zeros_like(rows)

        def step(offset):
            _, ds, _ = _fox_p_ds(offset, qa_ref, ka_ref, v_ref, o_ref, lse_ref, do_ref)
            acc[...] += bdot(ds, k_ref[...])
            rows[...] += jnp.sum(ds, axis=-1, keepdims=True)

        _fox_on_diagonal(i, j, r, tk, step)

        @pl.when(j == r * i + r - 1)
        def _():
            dq_ref[...] = (acc[...] * (HEAD_DIM ** -0.5)).astype(dq_ref.dtype)
            drow_ref[...] = rows[...] + rest[0][...] if n_prev else rows[...]

    qaspec, qspec, dospec, kaspec, kspec, vspec, _ = _fox_specs(tq, tk, do_col)
    return _pcall_tables(body, name="fox_bwd_dq" + ("_acc" if n_prev else ""),
                         out_shape=[jax.ShapeDtypeStruct((n_rows, FOX_WIDTH), BF16), jax.ShapeDtypeStruct((n_rows, FOX_WIDTH), F32)],
                         grid=(FOX_HEADS, tables[0].shape[0]), tables=tables,
                         in_specs=[qaspec, kaspec, kspec, vspec, qspec, qspec, dospec] + [qspec] * n_prev,
                         out_specs=[qspec, qspec], scratch=[pltpu.VMEM((tq, HEAD_DIM), F32)] * 2,
                         sem=("parallel", "arbitrary"))(*tables, qa, ka, kv, kv, o, lse, do, *([prev] if n_prev else []))


def fox_bwd_dkv(qa, kv, ka, o, lse, do, prev=None):
    do, _, do_col = do
    do_col *= FOX_HEADS
    n_rows = kv.shape[0]
    tq, tk = _fox_tiles(n_rows)
    nq, r = n_rows // tq, tq // tk
    n_prev = 0 if prev is None else 3
    tables = _fox_pairs(n_rows, False)

    def body(it, jt, qa_ref, ka_ref, v_ref, o_ref, lse_ref, do_ref, *rest):
        prev_refs = rest[:n_prev]
        dk_ref, dv_ref, dc_ref, dk_acc, dv_acc, dc_acc = rest[n_prev:]
        i, j = it[pl.program_id(1)], jt[pl.program_id(1)]

        @pl.when(j >= r * i)
        def _():
            dk_acc[...] = jnp.zeros_like(dk_acc)
            dv_acc[...] = jnp.zeros_like(dv_acc)
            dc_acc[...] = jnp.zeros_like(dc_acc)

        def step(offset):
            p, ds, d_o = _fox_p_ds(offset, qa_ref, ka_ref, v_ref, o_ref, lse_ref, do_ref)
            dv_acc[...] += bdot.tn(p, d_o)
            dk_acc[...] += bdot.tn(ds, qa_ref[:, :HEAD_DIM])
            dc_acc[...] -= jnp.sum(ds, axis=0, keepdims=True)

        _fox_on_diagonal(i, j, r, tk, step)

        @pl.when(i == nq - 1)
        def _():
            dk, dv, dc = dk_acc[...], dv_acc[...], dc_acc[...]
            if n_prev:
                dk, dv, dc = dk + prev_refs[0][...], dv + prev_refs[1][...], dc + prev_refs[2][0]
            dk_ref[...] = dk
            dv_ref[...] = dv
            dc_ref[0] = dc

    qaspec, qspec, dospec, kaspec, kspec, vspec, cspec = _fox_specs(tq, tk, do_col)
    return _pcall_tables(body, name="fox_bwd_dkv" + ("_acc" if n_prev else ""),
                         out_shape=[jax.ShapeDtypeStruct((n_rows, FOX_WIDTH), F32), jax.ShapeDtypeStruct((n_rows, FOX_WIDTH), F32),
                                    jax.ShapeDtypeStruct((FOX_HEADS, 1, n_rows), F32)],
                         grid=(FOX_HEADS, tables[0].shape[0]), tables=tables,
                         in_specs=[qaspec, kaspec, vspec, qspec, qspec, dospec] + [kspec, kspec, cspec][:n_prev],
                         out_specs=[kspec, kspec, cspec],
                         scratch=[pltpu.VMEM((tk, HEAD_DIM), F32), pltpu.VMEM((tk, HEAD_DIM), F32), pltpu.VMEM((1, tk), F32)],
                         sem=("parallel", "arbitrary"))(*tables, qa, ka, kv, o, lse, do, *(prev or ()))


def loss_head(h, gain, target, *, tile=512):
    n_rows, d = h.shape
    tile = min(tile, n_rows)

    def body(h_ref, g_ref, t_ref, part_ref, dy_ref):
        (y,) = f_rmsnorm(h_ref[...], g_ref[...])
        diff = y - t_ref[...]
        dy_ref[...] = diff * (1.0 / d)
        part = jnp.sum(diff * diff, axis=0, keepdims=True)
        first = pl.program_id(0) == 0

        @pl.when(first)
        def _():
            part_ref[...] = part

        @pl.when(jnp.logical_not(first))
        def _():
            part_ref[...] += part

    blk = pl.BlockSpec((tile, d), lambda i: (i, 0))
    one = pl.BlockSpec((1, d), lambda i: (0, 0))
    return _pcall(body, name="loss_head",
                  out_shape=[jax.ShapeDtypeStruct((1, d), F32), jax.ShapeDtypeStruct((n_rows, d), F32)],
                  grid=(n_rows // tile,), in_specs=[blk, one, blk], out_specs=[one, blk], sem=("arbitrary",))(h, gain, target)


def adamw(w, g, m, v, *, name):
    shape = w.shape
    cols = shape[-1] if w.ndim >= 2 else w.size
    rows = w.size // cols
    tile = _pick(rows, (256, 128, 64, 32, 16, 8))
    as2d = lambda a: a.reshape(rows, cols)

    def body(w_ref, g_ref, m_ref, v_ref, d_ref, nm_ref, nv_ref):
        g_ = g_ref[...]
        m_ = ADAM_B1 * m_ref[...] + (1.0 - ADAM_B1) * g_
        v_ = ADAM_B2 * v_ref[...] + (1.0 - ADAM_B2) * (g_ * g_)
        m_hat = m_ / (1.0 - ADAM_B1 ** ADAM_STEP)
        v_hat = v_ / (1.0 - ADAM_B2 ** ADAM_STEP)
        d_ref[...] = -ADAM_LR * (m_hat / (jnp.sqrt(v_hat) + ADAM_EPS) + ADAM_WD * w_ref[...])
        nm_ref[...] = m_
        nv_ref[...] = v_

    blk = pl.BlockSpec((tile, cols), lambda i: (i, 0))
    outs = _pcall(body, name=name, out_shape=[jax.ShapeDtypeStruct((rows, cols), F32)] * 3, grid=(rows // tile,),
                  in_specs=[blk] * 4, out_specs=[blk] * 3, sem=("parallel",))(as2d(w), as2d(g), as2d(m), as2d(v))
    return tuple(o.reshape(shape) for o in outs)


def sum_leading(a, *, name):
    p, r, c = a.shape
    tile = _pick(r, (256, 128, 64, 32, 16, 8))

    def body(a_ref, o_ref):
        total = a_ref[0].astype(F32)
        for k in range(1, p):
            total = total + a_ref[k].astype(F32)
        o_ref[...] = total

    return _pcall(body, name=name, out_shape=jax.ShapeDtypeStruct((r, c), F32), grid=(r // tile,),
                  in_specs=[pl.BlockSpec((p, tile, c), lambda i: (0, i, 0))],
                  out_specs=pl.BlockSpec((tile, c), lambda i: (i, 0)), sem=("parallel",))(a)


_HBM = pl.BlockSpec(memory_space=pltpu.HBM)


def _comm_call(body, *, name, out_shape, n_in, scratch):
    return pl.pallas_call(body, name=name, out_shape=out_shape, in_specs=[_HBM] * n_in, out_specs=_HBM,
                          scratch_shapes=scratch,
                          compiler_params=pltpu.CompilerParams(has_side_effects=True))


def all_gather8(a, *, name):
    m_per, n = a.shape

    def body(x_ref, out_ref, send_sems, recv_sems, local_sem):
        x, y, c = lax.axis_index("x"), lax.axis_index("y"), lax.axis_index("c")
        me, sibling = (x, y, c), (x, y, 1 - c)
        chips = [(1 - x, y), (x, 1 - y), (1 - x, 1 - y)]

        def rows(px, py, pc):
            return out_ref.at[pl.ds((4 * px + 2 * py + pc) * m_per, m_per), :]

        def copy(k, block, to, src=None):
            return pltpu.make_async_remote_copy(
                src_ref=rows(*block) if src is None else src, dst_ref=rows(*block),
                send_sem=send_sems.at[k], recv_sem=recv_sems.at[k], device_id=to, device_id_type=MESH)

        mine = pltpu.make_async_copy(x_ref, rows(*me), local_sem)
        mine.start()
        first = [copy(0, me, sibling, src=x_ref)]
        first += [copy(1 + j, me, (*chip, c), src=x_ref) for j, chip in enumerate(chips)]
        for cp in first:
            cp.start()
        passed = [copy(4 + j, (*chip, c), sibling) for j, chip in enumerate(chips)]
        for j, chip in enumerate(chips):
            copy(1 + j, (*chip, c), me).wait_recv()
            passed[j].start()
        copy(0, sibling, me).wait_recv()
        for j, chip in enumerate(chips):
            copy(4 + j, (*chip, 1 - c), me).wait_recv()
        for cp in first + passed:
            cp.wait_send()
        mine.wait()

    return _comm_call(body, name=name, out_shape=jax.ShapeDtypeStruct((N_DEV * m_per, n), a.dtype), n_in=1,
                      scratch=[pltpu.SemaphoreType.DMA((7,)), pltpu.SemaphoreType.DMA((7,)), pltpu.SemaphoreType.DMA])(a)


PACK_COLS = 1024
PACK_ROW_MULTIPLE = 32

WEIGHT_NAMES = ["ffn1_norm", "ffn1_w_gate_up", "ffn1_w_down", "mix_norm", "ffn2_norm", "ffn2_w_gate_up", "ffn2_w_down",
                "gdn_w_in", "gdn_conv", "gdn_A_log", "gdn_dt_bias", "gdn_out_norm", "fox_w_in", "w_out", "mem_norm",
                "mem_w_kv", "kv_norm", "kv_w", "kv_b_f", "final_norm"]
SHARDED = [("ffn1_w_gate_up", 2), ("ffn1_w_down", 1), ("ffn2_w_gate_up", 2), ("ffn2_w_down", 1), ("gdn_w_in", 2),
           ("gdn_conv", 2), ("fox_w_in", 1), ("w_out", 1), ("mem_w_kv", 1), ("kv_w", 0)]
REPLICATED = [n for n in WEIGHT_NAMES if n not in dict(SHARDED)]


PACK_PIECE_ROWS = 16


def _rows_of(size):
    return -(-size // (PACK_COLS * PACK_PIECE_ROWS)) * PACK_PIECE_ROWS


def pack(pieces, dtype, row_multiple=PACK_ROW_MULTIPLE):
    bufs, total = [], 0
    for p in pieces:
        flat = p.astype(dtype).reshape(-1)
        rows = _rows_of(flat.size)
        bufs.append(jnp.pad(flat, (0, rows * PACK_COLS - flat.size)).reshape(rows, PACK_COLS))
        total += rows
    pad = -total % row_multiple
    if pad:
        bufs.append(jnp.zeros((pad, PACK_COLS), dtype))
    return jnp.concatenate(bufs, axis=0)


def unpack(buf, shapes):
    out, row = [], 0
    for shape in shapes:
        size = 1
        for s in shape:
            size *= s
        rows = _rows_of(size)
        out.append(buf[row:row + rows].reshape(-1)[:size].reshape(shape))
        row += rows
    return out


def _row(vec, width=None):
    vec = vec.astype(F32).reshape(1, -1)
    if width is not None and vec.shape[1] < width:
        vec = jnp.pad(vec, ((0, 0), (0, width - vec.shape[1])))
    return vec


ROW_TILE = 512
GDN_PROJ_WIDTH = 4 * GDN_WIDTH + MEM_WIDTH + LANES
GDN_Z_COL, GDN_QMEM_COL, GDN_AB_COL = 3, 4 * GDN_WIDTH // MEM_WIDTH, (4 * GDN_WIDTH + MEM_WIDTH) // LANES
FOX_QMEM_COL = FOX_WIDTH // MEM_WIDTH
KV_PAD_WIDTH = 2 * FOX_WIDTH + LANES


def rms_fwd(x, gain_row, out_dtype=BF16):
    return rowwise(f_rmsnorm, [x], [gain_row], [(x.shape[1], out_dtype)], tile=ROW_TILE, name="rms_fwd")[0]


def rms_bwd(x, gain_row, dy, dres=None):
    return rowwise_bwd(f_rmsnorm, [x], [gain_row], [dy], tile=ROW_TILE, name="rms_bwd", row_grads=[F32],
                       const_grads=[True], add=None if dres is None else {0: dres})


def _ffn_tiles(n_rows):
    return _pick(n_rows, (512, 256, 128)), _pick(FFN_HIDDEN, (1408, 256, 128))


def ffn_up_act(n, wgu):
    n_rows, d = n.shape
    tm, tn = _ffn_tiles(n_rows)
    nj = FFN_HIDDEN // tn

    def body(n_ref, wg_ref, wu_ref, gu_ref, act_ref):
        x = n_ref[...].astype(BF16)
        g = bdot.nn(x, wg_ref[...])
        u = bdot.nn(x, wu_ref[...])
        gu_ref[0] = g.astype(gu_ref.dtype)
        gu_ref[1] = u.astype(gu_ref.dtype)
        act_ref[...] = (_silu(g) * u).astype(act_ref.dtype)

    return _pcall(body, name="ffn_up_act",
                  out_shape=[jax.ShapeDtypeStruct((2, n_rows, FFN_HIDDEN), BF16), jax.ShapeDtypeStruct((n_rows, FFN_HIDDEN), BF16)],
                  grid=(nj, n_rows // tm),
                  in_specs=[pl.BlockSpec((tm, d), lambda j, i: (i, 0)), pl.BlockSpec((d, tn), lambda j, i: (0, j)),
                            pl.BlockSpec((d, tn), lambda j, i: (0, nj + j))],
                  out_specs=[pl.BlockSpec((2, tm, tn), lambda j, i: (0, i, j)), pl.BlockSpec((tm, tn), lambda j, i: (i, j))],
                  sem=("parallel", "parallel"))(n, wgu, wgu)


def ffn_down_dx_act(dh, wd, gu):
    n_rows, d = dh.shape
    tm, tn = _ffn_tiles(n_rows)

    def body(dh_ref, wd_ref, gu_ref, dgu_ref):
        dact = 0.5 * bdot.nt(dh_ref[...], wd_ref[...])
        gate, up = gu_ref[0].astype(F32), gu_ref[1].astype(F32)
        sg = _sigmoid(gate)
        dgu_ref[0] = (dact * up * (sg * (1.0 + gate * (1.0 - sg)))).astype(dgu_ref.dtype)
        dgu_ref[1] = (dact * (gate * sg)).astype(dgu_ref.dtype)

    blk = pl.BlockSpec((2, tm, tn), lambda j, i: (0, i, j))
    return _pcall(body, name="ffn_down_dx_act", out_shape=jax.ShapeDtypeStruct((2, n_rows, FFN_HIDDEN), BF16),
                  grid=(FFN_HIDDEN // tn, n_rows // tm),
                  in_specs=[pl.BlockSpec((tm, d), lambda j, i: (i, 0)), pl.BlockSpec((tn, d), lambda j, i: (j, 0)), blk],
                  out_specs=blk, sem=("parallel", "parallel"))(dh, wd, gu)


def ffn_fwd(h, gain_row, wgu, wd):
    n = rms_fwd(h, gain_row)
    gu, act = ffn_up_act(n, wgu)
    return mm(act, wd, scale=0.5, res=h, name="ffn_down"), (h, n, gu, act)


def ffn_bwd(dh, saved, gain_row, wgu, wd):
    h, n, gu, act = saved
    dgu = ffn_down_dx_act(dh, wd, gu)
    dwd = mm(act, dh, ta=True, scale=0.5, name="ffn_down_dw")
    dwgu = mm(n, dgu, ta=True, b_split=True, name="ffn_up_dw")
    dh, dgain = mm_rms_bwd(dgu, wgu, h, gain_row, dh, a_split=True, name="ffn_up_dx")
    return dh, dwgu, dwd, dgain


def gdn_fwd(proj, w8, a_row, dt_row, onorm_row):
    wide = [(GDN_WIDTH, F32)] * 5
    xc = conv_fwd(proj, w8, width=3 * GDN_WIDTH)
    q, k, v, gc, beta = rowwise(f_gdn_pre, [xc, (proj, LANES, GDN_AB_COL)], [a_row, dt_row], wide, tile=GDN_CHUNK,
                                name="gdn_pre_fwd")
    u, w, qk, kt, qh, inv = rowwise(f_gdn_intra, [q, k, v, gc, beta], [], wide + wide[:1], tile=GDN_CHUNK,
                                    name="gdn_intra_fwd")
    o, vn, sin = gdn_scan_fwd(u, w, qk, kt, qh, gc)
    main = rowwise(f_gdn_post, [o, (proj, GDN_WIDTH, GDN_Z_COL)], [onorm_row], [(GDN_WIDTH, BF16)], tile=ROW_TILE,
                   name="gdn_post_fwd")[0]
    return main, (xc, q, k, v, gc, beta, inv, w, qk, kt, qh, vn, sin, o)


def gdn_bwd(dmain, proj, saved, w8, a_row, dt_row, onorm_row):
    xc, q, k, v, gc, beta, inv, w, qk, kt, qh, vn, sin, o = saved
    do, dz, donorm = rowwise_bwd(f_gdn_post, [o, (proj, GDN_WIDTH, GDN_Z_COL)], [onorm_row], [dmain], tile=ROW_TILE,
                                 name="gdn_post_bwd", row_grads=[F32, BF16], const_grads=[True])
    du, dw, dqk, dkt, dqh, dgl = gdn_scan_bwd(do, w, qk, kt, qh, gc, vn, sin)
    dq, dk, dv, dgc, dbeta = gdn_intra_bwd(q, k, v, gc, beta, inv, (du, dw, dqk, dkt, dqh), dgl)
    dxc, dab, da, ddt = rowwise_bwd(f_gdn_pre, [xc, (proj, LANES, GDN_AB_COL)], [a_row, dt_row], [dq, dk, dv, dgc, dbeta],
                                    tile=GDN_CHUNK, name="gdn_pre_bwd", row_grads=[F32, BF16], const_grads=[True, True])
    dqkv, dw8 = conv_bwd(proj, w8, dxc, width=3 * GDN_WIDTH)
    return dqkv, dz, dab, dw8, da, ddt, donorm


def mem_fwd(q, kmem, vmem):
    return rowwise(f_mem_attn, [q], [kmem, vmem], [(MEM_WIDTH, BF16)], tile=ROW_TILE, name="mem_attn_fwd")[0]


def mem_bwd(q, kmem, vmem, dout):
    return rowwise_bwd(f_mem_attn, [q], [kmem, vmem], [dout], tile=ROW_TILE, name="mem_attn_bwd", row_grads=[BF16],
                       const_grads=[True, True])


def forward_backward(xs, mems, target, P):
    depth, n_a = 4, 2
    G = {}
    mem_gain = _row(P["mem_norm"])
    mem_n = rms_fwd(mems, mem_gain)
    h = xs
    saved = []
    shared = None
    for l in range(depth):
        h0 = h
        h1, s1 = ffn_fwd(h0, _row(P["ffn1_norm"][l]), P["ffn1_w_gate_up"][l], P["ffn1_w_down"][l])
        u = rms_fwd(h1, _row(P["mix_norm"][l]))
        kvm = mm(mem_n, P["mem_w_kv"][l], name="mem_kv")
        kmem, vmem = kvm[:, :MEM_WIDTH], kvm[:, MEM_WIDTH:]
        if l < n_a:
            gp = (P["conv8"][l], _row(P["gdn_A_log"][l], LANES), _row(P["gdn_dt_bias"][l], LANES), _row(P["gdn_out_norm"][l]))
            proj = mm(u, P["gdn_w_in_pad"][l], name="gdn_in")
            main, sm = gdn_fwd(proj, *gp)
            qm = (proj, MEM_WIDTH, GDN_QMEM_COL)
        else:
            proj = mm(u, P["fox_w_in"][l - n_a], out_dtype=BF16, name="fox_in")
            kv, ka, cb = shared
            qa = fox_augment(proj, cb, True)
            main, lse = fox_fwd(qa, kv, ka)
            sm = (main, lse, qa)
            qm = (proj, MEM_WIDTH, FOX_QMEM_COL)
        mo = mem_fwd(qm, kmem, vmem)
        cat = jnp.concatenate([main, mo], axis=1)
        h2 = mm(cat, P["w_out"][l], res=h1, name="mix_out")
        h3, s2 = ffn_fwd(h2, _row(P["ffn2_norm"][l]), P["ffn2_w_gate_up"][l], P["ffn2_w_down"][l])
        saved.append((s1, h1, u, kmem, vmem, proj, sm, qm, cat, s2))
        h = h3
        if l == n_a - 1:
            nkv = rms_fwd(h, _row(P["kv_norm"]))
            kv = mm(nkv, P["kv_w_pad"][:, :2 * FOX_WIDTH], out_dtype=BF16, name="fox_kv")
            f = mm(nkv, P["kv_w_pad"][:, 2 * FOX_WIDTH:], name="fox_f")
            bf_row = _row(P["kv_b_f"], LANES)
            cb = fox_gate_fwd(f, bf_row)
            shared = (kv, fox_augment(kv, cb, False), cb)
            kv_saved = (h, nkv, f, bf_row)

    part, dy = loss_head(h, _row(P["final_norm"]), target)
    dh, G["final_norm"] = rms_bwd(h, _row(P["final_norm"]), dy)

    per_layer = {n: [None] * depth for n in ("ffn1_norm", "ffn1_w_gate_up", "ffn1_w_down", "mix_norm", "ffn2_norm",
                                             "ffn2_w_gate_up", "ffn2_w_down", "w_out", "mem_w_kv")}
    gdn_g = {n: [None] * n_a for n in ("gdn_w_in_pad", "conv8", "gdn_A_log", "gdn_dt_bias", "gdn_out_norm")}
    fox_g = [None] * (depth - n_a)
    dmem_n = None
    dkv_acc = dcb_acc = None
    for l in reversed(range(depth)):
        s1, h1, u, kmem, vmem, proj, sm, qm, cat, s2 = saved[l]
        if l == n_a - 1:
            hk, nkv, f, bf_row = kv_saved
            dk, dv, dcrow = dkv_acc
            df, dbf = fox_gate_bwd(f, bf_row, dcrow, dcb_acc)
            dp = jnp.concatenate([dk.astype(BF16), dv.astype(BF16), df.astype(BF16)], axis=1)
            G["kv_w_pad"] = mm(nkv, dp, ta=True, name="fox_kv_dw")
            G["kv_b_f"] = dbf
            dh, G["kv_norm"] = mm_rms_bwd(dp, P["kv_w_pad"], hk, _row(P["kv_norm"]), dh, name="fox_kv_dx")
        dh, per_layer["ffn2_w_gate_up"][l], per_layer["ffn2_w_down"][l], per_layer["ffn2_norm"][l] = ffn_bwd(
            dh, s2, _row(P["ffn2_norm"][l]), P["ffn2_w_gate_up"][l], P["ffn2_w_down"][l])
        dcat = mm(dh, P["w_out"][l], tb=True, out_dtype=BF16, name="mix_out_dx")
        per_layer["w_out"][l] = mm(cat, dh, ta=True, name="mix_out_dw")
        dqm, dkm, dvm = mem_bwd(qm, kmem, vmem, (dcat, MEM_WIDTH, FOX_QMEM_COL))
        dkvm = jnp.concatenate([dkm, dvm], axis=1)
        per_layer["mem_w_kv"][l] = mm(mem_n, dkvm, ta=True, name="mem_kv_dw")
        dmem_n = mm(dkvm, P["mem_w_kv"][l], tb=True, res=dmem_n, name="mem_kv_dx")
        dmain = (dcat, GDN_WIDTH, 0)
        if l < n_a:
            gp = (P["conv8"][l], _row(P["gdn_A_log"][l], LANES), _row(P["gdn_dt_bias"][l], LANES), _row(P["gdn_out_norm"][l]))
            dqkv, dz, dab, gdn_g["conv8"][l], gdn_g["gdn_A_log"][l], gdn_g["gdn_dt_bias"][l], gdn_g["gdn_out_norm"][l] = gdn_bwd(
                dmain, proj, sm, *gp)
            dproj = jnp.concatenate([dqkv, dz, dqm, dab], axis=1)
            gdn_g["gdn_w_in_pad"][l] = mm(u, dproj, ta=True, name="gdn_in_dw")
            w_in = P["gdn_w_in_pad"][l]
        else:
            o, lse, qa = sm
            kv, ka, _ = shared
            dq, dcb_acc = fox_bwd_dq(qa, kv, ka, o, lse, dmain, dcb_acc)
            dkv_acc = fox_bwd_dkv(qa, kv, ka, o, lse, dmain, dkv_acc)
            dproj = jnp.concatenate([dq, dqm], axis=1)
            fox_g[l - n_a] = mm(u, dproj, ta=True, name="fox_in_dw")
            w_in = P["fox_w_in"][l - n_a]
        dh, per_layer["mix_norm"][l] = mm_rms_bwd(dproj, w_in, h1, _row(P["mix_norm"][l]), dh, name="mix_in_dx")
        dh, per_layer["ffn1_w_gate_up"][l], per_layer["ffn1_w_down"][l], per_layer["ffn1_norm"][l] = ffn_bwd(
            dh, s1, _row(P["ffn1_norm"][l]), P["ffn1_w_gate_up"][l], P["ffn1_w_down"][l])

    (G["mem_norm"],) = rowwise_bwd(f_rmsnorm, [mems], [mem_gain], [dmem_n], tile=ROW_TILE, name="mem_norm_bwd",
                                   row_grads=[None], const_grads=[True])
    for n, v in per_layer.items():
        G[n] = jnp.stack(v)
    for n, v in gdn_g.items():
        G[n] = jnp.stack(v)
    G["fox_w_in"] = jnp.stack(fox_g)
    return part, dh, G


_GDN_O0 = 4 * GDN_WIDTH
_GDN_O1 = _GDN_O0 + 2 * GDN_HEADS
_KV_WIDTH = 2 * FOX_WIDTH + FOX_HEADS


def derived_weights(gdn_w_in, gdn_conv, kv_w=None):
    zeros = jnp.zeros(gdn_w_in.shape[:-1] + (LANES - 2 * GDN_HEADS,), gdn_w_in.dtype)
    out = dict(
        gdn_w_in_pad=jnp.concatenate([gdn_w_in[..., :_GDN_O0], gdn_w_in[..., _GDN_O1:], gdn_w_in[..., _GDN_O0:_GDN_O1], zeros], axis=-1),
        conv8=jnp.pad(gdn_conv.astype(F32), ((0, 0), (0, 8 - CONV_WIDTH), (0, 0))))
    if kv_w is not None:
        out["kv_w_pad"] = jnp.pad(kv_w, ((0, 0), (0, KV_PAD_WIDTH - _KV_WIDTH)))
    return out


def reference_layout(G):
    gp = G["gdn_w_in_pad"]
    out = dict(G)
    out["gdn_w_in"] = jnp.concatenate([gp[..., :_GDN_O0], gp[..., _GDN_O0 + MEM_WIDTH:_GDN_O0 + MEM_WIDTH + 2 * GDN_HEADS],
                                       gp[..., _GDN_O0:_GDN_O0 + MEM_WIDTH]], axis=-1)
    out["gdn_conv"] = G["conv8"][:, :CONV_WIDTH]
    out["kv_w"] = G["kv_w_pad"][:, :_KV_WIDTH]
    out["gdn_A_log"] = G["gdn_A_log"][:, 0, :GDN_HEADS]
    out["gdn_dt_bias"] = G["gdn_dt_bias"][:, 0, :GDN_HEADS]
    out["gdn_out_norm"] = G["gdn_out_norm"][:, 0, :]
    out["kv_b_f"] = G["kv_b_f"][0, :FOX_HEADS]
    for n in ("ffn1_norm", "mix_norm", "ffn2_norm"):
        out[n] = G[n][:, 0, :]
    for n in ("mem_norm", "kv_norm", "final_norm"):
        out[n] = G[n][0]
    return {n: out[n] for n in WEIGHT_NAMES}


EXCHANGED = [("ffn1_w_gate_up", 2, 0), ("ffn1_w_down", 1, 0), ("ffn2_w_gate_up", 2, 0), ("ffn2_w_down", 1, 0),
             ("gdn_w_in", 2, 0), ("fox_w_in", 1, 0), ("w_out", 1, 0), ("mem_w_kv", 1, 0), ("kv_w", 0, 1)]
GDN_IN_SHARD = (4 * GDN_WIDTH + 2 * GDN_HEADS + MEM_WIDTH) // N_CHIPS
GDN_IN_SLOT = 896


def _slab(ref, axis_slices):
    idx = [slice(None)] * len(ref.shape)
    for axis, (start, size) in axis_slices.items():
        idx[axis] = pl.ds(start, size)
    return ref.at[tuple(idx)]


def _comm_multi(body, *, name, n_in, out_shapes, scratch):
    return pl.pallas_call(body, name=name, out_shape=out_shapes, in_specs=[_HBM] * n_in, out_specs=[_HBM] * len(out_shapes),
                          scratch_shapes=scratch, compiler_params=pltpu.CompilerParams(has_side_effects=True))


def gather_shards(shards, layout):
    n = len(shards)
    fulls = [tuple(d * (N_CHIPS if a == sa else 1) for a, d in enumerate(s.shape)) for s, (sa, _) in zip(shards, layout)]

    def body(*refs):
        ins, outs = refs[:n], refs[n:2 * n]
        send_sems, recv_sems, local_sems = refs[2 * n:]
        x, y, c = lax.axis_index("x"), lax.axis_index("y"), lax.axis_index("c")
        me, sibling = (x, y, c), (x, y, 1 - c)
        chips = [(1 - x, y), (x, 1 - y), (1 - x, 1 - y)]

        def region(w, px, py, pc):
            (sa, ha), shard = layout[w], shards[w].shape
            return _slab(outs[w], {sa: ((2 * px + py) * shard[sa], shard[sa]), ha: (pc * (shard[ha] // 2), shard[ha] // 2)})

        def my_half(w):
            ha, shard = layout[w][1], shards[w].shape
            return _slab(ins[w], {ha: (c * (shard[ha] // 2), shard[ha] // 2)})

        def copy(w, k, block, to, src=None):
            return pltpu.make_async_remote_copy(
                src_ref=region(w, *block) if src is None else src, dst_ref=region(w, *block),
                send_sem=send_sems.at[7 * w + k], recv_sem=recv_sems.at[7 * w + k], device_id=to, device_id_type=MESH)

        mine, first, passed = [], [], []
        for w in range(n):
            mine.append(pltpu.make_async_copy(my_half(w), region(w, *me), local_sems.at[w]))
            mine[w].start()
            first.append([copy(w, 0, me, sibling, src=my_half(w))]
                         + [copy(w, 1 + j, me, (*chip, c), src=my_half(w)) for j, chip in enumerate(chips)])
            for cp in first[w]:
                cp.start()
            passed.append([copy(w, 4 + j, (*chip, c), sibling) for j, chip in enumerate(chips)])
        for w in range(n):
            for j, chip in enumerate(chips):
                copy(w, 1 + j, (*chip, c), me).wait_recv()
                passed[w][j].start()
        for w in range(n):
            copy(w, 0, sibling, me).wait_recv()
            for j, chip in enumerate(chips):
                copy(w, 4 + j, (*chip, 1 - c), me).wait_recv()
        for w in range(n):
            for cp in first[w] + passed[w]:
                cp.wait_send()
            mine[w].wait()

    return _comm_multi(body, name="gather_shards", n_in=n,
                       out_shapes=[jax.ShapeDtypeStruct(f, s.dtype) for f, s in zip(fulls, shards)],
                       scratch=[pltpu.SemaphoreType.DMA((7 * n,)), pltpu.SemaphoreType.DMA((7 * n,)),
                                pltpu.SemaphoreType.DMA((n,))])(*shards)


def swap_other_halves(arrays, layout):
    n = len(arrays)
    halves = [tuple(d // 2 if a == ha else d for a, d in enumerate(g.shape)) for g, (_, ha) in zip(arrays, layout)]

    def body(*refs):
        ins, outs = refs[:n], refs[n:2 * n]
        send_sems, recv_sems = refs[2 * n:]
        x, y, c = lax.axis_index("x"), lax.axis_index("y"), lax.axis_index("c")
        copies = []
        for w in range(n):
            ha, size = layout[w][1], halves[w][layout[w][1]]
            copies.append(pltpu.make_async_remote_copy(
                src_ref=_slab(ins[w], {ha: ((1 - c) * size, size)}), dst_ref=outs[w], send_sem=send_sems.at[w],
                recv_sem=recv_sems.at[w], device_id=(x, y, 1 - c), device_id_type=MESH))
            copies[w].start()
        for cp in copies:
            cp.wait()

    return _comm_multi(body, name="grad_pair_swap", n_in=n,
                       out_shapes=[jax.ShapeDtypeStruct(h, g.dtype) for h, g in zip(halves, arrays)],
                       scratch=[pltpu.SemaphoreType.DMA((n,)), pltpu.SemaphoreType.DMA((n,))])(*arrays)


def scatter_to_chips(arrays, layout):
    n = len(arrays)
    slabs = [tuple(d // N_CHIPS if a == sa else d for a, d in enumerate(p.shape)) for p, (sa, _) in zip(arrays, layout)]

    def body(*refs):
        ins, outs = refs[:n], refs[n:2 * n]
        send_sems, recv_sems, local_sems = refs[2 * n:]
        x, y, c = lax.axis_index("x"), lax.axis_index("y"), lax.axis_index("c")
        me = 2 * x + y

        def slab(w, k):
            sa, size = layout[w][0], slabs[w][layout[w][0]]
            return _slab(ins[w], {sa: (k * size, size)})

        local, copies = [], []
        for w in range(n):
            local.append(pltpu.make_async_copy(slab(w, me), outs[w].at[me], local_sems.at[w]))
            local[w].start()
            for j, (px, py) in enumerate([(1 - x, y), (x, 1 - y), (1 - x, 1 - y)]):
                copies.append(pltpu.make_async_remote_copy(
                    src_ref=slab(w, 2 * px + py), dst_ref=outs[w].at[me], send_sem=send_sems.at[3 * w + j],
                    recv_sem=recv_sems.at[3 * w + j], device_id=(px, py, c), device_id_type=MESH))
                copies[-1].start()
        for cp in copies:
            cp.wait()
        for cp in local:
            cp.wait()

    return _comm_multi(body, name="grad_all_to_all", n_in=n,
                       out_shapes=[jax.ShapeDtypeStruct((N_CHIPS,) + s, p.dtype) for s, p in zip(slabs, arrays)],
                       scratch=[pltpu.SemaphoreType.DMA((3 * n,)), pltpu.SemaphoreType.DMA((3 * n,)),
                                pltpu.SemaphoreType.DMA((n,))])(*arrays)


def swap_with_sibling(arrays):
    n = len(arrays)

    def body(*refs):
        ins, outs = refs[:n], refs[n:2 * n]
        send_sems, recv_sems = refs[2 * n:]
        x, y, c = lax.axis_index("x"), lax.axis_index("y"), lax.axis_index("c")
        copies = [pltpu.make_async_remote_copy(src_ref=ins[w], dst_ref=outs[w], send_sem=send_sems.at[w], recv_sem=recv_sems.at[w],
                                               device_id=(x, y, 1 - c), device_id_type=MESH) for w in range(n)]
        for cp in copies:
            cp.start()
        for cp in copies:
            cp.wait()

    return _comm_multi(body, name="grad_half_swap", n_in=n, out_shapes=[jax.ShapeDtypeStruct(a.shape, a.dtype) for a in arrays],
                       scratch=[pltpu.SemaphoreType.DMA((n,)), pltpu.SemaphoreType.DMA((n,))])(*arrays)


def join_halves(mine, other, half_axis, c_arr, *, name):
    a0, a1, a2 = mine.shape
    tile = _row_tile(a1, a2)

    def body(c_ref, m_ref, o_ref, out_ref):
        for half in range(2):
            @pl.when(c_ref[0] == half)
            def _(half=half):
                out_ref[half] = m_ref[...]
                out_ref[1 - half] = o_ref[...]

    blk = pl.BlockSpec((None, tile, a2), lambda i, j, c_ref: (i, j, 0))
    if half_axis == 0:
        out_shape, out_blk = (2, a0, a1, a2), pl.BlockSpec((2, None, tile, a2), lambda i, j, c_ref: (0, i, j, 0))
    else:
        out_shape, out_blk = (a0, 2, a1, a2), pl.BlockSpec((None, 2, tile, a2), lambda i, j, c_ref: (i, 0, j, 0))
    spec = pltpu.PrefetchScalarGridSpec(num_scalar_prefetch=1, grid=(a0, a1 // tile), in_specs=[blk, blk], out_specs=out_blk)
    out = pl.pallas_call(body, name=name, out_shape=jax.ShapeDtypeStruct(out_shape, mine.dtype), grid_spec=spec,
                         compiler_params=pltpu.CompilerParams(vmem_limit_bytes=VMEM_LIMIT_BYTES,
                                                              dimension_semantics=("parallel", "parallel")))(c_arr, mine, other)
    return out.reshape((2 * a0, a1, a2) if half_axis == 0 else (a0, 2 * a1, a2))


def _row_tile(rows, cols, itemsize=4, budget=2 * 1024 * 1024):
    for t in (1024, 512, 256, 128, 64, 32, 16):
        if rows % t == 0 and t * cols * itemsize <= budget:
            return t
    return rows


def add_own_half(full, recv, half_axis, c_arr, *, name):
    a0, a1, a2 = recv.shape
    tile = _row_tile(a1, a2)

    def body(c_ref, f_ref, r_ref, o_ref):
        o_ref[...] = (f_ref[...] + r_ref[...]).astype(o_ref.dtype)

    if half_axis == 0:
        f_spec = pl.BlockSpec((None, tile, a2), lambda i, j, c_ref: (c_ref[0] * a0 + i, j, 0))
    else:
        f_spec = pl.BlockSpec((None, tile, a2), lambda i, j, c_ref: (i, c_ref[0] * (a1 // tile) + j, 0))
    blk = pl.BlockSpec((None, tile, a2), lambda i, j, c_ref: (i, j, 0))
    spec = pltpu.PrefetchScalarGridSpec(num_scalar_prefetch=1, grid=(a0, a1 // tile), in_specs=[f_spec, blk], out_specs=blk)
    return pl.pallas_call(body, name=name, out_shape=jax.ShapeDtypeStruct(recv.shape, BF16), grid_spec=spec,
                          compiler_params=pltpu.CompilerParams(vmem_limit_bytes=VMEM_LIMIT_BYTES,
                                                               dimension_semantics=("parallel", "parallel")))(c_arr, full, recv)


def sum_slots(q, *, name):
    _, a0, a1, a2 = q.shape
    tile = _row_tile(a1, a2, budget=1024 * 1024)

    def body(q_ref, o_ref):
        total = q_ref[0].astype(F32)
        for k in range(1, N_CHIPS):
            total = total + q_ref[k].astype(F32)
        o_ref[...] = total

    return _pcall(body, name=name, out_shape=jax.ShapeDtypeStruct((a0, a1, a2), F32), grid=(a0, a1 // tile),
                  in_specs=[pl.BlockSpec((N_CHIPS, None, tile, a2), lambda i, j: (0, i, j, 0))],
                  out_specs=pl.BlockSpec((None, tile, a2), lambda i, j: (i, j, 0)), sem=("parallel", "parallel"))(q)


def gather_weights(W):
    shards = []
    for name, _, _ in EXCHANGED:
        w = W[name].astype(BF16)
        if name == "gdn_w_in":
            w = jnp.pad(w, ((0, 0), (0, 0), (0, GDN_IN_SLOT - GDN_IN_SHARD)))
        if name == "kv_w":
            w = jnp.pad(w, ((0, 0), (0, KV_PAD_WIDTH - _KV_WIDTH)))[None]
        shards.append(w)
    fulls = dict(zip([n for n, _, _ in EXCHANGED], gather_shards(shards, [(sa, ha) for _, sa, ha in EXCHANGED])))
    slots = fulls["gdn_w_in"]
    fulls["gdn_w_in"] = jnp.concatenate([slots[..., k * GDN_IN_SLOT:k * GDN_IN_SLOT + GDN_IN_SHARD] for k in range(N_CHIPS)], axis=-1)
    fulls["kv_w_pad"] = fulls.pop("kv_w").reshape(-1, KV_PAD_WIDTH)
    conv = pack([W["gdn_conv"]], F32, row_multiple=8)
    conv_all = all_gather8(conv, name="gather_conv").reshape(N_DEV, conv.shape[0], PACK_COLS)
    fulls["gdn_conv"] = jnp.concatenate([unpack(conv_all[2 * k], [W["gdn_conv"].shape])[0] for k in range(N_CHIPS)], axis=-1)
    return fulls


def reduce_gradients(G):
    layout = [(sa, ha) for _, sa, ha in EXCHANGED]
    c_arr = lax.axis_index("c").astype(jnp.int32).reshape(1)
    received = swap_other_halves(G, layout)
    pairs = [add_own_half(g, r, ha, c_arr, name="grad_pair_sum") for g, r, (_, ha) in zip(G, received, layout)]
    slots = scatter_to_chips(pairs, layout)
    halves = [sum_slots(q, name="grad_chip_sum") for q in slots]
    others = swap_with_sibling(halves)
    return [join_halves(h, o, ha, c_arr, name="grad_join_halves") for h, o, (_, ha) in zip(halves, others, layout)]


def allreduce_small(G, names):
    packed = pack([G[n] for n in names], F32, row_multiple=8)
    gathered = all_gather8(packed, name="gather_small_grads").reshape(N_DEV, packed.shape[0], PACK_COLS)
    total = sum_leading(gathered, name="sum_small_grads")
    return dict(zip(names, unpack(total, [G[n].shape for n in names])))


def kernel(x, mem, *rest):
    n_w = len(WEIGHT_NAMES)
    W = dict(zip(WEIGHT_NAMES, rest[:n_w]))
    target = rest[n_w]
    M = dict(zip(WEIGHT_NAMES, rest[n_w + 1:2 * n_w + 1]))
    V = dict(zip(WEIGHT_NAMES, rest[2 * n_w + 1:3 * n_w + 1]))

    full = gather_weights(W)
    P = {n: W[n] for n in REPLICATED}
    P.update({n: full[n] for n in ("ffn1_w_gate_up", "ffn1_w_down", "ffn2_w_gate_up", "ffn2_w_down", "fox_w_in", "w_out",
                                   "mem_w_kv", "kv_w_pad")})
    derived = derived_weights(full["gdn_w_in"], full["gdn_conv"])
    P.update(gdn_w_in_pad=derived["gdn_w_in_pad"], conv8=derived["conv8"])

    part, dx, G = forward_backward(x[0], mem[0], target[0], P)
    loss = lax.psum(0.5 / x.shape[-1] * jnp.sum(part), ("x", "y", "c"))

    ref = reference_layout(G)
    exchange = {n: ref[n] for n, _, _ in EXCHANGED}
    exchange["gdn_w_in"] = jnp.concatenate(
        [jnp.pad(ref["gdn_w_in"][..., k * GDN_IN_SHARD:(k + 1) * GDN_IN_SHARD], ((0, 0), (0, 0), (0, GDN_IN_SLOT - GDN_IN_SHARD)))
         for k in range(N_CHIPS)], axis=-1)
    exchange["kv_w"] = G["kv_w_pad"].reshape(N_CHIPS, -1, KV_PAD_WIDTH)
    shards = dict(zip([n for n, _, _ in EXCHANGED], reduce_gradients([exchange[n] for n, _, _ in EXCHANGED])))
    shards["gdn_w_in"] = shards["gdn_w_in"][..., :GDN_IN_SHARD]
    shards["kv_w"] = shards["kv_w"][0, :, :_KV_WIDTH]
    grads = allreduce_small(ref, REPLICATED + ["gdn_conv"])
    conv_cols = W["gdn_conv"].shape[-1]
    chip = 2 * lax.axis_index("x") + lax.axis_index("y")
    grads["gdn_conv"] = lax.dynamic_slice_in_dim(grads["gdn_conv"], chip * conv_cols, conv_cols, axis=2)
    grads.update(shards)

    outs = {n: adamw(W[n], grads[n], M[n], V[n], name="adamw_" + n) for n in WEIGHT_NAMES}
    return (loss, dx[None], *[grads[n] for n in WEIGHT_NAMES], *[outs[n][0] for n in WEIGHT_NAMES],
            *[outs[n][1] for n in WEIGHT_NAMES], *[outs[n][2] for n in WEIGHT_NAMES])
```

```python
import jax
import jax.numpy as jnp
from jax import lax
from jax.experimental import pallas as pl
from jax.experimental.pallas import tpu as pltpu

F32, BF16 = jnp.float32, jnp.bfloat16
HI = lax.Precision.HIGHEST
MESH = pl.DeviceIdType.MESH

VMEM_LIMIT_BYTES = 48 * 1024 * 1024
LANES = 128
EPS = 1e-6
NEG_INF = -1e30

D_MODEL = 1024
HEAD_DIM = 128
GDN_HEADS = 6
GDN_WIDTH = GDN_HEADS * HEAD_DIM
FOX_HEADS = 6
FOX_WIDTH = FOX_HEADS * HEAD_DIM
MEM_HEADS = 4
MEM_HEAD_DIM = 64
MEM_WIDTH = MEM_HEADS * MEM_HEAD_DIM
FFN_HIDDEN = 2816
CONV_WIDTH = 4
GDN_CHUNK = 128
N_CHIPS = 4
N_DEV = 8

ADAM_LR, ADAM_B1, ADAM_B2, ADAM_EPS, ADAM_WD, ADAM_STEP = 0.001, 0.9, 0.999, 1e-08, 0.01, 10


def _pcall(body, *, name, out_shape, grid=(), in_specs=None, out_specs=None, scratch=(), sem=None):
    params = dict(vmem_limit_bytes=VMEM_LIMIT_BYTES)
    if sem is not None:
        params["dimension_semantics"] = sem
    kw = dict(grid=grid, in_specs=in_specs, out_specs=out_specs) if grid else {}
    return pl.pallas_call(body, name=name, out_shape=out_shape, scratch_shapes=list(scratch),
                          compiler_params=pltpu.CompilerParams(**params), **kw)


def _pick(n, cands):
    for c in cands:
        if n % c == 0:
            return c
    return n


def _make_dot(dtype, precision):
    def raw(a, b, dims):
        return lax.dot_general(a.astype(dtype), b.astype(dtype), (dims, ((), ())),
                               precision=precision, preferred_element_type=F32)

    @jax.custom_vjp
    def dot(a, b):
        return raw(a, b, ((1,), (0,)))

    def fwd(a, b):
        return dot(a, b), (a, b)

    def bwd(resid, ct):
        a, b = resid
        return raw(ct, b, ((1,), (1,))).astype(a.dtype), raw(a, ct, ((0,), (0,))).astype(b.dtype)

    dot.defvjp(fwd, bwd)
    dot.nn = lambda a, b: raw(a, b, ((1,), (0,)))
    dot.nt = lambda a, b: raw(a, b, ((1,), (1,)))
    dot.tn = lambda a, b: raw(a, b, ((0,), (0,)))
    return dot


bdot = _make_dot(BF16, None)
fdot = _make_dot(F32, HI)
idot = _make_dot(F32, lax.Precision.HIGH)
sdot = idot


def _sigmoid(x):
    return 0.5 * jnp.tanh(0.5 * x) + 0.5


def _silu(x):
    return x * _sigmoid(x)


def _softplus(x):
    return jnp.maximum(x, 0.0) + jnp.log(1.0 + jnp.exp(-jnp.abs(x)))


def _log_sigmoid(x):
    return -_softplus(-x)


def _iota2(shape, dim):
    return lax.broadcasted_iota(jnp.int32, shape, dim)


def mm(a, b, *, ta=False, tb=False, a_split=False, b_split=False, out_dtype=F32, scale=1.0, res=None, name):
    assert not (a_split and ta) and not (b_split and tb)
    (K, M) = a.shape if ta else ((2 * a.shape[2], a.shape[1]) if a_split else a.shape[::-1])
    (N, Kb) = b.shape if tb else ((2 * b.shape[2], b.shape[1]) if b_split else b.shape[::-1])
    assert K == Kb, (a.shape, b.shape, ta, tb)
    tm = _pick(M, (1024, 1408, 512, 256, 128))
    tn = _pick(N, (1024, 1408, 1152, 1664, 768, 512, 384, 256, 128))
    tk = _pick(K, (1024, 512, 256, 128)) if ta else _pick(K, (1024, 1408, 1152, 1664, 512, 256, 128))
    assert not a_split or (K // 2) % tk == 0
    assert not b_split or (N // 2) % tn == 0
    nk = K // tk
    dims = (((0 if ta else 1,), (1 if tb else 0,)), ((), ()))

    def body(a_ref, b_ref, *rest):
        o_ref, acc = rest[-2], rest[-1]
        k = pl.program_id(2)

        @pl.when(k == 0)
        def _():
            acc[...] = jnp.zeros_like(acc)

        acc[...] += lax.dot_general(a_ref[...].astype(BF16), b_ref[...].astype(BF16), dims,
                                    preferred_element_type=F32)

        @pl.when(k == nk - 1)
        def _():
            out = acc[...] * scale
            if res is not None:
                out = out + rest[0][...].astype(F32)
            o_ref[...] = out.astype(o_ref.dtype)

    a_spec = pl.BlockSpec((tk, tm), lambda i, j, k: (k, i)) if ta else pl.BlockSpec((tm, tk), lambda i, j, k: (i, k))
    b_spec = pl.BlockSpec((tn, tk), lambda i, j, k: (j, k)) if tb else pl.BlockSpec((tk, tn), lambda i, j, k: (k, j))
    if a_split:
        per_half = K // 2 // tk
        a_spec = pl.BlockSpec((None, tm, tk), lambda i, j, k: (k // per_half, i, k % per_half))
    if b_split:
        per_half = N // 2 // tn
        b_spec = pl.BlockSpec((None, tk, tn), lambda i, j, k: (j // per_half, k, j % per_half))
    o_spec = pl.BlockSpec((tm, tn), lambda i, j, k: (i, j))
    ins, specs = [a, b], [a_spec, b_spec]
    if res is not None:
        ins.append(res)
        specs.append(o_spec)
    return _pcall(body, name=name, out_shape=jax.ShapeDtypeStruct((M, N), out_dtype),
                  grid=(M // tm, N // tn, nk), in_specs=specs, out_specs=o_spec,
                  scratch=[pltpu.VMEM((tm, tn), F32)], sem=("parallel", "parallel", "arbitrary"))(*ins)


def mm_rms_bwd(a, b, x, gain_row, dres, *, a_split=False, name):
    (K, M) = (2 * a.shape[2], a.shape[1]) if a_split else a.shape[::-1]
    D = b.shape[0]
    assert b.shape[1] == K and x.shape == (M, D)
    tm = _pick(M, (1024, 512, 256, 128))
    rows = _pick(tm, (256, 128))
    tk = _pick(K, (1024, 1408, 1152, 1664, 512, 256, 128))
    assert not a_split or (K // 2) % tk == 0
    nk = K // tk

    def body(a_ref, b_ref, x_ref, g_ref, r_ref, o_ref, dg_ref, acc):
        i, k = pl.program_id(0), pl.program_id(1)

        @pl.when(k == 0)
        def _():
            acc[...] = jnp.zeros_like(acc)

        acc[...] += bdot.nt(a_ref[...], b_ref[...])

        @pl.when(k == nk - 1)
        def _():
            dg = jnp.zeros((1, D), F32)
            for s in range(tm // rows):
                sl = slice(s * rows, (s + 1) * rows)
                xv = x_ref[sl, :]
                rstd = lax.rsqrt(jnp.mean(xv * xv, axis=-1, keepdims=True) + EPS)
                xh = xv * rstd
                dn = acc[sl, :]
                dy = dn * g_ref[...]
                o_ref[sl, :] = (dy - xh * jnp.mean(dy * xh, axis=-1, keepdims=True)) * rstd + r_ref[sl, :]
                dg = dg + jnp.sum(dn * xh, axis=0, keepdims=True)

            @pl.when(i == 0)
            def _():
                dg_ref[...] = dg

            @pl.when(i > 0)
            def _():
                dg_ref[...] += dg

    a_spec = pl.BlockSpec((tm, tk), lambda i, k: (i, k))
    if a_split:
        per_half = K // 2 // tk
        a_spec = pl.BlockSpec((None, tm, tk), lambda i, k: (k // per_half, i, k % per_half))
    row = pl.BlockSpec((tm, D), lambda i, k: (i, 0))
    one = pl.BlockSpec((1, D), lambda i, k: (0, 0))
    return _pcall(body, name=name, out_shape=[jax.ShapeDtypeStruct((M, D), F32), jax.ShapeDtypeStruct((1, D), F32)],
                  grid=(M // tm, nk), in_specs=[a_spec, pl.BlockSpec((D, tk), lambda i, k: (0, k)), row, one, row],
                  out_specs=[row, one], scratch=[pltpu.VMEM((tm, D), F32)], sem=("arbitrary", "arbitrary"))(a, b, x, gain_row, dres)


def _row_spec(r, tile):
    if isinstance(r, tuple):
        arr, width, col = r
        return arr, pl.BlockSpec((tile, width), lambda i, col=col: (i, col))
    return r, pl.BlockSpec((tile, r.shape[1]), lambda i: (i, 0))


def _const_spec(c):
    return pl.BlockSpec(c.shape, lambda i: (0,) * c.ndim)


def rowwise(fn, rows, consts, outs, *, tile, name):
    arrs, specs = zip(*[_row_spec(r, tile) for r in rows])
    n_rows = arrs[0].shape[0]
    tile = min(tile, n_rows)
    n_in = len(rows) + len(consts)

    def body(*refs):
        res = fn(*[r[...] for r in refs[:n_in]])
        for o_ref, o in zip(refs[n_in:], res):
            o_ref[...] = o.astype(o_ref.dtype)

    arrs, specs = zip(*[_row_spec(r, tile) for r in rows])
    return _pcall(body, name=name,
                  out_shape=[jax.ShapeDtypeStruct((n_rows, w), dt) for w, dt in outs],
                  grid=(n_rows // tile,),
                  in_specs=list(specs) + [_const_spec(c) for c in consts],
                  out_specs=[pl.BlockSpec((tile, w), lambda i: (i, 0)) for w, _ in outs],
                  sem=("parallel",))(*arrs, *consts)


def rowwise_bwd(fn, rows, consts, cts, *, tile, name, row_grads, const_grads, add=None):
    arrs, _ = zip(*[_row_spec(r, tile) for r in rows])
    n_rows = arrs[0].shape[0]
    tile = min(tile, n_rows)
    arrs, specs = zip(*[_row_spec(r, tile) for r in rows])
    ct_arrs, ct_specs = zip(*[_row_spec(c, tile) for c in cts])
    add = add or {}
    add_idx = sorted(add)
    add_arrs, add_specs = (zip(*[_row_spec(add[i], tile) for i in add_idx]) if add_idx else ((), ()))
    nr, nc, nct, na = len(rows), len(consts), len(cts), len(add_idx)
    want_rows = [i for i, d in enumerate(row_grads) if d is not None]
    want_consts = [i for i, w in enumerate(const_grads) if w]

    def body(*refs):
        row_v = [r[...] for r in refs[:nr]]
        const_v = [r[...] for r in refs[nr:nr + nc]]
        ct_v = [r[...] for r in refs[nr + nc:nr + nc + nct]]
        add_v = {i: refs[nr + nc + nct + j][...] for j, i in enumerate(add_idx)}
        out_refs = refs[nr + nc + nct + na:]
        res, vjp = jax.vjp(fn, *row_v, *const_v)
        grads = vjp(tuple(c.astype(o.dtype) for c, o in zip(ct_v, res)))
        for o_ref, i in zip(out_refs, want_rows):
            g = grads[i].astype(F32)
            if i in add_v:
                g = g + add_v[i].astype(F32)
            o_ref[...] = g.astype(o_ref.dtype)
        first = pl.program_id(0) == 0
        for o_ref, i in zip(out_refs[len(want_rows):], want_consts):
            g = grads[nr + i].astype(F32)

            @pl.when(first)
            def _(o_ref=o_ref, g=g):
                o_ref[...] = g

            @pl.when(jnp.logical_not(first))
            def _(o_ref=o_ref, g=g):
                o_ref[...] += g

    def width(r):
        return r[1] if isinstance(r, tuple) else r.shape[1]

    out_shape = [jax.ShapeDtypeStruct((n_rows, width(rows[i])), row_grads[i]) for i in want_rows]
    out_shape += [jax.ShapeDtypeStruct(consts[i].shape, F32) for i in want_consts]
    out_specs = [pl.BlockSpec((tile, width(rows[i])), lambda i_: (i_, 0)) for i in want_rows]
    out_specs += [_const_spec(consts[i]) for i in want_consts]
    return _pcall(body, name=name, out_shape=out_shape, grid=(n_rows // tile,),
                  in_specs=list(specs) + [_const_spec(c) for c in consts] + list(ct_specs) + list(add_specs),
                  out_specs=out_specs, sem=("arbitrary",))(*arrs, *consts, *ct_arrs, *add_arrs)


def f_rmsnorm(x, g):
    x = x.astype(F32)
    return (x * lax.rsqrt(jnp.mean(x * x, axis=-1, keepdims=True) + EPS) * g,)


def _head_sel(first_lane):
    r, c = _iota2((LANES, GDN_WIDTH), 0), _iota2((LANES, GDN_WIDTH), 1)
    return (r == c // HEAD_DIM + first_lane).astype(F32)


def _tri(n, strict=False):
    r, c = _iota2((n, n), 0), _iota2((n, n), 1)
    return r > c if strict else r >= c


def f_gdn_pre(xc, ab, a_log, dt_bias):
    s = _silu(xc.astype(F32))
    qs, ks = [], []
    for h in range(GDN_HEADS):
        qh = s[:, h * HEAD_DIM:(h + 1) * HEAD_DIM]
        kh = s[:, GDN_WIDTH + h * HEAD_DIM:GDN_WIDTH + (h + 1) * HEAD_DIM]
        qs.append(qh * lax.rsqrt(jnp.sum(qh * qh, axis=-1, keepdims=True) + EPS) * (HEAD_DIM ** -0.5))
        ks.append(kh * lax.rsqrt(jnp.sum(kh * kh, axis=-1, keepdims=True) + EPS))
    q, k = jnp.concatenate(qs, axis=1), jnp.concatenate(ks, axis=1)
    v = s[:, 2 * GDN_WIDTH:]
    ab = ab.astype(F32)
    g = -jnp.exp(a_log) * _softplus(ab + dt_bias)
    gc = _head_broadcast(fdot(_tri(GDN_CHUNK).astype(F32), g), 0)
    beta = _head_broadcast(_sigmoid(ab), GDN_HEADS)
    return q, k, v, gc, beta


def _head_broadcast(x, first_lane):
    lane = _iota2(x.shape, 1)
    cols = [jnp.sum(jnp.where(lane == first_lane + h, x, 0.0), axis=1, keepdims=True) for h in range(GDN_HEADS)]
    return jnp.concatenate([jnp.broadcast_to(c, x.shape) for c in cols], axis=1)


def _unit_lower_inverses(neg_lowers):
    C = neg_lowers[0].shape[0]
    eye = (_iota2((C, C), 0) == _iota2((C, C), 1)).astype(F32)
    invs = [eye + n for n in neg_lowers]
    powers = list(neg_lowers)
    for _ in range(6):
        powers = [idot.nn(p, p) for p in powers]
        invs = [inv + idot.nn(p, inv) for p, inv in zip(powers, invs)]
    return invs


@jax.custom_vjp
def _solve_with_inverse(inv, neg_lower, rhs):
    return idot.nn(inv, rhs)


def _solve_fwd(inv, neg_lower, rhs):
    x = idot.nn(inv, rhs)
    return x, (inv, x)


def _solve_bwd(resid, ct):
    inv, x = resid
    d_rhs = idot.tn(inv, ct)
    return jnp.zeros_like(inv), idot.nt(d_rhs, x), d_rhs


_solve_with_inverse.defvjp(_solve_fwd, _solve_bwd)


def f_gdn_intra(q, k, v, gc, beta, inv=None):
    C = GDN_CHUNK
    causal, strict = _tri(C), _tri(C, strict=True)
    is_last = _iota2((C, HEAD_DIM), 0) == C - 1
    heads = range(GDN_HEADS)
    sls = [slice(h * HEAD_DIM, (h + 1) * HEAD_DIM) for h in heads]
    qs, ks, vs, gs, bs = ([a[:, sl] for sl in sls] for a in (q, k, v, gc, beta))
    decays = [jnp.where(causal, jnp.exp(jnp.where(causal, g - g.T, 0.0)), 0.0) for g in gs]
    kbs = [kh * bh for kh, bh in zip(ks, bs)]
    kts = [kh.T for kh in ks]
    neg_lowers = [jnp.where(strict, -(idot(kb, kt) * d), 0.0) for kb, kt, d in zip(kbs, kts, decays)]
    qks = [jnp.where(causal, idot(qh, kt) * d, 0.0) for qh, kt, d in zip(qs, kts, decays)]
    rhss = [jnp.concatenate([vh * bh, kb * jnp.exp(g)], axis=1) for vh, bh, kb, g in zip(vs, bs, kbs, gs)]
    if inv is None:
        invs = _unit_lower_inverses(neg_lowers)
        sols = [idot.nn(m, r) for m, r in zip(invs, rhss)]
    else:
        sols = [_solve_with_inverse(inv[:, sl], n, r) for sl, n, r in zip(sls, neg_lowers, rhss)]
    g_lasts = [jnp.sum(jnp.where(is_last, g, 0.0), axis=0, keepdims=True) for g in gs]
    outs = [[s[:, :HEAD_DIM] for s in sols], [s[:, HEAD_DIM:] for s in sols], qks,
            [kh * jnp.exp(gl - g) for kh, gl, g in zip(ks, g_lasts, gs)], [qh * jnp.exp(g) for qh, g in zip(qs, gs)]]
    if inv is None:
        outs.append(invs)
    return tuple(jnp.concatenate(o, axis=1) for o in outs)


def f_gdn_post(o, z, gain):
    z = z.astype(F32)
    parts = []
    for h in range(GDN_HEADS):
        oh = o[:, h * HEAD_DIM:(h + 1) * HEAD_DIM]
        parts.append(oh * lax.rsqrt(jnp.mean(oh * oh, axis=-1, keepdims=True) + EPS) * gain)
    return (jnp.concatenate(parts, axis=1) * _silu(z),)


def f_mem_attn(q, k, v):
    q = q.astype(F32)
    lane_head = _iota2((1, MEM_WIDTH), 1) // MEM_HEAD_DIM
    kt = k.astype(F32).T
    masks = [(lane_head == h).astype(F32) for h in range(MEM_HEADS)]
    logits = [bdot(q * mask, kt) * (MEM_HEAD_DIM ** -0.5) for mask in masks]
    ps = [jnp.exp(s - jnp.max(s, axis=-1, keepdims=True)) for s in logits]
    ps = [p / jnp.sum(p, axis=-1, keepdims=True) for p in ps]
    outs = [bdot(p, v) * mask for p, mask in zip(ps, masks)]
    return ((outs[0] + outs[1]) + (outs[2] + outs[3]),)


def f_loss(y, t):
    d = y - t
    return (d * d,)


def conv_fwd(proj, w8, *, width, tile=512):
    n_rows = proj.shape[0]
    tile = min(tile, n_rows)

    def body(x_ref, halo_ref, w_ref, o_ref):
        i = pl.program_id(0)
        halo = jnp.where(i > 0, halo_ref[...].astype(F32), 0.0)
        xs = jnp.concatenate([halo, x_ref[...].astype(F32)], axis=0)
        acc = xs[8:] * w_ref[3:4, :]
        for j in range(CONV_WIDTH - 1):
            acc = acc + pltpu.roll(xs, CONV_WIDTH - 1 - j, 0)[8:] * w_ref[j:j + 1, :]
        o_ref[...] = acc

    return _pcall(body, name="gdn_conv_fwd", out_shape=jax.ShapeDtypeStruct((n_rows, width), F32),
                  grid=(n_rows // tile,),
                  in_specs=[pl.BlockSpec((tile, width), lambda i: (i, 0)),
                            pl.BlockSpec((8, width), lambda i: (jnp.maximum(i * (tile // 8) - 1, 0), 0)),
                            pl.BlockSpec((8, width), lambda i: (0, 0))],
                  out_specs=pl.BlockSpec((tile, width), lambda i: (i, 0)), sem=("parallel",))(proj, proj, w8)


def conv_bwd(proj, w8, dy, *, width, tile=512):
    n_rows = proj.shape[0]
    tile = min(tile, n_rows)
    n = n_rows // tile

    def body(x_ref, xhalo_ref, w_ref, dy_ref, dyhalo_ref, dx_ref, dw_ref):
        i = pl.program_id(0)
        dy = dy_ref[...]
        after = jnp.where(i < n - 1, dyhalo_ref[...], 0.0)
        ds = jnp.concatenate([dy, after], axis=0)
        dx = dy * w_ref[3:4, :]
        for j in range(CONV_WIDTH - 1):
            shift = CONV_WIDTH - 1 - j
            dx = dx + pltpu.roll(ds, tile + 8 - shift, 0)[:tile] * w_ref[j:j + 1, :]
        dx_ref[...] = dx.astype(dx_ref.dtype)
        halo = jnp.where(i > 0, xhalo_ref[...].astype(F32), 0.0)
        xs = jnp.concatenate([halo, x_ref[...].astype(F32)], axis=0)
        rows = [jnp.sum(dy * pltpu.roll(xs, CONV_WIDTH - 1 - j, 0)[8:], axis=0, keepdims=True)
                for j in range(CONV_WIDTH - 1)]
        rows.append(jnp.sum(dy * xs[8:], axis=0, keepdims=True))
        dw = jnp.concatenate(rows + [jnp.zeros((8 - CONV_WIDTH, width), F32)], axis=0)

        @pl.when(i == 0)
        def _():
            dw_ref[...] = dw

        @pl.when(i > 0)
        def _():
            dw_ref[...] += dw

    t8 = tile // 8
    return _pcall(body, name="gdn_conv_bwd",
                  out_shape=[jax.ShapeDtypeStruct((n_rows, width), BF16), jax.ShapeDtypeStruct((8, width), F32)],
                  grid=(n,),
                  in_specs=[pl.BlockSpec((tile, width), lambda i: (i, 0)),
                            pl.BlockSpec((8, width), lambda i: (jnp.maximum(i * t8 - 1, 0), 0)),
                            pl.BlockSpec((8, width), lambda i: (0, 0)),
                            pl.BlockSpec((tile, width), lambda i: (i, 0)),
                            pl.BlockSpec((8, width), lambda i: (jnp.minimum((i + 1) * t8, n * t8 - 1), 0))],
                  out_specs=[pl.BlockSpec((tile, width), lambda i: (i, 0)), pl.BlockSpec((8, width), lambda i: (0, 0))],
                  sem=("arbitrary",))(proj, proj, w8, dy, dy)


def gdn_scan_fwd(u, w, qk, kt, qh, gc):
    n_rows = u.shape[0]
    C, n = GDN_CHUNK, u.shape[0] // GDN_CHUNK

    def body(u_ref, w_ref, qk_ref, kt_ref, qh_ref, gc_ref, o_ref, vn_ref, sin_ref, st):
        @pl.when(pl.program_id(0) == 0)
        def _():
            st[...] = jnp.zeros_like(st)

        sin_ref[0] = st[...]
        sls = [slice(h * HEAD_DIM, (h + 1) * HEAD_DIM) for h in range(GDN_HEADS)]
        states = [st[sl, :] for sl in sls]
        v_news = [u_ref[:, sl] - sdot(w_ref[:, sl], s) for sl, s in zip(sls, states)]
        from_state = [sdot(qh_ref[:, sl], s) for sl, s in zip(sls, states)]
        for sl, a, v_new in zip(sls, from_state, v_news):
            o_ref[:, sl] = a + sdot(qk_ref[:, sl], v_new)
            vn_ref[:, sl] = v_new
        for sl, s, v_new in zip(sls, states, v_news):
            st[sl, :] = s * jnp.exp(gc_ref[C - 1:C, sl]) + sdot.tn(kt_ref[:, sl], v_new)

    blk = pl.BlockSpec((C, GDN_WIDTH), lambda i: (i, 0))
    return _pcall(body, name="gdn_scan_fwd",
                  out_shape=[jax.ShapeDtypeStruct((n_rows, GDN_WIDTH), F32), jax.ShapeDtypeStruct((n_rows, GDN_WIDTH), F32),
                             jax.ShapeDtypeStruct((n, GDN_WIDTH, HEAD_DIM), F32)],
                  grid=(n,), in_specs=[blk] * 6,
                  out_specs=[blk, blk, pl.BlockSpec((1, GDN_WIDTH, HEAD_DIM), lambda i: (i, 0, 0))],
                  scratch=[pltpu.VMEM((GDN_WIDTH, HEAD_DIM), F32)], sem=("arbitrary",))(u, w, qk, kt, qh, gc)


def gdn_scan_bwd(do, w, qk, kt, qh, gc, vn, sin):
    n_rows = do.shape[0]
    C, n = GDN_CHUNK, do.shape[0] // GDN_CHUNK

    def body(do_ref, w_ref, qk_ref, kt_ref, qh_ref, gc_ref, vn_ref, sin_ref,
             du_ref, dw_ref, dqk_ref, dkt_ref, dqh_ref, dgl_ref, dst):
        @pl.when(pl.program_id(0) == 0)
        def _():
            dst[...] = jnp.zeros_like(dst)

        sls = [slice(h * HEAD_DIM, (h + 1) * HEAD_DIM) for h in range(GDN_HEADS)]
        dvns = [sdot.tn(qk_ref[:, sl], do_ref[:, sl]) + sdot(kt_ref[:, sl], dst[sl, :]) for sl in sls]
        for sl, dvn in zip(sls, dvns):
            du_ref[:, sl] = dvn
            dw_ref[:, sl] = -sdot.nt(dvn, sin_ref[0, sl, :])
        for sl in sls:
            dqk_ref[:, sl] = sdot.nt(do_ref[:, sl], vn_ref[:, sl])
            dkt_ref[:, sl] = sdot.nt(vn_ref[:, sl], dst[sl, :])
            dqh_ref[:, sl] = sdot.nt(do_ref[:, sl], sin_ref[0, sl, :])
        for sl, dvn in zip(sls, dvns):
            ds_out = dst[sl, :]
            e = jnp.exp(gc_ref[C - 1:C, sl])
            dgl = jnp.sum(ds_out * sin_ref[0, sl, :], axis=0, keepdims=True) * e
            dgl_ref[:, sl] = jnp.broadcast_to(dgl, (8, HEAD_DIM))
            dst[sl, :] = sdot.tn(qh_ref[:, sl], do_ref[:, sl]) + e * ds_out - sdot.tn(w_ref[:, sl], dvn)

    blk = pl.BlockSpec((C, GDN_WIDTH), lambda i: (n - 1 - i, 0))
    row = jax.ShapeDtypeStruct((n_rows, GDN_WIDTH), F32)
    return _pcall(body, name="gdn_scan_bwd",
                  out_shape=[row] * 5 + [jax.ShapeDtypeStruct((n * 8, GDN_WIDTH), F32)],
                  grid=(n,), in_specs=[blk] * 7 + [pl.BlockSpec((1, GDN_WIDTH, HEAD_DIM), lambda i: (n - 1 - i, 0, 0))],
                  out_specs=[blk] * 5 + [pl.BlockSpec((8, GDN_WIDTH), lambda i: (n - 1 - i, 0))],
                  scratch=[pltpu.VMEM((GDN_WIDTH, HEAD_DIM), F32)], sem=("arbitrary",))(do, w, qk, kt, qh, gc, vn, sin)


def gdn_intra_bwd(q, k, v, gc, beta, inv, cts, dgl):
    n_rows = q.shape[0]
    C = GDN_CHUNK

    def body(*refs):
        ins = [r[...] for r in refs[:5]]
        inv_v = refs[5][...]
        ct = tuple(r[...] for r in refs[6:11])
        dgl_v = refs[11][...]
        _, vjp = jax.vjp(lambda *a: f_gdn_intra(*a, inv=inv_v), *ins)
        grads = list(vjp(ct))
        last = _iota2((C, GDN_WIDTH), 0) == C - 1
        grads[3] = grads[3] + jnp.where(last, jnp.broadcast_to(dgl_v[0:1, :], (C, GDN_WIDTH)), 0.0)
        for o_ref, g in zip(refs[12:], grads):
            o_ref[...] = g

    blk = pl.BlockSpec((C, GDN_WIDTH), lambda i: (i, 0))
    return _pcall(body, name="gdn_intra_bwd", out_shape=[jax.ShapeDtypeStruct((n_rows, GDN_WIDTH), F32)] * 5,
                  grid=(n_rows // C,), in_specs=[blk] * 11 + [pl.BlockSpec((8, GDN_WIDTH), lambda i: (i, 0))],
                  out_specs=[blk] * 5, sem=("parallel",))(q, k, v, gc, beta, inv, *cts, dgl)


def fox_gate_fwd(f, b_f):
    n_rows = f.shape[0]
    T = LANES

    def body(f_ref, b_ref, cb_ref, carry):
        @pl.when(pl.program_id(0) == 0)
        def _():
            carry[...] = jnp.zeros_like(carry)

        c = fdot(_tri(T).astype(F32), _log_sigmoid(f_ref[...] + b_ref[...])) + carry[...]
        carry[...] = c[T - 1:T, :]
        cb_ref[...] = fdot(c, _head_sel(0))

    return _pcall(body, name="fox_gate_fwd", out_shape=jax.ShapeDtypeStruct((n_rows, FOX_WIDTH), F32),
                  grid=(n_rows // T,),
                  in_specs=[pl.BlockSpec((T, LANES), lambda i: (i, 0)), pl.BlockSpec((1, LANES), lambda i: (0, 0))],
                  out_specs=pl.BlockSpec((T, FOX_WIDTH), lambda i: (i, 0)),
                  scratch=[pltpu.VMEM((1, LANES), F32)], sem=("arbitrary",))(f, b_f)


def fox_gate_bwd(f, b_f, dcrow, dcb):
    n_rows = f.shape[0]
    T = LANES
    n = n_rows // T

    def body(f_ref, b_ref, dc_ref, dcb_ref, df_ref, db_ref, carry):
        i = pl.program_id(0)

        @pl.when(i == 0)
        def _():
            carry[...] = jnp.zeros_like(carry)

        rows = [dc_ref[h] for h in range(FOX_HEADS)] + [jnp.zeros((T - FOX_HEADS, T), F32)]
        first_lane = (_iota2((FOX_WIDTH, LANES), 0) == _iota2((FOX_WIDTH, LANES), 1) * HEAD_DIM).astype(F32)
        dc = jnp.concatenate(rows, axis=0).T + fdot(dcb_ref[...], first_lane)
        dlog = fdot.tn(_tri(T).astype(F32), dc) + carry[...]
        carry[...] = dlog[0:1, :]
        df = dlog * (1.0 - _sigmoid(f_ref[...] + b_ref[...]))
        df_ref[...] = df
        db = jnp.sum(df, axis=0, keepdims=True)

        @pl.when(i == 0)
        def _():
            db_ref[...] = db

        @pl.when(i > 0)
        def _():
            db_ref[...] += db

    return _pcall(body, name="fox_gate_bwd",
                  out_shape=[jax.ShapeDtypeStruct((n_rows, LANES), F32), jax.ShapeDtypeStruct((1, LANES), F32)],
                  grid=(n,),
                  in_specs=[pl.BlockSpec((T, LANES), lambda i: (n - 1 - i, 0)), pl.BlockSpec((1, LANES), lambda i: (0, 0)),
                            pl.BlockSpec((FOX_HEADS, 1, T), lambda i: (0, 0, n - 1 - i)),
                            pl.BlockSpec((T, FOX_WIDTH), lambda i: (n - 1 - i, 0))],
                  out_specs=[pl.BlockSpec((T, LANES), lambda i: (n - 1 - i, 0)), pl.BlockSpec((1, LANES), lambda i: (0, 0))],
                  scratch=[pltpu.VMEM((1, LANES), F32)], sem=("arbitrary",))(f, b_f, dcrow, dcb)


FOX_AUG = 2 * HEAD_DIM


def _fox_tiles(n_rows):
    return min(1024, n_rows), min(512, n_rows)


def _fox_pairs(n_rows, query_major):
    tq, tk = _fox_tiles(n_rows)
    nq, r = n_rows // tq, tq // tk
    if query_major:
        pairs = [(i, j) for i in range(nq) for j in range(r * (i + 1))]
    else:
        pairs = [(i, j) for j in range(nq * r) for i in range(j // r, nq)]
    return jnp.asarray([p[0] for p in pairs], jnp.int32), jnp.asarray([p[1] for p in pairs], jnp.int32)


def fox_augment(x, cb, query_side):
    def fn(xt, ct):
        xt = xt.astype(F32)
        lane = _iota2((xt.shape[0], HEAD_DIM), 1)
        parts = []
        for h in range(FOX_HEADS):
            sl = slice(h * HEAD_DIM, (h + 1) * HEAD_DIM)
            c = ct[:, sl]
            hi = c.astype(BF16).astype(F32)
            mid = (c - hi).astype(BF16).astype(F32)
            lo = (c - hi - mid).astype(BF16).astype(F32)
            terms = jnp.where(lane % 3 == 0, hi, jnp.where(lane % 3 == 1, mid, lo))
            if query_side:
                extra = jnp.where(lane < 3, terms, jnp.where(lane < 6, 1.0, 0.0))
                parts += [xt[:, sl] * (HEAD_DIM ** -0.5), extra]
            else:
                extra = jnp.where(lane < 3, 1.0, jnp.where(lane < 6, -terms, 0.0))
                parts += [xt[:, sl], extra]
        return (jnp.concatenate(parts, axis=1),)

    return rowwise(fn, [(x, FOX_WIDTH, 0), cb], [], [(FOX_HEADS * FOX_AUG, BF16)], tile=ROW_TILE,
                   name="fox_augment_q" if query_side else "fox_augment_k")[0]


def _pcall_tables(body, *, name, out_shape, grid, tables, in_specs, out_specs, scratch, sem):
    spec = pltpu.PrefetchScalarGridSpec(num_scalar_prefetch=len(tables), grid=grid, in_specs=in_specs, out_specs=out_specs,
                                        scratch_shapes=list(scratch))
    return pl.pallas_call(body, name=name, out_shape=out_shape, grid_spec=spec,
                          compiler_params=pltpu.CompilerParams(vmem_limit_bytes=VMEM_LIMIT_BYTES, dimension_semantics=sem))


def _fox_logits(qa, ka, offset):
    s = bdot.nt(qa, ka)
    if offset is not None:
        s = jnp.where(_iota2(s.shape, 0) + offset >= _iota2(s.shape, 1), s, NEG_INF)
    return s


def _fox_p_ds(offset, qa_ref, ka_ref, v_ref, o_ref, lse_ref, do_ref):
    s = _fox_logits(qa_ref[...], ka_ref[...], offset)
    p = jnp.exp(s - jnp.tile(lse_ref[...], (1, s.shape[1] // LANES)))
    d_o = do_ref[...].astype(F32)
    delta = jnp.sum(d_o * o_ref[...].astype(F32), axis=-1, keepdims=True)
    return p, p * (bdot.nt(d_o, v_ref[...]) - delta), d_o


def _fox_on_diagonal(i, j, r, tk, step):
    @pl.when(j < r * i)
    def _():
        step(None)

    for m in range(r):
        @pl.when(j == r * i + m)
        def _(m=m):
            step(-m * tk)


def _fox_specs(tq, tk, do_col):
    qaspec = pl.BlockSpec((tq, FOX_AUG), lambda h, p, it, jt: (it[p], h))
    qspec = pl.BlockSpec((tq, HEAD_DIM), lambda h, p, it, jt: (it[p], h))
    dospec = pl.BlockSpec((tq, HEAD_DIM), lambda h, p, it, jt: (it[p], do_col + h))
    kaspec = pl.BlockSpec((tk, FOX_AUG), lambda h, p, it, jt: (jt[p], h))
    kspec = pl.BlockSpec((tk, HEAD_DIM), lambda h, p, it, jt: (jt[p], h))
    vspec = pl.BlockSpec((tk, HEAD_DIM), lambda h, p, it, jt: (jt[p], FOX_HEADS + h))
    cspec = pl.BlockSpec((1, 1, tk), lambda h, p, it, jt: (h, 0, jt[p]))
    return qaspec, qspec, dospec, kaspec, kspec, vspec, cspec


def fox_fwd(qa, kv, ka):
    n_rows = kv.shape[0]
    tq, tk = _fox_tiles(n_rows)
    r = tq // tk
    tables = _fox_pairs(n_rows, True)

    def body(it, jt, qa_ref, ka_ref, v_ref, o_ref, lse_ref, m_sc, l_sc, acc):
        i, j = it[pl.program_id(1)], jt[pl.program_id(1)]

        @pl.when(j == 0)
        def _():
            m_sc[...] = jnp.full(m_sc.shape, NEG_INF, F32)
            l_sc[...] = jnp.zeros_like(l_sc)
            acc[...] = jnp.zeros_like(acc)

        def step(offset):
            s = _fox_logits(qa_ref[...], ka_ref[...], offset)
            m_old = m_sc[...]
            m_new = jnp.maximum(m_old, jnp.max(s, axis=-1, keepdims=True))
            alpha = jnp.exp(m_old - m_new)
            p = jnp.exp(s - jnp.tile(m_new, (1, tk // LANES)))
            l_sc[...] = l_sc[...] * alpha + jnp.sum(p, axis=-1, keepdims=True)
            acc[...] = acc[...] * alpha + bdot(p, v_ref[...])
            m_sc[...] = m_new

        _fox_on_diagonal(i, j, r, tk, step)

        @pl.when(j == r * i + r - 1)
        def _():
            o_ref[...] = (acc[...] / l_sc[...]).astype(o_ref.dtype)
            lse_ref[...] = m_sc[...] + jnp.log(l_sc[...])

    qaspec, qspec, _, kaspec, _, vspec, _ = _fox_specs(tq, tk, 0)
    return _pcall_tables(body, name="fox_fwd",
                         out_shape=[jax.ShapeDtypeStruct((n_rows, FOX_WIDTH), BF16), jax.ShapeDtypeStruct((n_rows, FOX_WIDTH), F32)],
                         grid=(FOX_HEADS, tables[0].shape[0]), tables=tables,
                         in_specs=[qaspec, kaspec, vspec], out_specs=[qspec, qspec],
                         scratch=[pltpu.VMEM((tq, HEAD_DIM), F32)] * 3, sem=("parallel", "arbitrary"))(*tables, qa, ka, kv)


def fox_bwd_dq(qa, kv, ka, o, lse, do, prev=None):
    do, _, do_col = do
    do_col *= FOX_HEADS
    n_rows = kv.shape[0]
    tq, tk = _fox_tiles(n_rows)
    r = tq // tk
    n_prev = 0 if prev is None else 1
    tables = _fox_pairs(n_rows, True)

    def body(it, jt, qa_ref, ka_ref, k_ref, v_ref, o_ref, lse_ref, do_ref, *rest):
        dq_ref, drow_ref, acc, rows = rest[n_prev:]
        i, j = it[pl.program_id(1)], jt[pl.program_id(1)]

        @pl.when(j == 0)
        def _():
            acc[...] = jnp.zeros_like(acc)
            rows[...] = jnp.zeros_like(rows)

        def step(offset):
            _, ds, _ = _fox_p_ds(offset, qa_ref, ka_ref, v_ref, o_ref, lse_ref, do_ref)
            acc[...] += bdot(ds, k_ref[...])
            rows[...] += jnp.sum(ds, axis=-1, keepdims=True)

        _fox_on_diagonal(i, j, r, tk, step)

        @pl.when(j == r * i + r - 1)
        def _():
            dq_ref[...] = (acc[...] * (HEAD_DIM ** -0.5)).astype(dq_ref.dtype)
            drow_ref[...] = rows[...] + rest[0][...] if n_prev else rows[...]

    qaspec, qspec, dospec, kaspec, kspec, vspec, _ = _fox_specs(tq, tk, do_col)
    return _pcall_tables(body, name="fox_bwd_dq" + ("_acc" if n_prev else ""),
                         out_shape=[jax.ShapeDtypeStruct((n_rows, FOX_WIDTH), BF16), jax.ShapeDtypeStruct((n_rows, FOX_WIDTH), F32)],
                         grid=(FOX_HEADS, tables[0].shape[0]), tables=tables,
                         in_specs=[qaspec, kaspec, kspec, vspec, qspec, qspec, dospec] + [qspec] * n_prev,
                         out_specs=[qspec, qspec], scratch=[pltpu.VMEM((tq, HEAD_DIM), F32)] * 2,
                         sem=("parallel", "arbitrary"))(*tables, qa, ka, kv, kv, o, lse, do, *([prev] if n_prev else []))


def fox_bwd_dkv(qa, kv, ka, o, lse, do, prev=None):
    do, _, do_col = do
    do_col *= FOX_HEADS
    n_rows = kv.shape[0]
    tq, tk = _fox_tiles(n_rows)
    nq, r = n_rows // tq, tq // tk
    n_prev = 0 if prev is None else 3
    tables = _fox_pairs(n_rows, False)

    def body(it, jt, qa_ref, ka_ref, v_ref, o_ref, lse_ref, do_ref, *rest):
        prev_refs = rest[:n_prev]
        dk_ref, dv_ref, dc_ref, dk_acc, dv_acc, dc_acc = rest[n_prev:]
        i, j = it[pl.program_id(1)], jt[pl.program_id(1)]

        @pl.when(j >= r * i)
        def _():
            dk_acc[...] = jnp.zeros_like(dk_acc)
            dv_acc[...] = jnp.zeros_like(dv_acc)
            dc_acc[...] = jnp.zeros_like(dc_acc)

        def step(offset):
            p, ds, d_o = _fox_p_ds(offset, qa_ref, ka_ref, v_ref, o_ref, lse_ref, do_ref)
            dv_acc[...] += bdot.tn(p, d_o)
            dk_acc[...] += bdot.tn(ds, qa_ref[:, :HEAD_DIM])
            dc_acc[...] -= jnp.sum(ds, axis=0, keepdims=True)

        _fox_on_diagonal(i, j, r, tk, step)

        @pl.when(i == nq - 1)
        def _():
            dk, dv, dc = dk_acc[...], dv_acc[...], dc_acc[...]
            if n_prev:
                dk, dv, dc = dk + prev_refs[0][...], dv + prev_refs[1][...], dc + prev_refs[2][0]
            dk_ref[...] = dk
            dv_ref[...] = dv
            dc_ref[0] = dc

    qaspec, qspec, dospec, kaspec, kspec, vspec, cspec = _fox_specs(tq, tk, do_col)
    return _pcall_tables(body, name="fox_bwd_dkv" + ("_acc" if n_prev else ""),
                         out_shape=[jax.ShapeDtypeStruct((n_rows, FOX_WIDTH), F32), jax.ShapeDtypeStruct((n_rows, FOX_WIDTH), F32),
                                    jax.ShapeDtypeStruct((FOX_HEADS, 1, n_rows), F32)],
                         grid=(FOX_HEADS, tables[0].shape[0]), tables=tables,
                         in_specs=[qaspec, kaspec, vspec, qspec, qspec, dospec] + [kspec, kspec, cspec][:n_prev],
                         out_specs=[kspec, kspec, cspec],
                         scratch=[pltpu.VMEM((tk, HEAD_DIM), F32), pltpu.VMEM((tk, HEAD_DIM), F32), pltpu.VMEM((1, tk), F32)],
                         sem=("parallel", "arbitrary"))(*tables, qa, ka, kv, o, lse, do, *(prev or ()))


def loss_head(h, gain, target, *, tile=512):
    n_rows, d = h.shape
    tile = min(tile, n_rows)

    def body(h_ref, g_ref, t_ref, part_ref, dy_ref):
        (y,) = f_rmsnorm(h_ref[...], g_ref[...])
        diff = y - t_ref[...]
        dy_ref[...] = diff * (1.0 / d)
        part = jnp.sum(diff * diff, axis=0, keepdims=True)
        first = pl.program_id(0) == 0

        @pl.when(first)
        def _():
            part_ref[...] = part

        @pl.when(jnp.logical_not(first))
        def _():
            part_ref[...] += part

    blk = pl.BlockSpec((tile, d), lambda i: (i, 0))
    one = pl.BlockSpec((1, d), lambda i: (0, 0))
    return _pcall(body, name="loss_head",
                  out_shape=[jax.ShapeDtypeStruct((1, d), F32), jax.ShapeDtypeStruct((n_rows, d), F32)],
                  grid=(n_rows // tile,), in_specs=[blk, one, blk], out_specs=[one, blk], sem=("arbitrary",))(h, gain, target)


def adamw(w, g, m, v, *, name):
    shape = w.shape
    cols = shape[-1] if w.ndim >= 2 else w.size
    rows = w.size // cols
    tile = _pick(rows, (256, 128, 64, 32, 16, 8))
    as2d = lambda a: a.reshape(rows, cols)

    def body(w_ref, g_ref, m_ref, v_ref, d_ref, nm_ref, nv_ref):
        g_ = g_ref[...]
        m_ = ADAM_B1 * m_ref[...] + (1.0 - ADAM_B1) * g_
        v_ = ADAM_B2 * v_ref[...] + (1.0 - ADAM_B2) * (g_ * g_)
        m_hat = m_ / (1.0 - ADAM_B1 ** ADAM_STEP)
        v_hat = v_ / (1.0 - ADAM_B2 ** ADAM_STEP)
        d_ref[...] = -ADAM_LR * (m_hat / (jnp.sqrt(v_hat) + ADAM_EPS) + ADAM_WD * w_ref[...])
        nm_ref[...] = m_
        nv_ref[...] = v_

    blk = pl.BlockSpec((tile, cols), lambda i: (i, 0))
    outs = _pcall(body, name=name, out_shape=[jax.ShapeDtypeStruct((rows, cols), F32)] * 3, grid=(rows // tile,),
                  in_specs=[blk] * 4, out_specs=[blk] * 3, sem=("parallel",))(as2d(w), as2d(g), as2d(m), as2d(v))
    return tuple(o.reshape(shape) for o in outs)


def sum_leading(a, *, name):
    p, r, c = a.shape
    tile = _pick(r, (256, 128, 64, 32, 16, 8))

    def body(a_ref, o_ref):
        total = a_ref[0].astype(F32)
        for k in range(1, p):
            total = total + a_ref[k].astype(F32)
        o_ref[...] = total

    return _pcall(body, name=name, out_shape=jax.ShapeDtypeStruct((r, c), F32), grid=(r // tile,),
                  in_specs=[pl.BlockSpec((p, tile, c), lambda i: (0, i, 0))],
                  out_specs=pl.BlockSpec((tile, c), lambda i: (i, 0)), sem=("parallel",))(a)


_HBM = pl.BlockSpec(memory_space=pltpu.HBM)


def _comm_call(body, *, name, out_shape, n_in, scratch):
    return pl.pallas_call(body, name=name, out_shape=out_shape, in_specs=[_HBM] * n_in, out_specs=_HBM,
                          scratch_shapes=scratch,
                          compiler_params=pltpu.CompilerParams(has_side_effects=True))


def all_gather8(a, *, name):
    m_per, n = a.shape

    def body(x_ref, out_ref, send_sems, recv_sems, local_sem):
        x, y, c = lax.axis_index("x"), lax.axis_index("y"), lax.axis_index("c")
        me, sibling = (x, y, c), (x, y, 1 - c)
        chips = [(1 - x, y), (x, 1 - y), (1 - x, 1 - y)]

        def rows(px, py, pc):
            return out_ref.at[pl.ds((4 * px + 2 * py + pc) * m_per, m_per), :]

        def copy(k, block, to, src=None):
            return pltpu.make_async_remote_copy(
                src_ref=rows(*block) if src is None else src, dst_ref=rows(*block),
                send_sem=send_sems.at[k], recv_sem=recv_sems.at[k], device_id=to, device_id_type=MESH)

        mine = pltpu.make_async_copy(x_ref, rows(*me), local_sem)
        mine.start()
        first = [copy(0, me, sibling, src=x_ref)]
        first += [copy(1 + j, me, (*chip, c), src=x_ref) for j, chip in enumerate(chips)]
        for cp in first:
            cp.start()
        passed = [copy(4 + j, (*chip, c), sibling) for j, chip in enumerate(chips)]
        for j, chip in enumerate(chips):
            copy(1 + j, (*chip, c), me).wait_recv()
            passed[j].start()
        copy(0, sibling, me).wait_recv()
        for j, chip in enumerate(chips):
            copy(4 + j, (*chip, 1 - c), me).wait_recv()
        for cp in first + passed:
            cp.wait_send()
        mine.wait()

    return _comm_call(body, name=name, out_shape=jax.ShapeDtypeStruct((N_DEV * m_per, n), a.dtype), n_in=1,
                      scratch=[pltpu.SemaphoreType.DMA((7,)), pltpu.SemaphoreType.DMA((7,)), pltpu.SemaphoreType.DMA])(a)


PACK_COLS = 1024
PACK_ROW_MULTIPLE = 32

WEIGHT_NAMES = ["ffn1_norm", "ffn1_w_gate_up", "ffn1_w_down", "mix_norm", "ffn2_norm", "ffn2_w_gate_up", "ffn2_w_down",
                "gdn_w_in", "gdn_conv", "gdn_A_log", "gdn_dt_bias", "gdn_out_norm", "fox_w_in", "w_out", "mem_norm",
                "mem_w_kv", "kv_norm", "kv_w", "kv_b_f", "final_norm"]
SHARDED = [("ffn1_w_gate_up", 2), ("ffn1_w_down", 1), ("ffn2_w_gate_up", 2), ("ffn2_w_down", 1), ("gdn_w_in", 2),
           ("gdn_conv", 2), ("fox_w_in", 1), ("w_out", 1), ("mem_w_kv", 1), ("kv_w", 0)]
REPLICATED = [n for n in WEIGHT_NAMES if n not in dict(SHARDED)]


PACK_PIECE_ROWS = 16


def _rows_of(size):
    return -(-size // (PACK_COLS * PACK_PIECE_ROWS)) * PACK_PIECE_ROWS


def pack(pieces, dtype, row_multiple=PACK_ROW_MULTIPLE):
    bufs, total = [], 0
    for p in pieces:
        flat = p.astype(dtype).reshape(-1)
        rows = _rows_of(flat.size)
        bufs.append(jnp.pad(flat, (0, rows * PACK_COLS - flat.size)).reshape(rows, PACK_COLS))
        total += rows
    pad = -total % row_multiple
    if pad:
        bufs.append(jnp.zeros((pad, PACK_COLS), dtype))
    return jnp.concatenate(bufs, axis=0)


def unpack(buf, shapes):
    out, row = [], 0
    for shape in shapes:
        size = 1
        for s in shape:
            size *= s
        rows = _rows_of(size)
        out.append(buf[row:row + rows].reshape(-1)[:size].reshape(shape))
        row += rows
    return out


def _row(vec, width=None):
    vec = vec.astype(F32).reshape(1, -1)
    if width is not None and vec.shape[1] < width:
        vec = jnp.pad(vec, ((0, 0), (0, width - vec.shape[1])))
    return vec


ROW_TILE = 512
GDN_PROJ_WIDTH = 4 * GDN_WIDTH + MEM_WIDTH + LANES
GDN_Z_COL, GDN_QMEM_COL, GDN_AB_COL = 3, 4 * GDN_WIDTH // MEM_WIDTH, (4 * GDN_WIDTH + MEM_WIDTH) // LANES
FOX_QMEM_COL = FOX_WIDTH // MEM_WIDTH
KV_PAD_WIDTH = 2 * FOX_WIDTH + LANES


def rms_fwd(x, gain_row, out_dtype=BF16):
    return rowwise(f_rmsnorm, [x], [gain_row], [(x.shape[1], out_dtype)], tile=ROW_TILE, name="rms_fwd")[0]


def rms_bwd(x, gain_row, dy, dres=None):
    return rowwise_bwd(f_rmsnorm, [x], [gain_row], [dy], tile=ROW_TILE, name="rms_bwd", row_grads=[F32],
                       const_grads=[True], add=None if dres is None else {0: dres})


def _ffn_tiles(n_rows):
    return _pick(n_rows, (512, 256, 128)), _pick(FFN_HIDDEN, (1408, 256, 128))


def ffn_up_act(n, wgu):
    n_rows, d = n.shape
    tm, tn = _ffn_tiles(n_rows)
    nj = FFN_HIDDEN // tn

    def body(n_ref, wg_ref, wu_ref, gu_ref, act_ref):
        x = n_ref[...].astype(BF16)
        g = bdot.nn(x, wg_ref[...])
        u = bdot.nn(x, wu_ref[...])
        gu_ref[0] = g.astype(gu_ref.dtype)
        gu_ref[1] = u.astype(gu_ref.dtype)
        act_ref[...] = (_silu(g) * u).astype(act_ref.dtype)

    return _pcall(body, name="ffn_up_act",
                  out_shape=[jax.ShapeDtypeStruct((2, n_rows, FFN_HIDDEN), BF16), jax.ShapeDtypeStruct((n_rows, FFN_HIDDEN), BF16)],
                  grid=(nj, n_rows // tm),
                  in_specs=[pl.BlockSpec((tm, d), lambda j, i: (i, 0)), pl.BlockSpec((d, tn), lambda j, i: (0, j)),
                            pl.BlockSpec((d, tn), lambda j, i: (0, nj + j))],
                  out_specs=[pl.BlockSpec((2, tm, tn), lambda j, i: (0, i, j)), pl.BlockSpec((tm, tn), lambda j, i: (i, j))],
                  sem=("parallel", "parallel"))(n, wgu, wgu)


def ffn_down_dx_act(dh, wd, gu):
    n_rows, d = dh.shape
    tm, tn = _ffn_tiles(n_rows)

    def body(dh_ref, wd_ref, gu_ref, dgu_ref):
        dact = 0.5 * bdot.nt(dh_ref[...], wd_ref[...])
        gate, up = gu_ref[0].astype(F32), gu_ref[1].astype(F32)
        sg = _sigmoid(gate)
        dgu_ref[0] = (dact * up * (sg * (1.0 + gate * (1.0 - sg)))).astype(dgu_ref.dtype)
        dgu_ref[1] = (dact * (gate * sg)).astype(dgu_ref.dtype)

    blk = pl.BlockSpec((2, tm, tn), lambda j, i: (0, i, j))
    return _pcall(body, name="ffn_down_dx_act", out_shape=jax.ShapeDtypeStruct((2, n_rows, FFN_HIDDEN), BF16),
                  grid=(FFN_HIDDEN // tn, n_rows // tm),
                  in_specs=[pl.BlockSpec((tm, d), lambda j, i: (i, 0)), pl.BlockSpec((tn, d), lambda j, i: (j, 0)), blk],
                  out_specs=blk, sem=("parallel", "parallel"))(dh, wd, gu)


def ffn_fwd(h, gain_row, wgu, wd):
    n = rms_fwd(h, gain_row)
    gu, act = ffn_up_act(n, wgu)
    return mm(act, wd, scale=0.5, res=h, name="ffn_down"), (h, n, gu, act)


def ffn_bwd(dh, saved, gain_row, wgu, wd):
    h, n, gu, act = saved
    dgu = ffn_down_dx_act(dh, wd, gu)
    dwd = mm(act, dh, ta=True, scale=0.5, name="ffn_down_dw")
    dwgu = mm(n, dgu, ta=True, b_split=True, name="ffn_up_dw")
    dh, dgain = mm_rms_bwd(dgu, wgu, h, gain_row, dh, a_split=True, name="ffn_up_dx")
    return dh, dwgu, dwd, dgain


def gdn_fwd(proj, w8, a_row, dt_row, onorm_row):
    wide = [(GDN_WIDTH, F32)] * 5
    xc = conv_fwd(proj, w8, width=3 * GDN_WIDTH)
    q, k, v, gc, beta = rowwise(f_gdn_pre, [xc, (proj, LANES, GDN_AB_COL)], [a_row, dt_row], wide, tile=GDN_CHUNK,
                                name="gdn_pre_fwd")
    u, w, qk, kt, qh, inv = rowwise(f_gdn_intra, [q, k, v, gc, beta], [], wide + wide[:1], tile=GDN_CHUNK,
                                    name="gdn_intra_fwd")
    o, vn, sin = gdn_scan_fwd(u, w, qk, kt, qh, gc)
    main = rowwise(f_gdn_post, [o, (proj, GDN_WIDTH, GDN_Z_COL)], [onorm_row], [(GDN_WIDTH, BF16)], tile=ROW_TILE,
                   name="gdn_post_fwd")[0]
    return main, (xc, q, k, v, gc, beta, inv, w, qk, kt, qh, vn, sin, o)


def gdn_bwd(dmain, proj, saved, w8, a_row, dt_row, onorm_row):
    xc, q, k, v, gc, beta, inv, w, qk, kt, qh, vn, sin, o = saved
    do, dz, donorm = rowwise_bwd(f_gdn_post, [o, (proj, GDN_WIDTH, GDN_Z_COL)], [onorm_row], [dmain], tile=ROW_TILE,
                                 name="gdn_post_bwd", row_grads=[F32, BF16], const_grads=[True])
    du, dw, dqk, dkt, dqh, dgl = gdn_scan_bwd(do, w, qk, kt, qh, gc, vn, sin)
    dq, dk, dv, dgc, dbeta = gdn_intra_bwd(q, k, v, gc, beta, inv, (du, dw, dqk, dkt, dqh), dgl)
    dxc, dab, da, ddt = rowwise_bwd(f_gdn_pre, [xc, (proj, LANES, GDN_AB_COL)], [a_row, dt_row], [dq, dk, dv, dgc, dbeta],
                                    tile=GDN_CHUNK, name="gdn_pre_bwd", row_grads=[F32, BF16], const_grads=[True, True])
    dqkv, dw8 = conv_bwd(proj, w8, dxc, width=3 * GDN_WIDTH)
    return dqkv, dz, dab, dw8, da, ddt, donorm


def mem_fwd(q, kmem, vmem):
    return rowwise(f_mem_attn, [q], [kmem, vmem], [(MEM_WIDTH, BF16)], tile=ROW_TILE, name="mem_attn_fwd")[0]


def mem_bwd(q, kmem, vmem, dout):
    return rowwise_bwd(f_mem_attn, [q], [kmem, vmem], [dout], tile=ROW_TILE, name="mem_attn_bwd", row_grads=[BF16],
                       const_grads=[True, True])


def forward_backward(xs, mems, target, P):
    depth, n_a = 4, 2
    G = {}
    mem_gain = _row(P["mem_norm"])
    mem_n = rms_fwd(mems, mem_gain)
    h = xs
    saved = []
    shared = None
    for l in range(depth):
        h0 = h
        h1, s1 = ffn_fwd(h0, _row(P["ffn1_norm"][l]), P["ffn1_w_gate_up"][l], P["ffn1_w_down"][l])
        u = rms_fwd(h1, _row(P["mix_norm"][l]))
        kvm = mm(mem_n, P["mem_w_kv"][l], name="mem_kv")
        kmem, vmem = kvm[:, :MEM_WIDTH], kvm[:, MEM_WIDTH:]
        if l < n_a:
            gp = (P["conv8"][l], _row(P["gdn_A_log"][l], LANES), _row(P["gdn_dt_bias"][l], LANES), _row(P["gdn_out_norm"][l]))
            proj = mm(u, P["gdn_w_in_pad"][l], name="gdn_in")
            main, sm = gdn_fwd(proj, *gp)
            qm = (proj, MEM_WIDTH, GDN_QMEM_COL)
        else:
            proj = mm(u, P["fox_w_in"][l - n_a], out_dtype=BF16, name="fox_in")
            kv, ka, cb = shared
            qa = fox_augment(proj, cb, True)
            main, lse = fox_fwd(qa, kv, ka)
            sm = (main, lse, qa)
            qm = (proj, MEM_WIDTH, FOX_QMEM_COL)
        mo = mem_fwd(qm, kmem, vmem)
        cat = jnp.concatenate([main, mo], axis=1)
        h2 = mm(cat, P["w_out"][l], res=h1, name="mix_out")
        h3, s2 = ffn_fwd(h2, _row(P["ffn2_norm"][l]), P["ffn2_w_gate_up"][l], P["ffn2_w_down"][l])
        saved.append((s1, h1, u, kmem, vmem, proj, sm, qm, cat, s2))
        h = h3
        if l == n_a - 1:
            nkv = rms_fwd(h, _row(P["kv_norm"]))
            kv = mm(nkv, P["kv_w_pad"][:, :2 * FOX_WIDTH], out_dtype=BF16, name="fox_kv")
            f = mm(nkv, P["kv_w_pad"][:, 2 * FOX_WIDTH:], name="fox_f")
            bf_row = _row(P["kv_b_f"], LANES)
            cb = fox_gate_fwd(f, bf_row)
            shared = (kv, fox_augment(kv, cb, False), cb)
            kv_saved = (h, nkv, f, bf_row)

    part, dy = loss_head(h, _row(P["final_norm"]), target)
    dh, G["final_norm"] = rms_bwd(h, _row(P["final_norm"]), dy)

    per_layer = {n: [None] * depth for n in ("ffn1_norm", "ffn1_w_gate_up", "ffn1_w_down", "mix_norm", "ffn2_norm",
                                             "ffn2_w_gate_up", "ffn2_w_down", "w_out", "mem_w_kv")}
    gdn_g = {n: [None] * n_a for n in ("gdn_w_in_pad", "conv8", "gdn_A_log", "gdn_dt_bias", "gdn_out_norm")}
    fox_g = [None] * (depth - n_a)
    dmem_n = None
    dkv_acc = dcb_acc = None
    for l in reversed(range(depth)):
        s1, h1, u, kmem, vmem, proj, sm, qm, cat, s2 = saved[l]
        if l == n_a - 1:
            hk, nkv, f, bf_row = kv_saved
            dk, dv, dcrow = dkv_acc
            df, dbf = fox_gate_bwd(f, bf_row, dcrow, dcb_acc)
            dp = jnp.concatenate([dk.astype(BF16), dv.astype(BF16), df.astype(BF16)], axis=1)
            G["kv_w_pad"] = mm(nkv, dp, ta=True, name="fox_kv_dw")
            G["kv_b_f"] = dbf
            dh, G["kv_norm"] = mm_rms_bwd(dp, P["kv_w_pad"], hk, _row(P["kv_norm"]), dh, name="fox_kv_dx")
        dh, per_layer["ffn2_w_gate_up"][l], per_layer["ffn2_w_down"][l], per_layer["ffn2_norm"][l] = ffn_bwd(
            dh, s2, _row(P["ffn2_norm"][l]), P["ffn2_w_gate_up"][l], P["ffn2_w_down"][l])
        dcat = mm(dh, P["w_out"][l], tb=True, out_dtype=BF16, name="mix_out_dx")
        per_layer["w_out"][l] = mm(cat, dh, ta=True, name="mix_out_dw")
        dqm, dkm, dvm = mem_bwd(qm, kmem, vmem, (dcat, MEM_WIDTH, FOX_QMEM_COL))
        dkvm = jnp.concatenate([dkm, dvm], axis=1)
        per_layer["mem_w_kv"][l] = mm(mem_n, dkvm, ta=True, name="mem_kv_dw")
        dmem_n = mm(dkvm, P["mem_w_kv"][l], tb=True, res=dmem_n, name="mem_kv_dx")
        dmain = (dcat, GDN_WIDTH, 0)
        if l < n_a:
            gp = (P["conv8"][l], _row(P["gdn_A_log"][l], LANES), _row(P["gdn_dt_bias"][l], LANES), _row(P["gdn_out_norm"][l]))
            dqkv, dz, dab, gdn_g["conv8"][l], gdn_g["gdn_A_log"][l], gdn_g["gdn_dt_bias"][l], gdn_g["gdn_out_norm"][l] = gdn_bwd(
                dmain, proj, sm, *gp)
            dproj = jnp.concatenate([dqkv, dz, dqm, dab], axis=1)
            gdn_g["gdn_w_in_pad"][l] = mm(u, dproj, ta=True, name="gdn_in_dw")
            w_in = P["gdn_w_in_pad"][l]
        else:
            o, lse, qa = sm
            kv, ka, _ = shared
            dq, dcb_acc = fox_bwd_dq(qa, kv, ka, o, lse, dmain, dcb_acc)
            dkv_acc = fox_bwd_dkv(qa, kv, ka, o, lse, dmain, dkv_acc)
            dproj = jnp.concatenate([dq, dqm], axis=1)
            fox_g[l - n_a] = mm(u, dproj, ta=True, name="fox_in_dw")
            w_in = P["fox_w_in"][l - n_a]
        dh, per_layer["mix_norm"][l] = mm_rms_bwd(dproj, w_in, h1, _row(P["mix_norm"][l]), dh, name="mix_in_dx")
        dh, per_layer["ffn1_w_gate_up"][l], per_layer["ffn1_w_down"][l], per_layer["ffn1_norm"][l] = ffn_bwd(
            dh, s1, _row(P["ffn1_norm"][l]), P["ffn1_w_gate_up"][l], P["ffn1_w_down"][l])

    (G["mem_norm"],) = rowwise_bwd(f_rmsnorm, [mems], [mem_gain], [dmem_n], tile=ROW_TILE, name="mem_norm_bwd",
                                   row_grads=[None], const_grads=[True])
    for n, v in per_layer.items():
        G[n] = jnp.stack(v)
    for n, v in gdn_g.items():
        G[n] = jnp.stack(v)
    G["fox_w_in"] = jnp.stack(fox_g)
    return part, dh, G


_GDN_O0 = 4 * GDN_WIDTH
_GDN_O1 = _GDN_O0 + 2 * GDN_HEADS
_KV_WIDTH = 2 * FOX_WIDTH + FOX_HEADS


def derived_weights(gdn_w_in, gdn_conv, kv_w=None):
    zeros = jnp.zeros(gdn_w_in.shape[:-1] + (LANES - 2 * GDN_HEADS,), gdn_w_in.dtype)
    out = dict(
        gdn_w_in_pad=jnp.concatenate([gdn_w_in[..., :_GDN_O0], gdn_w_in[..., _GDN_O1:], gdn_w_in[..., _GDN_O0:_GDN_O1], zeros], axis=-1),
        conv8=jnp.pad(gdn_conv.astype(F32), ((0, 0), (0, 8 - CONV_WIDTH), (0, 0))))
    if kv_w is not None:
        out["kv_w_pad"] = jnp.pad(kv_w, ((0, 0), (0, KV_PAD_WIDTH - _KV_WIDTH)))
    return out


def reference_layout(G):
    gp = G["gdn_w_in_pad"]
    out = dict(G)
    out["gdn_w_in"] = jnp.concatenate([gp[..., :_GDN_O0], gp[..., _GDN_O0 + MEM_WIDTH:_GDN_O0 + MEM_WIDTH + 2 * GDN_HEADS],
                                       gp[..., _GDN_O0:_GDN_O0 + MEM_WIDTH]], axis=-1)
    out["gdn_conv"] = G["conv8"][:, :CONV_WIDTH]
    out["kv_w"] = G["kv_w_pad"][:, :_KV_WIDTH]
    out["gdn_A_log"] = G["gdn_A_log"][:, 0, :GDN_HEADS]
    out["gdn_dt_bias"] = G["gdn_dt_bias"][:, 0, :GDN_HEADS]
    out["gdn_out_norm"] = G["gdn_out_norm"][:, 0, :]
    out["kv_b_f"] = G["kv_b_f"][0, :FOX_HEADS]
    for n in ("ffn1_norm", "mix_norm", "ffn2_norm"):
        out[n] = G[n][:, 0, :]
    for n in ("mem_norm", "kv_norm", "final_norm"):
        out[n] = G[n][0]
    return {n: out[n] for n in WEIGHT_NAMES}


EXCHANGED = [("ffn1_w_gate_up", 2, 0), ("ffn1_w_down", 1, 0), ("ffn2_w_gate_up", 2, 0), ("ffn2_w_down", 1, 0),
             ("gdn_w_in", 2, 0), ("fox_w_in", 1, 0), ("w_out", 1, 0), ("mem_w_kv", 1, 0), ("kv_w", 0, 1)]
GDN_IN_SHARD = (4 * GDN_WIDTH + 2 * GDN_HEADS + MEM_WIDTH) // N_CHIPS
GDN_IN_SLOT = 896


def _slab(ref, axis_slices):
    idx = [slice(None)] * len(ref.shape)
    for axis, (start, size) in axis_slices.items():
        idx[axis] = pl.ds(start, size)
    return ref.at[tuple(idx)]


def _comm_multi(body, *, name, n_in, out_shapes, scratch):
    return pl.pallas_call(body, name=name, out_shape=out_shapes, in_specs=[_HBM] * n_in, out_specs=[_HBM] * len(out_shapes),
                          scratch_shapes=scratch, compiler_params=pltpu.CompilerParams(has_side_effects=True))


def gather_shards(shards, layout):
    n = len(shards)
    fulls = [tuple(d * (N_CHIPS if a == sa else 1) for a, d in enumerate(s.shape)) for s, (sa, _) in zip(shards, layout)]
    n_sem = 9

    def pieces(w):
        sa, ha = layout[w]
        axis = 3 - sa - ha
        size = shards[w].shape[axis]
        unit = LANES if axis == 2 else 16
        first = -(-(size // 2) // unit) * unit
        return axis, [(0, first), (first, size - first)]

    def body(*refs):
        ins, outs = refs[:n], refs[n:2 * n]
        send_sems, recv_sems, local_sems = refs[2 * n:]
        x, y, c = lax.axis_index("x"), lax.axis_index("y"), lax.axis_index("c")
        me, sibling, x_nbr, y_nbr = (x, y, c), (x, y, 1 - c), (1 - x, y, c), (x, 1 - y, c)
        chip_x, chip_y, chip_d = (1 - x, y), (x, 1 - y), (1 - x, 1 - y)

        def region(w, chip, pc, piece=None):
            (sa, ha), shard = layout[w], shards[w].shape
            where = {sa: ((2 * chip[0] + chip[1]) * shard[sa], shard[sa]), ha: (pc * (shard[ha] // 2), shard[ha] // 2)}
            if piece is not None:
                axis, parts = pieces(w)
                where[axis] = parts[piece]
            return _slab(outs[w], where)

        def my_half(w):
            ha, shard = layout[w][1], shards[w].shape
            return _slab(ins[w], {ha: (c * (shard[ha] // 2), shard[ha] // 2)})

        def copy(w, k, where, to, src=None):
            return pltpu.make_async_remote_copy(
                src_ref=where if src is None else src, dst_ref=where, send_sem=send_sems.at[n_sem * w + k],
                recv_sem=recv_sems.at[n_sem * w + k], device_id=to, device_id_type=MESH)

        mine, sends = [], [[] for _ in range(n)]
        for w in range(n):
            mine.append(pltpu.make_async_copy(my_half(w), region(w, (x, y), c), local_sems.at[w]))
            mine[w].start()
            sends[w] = [copy(w, k, region(w, (x, y), c), to, src=my_half(w)) for k, to in enumerate((sibling, x_nbr, y_nbr))]
            for cp in sends[w]:
                cp.start()
        for w in range(n):
            copy(w, 1, region(w, chip_x, c), me).wait_recv()
            onward = [copy(w, 3, region(w, chip_x, c, 0), y_nbr), copy(w, 5, region(w, chip_x, c), sibling)]
            for cp in onward:
                cp.start()
            sends[w] += onward
            copy(w, 2, region(w, chip_y, c), me).wait_recv()
            onward = [copy(w, 4, region(w, chip_y, c, 1), x_nbr), copy(w, 6, region(w, chip_y, c), sibling)]
            for cp in onward:
                cp.start()
            sends[w] += onward
        for w in range(n):
            copy(w, 3, region(w, chip_d, c, 0), me).wait_recv()
            copy(w, 4, region(w, chip_d, c, 1), me).wait_recv()
            onward = [copy(w, 7, region(w, chip_d, c, 0), sibling), copy(w, 8, region(w, chip_d, c, 1), sibling)]
            for cp in onward:
                cp.start()
            sends[w] += onward
        for w in range(n):
            copy(w, 0, region(w, (x, y), 1 - c), me).wait_recv()
            copy(w, 5, region(w, chip_x, 1 - c), me).wait_recv()
            copy(w, 6, region(w, chip_y, 1 - c), me).wait_recv()
            copy(w, 7, region(w, chip_d, 1 - c, 0), me).wait_recv()
            copy(w, 8, region(w, chip_d, 1 - c, 1), me).wait_recv()
        for w in range(n):
            for cp in sends[w]:
                cp.wait_send()
            mine[w].wait()

    return _comm_multi(body, name="gather_shards", n_in=n,
                       out_shapes=[jax.ShapeDtypeStruct(f, s.dtype) for f, s in zip(fulls, shards)],
                       scratch=[pltpu.SemaphoreType.DMA((n_sem * n,)), pltpu.SemaphoreType.DMA((n_sem * n,)),
                                pltpu.SemaphoreType.DMA((n,))])(*shards)


def swap_other_halves(arrays, layout):
    n = len(arrays)
    halves = [tuple(d // 2 if a == ha else d for a, d in enumerate(g.shape)) for g, (_, ha) in zip(arrays, layout)]

    def body(*refs):
        ins, outs = refs[:n], refs[n:2 * n]
        send_sems, recv_sems = refs[2 * n:]
        x, y, c = lax.axis_index("x"), lax.axis_index("y"), lax.axis_index("c")
        copies = []
        for w in range(n):
            ha, size = layout[w][1], halves[w][layout[w][1]]
            copies.append(pltpu.make_async_remote_copy(
                src_ref=_slab(ins[w], {ha: ((1 - c) * size, size)}), dst_ref=outs[w], send_sem=send_sems.at[w],
                recv_sem=recv_sems.at[w], device_id=(x, y, 1 - c), device_id_type=MESH))
            copies[w].start()
        for cp in copies:
            cp.wait()

    return _comm_multi(body, name="grad_pair_swap", n_in=n,
                       out_shapes=[jax.ShapeDtypeStruct(h, g.dtype) for h, g in zip(halves, arrays)],
                       scratch=[pltpu.SemaphoreType.DMA((n,)), pltpu.SemaphoreType.DMA((n,))])(*arrays)


def scatter_to_chips(arrays, layout):
    n = len(arrays)
    slabs = [tuple(d // N_CHIPS if a == sa else d for a, d in enumerate(p.shape)) for p, (sa, _) in zip(arrays, layout)]

    def body(*refs):
        ins, outs = refs[:n], refs[n:2 * n]
        send_sems, recv_sems, local_sems = refs[2 * n:]
        x, y, c = lax.axis_index("x"), lax.axis_index("y"), lax.axis_index("c")
        me = 2 * x + y

        def slab(w, k):
            sa, size = layout[w][0], slabs[w][layout[w][0]]
            return _slab(ins[w], {sa: (k * size, size)})

        local, copies = [], []
        for w in range(n):
            local.append(pltpu.make_async_copy(slab(w, me), outs[w].at[me], local_sems.at[w]))
            local[w].start()
            for j, (px, py) in enumerate([(1 - x, y), (x, 1 - y), (1 - x, 1 - y)]):
                copies.append(pltpu.make_async_remote_copy(
                    src_ref=slab(w, 2 * px + py), dst_ref=outs[w].at[me], send_sem=send_sems.at[3 * w + j],
                    recv_sem=recv_sems.at[3 * w + j], device_id=(px, py, c), device_id_type=MESH))
                copies[-1].start()
        for cp in copies:
            cp.wait()
        for cp in local:
            cp.wait()

    return _comm_multi(body, name="grad_all_to_all", n_in=n,
                       out_shapes=[jax.ShapeDtypeStruct((N_CHIPS,) + s, p.dtype) for s, p in zip(slabs, arrays)],
                       scratch=[pltpu.SemaphoreType.DMA((3 * n,)), pltpu.SemaphoreType.DMA((3 * n,)),
                                pltpu.SemaphoreType.DMA((n,))])(*arrays)


def swap_with_sibling(arrays):
    n = len(arrays)

    def body(*refs):
        ins, outs = refs[:n], refs[n:2 * n]
        send_sems, recv_sems = refs[2 * n:]
        x, y, c = lax.axis_index("x"), lax.axis_index("y"), lax.axis_index("c")
        copies = [pltpu.make_async_remote_copy(src_ref=ins[w], dst_ref=outs[w], send_sem=send_sems.at[w], recv_sem=recv_sems.at[w],
                                               device_id=(x, y, 1 - c), device_id_type=MESH) for w in range(n)]
        for cp in copies:
            cp.start()
        for cp in copies:
            cp.wait()

    return _comm_multi(body, name="grad_half_swap", n_in=n, out_shapes=[jax.ShapeDtypeStruct(a.shape, a.dtype) for a in arrays],
                       scratch=[pltpu.SemaphoreType.DMA((n,)), pltpu.SemaphoreType.DMA((n,))])(*arrays)


def join_halves(mine, other, half_axis, c_arr, *, name):
    a0, a1, a2 = mine.shape
    tile = _row_tile(a1, a2)

    def body(c_ref, m_ref, o_ref, out_ref):
        for half in range(2):
            @pl.when(c_ref[0] == half)
            def _(half=half):
                out_ref[half] = m_ref[...]
                out_ref[1 - half] = o_ref[...]

    blk = pl.BlockSpec((None, tile, a2), lambda i, j, c_ref: (i, j, 0))
    if half_axis == 0:
        out_shape, out_blk = (2, a0, a1, a2), pl.BlockSpec((2, None, tile, a2), lambda i, j, c_ref: (0, i, j, 0))
    else:
        out_shape, out_blk = (a0, 2, a1, a2), pl.BlockSpec((None, 2, tile, a2), lambda i, j, c_ref: (i, 0, j, 0))
    spec = pltpu.PrefetchScalarGridSpec(num_scalar_prefetch=1, grid=(a0, a1 // tile), in_specs=[blk, blk], out_specs=out_blk)
    out = pl.pallas_call(body, name=name, out_shape=jax.ShapeDtypeStruct(out_shape, mine.dtype), grid_spec=spec,
                         compiler_params=pltpu.CompilerParams(vmem_limit_bytes=VMEM_LIMIT_BYTES,
                                                              dimension_semantics=("parallel", "parallel")))(c_arr, mine, other)
    return out.reshape((2 * a0, a1, a2) if half_axis == 0 else (a0, 2 * a1, a2))


def _row_tile(rows, cols, itemsize=4, budget=2 * 1024 * 1024):
    for t in (1024, 512, 256, 128, 64, 32, 16):
        if rows % t == 0 and t * cols * itemsize <= budget:
            return t
    return rows


def add_own_half(full, recv, half_axis, c_arr, *, name):
    a0, a1, a2 = recv.shape
    tile = _row_tile(a1, a2)

    def body(c_ref, f_ref, r_ref, o_ref):
        o_ref[...] = (f_ref[...] + r_ref[...]).astype(o_ref.dtype)

    if half_axis == 0:
        f_spec = pl.BlockSpec((None, tile, a2), lambda i, j, c_ref: (c_ref[0] * a0 + i, j, 0))
    else:
        f_spec = pl.BlockSpec((None, tile, a2), lambda i, j, c_ref: (i, c_ref[0] * (a1 // tile) + j, 0))
    blk = pl.BlockSpec((None, tile, a2), lambda i, j, c_ref: (i, j, 0))
    spec = pltpu.PrefetchScalarGridSpec(num_scalar_prefetch=1, grid=(a0, a1 // tile), in_specs=[f_spec, blk], out_specs=blk)
    return pl.pallas_call(body, name=name, out_shape=jax.ShapeDtypeStruct(recv.shape, BF16), grid_spec=spec,
                          compiler_params=pltpu.CompilerParams(vmem_limit_bytes=VMEM_LIMIT_BYTES,
                                                               dimension_semantics=("parallel", "parallel")))(c_arr, full, recv)


def sum_slots(q, *, name):
    _, a0, a1, a2 = q.shape
    tile = _row_tile(a1, a2, budget=1024 * 1024)

    def body(q_ref, o_ref):
        total = q_ref[0].astype(F32)
        for k in range(1, N_CHIPS):
            total = total + q_ref[k].astype(F32)
        o_ref[...] = total

    return _pcall(body, name=name, out_shape=jax.ShapeDtypeStruct((a0, a1, a2), F32), grid=(a0, a1 // tile),
                  in_specs=[pl.BlockSpec((N_CHIPS, None, tile, a2), lambda i, j: (0, i, j, 0))],
                  out_specs=pl.BlockSpec((None, tile, a2), lambda i, j: (i, j, 0)), sem=("parallel", "parallel"))(q)


def gather_weights(W):
    shards = []
    for name, _, _ in EXCHANGED:
        w = W[name].astype(BF16)
        if name == "gdn_w_in":
            w = jnp.pad(w, ((0, 0), (0, 0), (0, GDN_IN_SLOT - GDN_IN_SHARD)))
        if name == "kv_w":
            w = jnp.pad(w, ((0, 0), (0, KV_PAD_WIDTH - _KV_WIDTH)))[None]
        shards.append(w)
    fulls = dict(zip([n for n, _, _ in EXCHANGED], gather_shards(shards, [(sa, ha) for _, sa, ha in EXCHANGED])))
    slots = fulls["gdn_w_in"]
    fulls["gdn_w_in"] = jnp.concatenate([slots[..., k * GDN_IN_SLOT:k * GDN_IN_SLOT + GDN_IN_SHARD] for k in range(N_CHIPS)], axis=-1)
    fulls["kv_w_pad"] = fulls.pop("kv_w").reshape(-1, KV_PAD_WIDTH)
    conv = pack([W["gdn_conv"]], F32, row_multiple=8)
    conv_all = all_gather8(conv, name="gather_conv").reshape(N_DEV, conv.shape[0], PACK_COLS)
    fulls["gdn_conv"] = jnp.concatenate([unpack(conv_all[2 * k], [W["gdn_conv"].shape])[0] for k in range(N_CHIPS)], axis=-1)
    return fulls


def reduce_gradients(G):
    layout = [(sa, ha) for _, sa, ha in EXCHANGED]
    c_arr = lax.axis_index("c").astype(jnp.int32).reshape(1)
    received = swap_other_halves(G, layout)
    pairs = [add_own_half(g, r, ha, c_arr, name="grad_pair_sum") for g, r, (_, ha) in zip(G, received, layout)]
    slots = scatter_to_chips(pairs, layout)
    halves = [sum_slots(q, name="grad_chip_sum") for q in slots]
    others = swap_with_sibling(halves)
    return [join_halves(h, o, ha, c_arr, name="grad_join_halves") for h, o, (_, ha) in zip(halves, others, layout)]


def allreduce_small(G, names):
    packed = pack([G[n] for n in names], F32, row_multiple=8)
    gathered = all_gather8(packed, name="gather_small_grads").reshape(N_DEV, packed.shape[0], PACK_COLS)
    total = sum_leading(gathered, name="sum_small_grads")
    return dict(zip(names, unpack(total, [G[n].shape for n in names])))


def kernel(x, mem, *rest):
    n_w = len(WEIGHT_NAMES)
    W = dict(zip(WEIGHT_NAMES, rest[:n_w]))
    target = rest[n_w]
    M = dict(zip(WEIGHT_NAMES, rest[n_w + 1:2 * n_w + 1]))
    V = dict(zip(WEIGHT_NAMES, rest[2 * n_w + 1:3 * n_w + 1]))

    full = gather_weights(W)
    P = {n: W[n] for n in REPLICATED}
    P.update({n: full[n] for n in ("ffn1_w_gate_up", "ffn1_w_down", "ffn2_w_gate_up", "ffn2_w_down", "fox_w_in", "w_out",
                                   "mem_w_kv", "kv_w_pad")})
    derived = derived_weights(full["gdn_w_in"], full["gdn_conv"])
    P.update(gdn_w_in_pad=derived["gdn_w_in_pad"], conv8=derived["conv8"])

    part, dx, G = forward_backward(x[0], mem[0], target[0], P)
    loss = lax.psum(0.5 / x.shape[-1] * jnp.sum(part), ("x", "y", "c"))

    ref = reference_layout(G)
    exchange = {n: ref[n] for n, _, _ in EXCHANGED}
    exchange["gdn_w_in"] = jnp.concatenate(
        [jnp.pad(ref["gdn_w_in"][..., k * GDN_IN_SHARD:(k + 1) * GDN_IN_SHARD], ((0, 0), (0, 0), (0, GDN_IN_SLOT - GDN_IN_SHARD)))
         for k in range(N_CHIPS)], axis=-1)
    exchange["kv_w"] = G["kv_w_pad"].reshape(N_CHIPS, -1, KV_PAD_WIDTH)
    shards = dict(zip([n for n, _, _ in EXCHANGED], reduce_gradients([exchange[n] for n, _, _ in EXCHANGED])))
    shards["gdn_w_in"] = shards["gdn_w_in"][..., :GDN_IN_SHARD]
    shards["kv_w"] = shards["kv_w"][0, :, :_KV_WIDTH]
    grads = allreduce_small(ref, REPLICATED + ["gdn_conv"])
    conv_cols = W["gdn_conv"].shape[-1]
    chip = 2 * lax.axis_index("x") + lax.axis_index("y")
    grads["gdn_conv"] = lax.dynamic_slice_in_dim(grads["gdn_conv"], chip * conv_cols, conv_cols, axis=2)
    grads.update(shards)

    outs = {n: adamw(W[n], grads[n], M[n], V[n], name="adamw_" + n) for n in WEIGHT_NAMES}
    return (loss, dx[None], *[grads[n] for n in WEIGHT_NAMES], *[outs[n][0] for n in WEIGHT_NAMES],
            *[outs[n][1] for n in WEIGHT_NAMES], *[outs[n][2] for n in WEIGHT_NAMES])
```

```python
import jax
import jax.numpy as jnp
from jax import lax
from jax.experimental import pallas as pl
from jax.experimental.pallas import tpu as pltpu

F32, BF16 = jnp.float32, jnp.bfloat16
HI = lax.Precision.HIGHEST
MESH = pl.DeviceIdType.MESH

VMEM_LIMIT_BYTES = 48 * 1024 * 1024
LANES = 128
EPS = 1e-6
NEG_INF = -1e30

D_MODEL = 1024
HEAD_DIM = 128
GDN_HEADS = 6
GDN_WIDTH = GDN_HEADS * HEAD_DIM
FOX_HEADS = 6
FOX_WIDTH = FOX_HEADS * HEAD_DIM
MEM_HEADS = 4
MEM_HEAD_DIM = 64
MEM_WIDTH = MEM_HEADS * MEM_HEAD_DIM
FFN_HIDDEN = 2816
CONV_WIDTH = 4
GDN_CHUNK = 128
N_CHIPS = 4
N_DEV = 8

ADAM_LR, ADAM_B1, ADAM_B2, ADAM_EPS, ADAM_WD, ADAM_STEP = 0.001, 0.9, 0.999, 1e-08, 0.01, 10


def _pcall(body, *, name, out_shape, grid=(), in_specs=None, out_specs=None, scratch=(), sem=None):
    params = dict(vmem_limit_bytes=VMEM_LIMIT_BYTES)
    if sem is not None:
        params["dimension_semantics"] = sem
    kw = dict(grid=grid, in_specs=in_specs, out_specs=out_specs) if grid else {}
    return pl.pallas_call(body, name=name, out_shape=out_shape, scratch_shapes=list(scratch),
                          compiler_params=pltpu.CompilerParams(**params), **kw)


def _pick(n, cands):
    for c in cands:
        if n % c == 0:
            return c
    return n


def _make_dot(dtype, precision):
    def raw(a, b, dims):
        return lax.dot_general(a.astype(dtype), b.astype(dtype), (dims, ((), ())),
                               precision=precision, preferred_element_type=F32)

    @jax.custom_vjp
    def dot(a, b):
        return raw(a, b, ((1,), (0,)))

    def fwd(a, b):
        return dot(a, b), (a, b)

    def bwd(resid, ct):
        a, b = resid
        return raw(ct, b, ((1,), (1,))).astype(a.dtype), raw(a, ct, ((0,), (0,))).astype(b.dtype)

    dot.defvjp(fwd, bwd)
    dot.nn = lambda a, b: raw(a, b, ((1,), (0,)))
    dot.nt = lambda a, b: raw(a, b, ((1,), (1,)))
    dot.tn = lambda a, b: raw(a, b, ((0,), (0,)))
    return dot


bdot = _make_dot(BF16, None)
fdot = _make_dot(F32, HI)
idot = _make_dot(F32, lax.Precision.HIGH)
sdot = idot


def _sigmoid(x):
    return 0.5 * jnp.tanh(0.5 * x) + 0.5


def _silu(x):
    return x * _sigmoid(x)


def _softplus(x):
    return jnp.maximum(x, 0.0) + jnp.log(1.0 + jnp.exp(-jnp.abs(x)))


def _log_sigmoid(x):
    return -_softplus(-x)


def _iota2(shape, dim):
    return lax.broadcasted_iota(jnp.int32, shape, dim)


def mm(a, b, *, ta=False, tb=False, a_split=False, b_split=False, out_dtype=F32, scale=1.0, res=None, name):
    assert not (a_split and ta) and not (b_split and tb)
    (K, M) = a.shape if ta else ((2 * a.shape[2], a.shape[1]) if a_split else a.shape[::-1])
    (N, Kb) = b.shape if tb else ((2 * b.shape[2], b.shape[1]) if b_split else b.shape[::-1])
    assert K == Kb, (a.shape, b.shape, ta, tb)
    tm = _pick(M, (1024, 1408, 512, 256, 128))
    tn = _pick(N, (1024, 1408, 1152, 1664, 768, 512, 384, 256, 128))
    tk = _pick(K, (1024, 512, 256, 128)) if ta else _pick(K, (1024, 1408, 1152, 1664, 512, 256, 128))
    assert not a_split or (K // 2) % tk == 0
    assert not b_split or (N // 2) % tn == 0
    nk = K // tk
    dims = (((0 if ta else 1,), (1 if tb else 0,)), ((), ()))

    def body(a_ref, b_ref, *rest):
        o_ref, acc = rest[-2], rest[-1]
        k = pl.program_id(2)

        @pl.when(k == 0)
        def _():
            acc[...] = jnp.zeros_like(acc)

        acc[...] += lax.dot_general(a_ref[...].astype(BF16), b_ref[...].astype(BF16), dims,
                                    preferred_element_type=F32)

        @pl.when(k == nk - 1)
        def _():
            out = acc[...] * scale
            if res is not None:
                out = out + rest[0][...].astype(F32)
            o_ref[...] = out.astype(o_ref.dtype)

    a_spec = pl.BlockSpec((tk, tm), lambda i, j, k: (k, i)) if ta else pl.BlockSpec((tm, tk), lambda i, j, k: (i, k))
    b_spec = pl.BlockSpec((tn, tk), lambda i, j, k: (j, k)) if tb else pl.BlockSpec((tk, tn), lambda i, j, k: (k, j))
    if a_split:
        per_half = K // 2 // tk
        a_spec = pl.BlockSpec((None, tm, tk), lambda i, j, k: (k // per_half, i, k % per_half))
    if b_split:
        per_half = N // 2 // tn
        b_spec = pl.BlockSpec((None, tk, tn), lambda i, j, k: (j // per_half, k, j % per_half))
    o_spec = pl.BlockSpec((tm, tn), lambda i, j, k: (i, j))
    ins, specs = [a, b], [a_spec, b_spec]
    if res is not None:
        ins.append(res)
        specs.append(o_spec)
    return _pcall(body, name=name, out_shape=jax.ShapeDtypeStruct((M, N), out_dtype),
                  grid=(M // tm, N // tn, nk), in_specs=specs, out_specs=o_spec,
                  scratch=[pltpu.VMEM((tm, tn), F32)], sem=("parallel", "parallel", "arbitrary"))(*ins)


def mm_rms_bwd(a, b, x, gain_row, dres, *, a_split=False, name):
    (K, M) = (2 * a.shape[2], a.shape[1]) if a_split else a.shape[::-1]
    D = b.shape[0]
    assert b.shape[1] == K and x.shape == (M, D)
    tm = _pick(M, (1024, 512, 256, 128))
    rows = _pick(tm, (256, 128))
    tk = _pick(K, (1024, 1408, 1152, 1664, 512, 256, 128))
    assert not a_split or (K // 2) % tk == 0
    nk = K // tk

    def body(a_ref, b_ref, x_ref, g_ref, r_ref, o_ref, dg_ref, acc):
        i, k = pl.program_id(0), pl.program_id(1)

        @pl.when(k == 0)
        def _():
            acc[...] = jnp.zeros_like(acc)

        acc[...] += bdot.nt(a_ref[...], b_ref[...])

        @pl.when(k == nk - 1)
        def _():
            dg = jnp.zeros((1, D), F32)
            for s in range(tm // rows):
                sl = slice(s * rows, (s + 1) * rows)
                xv = x_ref[sl, :]
                rstd = lax.rsqrt(jnp.mean(xv * xv, axis=-1, keepdims=True) + EPS)
                xh = xv * rstd
                dn = acc[sl, :]
                dy = dn * g_ref[...]
                o_ref[sl, :] = (dy - xh * jnp.mean(dy * xh, axis=-1, keepdims=True)) * rstd + r_ref[sl, :]
                dg = dg + jnp.sum(dn * xh, axis=0, keepdims=True)

            @pl.when(i == 0)
            def _():
                dg_ref[...] = dg

            @pl.when(i > 0)
            def _():
                dg_ref[...] += dg

    a_spec = pl.BlockSpec((tm, tk), lambda i, k: (i, k))
    if a_split:
        per_half = K // 2 // tk
        a_spec = pl.BlockSpec((None, tm, tk), lambda i, k: (k // per_half, i, k % per_half))
    row = pl.BlockSpec((tm, D), lambda i, k: (i, 0))
    one = pl.BlockSpec((1, D), lambda i, k: (0, 0))
    return _pcall(body, name=name, out_shape=[jax.ShapeDtypeStruct((M, D), F32), jax.ShapeDtypeStruct((1, D), F32)],
                  grid=(M // tm, nk), in_specs=[a_spec, pl.BlockSpec((D, tk), lambda i, k: (0, k)), row, one, row],
                  out_specs=[row, one], scratch=[pltpu.VMEM((tm, D), F32)], sem=("arbitrary", "arbitrary"))(a, b, x, gain_row, dres)


def _row_spec(r, tile):
    if isinstance(r, tuple):
        arr, width, col = r
        return arr, pl.BlockSpec((tile, width), lambda i, col=col: (i, col))
    return r, pl.BlockSpec((tile, r.shape[1]), lambda i: (i, 0))


def _const_spec(c):
    return pl.BlockSpec(c.shape, lambda i: (0,) * c.ndim)


def rowwise(fn, rows, consts, outs, *, tile, name):
    arrs, specs = zip(*[_row_spec(r, tile) for r in rows])
    n_rows = arrs[0].shape[0]
    tile = min(tile, n_rows)
    n_in = len(rows) + len(consts)

    def body(*refs):
        res = fn(*[r[...] for r in refs[:n_in]])
        for o_ref, o in zip(refs[n_in:], res):
            o_ref[...] = o.astype(o_ref.dtype)

    arrs, specs = zip(*[_row_spec(r, tile) for r in rows])
    return _pcall(body, name=name,
                  out_shape=[jax.ShapeDtypeStruct((n_rows, w), dt) for w, dt in outs],
                  grid=(n_rows // tile,),
                  in_specs=list(specs) + [_const_spec(c) for c in consts],
                  out_specs=[pl.BlockSpec((tile, w), lambda i: (i, 0)) for w, _ in outs],
                  sem=("parallel",))(*arrs, *consts)


def rowwise_bwd(fn, rows, consts, cts, *, tile, name, row_grads, const_grads, add=None):
    arrs, _ = zip(*[_row_spec(r, tile) for r in rows])
    n_rows = arrs[0].shape[0]
    tile = min(tile, n_rows)
    arrs, specs = zip(*[_row_spec(r, tile) for r in rows])
    ct_arrs, ct_specs = zip(*[_row_spec(c, tile) for c in cts])
    add = add or {}
    add_idx = sorted(add)
    add_arrs, add_specs = (zip(*[_row_spec(add[i], tile) for i in add_idx]) if add_idx else ((), ()))
    nr, nc, nct, na = len(rows), len(consts), len(cts), len(add_idx)
    want_rows = [i for i, d in enumerate(row_grads) if d is not None]
    want_consts = [i for i, w in enumerate(const_grads) if w]

    def body(*refs):
        row_v = [r[...] for r in refs[:nr]]
        const_v = [r[...] for r in refs[nr:nr + nc]]
        ct_v = [r[...] for r in refs[nr + nc:nr + nc + nct]]
        add_v = {i: refs[nr + nc + nct + j][...] for j, i in enumerate(add_idx)}
        out_refs = refs[nr + nc + nct + na:]
        res, vjp = jax.vjp(fn, *row_v, *const_v)
        grads = vjp(tuple(c.astype(o.dtype) for c, o in zip(ct_v, res)))
        for o_ref, i in zip(out_refs, want_rows):
            g = grads[i].astype(F32)
            if i in add_v:
                g = g + add_v[i].astype(F32)
            o_ref[...] = g.astype(o_ref.dtype)
        first = pl.program_id(0) == 0
        for o_ref, i in zip(out_refs[len(want_rows):], want_consts):
            g = grads[nr + i].astype(F32)

            @pl.when(first)
            def _(o_ref=o_ref, g=g):
                o_ref[...] = g

            @pl.when(jnp.logical_not(first))
            def _(o_ref=o_ref, g=g):
                o_ref[...] += g

    def width(r):
        return r[1] if isinstance(r, tuple) else r.shape[1]

    out_shape = [jax.ShapeDtypeStruct((n_rows, width(rows[i])), row_grads[i]) for i in want_rows]
    out_shape += [jax.ShapeDtypeStruct(consts[i].shape, F32) for i in want_consts]
    out_specs = [pl.BlockSpec((tile, width(rows[i])), lambda i_: (i_, 0)) for i in want_rows]
    out_specs += [_const_spec(consts[i]) for i in want_consts]
    return _pcall(body, name=name, out_shape=out_shape, grid=(n_rows // tile,),
                  in_specs=list(specs) + [_const_spec(c) for c in consts] + list(ct_specs) + list(add_specs),
                  out_specs=out_specs, sem=("arbitrary",))(*arrs, *consts, *ct_arrs, *add_arrs)


def f_rmsnorm(x, g):
    x = x.astype(F32)
    return (x * lax.rsqrt(jnp.mean(x * x, axis=-1, keepdims=True) + EPS) * g,)


def _head_sel(first_lane):
    r, c = _iota2((LANES, GDN_WIDTH), 0), _iota2((LANES, GDN_WIDTH), 1)
    return (r == c // HEAD_DIM + first_lane).astype(F32)


def _tri(n, strict=False):
    r, c = _iota2((n, n), 0), _iota2((n, n), 1)
    return r > c if strict else r >= c


def f_gdn_pre(xc, ab, a_log, dt_bias):
    s = _silu(xc.astype(F32))
    qs, ks = [], []
    for h in range(GDN_HEADS):
        qh = s[:, h * HEAD_DIM:(h + 1) * HEAD_DIM]
        kh = s[:, GDN_WIDTH + h * HEAD_DIM:GDN_WIDTH + (h + 1) * HEAD_DIM]
        qs.append(qh * lax.rsqrt(jnp.sum(qh * qh, axis=-1, keepdims=True) + EPS) * (HEAD_DIM ** -0.5))
        ks.append(kh * lax.rsqrt(jnp.sum(kh * kh, axis=-1, keepdims=True) + EPS))
    q, k = jnp.concatenate(qs, axis=1), jnp.concatenate(ks, axis=1)
    v = s[:, 2 * GDN_WIDTH:]
    ab = ab.astype(F32)
    g = -jnp.exp(a_log) * _softplus(ab + dt_bias)
    gc = _head_broadcast(fdot(_tri(GDN_CHUNK).astype(F32), g), 0)
    beta = _head_broadcast(_sigmoid(ab), GDN_HEADS)
    return q, k, v, gc, beta


def _head_broadcast(x, first_lane):
    lane = _iota2(x.shape, 1)
    cols = [jnp.sum(jnp.where(lane == first_lane + h, x, 0.0), axis=1, keepdims=True) for h in range(GDN_HEADS)]
    return jnp.concatenate([jnp.broadcast_to(c, x.shape) for c in cols], axis=1)


def _unit_lower_inverses(neg_lowers):
    C = neg_lowers[0].shape[0]
    eye = (_iota2((C, C), 0) == _iota2((C, C), 1)).astype(F32)
    invs = [eye + n for n in neg_lowers]
    powers = list(neg_lowers)
    for _ in range(6):
        powers = [idot.nn(p, p) for p in powers]
        invs = [inv + idot.nn(p, inv) for p, inv in zip(powers, invs)]
    return invs


@jax.custom_vjp
def _solve_with_inverse(inv, neg_lower, rhs):
    return idot.nn(inv, rhs)


def _solve_fwd(inv, neg_lower, rhs):
    x = idot.nn(inv, rhs)
    return x, (inv, x)


def _solve_bwd(resid, ct):
    inv, x = resid
    d_rhs = idot.tn(inv, ct)
    return jnp.zeros_like(inv), idot.nt(d_rhs, x), d_rhs


_solve_with_inverse.defvjp(_solve_fwd, _solve_bwd)


def f_gdn_intra(q, k, v, gc, beta, inv=None):
    C = GDN_CHUNK
    causal, strict = _tri(C), _tri(C, strict=True)
    is_last = _iota2((C, HEAD_DIM), 0) == C - 1
    heads = range(GDN_HEADS)
    sls = [slice(h * HEAD_DIM, (h + 1) * HEAD_DIM) for h in heads]
    qs, ks, vs, gs, bs = ([a[:, sl] for sl in sls] for a in (q, k, v, gc, beta))
    decays = [jnp.where(causal, jnp.exp(jnp.where(causal, g - g.T, 0.0)), 0.0) for g in gs]
    kbs = [kh * bh for kh, bh in zip(ks, bs)]
    kts = [kh.T for kh in ks]
    neg_lowers = [jnp.where(strict, -(idot(kb, kt) * d), 0.0) for kb, kt, d in zip(kbs, kts, decays)]
    qks = [jnp.where(causal, idot(qh, kt) * d, 0.0) for qh, kt, d in zip(qs, kts, decays)]
    rhss = [jnp.concatenate([vh * bh, kb * jnp.exp(g)], axis=1) for vh, bh, kb, g in zip(vs, bs, kbs, gs)]
    if inv is None:
        invs = _unit_lower_inverses(neg_lowers)
        sols = [idot.nn(m, r) for m, r in zip(invs, rhss)]
    else:
        sols = [_solve_with_inverse(inv[:, sl], n, r) for sl, n, r in zip(sls, neg_lowers, rhss)]
    g_lasts = [jnp.sum(jnp.where(is_last, g, 0.0), axis=0, keepdims=True) for g in gs]
    outs = [[s[:, :HEAD_DIM] for s in sols], [s[:, HEAD_DIM:] for s in sols], qks,
            [kh * jnp.exp(gl - g) for kh, gl, g in zip(ks, g_lasts, gs)], [qh * jnp.exp(g) for qh, g in zip(qs, gs)]]
    if inv is None:
        outs.append(invs)
    return tuple(jnp.concatenate(o, axis=1) for o in outs)


def f_gdn_post(o, z, gain):
    z = z.astype(F32)
    parts = []
    for h in range(GDN_HEADS):
        oh = o[:, h * HEAD_DIM:(h + 1) * HEAD_DIM]
        parts.append(oh * lax.rsqrt(jnp.mean(oh * oh, axis=-1, keepdims=True) + EPS) * gain)
    return (jnp.concatenate(parts, axis=1) * _silu(z),)


def f_mem_attn(q, k, v):
    q = q.astype(F32)
    lane_head = _iota2((1, MEM_WIDTH), 1) // MEM_HEAD_DIM
    kt = k.astype(F32).T
    masks = [(lane_head == h).astype(F32) for h in range(MEM_HEADS)]
    logits = [bdot(q * mask, kt) * (MEM_HEAD_DIM ** -0.5) for mask in masks]
    ps = [jnp.exp(s - jnp.max(s, axis=-1, keepdims=True)) for s in logits]
    ps = [p / jnp.sum(p, axis=-1, keepdims=True) for p in ps]
    outs = [bdot(p, v) * mask for p, mask in zip(ps, masks)]
    return ((outs[0] + outs[1]) + (outs[2] + outs[3]),)


def f_loss(y, t):
    d = y - t
    return (d * d,)


def conv_fwd(proj, w8, *, width, tile=512):
    n_rows = proj.shape[0]
    tile = min(tile, n_rows)

    def body(x_ref, halo_ref, w_ref, o_ref):
        i = pl.program_id(0)
        halo = jnp.where(i > 0, halo_ref[...].astype(F32), 0.0)
        xs = jnp.concatenate([halo, x_ref[...].astype(F32)], axis=0)
        acc = xs[8:] * w_ref[3:4, :]
        for j in range(CONV_WIDTH - 1):
            acc = acc + pltpu.roll(xs, CONV_WIDTH - 1 - j, 0)[8:] * w_ref[j:j + 1, :]
        o_ref[...] = acc

    return _pcall(body, name="gdn_conv_fwd", out_shape=jax.ShapeDtypeStruct((n_rows, width), F32),
                  grid=(n_rows // tile,),
                  in_specs=[pl.BlockSpec((tile, width), lambda i: (i, 0)),
                            pl.BlockSpec((8, width), lambda i: (jnp.maximum(i * (tile // 8) - 1, 0), 0)),
                            pl.BlockSpec((8, width), lambda i: (0, 0))],
                  out_specs=pl.BlockSpec((tile, width), lambda i: (i, 0)), sem=("parallel",))(proj, proj, w8)


def conv_bwd(proj, w8, dy, *, width, tile=512):
    n_rows = proj.shape[0]
    tile = min(tile, n_rows)
    n = n_rows // tile

    def body(x_ref, xhalo_ref, w_ref, dy_ref, dyhalo_ref, dx_ref, dw_ref):
        i = pl.program_id(0)
        dy = dy_ref[...]
        after = jnp.where(i < n - 1, dyhalo_ref[...], 0.0)
        ds = jnp.concatenate([dy, after], axis=0)
        dx = dy * w_ref[3:4, :]
        for j in range(CONV_WIDTH - 1):
            shift = CONV_WIDTH - 1 - j
            dx = dx + pltpu.roll(ds, tile + 8 - shift, 0)[:tile] * w_ref[j:j + 1, :]
        dx_ref[...] = dx.astype(dx_ref.dtype)
        halo = jnp.where(i > 0, xhalo_ref[...].astype(F32), 0.0)
        xs = jnp.concatenate([halo, x_ref[...].astype(F32)], axis=0)
        rows = [jnp.sum(dy * pltpu.roll(xs, CONV_WIDTH - 1 - j, 0)[8:], axis=0, keepdims=True)
                for j in range(CONV_WIDTH - 1)]
        rows.append(jnp.sum(dy * xs[8:], axis=0, keepdims=True))
        dw = jnp.concatenate(rows + [jnp.zeros((8 - CONV_WIDTH, width), F32)], axis=0)

        @pl.when(i == 0)
        def _():
            dw_ref[...] = dw

        @pl.when(i > 0)
        def _():
            dw_ref[...] += dw

    t8 = tile // 8
    return _pcall(body, name="gdn_conv_bwd",
                  out_shape=[jax.ShapeDtypeStruct((n_rows, width), BF16), jax.ShapeDtypeStruct((8, width), F32)],
                  grid=(n,),
                  in_specs=[pl.BlockSpec((tile, width), lambda i: (i, 0)),
                            pl.BlockSpec((8, width), lambda i: (jnp.maximum(i * t8 - 1, 0), 0)),
                            pl.BlockSpec((8, width), lambda i: (0, 0)),
                            pl.BlockSpec((tile, width), lambda i: (i, 0)),
                            pl.BlockSpec((8, width), lambda i: (jnp.minimum((i + 1) * t8, n * t8 - 1), 0))],
                  out_specs=[pl.BlockSpec((tile, width), lambda i: (i, 0)), pl.BlockSpec((8, width), lambda i: (0, 0))],
                  sem=("arbitrary",))(proj, proj, w8, dy, dy)


def gdn_scan_fwd(u, w, qk, kt, qh, gc):
    n_rows = u.shape[0]
    C, n = GDN_CHUNK, u.shape[0] // GDN_CHUNK

    def body(u_ref, w_ref, qk_ref, kt_ref, qh_ref, gc_ref, o_ref, vn_ref, sin_ref, st):
        @pl.when(pl.program_id(0) == 0)
        def _():
            st[...] = jnp.zeros_like(st)

        sin_ref[0] = st[...]
        sls = [slice(h * HEAD_DIM, (h + 1) * HEAD_DIM) for h in range(GDN_HEADS)]
        states = [st[sl, :] for sl in sls]
        v_news = [u_ref[:, sl] - sdot(w_ref[:, sl], s) for sl, s in zip(sls, states)]
        from_state = [sdot(qh_ref[:, sl], s) for sl, s in zip(sls, states)]
        for sl, a, v_new in zip(sls, from_state, v_news):
            o_ref[:, sl] = a + sdot(qk_ref[:, sl], v_new)
            vn_ref[:, sl] = v_new
        for sl, s, v_new in zip(sls, states, v_news):
            st[sl, :] = s * jnp.exp(gc_ref[C - 1:C, sl]) + sdot.tn(kt_ref[:, sl], v_new)

    blk = pl.BlockSpec((C, GDN_WIDTH), lambda i: (i, 0))
    return _pcall(body, name="gdn_scan_fwd",
                  out_shape=[jax.ShapeDtypeStruct((n_rows, GDN_WIDTH), F32), jax.ShapeDtypeStruct((n_rows, GDN_WIDTH), F32),
                             jax.ShapeDtypeStruct((n, GDN_WIDTH, HEAD_DIM), F32)],
                  grid=(n,), in_specs=[blk] * 6,
                  out_specs=[blk, blk, pl.BlockSpec((1, GDN_WIDTH, HEAD_DIM), lambda i: (i, 0, 0))],
                  scratch=[pltpu.VMEM((GDN_WIDTH, HEAD_DIM), F32)], sem=("arbitrary",))(u, w, qk, kt, qh, gc)


def gdn_scan_bwd(do, w, qk, kt, qh, gc, vn, sin):
    n_rows = do.shape[0]
    C, n = GDN_CHUNK, do.shape[0] // GDN_CHUNK

    def body(do_ref, w_ref, qk_ref, kt_ref, qh_ref, gc_ref, vn_ref, sin_ref,
             du_ref, dw_ref, dqk_ref, dkt_ref, dqh_ref, dgl_ref, dst):
        @pl.when(pl.program_id(0) == 0)
        def _():
            dst[...] = jnp.zeros_like(dst)

        sls = [slice(h * HEAD_DIM, (h + 1) * HEAD_DIM) for h in range(GDN_HEADS)]
        dvns = [sdot.tn(qk_ref[:, sl], do_ref[:, sl]) + sdot(kt_ref[:, sl], dst[sl, :]) for sl in sls]
        for sl, dvn in zip(sls, dvns):
            du_ref[:, sl] = dvn
            dw_ref[:, sl] = -sdot.nt(dvn, sin_ref[0, sl, :])
        for sl in sls:
            dqk_ref[:, sl] = sdot.nt(do_ref[:, sl], vn_ref[:, sl])
            dkt_ref[:, sl] = sdot.nt(vn_ref[:, sl], dst[sl, :])
            dqh_ref[:, sl] = sdot.nt(do_ref[:, sl], sin_ref[0, sl, :])
        for sl, dvn in zip(sls, dvns):
            ds_out = dst[sl, :]
            e = jnp.exp(gc_ref[C - 1:C, sl])
            dgl = jnp.sum(ds_out * sin_ref[0, sl, :], axis=0, keepdims=True) * e
            dgl_ref[:, sl] = jnp.broadcast_to(dgl, (8, HEAD_DIM))
            dst[sl, :] = sdot.tn(qh_ref[:, sl], do_ref[:, sl]) + e * ds_out - sdot.tn(w_ref[:, sl], dvn)

    blk = pl.BlockSpec((C, GDN_WIDTH), lambda i: (n - 1 - i, 0))
    row = jax.ShapeDtypeStruct((n_rows, GDN_WIDTH), F32)
    return _pcall(body, name="gdn_scan_bwd",
                  out_shape=[row] * 5 + [jax.ShapeDtypeStruct((n * 8, GDN_WIDTH), F32)],
                  grid=(n,), in_specs=[blk] * 7 + [pl.BlockSpec((1, GDN_WIDTH, HEAD_DIM), lambda i: (n - 1 - i, 0, 0))],
                  out_specs=[blk] * 5 + [pl.BlockSpec((8, GDN_WIDTH), lambda i: (n - 1 - i, 0))],
                  scratch=[pltpu.VMEM((GDN_WIDTH, HEAD_DIM), F32)], sem=("arbitrary",))(do, w, qk, kt, qh, gc, vn, sin)


def gdn_intra_bwd(q, k, v, gc, beta, inv, cts, dgl):
    n_rows = q.shape[0]
    C = GDN_CHUNK

    def body(*refs):
        ins = [r[...] for r in refs[:5]]
        inv_v = refs[5][...]
        ct = tuple(r[...] for r in refs[6:11])
        dgl_v = refs[11][...]
        _, vjp = jax.vjp(lambda *a: f_gdn_intra(*a, inv=inv_v), *ins)
        grads = list(vjp(ct))
        last = _iota2((C, GDN_WIDTH), 0) == C - 1
        grads[3] = grads[3] + jnp.where(last, jnp.broadcast_to(dgl_v[0:1, :], (C, GDN_WIDTH)), 0.0)
        for o_ref, g in zip(refs[12:], grads):
            o_ref[...] = g

    blk = pl.BlockSpec((C, GDN_WIDTH), lambda i: (i, 0))
    return _pcall(body, name="gdn_intra_bwd", out_shape=[jax.ShapeDtypeStruct((n_rows, GDN_WIDTH), F32)] * 5,
                  grid=(n_rows // C,), in_specs=[blk] * 11 + [pl.BlockSpec((8, GDN_WIDTH), lambda i: (i, 0))],
                  out_specs=[blk] * 5, sem=("parallel",))(q, k, v, gc, beta, inv, *cts, dgl)


def fox_gate_fwd(f, b_f):
    n_rows = f.shape[0]
    T = LANES

    def body(f_ref, b_ref, cb_ref, carry):
        @pl.when(pl.program_id(0) == 0)
        def _():
            carry[...] = jnp.zeros_like(carry)

        c = fdot(_tri(T).astype(F32), _log_sigmoid(f_ref[...] + b_ref[...])) + carry[...]
        carry[...] = c[T - 1:T, :]
        cb_ref[...] = fdot(c, _head_sel(0))

    return _pcall(body, name="fox_gate_fwd", out_shape=jax.ShapeDtypeStruct((n_rows, FOX_WIDTH), F32),
                  grid=(n_rows // T,),
                  in_specs=[pl.BlockSpec((T, LANES), lambda i: (i, 0)), pl.BlockSpec((1, LANES), lambda i: (0, 0))],
                  out_specs=pl.BlockSpec((T, FOX_WIDTH), lambda i: (i, 0)),
                  scratch=[pltpu.VMEM((1, LANES), F32)], sem=("arbitrary",))(f, b_f)


def fox_gate_bwd(f, b_f, dcrow, dcb):
    n_rows = f.shape[0]
    T = LANES
    n = n_rows // T

    def body(f_ref, b_ref, dc_ref, dcb_ref, df_ref, db_ref, carry):
        i = pl.program_id(0)

        @pl.when(i == 0)
        def _():
            carry[...] = jnp.zeros_like(carry)

        rows = [dc_ref[h] for h in range(FOX_HEADS)] + [jnp.zeros((T - FOX_HEADS, T), F32)]
        first_lane = (_iota2((FOX_WIDTH, LANES), 0) == _iota2((FOX_WIDTH, LANES), 1) * HEAD_DIM).astype(F32)
        dc = jnp.concatenate(rows, axis=0).T + fdot(dcb_ref[...], first_lane)
        dlog = fdot.tn(_tri(T).astype(F32), dc) + carry[...]
        carry[...] = dlog[0:1, :]
        df = dlog * (1.0 - _sigmoid(f_ref[...] + b_ref[...]))
        df_ref[...] = df
        db = jnp.sum(df, axis=0, keepdims=True)

        @pl.when(i == 0)
        def _():
            db_ref[...] = db

        @pl.when(i > 0)
        def _():
            db_ref[...] += db

    return _pcall(body, name="fox_gate_bwd",
                  out_shape=[jax.ShapeDtypeStruct((n_rows, LANES), F32), jax.ShapeDtypeStruct((1, LANES), F32)],
                  grid=(n,),
                  in_specs=[pl.BlockSpec((T, LANES), lambda i: (n - 1 - i, 0)), pl.BlockSpec((1, LANES), lambda i: (0, 0)),
                            pl.BlockSpec((FOX_HEADS, 1, T), lambda i: (0, 0, n - 1 - i)),
                            pl.BlockSpec((T, FOX_WIDTH), lambda i: (n - 1 - i, 0))],
                  out_specs=[pl.BlockSpec((T, LANES), lambda i: (n - 1 - i, 0)), pl.BlockSpec((1, LANES), lambda i: (0, 0))],
                  scratch=[pltpu.VMEM((1, LANES), F32)], sem=("arbitrary",))(f, b_f, dcrow, dcb)


FOX_AUG = 2 * HEAD_DIM


def _fox_tiles(n_rows):
    return min(1024, n_rows), min(512, n_rows)


def _fox_pairs(n_rows, query_major):
    tq, tk = _fox_tiles(n_rows)
    nq, r = n_rows // tq, tq // tk
    if query_major:
        pairs = [(i, j) for i in range(nq) for j in range(r * (i + 1))]
    else:
        pairs = [(i, j) for j in range(nq * r) for i in range(j // r, nq)]
    return jnp.asarray([p[0] for p in pairs], jnp.int32), jnp.asarray([p[1] for p in pairs], jnp.int32)


def fox_augment(x, cb, query_side):
    def fn(xt, ct):
        xt = xt.astype(F32)
        lane = _iota2((xt.shape[0], HEAD_DIM), 1)
        parts = []
        for h in range(FOX_HEADS):
            sl = slice(h * HEAD_DIM, (h + 1) * HEAD_DIM)
            c = ct[:, sl]
            hi = c.astype(BF16).astype(F32)
            mid = (c - hi).astype(BF16).astype(F32)
            lo = (c - hi - mid).astype(BF16).astype(F32)
            terms = jnp.where(lane % 3 == 0, hi, jnp.where(lane % 3 == 1, mid, lo))
            if query_side:
                extra = jnp.where(lane < 3, terms, jnp.where(lane < 6, 1.0, 0.0))
                parts += [xt[:, sl] * (HEAD_DIM ** -0.5), extra]
            else:
                extra = jnp.where(lane < 3, 1.0, jnp.where(lane < 6, -terms, 0.0))
                parts += [xt[:, sl], extra]
        return (jnp.concatenate(parts, axis=1),)

    return rowwise(fn, [(x, FOX_WIDTH, 0), cb], [], [(FOX_HEADS * FOX_AUG, BF16)], tile=ROW_TILE,
                   name="fox_augment_q" if query_side else "fox_augment_k")[0]


def _pcall_tables(body, *, name, out_shape, grid, tables, in_specs, out_specs, scratch, sem):
    spec = pltpu.PrefetchScalarGridSpec(num_scalar_prefetch=len(tables), grid=grid, in_specs=in_specs, out_specs=out_specs,
                                        scratch_shapes=list(scratch))
    return pl.pallas_call(body, name=name, out_shape=out_shape, grid_spec=spec,
                          compiler_params=pltpu.CompilerParams(vmem_limit_bytes=VMEM_LIMIT_BYTES, dimension_semantics=sem))


def _fox_logits(qa, ka, offset):
    s = bdot.nt(qa, ka)
    if offset is not None:
        s = jnp.where(_iota2(s.shape, 0) + offset >= _iota2(s.shape, 1), s, NEG_INF)
    return s


def _fox_p_ds(offset, qa_ref, ka_ref, v_ref, o_ref, lse_ref, do_ref):
    s = _fox_logits(qa_ref[...], ka_ref[...], offset)
    p = jnp.exp(s - jnp.tile(lse_ref[...], (1, s.shape[1] // LANES)))
    d_o = do_ref[...].astype(F32)
    delta = jnp.sum(d_o * o_ref[...].astype(F32), axis=-1, keepdims=True)
    return p, p * (bdot.nt(d_o, v_ref[...]) - delta), d_o


def _fox_on_diagonal(i, j, r, tk, step):
    @pl.when(j < r * i)
    def _():
        step(None)

    for m in range(r):
        @pl.when(j == r * i + m)
        def _(m=m):
            step(-m * tk)


def _fox_specs(tq, tk, do_col):
    qaspec = pl.BlockSpec((tq, FOX_AUG), lambda h, p, it, jt: (it[p], h))
    qspec = pl.BlockSpec((tq, HEAD_DIM), lambda h, p, it, jt: (it[p], h))
    dospec = pl.BlockSpec((tq, HEAD_DIM), lambda h, p, it, jt: (it[p], do_col + h))
    kaspec = pl.BlockSpec((tk, FOX_AUG), lambda h, p, it, jt: (jt[p], h))
    kspec = pl.BlockSpec((tk, HEAD_DIM), lambda h, p, it, jt: (jt[p], h))
    vspec = pl.BlockSpec((tk, HEAD_DIM), lambda h, p, it, jt: (jt[p], FOX_HEADS + h))
    cspec = pl.BlockSpec((1, 1, tk), lambda h, p, it, jt: (h, 0, jt[p]))
    return qaspec, qspec, dospec, kaspec, kspec, vspec, cspec


def fox_fwd(qa, kv, ka):
    n_rows = kv.shape[0]
    tq, tk = _fox_tiles(n_rows)
    r = tq // tk
    tables = _fox_pairs(n_rows, True)

    def body(it, jt, qa_ref, ka_ref, v_ref, o_ref, lse_ref, m_sc, l_sc, acc):
        i, j = it[pl.program_id(1)], jt[pl.program_id(1)]

        @pl.when(j == 0)
        def _():
            m_sc[...] = jnp.full(m_sc.shape, NEG_INF, F32)
            l_sc[...] = jnp.zeros_like(l_sc)
            acc[...] = jnp.zeros_like(acc)

        def step(offset):
            s = _fox_logits(qa_ref[...], ka_ref[...], offset)
            m_old = m_sc[...]
            m_new = jnp.maximum(m_old, jnp.max(s, axis=-1, keepdims=True))
            alpha = jnp.exp(m_old - m_new)
            p = jnp.exp(s - jnp.tile(m_new, (1, tk // LANES)))
            l_sc[...] = l_sc[...] * alpha + jnp.sum(p, axis=-1, keepdims=True)
            acc[...] = acc[...] * alpha + bdot(p, v_ref[...])
            m_sc[...] = m_new

        _fox_on_diagonal(i, j, r, tk, step)

        @pl.when(j == r * i + r - 1)
        def _():
            o_ref[...] = (acc[...] / l_sc[...]).astype(o_ref.dtype)
            lse_ref[...] = m_sc[...] + jnp.log(l_sc[...])

    qaspec, qspec, _, kaspec, _, vspec, _ = _fox_specs(tq, tk, 0)
    return _pcall_tables(body, name="fox_fwd",
                         out_shape=[jax.ShapeDtypeStruct((n_rows, FOX_WIDTH), BF16), jax.ShapeDtypeStruct((n_rows, FOX_WIDTH), F32)],
                         grid=(FOX_HEADS, tables[0].shape[0]), tables=tables,
                         in_specs=[qaspec, kaspec, vspec], out_specs=[qspec, qspec],
                         scratch=[pltpu.VMEM((tq, HEAD_DIM), F32)] * 3, sem=("parallel", "arbitrary"))(*tables, qa, ka, kv)


def fox_bwd_dq(qa, kv, ka, o, lse, do, prev=None):
    do, _, do_col = do
    do_col *= FOX_HEADS
    n_rows = kv.shape[0]
    tq, tk = _fox_tiles(n_rows)
    r = tq // tk
    n_prev = 0 if prev is None else 1
    tables = _fox_pairs(n_rows, True)

    def body(it, jt, qa_ref, ka_ref, k_ref, v_ref, o_ref, lse_ref, do_ref, *rest):
        dq_ref, drow_ref, acc, rows = rest[n_prev:]
        i, j = it[pl.program_id(1)], jt[pl.program_id(1)]

        @pl.when(j == 0)
        def _():
            acc[...] = jnp.zeros_like(acc)
            rows[...] = jnp.zeros_like(rows)

        def step(offset):
            _, ds, _ = _fox_p_ds(offset, qa_ref, ka_ref, v_ref, o_ref, lse_ref, do_ref)
            acc[...] += bdot(ds, k_ref[...])
            rows[...] += jnp.sum(ds, axis=-1, keepdims=True)

        _fox_on_diagonal(i, j, r, tk, step)

        @pl.when(j == r * i + r - 1)
        def _():
            dq_ref[...] = (acc[...] * (HEAD_DIM ** -0.5)).astype(dq_ref.dtype)
            drow_ref[...] = rows[...] + rest[0][...] if n_prev else rows[...]

    qaspec, qspec, dospec, kaspec, kspec, vspec, _ = _fox_specs(tq, tk, do_col)
    return _pcall_tables(body, name="fox_bwd_dq" + ("_acc" if n_prev else ""),
                         out_shape=[jax.ShapeDtypeStruct((n_rows, FOX_WIDTH), BF16), jax.ShapeDtypeStruct((n_rows, FOX_WIDTH), F32)],
                         grid=(FOX_HEADS, tables[0].shape[0]), tables=tables,
                         in_specs=[qaspec, kaspec, kspec, vspec, qspec, qspec, dospec] + [qspec] * n_prev,
                         out_specs=[qspec, qspec], scratch=[pltpu.VMEM((tq, HEAD_DIM), F32)] * 2,
                         sem=("parallel", "arbitrary"))(*tables, qa, ka, kv, kv, o, lse, do, *([prev] if n_prev else []))


def fox_bwd_dkv(qa, kv, ka, o, lse, do, prev=None):
    do, _, do_col = do
    do_col *= FOX_HEADS
    n_rows = kv.shape[0]
    tq, tk = _fox_tiles(n_rows)
    nq, r = n_rows // tq, tq // tk
    n_prev = 0 if prev is None else 3
    tables = _fox_pairs(n_rows, False)

    def body(it, jt, qa_ref, ka_ref, v_ref, o_ref, lse_ref, do_ref, *rest):
        prev_refs = rest[:n_prev]
        dk_ref, dv_ref, dc_ref, dk_acc, dv_acc, dc_acc = rest[n_prev:]
        i, j = it[pl.program_id(1)], jt[pl.program_id(1)]

        @pl.when(j >= r * i)
        def _():
            dk_acc[...] = jnp.zeros_like(dk_acc)
            dv_acc[...] = jnp.zeros_like(dv_acc)
            dc_acc[...] = jnp.zeros_like(dc_acc)

        def step(offset):
            p, ds, d_o = _fox_p_ds(offset, qa_ref, ka_ref, v_ref, o_ref, lse_ref, do_ref)
            dv_acc[...] += bdot.tn(p, d_o)
            dk_acc[...] += bdot.tn(ds, qa_ref[:, :HEAD_DIM])
            dc_acc[...] -= jnp.sum(ds, axis=0, keepdims=True)

        _fox_on_diagonal(i, j, r, tk, step)

        @pl.when(i == nq - 1)
        def _():
            dk, dv, dc = dk_acc[...], dv_acc[...], dc_acc[...]
            if n_prev:
                dk, dv, dc = dk + prev_refs[0][...], dv + prev_refs[1][...], dc + prev_refs[2][0]
            dk_ref[...] = dk
            dv_ref[...] = dv
            dc_ref[0] = dc

    qaspec, qspec, dospec, kaspec, kspec, vspec, cspec = _fox_specs(tq, tk, do_col)
    return _pcall_tables(body, name="fox_bwd_dkv" + ("_acc" if n_prev else ""),
                         out_shape=[jax.ShapeDtypeStruct((n_rows, FOX_WIDTH), F32), jax.ShapeDtypeStruct((n_rows, FOX_WIDTH), F32),
                                    jax.ShapeDtypeStruct((FOX_HEADS, 1, n_rows), F32)],
                         grid=(FOX_HEADS, tables[0].shape[0]), tables=tables,
                         in_specs=[qaspec, kaspec, vspec, qspec, qspec, dospec] + [kspec, kspec, cspec][:n_prev],
                         out_specs=[kspec, kspec, cspec],
                         scratch=[pltpu.VMEM((tk, HEAD_DIM), F32), pltpu.VMEM((tk, HEAD_DIM), F32), pltpu.VMEM((1, tk), F32)],
                         sem=("parallel", "arbitrary"))(*tables, qa, ka, kv, o, lse, do, *(prev or ()))


def loss_head(h, gain, target, *, tile=512):
    n_rows, d = h.shape
    tile = min(tile, n_rows)

    def body(h_ref, g_ref, t_ref, part_ref, dy_ref):
        (y,) = f_rmsnorm(h_ref[...], g_ref[...])
        diff = y - t_ref[...]
        dy_ref[...] = diff * (1.0 / d)
        part = jnp.sum(diff * diff, axis=0, keepdims=True)
        first = pl.program_id(0) == 0

        @pl.when(first)
        def _():
            part_ref[...] = part

        @pl.when(jnp.logical_not(first))
        def _():
            part_ref[...] += part

    blk = pl.BlockSpec((tile, d), lambda i: (i, 0))
    one = pl.BlockSpec((1, d), lambda i: (0, 0))
    return _pcall(body, name="loss_head",
                  out_shape=[jax.ShapeDtypeStruct((1, d), F32), jax.ShapeDtypeStruct((n_rows, d), F32)],
                  grid=(n_rows // tile,), in_specs=[blk, one, blk], out_specs=[one, blk], sem=("arbitrary",))(h, gain, target)


def adamw(w, g, m, v, *, name):
    shape = w.shape
    cols = shape[-1] if w.ndim >= 2 else w.size
    rows = w.size // cols
    tile = _pick(rows, (256, 128, 64, 32, 16, 8))
    as2d = lambda a: a.reshape(rows, cols)

    def body(w_ref, g_ref, m_ref, v_ref, d_ref, nm_ref, nv_ref):
        g_ = g_ref[...]
        m_ = ADAM_B1 * m_ref[...] + (1.0 - ADAM_B1) * g_
        v_ = ADAM_B2 * v_ref[...] + (1.0 - ADAM_B2) * (g_ * g_)
        m_hat = m_ / (1.0 - ADAM_B1 ** ADAM_STEP)
        v_hat = v_ / (1.0 - ADAM_B2 ** ADAM_STEP)
        d_ref[...] = -ADAM_LR * (m_hat / (jnp.sqrt(v_hat) + ADAM_EPS) + ADAM_WD * w_ref[...])
        nm_ref[...] = m_
        nv_ref[...] = v_

    blk = pl.BlockSpec((tile, cols), lambda i: (i, 0))
    outs = _pcall(body, name=name, out_shape=[jax.ShapeDtypeStruct((rows, cols), F32)] * 3, grid=(rows // tile,),
                  in_specs=[blk] * 4, out_specs=[blk] * 3, sem=("parallel",))(as2d(w), as2d(g), as2d(m), as2d(v))
    return tuple(o.reshape(shape) for o in outs)


def sum_leading(a, *, name):
    p, r, c = a.shape
    tile = _pick(r, (256, 128, 64, 32, 16, 8))

    def body(a_ref, o_ref):
        total = a_ref[0].astype(F32)
        for k in range(1, p):
            total = total + a_ref[k].astype(F32)
        o_ref[...] = total

    return _pcall(body, name=name, out_shape=jax.ShapeDtypeStruct((r, c), F32), grid=(r // tile,),
                  in_specs=[pl.BlockSpec((p, tile, c), lambda i: (0, i, 0))],
                  out_specs=pl.BlockSpec((tile, c), lambda i: (i, 0)), sem=("parallel",))(a)


_HBM = pl.BlockSpec(memory_space=pltpu.HBM)


def _comm_call(body, *, name, out_shape, n_in, scratch):
    return pl.pallas_call(body, name=name, out_shape=out_shape, in_specs=[_HBM] * n_in, out_specs=_HBM,
                          scratch_shapes=scratch,
                          compiler_params=pltpu.CompilerParams(has_side_effects=True))


def all_gather8(a, *, name):
    m_per, n = a.shape

    def body(x_ref, out_ref, send_sems, recv_sems, local_sem):
        x, y, c = lax.axis_index("x"), lax.axis_index("y"), lax.axis_index("c")
        me, sibling = (x, y, c), (x, y, 1 - c)
        chips = [(1 - x, y), (x, 1 - y), (1 - x, 1 - y)]

        def rows(px, py, pc):
            return out_ref.at[pl.ds((4 * px + 2 * py + pc) * m_per, m_per), :]

        def copy(k, block, to, src=None):
            return pltpu.make_async_remote_copy(
                src_ref=rows(*block) if src is None else src, dst_ref=rows(*block),
                send_sem=send_sems.at[k], recv_sem=recv_sems.at[k], device_id=to, device_id_type=MESH)

        mine = pltpu.make_async_copy(x_ref, rows(*me), local_sem)
        mine.start()
        first = [copy(0, me, sibling, src=x_ref)]
        first += [copy(1 + j, me, (*chip, c), src=x_ref) for j, chip in enumerate(chips)]
        for cp in first:
            cp.start()
        passed = [copy(4 + j, (*chip, c), sibling) for j, chip in enumerate(chips)]
        for j, chip in enumerate(chips):
            copy(1 + j, (*chip, c), me).wait_recv()
            passed[j].start()
        copy(0, sibling, me).wait_recv()
        for j, chip in enumerate(chips):
            copy(4 + j, (*chip, 1 - c), me).wait_recv()
        for cp in first + passed:
            cp.wait_send()
        mine.wait()

    return _comm_call(body, name=name, out_shape=jax.ShapeDtypeStruct((N_DEV * m_per, n), a.dtype), n_in=1,
                      scratch=[pltpu.SemaphoreType.DMA((7,)), pltpu.SemaphoreType.DMA((7,)), pltpu.SemaphoreType.DMA])(a)


PACK_COLS = 1024
PACK_ROW_MULTIPLE = 32

WEIGHT_NAMES = ["ffn1_norm", "ffn1_w_gate_up", "ffn1_w_down", "mix_norm", "ffn2_norm", "ffn2_w_gate_up", "ffn2_w_down",
                "gdn_w_in", "gdn_conv", "gdn_A_log", "gdn_dt_bias", "gdn_out_norm", "fox_w_in", "w_out", "mem_norm",
                "mem_w_kv", "kv_norm", "kv_w", "kv_b_f", "final_norm"]
SHARDED = [("ffn1_w_gate_up", 2), ("ffn1_w_down", 1), ("ffn2_w_gate_up", 2), ("ffn2_w_down", 1), ("gdn_w_in", 2),
           ("gdn_conv", 2), ("fox_w_in", 1), ("w_out", 1), ("mem_w_kv", 1), ("kv_w", 0)]
REPLICATED = [n for n in WEIGHT_NAMES if n not in dict(SHARDED)]


PACK_PIECE_ROWS = 16


def _rows_of(size):
    return -(-size // (PACK_COLS * PACK_PIECE_ROWS)) * PACK_PIECE_ROWS


def pack(pieces, dtype, row_multiple=PACK_ROW_MULTIPLE):
    bufs, total = [], 0
    for p in pieces:
        flat = p.astype(dtype).reshape(-1)
        rows = _rows_of(flat.size)
        bufs.append(jnp.pad(flat, (0, rows * PACK_COLS - flat.size)).reshape(rows, PACK_COLS))
        total += rows
    pad = -total % row_multiple
    if pad:
        bufs.append(jnp.zeros((pad, PACK_COLS), dtype))
    return jnp.concatenate(bufs, axis=0)


def unpack(buf, shapes):
    out, row = [], 0
    for shape in shapes:
        size = 1
        for s in shape:
            size *= s
        rows = _rows_of(size)
        out.append(buf[row:row + rows].reshape(-1)[:size].reshape(shape))
        row += rows
    return out


def _row(vec, width=None):
    vec = vec.astype(F32).reshape(1, -1)
    if width is not None and vec.shape[1] < width:
        vec = jnp.pad(vec, ((0, 0), (0, width - vec.shape[1])))
    return vec


ROW_TILE = 512
GDN_PROJ_WIDTH = 4 * GDN_WIDTH + MEM_WIDTH + LANES
GDN_Z_COL, GDN_QMEM_COL, GDN_AB_COL = 3, 4 * GDN_WIDTH // MEM_WIDTH, (4 * GDN_WIDTH + MEM_WIDTH) // LANES
FOX_QMEM_COL = FOX_WIDTH // MEM_WIDTH
KV_PAD_WIDTH = 2 * FOX_WIDTH + LANES


def rms_fwd(x, gain_row, out_dtype=BF16):
    return rowwise(f_rmsnorm, [x], [gain_row], [(x.shape[1], out_dtype)], tile=ROW_TILE, name="rms_fwd")[0]


def rms_bwd(x, gain_row, dy, dres=None):
    return rowwise_bwd(f_rmsnorm, [x], [gain_row], [dy], tile=ROW_TILE, name="rms_bwd", row_grads=[F32],
                       const_grads=[True], add=None if dres is None else {0: dres})


def _ffn_tiles(n_rows):
    return _pick(n_rows, (512, 256, 128)), _pick(FFN_HIDDEN, (1408, 256, 128))


def ffn_up_act(n, wgu):
    n_rows, d = n.shape
    tm, tn = _ffn_tiles(n_rows)
    nj = FFN_HIDDEN // tn

    def body(n_ref, wg_ref, wu_ref, gu_ref, act_ref):
        x = n_ref[...].astype(BF16)
        g = bdot.nn(x, wg_ref[...])
        u = bdot.nn(x, wu_ref[...])
        gu_ref[0] = g.astype(gu_ref.dtype)
        gu_ref[1] = u.astype(gu_ref.dtype)
        act_ref[...] = (_silu(g) * u).astype(act_ref.dtype)

    return _pcall(body, name="ffn_up_act",
                  out_shape=[jax.ShapeDtypeStruct((2, n_rows, FFN_HIDDEN), BF16), jax.ShapeDtypeStruct((n_rows, FFN_HIDDEN), BF16)],
                  grid=(nj, n_rows // tm),
                  in_specs=[pl.BlockSpec((tm, d), lambda j, i: (i, 0)), pl.BlockSpec((d, tn), lambda j, i: (0, j)),
                            pl.BlockSpec((d, tn), lambda j, i: (0, nj + j))],
                  out_specs=[pl.BlockSpec((2, tm, tn), lambda j, i: (0, i, j)), pl.BlockSpec((tm, tn), lambda j, i: (i, j))],
                  sem=("parallel", "parallel"))(n, wgu, wgu)


def ffn_down_dx_act(dh, wd, gu):
    n_rows, d = dh.shape
    tm, tn = _ffn_tiles(n_rows)

    def body(dh_ref, wd_ref, gu_ref, dgu_ref):
        dact = 0.5 * bdot.nt(dh_ref[...], wd_ref[...])
        gate, up = gu_ref[0].astype(F32), gu_ref[1].astype(F32)
        sg = _sigmoid(gate)
        dgu_ref[0] = (dact * up * (sg * (1.0 + gate * (1.0 - sg)))).astype(dgu_ref.dtype)
        dgu_ref[1] = (dact * (gate * sg)).astype(dgu_ref.dtype)

    blk = pl.BlockSpec((2, tm, tn), lambda j, i: (0, i, j))
    return _pcall(body, name="ffn_down_dx_act", out_shape=jax.ShapeDtypeStruct((2, n_rows, FFN_HIDDEN), BF16),
                  grid=(FFN_HIDDEN // tn, n_rows // tm),
                  in_specs=[pl.BlockSpec((tm, d), lambda j, i: (i, 0)), pl.BlockSpec((tn, d), lambda j, i: (j, 0)), blk],
                  out_specs=blk, sem=("parallel", "parallel"))(dh, wd, gu)


def ffn_fwd(h, gain_row, wgu, wd):
    n = rms_fwd(h, gain_row)
    gu, act = ffn_up_act(n, wgu)
    return mm(act, wd, scale=0.5, res=h, name="ffn_down"), (h, n, gu, act)


def ffn_bwd(dh, saved, gain_row, wgu, wd):
    h, n, gu, act = saved
    dgu = ffn_down_dx_act(dh, wd, gu)
    dwd = mm(act, dh, ta=True, scale=0.5, name="ffn_down_dw")
    dwgu = mm(n, dgu, ta=True, b_split=True, name="ffn_up_dw")
    dh, dgain = mm_rms_bwd(dgu, wgu, h, gain_row, dh, a_split=True, name="ffn_up_dx")
    return dh, dwgu, dwd, dgain


def gdn_fwd(proj, w8, a_row, dt_row, onorm_row):
    wide = [(GDN_WIDTH, F32)] * 5
    xc = conv_fwd(proj, w8, width=3 * GDN_WIDTH)
    q, k, v, gc, beta = rowwise(f_gdn_pre, [xc, (proj, LANES, GDN_AB_COL)], [a_row, dt_row], wide, tile=GDN_CHUNK,
                                name="gdn_pre_fwd")
    u, w, qk, kt, qh, inv = rowwise(f_gdn_intra, [q, k, v, gc, beta], [], wide + wide[:1], tile=GDN_CHUNK,
                                    name="gdn_intra_fwd")
    o, vn, sin = gdn_scan_fwd(u, w, qk, kt, qh, gc)
    main = rowwise(f_gdn_post, [o, (proj, GDN_WIDTH, GDN_Z_COL)], [onorm_row], [(GDN_WIDTH, BF16)], tile=ROW_TILE,
                   name="gdn_post_fwd")[0]
    return main, (xc, q, k, v, gc, beta, inv, w, qk, kt, qh, vn, sin, o)


def gdn_bwd(dmain, proj, saved, w8, a_row, dt_row, onorm_row):
    xc, q, k, v, gc, beta, inv, w, qk, kt, qh, vn, sin, o = saved
    do, dz, donorm = rowwise_bwd(f_gdn_post, [o, (proj, GDN_WIDTH, GDN_Z_COL)], [onorm_row], [dmain], tile=ROW_TILE,
                                 name="gdn_post_bwd", row_grads=[F32, BF16], const_grads=[True])
    du, dw, dqk, dkt, dqh, dgl = gdn_scan_bwd(do, w, qk, kt, qh, gc, vn, sin)
    dq, dk, dv, dgc, dbeta = gdn_intra_bwd(q, k, v, gc, beta, inv, (du, dw, dqk, dkt, dqh), dgl)
    dxc, dab, da, ddt = rowwise_bwd(f_gdn_pre, [xc, (proj, LANES, GDN_AB_COL)], [a_row, dt_row], [dq, dk, dv, dgc, dbeta],
                                    tile=GDN_CHUNK, name="gdn_pre_bwd", row_grads=[F32, BF16], const_grads=[True, True])
    dqkv, dw8 = conv_bwd(proj, w8, dxc, width=3 * GDN_WIDTH)
    return dqkv, dz, dab, dw8, da, ddt, donorm


def mem_fwd(q, kmem, vmem):
    return rowwise(f_mem_attn, [q], [kmem, vmem], [(MEM_WIDTH, BF16)], tile=ROW_TILE, name="mem_attn_fwd")[0]


def mem_bwd(q, kmem, vmem, dout):
    return rowwise_bwd(f_mem_attn, [q], [kmem, vmem], [dout], tile=ROW_TILE, name="mem_attn_bwd", row_grads=[BF16],
                       const_grads=[True, True])


def forward_backward(xs, mems, target, P):
    depth, n_a = 4, 2
    G = {}
    mem_gain = _row(P["mem_norm"])
    mem_n = rms_fwd(mems, mem_gain)
    h = xs
    saved = []
    shared = None
    for l in range(depth):
        h0 = h
        h1, s1 = ffn_fwd(h0, _row(P["ffn1_norm"][l]), P["ffn1_w_gate_up"][l], P["ffn1_w_down"][l])
        u = rms_fwd(h1, _row(P["mix_norm"][l]))
        kvm = mm(mem_n, P["mem_w_kv"][l], name="mem_kv")
        kmem, vmem = kvm[:, :MEM_WIDTH], kvm[:, MEM_WIDTH:]
        if l < n_a:
            gp = (P["conv8"][l], _row(P["gdn_A_log"][l], LANES), _row(P["gdn_dt_bias"][l], LANES), _row(P["gdn_out_norm"][l]))
            proj = mm(u, P["gdn_w_in_pad"][l], name="gdn_in")
            main, sm = gdn_fwd(proj, *gp)
            qm = (proj, MEM_WIDTH, GDN_QMEM_COL)
        else:
            proj = mm(u, P["fox_w_in"][l - n_a], out_dtype=BF16, name="fox_in")
            kv, ka, cb = shared
            qa = fox_augment(proj, cb, True)
            main, lse = fox_fwd(qa, kv, ka)
            sm = (main, lse, qa)
            qm = (proj, MEM_WIDTH, FOX_QMEM_COL)
        mo = mem_fwd(qm, kmem, vmem)
        cat = jnp.concatenate([main, mo], axis=1)
        h2 = mm(cat, P["w_out"][l], res=h1, name="mix_out")
        h3, s2 = ffn_fwd(h2, _row(P["ffn2_norm"][l]), P["ffn2_w_gate_up"][l], P["ffn2_w_down"][l])
        saved.append((s1, h1, u, kmem, vmem, proj, sm, qm, cat, s2))
        h = h3
        if l == n_a - 1:
            nkv = rms_fwd(h, _row(P["kv_norm"]))
            kv = mm(nkv, P["kv_w_pad"][:, :2 * FOX_WIDTH], out_dtype=BF16, name="fox_kv")
            f = mm(nkv, P["kv_w_pad"][:, 2 * FOX_WIDTH:], name="fox_f")
            bf_row = _row(P["kv_b_f"], LANES)
            cb = fox_gate_fwd(f, bf_row)
            shared = (kv, fox_augment(kv, cb, False), cb)
            kv_saved = (h, nkv, f, bf_row)

    part, dy = loss_head(h, _row(P["final_norm"]), target)
    dh, G["final_norm"] = rms_bwd(h, _row(P["final_norm"]), dy)

    per_layer = {n: [None] * depth for n in ("ffn1_norm", "ffn1_w_gate_up", "ffn1_w_down", "mix_norm", "ffn2_norm",
                                             "ffn2_w_gate_up", "ffn2_w_down", "w_out", "mem_w_kv")}
    gdn_g = {n: [None] * n_a for n in ("gdn_w_in_pad", "conv8", "gdn_A_log", "gdn_dt_bias", "gdn_out_norm")}
    fox_g = [None] * (depth - n_a)
    dmem_n = None
    dkv_acc = dcb_acc = None
    for l in reversed(range(depth)):
        s1, h1, u, kmem, vmem, proj, sm, qm, cat, s2 = saved[l]
        if l == n_a - 1:
            hk, nkv, f, bf_row = kv_saved
            dk, dv, dcrow = dkv_acc
            df, dbf = fox_gate_bwd(f, bf_row, dcrow, dcb_acc)
            dp = jnp.concatenate([dk.astype(BF16), dv.astype(BF16), df.astype(BF16)], axis=1)
            G["kv_w_pad"] = mm(nkv, dp, ta=True, name="fox_kv_dw")
            G["kv_b_f"] = dbf
            dh, G["kv_norm"] = mm_rms_bwd(dp, P["kv_w_pad"], hk, _row(P["kv_norm"]), dh, name="fox_kv_dx")
        dh, per_layer["ffn2_w_gate_up"][l], per_layer["ffn2_w_down"][l], per_layer["ffn2_norm"][l] = ffn_bwd(
            dh, s2, _row(P["ffn2_norm"][l]), P["ffn2_w_gate_up"][l], P["ffn2_w_down"][l])
        dcat = mm(dh, P["w_out"][l], tb=True, out_dtype=BF16, name="mix_out_dx")
        per_layer["w_out"][l] = mm(cat, dh, ta=True, name="mix_out_dw")
        dqm, dkm, dvm = mem_bwd(qm, kmem, vmem, (dcat, MEM_WIDTH, FOX_QMEM_COL))
        dkvm = jnp.concatenate([dkm, dvm], axis=1)
        per_layer["mem_w_kv"][l] = mm(mem_n, dkvm, ta=True, name="mem_kv_dw")
        dmem_n = mm(dkvm, P["mem_w_kv"][l], tb=True, res=dmem_n, name="mem_kv_dx")
        dmain = (dcat, GDN_WIDTH, 0)
        if l < n_a:
            gp = (P["conv8"][l], _row(P["gdn_A_log"][l], LANES), _row(P["gdn_dt_bias"][l], LANES), _row(P["gdn_out_norm"][l]))
            dqkv, dz, dab, gdn_g["conv8"][l], gdn_g["gdn_A_log"][l], gdn_g["gdn_dt_bias"][l], gdn_g["gdn_out_norm"][l] = gdn_bwd(
                dmain, proj, sm, *gp)
            dproj = jnp.concatenate([dqkv, dz, dqm, dab], axis=1)
            gdn_g["gdn_w_in_pad"][l] = mm(u, dproj, ta=True, name="gdn_in_dw")
            w_in = P["gdn_w_in_pad"][l]
        else:
            o, lse, qa = sm
            kv, ka, _ = shared
            dq, dcb_acc = fox_bwd_dq(qa, kv, ka, o, lse, dmain, dcb_acc)
            dkv_acc = fox_bwd_dkv(qa, kv, ka, o, lse, dmain, dkv_acc)
            dproj = jnp.concatenate([dq, dqm], axis=1)
            fox_g[l - n_a] = mm(u, dproj, ta=True, name="fox_in_dw")
            w_in = P["fox_w_in"][l - n_a]
        dh, per_layer["mix_norm"][l] = mm_rms_bwd(dproj, w_in, h1, _row(P["mix_norm"][l]), dh, name="mix_in_dx")
        dh, per_layer["ffn1_w_gate_up"][l], per_layer["ffn1_w_down"][l], per_layer["ffn1_norm"][l] = ffn_bwd(
            dh, s1, _row(P["ffn1_norm"][l]), P["ffn1_w_gate_up"][l], P["ffn1_w_down"][l])

    (G["mem_norm"],) = rowwise_bwd(f_rmsnorm, [mems], [mem_gain], [dmem_n], tile=ROW_TILE, name="mem_norm_bwd",
                                   row_grads=[None], const_grads=[True])
    for n, v in per_layer.items():
        G[n] = jnp.stack(v)
    for n, v in gdn_g.items():
        G[n] = jnp.stack(v)
    G["fox_w_in"] = jnp.stack(fox_g)
    return part, dh, G


_GDN_O0 = 4 * GDN_WIDTH
_GDN_O1 = _GDN_O0 + 2 * GDN_HEADS
_KV_WIDTH = 2 * FOX_WIDTH + FOX_HEADS


def derived_weights(gdn_w_in, gdn_conv, kv_w=None):
    zeros = jnp.zeros(gdn_w_in.shape[:-1] + (LANES - 2 * GDN_HEADS,), gdn_w_in.dtype)
    out = dict(
        gdn_w_in_pad=jnp.concatenate([gdn_w_in[..., :_GDN_O0], gdn_w_in[..., _GDN_O1:], gdn_w_in[..., _GDN_O0:_GDN_O1], zeros], axis=-1),
        conv8=jnp.pad(gdn_conv.astype(F32), ((0, 0), (0, 8 - CONV_WIDTH), (0, 0))))
    if kv_w is not None:
        out["kv_w_pad"] = jnp.pad(kv_w, ((0, 0), (0, KV_PAD_WIDTH - _KV_WIDTH)))
    return out


def reference_layout(G):
    gp = G["gdn_w_in_pad"]
    out = dict(G)
    out["gdn_w_in"] = jnp.concatenate([gp[..., :_GDN_O0], gp[..., _GDN_O0 + MEM_WIDTH:_GDN_O0 + MEM_WIDTH + 2 * GDN_HEADS],
                                       gp[..., _GDN_O0:_GDN_O0 + MEM_WIDTH]], axis=-1)
    out["gdn_conv"] = G["conv8"][:, :CONV_WIDTH]
    out["kv_w"] = G["kv_w_pad"][:, :_KV_WIDTH]
    out["gdn_A_log"] = G["gdn_A_log"][:, 0, :GDN_HEADS]
    out["gdn_dt_bias"] = G["gdn_dt_bias"][:, 0, :GDN_HEADS]
    out["gdn_out_norm"] = G["gdn_out_norm"][:, 0, :]
    out["kv_b_f"] = G["kv_b_f"][0, :FOX_HEADS]
    for n in ("ffn1_norm", "mix_norm", "ffn2_norm"):
        out[n] = G[n][:, 0, :]
    for n in ("mem_norm", "kv_norm", "final_norm"):
        out[n] = G[n][0]
    return {n: out[n] for n in WEIGHT_NAMES}


EXCHANGED = [("ffn1_w_gate_up", 2, 0), ("ffn1_w_down", 1, 0), ("ffn2_w_gate_up", 2, 0), ("ffn2_w_down", 1, 0),
             ("gdn_w_in", 2, 0), ("fox_w_in", 1, 0), ("w_out", 1, 0), ("mem_w_kv", 1, 0), ("kv_w", 0, 1)]
GDN_IN_SHARD = (4 * GDN_WIDTH + 2 * GDN_HEADS + MEM_WIDTH) // N_CHIPS
GDN_IN_SLOT = 896


def _slab(ref, axis_slices):
    idx = [slice(None)] * len(ref.shape)
    for axis, (start, size) in axis_slices.items():
        idx[axis] = pl.ds(start, size)
    return ref.at[tuple(idx)]


def _comm_multi(body, *, name, n_in, out_shapes, scratch):
    return pl.pallas_call(body, name=name, out_shape=out_shapes, in_specs=[_HBM] * n_in, out_specs=[_HBM] * len(out_shapes),
                          scratch_shapes=scratch, compiler_params=pltpu.CompilerParams(has_side_effects=True))


def gather_shards(shards, layout):
    n = len(shards)
    fulls = [tuple(d * (N_CHIPS if a == sa else 1) for a, d in enumerate(s.shape)) for s, (sa, _) in zip(shards, layout)]
    n_sem = 9

    def pieces(w):
        sa, ha = layout[w]
        axis = 3 - sa - ha
        size = shards[w].shape[axis]
        unit = LANES if axis == 2 else 16
        first = -(-(size // 2) // unit) * unit
        return axis, [(0, first), (first, size - first)]

    def body(*refs):
        ins, outs = refs[:n], refs[n:2 * n]
        send_sems, recv_sems, local_sems = refs[2 * n:]
        x, y, c = lax.axis_index("x"), lax.axis_index("y"), lax.axis_index("c")
        me, sibling, x_nbr, y_nbr = (x, y, c), (x, y, 1 - c), (1 - x, y, c), (x, 1 - y, c)
        chip_x, chip_y, chip_d = (1 - x, y), (x, 1 - y), (1 - x, 1 - y)

        def region(w, chip, pc, piece=None):
            (sa, ha), shard = layout[w], shards[w].shape
            where = {sa: ((2 * chip[0] + chip[1]) * shard[sa], shard[sa]), ha: (pc * (shard[ha] // 2), shard[ha] // 2)}
            if piece is not None:
                axis, parts = pieces(w)
                where[axis] = parts[piece]
            return _slab(outs[w], where)

        def my_half(w):
            ha, shard = layout[w][1], shards[w].shape
            return _slab(ins[w], {ha: (c * (shard[ha] // 2), shard[ha] // 2)})

        def copy(w, k, where, to, src=None):
            return pltpu.make_async_remote_copy(
                src_ref=where if src is None else src, dst_ref=where, send_sem=send_sems.at[n_sem * w + k],
                recv_sem=recv_sems.at[n_sem * w + k], device_id=to, device_id_type=MESH)

        mine, sends = [], [[] for _ in range(n)]
        for w in range(n):
            mine.append(pltpu.make_async_copy(my_half(w), region(w, (x, y), c), local_sems.at[w]))
            mine[w].start()
            sends[w] = [copy(w, k, region(w, (x, y), c), to, src=my_half(w)) for k, to in enumerate((sibling, x_nbr, y_nbr))]
            for cp in sends[w]:
                cp.start()
        for w in range(n):
            copy(w, 1, region(w, chip_x, c), me).wait_recv()
            onward = [copy(w, 3, region(w, chip_x, c, 0), y_nbr), copy(w, 5, region(w, chip_x, c), sibling)]
            for cp in onward:
                cp.start()
            sends[w] += onward
            copy(w, 2, region(w, chip_y, c), me).wait_recv()
            onward = [copy(w, 4, region(w, chip_y, c, 1), x_nbr), copy(w, 6, region(w, chip_y, c), sibling)]
            for cp in onward:
                cp.start()
            sends[w] += onward
        for w in range(n):
            copy(w, 3, region(w, chip_d, c, 0), me).wait_recv()
            copy(w, 4, region(w, chip_d, c, 1), me).wait_recv()
            onward = [copy(w, 7, region(w, chip_d, c, 0), sibling), copy(w, 8, region(w, chip_d, c, 1), sibling)]
            for cp in onward:
                cp.start()
            sends[w] += onward
        for w in range(n):
            copy(w, 0, region(w, (x, y), 1 - c), me).wait_recv()
            copy(w, 5, region(w, chip_x, 1 - c), me).wait_recv()
            copy(w, 6, region(w, chip_y, 1 - c), me).wait_recv()
            copy(w, 7, region(w, chip_d, 1 - c, 0), me).wait_recv()
            copy(w, 8, region(w, chip_d, 1 - c, 1), me).wait_recv()
        for w in range(n):
            for cp in sends[w]:
                cp.wait_send()
            mine[w].wait()

    return _comm_multi(body, name="gather_shards", n_in=n,
                       out_shapes=[jax.ShapeDtypeStruct(f, s.dtype) for f, s in zip(fulls, shards)],
                       scratch=[pltpu.SemaphoreType.DMA((n_sem * n,)), pltpu.SemaphoreType.DMA((n_sem * n,)),
                                pltpu.SemaphoreType.DMA((n,))])(*shards)


def swap_other_halves(arrays, layout):
    n = len(arrays)
    halves = [tuple(d // 2 if a == ha else d for a, d in enumerate(g.shape)) for g, (_, ha) in zip(arrays, layout)]

    def body(*refs):
        ins, outs = refs[:n], refs[n:2 * n]
        send_sems, recv_sems = refs[2 * n:]
        x, y, c = lax.axis_index("x"), lax.axis_index("y"), lax.axis_index("c")
        copies = []
        for w in range(n):
            ha, size = layout[w][1], halves[w][layout[w][1]]
            copies.append(pltpu.make_async_remote_copy(
                src_ref=_slab(ins[w], {ha: ((1 - c) * size, size)}), dst_ref=outs[w], send_sem=send_sems.at[w],
                recv_sem=recv_sems.at[w], device_id=(x, y, 1 - c), device_id_type=MESH))
            copies[w].start()
        for cp in copies:
            cp.wait()

    return _comm_multi(body, name="grad_pair_swap", n_in=n,
                       out_shapes=[jax.ShapeDtypeStruct(h, g.dtype) for h, g in zip(halves, arrays)],
                       scratch=[pltpu.SemaphoreType.DMA((n,)), pltpu.SemaphoreType.DMA((n,))])(*arrays)


def scatter_to_chips(arrays, layout):
    n = len(arrays)
    slabs = [tuple(d // N_CHIPS if a == sa else d for a, d in enumerate(p.shape)) for p, (sa, _) in zip(arrays, layout)]
    n_sem = 6

    def pieces(w):
        sa, ha = layout[w]
        axis = 3 - sa - ha
        size = slabs[w][axis]
        unit = LANES if axis == 2 else 16
        first = -(-(size // 2) // unit) * unit
        return axis, [(0, first), (first, size - first)]

    def piece_shape(w, p):
        axis, parts = pieces(w)
        return tuple(parts[p][1] if a == axis else d for a, d in enumerate(slabs[w]))

    def body(*refs):
        ins, outs, stage = refs[:n], refs[n:2 * n], (refs[2 * n:3 * n], refs[3 * n:4 * n])
        send_sems, recv_sems, local_sems = refs[4 * n:]
        x, y, c = lax.axis_index("x"), lax.axis_index("y"), lax.axis_index("c")
        me, chip_x, chip_y, chip_d = 2 * x + y, 2 * (1 - x) + y, 2 * x + (1 - y), 2 * (1 - x) + (1 - y)
        x_nbr, y_nbr = (1 - x, y, c), (x, 1 - y, c)

        def slab(w, k, p=None):
            sa, size = layout[w][0], slabs[w][layout[w][0]]
            where = {sa: (k * size, size)}
            if p is not None:
                axis, parts = pieces(w)
                where[axis] = parts[p]
            return _slab(ins[w], where)

        def slot(w, k, p=None):
            if p is None:
                return outs[w].at[k]
            axis, parts = pieces(w)
            idx = [slice(None)] * 3
            idx[axis] = pl.ds(*parts[p])
            return outs[w].at[(k, *idx)]

        def copy(w, k, src, dst, to):
            return pltpu.make_async_remote_copy(src_ref=src, dst_ref=dst, send_sem=send_sems.at[n_sem * w + k],
                                                recv_sem=recv_sems.at[n_sem * w + k], device_id=to, device_id_type=MESH)

        local, sends = [], [[] for _ in range(n)]
        for w in range(n):
            local.append(pltpu.make_async_copy(slab(w, me), slot(w, me), local_sems.at[w]))
            local[w].start()
            sends[w] = [copy(w, 0, slab(w, chip_x), slot(w, me), x_nbr), copy(w, 1, slab(w, chip_y), slot(w, me), y_nbr),
                        copy(w, 2, slab(w, chip_d, 0), stage[0][w], x_nbr), copy(w, 3, slab(w, chip_d, 1), stage[1][w], y_nbr)]
            for cp in sends[w]:
                cp.start()
        for w in range(n):
            copy(w, 2, stage[0][w], stage[0][w], x_nbr).wait_recv()
            onward = copy(w, 4, stage[0][w], slot(w, chip_x, 0), y_nbr)
            onward.start()
            sends[w].append(onward)
            copy(w, 3, stage[1][w], stage[1][w], y_nbr).wait_recv()
            onward = copy(w, 5, stage[1][w], slot(w, chip_y, 1), x_nbr)
            onward.start()
            sends[w].append(onward)
        for w in range(n):
            copy(w, 0, slab(w, chip_x), slot(w, chip_x), x_nbr).wait_recv()
            copy(w, 1, slab(w, chip_y), slot(w, chip_y), y_nbr).wait_recv()
            copy(w, 4, stage[0][w], slot(w, chip_d, 0), y_nbr).wait_recv()
            copy(w, 5, stage[1][w], slot(w, chip_d, 1), x_nbr).wait_recv()
        for w in range(n):
            for cp in sends[w]:
                cp.wait_send()
            local[w].wait()

    outs = _comm_multi(body, name="grad_all_to_all", n_in=n,
                       out_shapes=[jax.ShapeDtypeStruct((N_CHIPS,) + s, p.dtype) for s, p in zip(slabs, arrays)]
                       + [jax.ShapeDtypeStruct(piece_shape(w, 0), arrays[w].dtype) for w in range(n)]
                       + [jax.ShapeDtypeStruct(piece_shape(w, 1), arrays[w].dtype) for w in range(n)],
                       scratch=[pltpu.SemaphoreType.DMA((n_sem * n,)), pltpu.SemaphoreType.DMA((n_sem * n,)),
                                pltpu.SemaphoreType.DMA((n,))])(*arrays)
    return outs[:n]


def swap_with_sibling(arrays):
    n = len(arrays)

    def body(*refs):
        ins, outs = refs[:n], refs[n:2 * n]
        send_sems, recv_sems = refs[2 * n:]
        x, y, c = lax.axis_index("x"), lax.axis_index("y"), lax.axis_index("c")
        copies = [pltpu.make_async_remote_copy(src_ref=ins[w], dst_ref=outs[w], send_sem=send_sems.at[w], recv_sem=recv_sems.at[w],
                                               device_id=(x, y, 1 - c), device_id_type=MESH) for w in range(n)]
        for cp in copies:
            cp.start()
        for cp in copies:
            cp.wait()

    return _comm_multi(body, name="grad_half_swap", n_in=n, out_shapes=[jax.ShapeDtypeStruct(a.shape, a.dtype) for a in arrays],
                       scratch=[pltpu.SemaphoreType.DMA((n,)), pltpu.SemaphoreType.DMA((n,))])(*arrays)


def join_halves(mine, other, half_axis, c_arr, *, name):
    a0, a1, a2 = mine.shape
    tile = _row_tile(a1, a2)

    def body(c_ref, m_ref, o_ref, out_ref):
        for half in range(2):
            @pl.when(c_ref[0] == half)
            def _(half=half):
                out_ref[half] = m_ref[...]
                out_ref[1 - half] = o_ref[...]

    blk = pl.BlockSpec((None, tile, a2), lambda i, j, c_ref: (i, j, 0))
    if half_axis == 0:
        out_shape, out_blk = (2, a0, a1, a2), pl.BlockSpec((2, None, tile, a2), lambda i, j, c_ref: (0, i, j, 0))
    else:
        out_shape, out_blk = (a0, 2, a1, a2), pl.BlockSpec((None, 2, tile, a2), lambda i, j, c_ref: (i, 0, j, 0))
    spec = pltpu.PrefetchScalarGridSpec(num_scalar_prefetch=1, grid=(a0, a1 // tile), in_specs=[blk, blk], out_specs=out_blk)
    out = pl.pallas_call(body, name=name, out_shape=jax.ShapeDtypeStruct(out_shape, mine.dtype), grid_spec=spec,
                         compiler_params=pltpu.CompilerParams(vmem_limit_bytes=VMEM_LIMIT_BYTES,
                                                              dimension_semantics=("parallel", "parallel")))(c_arr, mine, other)
    return out.reshape((2 * a0, a1, a2) if half_axis == 0 else (a0, 2 * a1, a2))


def _row_tile(rows, cols, itemsize=4, budget=2 * 1024 * 1024):
    for t in (1024, 512, 256, 128, 64, 32, 16):
        if rows % t == 0 and t * cols * itemsize <= budget:
            return t
    return rows


def add_own_half(full, recv, half_axis, c_arr, *, name):
    a0, a1, a2 = recv.shape
    tile = _row_tile(a1, a2)

    def body(c_ref, f_ref, r_ref, o_ref):
        o_ref[...] = (f_ref[...] + r_ref[...]).astype(o_ref.dtype)

    if half_axis == 0:
        f_spec = pl.BlockSpec((None, tile, a2), lambda i, j, c_ref: (c_ref[0] * a0 + i, j, 0))
    else:
        f_spec = pl.BlockSpec((None, tile, a2), lambda i, j, c_ref: (i, c_ref[0] * (a1 // tile) + j, 0))
    blk = pl.BlockSpec((None, tile, a2), lambda i, j, c_ref: (i, j, 0))
    spec = pltpu.PrefetchScalarGridSpec(num_scalar_prefetch=1, grid=(a0, a1 // tile), in_specs=[f_spec, blk], out_specs=blk)
    return pl.pallas_call(body, name=name, out_shape=jax.ShapeDtypeStruct(recv.shape, BF16), grid_spec=spec,
                          compiler_params=pltpu.CompilerParams(vmem_limit_bytes=VMEM_LIMIT_BYTES,
                                                               dimension_semantics=("parallel", "parallel")))(c_arr, full, recv)


def sum_slots(q, *, name):
    _, a0, a1, a2 = q.shape
    tile = _row_tile(a1, a2, budget=1024 * 1024)

    def body(q_ref, o_ref):
        total = q_ref[0].astype(F32)
        for k in range(1, N_CHIPS):
            total = total + q_ref[k].astype(F32)
        o_ref[...] = total

    return _pcall(body, name=name, out_shape=jax.ShapeDtypeStruct((a0, a1, a2), F32), grid=(a0, a1 // tile),
                  in_specs=[pl.BlockSpec((N_CHIPS, None, tile, a2), lambda i, j: (0, i, j, 0))],
                  out_specs=pl.BlockSpec((None, tile, a2), lambda i, j: (i, j, 0)), sem=("parallel", "parallel"))(q)


def gather_weights(W):
    shards = []
    for name, _, _ in EXCHANGED:
        w = W[name].astype(BF16)
        if name == "gdn_w_in":
            w = jnp.pad(w, ((0, 0), (0, 0), (0, GDN_IN_SLOT - GDN_IN_SHARD)))
        if name == "kv_w":
            w = jnp.pad(w, ((0, 0), (0, KV_PAD_WIDTH - _KV_WIDTH)))[None]
        shards.append(w)
    fulls = dict(zip([n for n, _, _ in EXCHANGED], gather_shards(shards, [(sa, ha) for _, sa, ha in EXCHANGED])))
    slots = fulls["gdn_w_in"]
    fulls["gdn_w_in"] = jnp.concatenate([slots[..., k * GDN_IN_SLOT:k * GDN_IN_SLOT + GDN_IN_SHARD] for k in range(N_CHIPS)], axis=-1)
    fulls["kv_w_pad"] = fulls.pop("kv_w").reshape(-1, KV_PAD_WIDTH)
    conv = pack([W["gdn_conv"]], F32, row_multiple=8)
    conv_all = all_gather8(conv, name="gather_conv").reshape(N_DEV, conv.shape[0], PACK_COLS)
    fulls["gdn_conv"] = jnp.concatenate([unpack(conv_all[2 * k], [W["gdn_conv"].shape])[0] for k in range(N_CHIPS)], axis=-1)
    return fulls


def reduce_gradients(G):
    layout = [(sa, ha) for _, sa, ha in EXCHANGED]
    c_arr = lax.axis_index("c").astype(jnp.int32).reshape(1)
    received = swap_other_halves(G, layout)
    pairs = [add_own_half(g, r, ha, c_arr, name="grad_pair_sum") for g, r, (_, ha) in zip(G, received, layout)]
    slots = scatter_to_chips(pairs, layout)
    halves = [sum_slots(q, name="grad_chip_sum") for q in slots]
    others = swap_with_sibling(halves)
    return [join_halves(h, o, ha, c_arr, name="grad_join_halves") for h, o, (_, ha) in zip(halves, others, layout)]


def allreduce_small(G, names):
    packed = pack([G[n] for n in names], F32, row_multiple=8)
    gathered = all_gather8(packed, name="gather_small_grads").reshape(N_DEV, packed.shape[0], PACK_COLS)
    total = sum_leading(gathered, name="sum_small_grads")
    return dict(zip(names, unpack(total, [G[n].shape for n in names])))


def kernel(x, mem, *rest):
    n_w = len(WEIGHT_NAMES)
    W = dict(zip(WEIGHT_NAMES, rest[:n_w]))
    target = rest[n_w]
    M = dict(zip(WEIGHT_NAMES, rest[n_w + 1:2 * n_w + 1]))
    V = dict(zip(WEIGHT_NAMES, rest[2 * n_w + 1:3 * n_w + 1]))

    full = gather_weights(W)
    P = {n: W[n] for n in REPLICATED}
    P.update({n: full[n] for n in ("ffn1_w_gate_up", "ffn1_w_down", "ffn2_w_gate_up", "ffn2_w_down", "fox_w_in", "w_out",
                                   "mem_w_kv", "kv_w_pad")})
    derived = derived_weights(full["gdn_w_in"], full["gdn_conv"])
    P.update(gdn_w_in_pad=derived["gdn_w_in_pad"], conv8=derived["conv8"])

    part, dx, G = forward_backward(x[0], mem[0], target[0], P)
    loss = lax.psum(0.5 / x.shape[-1] * jnp.sum(part), ("x", "y", "c"))

    ref = reference_layout(G)
    exchange = {n: ref[n] for n, _, _ in EXCHANGED}
    exchange["gdn_w_in"] = jnp.concatenate(
        [jnp.pad(ref["gdn_w_in"][..., k * GDN_IN_SHARD:(k + 1) * GDN_IN_SHARD], ((0, 0), (0, 0), (0, GDN_IN_SLOT - GDN_IN_SHARD)))
         for k in range(N_CHIPS)], axis=-1)
    exchange["kv_w"] = G["kv_w_pad"].reshape(N_CHIPS, -1, KV_PAD_WIDTH)
    shards = dict(zip([n for n, _, _ in EXCHANGED], reduce_gradients([exchange[n] for n, _, _ in EXCHANGED])))
    shards["gdn_w_in"] = shards["gdn_w_in"][..., :GDN_IN_SHARD]
    shards["kv_w"] = shards["kv_w"][0, :, :_KV_WIDTH]
    grads = allreduce_small(ref, REPLICATED + ["gdn_conv"])
    conv_cols = W["gdn_conv"].shape[-1]
    chip = 2 * lax.axis_index("x") + lax.axis_index("y")
    grads["gdn_conv"] = lax.dynamic_slice_in_dim(grads["gdn_conv"], chip * conv_cols, conv_cols, axis=2)
    grads.update(shards)

    outs = {n: adamw(W[n], grads[n], M[n], V[n], name="adamw_" + n) for n in WEIGHT_NAMES}
    return (loss, dx[None], *[grads[n] for n in WEIGHT_NAMES], *[outs[n][0] for n in WEIGHT_NAMES],
            *[outs[n][1] for n in WEIGHT_NAMES], *[outs[n][2] for n in WEIGHT_NAMES])
```

```python
import jax
import jax.numpy as jnp
from jax import lax
from jax.experimental import pallas as pl
from jax.experimental.pallas import tpu as pltpu

F32, BF16 = jnp.float32, jnp.bfloat16
HI = lax.Precision.HIGHEST
MESH = pl.DeviceIdType.MESH

VMEM_LIMIT_BYTES = 48 * 1024 * 1024
LANES = 128
EPS = 1e-6
NEG_INF = -1e30

D_MODEL = 1024
HEAD_DIM = 128
GDN_HEADS = 6
GDN_WIDTH = GDN_HEADS * HEAD_DIM
FOX_HEADS = 6
FOX_WIDTH = FOX_HEADS * HEAD_DIM
MEM_HEADS = 4
MEM_HEAD_DIM = 64
MEM_WIDTH = MEM_HEADS * MEM_HEAD_DIM
FFN_HIDDEN = 2816
CONV_WIDTH = 4
GDN_CHUNK = 128
N_CHIPS = 4
N_DEV = 8

ADAM_LR, ADAM_B1, ADAM_B2, ADAM_EPS, ADAM_WD, ADAM_STEP = 0.001, 0.9, 0.999, 1e-08, 0.01, 10


def _pcall(body, *, name, out_shape, grid=(), in_specs=None, out_specs=None, scratch=(), sem=None):
    params = dict(vmem_limit_bytes=VMEM_LIMIT_BYTES)
    if sem is not None:
        params["dimension_semantics"] = sem
    kw = dict(grid=grid, in_specs=in_specs, out_specs=out_specs) if grid else {}
    return pl.pallas_call(body, name=name, out_shape=out_shape, scratch_shapes=list(scratch),
                          compiler_params=pltpu.CompilerParams(**params), **kw)


def _pick(n, cands):
    for c in cands:
        if n % c == 0:
            return c
    return n


def _make_dot(dtype, precision):
    def raw(a, b, dims):
        return lax.dot_general(a.astype(dtype), b.astype(dtype), (dims, ((), ())),
                               precision=precision, preferred_element_type=F32)

    @jax.custom_vjp
    def dot(a, b):
        return raw(a, b, ((1,), (0,)))

    def fwd(a, b):
        return dot(a, b), (a, b)

    def bwd(resid, ct):
        a, b = resid
        return raw(ct, b, ((1,), (1,))).astype(a.dtype), raw(a, ct, ((0,), (0,))).astype(b.dtype)

    dot.defvjp(fwd, bwd)
    dot.nn = lambda a, b: raw(a, b, ((1,), (0,)))
    dot.nt = lambda a, b: raw(a, b, ((1,), (1,)))
    dot.tn = lambda a, b: raw(a, b, ((0,), (0,)))
    return dot


bdot = _make_dot(BF16, None)
fdot = _make_dot(F32, HI)
idot = _make_dot(F32, lax.Precision.HIGH)
sdot = idot


def _sigmoid(x):
    return 0.5 * jnp.tanh(0.5 * x) + 0.5


def _silu(x):
    return x * _sigmoid(x)


def _softplus(x):
    return jnp.maximum(x, 0.0) + jnp.log(1.0 + jnp.exp(-jnp.abs(x)))


def _log_sigmoid(x):
    return -_softplus(-x)


def _iota2(shape, dim):
    return lax.broadcasted_iota(jnp.int32, shape, dim)


def mm(a, b, *, ta=False, tb=False, a_split=False, b_split=False, out_dtype=F32, scale=1.0, res=None, name):
    assert not (a_split and ta) and not (b_split and tb)
    (K, M) = a.shape if ta else ((2 * a.shape[2], a.shape[1]) if a_split else a.shape[::-1])
    (N, Kb) = b.shape if tb else ((2 * b.shape[2], b.shape[1]) if b_split else b.shape[::-1])
    assert K == Kb, (a.shape, b.shape, ta, tb)
    tm = _pick(M, (1024, 1408, 512, 256, 128))
    tn = _pick(N, (1024, 1408, 1152, 1664, 768, 512, 384, 256, 128))
    tk = _pick(K, (1024, 512, 256, 128)) if ta else _pick(K, (1024, 1408, 1152, 1664, 512, 256, 128))
    assert not a_split or (K // 2) % tk == 0
    assert not b_split or (N // 2) % tn == 0
    nk = K // tk
    dims = (((0 if ta else 1,), (1 if tb else 0,)), ((), ()))

    def body(a_ref, b_ref, *rest):
        o_ref, acc = rest[-2], rest[-1]
        k = pl.program_id(2)

        @pl.when(k == 0)
        def _():
            acc[...] = jnp.zeros_like(acc)

        acc[...] += lax.dot_general(a_ref[...].astype(BF16), b_ref[...].astype(BF16), dims,
                                    preferred_element_type=F32)

        @pl.when(k == nk - 1)
        def _():
            out = acc[...] * scale
            if res is not None:
                out = out + rest[0][...].astype(F32)
            o_ref[...] = out.astype(o_ref.dtype)

    a_spec = pl.BlockSpec((tk, tm), lambda i, j, k: (k, i)) if ta else pl.BlockSpec((tm, tk), lambda i, j, k: (i, k))
    b_spec = pl.BlockSpec((tn, tk), lambda i, j, k: (j, k)) if tb else pl.BlockSpec((tk, tn), lambda i, j, k: (k, j))
    if a_split:
        per_half = K // 2 // tk
        a_spec = pl.BlockSpec((None, tm, tk), lambda i, j, k: (k // per_half, i, k % per_half))
    if b_split:
        per_half = N // 2 // tn
        b_spec = pl.BlockSpec((None, tk, tn), lambda i, j, k: (j // per_half, k, j % per_half))
    o_spec = pl.BlockSpec((tm, tn), lambda i, j, k: (i, j))
    ins, specs = [a, b], [a_spec, b_spec]
    if res is not None:
        ins.append(res)
        specs.append(o_spec)
    return _pcall(body, name=name, out_shape=jax.ShapeDtypeStruct((M, N), out_dtype),
                  grid=(M // tm, N // tn, nk), in_specs=specs, out_specs=o_spec,
                  scratch=[pltpu.VMEM((tm, tn), F32)], sem=("parallel", "parallel", "arbitrary"))(*ins)


def mm_rms_bwd(a, b, x, gain_row, dres, *, a_split=False, name):
    (K, M) = (2 * a.shape[2], a.shape[1]) if a_split else a.shape[::-1]
    D = b.shape[0]
    assert b.shape[1] == K and x.shape == (M, D)
    tm = _pick(M, (1024, 512, 256, 128))
    rows = _pick(tm, (256, 128))
    tk = _pick(K, (1024, 1408, 1152, 1664, 512, 256, 128))
    assert not a_split or (K // 2) % tk == 0
    nk = K // tk

    def body(a_ref, b_ref, x_ref, g_ref, r_ref, o_ref, dg_ref, acc):
        i, k = pl.program_id(0), pl.program_id(1)

        @pl.when(k == 0)
        def _():
            acc[...] = jnp.zeros_like(acc)

        acc[...] += bdot.nt(a_ref[...], b_ref[...])

        @pl.when(k == nk - 1)
        def _():
            dg = jnp.zeros((1, D), F32)
            for s in range(tm // rows):
                sl = slice(s * rows, (s + 1) * rows)
                xv = x_ref[sl, :]
                rstd = lax.rsqrt(jnp.mean(xv * xv, axis=-1, keepdims=True) + EPS)
                xh = xv * rstd
                dn = acc[sl, :]
                dy = dn * g_ref[...]
                o_ref[sl, :] = (dy - xh * jnp.mean(dy * xh, axis=-1, keepdims=True)) * rstd + r_ref[sl, :]
                dg = dg + jnp.sum(dn * xh, axis=0, keepdims=True)

            @pl.when(i == 0)
            def _():
                dg_ref[...] = dg

            @pl.when(i > 0)
            def _():
                dg_ref[...] += dg

    a_spec = pl.BlockSpec((tm, tk), lambda i, k: (i, k))
    if a_split:
        per_half = K // 2 // tk
        a_spec = pl.BlockSpec((None, tm, tk), lambda i, k: (k // per_half, i, k % per_half))
    row = pl.BlockSpec((tm, D), lambda i, k: (i, 0))
    one = pl.BlockSpec((1, D), lambda i, k: (0, 0))
    return _pcall(body, name=name, out_shape=[jax.ShapeDtypeStruct((M, D), F32), jax.ShapeDtypeStruct((1, D), F32)],
                  grid=(M // tm, nk), in_specs=[a_spec, pl.BlockSpec((D, tk), lambda i, k: (0, k)), row, one, row],
                  out_specs=[row, one], scratch=[pltpu.VMEM((tm, D), F32)], sem=("arbitrary", "arbitrary"))(a, b, x, gain_row, dres)


def _row_spec(r, tile):
    if isinstance(r, tuple):
        arr, width, col = r
        return arr, pl.BlockSpec((tile, width), lambda i, col=col: (i, col))
    return r, pl.BlockSpec((tile, r.shape[1]), lambda i: (i, 0))


def _const_spec(c):
    return pl.BlockSpec(c.shape, lambda i: (0,) * c.ndim)


def rowwise(fn, rows, consts, outs, *, tile, name):
    arrs, specs = zip(*[_row_spec(r, tile) for r in rows])
    n_rows = arrs[0].shape[0]
    tile = min(tile, n_rows)
    n_in = len(rows) + len(consts)

    def body(*refs):
        res = fn(*[r[...] for r in refs[:n_in]])
        for o_ref, o in zip(refs[n_in:], res):
            o_ref[...] = o.astype(o_ref.dtype)

    arrs, specs = zip(*[_row_spec(r, tile) for r in rows])
    return _pcall(body, name=name,
                  out_shape=[jax.ShapeDtypeStruct((n_rows, w), dt) for w, dt in outs],
                  grid=(n_rows // tile,),
                  in_specs=list(specs) + [_const_spec(c) for c in consts],
                  out_specs=[pl.BlockSpec((tile, w), lambda i: (i, 0)) for w, _ in outs],
                  sem=("parallel",))(*arrs, *consts)


def rowwise_bwd(fn, rows, consts, cts, *, tile, name, row_grads, const_grads, add=None):
    arrs, _ = zip(*[_row_spec(r, tile) for r in rows])
    n_rows = arrs[0].shape[0]
    tile = min(tile, n_rows)
    arrs, specs = zip(*[_row_spec(r, tile) for r in rows])
    ct_arrs, ct_specs = zip(*[_row_spec(c, tile) for c in cts])
    add = add or {}
    add_idx = sorted(add)
    add_arrs, add_specs = (zip(*[_row_spec(add[i], tile) for i in add_idx]) if add_idx else ((), ()))
    nr, nc, nct, na = len(rows), len(consts), len(cts), len(add_idx)
    want_rows = [i for i, d in enumerate(row_grads) if d is not None]
    want_consts = [i for i, w in enumerate(const_grads) if w]

    def body(*refs):
        row_v = [r[...] for r in refs[:nr]]
        const_v = [r[...] for r in refs[nr:nr + nc]]
        ct_v = [r[...] for r in refs[nr + nc:nr + nc + nct]]
        add_v = {i: refs[nr + nc + nct + j][...] for j, i in enumerate(add_idx)}
        out_refs = refs[nr + nc + nct + na:]
        res, vjp = jax.vjp(fn, *row_v, *const_v)
        grads = vjp(tuple(c.astype(o.dtype) for c, o in zip(ct_v, res)))
        for o_ref, i in zip(out_refs, want_rows):
            g = grads[i].astype(F32)
            if i in add_v:
                g = g + add_v[i].astype(F32)
            o_ref[...] = g.astype(o_ref.dtype)
        first = pl.program_id(0) == 0
        for o_ref, i in zip(out_refs[len(want_rows):], want_consts):
            g = grads[nr + i].astype(F32)

            @pl.when(first)
            def _(o_ref=o_ref, g=g):
                o_ref[...] = g

            @pl.when(jnp.logical_not(first))
            def _(o_ref=o_ref, g=g):
                o_ref[...] += g

    def width(r):
        return r[1] if isinstance(r, tuple) else r.shape[1]

    out_shape = [jax.ShapeDtypeStruct((n_rows, width(rows[i])), row_grads[i]) for i in want_rows]
    out_shape += [jax.ShapeDtypeStruct(consts[i].shape, F32) for i in want_consts]
    out_specs = [pl.BlockSpec((tile, width(rows[i])), lambda i_: (i_, 0)) for i in want_rows]
    out_specs += [_const_spec(consts[i]) for i in want_consts]
    return _pcall(body, name=name, out_shape=out_shape, grid=(n_rows // tile,),
                  in_specs=list(specs) + [_const_spec(c) for c in consts] + list(ct_specs) + list(add_specs),
                  out_specs=out_specs, sem=("arbitrary",))(*arrs, *consts, *ct_arrs, *add_arrs)


def f_rmsnorm(x, g):
    x = x.astype(F32)
    return (x * lax.rsqrt(jnp.mean(x * x, axis=-1, keepdims=True) + EPS) * g,)


def _head_sel(first_lane):
    r, c = _iota2((LANES, GDN_WIDTH), 0), _iota2((LANES, GDN_WIDTH), 1)
    return (r == c // HEAD_DIM + first_lane).astype(F32)


def _tri(n, strict=False):
    r, c = _iota2((n, n), 0), _iota2((n, n), 1)
    return r > c if strict else r >= c


def f_gdn_pre(xc, ab, a_log, dt_bias):
    s = _silu(xc.astype(F32))
    qs, ks = [], []
    for h in range(GDN_HEADS):
        qh = s[:, h * HEAD_DIM:(h + 1) * HEAD_DIM]
        kh = s[:, GDN_WIDTH + h * HEAD_DIM:GDN_WIDTH + (h + 1) * HEAD_DIM]
        qs.append(qh * lax.rsqrt(jnp.sum(qh * qh, axis=-1, keepdims=True) + EPS) * (HEAD_DIM ** -0.5))
        ks.append(kh * lax.rsqrt(jnp.sum(kh * kh, axis=-1, keepdims=True) + EPS))
    q, k = jnp.concatenate(qs, axis=1), jnp.concatenate(ks, axis=1)
    v = s[:, 2 * GDN_WIDTH:]
    ab = ab.astype(F32)
    g = -jnp.exp(a_log) * _softplus(ab + dt_bias)
    gc = _head_broadcast(fdot(_tri(GDN_CHUNK).astype(F32), g), 0)
    beta = _head_broadcast(_sigmoid(ab), GDN_HEADS)
    return q, k, v, gc, beta


def _head_broadcast(x, first_lane):
    lane = _iota2(x.shape, 1)
    cols = [jnp.sum(jnp.where(lane == first_lane + h, x, 0.0), axis=1, keepdims=True) for h in range(GDN_HEADS)]
    return jnp.concatenate([jnp.broadcast_to(c, x.shape) for c in cols], axis=1)


def _unit_lower_inverses(neg_lowers):
    C = neg_lowers[0].shape[0]
    eye = (_iota2((C, C), 0) == _iota2((C, C), 1)).astype(F32)
    invs = [eye + n for n in neg_lowers]
    powers = list(neg_lowers)
    for _ in range(6):
        powers = [idot.nn(p, p) for p in powers]
        invs = [inv + idot.nn(p, inv) for p, inv in zip(powers, invs)]
    return invs


@jax.custom_vjp
def _solve_with_inverse(inv, neg_lower, rhs):
    return idot.nn(inv, rhs)


def _solve_fwd(inv, neg_lower, rhs):
    x = idot.nn(inv, rhs)
    return x, (inv, x)


def _solve_bwd(resid, ct):
    inv, x = resid
    d_rhs = idot.tn(inv, ct)
    return jnp.zeros_like(inv), idot.nt(d_rhs, x), d_rhs


_solve_with_inverse.defvjp(_solve_fwd, _solve_bwd)


def f_gdn_intra(q, k, v, gc, beta, inv=None):
    C = GDN_CHUNK
    causal, strict = _tri(C), _tri(C, strict=True)
    is_last = _iota2((C, HEAD_DIM), 0) == C - 1
    heads = range(GDN_HEADS)
    sls = [slice(h * HEAD_DIM, (h + 1) * HEAD_DIM) for h in heads]
    qs, ks, vs, gs, bs = ([a[:, sl] for sl in sls] for a in (q, k, v, gc, beta))
    decays = [jnp.where(causal, jnp.exp(jnp.where(causal, g - g.T, 0.0)), 0.0) for g in gs]
    kbs = [kh * bh for kh, bh in zip(ks, bs)]
    kts = [kh.T for kh in ks]
    neg_lowers = [jnp.where(strict, -(idot(kb, kt) * d), 0.0) for kb, kt, d in zip(kbs, kts, decays)]
    qks = [jnp.where(causal, idot(qh, kt) * d, 0.0) for qh, kt, d in zip(qs, kts, decays)]
    rhss = [jnp.concatenate([vh * bh, kb * jnp.exp(g)], axis=1) for vh, bh, kb, g in zip(vs, bs, kbs, gs)]
    if inv is None:
        invs = _unit_lower_inverses(neg_lowers)
        sols = [idot.nn(m, r) for m, r in zip(invs, rhss)]
    else:
        sols = [_solve_with_inverse(inv[:, sl], n, r) for sl, n, r in zip(sls, neg_lowers, rhss)]
    g_lasts = [jnp.sum(jnp.where(is_last, g, 0.0), axis=0, keepdims=True) for g in gs]
    outs = [[s[:, :HEAD_DIM] for s in sols], [s[:, HEAD_DIM:] for s in sols], qks,
            [kh * jnp.exp(gl - g) for kh, gl, g in zip(ks, g_lasts, gs)], [qh * jnp.exp(g) for qh, g in zip(qs, gs)]]
    if inv is None:
        outs.append(invs)
    return tuple(jnp.concatenate(o, axis=1) for o in outs)


def f_gdn_post(o, z, gain):
    z = z.astype(F32)
    parts = []
    for h in range(GDN_HEADS):
        oh = o[:, h * HEAD_DIM:(h + 1) * HEAD_DIM]
        parts.append(oh * lax.rsqrt(jnp.mean(oh * oh, axis=-1, keepdims=True) + EPS) * gain)
    return (jnp.concatenate(parts, axis=1) * _silu(z),)


def f_mem_attn(q, k, v):
    q = q.astype(F32)
    lane_head = _iota2((1, MEM_WIDTH), 1) // MEM_HEAD_DIM
    kt = k.astype(F32).T
    masks = [(lane_head == h).astype(F32) for h in range(MEM_HEADS)]
    logits = [bdot(q * mask, kt) * (MEM_HEAD_DIM ** -0.5) for mask in masks]
    ps = [jnp.exp(s - jnp.max(s, axis=-1, keepdims=True)) for s in logits]
    ps = [p / jnp.sum(p, axis=-1, keepdims=True) for p in ps]
    outs = [bdot(p, v) * mask for p, mask in zip(ps, masks)]
    return ((outs[0] + outs[1]) + (outs[2] + outs[3]),)


def f_loss(y, t):
    d = y - t
    return (d * d,)


def conv_fwd(proj, w8, *, width, tile=512):
    n_rows = proj.shape[0]
    tile = min(tile, n_rows)

    def body(x_ref, halo_ref, w_ref, o_ref):
        i = pl.program_id(0)
        halo = jnp.where(i > 0, halo_ref[...].astype(F32), 0.0)
        xs = jnp.concatenate([halo, x_ref[...].astype(F32)], axis=0)
        acc = xs[8:] * w_ref[3:4, :]
        for j in range(CONV_WIDTH - 1):
            acc = acc + pltpu.roll(xs, CONV_WIDTH - 1 - j, 0)[8:] * w_ref[j:j + 1, :]
        o_ref[...] = acc

    return _pcall(body, name="gdn_conv_fwd", out_shape=jax.ShapeDtypeStruct((n_rows, width), F32),
                  grid=(n_rows // tile,),
                  in_specs=[pl.BlockSpec((tile, width), lambda i: (i, 0)),
                            pl.BlockSpec((8, width), lambda i: (jnp.maximum(i * (tile // 8) - 1, 0), 0)),
                            pl.BlockSpec((8, width), lambda i: (0, 0))],
                  out_specs=pl.BlockSpec((tile, width), lambda i: (i, 0)), sem=("parallel",))(proj, proj, w8)


def conv_bwd(proj, w8, dy, *, width, tile=512):
    n_rows = proj.shape[0]
    tile = min(tile, n_rows)
    n = n_rows // tile

    def body(x_ref, xhalo_ref, w_ref, dy_ref, dyhalo_ref, dx_ref, dw_ref):
        i = pl.program_id(0)
        dy = dy_ref[...]
        after = jnp.where(i < n - 1, dyhalo_ref[...], 0.0)
        ds = jnp.concatenate([dy, after], axis=0)
        dx = dy * w_ref[3:4, :]
        for j in range(CONV_WIDTH - 1):
            shift = CONV_WIDTH - 1 - j
            dx = dx + pltpu.roll(ds, tile + 8 - shift, 0)[:tile] * w_ref[j:j + 1, :]
        dx_ref[...] = dx.astype(dx_ref.dtype)
        halo = jnp.where(i > 0, xhalo_ref[...].astype(F32), 0.0)
        xs = jnp.concatenate([halo, x_ref[...].astype(F32)], axis=0)
        rows = [jnp.sum(dy * pltpu.roll(xs, CONV_WIDTH - 1 - j, 0)[8:], axis=0, keepdims=True)
                for j in range(CONV_WIDTH - 1)]
        rows.append(jnp.sum(dy * xs[8:], axis=0, keepdims=True))
        dw = jnp.concatenate(rows + [jnp.zeros((8 - CONV_WIDTH, width), F32)], axis=0)

        @pl.when(i == 0)
        def _():
            dw_ref[...] = dw

        @pl.when(i > 0)
        def _():
            dw_ref[...] += dw

    t8 = tile // 8
    return _pcall(body, name="gdn_conv_bwd",
                  out_shape=[jax.ShapeDtypeStruct((n_rows, width), BF16), jax.ShapeDtypeStruct((8, width), F32)],
                  grid=(n,),
                  in_specs=[pl.BlockSpec((tile, width), lambda i: (i, 0)),
                            pl.BlockSpec((8, width), lambda i: (jnp.maximum(i * t8 - 1, 0), 0)),
                            pl.BlockSpec((8, width), lambda i: (0, 0)),
                            pl.BlockSpec((tile, width), lambda i: (i, 0)),
                            pl.BlockSpec((8, width), lambda i: (jnp.minimum((i + 1) * t8, n * t8 - 1), 0))],
                  out_specs=[pl.BlockSpec((tile, width), lambda i: (i, 0)), pl.BlockSpec((8, width), lambda i: (0, 0))],
                  sem=("arbitrary",))(proj, proj, w8, dy, dy)


def gdn_scan_fwd(u, w, qk, kt, qh, gc):
    n_rows = u.shape[0]
    C, n = GDN_CHUNK, u.shape[0] // GDN_CHUNK

    def body(u_ref, w_ref, qk_ref, kt_ref, qh_ref, gc_ref, o_ref, vn_ref, sin_ref, st):
        @pl.when(pl.program_id(0) == 0)
        def _():
            st[...] = jnp.zeros_like(st)

        sin_ref[0] = st[...]
        sls = [slice(h * HEAD_DIM, (h + 1) * HEAD_DIM) for h in range(GDN_HEADS)]
        states = [st[sl, :] for sl in sls]
        v_news = [u_ref[:, sl] - sdot(w_ref[:, sl], s) for sl, s in zip(sls, states)]
        from_state = [sdot(qh_ref[:, sl], s) for sl, s in zip(sls, states)]
        for sl, a, v_new in zip(sls, from_state, v_news):
            o_ref[:, sl] = a + sdot(qk_ref[:, sl], v_new)
            vn_ref[:, sl] = v_new
        for sl, s, v_new in zip(sls, states, v_news):
            st[sl, :] = s * jnp.exp(gc_ref[C - 1:C, sl]) + sdot.tn(kt_ref[:, sl], v_new)

    blk = pl.BlockSpec((C, GDN_WIDTH), lambda i: (i, 0))
    return _pcall(body, name="gdn_scan_fwd",
                  out_shape=[jax.ShapeDtypeStruct((n_rows, GDN_WIDTH), F32), jax.ShapeDtypeStruct((n_rows, GDN_WIDTH), F32),
                             jax.ShapeDtypeStruct((n, GDN_WIDTH, HEAD_DIM), F32)],
                  grid=(n,), in_specs=[blk] * 6,
                  out_specs=[blk, blk, pl.BlockSpec((1, GDN_WIDTH, HEAD_DIM), lambda i: (i, 0, 0))],
                  scratch=[pltpu.VMEM((GDN_WIDTH, HEAD_DIM), F32)], sem=("arbitrary",))(u, w, qk, kt, qh, gc)


def gdn_scan_bwd(do, w, qk, kt, qh, gc, vn, sin):
    n_rows = do.shape[0]
    C, n = GDN_CHUNK, do.shape[0] // GDN_CHUNK

    def body(do_ref, w_ref, qk_ref, kt_ref, qh_ref, gc_ref, vn_ref, sin_ref,
             du_ref, dw_ref, dqk_ref, dkt_ref, dqh_ref, dgl_ref, dst):
        @pl.when(pl.program_id(0) == 0)
        def _():
            dst[...] = jnp.zeros_like(dst)

        sls = [slice(h * HEAD_DIM, (h + 1) * HEAD_DIM) for h in range(GDN_HEADS)]
        dvns = [sdot.tn(qk_ref[:, sl], do_ref[:, sl]) + sdot(kt_ref[:, sl], dst[sl, :]) for sl in sls]
        for sl, dvn in zip(sls, dvns):
            du_ref[:, sl] = dvn
            dw_ref[:, sl] = -sdot.nt(dvn, sin_ref[0, sl, :])
        for sl in sls:
            dqk_ref[:, sl] = sdot.nt(do_ref[:, sl], vn_ref[:, sl])
            dkt_ref[:, sl] = sdot.nt(vn_ref[:, sl], dst[sl, :])
            dqh_ref[:, sl] = sdot.nt(do_ref[:, sl], sin_ref[0, sl, :])
        for sl, dvn in zip(sls, dvns):
            ds_out = dst[sl, :]
            e = jnp.exp(gc_ref[C - 1:C, sl])
            dgl = jnp.sum(ds_out * sin_ref[0, sl, :], axis=0, keepdims=True) * e
            dgl_ref[:, sl] = jnp.broadcast_to(dgl, (8, HEAD_DIM))
            dst[sl, :] = sdot.tn(qh_ref[:, sl], do_ref[:, sl]) + e * ds_out - sdot.tn(w_ref[:, sl], dvn)

    blk = pl.BlockSpec((C, GDN_WIDTH), lambda i: (n - 1 - i, 0))
    row = jax.ShapeDtypeStruct((n_rows, GDN_WIDTH), F32)
    return _pcall(body, name="gdn_scan_bwd",
                  out_shape=[row] * 5 + [jax.ShapeDtypeStruct((n * 8, GDN_WIDTH), F32)],
                  grid=(n,), in_specs=[blk] * 7 + [pl.BlockSpec((1, GDN_WIDTH, HEAD_DIM), lambda i: (n - 1 - i, 0, 0))],
                  out_specs=[blk] * 5 + [pl.BlockSpec((8, GDN_WIDTH), lambda i: (n - 1 - i, 0))],
                  scratch=[pltpu.VMEM((GDN_WIDTH, HEAD_DIM), F32)], sem=("arbitrary",))(do, w, qk, kt, qh, gc, vn, sin)


def gdn_intra_bwd(q, k, v, gc, beta, inv, cts, dgl):
    n_rows = q.shape[0]
    C = GDN_CHUNK

    def body(*refs):
        ins = [r[...] for r in refs[:5]]
        inv_v = refs[5][...]
        ct = tuple(r[...] for r in refs[6:11])
        dgl_v = refs[11][...]
        _, vjp = jax.vjp(lambda *a: f_gdn_intra(*a, inv=inv_v), *ins)
        grads = list(vjp(ct))
        last = _iota2((C, GDN_WIDTH), 0) == C - 1
        grads[3] = grads[3] + jnp.where(last, jnp.broadcast_to(dgl_v[0:1, :], (C, GDN_WIDTH)), 0.0)
        for o_ref, g in zip(refs[12:], grads):
            o_ref[...] = g

    blk = pl.BlockSpec((C, GDN_WIDTH), lambda i: (i, 0))
    return _pcall(body, name="gdn_intra_bwd", out_shape=[jax.ShapeDtypeStruct((n_rows, GDN_WIDTH), F32)] * 5,
                  grid=(n_rows // C,), in_specs=[blk] * 11 + [pl.BlockSpec((8, GDN_WIDTH), lambda i: (i, 0))],
                  out_specs=[blk] * 5, sem=("parallel",))(q, k, v, gc, beta, inv, *cts, dgl)


def fox_gate_fwd(f, b_f):
    n_rows = f.shape[0]
    T = LANES

    def body(f_ref, b_ref, cb_ref, carry):
        @pl.when(pl.program_id(0) == 0)
        def _():
            carry[...] = jnp.zeros_like(carry)

        c = fdot(_tri(T).astype(F32), _log_sigmoid(f_ref[...] + b_ref[...])) + carry[...]
        carry[...] = c[T - 1:T, :]
        cb_ref[...] = fdot(c, _head_sel(0))

    return _pcall(body, name="fox_gate_fwd", out_shape=jax.ShapeDtypeStruct((n_rows, FOX_WIDTH), F32),
                  grid=(n_rows // T,),
                  in_specs=[pl.BlockSpec((T, LANES), lambda i: (i, 0)), pl.BlockSpec((1, LANES), lambda i: (0, 0))],
                  out_specs=pl.BlockSpec((T, FOX_WIDTH), lambda i: (i, 0)),
                  scratch=[pltpu.VMEM((1, LANES), F32)], sem=("arbitrary",))(f, b_f)


def fox_gate_bwd(f, b_f, dcrow, dcb):
    n_rows = f.shape[0]
    T = LANES
    n = n_rows // T

    def body(f_ref, b_ref, dc_ref, dcb_ref, df_ref, db_ref, carry):
        i = pl.program_id(0)

        @pl.when(i == 0)
        def _():
            carry[...] = jnp.zeros_like(carry)

        rows = [dc_ref[h] for h in range(FOX_HEADS)] + [jnp.zeros((T - FOX_HEADS, T), F32)]
        first_lane = (_iota2((FOX_WIDTH, LANES), 0) == _iota2((FOX_WIDTH, LANES), 1) * HEAD_DIM).astype(F32)
        dc = jnp.concatenate(rows, axis=0).T + fdot(dcb_ref[...], first_lane)
        dlog = fdot.tn(_tri(T).astype(F32), dc) + carry[...]
        carry[...] = dlog[0:1, :]
        df = dlog * (1.0 - _sigmoid(f_ref[...] + b_ref[...]))
        df_ref[...] = df
        db = jnp.sum(df, axis=0, keepdims=True)

        @pl.when(i == 0)
        def _():
            db_ref[...] = db

        @pl.when(i > 0)
        def _():
            db_ref[...] += db

    return _pcall(body, name="fox_gate_bwd",
                  out_shape=[jax.ShapeDtypeStruct((n_rows, LANES), F32), jax.ShapeDtypeStruct((1, LANES), F32)],
                  grid=(n,),
                  in_specs=[pl.BlockSpec((T, LANES), lambda i: (n - 1 - i, 0)), pl.BlockSpec((1, LANES), lambda i: (0, 0)),
                            pl.BlockSpec((FOX_HEADS, 1, T), lambda i: (0, 0, n - 1 - i)),
                            pl.BlockSpec((T, FOX_WIDTH), lambda i: (n - 1 - i, 0))],
                  out_specs=[pl.BlockSpec((T, LANES), lambda i: (n - 1 - i, 0)), pl.BlockSpec((1, LANES), lambda i: (0, 0))],
                  scratch=[pltpu.VMEM((1, LANES), F32)], sem=("arbitrary",))(f, b_f, dcrow, dcb)


FOX_AUG = 2 * HEAD_DIM


def _fox_tiles(n_rows):
    return min(1024, n_rows), min(1024, n_rows)


def _fox_pairs(n_rows, query_major):
    tq, tk = _fox_tiles(n_rows)
    nq, r = n_rows // tq, tq // tk
    if query_major:
        pairs = [(i, j) for i in range(nq) for j in range(r * (i + 1))]
    else:
        pairs = [(i, j) for j in range(nq * r) for i in range(j // r, nq)]
    return jnp.asarray([p[0] for p in pairs], jnp.int32), jnp.asarray([p[1] for p in pairs], jnp.int32)


def fox_augment(x, cb, query_side):
    def fn(xt, ct):
        xt = xt.astype(F32)
        lane = _iota2((xt.shape[0], HEAD_DIM), 1)
        parts = []
        for h in range(FOX_HEADS):
            sl = slice(h * HEAD_DIM, (h + 1) * HEAD_DIM)
            c = ct[:, sl]
            hi = c.astype(BF16).astype(F32)
            mid = (c - hi).astype(BF16).astype(F32)
            lo = (c - hi - mid).astype(BF16).astype(F32)
            terms = jnp.where(lane % 3 == 0, hi, jnp.where(lane % 3 == 1, mid, lo))
            if query_side:
                extra = jnp.where(lane < 3, terms, jnp.where(lane < 6, 1.0, 0.0))
                parts += [xt[:, sl] * (HEAD_DIM ** -0.5), extra]
            else:
                extra = jnp.where(lane < 3, 1.0, jnp.where(lane < 6, -terms, 0.0))
                parts += [xt[:, sl], extra]
        return (jnp.concatenate(parts, axis=1),)

    return rowwise(fn, [(x, FOX_WIDTH, 0), cb], [], [(FOX_HEADS * FOX_AUG, BF16)], tile=ROW_TILE,
                   name="fox_augment_q" if query_side else "fox_augment_k")[0]


def _pcall_tables(body, *, name, out_shape, grid, tables, in_specs, out_specs, scratch, sem):
    spec = pltpu.PrefetchScalarGridSpec(num_scalar_prefetch=len(tables), grid=grid, in_specs=in_specs, out_specs=out_specs,
                                        scratch_shapes=list(scratch))
    return pl.pallas_call(body, name=name, out_shape=out_shape, grid_spec=spec,
                          compiler_params=pltpu.CompilerParams(vmem_limit_bytes=VMEM_LIMIT_BYTES, dimension_semantics=sem))


def _fox_logits(qa, ka, offset):
    s = bdot.nt(qa, ka)
    if offset is not None:
        s = jnp.where(_iota2(s.shape, 0) + offset >= _iota2(s.shape, 1), s, NEG_INF)
    return s


def _fox_p_ds(offset, qa_ref, ka_ref, v_ref, o_ref, lse_ref, do_ref):
    s = _fox_logits(qa_ref[...], ka_ref[...], offset)
    p = jnp.exp(s - jnp.tile(lse_ref[...], (1, s.shape[1] // LANES)))
    d_o = do_ref[...].astype(F32)
    delta = jnp.sum(d_o * o_ref[...].astype(F32), axis=-1, keepdims=True)
    return p, p * (bdot.nt(d_o, v_ref[...]) - delta), d_o


def _fox_on_diagonal(i, j, r, tk, step):
    @pl.when(j < r * i)
    def _():
        step(None)

    for m in range(r):
        @pl.when(j == r * i + m)
        def _(m=m):
            step(-m * tk)


def _fox_specs(tq, tk, do_col):
    qaspec = pl.BlockSpec((tq, FOX_AUG), lambda h, p, it, jt: (it[p], h))
    qspec = pl.BlockSpec((tq, HEAD_DIM), lambda h, p, it, jt: (it[p], h))
    dospec = pl.BlockSpec((tq, HEAD_DIM), lambda h, p, it, jt: (it[p], do_col + h))
    kaspec = pl.BlockSpec((tk, FOX_AUG), lambda h, p, it, jt: (jt[p], h))
    kspec = pl.BlockSpec((tk, HEAD_DIM), lambda h, p, it, jt: (jt[p], h))
    vspec = pl.BlockSpec((tk, HEAD_DIM), lambda h, p, it, jt: (jt[p], FOX_HEADS + h))
    cspec = pl.BlockSpec((1, 1, tk), lambda h, p, it, jt: (h, 0, jt[p]))
    return qaspec, qspec, dospec, kaspec, kspec, vspec, cspec


def fox_fwd(qa, kv, ka):
    n_rows = kv.shape[0]
    tq, tk = _fox_tiles(n_rows)
    r = tq // tk
    tables = _fox_pairs(n_rows, True)

    def body(it, jt, qa_ref, ka_ref, v_ref, o_ref, lse_ref, m_sc, l_sc, acc):
        i, j = it[pl.program_id(1)], jt[pl.program_id(1)]

        @pl.when(j == 0)
        def _():
            m_sc[...] = jnp.full(m_sc.shape, NEG_INF, F32)
            l_sc[...] = jnp.zeros_like(l_sc)
            acc[...] = jnp.zeros_like(acc)

        def step(offset):
            s = _fox_logits(qa_ref[...], ka_ref[...], offset)
            m_old = m_sc[...]
            m_new = jnp.maximum(m_old, jnp.max(s, axis=-1, keepdims=True))
            alpha = jnp.exp(m_old - m_new)
            p = jnp.exp(s - jnp.tile(m_new, (1, tk // LANES)))
            l_sc[...] = l_sc[...] * alpha + jnp.sum(p, axis=-1, keepdims=True)
            acc[...] = acc[...] * alpha + bdot(p, v_ref[...])
            m_sc[...] = m_new

        _fox_on_diagonal(i, j, r, tk, step)

        @pl.when(j == r * i + r - 1)
        def _():
            o_ref[...] = (acc[...] / l_sc[...]).astype(o_ref.dtype)
            lse_ref[...] = m_sc[...] + jnp.log(l_sc[...])

    qaspec, qspec, _, kaspec, _, vspec, _ = _fox_specs(tq, tk, 0)
    return _pcall_tables(body, name="fox_fwd",
                         out_shape=[jax.ShapeDtypeStruct((n_rows, FOX_WIDTH), BF16), jax.ShapeDtypeStruct((n_rows, FOX_WIDTH), F32)],
                         grid=(FOX_HEADS, tables[0].shape[0]), tables=tables,
                         in_specs=[qaspec, kaspec, vspec], out_specs=[qspec, qspec],
                         scratch=[pltpu.VMEM((tq, HEAD_DIM), F32)] * 3, sem=("parallel", "arbitrary"))(*tables, qa, ka, kv)


def fox_bwd_dq(qa, kv, ka, o, lse, do, prev=None):
    do, _, do_col = do
    do_col *= FOX_HEADS
    n_rows = kv.shape[0]
    tq, tk = _fox_tiles(n_rows)
    r = tq // tk
    n_prev = 0 if prev is None else 1
    tables = _fox_pairs(n_rows, True)

    def body(it, jt, qa_ref, ka_ref, k_ref, v_ref, o_ref, lse_ref, do_ref, *rest):
        dq_ref, drow_ref, acc, rows = rest[n_prev:]
        i, j = it[pl.program_id(1)], jt[pl.program_id(1)]

        @pl.when(j == 0)
        def _():
            acc[...] = jnp.zeros_like(acc)
            rows[...] = jnp.zeros_like(rows)

        def step(offset):
            _, ds, _ = _fox_p_ds(offset, qa_ref, ka_ref, v_ref, o_ref, lse_ref, do_ref)
            acc[...] += bdot(ds, k_ref[...])
            rows[...] += jnp.sum(ds, axis=-1, keepdims=True)

        _fox_on_diagonal(i, j, r, tk, step)

        @pl.when(j == r * i + r - 1)
        def _():
            dq_ref[...] = (acc[...] * (HEAD_DIM ** -0.5)).astype(dq_ref.dtype)
            drow_ref[...] = rows[...] + rest[0][...] if n_prev else rows[...]

    qaspec, qspec, dospec, kaspec, kspec, vspec, _ = _fox_specs(tq, tk, do_col)
    return _pcall_tables(body, name="fox_bwd_dq" + ("_acc" if n_prev else ""),
                         out_shape=[jax.ShapeDtypeStruct((n_rows, FOX_WIDTH), BF16), jax.ShapeDtypeStruct((n_rows, FOX_WIDTH), F32)],
                         grid=(FOX_HEADS, tables[0].shape[0]), tables=tables,
                         in_specs=[qaspec, kaspec, kspec, vspec, qspec, qspec, dospec] + [qspec] * n_prev,
                         out_specs=[qspec, qspec], scratch=[pltpu.VMEM((tq, HEAD_DIM), F32)] * 2,
                         sem=("parallel", "arbitrary"))(*tables, qa, ka, kv, kv, o, lse, do, *([prev] if n_prev else []))


def fox_bwd_dkv(qa, kv, ka, o, lse, do, prev=None):
    do, _, do_col = do
    do_col *= FOX_HEADS
    n_rows = kv.shape[0]
    tq, tk = _fox_tiles(n_rows)
    nq, r = n_rows // tq, tq // tk
    n_prev = 0 if prev is None else 3
    tables = _fox_pairs(n_rows, False)

    def body(it, jt, qa_ref, ka_ref, v_ref, o_ref, lse_ref, do_ref, *rest):
        prev_refs = rest[:n_prev]
        dk_ref, dv_ref, dc_ref, dk_acc, dv_acc, dc_acc = rest[n_prev:]
        i, j = it[pl.program_id(1)], jt[pl.program_id(1)]

        @pl.when(j >= r * i)
        def _():
            dk_acc[...] = jnp.zeros_like(dk_acc)
            dv_acc[...] = jnp.zeros_like(dv_acc)
            dc_acc[...] = jnp.zeros_like(dc_acc)

        def step(offset):
            p, ds, d_o = _fox_p_ds(offset, qa_ref, ka_ref, v_ref, o_ref, lse_ref, do_ref)
            dv_acc[...] += bdot.tn(p, d_o)
            dk_acc[...] += bdot.tn(ds, qa_ref[:, :HEAD_DIM])
            dc_acc[...] -= jnp.sum(ds, axis=0, keepdims=True)

        _fox_on_diagonal(i, j, r, tk, step)

        @pl.when(i == nq - 1)
        def _():
            dk, dv, dc = dk_acc[...], dv_acc[...], dc_acc[...]
            if n_prev:
                dk, dv, dc = dk + prev_refs[0][...], dv + prev_refs[1][...], dc + prev_refs[2][0]
            dk_ref[...] = dk
            dv_ref[...] = dv
            dc_ref[0] = dc

    qaspec, qspec, dospec, kaspec, kspec, vspec, cspec = _fox_specs(tq, tk, do_col)
    return _pcall_tables(body, name="fox_bwd_dkv" + ("_acc" if n_prev else ""),
                         out_shape=[jax.ShapeDtypeStruct((n_rows, FOX_WIDTH), F32), jax.ShapeDtypeStruct((n_rows, FOX_WIDTH), F32),
                                    jax.ShapeDtypeStruct((FOX_HEADS, 1, n_rows), F32)],
                         grid=(FOX_HEADS, tables[0].shape[0]), tables=tables,
                         in_specs=[qaspec, kaspec, vspec, qspec, qspec, dospec] + [kspec, kspec, cspec][:n_prev],
                         out_specs=[kspec, kspec, cspec],
                         scratch=[pltpu.VMEM((tk, HEAD_DIM), F32), pltpu.VMEM((tk, HEAD_DIM), F32), pltpu.VMEM((1, tk), F32)],
                         sem=("parallel", "arbitrary"))(*tables, qa, ka, kv, o, lse, do, *(prev or ()))


def loss_head(h, gain, target, *, tile=512):
    n_rows, d = h.shape
    tile = min(tile, n_rows)

    def body(h_ref, g_ref, t_ref, part_ref, dy_ref):
        (y,) = f_rmsnorm(h_ref[...], g_ref[...])
        diff = y - t_ref[...]
        dy_ref[...] = diff * (1.0 / d)
        part = jnp.sum(diff * diff, axis=0, keepdims=True)
        first = pl.program_id(0) == 0

        @pl.when(first)
        def _():
            part_ref[...] = part

        @pl.when(jnp.logical_not(first))
        def _():
            part_ref[...] += part

    blk = pl.BlockSpec((tile, d), lambda i: (i, 0))
    one = pl.BlockSpec((1, d), lambda i: (0, 0))
    return _pcall(body, name="loss_head",
                  out_shape=[jax.ShapeDtypeStruct((1, d), F32), jax.ShapeDtypeStruct((n_rows, d), F32)],
                  grid=(n_rows // tile,), in_specs=[blk, one, blk], out_specs=[one, blk], sem=("arbitrary",))(h, gain, target)


def adamw(w, g, m, v, *, name):
    shape = w.shape
    cols = shape[-1] if w.ndim >= 2 else w.size
    rows = w.size // cols
    tile = _pick(rows, (256, 128, 64, 32, 16, 8))
    as2d = lambda a: a.reshape(rows, cols)

    def body(w_ref, g_ref, m_ref, v_ref, d_ref, nm_ref, nv_ref):
        g_ = g_ref[...]
        m_ = ADAM_B1 * m_ref[...] + (1.0 - ADAM_B1) * g_
        v_ = ADAM_B2 * v_ref[...] + (1.0 - ADAM_B2) * (g_ * g_)
        m_hat = m_ / (1.0 - ADAM_B1 ** ADAM_STEP)
        v_hat = v_ / (1.0 - ADAM_B2 ** ADAM_STEP)
        d_ref[...] = -ADAM_LR * (m_hat / (jnp.sqrt(v_hat) + ADAM_EPS) + ADAM_WD * w_ref[...])
        nm_ref[...] = m_
        nv_ref[...] = v_

    blk = pl.BlockSpec((tile, cols), lambda i: (i, 0))
    outs = _pcall(body, name=name, out_shape=[jax.ShapeDtypeStruct((rows, cols), F32)] * 3, grid=(rows // tile,),
                  in_specs=[blk] * 4, out_specs=[blk] * 3, sem=("parallel",))(as2d(w), as2d(g), as2d(m), as2d(v))
    return tuple(o.reshape(shape) for o in outs)


def sum_leading(a, *, name):
    p, r, c = a.shape
    tile = _pick(r, (256, 128, 64, 32, 16, 8))

    def body(a_ref, o_ref):
        total = a_ref[0].astype(F32)
        for k in range(1, p):
            total = total + a_ref[k].astype(F32)
        o_ref[...] = total

    return _pcall(body, name=name, out_shape=jax.ShapeDtypeStruct((r, c), F32), grid=(r // tile,),
                  in_specs=[pl.BlockSpec((p, tile, c), lambda i: (0, i, 0))],
                  out_specs=pl.BlockSpec((tile, c), lambda i: (i, 0)), sem=("parallel",))(a)


_HBM = pl.BlockSpec(memory_space=pltpu.HBM)


def _comm_call(body, *, name, out_shape, n_in, scratch):
    return pl.pallas_call(body, name=name, out_shape=out_shape, in_specs=[_HBM] * n_in, out_specs=_HBM,
                          scratch_shapes=scratch,
                          compiler_params=pltpu.CompilerParams(has_side_effects=True))


def all_gather8(a, *, name):
    m_per, n = a.shape

    def body(x_ref, out_ref, send_sems, recv_sems, local_sem):
        x, y, c = lax.axis_index("x"), lax.axis_index("y"), lax.axis_index("c")
        me, sibling = (x, y, c), (x, y, 1 - c)
        chips = [(1 - x, y), (x, 1 - y), (1 - x, 1 - y)]

        def rows(px, py, pc):
            return out_ref.at[pl.ds((4 * px + 2 * py + pc) * m_per, m_per), :]

        def copy(k, block, to, src=None):
            return pltpu.make_async_remote_copy(
                src_ref=rows(*block) if src is None else src, dst_ref=rows(*block),
                send_sem=send_sems.at[k], recv_sem=recv_sems.at[k], device_id=to, device_id_type=MESH)

        mine = pltpu.make_async_copy(x_ref, rows(*me), local_sem)
        mine.start()
        first = [copy(0, me, sibling, src=x_ref)]
        first += [copy(1 + j, me, (*chip, c), src=x_ref) for j, chip in enumerate(chips)]
        for cp in first:
            cp.start()
        passed = [copy(4 + j, (*chip, c), sibling) for j, chip in enumerate(chips)]
        for j, chip in enumerate(chips):
            copy(1 + j, (*chip, c), me).wait_recv()
            passed[j].start()
        copy(0, sibling, me).wait_recv()
        for j, chip in enumerate(chips):
            copy(4 + j, (*chip, 1 - c), me).wait_recv()
        for cp in first + passed:
            cp.wait_send()
        mine.wait()

    return _comm_call(body, name=name, out_shape=jax.ShapeDtypeStruct((N_DEV * m_per, n), a.dtype), n_in=1,
                      scratch=[pltpu.SemaphoreType.DMA((7,)), pltpu.SemaphoreType.DMA((7,)), pltpu.SemaphoreType.DMA])(a)


PACK_COLS = 1024
PACK_ROW_MULTIPLE = 32

WEIGHT_NAMES = ["ffn1_norm", "ffn1_w_gate_up", "ffn1_w_down", "mix_norm", "ffn2_norm", "ffn2_w_gate_up", "ffn2_w_down",
                "gdn_w_in", "gdn_conv", "gdn_A_log", "gdn_dt_bias", "gdn_out_norm", "fox_w_in", "w_out", "mem_norm",
                "mem_w_kv", "kv_norm", "kv_w", "kv_b_f", "final_norm"]
SHARDED = [("ffn1_w_gate_up", 2), ("ffn1_w_down", 1), ("ffn2_w_gate_up", 2), ("ffn2_w_down", 1), ("gdn_w_in", 2),
           ("gdn_conv", 2), ("fox_w_in", 1), ("w_out", 1), ("mem_w_kv", 1), ("kv_w", 0)]
REPLICATED = [n for n in WEIGHT_NAMES if n not in dict(SHARDED)]


PACK_PIECE_ROWS = 16


def _rows_of(size):
    return -(-size // (PACK_COLS * PACK_PIECE_ROWS)) * PACK_PIECE_ROWS


def pack(pieces, dtype, row_multiple=PACK_ROW_MULTIPLE):
    bufs, total = [], 0
    for p in pieces:
        flat = p.astype(dtype).reshape(-1)
        rows = _rows_of(flat.size)
        bufs.append(jnp.pad(flat, (0, rows * PACK_COLS - flat.size)).reshape(rows, PACK_COLS))
        total += rows
    pad = -total % row_multiple
    if pad:
        bufs.append(jnp.zeros((pad, PACK_COLS), dtype))
    return jnp.concatenate(bufs, axis=0)


def unpack(buf, shapes):
    out, row = [], 0
    for shape in shapes:
        size = 1
        for s in shape:
            size *= s
        rows = _rows_of(size)
        out.append(buf[row:row + rows].reshape(-1)[:size].reshape(shape))
        row += rows
    return out


def _row(vec, width=None):
    vec = vec.astype(F32).reshape(1, -1)
    if width is not None and vec.shape[1] < width:
        vec = jnp.pad(vec, ((0, 0), (0, width - vec.shape[1])))
    return vec


ROW_TILE = 512
GDN_PROJ_WIDTH = 4 * GDN_WIDTH + MEM_WIDTH + LANES
GDN_Z_COL, GDN_QMEM_COL, GDN_AB_COL = 3, 4 * GDN_WIDTH // MEM_WIDTH, (4 * GDN_WIDTH + MEM_WIDTH) // LANES
FOX_QMEM_COL = FOX_WIDTH // MEM_WIDTH
KV_PAD_WIDTH = 2 * FOX_WIDTH + LANES


def rms_fwd(x, gain_row, out_dtype=BF16):
    return rowwise(f_rmsnorm, [x], [gain_row], [(x.shape[1], out_dtype)], tile=ROW_TILE, name="rms_fwd")[0]


def rms_bwd(x, gain_row, dy, dres=None):
    return rowwise_bwd(f_rmsnorm, [x], [gain_row], [dy], tile=ROW_TILE, name="rms_bwd", row_grads=[F32],
                       const_grads=[True], add=None if dres is None else {0: dres})


def _ffn_tiles(n_rows):
    return _pick(n_rows, (512, 256, 128)), _pick(FFN_HIDDEN, (1408, 256, 128))


def ffn_up_act(n, wgu):
    n_rows, d = n.shape
    tm, tn = _ffn_tiles(n_rows)
    nj = FFN_HIDDEN // tn

    def body(n_ref, wg_ref, wu_ref, gu_ref, act_ref):
        x = n_ref[...].astype(BF16)
        g = bdot.nn(x, wg_ref[...])
        u = bdot.nn(x, wu_ref[...])
        gu_ref[0] = g.astype(gu_ref.dtype)
        gu_ref[1] = u.astype(gu_ref.dtype)
        act_ref[...] = (_silu(g) * u).astype(act_ref.dtype)

    return _pcall(body, name="ffn_up_act",
                  out_shape=[jax.ShapeDtypeStruct((2, n_rows, FFN_HIDDEN), BF16), jax.ShapeDtypeStruct((n_rows, FFN_HIDDEN), BF16)],
                  grid=(nj, n_rows // tm),
                  in_specs=[pl.BlockSpec((tm, d), lambda j, i: (i, 0)), pl.BlockSpec((d, tn), lambda j, i: (0, j)),
                            pl.BlockSpec((d, tn), lambda j, i: (0, nj + j))],
                  out_specs=[pl.BlockSpec((2, tm, tn), lambda j, i: (0, i, j)), pl.BlockSpec((tm, tn), lambda j, i: (i, j))],
                  sem=("parallel", "parallel"))(n, wgu, wgu)


def ffn_down_dx_act(dh, wd, gu):
    n_rows, d = dh.shape
    tm, tn = _ffn_tiles(n_rows)

    def body(dh_ref, wd_ref, gu_ref, dgu_ref):
        dact = 0.5 * bdot.nt(dh_ref[...], wd_ref[...])
        gate, up = gu_ref[0].astype(F32), gu_ref[1].astype(F32)
        sg = _sigmoid(gate)
        dgu_ref[0] = (dact * up * (sg * (1.0 + gate * (1.0 - sg)))).astype(dgu_ref.dtype)
        dgu_ref[1] = (dact * (gate * sg)).astype(dgu_ref.dtype)

    blk = pl.BlockSpec((2, tm, tn), lambda j, i: (0, i, j))
    return _pcall(body, name="ffn_down_dx_act", out_shape=jax.ShapeDtypeStruct((2, n_rows, FFN_HIDDEN), BF16),
                  grid=(FFN_HIDDEN // tn, n_rows // tm),
                  in_specs=[pl.BlockSpec((tm, d), lambda j, i: (i, 0)), pl.BlockSpec((tn, d), lambda j, i: (j, 0)), blk],
                  out_specs=blk, sem=("parallel", "parallel"))(dh, wd, gu)


def ffn_fwd(h, gain_row, wgu, wd):
    n = rms_fwd(h, gain_row)
    gu, act = ffn_up_act(n, wgu)
    return mm(act, wd, scale=0.5, res=h, name="ffn_down"), (h, n, gu, act)


def ffn_bwd(dh, saved, gain_row, wgu, wd):
    h, n, gu, act = saved
    dgu = ffn_down_dx_act(dh, wd, gu)
    dwd = mm(act, dh, ta=True, scale=0.5, name="ffn_down_dw")
    dwgu = mm(n, dgu, ta=True, b_split=True, name="ffn_up_dw")
    dh, dgain = mm_rms_bwd(dgu, wgu, h, gain_row, dh, a_split=True, name="ffn_up_dx")
    return dh, dwgu, dwd, dgain


def gdn_fwd(proj, w8, a_row, dt_row, onorm_row):
    wide = [(GDN_WIDTH, F32)] * 5
    xc = conv_fwd(proj, w8, width=3 * GDN_WIDTH)
    q, k, v, gc, beta = rowwise(f_gdn_pre, [xc, (proj, LANES, GDN_AB_COL)], [a_row, dt_row], wide, tile=GDN_CHUNK,
                                name="gdn_pre_fwd")
    u, w, qk, kt, qh, inv = rowwise(f_gdn_intra, [q, k, v, gc, beta], [], wide + wide[:1], tile=GDN_CHUNK,
                                    name="gdn_intra_fwd")
    o, vn, sin = gdn_scan_fwd(u, w, qk, kt, qh, gc)
    main = rowwise(f_gdn_post, [o, (proj, GDN_WIDTH, GDN_Z_COL)], [onorm_row], [(GDN_WIDTH, BF16)], tile=ROW_TILE,
                   name="gdn_post_fwd")[0]
    return main, (xc, q, k, v, gc, beta, inv, w, qk, kt, qh, vn, sin, o)


def gdn_bwd(dmain, proj, saved, w8, a_row, dt_row, onorm_row):
    xc, q, k, v, gc, beta, inv, w, qk, kt, qh, vn, sin, o = saved
    do, dz, donorm = rowwise_bwd(f_gdn_post, [o, (proj, GDN_WIDTH, GDN_Z_COL)], [onorm_row], [dmain], tile=ROW_TILE,
                                 name="gdn_post_bwd", row_grads=[F32, BF16], const_grads=[True])
    du, dw, dqk, dkt, dqh, dgl = gdn_scan_bwd(do, w, qk, kt, qh, gc, vn, sin)
    dq, dk, dv, dgc, dbeta = gdn_intra_bwd(q, k, v, gc, beta, inv, (du, dw, dqk, dkt, dqh), dgl)
    dxc, dab, da, ddt = rowwise_bwd(f_gdn_pre, [xc, (proj, LANES, GDN_AB_COL)], [a_row, dt_row], [dq, dk, dv, dgc, dbeta],
                                    tile=GDN_CHUNK, name="gdn_pre_bwd", row_grads=[F32, BF16], const_grads=[True, True])
    dqkv, dw8 = conv_bwd(proj, w8, dxc, width=3 * GDN_WIDTH)
    return dqkv, dz, dab, dw8, da, ddt, donorm


def mem_fwd(q, kmem, vmem):
    return rowwise(f_mem_attn, [q], [kmem, vmem], [(MEM_WIDTH, BF16)], tile=ROW_TILE, name="mem_attn_fwd")[0]


def mem_bwd(q, kmem, vmem, dout):
    return rowwise_bwd(f_mem_attn, [q], [kmem, vmem], [dout], tile=ROW_TILE, name="mem_attn_bwd", row_grads=[BF16],
                       const_grads=[True, True])


def forward_backward(xs, mems, target, P):
    depth, n_a = 4, 2
    G = {}
    mem_gain = _row(P["mem_norm"])
    mem_n = rms_fwd(mems, mem_gain)
    h = xs
    saved = []
    shared = None
    for l in range(depth):
        h0 = h
        h1, s1 = ffn_fwd(h0, _row(P["ffn1_norm"][l]), P["ffn1_w_gate_up"][l], P["ffn1_w_down"][l])
        u = rms_fwd(h1, _row(P["mix_norm"][l]))
        kvm = mm(mem_n, P["mem_w_kv"][l], name="mem_kv")
        kmem, vmem = kvm[:, :MEM_WIDTH], kvm[:, MEM_WIDTH:]
        if l < n_a:
            gp = (P["conv8"][l], _row(P["gdn_A_log"][l], LANES), _row(P["gdn_dt_bias"][l], LANES), _row(P["gdn_out_norm"][l]))
            proj = mm(u, P["gdn_w_in_pad"][l], name="gdn_in")
            main, sm = gdn_fwd(proj, *gp)
            qm = (proj, MEM_WIDTH, GDN_QMEM_COL)
        else:
            proj = mm(u, P["fox_w_in"][l - n_a], out_dtype=BF16, name="fox_in")
            kv, ka, cb = shared
            qa = fox_augment(proj, cb, True)
            main, lse = fox_fwd(qa, kv, ka)
            sm = (main, lse, qa)
            qm = (proj, MEM_WIDTH, FOX_QMEM_COL)
        mo = mem_fwd(qm, kmem, vmem)
        cat = jnp.concatenate([main, mo], axis=1)
        h2 = mm(cat, P["w_out"][l], res=h1, name="mix_out")
        h3, s2 = ffn_fwd(h2, _row(P["ffn2_norm"][l]), P["ffn2_w_gate_up"][l], P["ffn2_w_down"][l])
        saved.append((s1, h1, u, kmem, vmem, proj, sm, qm, cat, s2))
        h = h3
        if l == n_a - 1:
            nkv = rms_fwd(h, _row(P["kv_norm"]))
            kv = mm(nkv, P["kv_w_pad"][:, :2 * FOX_WIDTH], out_dtype=BF16, name="fox_kv")
            f = mm(nkv, P["kv_w_pad"][:, 2 * FOX_WIDTH:], name="fox_f")
            bf_row = _row(P["kv_b_f"], LANES)
            cb = fox_gate_fwd(f, bf_row)
            shared = (kv, fox_augment(kv, cb, False), cb)
            kv_saved = (h, nkv, f, bf_row)

    part, dy = loss_head(h, _row(P["final_norm"]), target)
    dh, G["final_norm"] = rms_bwd(h, _row(P["final_norm"]), dy)

    per_layer = {n: [None] * depth for n in ("ffn1_norm", "ffn1_w_gate_up", "ffn1_w_down", "mix_norm", "ffn2_norm",
                                             "ffn2_w_gate_up", "ffn2_w_down", "w_out", "mem_w_kv")}
    gdn_g = {n: [None] * n_a for n in ("gdn_w_in_pad", "conv8", "gdn_A_log", "gdn_dt_bias", "gdn_out_norm")}
    fox_g = [None] * (depth - n_a)
    dmem_n = None
    dkv_acc = dcb_acc = None
    for l in reversed(range(depth)):
        s1, h1, u, kmem, vmem, proj, sm, qm, cat, s2 = saved[l]
        if l == n_a - 1:
            hk, nkv, f, bf_row = kv_saved
            dk, dv, dcrow = dkv_acc
            df, dbf = fox_gate_bwd(f, bf_row, dcrow, dcb_acc)
            dp = jnp.concatenate([dk.astype(BF16), dv.astype(BF16), df.astype(BF16)], axis=1)
            G["kv_w_pad"] = mm(nkv, dp, ta=True, name="fox_kv_dw")
            G["kv_b_f"] = dbf
            dh, G["kv_norm"] = mm_rms_bwd(dp, P["kv_w_pad"], hk, _row(P["kv_norm"]), dh, name="fox_kv_dx")
        dh, per_layer["ffn2_w_gate_up"][l], per_layer["ffn2_w_down"][l], per_layer["ffn2_norm"][l] = ffn_bwd(
            dh, s2, _row(P["ffn2_norm"][l]), P["ffn2_w_gate_up"][l], P["ffn2_w_down"][l])
        dcat = mm(dh, P["w_out"][l], tb=True, out_dtype=BF16, name="mix_out_dx")
        per_layer["w_out"][l] = mm(cat, dh, ta=True, name="mix_out_dw")
        dqm, dkm, dvm = mem_bwd(qm, kmem, vmem, (dcat, MEM_WIDTH, FOX_QMEM_COL))
        dkvm = jnp.concatenate([dkm, dvm], axis=1)
        per_layer["mem_w_kv"][l] = mm(mem_n, dkvm, ta=True, name="mem_kv_dw")
        dmem_n = mm(dkvm, P["mem_w_kv"][l], tb=True, res=dmem_n, name="mem_kv_dx")
        dmain = (dcat, GDN_WIDTH, 0)
        if l < n_a:
            gp = (P["conv8"][l], _row(P["gdn_A_log"][l], LANES), _row(P["gdn_dt_bias"][l], LANES), _row(P["gdn_out_norm"][l]))
            dqkv, dz, dab, gdn_g["conv8"][l], gdn_g["gdn_A_log"][l], gdn_g["gdn_dt_bias"][l], gdn_g["gdn_out_norm"][l] = gdn_bwd(
                dmain, proj, sm, *gp)
            dproj = jnp.concatenate([dqkv, dz, dqm, dab], axis=1)
            gdn_g["gdn_w_in_pad"][l] = mm(u, dproj, ta=True, name="gdn_in_dw")
            w_in = P["gdn_w_in_pad"][l]
        else:
            o, lse, qa = sm
            kv, ka, _ = shared
            dq, dcb_acc = fox_bwd_dq(qa, kv, ka, o, lse, dmain, dcb_acc)
            dkv_acc = fox_bwd_dkv(qa, kv, ka, o, lse, dmain, dkv_acc)
            dproj = jnp.concatenate([dq, dqm], axis=1)
            fox_g[l - n_a] = mm(u, dproj, ta=True, name="fox_in_dw")
            w_in = P["fox_w_in"][l - n_a]
        dh, per_layer["mix_norm"][l] = mm_rms_bwd(dproj, w_in, h1, _row(P["mix_norm"][l]), dh, name="mix_in_dx")
        dh, per_layer["ffn1_w_gate_up"][l], per_layer["ffn1_w_down"][l], per_layer["ffn1_norm"][l] = ffn_bwd(
            dh, s1, _row(P["ffn1_norm"][l]), P["ffn1_w_gate_up"][l], P["ffn1_w_down"][l])

    (G["mem_norm"],) = rowwise_bwd(f_rmsnorm, [mems], [mem_gain], [dmem_n], tile=ROW_TILE, name="mem_norm_bwd",
                                   row_grads=[None], const_grads=[True])
    for n, v in per_layer.items():
        G[n] = jnp.stack(v)
    for n, v in gdn_g.items():
        G[n] = jnp.stack(v)
    G["fox_w_in"] = jnp.stack(fox_g)
    return part, dh, G


_GDN_O0 = 4 * GDN_WIDTH
_GDN_O1 = _GDN_O0 + 2 * GDN_HEADS
_KV_WIDTH = 2 * FOX_WIDTH + FOX_HEADS


def derived_weights(gdn_w_in, gdn_conv, kv_w=None):
    zeros = jnp.zeros(gdn_w_in.shape[:-1] + (LANES - 2 * GDN_HEADS,), gdn_w_in.dtype)
    out = dict(
        gdn_w_in_pad=jnp.concatenate([gdn_w_in[..., :_GDN_O0], gdn_w_in[..., _GDN_O1:], gdn_w_in[..., _GDN_O0:_GDN_O1], zeros], axis=-1),
        conv8=jnp.pad(gdn_conv.astype(F32), ((0, 0), (0, 8 - CONV_WIDTH), (0, 0))))
    if kv_w is not None:
        out["kv_w_pad"] = jnp.pad(kv_w, ((0, 0), (0, KV_PAD_WIDTH - _KV_WIDTH)))
    return out


def reference_layout(G):
    gp = G["gdn_w_in_pad"]
    out = dict(G)
    out["gdn_w_in"] = jnp.concatenate([gp[..., :_GDN_O0], gp[..., _GDN_O0 + MEM_WIDTH:_GDN_O0 + MEM_WIDTH + 2 * GDN_HEADS],
                                       gp[..., _GDN_O0:_GDN_O0 + MEM_WIDTH]], axis=-1)
    out["gdn_conv"] = G["conv8"][:, :CONV_WIDTH]
    out["kv_w"] = G["kv_w_pad"][:, :_KV_WIDTH]
    out["gdn_A_log"] = G["gdn_A_log"][:, 0, :GDN_HEADS]
    out["gdn_dt_bias"] = G["gdn_dt_bias"][:, 0, :GDN_HEADS]
    out["gdn_out_norm"] = G["gdn_out_norm"][:, 0, :]
    out["kv_b_f"] = G["kv_b_f"][0, :FOX_HEADS]
    for n in ("ffn1_norm", "mix_norm", "ffn2_norm"):
        out[n] = G[n][:, 0, :]
    for n in ("mem_norm", "kv_norm", "final_norm"):
        out[n] = G[n][0]
    return {n: out[n] for n in WEIGHT_NAMES}


EXCHANGED = [("ffn1_w_gate_up", 2, 0), ("ffn1_w_down", 1, 0), ("ffn2_w_gate_up", 2, 0), ("ffn2_w_down", 1, 0),
             ("gdn_w_in", 2, 0), ("fox_w_in", 1, 0), ("w_out", 1, 0), ("mem_w_kv", 1, 0), ("kv_w", 0, 1)]
GDN_IN_SHARD = (4 * GDN_WIDTH + 2 * GDN_HEADS + MEM_WIDTH) // N_CHIPS
GDN_IN_SLOT = 896


def _slab(ref, axis_slices):
    idx = [slice(None)] * len(ref.shape)
    for axis, (start, size) in axis_slices.items():
        idx[axis] = pl.ds(start, size)
    return ref.at[tuple(idx)]


def _comm_multi(body, *, name, n_in, out_shapes, scratch):
    return pl.pallas_call(body, name=name, out_shape=out_shapes, in_specs=[_HBM] * n_in, out_specs=[_HBM] * len(out_shapes),
                          scratch_shapes=scratch, compiler_params=pltpu.CompilerParams(has_side_effects=True))


def gather_shards(shards, layout):
    n = len(shards)
    fulls = [tuple(d * (N_CHIPS if a == sa else 1) for a, d in enumerate(s.shape)) for s, (sa, _) in zip(shards, layout)]
    n_sem = 9

    def pieces(w):
        sa, ha = layout[w]
        axis = 3 - sa - ha
        size = shards[w].shape[axis]
        unit = LANES if axis == 2 else 16
        first = -(-(size // 2) // unit) * unit
        return axis, [(0, first), (first, size - first)]

    def body(*refs):
        ins, outs = refs[:n], refs[n:2 * n]
        send_sems, recv_sems, local_sems = refs[2 * n:]
        x, y, c = lax.axis_index("x"), lax.axis_index("y"), lax.axis_index("c")
        me, sibling, x_nbr, y_nbr = (x, y, c), (x, y, 1 - c), (1 - x, y, c), (x, 1 - y, c)
        chip_x, chip_y, chip_d = (1 - x, y), (x, 1 - y), (1 - x, 1 - y)

        def region(w, chip, pc, piece=None):
            (sa, ha), shard = layout[w], shards[w].shape
            where = {sa: ((2 * chip[0] + chip[1]) * shard[sa], shard[sa]), ha: (pc * (shard[ha] // 2), shard[ha] // 2)}
            if piece is not None:
                axis, parts = pieces(w)
                where[axis] = parts[piece]
            return _slab(outs[w], where)

        def my_half(w):
            ha, shard = layout[w][1], shards[w].shape
            return _slab(ins[w], {ha: (c * (shard[ha] // 2), shard[ha] // 2)})

        def copy(w, k, where, to, src=None):
            return pltpu.make_async_remote_copy(
                src_ref=where if src is None else src, dst_ref=where, send_sem=send_sems.at[n_sem * w + k],
                recv_sem=recv_sems.at[n_sem * w + k], device_id=to, device_id_type=MESH)

        mine, sends = [], [[] for _ in range(n)]
        for w in range(n):
            mine.append(pltpu.make_async_copy(my_half(w), region(w, (x, y), c), local_sems.at[w]))
            mine[w].start()
            sends[w] = [copy(w, k, region(w, (x, y), c), to, src=my_half(w)) for k, to in enumerate((sibling, x_nbr, y_nbr))]
            for cp in sends[w]:
                cp.start()
        for w in range(n):
            copy(w, 1, region(w, chip_x, c), me).wait_recv()
            onward = [copy(w, 3, region(w, chip_x, c, 0), y_nbr), copy(w, 5, region(w, chip_x, c), sibling)]
            for cp in onward:
                cp.start()
            sends[w] += onward
            copy(w, 2, region(w, chip_y, c), me).wait_recv()
            onward = [copy(w, 4, region(w, chip_y, c, 1), x_nbr), copy(w, 6, region(w, chip_y, c), sibling)]
            for cp in onward:
                cp.start()
            sends[w] += onward
        for w in range(n):
            copy(w, 3, region(w, chip_d, c, 0), me).wait_recv()
            copy(w, 4, region(w, chip_d, c, 1), me).wait_recv()
            onward = [copy(w, 7, region(w, chip_d, c, 0), sibling), copy(w, 8, region(w, chip_d, c, 1), sibling)]
            for cp in onward:
                cp.start()
            sends[w] += onward
        for w in range(n):
            copy(w, 0, region(w, (x, y), 1 - c), me).wait_recv()
            copy(w, 5, region(w, chip_x, 1 - c), me).wait_recv()
            copy(w, 6, region(w, chip_y, 1 - c), me).wait_recv()
            copy(w, 7, region(w, chip_d, 1 - c, 0), me).wait_recv()
            copy(w, 8, region(w, chip_d, 1 - c, 1), me).wait_recv()
        for w in range(n):
            for cp in sends[w]:
                cp.wait_send()
            mine[w].wait()

    return _comm_multi(body, name="gather_shards", n_in=n,
                       out_shapes=[jax.ShapeDtypeStruct(f, s.dtype) for f, s in zip(fulls, shards)],
                       scratch=[pltpu.SemaphoreType.DMA((n_sem * n,)), pltpu.SemaphoreType.DMA((n_sem * n,)),
                                pltpu.SemaphoreType.DMA((n,))])(*shards)


def swap_other_halves(arrays, layout):
    n = len(arrays)
    halves = [tuple(d // 2 if a == ha else d for a, d in enumerate(g.shape)) for g, (_, ha) in zip(arrays, layout)]

    def body(*refs):
        ins, outs = refs[:n], refs[n:2 * n]
        send_sems, recv_sems = refs[2 * n:]
        x, y, c = lax.axis_index("x"), lax.axis_index("y"), lax.axis_index("c")
        copies = []
        for w in range(n):
            ha, size = layout[w][1], halves[w][layout[w][1]]
            copies.append(pltpu.make_async_remote_copy(
                src_ref=_slab(ins[w], {ha: ((1 - c) * size, size)}), dst_ref=outs[w], send_sem=send_sems.at[w],
                recv_sem=recv_sems.at[w], device_id=(x, y, 1 - c), device_id_type=MESH))
            copies[w].start()
        for cp in copies:
            cp.wait()

    return _comm_multi(body, name="grad_pair_swap", n_in=n,
                       out_shapes=[jax.ShapeDtypeStruct(h, g.dtype) for h, g in zip(halves, arrays)],
                       scratch=[pltpu.SemaphoreType.DMA((n,)), pltpu.SemaphoreType.DMA((n,))])(*arrays)


def scatter_to_chips(arrays, layout):
    n = len(arrays)
    slabs = [tuple(d // N_CHIPS if a == sa else d for a, d in enumerate(p.shape)) for p, (sa, _) in zip(arrays, layout)]

    def body(*refs):
        ins, outs = refs[:n], refs[n:2 * n]
        send_sems, recv_sems, local_sems = refs[2 * n:]
        x, y, c = lax.axis_index("x"), lax.axis_index("y"), lax.axis_index("c")
        me = 2 * x + y

        def slab(w, k):
            sa, size = layout[w][0], slabs[w][layout[w][0]]
            return _slab(ins[w], {sa: (k * size, size)})

        local, copies = [], []
        for w in range(n):
            local.append(pltpu.make_async_copy(slab(w, me), outs[w].at[me], local_sems.at[w]))
            local[w].start()
            for j, (px, py) in enumerate([(1 - x, y), (x, 1 - y), (1 - x, 1 - y)]):
                copies.append(pltpu.make_async_remote_copy(
                    src_ref=slab(w, 2 * px + py), dst_ref=outs[w].at[me], send_sem=send_sems.at[3 * w + j],
                    recv_sem=recv_sems.at[3 * w + j], device_id=(px, py, c), device_id_type=MESH))
                copies[-1].start()
        for cp in copies:
            cp.wait()
        for cp in local:
            cp.wait()

    return _comm_multi(body, name="grad_all_to_all", n_in=n,
                       out_shapes=[jax.ShapeDtypeStruct((N_CHIPS,) + s, p.dtype) for s, p in zip(slabs, arrays)],
                       scratch=[pltpu.SemaphoreType.DMA((3 * n,)), pltpu.SemaphoreType.DMA((3 * n,)),
                                pltpu.SemaphoreType.DMA((n,))])(*arrays)


def swap_with_sibling(arrays):
    n = len(arrays)

    def body(*refs):
        ins, outs = refs[:n], refs[n:2 * n]
        send_sems, recv_sems = refs[2 * n:]
        x, y, c = lax.axis_index("x"), lax.axis_index("y"), lax.axis_index("c")
        copies = [pltpu.make_async_remote_copy(src_ref=ins[w], dst_ref=outs[w], send_sem=send_sems.at[w], recv_sem=recv_sems.at[w],
                                               device_id=(x, y, 1 - c), device_id_type=MESH) for w in range(n)]
        for cp in copies:
            cp.start()
        for cp in copies:
            cp.wait()

    return _comm_multi(body, name="grad_half_swap", n_in=n, out_shapes=[jax.ShapeDtypeStruct(a.shape, a.dtype) for a in arrays],
                       scratch=[pltpu.SemaphoreType.DMA((n,)), pltpu.SemaphoreType.DMA((n,))])(*arrays)


def join_halves(mine, other, half_axis, c_arr, *, name):
    a0, a1, a2 = mine.shape
    tile = _row_tile(a1, a2)

    def body(c_ref, m_ref, o_ref, out_ref):
        for half in range(2):
            @pl.when(c_ref[0] == half)
            def _(half=half):
                out_ref[half] = m_ref[...]
                out_ref[1 - half] = o_ref[...]

    blk = pl.BlockSpec((None, tile, a2), lambda i, j, c_ref: (i, j, 0))
    if half_axis == 0:
        out_shape, out_blk = (2, a0, a1, a2), pl.BlockSpec((2, None, tile, a2), lambda i, j, c_ref: (0, i, j, 0))
    else:
        out_shape, out_blk = (a0, 2, a1, a2), pl.BlockSpec((None, 2, tile, a2), lambda i, j, c_ref: (i, 0, j, 0))
    spec = pltpu.PrefetchScalarGridSpec(num_scalar_prefetch=1, grid=(a0, a1 // tile), in_specs=[blk, blk], out_specs=out_blk)
    out = pl.pallas_call(body, name=name, out_shape=jax.ShapeDtypeStruct(out_shape, mine.dtype), grid_spec=spec,
                         compiler_params=pltpu.CompilerParams(vmem_limit_bytes=VMEM_LIMIT_BYTES,
                                                              dimension_semantics=("parallel", "parallel")))(c_arr, mine, other)
    return out.reshape((2 * a0, a1, a2) if half_axis == 0 else (a0, 2 * a1, a2))


def _row_tile(rows, cols, itemsize=4, budget=2 * 1024 * 1024):
    for t in (1024, 512, 256, 128, 64, 32, 16):
        if rows % t == 0 and t * cols * itemsize <= budget:
            return t
    return rows


def add_own_half(full, recv, half_axis, c_arr, *, name):
    a0, a1, a2 = recv.shape
    tile = _row_tile(a1, a2)

    def body(c_ref, f_ref, r_ref, o_ref):
        o_ref[...] = (f_ref[...] + r_ref[...]).astype(o_ref.dtype)

    if half_axis == 0:
        f_spec = pl.BlockSpec((None, tile, a2), lambda i, j, c_ref: (c_ref[0] * a0 + i, j, 0))
    else:
        f_spec = pl.BlockSpec((None, tile, a2), lambda i, j, c_ref: (i, c_ref[0] * (a1 // tile) + j, 0))
    blk = pl.BlockSpec((None, tile, a2), lambda i, j, c_ref: (i, j, 0))
    spec = pltpu.PrefetchScalarGridSpec(num_scalar_prefetch=1, grid=(a0, a1 // tile), in_specs=[f_spec, blk], out_specs=blk)
    return pl.pallas_call(body, name=name, out_shape=jax.ShapeDtypeStruct(recv.shape, BF16), grid_spec=spec,
                          compiler_params=pltpu.CompilerParams(vmem_limit_bytes=VMEM_LIMIT_BYTES,
                                                               dimension_semantics=("parallel", "parallel")))(c_arr, full, recv)


def sum_slots(q, *, name):
    _, a0, a1, a2 = q.shape
    tile = _row_tile(a1, a2, budget=1024 * 1024)

    def body(q_ref, o_ref):
        total = q_ref[0].astype(F32)
        for k in range(1, N_CHIPS):
            total = total + q_ref[k].astype(F32)
        o_ref[...] = total

    return _pcall(body, name=name, out_shape=jax.ShapeDtypeStruct((a0, a1, a2), F32), grid=(a0, a1 // tile),
                  in_specs=[pl.BlockSpec((N_CHIPS, None, tile, a2), lambda i, j: (0, i, j, 0))],
                  out_specs=pl.BlockSpec((None, tile, a2), lambda i, j: (i, j, 0)), sem=("parallel", "parallel"))(q)


def gather_weights(W):
    shards = []
    for name, _, _ in EXCHANGED:
        w = W[name].astype(BF16)
        if name == "gdn_w_in":
            w = jnp.pad(w, ((0, 0), (0, 0), (0, GDN_IN_SLOT - GDN_IN_SHARD)))
        if name == "kv_w":
            w = jnp.pad(w, ((0, 0), (0, KV_PAD_WIDTH - _KV_WIDTH)))[None]
        shards.append(w)
    fulls = dict(zip([n for n, _, _ in EXCHANGED], gather_shards(shards, [(sa, ha) for _, sa, ha in EXCHANGED])))
    slots = fulls["gdn_w_in"]
    fulls["gdn_w_in"] = jnp.concatenate([slots[..., k * GDN_IN_SLOT:k * GDN_IN_SLOT + GDN_IN_SHARD] for k in range(N_CHIPS)], axis=-1)
    fulls["kv_w_pad"] = fulls.pop("kv_w").reshape(-1, KV_PAD_WIDTH)
    conv = pack([W["gdn_conv"]], F32, row_multiple=8)
    conv_all = all_gather8(conv, name="gather_conv").reshape(N_DEV, conv.shape[0], PACK_COLS)
    fulls["gdn_conv"] = jnp.concatenate([unpack(conv_all[2 * k], [W["gdn_conv"].shape])[0] for k in range(N_CHIPS)], axis=-1)
    return fulls


def reduce_gradients(G):
    layout = [(sa, ha) for _, sa, ha in EXCHANGED]
    c_arr = lax.axis_index("c").astype(jnp.int32).reshape(1)
    received = swap_other_halves(G, layout)
    pairs = [add_own_half(g, r, ha, c_arr, name="grad_pair_sum") for g, r, (_, ha) in zip(G, received, layout)]
    slots = scatter_to_chips(pairs, layout)
    halves = [sum_slots(q, name="grad_chip_sum") for q in slots]
    others = swap_with_sibling(halves)
    return [join_halves(h, o, ha, c_arr, name="grad_join_halves") for h, o, (_, ha) in zip(halves, others, layout)]


def allreduce_small(G, names):
    packed = pack([G[n] for n in names], F32, row_multiple=8)
    gathered = all_gather8(packed, name="gather_small_grads").reshape(N_DEV, packed.shape[0], PACK_COLS)
    total = sum_leading(gathered, name="sum_small_grads")
    return dict(zip(names, unpack(total, [G[n].shape for n in names])))


def kernel(x, mem, *rest):
    n_w = len(WEIGHT_NAMES)
    W = dict(zip(WEIGHT_NAMES, rest[:n_w]))
    target = rest[n_w]
    M = dict(zip(WEIGHT_NAMES, rest[n_w + 1:2 * n_w + 1]))
    V = dict(zip(WEIGHT_NAMES, rest[2 * n_w + 1:3 * n_w + 1]))

    full = gather_weights(W)
    P = {n: W[n] for n in REPLICATED}
    P.update({n: full[n] for n in ("ffn1_w_gate_up", "ffn1_w_down", "ffn2_w_gate_up", "ffn2_w_down", "fox_w_in", "w_out",
                                   "mem_w_kv", "kv_w_pad")})
    derived = derived_weights(full["gdn_w_in"], full["gdn_conv"])
    P.update(gdn_w_in_pad=derived["gdn_w_in_pad"], conv8=derived["conv8"])

    part, dx, G = forward_backward(x[0], mem[0], target[0], P)
    loss = lax.psum(0.5 / x.shape[-1] * jnp.sum(part), ("x", "y", "c"))

    ref = reference_layout(G)
    exchange = {n: ref[n] for n, _, _ in EXCHANGED}
    exchange["gdn_w_in"] = jnp.concatenate(
        [jnp.pad(ref["gdn_w_in"][..., k * GDN_IN_SHARD:(k + 1) * GDN_IN_SHARD], ((0, 0), (0, 0), (0, GDN_IN_SLOT - GDN_IN_SHARD)))
         for k in range(N_CHIPS)], axis=-1)
    exchange["kv_w"] = G["kv_w_pad"].reshape(N_CHIPS, -1, KV_PAD_WIDTH)
    shards = dict(zip([n for n, _, _ in EXCHANGED], reduce_gradients([exchange[n] for n, _, _ in EXCHANGED])))
    shards["gdn_w_in"] = shards["gdn_w_in"][..., :GDN_IN_SHARD]
    shards["kv_w"] = shards["kv_w"][0, :, :_KV_WIDTH]
    grads = allreduce_small(ref, REPLICATED + ["gdn_conv"])
    conv_cols = W["gdn_conv"].shape[-1]
    chip = 2 * lax.axis_index("x") + lax.axis_index("y")
    grads["gdn_conv"] = lax.dynamic_slice_in_dim(grads["gdn_conv"], chip * conv_cols, conv_cols, axis=2)
    grads.update(shards)

    outs = {n: adamw(W[n], grads[n], M[n], V[n], name="adamw_" + n) for n in WEIGHT_NAMES}
    return (loss, dx[None], *[grads[n] for n in WEIGHT_NAMES], *[outs[n][0] for n in WEIGHT_NAMES],
            *[outs[n][1] for n in WEIGHT_NAMES], *[outs[n][2] for n in WEIGHT_NAMES])
```

```python
import jax
import jax.numpy as jnp
from jax import lax
from jax.experimental import pallas as pl
from jax.experimental.pallas import tpu as pltpu

F32, BF16 = jnp.float32, jnp.bfloat16
HI = lax.Precision.HIGHEST
MESH = pl.DeviceIdType.MESH

VMEM_LIMIT_BYTES = 48 * 1024 * 1024
LANES = 128
EPS = 1e-6
NEG_INF = -1e30

D_MODEL = 1024
HEAD_DIM = 128
GDN_HEADS = 6
GDN_WIDTH = GDN_HEADS * HEAD_DIM
FOX_HEADS = 6
FOX_WIDTH = FOX_HEADS * HEAD_DIM
MEM_HEADS = 4
MEM_HEAD_DIM = 64
MEM_WIDTH = MEM_HEADS * MEM_HEAD_DIM
FFN_HIDDEN = 2816
CONV_WIDTH = 4
GDN_CHUNK = 128
N_CHIPS = 4
N_DEV = 8

ADAM_LR, ADAM_B1, ADAM_B2, ADAM_EPS, ADAM_WD, ADAM_STEP = 0.001, 0.9, 0.999, 1e-08, 0.01, 10


def _pcall(body, *, name, out_shape, grid=(), in_specs=None, out_specs=None, scratch=(), sem=None):
    params = dict(vmem_limit_bytes=VMEM_LIMIT_BYTES)
    if sem is not None:
        params["dimension_semantics"] = sem
    kw = dict(grid=grid, in_specs=in_specs, out_specs=out_specs) if grid else {}
    return pl.pallas_call(body, name=name, out_shape=out_shape, scratch_shapes=list(scratch),
                          compiler_params=pltpu.CompilerParams(**params), **kw)


def _pick(n, cands):
    for c in cands:
        if n % c == 0:
            return c
    return n


def _make_dot(dtype, precision):
    def raw(a, b, dims):
        return lax.dot_general(a.astype(dtype), b.astype(dtype), (dims, ((), ())),
                               precision=precision, preferred_element_type=F32)

    @jax.custom_vjp
    def dot(a, b):
        return raw(a, b, ((1,), (0,)))

    def fwd(a, b):
        return dot(a, b), (a, b)

    def bwd(resid, ct):
        a, b = resid
        return raw(ct, b, ((1,), (1,))).astype(a.dtype), raw(a, ct, ((0,), (0,))).astype(b.dtype)

    dot.defvjp(fwd, bwd)
    dot.nn = lambda a, b: raw(a, b, ((1,), (0,)))
    dot.nt = lambda a, b: raw(a, b, ((1,), (1,)))
    dot.tn = lambda a, b: raw(a, b, ((0,), (0,)))
    return dot


bdot = _make_dot(BF16, None)
fdot = _make_dot(F32, HI)
idot = _make_dot(F32, lax.Precision.HIGH)
sdot = idot


def _sigmoid(x):
    return 0.5 * jnp.tanh(0.5 * x) + 0.5


def _silu(x):
    return x * _sigmoid(x)


def _softplus(x):
    return jnp.maximum(x, 0.0) + jnp.log(1.0 + jnp.exp(-jnp.abs(x)))


def _log_sigmoid(x):
    return -_softplus(-x)


def _iota2(shape, dim):
    return lax.broadcasted_iota(jnp.int32, shape, dim)


def mm(a, b, *, ta=False, tb=False, a_split=False, b_split=False, out_dtype=F32, scale=1.0, res=None, name):
    assert not (a_split and ta) and not (b_split and tb)
    (K, M) = a.shape if ta else ((2 * a.shape[2], a.shape[1]) if a_split else a.shape[::-1])
    (N, Kb) = b.shape if tb else ((2 * b.shape[2], b.shape[1]) if b_split else b.shape[::-1])
    assert K == Kb, (a.shape, b.shape, ta, tb)
    tm = _pick(M, (1024, 1408, 512, 256, 128))
    tn = _pick(N, (1024, 1408, 1152, 1664, 768, 512, 384, 256, 128))
    deep = a.dtype == BF16 and b.dtype == BF16
    if ta:
        tk = _pick(K, (2048, 1024, 512, 256, 128) if deep else (1024, 512, 256, 128))
    else:
        tk = _pick(K, (1024, 2816, 1408, 1152, 1664, 512, 256, 128) if deep else (1024, 1408, 1152, 1664, 512, 256, 128))
    assert not a_split or (K // 2) % tk == 0
    assert not b_split or (N // 2) % tn == 0
    nk = K // tk
    dims = (((0 if ta else 1,), (1 if tb else 0,)), ((), ()))

    def body(a_ref, b_ref, *rest):
        o_ref, acc = rest[-2], rest[-1]
        k = pl.program_id(2)

        @pl.when(k == 0)
        def _():
            acc[...] = jnp.zeros_like(acc)

        acc[...] += lax.dot_general(a_ref[...].astype(BF16), b_ref[...].astype(BF16), dims,
                                    preferred_element_type=F32)

        @pl.when(k == nk - 1)
        def _():
            out = acc[...] * scale
            if res is not None:
                out = out + rest[0][...].astype(F32)
            o_ref[...] = out.astype(o_ref.dtype)

    a_spec = pl.BlockSpec((tk, tm), lambda i, j, k: (k, i)) if ta else pl.BlockSpec((tm, tk), lambda i, j, k: (i, k))
    b_spec = pl.BlockSpec((tn, tk), lambda i, j, k: (j, k)) if tb else pl.BlockSpec((tk, tn), lambda i, j, k: (k, j))
    if a_split:
        per_half = K // 2 // tk
        a_spec = pl.BlockSpec((None, tm, tk), lambda i, j, k: (k // per_half, i, k % per_half))
    if b_split:
        per_half = N // 2 // tn
        b_spec = pl.BlockSpec((None, tk, tn), lambda i, j, k: (j // per_half, k, j % per_half))
    o_spec = pl.BlockSpec((tm, tn), lambda i, j, k: (i, j))
    ins, specs = [a, b], [a_spec, b_spec]
    if res is not None:
        ins.append(res)
        specs.append(o_spec)
    return _pcall(body, name=name, out_shape=jax.ShapeDtypeStruct((M, N), out_dtype),
                  grid=(M // tm, N // tn, nk), in_specs=specs, out_specs=o_spec,
                  scratch=[pltpu.VMEM((tm, tn), F32)], sem=("parallel", "parallel", "arbitrary"))(*ins)


def mm_rms_bwd(a, b, x, gain_row, dres, *, a_split=False, name):
    (K, M) = (2 * a.shape[2], a.shape[1]) if a_split else a.shape[::-1]
    D = b.shape[0]
    assert b.shape[1] == K and x.shape == (M, D)
    tm = _pick(M, (1024, 512, 256, 128))
    rows = _pick(tm, (256, 128))
    tk = _pick(K, (1024, 1408, 1152, 1664, 512, 256, 128))
    assert not a_split or (K // 2) % tk == 0
    nk = K // tk

    def body(a_ref, b_ref, x_ref, g_ref, r_ref, o_ref, dg_ref, acc):
        i, k = pl.program_id(0), pl.program_id(1)

        @pl.when(k == 0)
        def _():
            acc[...] = jnp.zeros_like(acc)

        acc[...] += bdot.nt(a_ref[...], b_ref[...])

        @pl.when(k == nk - 1)
        def _():
            dg = jnp.zeros((1, D), F32)
            for s in range(tm // rows):
                sl = slice(s * rows, (s + 1) * rows)
                xv = x_ref[sl, :]
                rstd = lax.rsqrt(jnp.mean(xv * xv, axis=-1, keepdims=True) + EPS)
                xh = xv * rstd
                dn = acc[sl, :]
                dy = dn * g_ref[...]
                o_ref[sl, :] = (dy - xh * jnp.mean(dy * xh, axis=-1, keepdims=True)) * rstd + r_ref[sl, :]
                dg = dg + jnp.sum(dn * xh, axis=0, keepdims=True)

            @pl.when(i == 0)
            def _():
                dg_ref[...] = dg

            @pl.when(i > 0)
            def _():
                dg_ref[...] += dg

    a_spec = pl.BlockSpec((tm, tk), lambda i, k: (i, k))
    if a_split:
        per_half = K // 2 // tk
        a_spec = pl.BlockSpec((None, tm, tk), lambda i, k: (k // per_half, i, k % per_half))
    row = pl.BlockSpec((tm, D), lambda i, k: (i, 0))
    one = pl.BlockSpec((1, D), lambda i, k: (0, 0))
    return _pcall(body, name=name, out_shape=[jax.ShapeDtypeStruct((M, D), F32), jax.ShapeDtypeStruct((1, D), F32)],
                  grid=(M // tm, nk), in_specs=[a_spec, pl.BlockSpec((D, tk), lambda i, k: (0, k)), row, one, row],
                  out_specs=[row, one], scratch=[pltpu.VMEM((tm, D), F32)], sem=("arbitrary", "arbitrary"))(a, b, x, gain_row, dres)


def _row_spec(r, tile):
    if isinstance(r, tuple):
        arr, width, col = r
        return arr, pl.BlockSpec((tile, width), lambda i, col=col: (i, col))
    return r, pl.BlockSpec((tile, r.shape[1]), lambda i: (i, 0))


def _const_spec(c):
    return pl.BlockSpec(c.shape, lambda i: (0,) * c.ndim)


def rowwise(fn, rows, consts, outs, *, tile, name):
    arrs, specs = zip(*[_row_spec(r, tile) for r in rows])
    n_rows = arrs[0].shape[0]
    tile = min(tile, n_rows)
    n_in = len(rows) + len(consts)

    def body(*refs):
        res = fn(*[r[...] for r in refs[:n_in]])
        for o_ref, o in zip(refs[n_in:], res):
            o_ref[...] = o.astype(o_ref.dtype)

    arrs, specs = zip(*[_row_spec(r, tile) for r in rows])
    return _pcall(body, name=name,
                  out_shape=[jax.ShapeDtypeStruct((n_rows, w), dt) for w, dt in outs],
                  grid=(n_rows // tile,),
                  in_specs=list(specs) + [_const_spec(c) for c in consts],
                  out_specs=[pl.BlockSpec((tile, w), lambda i: (i, 0)) for w, _ in outs],
                  sem=("parallel",))(*arrs, *consts)


def rowwise_bwd(fn, rows, consts, cts, *, tile, name, row_grads, const_grads, add=None):
    arrs, _ = zip(*[_row_spec(r, tile) for r in rows])
    n_rows = arrs[0].shape[0]
    tile = min(tile, n_rows)
    arrs, specs = zip(*[_row_spec(r, tile) for r in rows])
    ct_arrs, ct_specs = zip(*[_row_spec(c, tile) for c in cts])
    add = add or {}
    add_idx = sorted(add)
    add_arrs, add_specs = (zip(*[_row_spec(add[i], tile) for i in add_idx]) if add_idx else ((), ()))
    nr, nc, nct, na = len(rows), len(consts), len(cts), len(add_idx)
    want_rows = [i for i, d in enumerate(row_grads) if d is not None]
    want_consts = [i for i, w in enumerate(const_grads) if w]

    def body(*refs):
        row_v = [r[...] for r in refs[:nr]]
        const_v = [r[...] for r in refs[nr:nr + nc]]
        ct_v = [r[...] for r in refs[nr + nc:nr + nc + nct]]
        add_v = {i: refs[nr + nc + nct + j][...] for j, i in enumerate(add_idx)}
        out_refs = refs[nr + nc + nct + na:]
        res, vjp = jax.vjp(fn, *row_v, *const_v)
        grads = vjp(tuple(c.astype(o.dtype) for c, o in zip(ct_v, res)))
        for o_ref, i in zip(out_refs, want_rows):
            g = grads[i].astype(F32)
            if i in add_v:
                g = g + add_v[i].astype(F32)
            o_ref[...] = g.astype(o_ref.dtype)
        first = pl.program_id(0) == 0
        for o_ref, i in zip(out_refs[len(want_rows):], want_consts):
            g = grads[nr + i].astype(F32)

            @pl.when(first)
            def _(o_ref=o_ref, g=g):
                o_ref[...] = g

            @pl.when(jnp.logical_not(first))
            def _(o_ref=o_ref, g=g):
                o_ref[...] += g

    def width(r):
        return r[1] if isinstance(r, tuple) else r.shape[1]

    out_shape = [jax.ShapeDtypeStruct((n_rows, width(rows[i])), row_grads[i]) for i in want_rows]
    out_shape += [jax.ShapeDtypeStruct(consts[i].shape, F32) for i in want_consts]
    out_specs = [pl.BlockSpec((tile, width(rows[i])), lambda i_: (i_, 0)) for i in want_rows]
    out_specs += [_const_spec(consts[i]) for i in want_consts]
    return _pcall(body, name=name, out_shape=out_shape, grid=(n_rows // tile,),
                  in_specs=list(specs) + [_const_spec(c) for c in consts] + list(ct_specs) + list(add_specs),
                  out_specs=out_specs, sem=("arbitrary",))(*arrs, *consts, *ct_arrs, *add_arrs)


def f_rmsnorm(x, g):
    x = x.astype(F32)
    return (x * lax.rsqrt(jnp.mean(x * x, axis=-1, keepdims=True) + EPS) * g,)


def _head_sel(first_lane):
    r, c = _iota2((LANES, GDN_WIDTH), 0), _iota2((LANES, GDN_WIDTH), 1)
    return (r == c // HEAD_DIM + first_lane).astype(F32)


def _tri(n, strict=False):
    r, c = _iota2((n, n), 0), _iota2((n, n), 1)
    return r > c if strict else r >= c


def f_gdn_pre(xc, ab, a_log, dt_bias):
    s = _silu(xc.astype(F32))
    qs, ks = [], []
    for h in range(GDN_HEADS):
        qh = s[:, h * HEAD_DIM:(h + 1) * HEAD_DIM]
        kh = s[:, GDN_WIDTH + h * HEAD_DIM:GDN_WIDTH + (h + 1) * HEAD_DIM]
        qs.append(qh * lax.rsqrt(jnp.sum(qh * qh, axis=-1, keepdims=True) + EPS) * (HEAD_DIM ** -0.5))
        ks.append(kh * lax.rsqrt(jnp.sum(kh * kh, axis=-1, keepdims=True) + EPS))
    q, k = jnp.concatenate(qs, axis=1), jnp.concatenate(ks, axis=1)
    v = s[:, 2 * GDN_WIDTH:]
    ab = ab.astype(F32)
    g = -jnp.exp(a_log) * _softplus(ab + dt_bias)
    gc = _head_broadcast(fdot(_tri(GDN_CHUNK).astype(F32), g), 0)
    beta = _head_broadcast(_sigmoid(ab), GDN_HEADS)
    return q, k, v, gc, beta


def _head_broadcast(x, first_lane):
    lane = _iota2(x.shape, 1)
    cols = [jnp.sum(jnp.where(lane == first_lane + h, x, 0.0), axis=1, keepdims=True) for h in range(GDN_HEADS)]
    return jnp.concatenate([jnp.broadcast_to(c, x.shape) for c in cols], axis=1)


def _unit_lower_inverses(neg_lowers):
    C = neg_lowers[0].shape[0]
    eye = (_iota2((C, C), 0) == _iota2((C, C), 1)).astype(F32)
    invs = [eye + n for n in neg_lowers]
    powers = list(neg_lowers)
    for _ in range(6):
        powers = [idot.nn(p, p) for p in powers]
        invs = [inv + idot.nn(p, inv) for p, inv in zip(powers, invs)]
    return invs


@jax.custom_vjp
def _solve_with_inverse(inv, neg_lower, rhs):
    return idot.nn(inv, rhs)


def _solve_fwd(inv, neg_lower, rhs):
    x = idot.nn(inv, rhs)
    return x, (inv, x)


def _solve_bwd(resid, ct):
    inv, x = resid
    d_rhs = idot.tn(inv, ct)
    return jnp.zeros_like(inv), idot.nt(d_rhs, x), d_rhs


_solve_with_inverse.defvjp(_solve_fwd, _solve_bwd)


def f_gdn_intra(q, k, v, gc, beta, inv=None):
    C = GDN_CHUNK
    causal, strict = _tri(C), _tri(C, strict=True)
    is_last = _iota2((C, HEAD_DIM), 0) == C - 1
    heads = range(GDN_HEADS)
    sls = [slice(h * HEAD_DIM, (h + 1) * HEAD_DIM) for h in heads]
    qs, ks, vs, gs, bs = ([a[:, sl] for sl in sls] for a in (q, k, v, gc, beta))
    decays = [jnp.where(causal, jnp.exp(jnp.where(causal, g - g.T, 0.0)), 0.0) for g in gs]
    kbs = [kh * bh for kh, bh in zip(ks, bs)]
    kts = [kh.T for kh in ks]
    neg_lowers = [jnp.where(strict, -(idot(kb, kt) * d), 0.0) for kb, kt, d in zip(kbs, kts, decays)]
    qks = [jnp.where(causal, idot(qh, kt) * d, 0.0) for qh, kt, d in zip(qs, kts, decays)]
    rhss = [jnp.concatenate([vh * bh, kb * jnp.exp(g)], axis=1) for vh, bh, kb, g in zip(vs, bs, kbs, gs)]
    if inv is None:
        invs = _unit_lower_inverses(neg_lowers)
        sols = [idot.nn(m, r) for m, r in zip(invs, rhss)]
    else:
        sols = [_solve_with_inverse(inv[:, sl], n, r) for sl, n, r in zip(sls, neg_lowers, rhss)]
    g_lasts = [jnp.sum(jnp.where(is_last, g, 0.0), axis=0, keepdims=True) for g in gs]
    outs = [[s[:, :HEAD_DIM] for s in sols], [s[:, HEAD_DIM:] for s in sols], qks,
            [kh * jnp.exp(gl - g) for kh, gl, g in zip(ks, g_lasts, gs)], [qh * jnp.exp(g) for qh, g in zip(qs, gs)]]
    if inv is None:
        outs.append(invs)
    return tuple(jnp.concatenate(o, axis=1) for o in outs)


def f_gdn_post(o, z, gain):
    z = z.astype(F32)
    parts = []
    for h in range(GDN_HEADS):
        oh = o[:, h * HEAD_DIM:(h + 1) * HEAD_DIM]
        parts.append(oh * lax.rsqrt(jnp.mean(oh * oh, axis=-1, keepdims=True) + EPS) * gain)
    return (jnp.concatenate(parts, axis=1) * _silu(z),)


def f_mem_attn(q, k, v):
    q = q.astype(F32)
    lane_head = _iota2((1, MEM_WIDTH), 1) // MEM_HEAD_DIM
    kt = k.astype(F32).T
    masks = [(lane_head == h).astype(F32) for h in range(MEM_HEADS)]
    logits = [bdot(q * mask, kt) * (MEM_HEAD_DIM ** -0.5) for mask in masks]
    ps = [jnp.exp(s - jnp.max(s, axis=-1, keepdims=True)) for s in logits]
    ps = [p / jnp.sum(p, axis=-1, keepdims=True) for p in ps]
    outs = [bdot(p, v) * mask for p, mask in zip(ps, masks)]
    return ((outs[0] + outs[1]) + (outs[2] + outs[3]),)


def f_loss(y, t):
    d = y - t
    return (d * d,)


def conv_fwd(proj, w8, *, width, tile=512):
    n_rows = proj.shape[0]
    tile = min(tile, n_rows)

    def body(x_ref, halo_ref, w_ref, o_ref):
        i = pl.program_id(0)
        halo = jnp.where(i > 0, halo_ref[...].astype(F32), 0.0)
        xs = jnp.concatenate([halo, x_ref[...].astype(F32)], axis=0)
        acc = xs[8:] * w_ref[3:4, :]
        for j in range(CONV_WIDTH - 1):
            acc = acc + pltpu.roll(xs, CONV_WIDTH - 1 - j, 0)[8:] * w_ref[j:j + 1, :]
        o_ref[...] = acc

    return _pcall(body, name="gdn_conv_fwd", out_shape=jax.ShapeDtypeStruct((n_rows, width), F32),
                  grid=(n_rows // tile,),
                  in_specs=[pl.BlockSpec((tile, width), lambda i: (i, 0)),
                            pl.BlockSpec((8, width), lambda i: (jnp.maximum(i * (tile // 8) - 1, 0), 0)),
                            pl.BlockSpec((8, width), lambda i: (0, 0))],
                  out_specs=pl.BlockSpec((tile, width), lambda i: (i, 0)), sem=("parallel",))(proj, proj, w8)


def conv_bwd(proj, w8, dy, *, width, tile=512):
    n_rows = proj.shape[0]
    tile = min(tile, n_rows)
    n = n_rows // tile

    def body(x_ref, xhalo_ref, w_ref, dy_ref, dyhalo_ref, dx_ref, dw_ref):
        i = pl.program_id(0)
        dy = dy_ref[...]
        after = jnp.where(i < n - 1, dyhalo_ref[...], 0.0)
        ds = jnp.concatenate([dy, after], axis=0)
        dx = dy * w_ref[3:4, :]
        for j in range(CONV_WIDTH - 1):
            shift = CONV_WIDTH - 1 - j
            dx = dx + pltpu.roll(ds, tile + 8 - shift, 0)[:tile] * w_ref[j:j + 1, :]
        dx_ref[...] = dx.astype(dx_ref.dtype)
        halo = jnp.where(i > 0, xhalo_ref[...].astype(F32), 0.0)
        xs = jnp.concatenate([halo, x_ref[...].astype(F32)], axis=0)
        rows = [jnp.sum(dy * pltpu.roll(xs, CONV_WIDTH - 1 - j, 0)[8:], axis=0, keepdims=True)
                for j in range(CONV_WIDTH - 1)]
        rows.append(jnp.sum(dy * xs[8:], axis=0, keepdims=True))
        dw = jnp.concatenate(rows + [jnp.zeros((8 - CONV_WIDTH, width), F32)], axis=0)

        @pl.when(i == 0)
        def _():
            dw_ref[...] = dw

        @pl.when(i > 0)
        def _():
            dw_ref[...] += dw

    t8 = tile // 8
    return _pcall(body, name="gdn_conv_bwd",
                  out_shape=[jax.ShapeDtypeStruct((n_rows, width), BF16), jax.ShapeDtypeStruct((8, width), F32)],
                  grid=(n,),
                  in_specs=[pl.BlockSpec((tile, width), lambda i: (i, 0)),
                            pl.BlockSpec((8, width), lambda i: (jnp.maximum(i * t8 - 1, 0), 0)),
                            pl.BlockSpec((8, width), lambda i: (0, 0)),
                            pl.BlockSpec((tile, width), lambda i: (i, 0)),
                            pl.BlockSpec((8, width), lambda i: (jnp.minimum((i + 1) * t8, n * t8 - 1), 0))],
                  out_specs=[pl.BlockSpec((tile, width), lambda i: (i, 0)), pl.BlockSpec((8, width), lambda i: (0, 0))],
                  sem=("arbitrary",))(proj, proj, w8, dy, dy)


def gdn_scan_fwd(u, w, qk, kt, qh, gc):
    n_rows = u.shape[0]
    C, n = GDN_CHUNK, u.shape[0] // GDN_CHUNK

    def body(u_ref, w_ref, qk_ref, kt_ref, qh_ref, gc_ref, o_ref, vn_ref, sin_ref, st):
        @pl.when(pl.program_id(0) == 0)
        def _():
            st[...] = jnp.zeros_like(st)

        sin_ref[0] = st[...]
        sls = [slice(h * HEAD_DIM, (h + 1) * HEAD_DIM) for h in range(GDN_HEADS)]
        states = [st[sl, :] for sl in sls]
        v_news = [u_ref[:, sl] - sdot(w_ref[:, sl], s) for sl, s in zip(sls, states)]
        from_state = [sdot(qh_ref[:, sl], s) for sl, s in zip(sls, states)]
        for sl, a, v_new in zip(sls, from_state, v_news):
            o_ref[:, sl] = a + sdot(qk_ref[:, sl], v_new)
            vn_ref[:, sl] = v_new
        for sl, s, v_new in zip(sls, states, v_news):
            st[sl, :] = s * jnp.exp(gc_ref[C - 1:C, sl]) + sdot.tn(kt_ref[:, sl], v_new)

    blk = pl.BlockSpec((C, GDN_WIDTH), lambda i: (i, 0))
    return _pcall(body, name="gdn_scan_fwd",
                  out_shape=[jax.ShapeDtypeStruct((n_rows, GDN_WIDTH), F32), jax.ShapeDtypeStruct((n_rows, GDN_WIDTH), F32),
                             jax.ShapeDtypeStruct((n, GDN_WIDTH, HEAD_DIM), F32)],
                  grid=(n,), in_specs=[blk] * 6,
                  out_specs=[blk, blk, pl.BlockSpec((1, GDN_WIDTH, HEAD_DIM), lambda i: (i, 0, 0))],
                  scratch=[pltpu.VMEM((GDN_WIDTH, HEAD_DIM), F32)], sem=("arbitrary",))(u, w, qk, kt, qh, gc)


def gdn_scan_bwd(do, w, qk, kt, qh, gc, vn, sin):
    n_rows = do.shape[0]
    C, n = GDN_CHUNK, do.shape[0] // GDN_CHUNK

    def body(do_ref, w_ref, qk_ref, kt_ref, qh_ref, gc_ref, vn_ref, sin_ref,
             du_ref, dw_ref, dqk_ref, dkt_ref, dqh_ref, dgl_ref, dst):
        @pl.when(pl.program_id(0) == 0)
        def _():
            dst[...] = jnp.zeros_like(dst)

        sls = [slice(h * HEAD_DIM, (h + 1) * HEAD_DIM) for h in range(GDN_HEADS)]
        dvns = [sdot.tn(qk_ref[:, sl], do_ref[:, sl]) + sdot(kt_ref[:, sl], dst[sl, :]) for sl in sls]
        for sl, dvn in zip(sls, dvns):
            du_ref[:, sl] = dvn
            dw_ref[:, sl] = -sdot.nt(dvn, sin_ref[0, sl, :])
        for sl in sls:
            dqk_ref[:, sl] = sdot.nt(do_ref[:, sl], vn_ref[:, sl])
            dkt_ref[:, sl] = sdot.nt(vn_ref[:, sl], dst[sl, :])
            dqh_ref[:, sl] = sdot.nt(do_ref[:, sl], sin_ref[0, sl, :])
        for sl, dvn in zip(sls, dvns):
            ds_out = dst[sl, :]
            e = jnp.exp(gc_ref[C - 1:C, sl])
            dgl = jnp.sum(ds_out * sin_ref[0, sl, :], axis=0, keepdims=True) * e
            dgl_ref[:, sl] = jnp.broadcast_to(dgl, (8, HEAD_DIM))
            dst[sl, :] = sdot.tn(qh_ref[:, sl], do_ref[:, sl]) + e * ds_out - sdot.tn(w_ref[:, sl], dvn)

    blk = pl.BlockSpec((C, GDN_WIDTH), lambda i: (n - 1 - i, 0))
    row = jax.ShapeDtypeStruct((n_rows, GDN_WIDTH), F32)
    return _pcall(body, name="gdn_scan_bwd",
                  out_shape=[row] * 5 + [jax.ShapeDtypeStruct((n * 8, GDN_WIDTH), F32)],
                  grid=(n,), in_specs=[blk] * 7 + [pl.BlockSpec((1, GDN_WIDTH, HEAD_DIM), lambda i: (n - 1 - i, 0, 0))],
                  out_specs=[blk] * 5 + [pl.BlockSpec((8, GDN_WIDTH), lambda i: (n - 1 - i, 0))],
                  scratch=[pltpu.VMEM((GDN_WIDTH, HEAD_DIM), F32)], sem=("arbitrary",))(do, w, qk, kt, qh, gc, vn, sin)


def gdn_intra_bwd(q, k, v, gc, beta, inv, cts, dgl):
    n_rows = q.shape[0]
    C = GDN_CHUNK

    def body(*refs):
        ins = [r[...] for r in refs[:5]]
        inv_v = refs[5][...]
        ct = tuple(r[...] for r in refs[6:11])
        dgl_v = refs[11][...]
        _, vjp = jax.vjp(lambda *a: f_gdn_intra(*a, inv=inv_v), *ins)
        grads = list(vjp(ct))
        last = _iota2((C, GDN_WIDTH), 0) == C - 1
        grads[3] = grads[3] + jnp.where(last, jnp.broadcast_to(dgl_v[0:1, :], (C, GDN_WIDTH)), 0.0)
        for o_ref, g in zip(refs[12:], grads):
            o_ref[...] = g

    blk = pl.BlockSpec((C, GDN_WIDTH), lambda i: (i, 0))
    return _pcall(body, name="gdn_intra_bwd", out_shape=[jax.ShapeDtypeStruct((n_rows, GDN_WIDTH), F32)] * 5,
                  grid=(n_rows // C,), in_specs=[blk] * 11 + [pl.BlockSpec((8, GDN_WIDTH), lambda i: (i, 0))],
                  out_specs=[blk] * 5, sem=("parallel",))(q, k, v, gc, beta, inv, *cts, dgl)


def fox_gate_fwd(f, b_f):
    n_rows = f.shape[0]
    T = LANES

    def body(f_ref, b_ref, cb_ref, carry):
        @pl.when(pl.program_id(0) == 0)
        def _():
            carry[...] = jnp.zeros_like(carry)

        c = fdot(_tri(T).astype(F32), _log_sigmoid(f_ref[...] + b_ref[...])) + carry[...]
        carry[...] = c[T - 1:T, :]
        cb_ref[...] = fdot(c, _head_sel(0))

    return _pcall(body, name="fox_gate_fwd", out_shape=jax.ShapeDtypeStruct((n_rows, FOX_WIDTH), F32),
                  grid=(n_rows // T,),
                  in_specs=[pl.BlockSpec((T, LANES), lambda i: (i, 0)), pl.BlockSpec((1, LANES), lambda i: (0, 0))],
                  out_specs=pl.BlockSpec((T, FOX_WIDTH), lambda i: (i, 0)),
                  scratch=[pltpu.VMEM((1, LANES), F32)], sem=("arbitrary",))(f, b_f)


def fox_gate_bwd(f, b_f, dcrow, dcb):
    n_rows = f.shape[0]
    T = LANES
    n = n_rows // T

    def body(f_ref, b_ref, dc_ref, dcb_ref, df_ref, db_ref, carry):
        i = pl.program_id(0)

        @pl.when(i == 0)
        def _():
            carry[...] = jnp.zeros_like(carry)

        rows = [dc_ref[h] for h in range(FOX_HEADS)] + [jnp.zeros((T - FOX_HEADS, T), F32)]
        first_lane = (_iota2((FOX_WIDTH, LANES), 0) == _iota2((FOX_WIDTH, LANES), 1) * HEAD_DIM).astype(F32)
        dc = jnp.concatenate(rows, axis=0).T + fdot(dcb_ref[...], first_lane)
        dlog = fdot.tn(_tri(T).astype(F32), dc) + carry[...]
        carry[...] = dlog[0:1, :]
        df = dlog * (1.0 - _sigmoid(f_ref[...] + b_ref[...]))
        df_ref[...] = df
        db = jnp.sum(df, axis=0, keepdims=True)

        @pl.when(i == 0)
        def _():
            db_ref[...] = db

        @pl.when(i > 0)
        def _():
            db_ref[...] += db

    return _pcall(body, name="fox_gate_bwd",
                  out_shape=[jax.ShapeDtypeStruct((n_rows, LANES), F32), jax.ShapeDtypeStruct((1, LANES), F32)],
                  grid=(n,),
                  in_specs=[pl.BlockSpec((T, LANES), lambda i: (n - 1 - i, 0)), pl.BlockSpec((1, LANES), lambda i: (0, 0)),
                            pl.BlockSpec((FOX_HEADS, 1, T), lambda i: (0, 0, n - 1 - i)),
                            pl.BlockSpec((T, FOX_WIDTH), lambda i: (n - 1 - i, 0))],
                  out_specs=[pl.BlockSpec((T, LANES), lambda i: (n - 1 - i, 0)), pl.BlockSpec((1, LANES), lambda i: (0, 0))],
                  scratch=[pltpu.VMEM((1, LANES), F32)], sem=("arbitrary",))(f, b_f, dcrow, dcb)


FOX_AUG = 2 * HEAD_DIM


def _fox_tiles(n_rows):
    return min(1024, n_rows), min(1024, n_rows)


def _fox_pairs(n_rows, query_major):
    tq, tk = _fox_tiles(n_rows)
    nq, r = n_rows // tq, tq // tk
    if query_major:
        pairs = [(i, j) for i in range(nq) for j in range(r * (i + 1))]
    else:
        pairs = [(i, j) for j in range(nq * r) for i in range(j // r, nq)]
    return jnp.asarray([p[0] for p in pairs], jnp.int32), jnp.asarray([p[1] for p in pairs], jnp.int32)


def fox_augment(x, cb, query_side):
    def fn(xt, ct):
        xt = xt.astype(F32)
        lane = _iota2((xt.shape[0], HEAD_DIM), 1)
        parts = []
        for h in range(FOX_HEADS):
            sl = slice(h * HEAD_DIM, (h + 1) * HEAD_DIM)
            c = ct[:, sl]
            hi = c.astype(BF16).astype(F32)
            mid = (c - hi).astype(BF16).astype(F32)
            lo = (c - hi - mid).astype(BF16).astype(F32)
            terms = jnp.where(lane % 3 == 0, hi, jnp.where(lane % 3 == 1, mid, lo))
            if query_side:
                extra = jnp.where(lane < 3, terms, jnp.where(lane < 6, 1.0, 0.0))
                parts += [xt[:, sl] * (HEAD_DIM ** -0.5), extra]
            else:
                extra = jnp.where(lane < 3, 1.0, jnp.where(lane < 6, -terms, 0.0))
                parts += [xt[:, sl], extra]
        return (jnp.concatenate(parts, axis=1),)

    return rowwise(fn, [(x, FOX_WIDTH, 0), cb], [], [(FOX_HEADS * FOX_AUG, BF16)], tile=ROW_TILE,
                   name="fox_augment_q" if query_side else "fox_augment_k")[0]


def _pcall_tables(body, *, name, out_shape, grid, tables, in_specs, out_specs, scratch, sem):
    spec = pltpu.PrefetchScalarGridSpec(num_scalar_prefetch=len(tables), grid=grid, in_specs=in_specs, out_specs=out_specs,
                                        scratch_shapes=list(scratch))
    return pl.pallas_call(body, name=name, out_shape=out_shape, grid_spec=spec,
                          compiler_params=pltpu.CompilerParams(vmem_limit_bytes=VMEM_LIMIT_BYTES, dimension_semantics=sem))


def _fox_logits(qa, ka, offset):
    s = bdot.nt(qa, ka)
    if offset is not None:
        s = jnp.where(_iota2(s.shape, 0) + offset >= _iota2(s.shape, 1), s, NEG_INF)
    return s


def _fox_p_ds(offset, qa_ref, ka_ref, v_ref, o_ref, lse_ref, do_ref):
    s = _fox_logits(qa_ref[...], ka_ref[...], offset)
    p = jnp.exp(s - jnp.tile(lse_ref[...], (1, s.shape[1] // LANES)))
    d_o = do_ref[...].astype(F32)
    delta = jnp.sum(d_o * o_ref[...].astype(F32), axis=-1, keepdims=True)
    return p, p * (bdot.nt(d_o, v_ref[...]) - delta), d_o


def _fox_on_diagonal(i, j, r, tk, step):
    @pl.when(j < r * i)
    def _():
        step(None)

    for m in range(r):
        @pl.when(j == r * i + m)
        def _(m=m):
            step(-m * tk)


def _fox_specs(tq, tk, do_col):
    qaspec = pl.BlockSpec((tq, FOX_AUG), lambda h, p, it, jt: (it[p], h))
    qspec = pl.BlockSpec((tq, HEAD_DIM), lambda h, p, it, jt: (it[p], h))
    dospec = pl.BlockSpec((tq, HEAD_DIM), lambda h, p, it, jt: (it[p], do_col + h))
    kaspec = pl.BlockSpec((tk, FOX_AUG), lambda h, p, it, jt: (jt[p], h))
    kspec = pl.BlockSpec((tk, HEAD_DIM), lambda h, p, it, jt: (jt[p], h))
    vspec = pl.BlockSpec((tk, HEAD_DIM), lambda h, p, it, jt: (jt[p], FOX_HEADS + h))
    cspec = pl.BlockSpec((1, 1, tk), lambda h, p, it, jt: (h, 0, jt[p]))
    return qaspec, qspec, dospec, kaspec, kspec, vspec, cspec


def fox_fwd(qa, kv, ka):
    n_rows = kv.shape[0]
    tq, tk = _fox_tiles(n_rows)
    r = tq // tk
    tables = _fox_pairs(n_rows, True)

    def body(it, jt, qa_ref, ka_ref, v_ref, o_ref, lse_ref, m_sc, l_sc, acc):
        i, j = it[pl.program_id(1)], jt[pl.program_id(1)]

        @pl.when(j == 0)
        def _():
            m_sc[...] = jnp.full(m_sc.shape, NEG_INF, F32)
            l_sc[...] = jnp.zeros_like(l_sc)
            acc[...] = jnp.zeros_like(acc)

        def step(offset):
            s = _fox_logits(qa_ref[...], ka_ref[...], offset)
            m_old = m_sc[...]
            m_new = jnp.maximum(m_old, jnp.max(s, axis=-1, keepdims=True))
            alpha = jnp.exp(m_old - m_new)
            p = jnp.exp(s - jnp.tile(m_new, (1, tk // LANES)))
            l_sc[...] = l_sc[...] * alpha + jnp.sum(p, axis=-1, keepdims=True)
            acc[...] = acc[...] * alpha + bdot(p, v_ref[...])
            m_sc[...] = m_new

        _fox_on_diagonal(i, j, r, tk, step)

        @pl.when(j == r * i + r - 1)
        def _():
            o_ref[...] = (acc[...] / l_sc[...]).astype(o_ref.dtype)
            lse_ref[...] = m_sc[...] + jnp.log(l_sc[...])

    qaspec, qspec, _, kaspec, _, vspec, _ = _fox_specs(tq, tk, 0)
    return _pcall_tables(body, name="fox_fwd",
                         out_shape=[jax.ShapeDtypeStruct((n_rows, FOX_WIDTH), BF16), jax.ShapeDtypeStruct((n_rows, FOX_WIDTH), F32)],
                         grid=(FOX_HEADS, tables[0].shape[0]), tables=tables,
                         in_specs=[qaspec, kaspec, vspec], out_specs=[qspec, qspec],
                         scratch=[pltpu.VMEM((tq, HEAD_DIM), F32)] * 3, sem=("parallel", "arbitrary"))(*tables, qa, ka, kv)


def fox_bwd_dq(qa, kv, ka, o, lse, do, prev=None):
    do, _, do_col = do
    do_col *= FOX_HEADS
    n_rows = kv.shape[0]
    tq, tk = _fox_tiles(n_rows)
    r = tq // tk
    n_prev = 0 if prev is None else 1
    tables = _fox_pairs(n_rows, True)

    def body(it, jt, qa_ref, ka_ref, k_ref, v_ref, o_ref, lse_ref, do_ref, *rest):
        dq_ref, drow_ref, acc, rows = rest[n_prev:]
        i, j = it[pl.program_id(1)], jt[pl.program_id(1)]

        @pl.when(j == 0)
        def _():
            acc[...] = jnp.zeros_like(acc)
            rows[...] = jnp.zeros_like(rows)

        def step(offset):
            _, ds, _ = _fox_p_ds(offset, qa_ref, ka_ref, v_ref, o_ref, lse_ref, do_ref)
            acc[...] += bdot(ds, k_ref[...])
            rows[...] += jnp.sum(ds, axis=-1, keepdims=True)

        _fox_on_diagonal(i, j, r, tk, step)

        @pl.when(j == r * i + r - 1)
        def _():
            dq_ref[...] = (acc[...] * (HEAD_DIM ** -0.5)).astype(dq_ref.dtype)
            drow_ref[...] = rows[...] + rest[0][...] if n_prev else rows[...]

    qaspec, qspec, dospec, kaspec, kspec, vspec, _ = _fox_specs(tq, tk, do_col)
    return _pcall_tables(body, name="fox_bwd_dq" + ("_acc" if n_prev else ""),
                         out_shape=[jax.ShapeDtypeStruct((n_rows, FOX_WIDTH), BF16), jax.ShapeDtypeStruct((n_rows, FOX_WIDTH), F32)],
                         grid=(FOX_HEADS, tables[0].shape[0]), tables=tables,
                         in_specs=[qaspec, kaspec, kspec, vspec, qspec, qspec, dospec] + [qspec] * n_prev,
                         out_specs=[qspec, qspec], scratch=[pltpu.VMEM((tq, HEAD_DIM), F32)] * 2,
                         sem=("parallel", "arbitrary"))(*tables, qa, ka, kv, kv, o, lse, do, *([prev] if n_prev else []))


def fox_bwd_dkv(qa, kv, ka, o, lse, do, prev=None):
    do, _, do_col = do
    do_col *= FOX_HEADS
    n_rows = kv.shape[0]
    tq, tk = _fox_tiles(n_rows)
    nq, r = n_rows // tq, tq // tk
    n_prev = 0 if prev is None else 3
    tables = _fox_pairs(n_rows, False)

    def body(it, jt, qa_ref, ka_ref, v_ref, o_ref, lse_ref, do_ref, *rest):
        prev_refs = rest[:n_prev]
        dk_ref, dv_ref, dc_ref, dk_acc, dv_acc, dc_acc = rest[n_prev:]
        i, j = it[pl.program_id(1)], jt[pl.program_id(1)]

        @pl.when(j >= r * i)
        def _():
            dk_acc[...] = jnp.zeros_like(dk_acc)
            dv_acc[...] = jnp.zeros_like(dv_acc)
            dc_acc[...] = jnp.zeros_like(dc_acc)

        def step(offset):
            p, ds, d_o = _fox_p_ds(offset, qa_ref, ka_ref, v_ref, o_ref, lse_ref, do_ref)
            dv_acc[...] += bdot.tn(p, d_o)
            dk_acc[...] += bdot.tn(ds, qa_ref[:, :HEAD_DIM])
            dc_acc[...] -= jnp.sum(ds, axis=0, keepdims=True)

        _fox_on_diagonal(i, j, r, tk, step)

        @pl.when(i == nq - 1)
        def _():
            dk, dv, dc = dk_acc[...], dv_acc[...], dc_acc[...]
            if n_prev:
                dk, dv, dc = dk + prev_refs[0][...], dv + prev_refs[1][...], dc + prev_refs[2][0]
            dk_ref[...] = dk
            dv_ref[...] = dv
            dc_ref[0] = dc

    qaspec, qspec, dospec, kaspec, kspec, vspec, cspec = _fox_specs(tq, tk, do_col)
    return _pcall_tables(body, name="fox_bwd_dkv" + ("_acc" if n_prev else ""),
                         out_shape=[jax.ShapeDtypeStruct((n_rows, FOX_WIDTH), F32), jax.ShapeDtypeStruct((n_rows, FOX_WIDTH), F32),
                                    jax.ShapeDtypeStruct((FOX_HEADS, 1, n_rows), F32)],
                         grid=(FOX_HEADS, tables[0].shape[0]), tables=tables,
                         in_specs=[qaspec, kaspec, vspec, qspec, qspec, dospec] + [kspec, kspec, cspec][:n_prev],
                         out_specs=[kspec, kspec, cspec],
                         scratch=[pltpu.VMEM((tk, HEAD_DIM), F32), pltpu.VMEM((tk, HEAD_DIM), F32), pltpu.VMEM((1, tk), F32)],
                         sem=("parallel", "arbitrary"))(*tables, qa, ka, kv, o, lse, do, *(prev or ()))


def loss_head(h, gain, target, *, tile=512):
    n_rows, d = h.shape
    tile = min(tile, n_rows)

    def body(h_ref, g_ref, t_ref, part_ref, dy_ref):
        (y,) = f_rmsnorm(h_ref[...], g_ref[...])
        diff = y - t_ref[...]
        dy_ref[...] = diff * (1.0 / d)
        part = jnp.sum(diff * diff, axis=0, keepdims=True)
        first = pl.program_id(0) == 0

        @pl.when(first)
        def _():
            part_ref[...] = part

        @pl.when(jnp.logical_not(first))
        def _():
            part_ref[...] += part

    blk = pl.BlockSpec((tile, d), lambda i: (i, 0))
    one = pl.BlockSpec((1, d), lambda i: (0, 0))
    return _pcall(body, name="loss_head",
                  out_shape=[jax.ShapeDtypeStruct((1, d), F32), jax.ShapeDtypeStruct((n_rows, d), F32)],
                  grid=(n_rows // tile,), in_specs=[blk, one, blk], out_specs=[one, blk], sem=("arbitrary",))(h, gain, target)


def adamw(w, g, m, v, *, name):
    shape = w.shape
    cols = shape[-1] if w.ndim >= 2 else w.size
    rows = w.size // cols
    tile = _pick(rows, (256, 128, 64, 32, 16, 8))
    as2d = lambda a: a.reshape(rows, cols)

    def body(w_ref, g_ref, m_ref, v_ref, d_ref, nm_ref, nv_ref):
        g_ = g_ref[...]
        m_ = ADAM_B1 * m_ref[...] + (1.0 - ADAM_B1) * g_
        v_ = ADAM_B2 * v_ref[...] + (1.0 - ADAM_B2) * (g_ * g_)
        m_hat = m_ / (1.0 - ADAM_B1 ** ADAM_STEP)
        v_hat = v_ / (1.0 - ADAM_B2 ** ADAM_STEP)
        d_ref[...] = -ADAM_LR * (m_hat / (jnp.sqrt(v_hat) + ADAM_EPS) + ADAM_WD * w_ref[...])
        nm_ref[...] = m_
        nv_ref[...] = v_

    blk = pl.BlockSpec((tile, cols), lambda i: (i, 0))
    outs = _pcall(body, name=name, out_shape=[jax.ShapeDtypeStruct((rows, cols), F32)] * 3, grid=(rows // tile,),
                  in_specs=[blk] * 4, out_specs=[blk] * 3, sem=("parallel",))(as2d(w), as2d(g), as2d(m), as2d(v))
    return tuple(o.reshape(shape) for o in outs)


def sum_leading(a, *, name):
    p, r, c = a.shape
    tile = _pick(r, (256, 128, 64, 32, 16, 8))

    def body(a_ref, o_ref):
        total = a_ref[0].astype(F32)
        for k in range(1, p):
            total = total + a_ref[k].astype(F32)
        o_ref[...] = total

    return _pcall(body, name=name, out_shape=jax.ShapeDtypeStruct((r, c), F32), grid=(r // tile,),
                  in_specs=[pl.BlockSpec((p, tile, c), lambda i: (0, i, 0))],
                  out_specs=pl.BlockSpec((tile, c), lambda i: (i, 0)), sem=("parallel",))(a)


_HBM = pl.BlockSpec(memory_space=pltpu.HBM)


def _comm_call(body, *, name, out_shape, n_in, scratch):
    return pl.pallas_call(body, name=name, out_shape=out_shape, in_specs=[_HBM] * n_in, out_specs=_HBM,
                          scratch_shapes=scratch,
                          compiler_params=pltpu.CompilerParams(has_side_effects=True))


def all_gather8(a, *, name):
    m_per, n = a.shape

    def body(x_ref, out_ref, send_sems, recv_sems, local_sem):
        x, y, c = lax.axis_index("x"), lax.axis_index("y"), lax.axis_index("c")
        me, sibling = (x, y, c), (x, y, 1 - c)
        chips = [(1 - x, y), (x, 1 - y), (1 - x, 1 - y)]

        def rows(px, py, pc):
            return out_ref.at[pl.ds((4 * px + 2 * py + pc) * m_per, m_per), :]

        def copy(k, block, to, src=None):
            return pltpu.make_async_remote_copy(
                src_ref=rows(*block) if src is None else src, dst_ref=rows(*block),
                send_sem=send_sems.at[k], recv_sem=recv_sems.at[k], device_id=to, device_id_type=MESH)

        mine = pltpu.make_async_copy(x_ref, rows(*me), local_sem)
        mine.start()
        first = [copy(0, me, sibling, src=x_ref)]
        first += [copy(1 + j, me, (*chip, c), src=x_ref) for j, chip in enumerate(chips)]
        for cp in first:
            cp.start()
        passed = [copy(4 + j, (*chip, c), sibling) for j, chip in enumerate(chips)]
        for j, chip in enumerate(chips):
            copy(1 + j, (*chip, c), me).wait_recv()
            passed[j].start()
        copy(0, sibling, me).wait_recv()
        for j, chip in enumerate(chips):
            copy(4 + j, (*chip, 1 - c), me).wait_recv()
        for cp in first + passed:
            cp.wait_send()
        mine.wait()

    return _comm_call(body, name=name, out_shape=jax.ShapeDtypeStruct((N_DEV * m_per, n), a.dtype), n_in=1,
                      scratch=[pltpu.SemaphoreType.DMA((7,)), pltpu.SemaphoreType.DMA((7,)), pltpu.SemaphoreType.DMA])(a)


PACK_COLS = 1024
PACK_ROW_MULTIPLE = 32

WEIGHT_NAMES = ["ffn1_norm", "ffn1_w_gate_up", "ffn1_w_down", "mix_norm", "ffn2_norm", "ffn2_w_gate_up", "ffn2_w_down",
                "gdn_w_in", "gdn_conv", "gdn_A_log", "gdn_dt_bias", "gdn_out_norm", "fox_w_in", "w_out", "mem_norm",
                "mem_w_kv", "kv_norm", "kv_w", "kv_b_f", "final_norm"]
SHARDED = [("ffn1_w_gate_up", 2), ("ffn1_w_down", 1), ("ffn2_w_gate_up", 2), ("ffn2_w_down", 1), ("gdn_w_in", 2),
           ("gdn_conv", 2), ("fox_w_in", 1), ("w_out", 1), ("mem_w_kv", 1), ("kv_w", 0)]
REPLICATED = [n for n in WEIGHT_NAMES if n not in dict(SHARDED)]


PACK_PIECE_ROWS = 16


def _rows_of(size):
    return -(-size // (PACK_COLS * PACK_PIECE_ROWS)) * PACK_PIECE_ROWS


def pack(pieces, dtype, row_multiple=PACK_ROW_MULTIPLE):
    bufs, total = [], 0
    for p in pieces:
        flat = p.astype(dtype).reshape(-1)
        rows = _rows_of(flat.size)
        bufs.append(jnp.pad(flat, (0, rows * PACK_COLS - flat.size)).reshape(rows, PACK_COLS))
        total += rows
    pad = -total % row_multiple
    if pad:
        bufs.append(jnp.zeros((pad, PACK_COLS), dtype))
    return jnp.concatenate(bufs, axis=0)


def unpack(buf, shapes):
    out, row = [], 0
    for shape in shapes:
        size = 1
        for s in shape:
            size *= s
        rows = _rows_of(size)
        out.append(buf[row:row + rows].reshape(-1)[:size].reshape(shape))
        row += rows
    return out


def _row(vec, width=None):
    vec = vec.astype(F32).reshape(1, -1)
    if width is not None and vec.shape[1] < width:
        vec = jnp.pad(vec, ((0, 0), (0, width - vec.shape[1])))
    return vec


ROW_TILE = 512
GDN_PROJ_WIDTH = 4 * GDN_WIDTH + MEM_WIDTH + LANES
GDN_Z_COL, GDN_QMEM_COL, GDN_AB_COL = 3, 4 * GDN_WIDTH // MEM_WIDTH, (4 * GDN_WIDTH + MEM_WIDTH) // LANES
FOX_QMEM_COL = FOX_WIDTH // MEM_WIDTH
KV_PAD_WIDTH = 2 * FOX_WIDTH + LANES


def rms_fwd(x, gain_row, out_dtype=BF16):
    return rowwise(f_rmsnorm, [x], [gain_row], [(x.shape[1], out_dtype)], tile=ROW_TILE, name="rms_fwd")[0]


def rms_bwd(x, gain_row, dy, dres=None):
    return rowwise_bwd(f_rmsnorm, [x], [gain_row], [dy], tile=ROW_TILE, name="rms_bwd", row_grads=[F32],
                       const_grads=[True], add=None if dres is None else {0: dres})


def _ffn_tiles(n_rows):
    return _pick(n_rows, (512, 256, 128)), _pick(FFN_HIDDEN, (1408, 256, 128))


def ffn_up_act(n, wgu):
    n_rows, d = n.shape
    tm, tn = _ffn_tiles(n_rows)
    nj = FFN_HIDDEN // tn

    def body(n_ref, wg_ref, wu_ref, gu_ref, act_ref):
        x = n_ref[...].astype(BF16)
        g = bdot.nn(x, wg_ref[...])
        u = bdot.nn(x, wu_ref[...])
        gu_ref[0] = g.astype(gu_ref.dtype)
        gu_ref[1] = u.astype(gu_ref.dtype)
        act_ref[...] = (_silu(g) * u).astype(act_ref.dtype)

    return _pcall(body, name="ffn_up_act",
                  out_shape=[jax.ShapeDtypeStruct((2, n_rows, FFN_HIDDEN), BF16), jax.ShapeDtypeStruct((n_rows, FFN_HIDDEN), BF16)],
                  grid=(nj, n_rows // tm),
                  in_specs=[pl.BlockSpec((tm, d), lambda j, i: (i, 0)), pl.BlockSpec((d, tn), lambda j, i: (0, j)),
                            pl.BlockSpec((d, tn), lambda j, i: (0, nj + j))],
                  out_specs=[pl.BlockSpec((2, tm, tn), lambda j, i: (0, i, j)), pl.BlockSpec((tm, tn), lambda j, i: (i, j))],
                  sem=("parallel", "parallel"))(n, wgu, wgu)


def ffn_down_dx_act(dh, wd, gu):
    n_rows, d = dh.shape
    tm, tn = _ffn_tiles(n_rows)

    def body(dh_ref, wd_ref, gu_ref, dgu_ref):
        dact = 0.5 * bdot.nt(dh_ref[...], wd_ref[...])
        gate, up = gu_ref[0].astype(F32), gu_ref[1].astype(F32)
        sg = _sigmoid(gate)
        dgu_ref[0] = (dact * up * (sg * (1.0 + gate * (1.0 - sg)))).astype(dgu_ref.dtype)
        dgu_ref[1] = (dact * (gate * sg)).astype(dgu_ref.dtype)

    blk = pl.BlockSpec((2, tm, tn), lambda j, i: (0, i, j))
    return _pcall(body, name="ffn_down_dx_act", out_shape=jax.ShapeDtypeStruct((2, n_rows, FFN_HIDDEN), BF16),
                  grid=(FFN_HIDDEN // tn, n_rows // tm),
                  in_specs=[pl.BlockSpec((tm, d), lambda j, i: (i, 0)), pl.BlockSpec((tn, d), lambda j, i: (j, 0)), blk],
                  out_specs=blk, sem=("parallel", "parallel"))(dh, wd, gu)


def ffn_fwd(h, gain_row, wgu, wd):
    n = rms_fwd(h, gain_row)
    gu, act = ffn_up_act(n, wgu)
    return mm(act, wd, scale=0.5, res=h, name="ffn_down"), (h, n, gu, act)


def ffn_bwd(dh, saved, gain_row, wgu, wd):
    h, n, gu, act = saved
    dgu = ffn_down_dx_act(dh, wd, gu)
    dwd = mm(act, dh, ta=True, scale=0.5, name="ffn_down_dw")
    dwgu = mm(n, dgu, ta=True, b_split=True, name="ffn_up_dw")
    dh, dgain = mm_rms_bwd(dgu, wgu, h, gain_row, dh, a_split=True, name="ffn_up_dx")
    return dh, dwgu, dwd, dgain


def gdn_fwd(proj, w8, a_row, dt_row, onorm_row):
    wide = [(GDN_WIDTH, F32)] * 5
    xc = conv_fwd(proj, w8, width=3 * GDN_WIDTH)
    q, k, v, gc, beta = rowwise(f_gdn_pre, [xc, (proj, LANES, GDN_AB_COL)], [a_row, dt_row], wide, tile=GDN_CHUNK,
                                name="gdn_pre_fwd")
    u, w, qk, kt, qh, inv = rowwise(f_gdn_intra, [q, k, v, gc, beta], [], wide + wide[:1], tile=GDN_CHUNK,
                                    name="gdn_intra_fwd")
    o, vn, sin = gdn_scan_fwd(u, w, qk, kt, qh, gc)
    main = rowwise(f_gdn_post, [o, (proj, GDN_WIDTH, GDN_Z_COL)], [onorm_row], [(GDN_WIDTH, BF16)], tile=ROW_TILE,
                   name="gdn_post_fwd")[0]
    return main, (xc, q, k, v, gc, beta, inv, w, qk, kt, qh, vn, sin, o)


def gdn_bwd(dmain, proj, saved, w8, a_row, dt_row, onorm_row):
    xc, q, k, v, gc, beta, inv, w, qk, kt, qh, vn, sin, o = saved
    do, dz, donorm = rowwise_bwd(f_gdn_post, [o, (proj, GDN_WIDTH, GDN_Z_COL)], [onorm_row], [dmain], tile=ROW_TILE,
                                 name="gdn_post_bwd", row_grads=[F32, BF16], const_grads=[True])
    du, dw, dqk, dkt, dqh, dgl = gdn_scan_bwd(do, w, qk, kt, qh, gc, vn, sin)
    dq, dk, dv, dgc, dbeta = gdn_intra_bwd(q, k, v, gc, beta, inv, (du, dw, dqk, dkt, dqh), dgl)
    dxc, dab, da, ddt = rowwise_bwd(f_gdn_pre, [xc, (proj, LANES, GDN_AB_COL)], [a_row, dt_row], [dq, dk, dv, dgc, dbeta],
                                    tile=GDN_CHUNK, name="gdn_pre_bwd", row_grads=[F32, BF16], const_grads=[True, True])
    dqkv, dw8 = conv_bwd(proj, w8, dxc, width=3 * GDN_WIDTH)
    return dqkv, dz, dab, dw8, da, ddt, donorm


def mem_fwd(q, kmem, vmem):
    return rowwise(f_mem_attn, [q], [kmem, vmem], [(MEM_WIDTH, BF16)], tile=ROW_TILE, name="mem_attn_fwd")[0]


def mem_bwd(q, kmem, vmem, dout):
    return rowwise_bwd(f_mem_attn, [q], [kmem, vmem], [dout], tile=ROW_TILE, name="mem_attn_bwd", row_grads=[BF16],
                       const_grads=[True, True])


def forward_backward(xs, mems, target, P):
    depth, n_a = 4, 2
    G = {}
    mem_gain = _row(P["mem_norm"])
    mem_n = rms_fwd(mems, mem_gain)
    h = xs
    saved = []
    shared = None
    for l in range(depth):
        h0 = h
        h1, s1 = ffn_fwd(h0, _row(P["ffn1_norm"][l]), P["ffn1_w_gate_up"][l], P["ffn1_w_down"][l])
        u = rms_fwd(h1, _row(P["mix_norm"][l]))
        kvm = mm(mem_n, P["mem_w_kv"][l], name="mem_kv")
        kmem, vmem = kvm[:, :MEM_WIDTH], kvm[:, MEM_WIDTH:]
        if l < n_a:
            gp = (P["conv8"][l], _row(P["gdn_A_log"][l], LANES), _row(P["gdn_dt_bias"][l], LANES), _row(P["gdn_out_norm"][l]))
            proj = mm(u, P["gdn_w_in_pad"][l], name="gdn_in")
            main, sm = gdn_fwd(proj, *gp)
            qm = (proj, MEM_WIDTH, GDN_QMEM_COL)
        else:
            proj = mm(u, P["fox_w_in"][l - n_a], out_dtype=BF16, name="fox_in")
            kv, ka, cb = shared
            qa = fox_augment(proj, cb, True)
            main, lse = fox_fwd(qa, kv, ka)
            sm = (main, lse, qa)
            qm = (proj, MEM_WIDTH, FOX_QMEM_COL)
        mo = mem_fwd(qm, kmem, vmem)
        cat = jnp.concatenate([main, mo], axis=1)
        h2 = mm(cat, P["w_out"][l], res=h1, name="mix_out")
        h3, s2 = ffn_fwd(h2, _row(P["ffn2_norm"][l]), P["ffn2_w_gate_up"][l], P["ffn2_w_down"][l])
        saved.append((s1, h1, u, kmem, vmem, proj, sm, qm, cat, s2))
        h = h3
        if l == n_a - 1:
            nkv = rms_fwd(h, _row(P["kv_norm"]))
            kv = mm(nkv, P["kv_w_pad"][:, :2 * FOX_WIDTH], out_dtype=BF16, name="fox_kv")
            f = mm(nkv, P["kv_w_pad"][:, 2 * FOX_WIDTH:], name="fox_f")
            bf_row = _row(P["kv_b_f"], LANES)
            cb = fox_gate_fwd(f, bf_row)
            shared = (kv, fox_augment(kv, cb, False), cb)
            kv_saved = (h, nkv, f, bf_row)

    part, dy = loss_head(h, _row(P["final_norm"]), target)
    dh, G["final_norm"] = rms_bwd(h, _row(P["final_norm"]), dy)

    per_layer = {n: [None] * depth for n in ("ffn1_norm", "ffn1_w_gate_up", "ffn1_w_down", "mix_norm", "ffn2_norm",
                                             "ffn2_w_gate_up", "ffn2_w_down", "w_out", "mem_w_kv")}
    gdn_g = {n: [None] * n_a for n in ("gdn_w_in_pad", "conv8", "gdn_A_log", "gdn_dt_bias", "gdn_out_norm")}
    fox_g = [None] * (depth - n_a)
    dmem_n = None
    dkv_acc = dcb_acc = None
    for l in reversed(range(depth)):
        s1, h1, u, kmem, vmem, proj, sm, qm, cat, s2 = saved[l]
        if l == n_a - 1:
            hk, nkv, f, bf_row = kv_saved
            dk, dv, dcrow = dkv_acc
            df, dbf = fox_gate_bwd(f, bf_row, dcrow, dcb_acc)
            dp = jnp.concatenate([dk.astype(BF16), dv.astype(BF16), df.astype(BF16)], axis=1)
            G["kv_w_pad"] = mm(nkv, dp, ta=True, name="fox_kv_dw")
            G["kv_b_f"] = dbf
            dh, G["kv_norm"] = mm_rms_bwd(dp, P["kv_w_pad"], hk, _row(P["kv_norm"]), dh, name="fox_kv_dx")
        dh, per_layer["ffn2_w_gate_up"][l], per_layer["ffn2_w_down"][l], per_layer["ffn2_norm"][l] = ffn_bwd(
            dh, s2, _row(P["ffn2_norm"][l]), P["ffn2_w_gate_up"][l], P["ffn2_w_down"][l])
        dcat = mm(dh, P["w_out"][l], tb=True, out_dtype=BF16, name="mix_out_dx")
        per_layer["w_out"][l] = mm(cat, dh, ta=True, name="mix_out_dw")
        dqm, dkm, dvm = mem_bwd(qm, kmem, vmem, (dcat, MEM_WIDTH, FOX_QMEM_COL))
        dkvm = jnp.concatenate([dkm, dvm], axis=1)
        per_layer["mem_w_kv"][l] = mm(mem_n, dkvm, ta=True, name="mem_kv_dw")
        dmem_n = mm(dkvm, P["mem_w_kv"][l], tb=True, res=dmem_n, name="mem_kv_dx")
        dmain = (dcat, GDN_WIDTH, 0)
        if l < n_a:
            gp = (P["conv8"][l], _row(P["gdn_A_log"][l], LANES), _row(P["gdn_dt_bias"][l], LANES), _row(P["gdn_out_norm"][l]))
            dqkv, dz, dab, gdn_g["conv8"][l], gdn_g["gdn_A_log"][l], gdn_g["gdn_dt_bias"][l], gdn_g["gdn_out_norm"][l] = gdn_bwd(
                dmain, proj, sm, *gp)
            dproj = jnp.concatenate([dqkv, dz, dqm, dab], axis=1)
            gdn_g["gdn_w_in_pad"][l] = mm(u, dproj, ta=True, name="gdn_in_dw")
            w_in = P["gdn_w_in_pad"][l]
        else:
            o, lse, qa = sm
            kv, ka, _ = shared
            dq, dcb_acc = fox_bwd_dq(qa, kv, ka, o, lse, dmain, dcb_acc)
            dkv_acc = fox_bwd_dkv(qa, kv, ka, o, lse, dmain, dkv_acc)
            dproj = jnp.concatenate([dq, dqm], axis=1)
            fox_g[l - n_a] = mm(u, dproj, ta=True, name="fox_in_dw")
            w_in = P["fox_w_in"][l - n_a]
        dh, per_layer["mix_norm"][l] = mm_rms_bwd(dproj, w_in, h1, _row(P["mix_norm"][l]), dh, name="mix_in_dx")
        dh, per_layer["ffn1_w_gate_up"][l], per_layer["ffn1_w_down"][l], per_layer["ffn1_norm"][l] = ffn_bwd(
            dh, s1, _row(P["ffn1_norm"][l]), P["ffn1_w_gate_up"][l], P["ffn1_w_down"][l])

    (G["mem_norm"],) = rowwise_bwd(f_rmsnorm, [mems], [mem_gain], [dmem_n], tile=ROW_TILE, name="mem_norm_bwd",
                                   row_grads=[None], const_grads=[True])
    for n, v in per_layer.items():
        G[n] = jnp.stack(v)
    for n, v in gdn_g.items():
        G[n] = jnp.stack(v)
    G["fox_w_in"] = jnp.stack(fox_g)
    return part, dh, G


_GDN_O0 = 4 * GDN_WIDTH
_GDN_O1 = _GDN_O0 + 2 * GDN_HEADS
_KV_WIDTH = 2 * FOX_WIDTH + FOX_HEADS


def derived_weights(gdn_w_in, gdn_conv, kv_w=None):
    zeros = jnp.zeros(gdn_w_in.shape[:-1] + (LANES - 2 * GDN_HEADS,), gdn_w_in.dtype)
    out = dict(
        gdn_w_in_pad=jnp.concatenate([gdn_w_in[..., :_GDN_O0], gdn_w_in[..., _GDN_O1:], gdn_w_in[..., _GDN_O0:_GDN_O1], zeros], axis=-1),
        conv8=jnp.pad(gdn_conv.astype(F32), ((0, 0), (0, 8 - CONV_WIDTH), (0, 0))))
    if kv_w is not None:
        out["kv_w_pad"] = jnp.pad(kv_w, ((0, 0), (0, KV_PAD_WIDTH - _KV_WIDTH)))
    return out


def reference_layout(G):
    gp = G["gdn_w_in_pad"]
    out = dict(G)
    out["gdn_w_in"] = jnp.concatenate([gp[..., :_GDN_O0], gp[..., _GDN_O0 + MEM_WIDTH:_GDN_O0 + MEM_WIDTH + 2 * GDN_HEADS],
                                       gp[..., _GDN_O0:_GDN_O0 + MEM_WIDTH]], axis=-1)
    out["gdn_conv"] = G["conv8"][:, :CONV_WIDTH]
    out["kv_w"] = G["kv_w_pad"][:, :_KV_WIDTH]
    out["gdn_A_log"] = G["gdn_A_log"][:, 0, :GDN_HEADS]
    out["gdn_dt_bias"] = G["gdn_dt_bias"][:, 0, :GDN_HEADS]
    out["gdn_out_norm"] = G["gdn_out_norm"][:, 0, :]
    out["kv_b_f"] = G["kv_b_f"][0, :FOX_HEADS]
    for n in ("ffn1_norm", "mix_norm", "ffn2_norm"):
        out[n] = G[n][:, 0, :]
    for n in ("mem_norm", "kv_norm", "final_norm"):
        out[n] = G[n][0]
    return {n: out[n] for n in WEIGHT_NAMES}


EXCHANGED = [("ffn1_w_gate_up", 2, 0), ("ffn1_w_down", 1, 0), ("ffn2_w_gate_up", 2, 0), ("ffn2_w_down", 1, 0),
             ("gdn_w_in", 2, 0), ("fox_w_in", 1, 0), ("w_out", 1, 0), ("mem_w_kv", 1, 0), ("kv_w", 0, 1)]
GDN_IN_SHARD = (4 * GDN_WIDTH + 2 * GDN_HEADS + MEM_WIDTH) // N_CHIPS
GDN_IN_SLOT = 896


def _slab(ref, axis_slices):
    idx = [slice(None)] * len(ref.shape)
    for axis, (start, size) in axis_slices.items():
        idx[axis] = pl.ds(start, size)
    return ref.at[tuple(idx)]


def _comm_multi(body, *, name, n_in, out_shapes, scratch):
    return pl.pallas_call(body, name=name, out_shape=out_shapes, in_specs=[_HBM] * n_in, out_specs=[_HBM] * len(out_shapes),
                          scratch_shapes=scratch, compiler_params=pltpu.CompilerParams(has_side_effects=True))


def gather_shards(shards, layout):
    n = len(shards)
    fulls = [tuple(d * (N_CHIPS if a == sa else 1) for a, d in enumerate(s.shape)) for s, (sa, _) in zip(shards, layout)]
    n_sem = 9

    def pieces(w):
        sa, ha = layout[w]
        axis = 3 - sa - ha
        size = shards[w].shape[axis]
        unit = LANES if axis == 2 else 16
        first = -(-(size // 2) // unit) * unit
        return axis, [(0, first), (first, size - first)]

    def body(*refs):
        ins, outs = refs[:n], refs[n:2 * n]
        send_sems, recv_sems, local_sems = refs[2 * n:]
        x, y, c = lax.axis_index("x"), lax.axis_index("y"), lax.axis_index("c")
        me, sibling, x_nbr, y_nbr = (x, y, c), (x, y, 1 - c), (1 - x, y, c), (x, 1 - y, c)
        chip_x, chip_y, chip_d = (1 - x, y), (x, 1 - y), (1 - x, 1 - y)

        def region(w, chip, pc, piece=None):
            (sa, ha), shard = layout[w], shards[w].shape
            where = {sa: ((2 * chip[0] + chip[1]) * shard[sa], shard[sa]), ha: (pc * (shard[ha] // 2), shard[ha] // 2)}
            if piece is not None:
                axis, parts = pieces(w)
                where[axis] = parts[piece]
            return _slab(outs[w], where)

        def my_half(w):
            ha, shard = layout[w][1], shards[w].shape
            return _slab(ins[w], {ha: (c * (shard[ha] // 2), shard[ha] // 2)})

        def copy(w, k, where, to, src=None):
            return pltpu.make_async_remote_copy(
                src_ref=where if src is None else src, dst_ref=where, send_sem=send_sems.at[n_sem * w + k],
                recv_sem=recv_sems.at[n_sem * w + k], device_id=to, device_id_type=MESH)

        mine, sends = [], [[] for _ in range(n)]
        for w in range(n):
            mine.append(pltpu.make_async_copy(my_half(w), region(w, (x, y), c), local_sems.at[w]))
            mine[w].start()
            sends[w] = [copy(w, k, region(w, (x, y), c), to, src=my_half(w)) for k, to in enumerate((sibling, x_nbr, y_nbr))]
            for cp in sends[w]:
                cp.start()
        for w in range(n):
            copy(w, 1, region(w, chip_x, c), me).wait_recv()
            onward = [copy(w, 3, region(w, chip_x, c, 0), y_nbr), copy(w, 5, region(w, chip_x, c), sibling)]
            for cp in onward:
                cp.start()
            sends[w] += onward
            copy(w, 2, region(w, chip_y, c), me).wait_recv()
            onward = [copy(w, 4, region(w, chip_y, c, 1), x_nbr), copy(w, 6, region(w, chip_y, c), sibling)]
            for cp in onward:
                cp.start()
            sends[w] += onward
        for w in range(n):
            copy(w, 3, region(w, chip_d, c, 0), me).wait_recv()
            copy(w, 4, region(w, chip_d, c, 1), me).wait_recv()
            onward = [copy(w, 7, region(w, chip_d, c, 0), sibling), copy(w, 8, region(w, chip_d, c, 1), sibling)]
            for cp in onward:
                cp.start()
            sends[w] += onward
        for w in range(n):
            copy(w, 0, region(w, (x, y), 1 - c), me).wait_recv()
            copy(w, 5, region(w, chip_x, 1 - c), me).wait_recv()
            copy(w, 6, region(w, chip_y, 1 - c), me).wait_recv()
            copy(w, 7, region(w, chip_d, 1 - c, 0), me).wait_recv()
            copy(w, 8, region(w, chip_d, 1 - c, 1), me).wait_recv()
        for w in range(n):
            for cp in sends[w]:
                cp.wait_send()
            mine[w].wait()

    return _comm_multi(body, name="gather_shards", n_in=n,
                       out_shapes=[jax.ShapeDtypeStruct(f, s.dtype) for f, s in zip(fulls, shards)],
                       scratch=[pltpu.SemaphoreType.DMA((n_sem * n,)), pltpu.SemaphoreType.DMA((n_sem * n,)),
                                pltpu.SemaphoreType.DMA((n,))])(*shards)


def swap_other_halves(arrays, layout):
    n = len(arrays)
    halves = [tuple(d // 2 if a == ha else d for a, d in enumerate(g.shape)) for g, (_, ha) in zip(arrays, layout)]

    def body(*refs):
        ins, outs = refs[:n], refs[n:2 * n]
        send_sems, recv_sems = refs[2 * n:]
        x, y, c = lax.axis_index("x"), lax.axis_index("y"), lax.axis_index("c")
        copies = []
        for w in range(n):
            ha, size = layout[w][1], halves[w][layout[w][1]]
            copies.append(pltpu.make_async_remote_copy(
                src_ref=_slab(ins[w], {ha: ((1 - c) * size, size)}), dst_ref=outs[w], send_sem=send_sems.at[w],
                recv_sem=recv_sems.at[w], device_id=(x, y, 1 - c), device_id_type=MESH))
            copies[w].start()
        for cp in copies:
            cp.wait()

    return _comm_multi(body, name="grad_pair_swap", n_in=n,
                       out_shapes=[jax.ShapeDtypeStruct(h, g.dtype) for h, g in zip(halves, arrays)],
                       scratch=[pltpu.SemaphoreType.DMA((n,)), pltpu.SemaphoreType.DMA((n,))])(*arrays)


def scatter_to_chips(arrays, layout):
    n = len(arrays)
    slabs = [tuple(d // N_CHIPS if a == sa else d for a, d in enumerate(p.shape)) for p, (sa, _) in zip(arrays, layout)]

    def body(*refs):
        ins, outs = refs[:n], refs[n:2 * n]
        send_sems, recv_sems, local_sems = refs[2 * n:]
        x, y, c = lax.axis_index("x"), lax.axis_index("y"), lax.axis_index("c")
        me = 2 * x + y

        def slab(w, k):
            sa, size = layout[w][0], slabs[w][layout[w][0]]
            return _slab(ins[w], {sa: (k * size, size)})

        local, copies = [], []
        for w in range(n):
            local.append(pltpu.make_async_copy(slab(w, me), outs[w].at[me], local_sems.at[w]))
            local[w].start()
            for j, (px, py) in enumerate([(1 - x, y), (x, 1 - y), (1 - x, 1 - y)]):
                copies.append(pltpu.make_async_remote_copy(
                    src_ref=slab(w, 2 * px + py), dst_ref=outs[w].at[me], send_sem=send_sems.at[3 * w + j],
                    recv_sem=recv_sems.at[3 * w + j], device_id=(px, py, c), device_id_type=MESH))
                copies[-1].start()
        for cp in copies:
            cp.wait()
        for cp in local:
            cp.wait()

    return _comm_multi(body, name="grad_all_to_all", n_in=n,
                       out_shapes=[jax.ShapeDtypeStruct((N_CHIPS,) + s, p.dtype) for s, p in zip(slabs, arrays)],
                       scratch=[pltpu.SemaphoreType.DMA((3 * n,)), pltpu.SemaphoreType.DMA((3 * n,)),
                                pltpu.SemaphoreType.DMA((n,))])(*arrays)


def swap_with_sibling(arrays):
    n = len(arrays)

    def body(*refs):
        ins, outs = refs[:n], refs[n:2 * n]
        send_sems, recv_sems = refs[2 * n:]
        x, y, c = lax.axis_index("x"), lax.axis_index("y"), lax.axis_index("c")
        copies = [pltpu.make_async_remote_copy(src_ref=ins[w], dst_ref=outs[w], send_sem=send_sems.at[w], recv_sem=recv_sems.at[w],
                                               device_id=(x, y, 1 - c), device_id_type=MESH) for w in range(n)]
        for cp in copies:
            cp.start()
        for cp in copies:
            cp.wait()

    return _comm_multi(body, name="grad_half_swap", n_in=n, out_shapes=[jax.ShapeDtypeStruct(a.shape, a.dtype) for a in arrays],
                       scratch=[pltpu.SemaphoreType.DMA((n,)), pltpu.SemaphoreType.DMA((n,))])(*arrays)


def join_halves(mine, other, half_axis, c_arr, *, name):
    a0, a1, a2 = mine.shape
    tile = _row_tile(a1, a2)

    def body(c_ref, m_ref, o_ref, out_ref):
        for half in range(2):
            @pl.when(c_ref[0] == half)
            def _(half=half):
                out_ref[half] = m_ref[...]
                out_ref[1 - half] = o_ref[...]

    blk = pl.BlockSpec((None, tile, a2), lambda i, j, c_ref: (i, j, 0))
    if half_axis == 0:
        out_shape, out_blk = (2, a0, a1, a2), pl.BlockSpec((2, None, tile, a2), lambda i, j, c_ref: (0, i, j, 0))
    else:
        out_shape, out_blk = (a0, 2, a1, a2), pl.BlockSpec((None, 2, tile, a2), lambda i, j, c_ref: (i, 0, j, 0))
    spec = pltpu.PrefetchScalarGridSpec(num_scalar_prefetch=1, grid=(a0, a1 // tile), in_specs=[blk, blk], out_specs=out_blk)
    out = pl.pallas_call(body, name=name, out_shape=jax.ShapeDtypeStruct(out_shape, mine.dtype), grid_spec=spec,
                         compiler_params=pltpu.CompilerParams(vmem_limit_bytes=VMEM_LIMIT_BYTES,
                                                              dimension_semantics=("parallel", "parallel")))(c_arr, mine, other)
    return out.reshape((2 * a0, a1, a2) if half_axis == 0 else (a0, 2 * a1, a2))


def _row_tile(rows, cols, itemsize=4, budget=2 * 1024 * 1024):
    for t in (1024, 512, 256, 128, 64, 32, 16):
        if rows % t == 0 and t * cols * itemsize <= budget:
            return t
    return rows


def add_own_half(full, recv, half_axis, c_arr, *, name):
    a0, a1, a2 = recv.shape
    tile = _row_tile(a1, a2)

    def body(c_ref, f_ref, r_ref, o_ref):
        o_ref[...] = (f_ref[...] + r_ref[...]).astype(o_ref.dtype)

    if half_axis == 0:
        f_spec = pl.BlockSpec((None, tile, a2), lambda i, j, c_ref: (c_ref[0] * a0 + i, j, 0))
    else:
        f_spec = pl.BlockSpec((None, tile, a2), lambda i, j, c_ref: (i, c_ref[0] * (a1 // tile) + j, 0))
    blk = pl.BlockSpec((None, tile, a2), lambda i, j, c_ref: (i, j, 0))
    spec = pltpu.PrefetchScalarGridSpec(num_scalar_prefetch=1, grid=(a0, a1 // tile), in_specs=[f_spec, blk], out_specs=blk)
    return pl.pallas_call(body, name=name, out_shape=jax.ShapeDtypeStruct(recv.shape, BF16), grid_spec=spec,
                          compiler_params=pltpu.CompilerParams(vmem_limit_bytes=VMEM_LIMIT_BYTES,
                                                               dimension_semantics=("parallel", "parallel")))(c_arr, full, recv)


def sum_slots(q, *, name):
    _, a0, a1, a2 = q.shape
    tile = _row_tile(a1, a2, budget=1024 * 1024)

    def body(q_ref, o_ref):
        total = q_ref[0].astype(F32)
        for k in range(1, N_CHIPS):
            total = total + q_ref[k].astype(F32)
        o_ref[...] = total

    return _pcall(body, name=name, out_shape=jax.ShapeDtypeStruct((a0, a1, a2), F32), grid=(a0, a1 // tile),
                  in_specs=[pl.BlockSpec((N_CHIPS, None, tile, a2), lambda i, j: (0, i, j, 0))],
                  out_specs=pl.BlockSpec((None, tile, a2), lambda i, j: (i, j, 0)), sem=("parallel", "parallel"))(q)


def gather_weights(W):
    shards = []
    for name, _, _ in EXCHANGED:
        w = W[name].astype(BF16)
        if name == "gdn_w_in":
            w = jnp.pad(w, ((0, 0), (0, 0), (0, GDN_IN_SLOT - GDN_IN_SHARD)))
        if name == "kv_w":
            w = jnp.pad(w, ((0, 0), (0, KV_PAD_WIDTH - _KV_WIDTH)))[None]
        shards.append(w)
    fulls = dict(zip([n for n, _, _ in EXCHANGED], gather_shards(shards, [(sa, ha) for _, sa, ha in EXCHANGED])))
    slots = fulls["gdn_w_in"]
    fulls["gdn_w_in"] = jnp.concatenate([slots[..., k * GDN_IN_SLOT:k * GDN_IN_SLOT + GDN_IN_SHARD] for k in range(N_CHIPS)], axis=-1)
    fulls["kv_w_pad"] = fulls.pop("kv_w").reshape(-1, KV_PAD_WIDTH)
    conv = pack([W["gdn_conv"]], F32, row_multiple=8)
    conv_all = all_gather8(conv, name="gather_conv").reshape(N_DEV, conv.shape[0], PACK_COLS)
    fulls["gdn_conv"] = jnp.concatenate([unpack(conv_all[2 * k], [W["gdn_conv"].shape])[0] for k in range(N_CHIPS)], axis=-1)
    return fulls


def reduce_gradients(G):
    layout = [(sa, ha) for _, sa, ha in EXCHANGED]
    c_arr = lax.axis_index("c").astype(jnp.int32).reshape(1)
    received = swap_other_halves(G, layout)
    pairs = [add_own_half(g, r, ha, c_arr, name="grad_pair_sum") for g, r, (_, ha) in zip(G, received, layout)]
    slots = scatter_to_chips(pairs, layout)
    halves = [sum_slots(q, name="grad_chip_sum") for q in slots]
    others = swap_with_sibling(halves)
    return [join_halves(h, o, ha, c_arr, name="grad_join_halves") for h, o, (_, ha) in zip(halves, others, layout)]


def allreduce_small(G, names):
    packed = pack([G[n] for n in names], F32, row_multiple=8)
    gathered = all_gather8(packed, name="gather_small_grads").reshape(N_DEV, packed.shape[0], PACK_COLS)
    total = sum_leading(gathered, name="sum_small_grads")
    return dict(zip(names, unpack(total, [G[n].shape for n in names])))


def kernel(x, mem, *rest):
    n_w = len(WEIGHT_NAMES)
    W = dict(zip(WEIGHT_NAMES, rest[:n_w]))
    target = rest[n_w]
    M = dict(zip(WEIGHT_NAMES, rest[n_w + 1:2 * n_w + 1]))
    V = dict(zip(WEIGHT_NAMES, rest[2 * n_w + 1:3 * n_w + 1]))

    full = gather_weights(W)
    P = {n: W[n] for n in REPLICATED}
    P.update({n: full[n] for n in ("ffn1_w_gate_up", "ffn1_w_down", "ffn2_w_gate_up", "ffn2_w_down", "fox_w_in", "w_out",
                                   "mem_w_kv", "kv_w_pad")})
    derived = derived_weights(full["gdn_w_in"], full["gdn_conv"])
    P.update(gdn_w_in_pad=derived["gdn_w_in_pad"], conv8=derived["conv8"])

    part, dx, G = forward_backward(x[0], mem[0], target[0], P)
    loss = lax.psum(0.5 / x.shape[-1] * jnp.sum(part), ("x", "y", "c"))

    ref = reference_layout(G)
    exchange = {n: ref[n] for n, _, _ in EXCHANGED}
    exchange["gdn_w_in"] = jnp.concatenate(
        [jnp.pad(ref["gdn_w_in"][..., k * GDN_IN_SHARD:(k + 1) * GDN_IN_SHARD], ((0, 0), (0, 0), (0, GDN_IN_SLOT - GDN_IN_SHARD)))
         for k in range(N_CHIPS)], axis=-1)
    exchange["kv_w"] = G["kv_w_pad"].reshape(N_CHIPS, -1, KV_PAD_WIDTH)
    shards = dict(zip([n for n, _, _ in EXCHANGED], reduce_gradients([exchange[n] for n, _, _ in EXCHANGED])))
    shards["gdn_w_in"] = shards["gdn_w_in"][..., :GDN_IN_SHARD]
    shards["kv_w"] = shards["kv_w"][0, :, :_KV_WIDTH]
    grads = allreduce_small(ref, REPLICATED + ["gdn_conv"])
    conv_cols = W["gdn_conv"].shape[-1]
    chip = 2 * lax.axis_index("x") + lax.axis_index("y")
    grads["gdn_conv"] = lax.dynamic_slice_in_dim(grads["gdn_conv"], chip * conv_cols, conv_cols, axis=2)
    grads.update(shards)

    outs = {n: adamw(W[n], grads[n], M[n], V[n], name="adamw_" + n) for n in WEIGHT_NAMES}
    return (loss, dx[None], *[grads[n] for n in WEIGHT_NAMES], *[outs[n][0] for n in WEIGHT_NAMES],
            *[outs[n][1] for n in WEIGHT_NAMES], *[outs[n][2] for n in WEIGHT_NAMES])
```

```python
import jax
import jax.numpy as jnp
from jax import lax
from jax.experimental import pallas as pl
from jax.experimental.pallas import tpu as pltpu

F32, BF16 = jnp.float32, jnp.bfloat16
HI = lax.Precision.HIGHEST
MESH = pl.DeviceIdType.MESH

VMEM_LIMIT_BYTES = 56 * 1024 * 1024
LANES = 128
EPS = 1e-6
NEG_INF = -1e30

D_MODEL = 1024
HEAD_DIM = 128
GDN_HEADS = 6
GDN_WIDTH = GDN_HEADS * HEAD_DIM
FOX_HEADS = 6
FOX_WIDTH = FOX_HEADS * HEAD_DIM
MEM_HEADS = 4
MEM_HEAD_DIM = 64
MEM_WIDTH = MEM_HEADS * MEM_HEAD_DIM
FFN_HIDDEN = 2816
CONV_WIDTH = 4
GDN_CHUNK = 128
N_CHIPS = 4
N_DEV = 8

ADAM_LR, ADAM_B1, ADAM_B2, ADAM_EPS, ADAM_WD, ADAM_STEP = 0.001, 0.9, 0.999, 1e-08, 0.01, 10


def _pcall(body, *, name, out_shape, grid=(), in_specs=None, out_specs=None, scratch=(), sem=None):
    params = dict(vmem_limit_bytes=VMEM_LIMIT_BYTES)
    if sem is not None:
        params["dimension_semantics"] = sem
    kw = dict(grid=grid, in_specs=in_specs, out_specs=out_specs) if grid else {}
    return pl.pallas_call(body, name=name, out_shape=out_shape, scratch_shapes=list(scratch),
                          compiler_params=pltpu.CompilerParams(**params), **kw)


def _pick(n, cands):
    for c in cands:
        if n % c == 0:
            return c
    return n


def _make_dot(dtype, precision):
    def raw(a, b, dims):
        return lax.dot_general(a.astype(dtype), b.astype(dtype), (dims, ((), ())),
                               precision=precision, preferred_element_type=F32)

    @jax.custom_vjp
    def dot(a, b):
        return raw(a, b, ((1,), (0,)))

    def fwd(a, b):
        return dot(a, b), (a, b)

    def bwd(resid, ct):
        a, b = resid
        return raw(ct, b, ((1,), (1,))).astype(a.dtype), raw(a, ct, ((0,), (0,))).astype(b.dtype)

    dot.defvjp(fwd, bwd)
    dot.nn = lambda a, b: raw(a, b, ((1,), (0,)))
    dot.nt = lambda a, b: raw(a, b, ((1,), (1,)))
    dot.tn = lambda a, b: raw(a, b, ((0,), (0,)))
    return dot


bdot = _make_dot(BF16, None)
fdot = _make_dot(F32, HI)
idot = _make_dot(F32, lax.Precision.HIGH)
sdot = idot


def _sigmoid(x):
    return 0.5 * jnp.tanh(0.5 * x) + 0.5


def _silu(x):
    return x * _sigmoid(x)


def _softplus(x):
    return jnp.maximum(x, 0.0) + jnp.log(1.0 + jnp.exp(-jnp.abs(x)))


def _log_sigmoid(x):
    return -_softplus(-x)


def _iota2(shape, dim):
    return lax.broadcasted_iota(jnp.int32, shape, dim)


def mm(a, b, *, ta=False, tb=False, a_split=False, b_split=False, out_dtype=F32, scale=1.0, res=None, name):
    assert not (a_split and ta) and not (b_split and tb)
    (K, M) = a.shape if ta else ((2 * a.shape[2], a.shape[1]) if a_split else a.shape[::-1])
    (N, Kb) = b.shape if tb else ((2 * b.shape[2], b.shape[1]) if b_split else b.shape[::-1])
    assert K == Kb, (a.shape, b.shape, ta, tb)
    tm = _pick(M, (1024, 1408, 512, 256, 128))
    tn = _pick(N, (1024, 1408, 1152, 1664, 768, 512, 384, 256, 128))
    deep = a.dtype == BF16 and b.dtype == BF16
    if ta:
        tk = _pick(K, (2048, 1024, 512, 256, 128) if deep else (1024, 512, 256, 128))
    else:
        tk = _pick(K, (1024, 2816, 1408, 1152, 1664, 512, 256, 128) if deep else (1024, 1408, 1152, 1664, 512, 256, 128))
    assert not a_split or (K // 2) % tk == 0
    assert not b_split or (N // 2) % tn == 0
    nk = K // tk
    dims = (((0 if ta else 1,), (1 if tb else 0,)), ((), ()))

    def body(a_ref, b_ref, *rest):
        o_ref, acc = rest[-2], rest[-1]
        k = pl.program_id(2)

        @pl.when(k == 0)
        def _():
            acc[...] = jnp.zeros_like(acc)

        acc[...] += lax.dot_general(a_ref[...].astype(BF16), b_ref[...].astype(BF16), dims,
                                    preferred_element_type=F32)

        @pl.when(k == nk - 1)
        def _():
            out = acc[...] * scale
            if res is not None:
                out = out + rest[0][...].astype(F32)
            o_ref[...] = out.astype(o_ref.dtype)

    a_spec = pl.BlockSpec((tk, tm), lambda i, j, k: (k, i)) if ta else pl.BlockSpec((tm, tk), lambda i, j, k: (i, k))
    b_spec = pl.BlockSpec((tn, tk), lambda i, j, k: (j, k)) if tb else pl.BlockSpec((tk, tn), lambda i, j, k: (k, j))
    if a_split:
        per_half = K // 2 // tk
        a_spec = pl.BlockSpec((None, tm, tk), lambda i, j, k: (k // per_half, i, k % per_half))
    if b_split:
        per_half = N // 2 // tn
        b_spec = pl.BlockSpec((None, tk, tn), lambda i, j, k: (j // per_half, k, j % per_half))
    o_spec = pl.BlockSpec((tm, tn), lambda i, j, k: (i, j))
    ins, specs = [a, b], [a_spec, b_spec]
    if res is not None:
        ins.append(res)
        specs.append(o_spec)
    return _pcall(body, name=name, out_shape=jax.ShapeDtypeStruct((M, N), out_dtype),
                  grid=(M // tm, N // tn, nk), in_specs=specs, out_specs=o_spec,
                  scratch=[pltpu.VMEM((tm, tn), F32)], sem=("parallel", "parallel", "arbitrary"))(*ins)


def mm_rms_bwd(a, b, x, gain_row, dres, *, a_split=False, name):
    (K, M) = (2 * a.shape[2], a.shape[1]) if a_split else a.shape[::-1]
    D = b.shape[0]
    assert b.shape[1] == K and x.shape == (M, D)
    tm = _pick(M, (1024, 512, 256, 128))
    rows = _pick(tm, (256, 128))
    tk = _pick(K, (1024, 2816, 1408, 1152, 1664, 512, 256, 128))
    assert not a_split or (K // 2) % tk == 0
    nk = K // tk

    def body(a_ref, b_ref, x_ref, g_ref, r_ref, o_ref, dg_ref, acc):
        i, k = pl.program_id(0), pl.program_id(1)

        @pl.when(k == 0)
        def _():
            acc[...] = jnp.zeros_like(acc)

        acc[...] += bdot.nt(a_ref[...], b_ref[...])

        @pl.when(k == nk - 1)
        def _():
            dg = jnp.zeros((1, D), F32)
            for s in range(tm // rows):
                sl = slice(s * rows, (s + 1) * rows)
                xv = x_ref[sl, :]
                rstd = lax.rsqrt(jnp.mean(xv * xv, axis=-1, keepdims=True) + EPS)
                xh = xv * rstd
                dn = acc[sl, :]
                dy = dn * g_ref[...]
                o_ref[sl, :] = (dy - xh * jnp.mean(dy * xh, axis=-1, keepdims=True)) * rstd + r_ref[sl, :]
                dg = dg + jnp.sum(dn * xh, axis=0, keepdims=True)

            @pl.when(i == 0)
            def _():
                dg_ref[...] = dg

            @pl.when(i > 0)
            def _():
                dg_ref[...] += dg

    a_spec = pl.BlockSpec((tm, tk), lambda i, k: (i, k))
    if a_split:
        per_half = K // 2 // tk
        a_spec = pl.BlockSpec((None, tm, tk), lambda i, k: (k // per_half, i, k % per_half))
    row = pl.BlockSpec((tm, D), lambda i, k: (i, 0))
    one = pl.BlockSpec((1, D), lambda i, k: (0, 0))
    return _pcall(body, name=name, out_shape=[jax.ShapeDtypeStruct((M, D), F32), jax.ShapeDtypeStruct((1, D), F32)],
                  grid=(M // tm, nk), in_specs=[a_spec, pl.BlockSpec((D, tk), lambda i, k: (0, k)), row, one, row],
                  out_specs=[row, one], scratch=[pltpu.VMEM((tm, D), F32)], sem=("arbitrary", "arbitrary"))(a, b, x, gain_row, dres)


def _row_spec(r, tile):
    if isinstance(r, tuple):
        arr, width, col = r
        return arr, pl.BlockSpec((tile, width), lambda i, col=col: (i, col))
    return r, pl.BlockSpec((tile, r.shape[1]), lambda i: (i, 0))


def _const_spec(c):
    return pl.BlockSpec(c.shape, lambda i: (0,) * c.ndim)


def rowwise(fn, rows, consts, outs, *, tile, name):
    arrs, specs = zip(*[_row_spec(r, tile) for r in rows])
    n_rows = arrs[0].shape[0]
    tile = min(tile, n_rows)
    n_in = len(rows) + len(consts)

    def body(*refs):
        res = fn(*[r[...] for r in refs[:n_in]])
        for o_ref, o in zip(refs[n_in:], res):
            o_ref[...] = o.astype(o_ref.dtype)

    arrs, specs = zip(*[_row_spec(r, tile) for r in rows])
    return _pcall(body, name=name,
                  out_shape=[jax.ShapeDtypeStruct((n_rows, w), dt) for w, dt in outs],
                  grid=(n_rows // tile,),
                  in_specs=list(specs) + [_const_spec(c) for c in consts],
                  out_specs=[pl.BlockSpec((tile, w), lambda i: (i, 0)) for w, _ in outs],
                  sem=("parallel",))(*arrs, *consts)


def rowwise_bwd(fn, rows, consts, cts, *, tile, name, row_grads, const_grads, add=None):
    arrs, _ = zip(*[_row_spec(r, tile) for r in rows])
    n_rows = arrs[0].shape[0]
    tile = min(tile, n_rows)
    arrs, specs = zip(*[_row_spec(r, tile) for r in rows])
    ct_arrs, ct_specs = zip(*[_row_spec(c, tile) for c in cts])
    add = add or {}
    add_idx = sorted(add)
    add_arrs, add_specs = (zip(*[_row_spec(add[i], tile) for i in add_idx]) if add_idx else ((), ()))
    nr, nc, nct, na = len(rows), len(consts), len(cts), len(add_idx)
    want_rows = [i for i, d in enumerate(row_grads) if d is not None]
    want_consts = [i for i, w in enumerate(const_grads) if w]

    def body(*refs):
        row_v = [r[...] for r in refs[:nr]]
        const_v = [r[...] for r in refs[nr:nr + nc]]
        ct_v = [r[...] for r in refs[nr + nc:nr + nc + nct]]
        add_v = {i: refs[nr + nc + nct + j][...] for j, i in enumerate(add_idx)}
        out_refs = refs[nr + nc + nct + na:]
        res, vjp = jax.vjp(fn, *row_v, *const_v)
        grads = vjp(tuple(c.astype(o.dtype) for c, o in zip(ct_v, res)))
        for o_ref, i in zip(out_refs, want_rows):
            g = grads[i].astype(F32)
            if i in add_v:
                g = g + add_v[i].astype(F32)
            o_ref[...] = g.astype(o_ref.dtype)
        first = pl.program_id(0) == 0
        for o_ref, i in zip(out_refs[len(want_rows):], want_consts):
            g = grads[nr + i].astype(F32)

            @pl.when(first)
            def _(o_ref=o_ref, g=g):
                o_ref[...] = g

            @pl.when(jnp.logical_not(first))
            def _(o_ref=o_ref, g=g):
                o_ref[...] += g

    def width(r):
        return r[1] if isinstance(r, tuple) else r.shape[1]

    out_shape = [jax.ShapeDtypeStruct((n_rows, width(rows[i])), row_grads[i]) for i in want_rows]
    out_shape += [jax.ShapeDtypeStruct(consts[i].shape, F32) for i in want_consts]
    out_specs = [pl.BlockSpec((tile, width(rows[i])), lambda i_: (i_, 0)) for i in want_rows]
    out_specs += [_const_spec(consts[i]) for i in want_consts]
    return _pcall(body, name=name, out_shape=out_shape, grid=(n_rows // tile,),
                  in_specs=list(specs) + [_const_spec(c) for c in consts] + list(ct_specs) + list(add_specs),
                  out_specs=out_specs, sem=("arbitrary",))(*arrs, *consts, *ct_arrs, *add_arrs)


def f_rmsnorm(x, g):
    x = x.astype(F32)
    return (x * lax.rsqrt(jnp.mean(x * x, axis=-1, keepdims=True) + EPS) * g,)


def _head_sel(first_lane):
    r, c = _iota2((LANES, GDN_WIDTH), 0), _iota2((LANES, GDN_WIDTH), 1)
    return (r == c // HEAD_DIM + first_lane).astype(F32)


def _tri(n, strict=False):
    r, c = _iota2((n, n), 0), _iota2((n, n), 1)
    return r > c if strict else r >= c


def f_gdn_pre(xc, ab, a_log, dt_bias):
    s = _silu(xc.astype(F32))
    qs, ks = [], []
    for h in range(GDN_HEADS):
        qh = s[:, h * HEAD_DIM:(h + 1) * HEAD_DIM]
        kh = s[:, GDN_WIDTH + h * HEAD_DIM:GDN_WIDTH + (h + 1) * HEAD_DIM]
        qs.append(qh * lax.rsqrt(jnp.sum(qh * qh, axis=-1, keepdims=True) + EPS) * (HEAD_DIM ** -0.5))
        ks.append(kh * lax.rsqrt(jnp.sum(kh * kh, axis=-1, keepdims=True) + EPS))
    q, k = jnp.concatenate(qs, axis=1), jnp.concatenate(ks, axis=1)
    v = s[:, 2 * GDN_WIDTH:]
    ab = ab.astype(F32)
    g = -jnp.exp(a_log) * _softplus(ab + dt_bias)
    gc = _head_broadcast(fdot(_tri(GDN_CHUNK).astype(F32), g), 0)
    beta = _head_broadcast(_sigmoid(ab), GDN_HEADS)
    return q, k, v, gc, beta


def _head_broadcast(x, first_lane):
    lane = _iota2(x.shape, 1)
    cols = [jnp.sum(jnp.where(lane == first_lane + h, x, 0.0), axis=1, keepdims=True) for h in range(GDN_HEADS)]
    return jnp.concatenate([jnp.broadcast_to(c, x.shape) for c in cols], axis=1)


def _unit_lower_inverses(neg_lowers):
    C = neg_lowers[0].shape[0]
    eye = (_iota2((C, C), 0) == _iota2((C, C), 1)).astype(F32)
    invs = [eye + n for n in neg_lowers]
    powers = list(neg_lowers)
    for _ in range(6):
        powers = [idot.nn(p, p) for p in powers]
        invs = [inv + idot.nn(p, inv) for p, inv in zip(powers, invs)]
    return invs


@jax.custom_vjp
def _solve_with_inverse(inv, neg_lower, rhs):
    return idot.nn(inv, rhs)


def _solve_fwd(inv, neg_lower, rhs):
    x = idot.nn(inv, rhs)
    return x, (inv, x)


def _solve_bwd(resid, ct):
    inv, x = resid
    d_rhs = idot.tn(inv, ct)
    return jnp.zeros_like(inv), idot.nt(d_rhs, x), d_rhs


_solve_with_inverse.defvjp(_solve_fwd, _solve_bwd)


def f_gdn_intra(q, k, v, gc, beta, inv=None):
    C = GDN_CHUNK
    causal, strict = _tri(C), _tri(C, strict=True)
    is_last = _iota2((C, HEAD_DIM), 0) == C - 1
    heads = range(GDN_HEADS)
    sls = [slice(h * HEAD_DIM, (h + 1) * HEAD_DIM) for h in heads]
    qs, ks, vs, gs, bs = ([a[:, sl] for sl in sls] for a in (q, k, v, gc, beta))
    decays = [jnp.where(causal, jnp.exp(jnp.where(causal, g - g.T, 0.0)), 0.0) for g in gs]
    kbs = [kh * bh for kh, bh in zip(ks, bs)]
    kts = [kh.T for kh in ks]
    neg_lowers = [jnp.where(strict, -(idot(kb, kt) * d), 0.0) for kb, kt, d in zip(kbs, kts, decays)]
    qks = [jnp.where(causal, idot(qh, kt) * d, 0.0) for qh, kt, d in zip(qs, kts, decays)]
    rhss = [jnp.concatenate([vh * bh, kb * jnp.exp(g)], axis=1) for vh, bh, kb, g in zip(vs, bs, kbs, gs)]
    if inv is None:
        invs = _unit_lower_inverses(neg_lowers)
        sols = [idot.nn(m, r) for m, r in zip(invs, rhss)]
    else:
        sols = [_solve_with_inverse(inv[:, sl], n, r) for sl, n, r in zip(sls, neg_lowers, rhss)]
    g_lasts = [jnp.sum(jnp.where(is_last, g, 0.0), axis=0, keepdims=True) for g in gs]
    outs = [[s[:, :HEAD_DIM] for s in sols], [s[:, HEAD_DIM:] for s in sols], qks,
            [kh * jnp.exp(gl - g) for kh, gl, g in zip(ks, g_lasts, gs)], [qh * jnp.exp(g) for qh, g in zip(qs, gs)]]
    if inv is None:
        outs.append(invs)
    return tuple(jnp.concatenate(o, axis=1) for o in outs)


def f_gdn_post(o, z, gain):
    z = z.astype(F32)
    parts = []
    for h in range(GDN_HEADS):
        oh = o[:, h * HEAD_DIM:(h + 1) * HEAD_DIM]
        parts.append(oh * lax.rsqrt(jnp.mean(oh * oh, axis=-1, keepdims=True) + EPS) * gain)
    return (jnp.concatenate(parts, axis=1) * _silu(z),)


def f_mem_attn(q, k, v):
    q = q.astype(F32)
    lane_head = _iota2((1, MEM_WIDTH), 1) // MEM_HEAD_DIM
    kt = k.astype(F32).T
    masks = [(lane_head == h).astype(F32) for h in range(MEM_HEADS)]
    logits = [bdot(q * mask, kt) * (MEM_HEAD_DIM ** -0.5) for mask in masks]
    ps = [jnp.exp(s - jnp.max(s, axis=-1, keepdims=True)) for s in logits]
    ps = [p / jnp.sum(p, axis=-1, keepdims=True) for p in ps]
    outs = [bdot(p, v) * mask for p, mask in zip(ps, masks)]
    return ((outs[0] + outs[1]) + (outs[2] + outs[3]),)


def f_loss(y, t):
    d = y - t
    return (d * d,)


def conv_fwd(proj, w8, *, width, tile=512):
    n_rows = proj.shape[0]
    tile = min(tile, n_rows)

    def body(x_ref, halo_ref, w_ref, o_ref):
        i = pl.program_id(0)
        halo = jnp.where(i > 0, halo_ref[...].astype(F32), 0.0)
        xs = jnp.concatenate([halo, x_ref[...].astype(F32)], axis=0)
        acc = xs[8:] * w_ref[3:4, :]
        for j in range(CONV_WIDTH - 1):
            acc = acc + pltpu.roll(xs, CONV_WIDTH - 1 - j, 0)[8:] * w_ref[j:j + 1, :]
        o_ref[...] = acc

    return _pcall(body, name="gdn_conv_fwd", out_shape=jax.ShapeDtypeStruct((n_rows, width), F32),
                  grid=(n_rows // tile,),
                  in_specs=[pl.BlockSpec((tile, width), lambda i: (i, 0)),
                            pl.BlockSpec((8, width), lambda i: (jnp.maximum(i * (tile // 8) - 1, 0), 0)),
                            pl.BlockSpec((8, width), lambda i: (0, 0))],
                  out_specs=pl.BlockSpec((tile, width), lambda i: (i, 0)), sem=("parallel",))(proj, proj, w8)


def conv_bwd(proj, w8, dy, *, width, tile=512):
    n_rows = proj.shape[0]
    tile = min(tile, n_rows)
    n = n_rows // tile

    def body(x_ref, xhalo_ref, w_ref, dy_ref, dyhalo_ref, dx_ref, dw_ref):
        i = pl.program_id(0)
        dy = dy_ref[...]
        after = jnp.where(i < n - 1, dyhalo_ref[...], 0.0)
        ds = jnp.concatenate([dy, after], axis=0)
        dx = dy * w_ref[3:4, :]
        for j in range(CONV_WIDTH - 1):
            shift = CONV_WIDTH - 1 - j
            dx = dx + pltpu.roll(ds, tile + 8 - shift, 0)[:tile] * w_ref[j:j + 1, :]
        dx_ref[...] = dx.astype(dx_ref.dtype)
        halo = jnp.where(i > 0, xhalo_ref[...].astype(F32), 0.0)
        xs = jnp.concatenate([halo, x_ref[...].astype(F32)], axis=0)
        rows = [jnp.sum(dy * pltpu.roll(xs, CONV_WIDTH - 1 - j, 0)[8:], axis=0, keepdims=True)
                for j in range(CONV_WIDTH - 1)]
        rows.append(jnp.sum(dy * xs[8:], axis=0, keepdims=True))
        dw = jnp.concatenate(rows + [jnp.zeros((8 - CONV_WIDTH, width), F32)], axis=0)

        @pl.when(i == 0)
        def _():
            dw_ref[...] = dw

        @pl.when(i > 0)
        def _():
            dw_ref[...] += dw

    t8 = tile // 8
    return _pcall(body, name="gdn_conv_bwd",
                  out_shape=[jax.ShapeDtypeStruct((n_rows, width), BF16), jax.ShapeDtypeStruct((8, width), F32)],
                  grid=(n,),
                  in_specs=[pl.BlockSpec((tile, width), lambda i: (i, 0)),
                            pl.BlockSpec((8, width), lambda i: (jnp.maximum(i * t8 - 1, 0), 0)),
                            pl.BlockSpec((8, width), lambda i: (0, 0)),
                            pl.BlockSpec((tile, width), lambda i: (i, 0)),
                            pl.BlockSpec((8, width), lambda i: (jnp.minimum((i + 1) * t8, n * t8 - 1), 0))],
                  out_specs=[pl.BlockSpec((tile, width), lambda i: (i, 0)), pl.BlockSpec((8, width), lambda i: (0, 0))],
                  sem=("arbitrary",))(proj, proj, w8, dy, dy)


def gdn_scan_fwd(u, w, qk, kt, qh, gc):
    n_rows = u.shape[0]
    C, n = GDN_CHUNK, u.shape[0] // GDN_CHUNK

    def body(u_ref, w_ref, qk_ref, kt_ref, qh_ref, gc_ref, o_ref, vn_ref, sin_ref, st):
        @pl.when(pl.program_id(0) == 0)
        def _():
            st[...] = jnp.zeros_like(st)

        sin_ref[0] = st[...]
        sls = [slice(h * HEAD_DIM, (h + 1) * HEAD_DIM) for h in range(GDN_HEADS)]
        states = [st[sl, :] for sl in sls]
        v_news = [u_ref[:, sl] - sdot(w_ref[:, sl], s) for sl, s in zip(sls, states)]
        from_state = [sdot(qh_ref[:, sl], s) for sl, s in zip(sls, states)]
        for sl, a, v_new in zip(sls, from_state, v_news):
            o_ref[:, sl] = a + sdot(qk_ref[:, sl], v_new)
            vn_ref[:, sl] = v_new
        for sl, s, v_new in zip(sls, states, v_news):
            st[sl, :] = s * jnp.exp(gc_ref[C - 1:C, sl]) + sdot.tn(kt_ref[:, sl], v_new)

    blk = pl.BlockSpec((C, GDN_WIDTH), lambda i: (i, 0))
    return _pcall(body, name="gdn_scan_fwd",
                  out_shape=[jax.ShapeDtypeStruct((n_rows, GDN_WIDTH), F32), jax.ShapeDtypeStruct((n_rows, GDN_WIDTH), F32),
                             jax.ShapeDtypeStruct((n, GDN_WIDTH, HEAD_DIM), F32)],
                  grid=(n,), in_specs=[blk] * 6,
                  out_specs=[blk, blk, pl.BlockSpec((1, GDN_WIDTH, HEAD_DIM), lambda i: (i, 0, 0))],
                  scratch=[pltpu.VMEM((GDN_WIDTH, HEAD_DIM), F32)], sem=("arbitrary",))(u, w, qk, kt, qh, gc)


def gdn_scan_bwd(do, w, qk, kt, qh, gc, vn, sin):
    n_rows = do.shape[0]
    C, n = GDN_CHUNK, do.shape[0] // GDN_CHUNK

    def body(do_ref, w_ref, qk_ref, kt_ref, qh_ref, gc_ref, vn_ref, sin_ref,
             du_ref, dw_ref, dqk_ref, dkt_ref, dqh_ref, dgl_ref, dst):
        @pl.when(pl.program_id(0) == 0)
        def _():
            dst[...] = jnp.zeros_like(dst)

        sls = [slice(h * HEAD_DIM, (h + 1) * HEAD_DIM) for h in range(GDN_HEADS)]
        dvns = [sdot.tn(qk_ref[:, sl], do_ref[:, sl]) + sdot(kt_ref[:, sl], dst[sl, :]) for sl in sls]
        for sl, dvn in zip(sls, dvns):
            du_ref[:, sl] = dvn
            dw_ref[:, sl] = -sdot.nt(dvn, sin_ref[0, sl, :])
        for sl in sls:
            dqk_ref[:, sl] = sdot.nt(do_ref[:, sl], vn_ref[:, sl])
            dkt_ref[:, sl] = sdot.nt(vn_ref[:, sl], dst[sl, :])
            dqh_ref[:, sl] = sdot.nt(do_ref[:, sl], sin_ref[0, sl, :])
        for sl, dvn in zip(sls, dvns):
            ds_out = dst[sl, :]
            e = jnp.exp(gc_ref[C - 1:C, sl])
            dgl = jnp.sum(ds_out * sin_ref[0, sl, :], axis=0, keepdims=True) * e
            dgl_ref[:, sl] = jnp.broadcast_to(dgl, (8, HEAD_DIM))
            dst[sl, :] = sdot.tn(qh_ref[:, sl], do_ref[:, sl]) + e * ds_out - sdot.tn(w_ref[:, sl], dvn)

    blk = pl.BlockSpec((C, GDN_WIDTH), lambda i: (n - 1 - i, 0))
    row = jax.ShapeDtypeStruct((n_rows, GDN_WIDTH), F32)
    return _pcall(body, name="gdn_scan_bwd",
                  out_shape=[row] * 5 + [jax.ShapeDtypeStruct((n * 8, GDN_WIDTH), F32)],
                  grid=(n,), in_specs=[blk] * 7 + [pl.BlockSpec((1, GDN_WIDTH, HEAD_DIM), lambda i: (n - 1 - i, 0, 0))],
                  out_specs=[blk] * 5 + [pl.BlockSpec((8, GDN_WIDTH), lambda i: (n - 1 - i, 0))],
                  scratch=[pltpu.VMEM((GDN_WIDTH, HEAD_DIM), F32)], sem=("arbitrary",))(do, w, qk, kt, qh, gc, vn, sin)


def gdn_intra_bwd(q, k, v, gc, beta, inv, cts, dgl):
    n_rows = q.shape[0]
    C = GDN_CHUNK

    def body(*refs):
        ins = [r[...] for r in refs[:5]]
        inv_v = refs[5][...]
        ct = tuple(r[...] for r in refs[6:11])
        dgl_v = refs[11][...]
        _, vjp = jax.vjp(lambda *a: f_gdn_intra(*a, inv=inv_v), *ins)
        grads = list(vjp(ct))
        last = _iota2((C, GDN_WIDTH), 0) == C - 1
        grads[3] = grads[3] + jnp.where(last, jnp.broadcast_to(dgl_v[0:1, :], (C, GDN_WIDTH)), 0.0)
        for o_ref, g in zip(refs[12:], grads):
            o_ref[...] = g

    blk = pl.BlockSpec((C, GDN_WIDTH), lambda i: (i, 0))
    return _pcall(body, name="gdn_intra_bwd", out_shape=[jax.ShapeDtypeStruct((n_rows, GDN_WIDTH), F32)] * 5,
                  grid=(n_rows // C,), in_specs=[blk] * 11 + [pl.BlockSpec((8, GDN_WIDTH), lambda i: (i, 0))],
                  out_specs=[blk] * 5, sem=("parallel",))(q, k, v, gc, beta, inv, *cts, dgl)


def fox_gate_fwd(f, b_f):
    n_rows = f.shape[0]
    T = LANES

    def body(f_ref, b_ref, cb_ref, carry):
        @pl.when(pl.program_id(0) == 0)
        def _():
            carry[...] = jnp.zeros_like(carry)

        c = fdot(_tri(T).astype(F32), _log_sigmoid(f_ref[...] + b_ref[...])) + carry[...]
        carry[...] = c[T - 1:T, :]
        cb_ref[...] = fdot(c, _head_sel(0))

    return _pcall(body, name="fox_gate_fwd", out_shape=jax.ShapeDtypeStruct((n_rows, FOX_WIDTH), F32),
                  grid=(n_rows // T,),
                  in_specs=[pl.BlockSpec((T, LANES), lambda i: (i, 0)), pl.BlockSpec((1, LANES), lambda i: (0, 0))],
                  out_specs=pl.BlockSpec((T, FOX_WIDTH), lambda i: (i, 0)),
                  scratch=[pltpu.VMEM((1, LANES), F32)], sem=("arbitrary",))(f, b_f)


def fox_gate_bwd(f, b_f, dcrow, dcb):
    n_rows = f.shape[0]
    T = LANES
    n = n_rows // T

    def body(f_ref, b_ref, dc_ref, dcb_ref, df_ref, db_ref, carry):
        i = pl.program_id(0)

        @pl.when(i == 0)
        def _():
            carry[...] = jnp.zeros_like(carry)

        rows = [dc_ref[h] for h in range(FOX_HEADS)] + [jnp.zeros((T - FOX_HEADS, T), F32)]
        first_lane = (_iota2((FOX_WIDTH, LANES), 0) == _iota2((FOX_WIDTH, LANES), 1) * HEAD_DIM).astype(F32)
        dc = jnp.concatenate(rows, axis=0).T + fdot(dcb_ref[...], first_lane)
        dlog = fdot.tn(_tri(T).astype(F32), dc) + carry[...]
        carry[...] = dlog[0:1, :]
        df = dlog * (1.0 - _sigmoid(f_ref[...] + b_ref[...]))
        df_ref[...] = df
        db = jnp.sum(df, axis=0, keepdims=True)

        @pl.when(i == 0)
        def _():
            db_ref[...] = db

        @pl.when(i > 0)
        def _():
            db_ref[...] += db

    return _pcall(body, name="fox_gate_bwd",
                  out_shape=[jax.ShapeDtypeStruct((n_rows, LANES), F32), jax.ShapeDtypeStruct((1, LANES), F32)],
                  grid=(n,),
                  in_specs=[pl.BlockSpec((T, LANES), lambda i: (n - 1 - i, 0)), pl.BlockSpec((1, LANES), lambda i: (0, 0)),
                            pl.BlockSpec((FOX_HEADS, 1, T), lambda i: (0, 0, n - 1 - i)),
                            pl.BlockSpec((T, FOX_WIDTH), lambda i: (n - 1 - i, 0))],
                  out_specs=[pl.BlockSpec((T, LANES), lambda i: (n - 1 - i, 0)), pl.BlockSpec((1, LANES), lambda i: (0, 0))],
                  scratch=[pltpu.VMEM((1, LANES), F32)], sem=("arbitrary",))(f, b_f, dcrow, dcb)


FOX_AUG = 2 * HEAD_DIM


def _fox_tiles(n_rows):
    return min(1024, n_rows), min(1024, n_rows)


def _fox_pairs(n_rows, query_major):
    tq, tk = _fox_tiles(n_rows)
    nq, r = n_rows // tq, tq // tk
    if query_major:
        pairs = [(i, j) for i in range(nq) for j in range(r * (i + 1))]
    else:
        pairs = [(i, j) for j in range(nq * r) for i in range(j // r, nq)]
    return jnp.asarray([p[0] for p in pairs], jnp.int32), jnp.asarray([p[1] for p in pairs], jnp.int32)


def fox_augment(x, cb, query_side):
    def fn(xt, ct):
        xt = xt.astype(F32)
        lane = _iota2((xt.shape[0], HEAD_DIM), 1)
        parts = []
        for h in range(FOX_HEADS):
            sl = slice(h * HEAD_DIM, (h + 1) * HEAD_DIM)
            c = ct[:, sl]
            hi = c.astype(BF16).astype(F32)
            mid = (c - hi).astype(BF16).astype(F32)
            lo = (c - hi - mid).astype(BF16).astype(F32)
            terms = jnp.where(lane % 3 == 0, hi, jnp.where(lane % 3 == 1, mid, lo))
            if query_side:
                extra = jnp.where(lane < 3, terms, jnp.where(lane < 6, 1.0, 0.0))
                parts += [xt[:, sl] * (HEAD_DIM ** -0.5), extra]
            else:
                extra = jnp.where(lane < 3, 1.0, jnp.where(lane < 6, -terms, 0.0))
                parts += [xt[:, sl], extra]
        return (jnp.concatenate(parts, axis=1),)

    return rowwise(fn, [(x, FOX_WIDTH, 0), cb], [], [(FOX_HEADS * FOX_AUG, BF16)], tile=ROW_TILE,
                   name="fox_augment_q" if query_side else "fox_augment_k")[0]


def _pcall_tables(body, *, name, out_shape, grid, tables, in_specs, out_specs, scratch, sem):
    spec = pltpu.PrefetchScalarGridSpec(num_scalar_prefetch=len(tables), grid=grid, in_specs=in_specs, out_specs=out_specs,
                                        scratch_shapes=list(scratch))
    return pl.pallas_call(body, name=name, out_shape=out_shape, grid_spec=spec,
                          compiler_params=pltpu.CompilerParams(vmem_limit_bytes=VMEM_LIMIT_BYTES, dimension_semantics=sem))


def _fox_logits(qa, ka, offset):
    s = bdot.nt(qa, ka)
    if offset is not None:
        s = jnp.where(_iota2(s.shape, 0) + offset >= _iota2(s.shape, 1), s, NEG_INF)
    return s


def _fox_p_ds(offset, qa_ref, ka_ref, v_ref, o_ref, lse_ref, do_ref):
    s = _fox_logits(qa_ref[...], ka_ref[...], offset)
    p = jnp.exp(s - jnp.tile(lse_ref[...], (1, s.shape[1] // LANES)))
    d_o = do_ref[...].astype(F32)
    delta = jnp.sum(d_o * o_ref[...].astype(F32), axis=-1, keepdims=True)
    return p, p * (bdot.nt(d_o, v_ref[...]) - delta), d_o


def _fox_on_diagonal(i, j, r, tk, step):
    @pl.when(j < r * i)
    def _():
        step(None)

    for m in range(r):
        @pl.when(j == r * i + m)
        def _(m=m):
            step(-m * tk)


def _fox_specs(tq, tk, do_col):
    qaspec = pl.BlockSpec((tq, FOX_AUG), lambda h, p, it, jt: (it[p], h))
    qspec = pl.BlockSpec((tq, HEAD_DIM), lambda h, p, it, jt: (it[p], h))
    dospec = pl.BlockSpec((tq, HEAD_DIM), lambda h, p, it, jt: (it[p], do_col + h))
    kaspec = pl.BlockSpec((tk, FOX_AUG), lambda h, p, it, jt: (jt[p], h))
    kspec = pl.BlockSpec((tk, HEAD_DIM), lambda h, p, it, jt: (jt[p], h))
    vspec = pl.BlockSpec((tk, HEAD_DIM), lambda h, p, it, jt: (jt[p], FOX_HEADS + h))
    cspec = pl.BlockSpec((1, 1, tk), lambda h, p, it, jt: (h, 0, jt[p]))
    return qaspec, qspec, dospec, kaspec, kspec, vspec, cspec


def fox_fwd(qa, kv, ka):
    n_rows = kv.shape[0]
    tq, tk = _fox_tiles(n_rows)
    r = tq // tk
    tables = _fox_pairs(n_rows, True)

    def body(it, jt, qa_ref, ka_ref, v_ref, o_ref, lse_ref, m_sc, l_sc, acc):
        i, j = it[pl.program_id(1)], jt[pl.program_id(1)]

        @pl.when(j == 0)
        def _():
            m_sc[...] = jnp.full(m_sc.shape, NEG_INF, F32)
            l_sc[...] = jnp.zeros_like(l_sc)
            acc[...] = jnp.zeros_like(acc)

        def step(offset):
            s = _fox_logits(qa_ref[...], ka_ref[...], offset)
            m_old = m_sc[...]
            m_new = jnp.maximum(m_old, jnp.max(s, axis=-1, keepdims=True))
            alpha = jnp.exp(m_old - m_new)
            p = jnp.exp(s - jnp.tile(m_new, (1, tk // LANES)))
            l_sc[...] = l_sc[...] * alpha + jnp.sum(p, axis=-1, keepdims=True)
            acc[...] = acc[...] * alpha + bdot(p, v_ref[...])
            m_sc[...] = m_new

        _fox_on_diagonal(i, j, r, tk, step)

        @pl.when(j == r * i + r - 1)
        def _():
            o_ref[...] = (acc[...] / l_sc[...]).astype(o_ref.dtype)
            lse_ref[...] = m_sc[...] + jnp.log(l_sc[...])

    qaspec, qspec, _, kaspec, _, vspec, _ = _fox_specs(tq, tk, 0)
    return _pcall_tables(body, name="fox_fwd",
                         out_shape=[jax.ShapeDtypeStruct((n_rows, FOX_WIDTH), BF16), jax.ShapeDtypeStruct((n_rows, FOX_WIDTH), F32)],
                         grid=(FOX_HEADS, tables[0].shape[0]), tables=tables,
                         in_specs=[qaspec, kaspec, vspec], out_specs=[qspec, qspec],
                         scratch=[pltpu.VMEM((tq, HEAD_DIM), F32)] * 3, sem=("parallel", "arbitrary"))(*tables, qa, ka, kv)


def fox_bwd_dq(qa, kv, ka, o, lse, do, prev=None):
    do, _, do_col = do
    do_col *= FOX_HEADS
    n_rows = kv.shape[0]
    tq, tk = _fox_tiles(n_rows)
    r = tq // tk
    n_prev = 0 if prev is None else 1
    tables = _fox_pairs(n_rows, True)

    def body(it, jt, qa_ref, ka_ref, k_ref, v_ref, o_ref, lse_ref, do_ref, *rest):
        dq_ref, drow_ref, acc, rows = rest[n_prev:]
        i, j = it[pl.program_id(1)], jt[pl.program_id(1)]

        @pl.when(j == 0)
        def _():
            acc[...] = jnp.zeros_like(acc)
            rows[...] = jnp.zeros_like(rows)

        def step(offset):
            _, ds, _ = _fox_p_ds(offset, qa_ref, ka_ref, v_ref, o_ref, lse_ref, do_ref)
            acc[...] += bdot(ds, k_ref[...])
            rows[...] += jnp.sum(ds, axis=-1, keepdims=True)

        _fox_on_diagonal(i, j, r, tk, step)

        @pl.when(j == r * i + r - 1)
        def _():
            dq_ref[...] = (acc[...] * (HEAD_DIM ** -0.5)).astype(dq_ref.dtype)
            drow_ref[...] = rows[...] + rest[0][...] if n_prev else rows[...]

    qaspec, qspec, dospec, kaspec, kspec, vspec, _ = _fox_specs(tq, tk, do_col)
    return _pcall_tables(body, name="fox_bwd_dq" + ("_acc" if n_prev else ""),
                         out_shape=[jax.ShapeDtypeStruct((n_rows, FOX_WIDTH), BF16), jax.ShapeDtypeStruct((n_rows, FOX_WIDTH), F32)],
                         grid=(FOX_HEADS, tables[0].shape[0]), tables=tables,
                         in_specs=[qaspec, kaspec, kspec, vspec, qspec, qspec, dospec] + [qspec] * n_prev,
                         out_specs=[qspec, qspec], scratch=[pltpu.VMEM((tq, HEAD_DIM), F32)] * 2,
                         sem=("parallel", "arbitrary"))(*tables, qa, ka, kv, kv, o, lse, do, *([prev] if n_prev else []))


def fox_bwd_dkv(qa, kv, ka, o, lse, do, prev=None):
    do, _, do_col = do
    do_col *= FOX_HEADS
    n_rows = kv.shape[0]
    tq, tk = _fox_tiles(n_rows)
    nq, r = n_rows // tq, tq // tk
    n_prev = 0 if prev is None else 3
    tables = _fox_pairs(n_rows, False)

    def body(it, jt, qa_ref, ka_ref, v_ref, o_ref, lse_ref, do_ref, *rest):
        prev_refs = rest[:n_prev]
        dk_ref, dv_ref, dc_ref, dk_acc, dv_acc, dc_acc = rest[n_prev:]
        i, j = it[pl.program_id(1)], jt[pl.program_id(1)]

        @pl.when(j >= r * i)
        def _():
            dk_acc[...] = jnp.zeros_like(dk_acc)
            dv_acc[...] = jnp.zeros_like(dv_acc)
            dc_acc[...] = jnp.zeros_like(dc_acc)

        def step(offset):
            p, ds, d_o = _fox_p_ds(offset, qa_ref, ka_ref, v_ref, o_ref, lse_ref, do_ref)
            dv_acc[...] += bdot.tn(p, d_o)
            dk_acc[...] += bdot.tn(ds, qa_ref[:, :HEAD_DIM])
            dc_acc[...] -= jnp.sum(ds, axis=0, keepdims=True)

        _fox_on_diagonal(i, j, r, tk, step)

        @pl.when(i == nq - 1)
        def _():
            dk, dv, dc = dk_acc[...], dv_acc[...], dc_acc[...]
            if n_prev:
                dk, dv, dc = dk + prev_refs[0][...], dv + prev_refs[1][...], dc + prev_refs[2][0]
            dk_ref[...] = dk
            dv_ref[...] = dv
            dc_ref[0] = dc

    qaspec, qspec, dospec, kaspec, kspec, vspec, cspec = _fox_specs(tq, tk, do_col)
    return _pcall_tables(body, name="fox_bwd_dkv" + ("_acc" if n_prev else ""),
                         out_shape=[jax.ShapeDtypeStruct((n_rows, FOX_WIDTH), F32), jax.ShapeDtypeStruct((n_rows, FOX_WIDTH), F32),
                                    jax.ShapeDtypeStruct((FOX_HEADS, 1, n_rows), F32)],
                         grid=(FOX_HEADS, tables[0].shape[0]), tables=tables,
                         in_specs=[qaspec, kaspec, vspec, qspec, qspec, dospec] + [kspec, kspec, cspec][:n_prev],
                         out_specs=[kspec, kspec, cspec],
                         scratch=[pltpu.VMEM((tk, HEAD_DIM), F32), pltpu.VMEM((tk, HEAD_DIM), F32), pltpu.VMEM((1, tk), F32)],
                         sem=("parallel", "arbitrary"))(*tables, qa, ka, kv, o, lse, do, *(prev or ()))


def loss_head(h, gain, target, *, tile=512):
    n_rows, d = h.shape
    tile = min(tile, n_rows)

    def body(h_ref, g_ref, t_ref, part_ref, dy_ref):
        (y,) = f_rmsnorm(h_ref[...], g_ref[...])
        diff = y - t_ref[...]
        dy_ref[...] = diff * (1.0 / d)
        part = jnp.sum(diff * diff, axis=0, keepdims=True)
        first = pl.program_id(0) == 0

        @pl.when(first)
        def _():
            part_ref[...] = part

        @pl.when(jnp.logical_not(first))
        def _():
            part_ref[...] += part

    blk = pl.BlockSpec((tile, d), lambda i: (i, 0))
    one = pl.BlockSpec((1, d), lambda i: (0, 0))
    return _pcall(body, name="loss_head",
                  out_shape=[jax.ShapeDtypeStruct((1, d), F32), jax.ShapeDtypeStruct((n_rows, d), F32)],
                  grid=(n_rows // tile,), in_specs=[blk, one, blk], out_specs=[one, blk], sem=("arbitrary",))(h, gain, target)


def adamw(w, g, m, v, *, name):
    shape = w.shape
    cols = shape[-1] if w.ndim >= 2 else w.size
    rows = w.size // cols
    tile = _pick(rows, (256, 128, 64, 32, 16, 8))
    as2d = lambda a: a.reshape(rows, cols)

    def body(w_ref, g_ref, m_ref, v_ref, d_ref, nm_ref, nv_ref):
        g_ = g_ref[...]
        m_ = ADAM_B1 * m_ref[...] + (1.0 - ADAM_B1) * g_
        v_ = ADAM_B2 * v_ref[...] + (1.0 - ADAM_B2) * (g_ * g_)
        m_hat = m_ / (1.0 - ADAM_B1 ** ADAM_STEP)
        v_hat = v_ / (1.0 - ADAM_B2 ** ADAM_STEP)
        d_ref[...] = -ADAM_LR * (m_hat / (jnp.sqrt(v_hat) + ADAM_EPS) + ADAM_WD * w_ref[...])
        nm_ref[...] = m_
        nv_ref[...] = v_

    blk = pl.BlockSpec((tile, cols), lambda i: (i, 0))
    outs = _pcall(body, name=name, out_shape=[jax.ShapeDtypeStruct((rows, cols), F32)] * 3, grid=(rows // tile,),
                  in_specs=[blk] * 4, out_specs=[blk] * 3, sem=("parallel",))(as2d(w), as2d(g), as2d(m), as2d(v))
    return tuple(o.reshape(shape) for o in outs)


def sum_leading(a, *, name):
    p, r, c = a.shape
    tile = _pick(r, (256, 128, 64, 32, 16, 8))

    def body(a_ref, o_ref):
        total = a_ref[0].astype(F32)
        for k in range(1, p):
            total = total + a_ref[k].astype(F32)
        o_ref[...] = total

    return _pcall(body, name=name, out_shape=jax.ShapeDtypeStruct((r, c), F32), grid=(r // tile,),
                  in_specs=[pl.BlockSpec((p, tile, c), lambda i: (0, i, 0))],
                  out_specs=pl.BlockSpec((tile, c), lambda i: (i, 0)), sem=("parallel",))(a)


_HBM = pl.BlockSpec(memory_space=pltpu.HBM)


def _comm_call(body, *, name, out_shape, n_in, scratch):
    return pl.pallas_call(body, name=name, out_shape=out_shape, in_specs=[_HBM] * n_in, out_specs=_HBM,
                          scratch_shapes=scratch,
                          compiler_params=pltpu.CompilerParams(has_side_effects=True))


def all_gather8(a, *, name):
    m_per, n = a.shape

    def body(x_ref, out_ref, send_sems, recv_sems, local_sem):
        x, y, c = lax.axis_index("x"), lax.axis_index("y"), lax.axis_index("c")
        me, sibling = (x, y, c), (x, y, 1 - c)
        chips = [(1 - x, y), (x, 1 - y), (1 - x, 1 - y)]

        def rows(px, py, pc):
            return out_ref.at[pl.ds((4 * px + 2 * py + pc) * m_per, m_per), :]

        def copy(k, block, to, src=None):
            return pltpu.make_async_remote_copy(
                src_ref=rows(*block) if src is None else src, dst_ref=rows(*block),
                send_sem=send_sems.at[k], recv_sem=recv_sems.at[k], device_id=to, device_id_type=MESH)

        mine = pltpu.make_async_copy(x_ref, rows(*me), local_sem)
        mine.start()
        first = [copy(0, me, sibling, src=x_ref)]
        first += [copy(1 + j, me, (*chip, c), src=x_ref) for j, chip in enumerate(chips)]
        for cp in first:
            cp.start()
        passed = [copy(4 + j, (*chip, c), sibling) for j, chip in enumerate(chips)]
        for j, chip in enumerate(chips):
            copy(1 + j, (*chip, c), me).wait_recv()
            passed[j].start()
        copy(0, sibling, me).wait_recv()
        for j, chip in enumerate(chips):
            copy(4 + j, (*chip, 1 - c), me).wait_recv()
        for cp in first + passed:
            cp.wait_send()
        mine.wait()

    return _comm_call(body, name=name, out_shape=jax.ShapeDtypeStruct((N_DEV * m_per, n), a.dtype), n_in=1,
                      scratch=[pltpu.SemaphoreType.DMA((7,)), pltpu.SemaphoreType.DMA((7,)), pltpu.SemaphoreType.DMA])(a)


PACK_COLS = 1024
PACK_ROW_MULTIPLE = 32

WEIGHT_NAMES = ["ffn1_norm", "ffn1_w_gate_up", "ffn1_w_down", "mix_norm", "ffn2_norm", "ffn2_w_gate_up", "ffn2_w_down",
                "gdn_w_in", "gdn_conv", "gdn_A_log", "gdn_dt_bias", "gdn_out_norm", "fox_w_in", "w_out", "mem_norm",
                "mem_w_kv", "kv_norm", "kv_w", "kv_b_f", "final_norm"]
SHARDED = [("ffn1_w_gate_up", 2), ("ffn1_w_down", 1), ("ffn2_w_gate_up", 2), ("ffn2_w_down", 1), ("gdn_w_in", 2),
           ("gdn_conv", 2), ("fox_w_in", 1), ("w_out", 1), ("mem_w_kv", 1), ("kv_w", 0)]
REPLICATED = [n for n in WEIGHT_NAMES if n not in dict(SHARDED)]


PACK_PIECE_ROWS = 16


def _rows_of(size):
    return -(-size // (PACK_COLS * PACK_PIECE_ROWS)) * PACK_PIECE_ROWS


def pack(pieces, dtype, row_multiple=PACK_ROW_MULTIPLE):
    bufs, total = [], 0
    for p in pieces:
        flat = p.astype(dtype).reshape(-1)
        rows = _rows_of(flat.size)
        bufs.append(jnp.pad(flat, (0, rows * PACK_COLS - flat.size)).reshape(rows, PACK_COLS))
        total += rows
    pad = -total % row_multiple
    if pad:
        bufs.append(jnp.zeros((pad, PACK_COLS), dtype))
    return jnp.concatenate(bufs, axis=0)


def unpack(buf, shapes):
    out, row = [], 0
    for shape in shapes:
        size = 1
        for s in shape:
            size *= s
        rows = _rows_of(size)
        out.append(buf[row:row + rows].reshape(-1)[:size].reshape(shape))
        row += rows
    return out


def _row(vec, width=None):
    vec = vec.astype(F32).reshape(1, -1)
    if width is not None and vec.shape[1] < width:
        vec = jnp.pad(vec, ((0, 0), (0, width - vec.shape[1])))
    return vec


ROW_TILE = 512
GDN_PROJ_WIDTH = 4 * GDN_WIDTH + MEM_WIDTH + LANES
GDN_Z_COL, GDN_QMEM_COL, GDN_AB_COL = 3, 4 * GDN_WIDTH // MEM_WIDTH, (4 * GDN_WIDTH + MEM_WIDTH) // LANES
FOX_QMEM_COL = FOX_WIDTH // MEM_WIDTH
KV_PAD_WIDTH = 2 * FOX_WIDTH + LANES


def rms_fwd(x, gain_row, out_dtype=BF16):
    return rowwise(f_rmsnorm, [x], [gain_row], [(x.shape[1], out_dtype)], tile=ROW_TILE, name="rms_fwd")[0]


def rms_bwd(x, gain_row, dy, dres=None):
    return rowwise_bwd(f_rmsnorm, [x], [gain_row], [dy], tile=ROW_TILE, name="rms_bwd", row_grads=[F32],
                       const_grads=[True], add=None if dres is None else {0: dres})


def _ffn_tiles(n_rows):
    return _pick(n_rows, (512, 256, 128)), _pick(FFN_HIDDEN, (1408, 256, 128))


def ffn_up_act(n, wgu):
    n_rows, d = n.shape
    tm, tn = _ffn_tiles(n_rows)
    nj = FFN_HIDDEN // tn

    def body(n_ref, wg_ref, wu_ref, gu_ref, act_ref):
        x = n_ref[...].astype(BF16)
        g = bdot.nn(x, wg_ref[...])
        u = bdot.nn(x, wu_ref[...])
        gu_ref[0] = g.astype(gu_ref.dtype)
        gu_ref[1] = u.astype(gu_ref.dtype)
        act_ref[...] = (_silu(g) * u).astype(act_ref.dtype)

    return _pcall(body, name="ffn_up_act",
                  out_shape=[jax.ShapeDtypeStruct((2, n_rows, FFN_HIDDEN), BF16), jax.ShapeDtypeStruct((n_rows, FFN_HIDDEN), BF16)],
                  grid=(nj, n_rows // tm),
                  in_specs=[pl.BlockSpec((tm, d), lambda j, i: (i, 0)), pl.BlockSpec((d, tn), lambda j, i: (0, j)),
                            pl.BlockSpec((d, tn), lambda j, i: (0, nj + j))],
                  out_specs=[pl.BlockSpec((2, tm, tn), lambda j, i: (0, i, j)), pl.BlockSpec((tm, tn), lambda j, i: (i, j))],
                  sem=("parallel", "parallel"))(n, wgu, wgu)


def ffn_down_dx_act(dh, wd, gu):
    n_rows, d = dh.shape
    tm, tn = _ffn_tiles(n_rows)

    def body(dh_ref, wd_ref, gu_ref, dgu_ref):
        dact = 0.5 * bdot.nt(dh_ref[...], wd_ref[...])
        gate, up = gu_ref[0].astype(F32), gu_ref[1].astype(F32)
        sg = _sigmoid(gate)
        dgu_ref[0] = (dact * up * (sg * (1.0 + gate * (1.0 - sg)))).astype(dgu_ref.dtype)
        dgu_ref[1] = (dact * (gate * sg)).astype(dgu_ref.dtype)

    blk = pl.BlockSpec((2, tm, tn), lambda j, i: (0, i, j))
    return _pcall(body, name="ffn_down_dx_act", out_shape=jax.ShapeDtypeStruct((2, n_rows, FFN_HIDDEN), BF16),
                  grid=(FFN_HIDDEN // tn, n_rows // tm),
                  in_specs=[pl.BlockSpec((tm, d), lambda j, i: (i, 0)), pl.BlockSpec((tn, d), lambda j, i: (j, 0)), blk],
                  out_specs=blk, sem=("parallel", "parallel"))(dh, wd, gu)


def ffn_fwd(h, gain_row, wgu, wd):
    n = rms_fwd(h, gain_row)
    gu, act = ffn_up_act(n, wgu)
    return mm(act, wd, scale=0.5, res=h, name="ffn_down"), (h, n, gu, act)


def ffn_bwd(dh, saved, gain_row, wgu, wd):
    h, n, gu, act = saved
    dgu = ffn_down_dx_act(dh, wd, gu)
    dwd = mm(act, dh, ta=True, scale=0.5, name="ffn_down_dw")
    dwgu = mm(n, dgu, ta=True, b_split=True, name="ffn_up_dw")
    dh, dgain = mm_rms_bwd(dgu, wgu, h, gain_row, dh, a_split=True, name="ffn_up_dx")
    return dh, dwgu, dwd, dgain


def gdn_fwd(proj, w8, a_row, dt_row, onorm_row):
    wide = [(GDN_WIDTH, F32)] * 5
    xc = conv_fwd(proj, w8, width=3 * GDN_WIDTH)
    q, k, v, gc, beta = rowwise(f_gdn_pre, [xc, (proj, LANES, GDN_AB_COL)], [a_row, dt_row], wide, tile=GDN_CHUNK,
                                name="gdn_pre_fwd")
    u, w, qk, kt, qh, inv = rowwise(f_gdn_intra, [q, k, v, gc, beta], [], wide + wide[:1], tile=GDN_CHUNK,
                                    name="gdn_intra_fwd")
    o, vn, sin = gdn_scan_fwd(u, w, qk, kt, qh, gc)
    main = rowwise(f_gdn_post, [o, (proj, GDN_WIDTH, GDN_Z_COL)], [onorm_row], [(GDN_WIDTH, BF16)], tile=ROW_TILE,
                   name="gdn_post_fwd")[0]
    return main, (xc, q, k, v, gc, beta, inv, w, qk, kt, qh, vn, sin, o)


def gdn_bwd(dmain, proj, saved, w8, a_row, dt_row, onorm_row):
    xc, q, k, v, gc, beta, inv, w, qk, kt, qh, vn, sin, o = saved
    do, dz, donorm = rowwise_bwd(f_gdn_post, [o, (proj, GDN_WIDTH, GDN_Z_COL)], [onorm_row], [dmain], tile=ROW_TILE,
                                 name="gdn_post_bwd", row_grads=[F32, BF16], const_grads=[True])
    du, dw, dqk, dkt, dqh, dgl = gdn_scan_bwd(do, w, qk, kt, qh, gc, vn, sin)
    dq, dk, dv, dgc, dbeta = gdn_intra_bwd(q, k, v, gc, beta, inv, (du, dw, dqk, dkt, dqh), dgl)
    dxc, dab, da, ddt = rowwise_bwd(f_gdn_pre, [xc, (proj, LANES, GDN_AB_COL)], [a_row, dt_row], [dq, dk, dv, dgc, dbeta],
                                    tile=GDN_CHUNK, name="gdn_pre_bwd", row_grads=[F32, BF16], const_grads=[True, True])
    dqkv, dw8 = conv_bwd(proj, w8, dxc, width=3 * GDN_WIDTH)
    return dqkv, dz, dab, dw8, da, ddt, donorm


def mem_fwd(q, kmem, vmem):
    return rowwise(f_mem_attn, [q], [kmem, vmem], [(MEM_WIDTH, BF16)], tile=ROW_TILE, name="mem_attn_fwd")[0]


def mem_bwd(q, kmem, vmem, dout):
    return rowwise_bwd(f_mem_attn, [q], [kmem, vmem], [dout], tile=ROW_TILE, name="mem_attn_bwd", row_grads=[BF16],
                       const_grads=[True, True])


def forward_backward(xs, mems, target, P):
    depth, n_a = 4, 2
    G = {}
    mem_gain = _row(P["mem_norm"])
    mem_n = rms_fwd(mems, mem_gain)
    h = xs
    saved = []
    shared = None
    for l in range(depth):
        h0 = h
        h1, s1 = ffn_fwd(h0, _row(P["ffn1_norm"][l]), P["ffn1_w_gate_up"][l], P["ffn1_w_down"][l])
        u = rms_fwd(h1, _row(P["mix_norm"][l]))
        kvm = mm(mem_n, P["mem_w_kv"][l], name="mem_kv")
        kmem, vmem = kvm[:, :MEM_WIDTH], kvm[:, MEM_WIDTH:]
        if l < n_a:
            gp = (P["conv8"][l], _row(P["gdn_A_log"][l], LANES), _row(P["gdn_dt_bias"][l], LANES), _row(P["gdn_out_norm"][l]))
            proj = mm(u, P["gdn_w_in_pad"][l], name="gdn_in")
            main, sm = gdn_fwd(proj, *gp)
            qm = (proj, MEM_WIDTH, GDN_QMEM_COL)
        else:
            proj = mm(u, P["fox_w_in"][l - n_a], out_dtype=BF16, name="fox_in")
            kv, ka, cb = shared
            qa = fox_augment(proj, cb, True)
            main, lse = fox_fwd(qa, kv, ka)
            sm = (main, lse, qa)
            qm = (proj, MEM_WIDTH, FOX_QMEM_COL)
        mo = mem_fwd(qm, kmem, vmem)
        cat = jnp.concatenate([main, mo], axis=1)
        h2 = mm(cat, P["w_out"][l], res=h1, name="mix_out")
        h3, s2 = ffn_fwd(h2, _row(P["ffn2_norm"][l]), P["ffn2_w_gate_up"][l], P["ffn2_w_down"][l])
        saved.append((s1, h1, u, kmem, vmem, proj, sm, qm, cat, s2))
        h = h3
        if l == n_a - 1:
            nkv = rms_fwd(h, _row(P["kv_norm"]))
            kv = mm(nkv, P["kv_w_pad"][:, :2 * FOX_WIDTH], out_dtype=BF16, name="fox_kv")
            f = mm(nkv, P["kv_w_pad"][:, 2 * FOX_WIDTH:], name="fox_f")
            bf_row = _row(P["kv_b_f"], LANES)
            cb = fox_gate_fwd(f, bf_row)
            shared = (kv, fox_augment(kv, cb, False), cb)
            kv_saved = (h, nkv, f, bf_row)

    part, dy = loss_head(h, _row(P["final_norm"]), target)
    dh, G["final_norm"] = rms_bwd(h, _row(P["final_norm"]), dy)

    per_layer = {n: [None] * depth for n in ("ffn1_norm", "ffn1_w_gate_up", "ffn1_w_down", "mix_norm", "ffn2_norm",
                                             "ffn2_w_gate_up", "ffn2_w_down", "w_out", "mem_w_kv")}
    gdn_g = {n: [None] * n_a for n in ("gdn_w_in_pad", "conv8", "gdn_A_log", "gdn_dt_bias", "gdn_out_norm")}
    fox_g = [None] * (depth - n_a)
    dmem_n = None
    dkv_acc = dcb_acc = None
    for l in reversed(range(depth)):
        s1, h1, u, kmem, vmem, proj, sm, qm, cat, s2 = saved[l]
        if l == n_a - 1:
            hk, nkv, f, bf_row = kv_saved
            dk, dv, dcrow = dkv_acc
            df, dbf = fox_gate_bwd(f, bf_row, dcrow, dcb_acc)
            dp = jnp.concatenate([dk.astype(BF16), dv.astype(BF16), df.astype(BF16)], axis=1)
            G["kv_w_pad"] = mm(nkv, dp, ta=True, name="fox_kv_dw")
            G["kv_b_f"] = dbf
            dh, G["kv_norm"] = mm_rms_bwd(dp, P["kv_w_pad"], hk, _row(P["kv_norm"]), dh, name="fox_kv_dx")
        dh, per_layer["ffn2_w_gate_up"][l], per_layer["ffn2_w_down"][l], per_layer["ffn2_norm"][l] = ffn_bwd(
            dh, s2, _row(P["ffn2_norm"][l]), P["ffn2_w_gate_up"][l], P["ffn2_w_down"][l])
        dcat = mm(dh, P["w_out"][l], tb=True, out_dtype=BF16, name="mix_out_dx")
        per_layer["w_out"][l] = mm(cat, dh, ta=True, name="mix_out_dw")
        dqm, dkm, dvm = mem_bwd(qm, kmem, vmem, (dcat, MEM_WIDTH, FOX_QMEM_COL))
        dkvm = jnp.concatenate([dkm, dvm], axis=1)
        per_layer["mem_w_kv"][l] = mm(mem_n, dkvm, ta=True, name="mem_kv_dw")
        dmem_n = mm(dkvm, P["mem_w_kv"][l], tb=True, res=dmem_n, name="mem_kv_dx")
        dmain = (dcat, GDN_WIDTH, 0)
        if l < n_a:
            gp = (P["conv8"][l], _row(P["gdn_A_log"][l], LANES), _row(P["gdn_dt_bias"][l], LANES), _row(P["gdn_out_norm"][l]))
            dqkv, dz, dab, gdn_g["conv8"][l], gdn_g["gdn_A_log"][l], gdn_g["gdn_dt_bias"][l], gdn_g["gdn_out_norm"][l] = gdn_bwd(
                dmain, proj, sm, *gp)
            dproj = jnp.concatenate([dqkv, dz, dqm, dab], axis=1)
            gdn_g["gdn_w_in_pad"][l] = mm(u, dproj, ta=True, name="gdn_in_dw")
            w_in = P["gdn_w_in_pad"][l]
        else:
            o, lse, qa = sm
            kv, ka, _ = shared
            dq, dcb_acc = fox_bwd_dq(qa, kv, ka, o, lse, dmain, dcb_acc)
            dkv_acc = fox_bwd_dkv(qa, kv, ka, o, lse, dmain, dkv_acc)
            dproj = jnp.concatenate([dq, dqm], axis=1)
            fox_g[l - n_a] = mm(u, dproj, ta=True, name="fox_in_dw")
            w_in = P["fox_w_in"][l - n_a]
        dh, per_layer["mix_norm"][l] = mm_rms_bwd(dproj, w_in, h1, _row(P["mix_norm"][l]), dh, name="mix_in_dx")
        dh, per_layer["ffn1_w_gate_up"][l], per_layer["ffn1_w_down"][l], per_layer["ffn1_norm"][l] = ffn_bwd(
            dh, s1, _row(P["ffn1_norm"][l]), P["ffn1_w_gate_up"][l], P["ffn1_w_down"][l])

    (G["mem_norm"],) = rowwise_bwd(f_rmsnorm, [mems], [mem_gain], [dmem_n], tile=ROW_TILE, name="mem_norm_bwd",
                                   row_grads=[None], const_grads=[True])
    for n, v in per_layer.items():
        G[n] = jnp.stack(v)
    for n, v in gdn_g.items():
        G[n] = jnp.stack(v)
    G["fox_w_in"] = jnp.stack(fox_g)
    return part, dh, G


_GDN_O0 = 4 * GDN_WIDTH
_GDN_O1 = _GDN_O0 + 2 * GDN_HEADS
_KV_WIDTH = 2 * FOX_WIDTH + FOX_HEADS


def derived_weights(gdn_w_in, gdn_conv, kv_w=None):
    zeros = jnp.zeros(gdn_w_in.shape[:-1] + (LANES - 2 * GDN_HEADS,), gdn_w_in.dtype)
    out = dict(
        gdn_w_in_pad=jnp.concatenate([gdn_w_in[..., :_GDN_O0], gdn_w_in[..., _GDN_O1:], gdn_w_in[..., _GDN_O0:_GDN_O1], zeros], axis=-1),
        conv8=jnp.pad(gdn_conv.astype(F32), ((0, 0), (0, 8 - CONV_WIDTH), (0, 0))))
    if kv_w is not None:
        out["kv_w_pad"] = jnp.pad(kv_w, ((0, 0), (0, KV_PAD_WIDTH - _KV_WIDTH)))
    return out


def reference_layout(G):
    gp = G["gdn_w_in_pad"]
    out = dict(G)
    out["gdn_w_in"] = jnp.concatenate([gp[..., :_GDN_O0], gp[..., _GDN_O0 + MEM_WIDTH:_GDN_O0 + MEM_WIDTH + 2 * GDN_HEADS],
                                       gp[..., _GDN_O0:_GDN_O0 + MEM_WIDTH]], axis=-1)
    out["gdn_conv"] = G["conv8"][:, :CONV_WIDTH]
    out["kv_w"] = G["kv_w_pad"][:, :_KV_WIDTH]
    out["gdn_A_log"] = G["gdn_A_log"][:, 0, :GDN_HEADS]
    out["gdn_dt_bias"] = G["gdn_dt_bias"][:, 0, :GDN_HEADS]
    out["gdn_out_norm"] = G["gdn_out_norm"][:, 0, :]
    out["kv_b_f"] = G["kv_b_f"][0, :FOX_HEADS]
    for n in ("ffn1_norm", "mix_norm", "ffn2_norm"):
        out[n] = G[n][:, 0, :]
    for n in ("mem_norm", "kv_norm", "final_norm"):
        out[n] = G[n][0]
    return {n: out[n] for n in WEIGHT_NAMES}


EXCHANGED = [("ffn1_w_gate_up", 2, 0), ("ffn1_w_down", 1, 0), ("ffn2_w_gate_up", 2, 0), ("ffn2_w_down", 1, 0),
             ("gdn_w_in", 2, 0), ("fox_w_in", 1, 0), ("w_out", 1, 0), ("mem_w_kv", 1, 0), ("kv_w", 0, 1)]
GDN_IN_SHARD = (4 * GDN_WIDTH + 2 * GDN_HEADS + MEM_WIDTH) // N_CHIPS
GDN_IN_SLOT = 896


def _slab(ref, axis_slices):
    idx = [slice(None)] * len(ref.shape)
    for axis, (start, size) in axis_slices.items():
        idx[axis] = pl.ds(start, size)
    return ref.at[tuple(idx)]


def _comm_multi(body, *, name, n_in, out_shapes, scratch):
    return pl.pallas_call(body, name=name, out_shape=out_shapes, in_specs=[_HBM] * n_in, out_specs=[_HBM] * len(out_shapes),
                          scratch_shapes=scratch, compiler_params=pltpu.CompilerParams(has_side_effects=True))


def gather_shards(shards, layout):
    n = len(shards)
    fulls = [tuple(d * (N_CHIPS if a == sa else 1) for a, d in enumerate(s.shape)) for s, (sa, _) in zip(shards, layout)]
    n_sem = 9

    def pieces(w):
        sa, ha = layout[w]
        axis = 3 - sa - ha
        size = shards[w].shape[axis]
        unit = LANES if axis == 2 else 16
        first = -(-(size // 2) // unit) * unit
        return axis, [(0, first), (first, size - first)]

    def body(*refs):
        ins, outs = refs[:n], refs[n:2 * n]
        send_sems, recv_sems, local_sems = refs[2 * n:]
        x, y, c = lax.axis_index("x"), lax.axis_index("y"), lax.axis_index("c")
        me, sibling, x_nbr, y_nbr = (x, y, c), (x, y, 1 - c), (1 - x, y, c), (x, 1 - y, c)
        chip_x, chip_y, chip_d = (1 - x, y), (x, 1 - y), (1 - x, 1 - y)

        def region(w, chip, pc, piece=None):
            (sa, ha), shard = layout[w], shards[w].shape
            where = {sa: ((2 * chip[0] + chip[1]) * shard[sa], shard[sa]), ha: (pc * (shard[ha] // 2), shard[ha] // 2)}
            if piece is not None:
                axis, parts = pieces(w)
                where[axis] = parts[piece]
            return _slab(outs[w], where)

        def my_half(w):
            ha, shard = layout[w][1], shards[w].shape
            return _slab(ins[w], {ha: (c * (shard[ha] // 2), shard[ha] // 2)})

        def copy(w, k, where, to, src=None):
            return pltpu.make_async_remote_copy(
                src_ref=where if src is None else src, dst_ref=where, send_sem=send_sems.at[n_sem * w + k],
                recv_sem=recv_sems.at[n_sem * w + k], device_id=to, device_id_type=MESH)

        mine, sends = [], [[] for _ in range(n)]
        for w in range(n):
            mine.append(pltpu.make_async_copy(my_half(w), region(w, (x, y), c), local_sems.at[w]))
            mine[w].start()
            sends[w] = [copy(w, k, region(w, (x, y), c), to, src=my_half(w)) for k, to in enumerate((sibling, x_nbr, y_nbr))]
            for cp in sends[w]:
                cp.start()
        for w in range(n):
            copy(w, 1, region(w, chip_x, c), me).wait_recv()
            onward = [copy(w, 3, region(w, chip_x, c, 0), y_nbr), copy(w, 5, region(w, chip_x, c), sibling)]
            for cp in onward:
                cp.start()
            sends[w] += onward
            copy(w, 2, region(w, chip_y, c), me).wait_recv()
            onward = [copy(w, 4, region(w, chip_y, c, 1), x_nbr), copy(w, 6, region(w, chip_y, c), sibling)]
            for cp in onward:
                cp.start()
            sends[w] += onward
        for w in range(n):
            copy(w, 3, region(w, chip_d, c, 0), me).wait_recv()
            copy(w, 4, region(w, chip_d, c, 1), me).wait_recv()
            onward = [copy(w, 7, region(w, chip_d, c, 0), sibling), copy(w, 8, region(w, chip_d, c, 1), sibling)]
            for cp in onward:
                cp.start()
            sends[w] += onward
        for w in range(n):
            copy(w, 0, region(w, (x, y), 1 - c), me).wait_recv()
            copy(w, 5, region(w, chip_x, 1 - c), me).wait_recv()
            copy(w, 6, region(w, chip_y, 1 - c), me).wait_recv()
            copy(w, 7, region(w, chip_d, 1 - c, 0), me).wait_recv()
            copy(w, 8, region(w, chip_d, 1 - c, 1), me).wait_recv()
        for w in range(n):
            for cp in sends[w]:
                cp.wait_send()
            mine[w].wait()

    return _comm_multi(body, name="gather_shards", n_in=n,
                       out_shapes=[jax.ShapeDtypeStruct(f, s.dtype) for f, s in zip(fulls, shards)],
                       scratch=[pltpu.SemaphoreType.DMA((n_sem * n,)), pltpu.SemaphoreType.DMA((n_sem * n,)),
                                pltpu.SemaphoreType.DMA((n,))])(*shards)


def swap_other_halves(arrays, layout):
    n = len(arrays)
    halves = [tuple(d // 2 if a == ha else d for a, d in enumerate(g.shape)) for g, (_, ha) in zip(arrays, layout)]

    def body(*refs):
        ins, outs = refs[:n], refs[n:2 * n]
        send_sems, recv_sems = refs[2 * n:]
        x, y, c = lax.axis_index("x"), lax.axis_index("y"), lax.axis_index("c")
        copies = []
        for w in range(n):
            ha, size = layout[w][1], halves[w][layout[w][1]]
            copies.append(pltpu.make_async_remote_copy(
                src_ref=_slab(ins[w], {ha: ((1 - c) * size, size)}), dst_ref=outs[w], send_sem=send_sems.at[w],
                recv_sem=recv_sems.at[w], device_id=(x, y, 1 - c), device_id_type=MESH))
            copies[w].start()
        for cp in copies:
            cp.wait()

    return _comm_multi(body, name="grad_pair_swap", n_in=n,
                       out_shapes=[jax.ShapeDtypeStruct(h, g.dtype) for h, g in zip(halves, arrays)],
                       scratch=[pltpu.SemaphoreType.DMA((n,)), pltpu.SemaphoreType.DMA((n,))])(*arrays)


def scatter_to_chips(arrays, layout):
    n = len(arrays)
    slabs = [tuple(d // N_CHIPS if a == sa else d for a, d in enumerate(p.shape)) for p, (sa, _) in zip(arrays, layout)]

    def body(*refs):
        ins, outs = refs[:n], refs[n:2 * n]
        send_sems, recv_sems, local_sems = refs[2 * n:]
        x, y, c = lax.axis_index("x"), lax.axis_index("y"), lax.axis_index("c")
        me = 2 * x + y

        def slab(w, k):
            sa, size = layout[w][0], slabs[w][layout[w][0]]
            return _slab(ins[w], {sa: (k * size, size)})

        local, copies = [], []
        for w in range(n):
            local.append(pltpu.make_async_copy(slab(w, me), outs[w].at[me], local_sems.at[w]))
            local[w].start()
            for j, (px, py) in enumerate([(1 - x, y), (x, 1 - y), (1 - x, 1 - y)]):
                copies.append(pltpu.make_async_remote_copy(
                    src_ref=slab(w, 2 * px + py), dst_ref=outs[w].at[me], send_sem=send_sems.at[3 * w + j],
                    recv_sem=recv_sems.at[3 * w + j], device_id=(px, py, c), device_id_type=MESH))
                copies[-1].start()
        for cp in copies:
            cp.wait()
        for cp in local:
            cp.wait()

    return _comm_multi(body, name="grad_all_to_all", n_in=n,
                       out_shapes=[jax.ShapeDtypeStruct((N_CHIPS,) + s, p.dtype) for s, p in zip(slabs, arrays)],
                       scratch=[pltpu.SemaphoreType.DMA((3 * n,)), pltpu.SemaphoreType.DMA((3 * n,)),
                                pltpu.SemaphoreType.DMA((n,))])(*arrays)


def swap_with_sibling(arrays):
    n = len(arrays)

    def body(*refs):
        ins, outs = refs[:n], refs[n:2 * n]
        send_sems, recv_sems = refs[2 * n:]
        x, y, c = lax.axis_index("x"), lax.axis_index("y"), lax.axis_index("c")
        copies = [pltpu.make_async_remote_copy(src_ref=ins[w], dst_ref=outs[w], send_sem=send_sems.at[w], recv_sem=recv_sems.at[w],
                                               device_id=(x, y, 1 - c), device_id_type=MESH) for w in range(n)]
        for cp in copies:
            cp.start()
        for cp in copies:
            cp.wait()

    return _comm_multi(body, name="grad_half_swap", n_in=n, out_shapes=[jax.ShapeDtypeStruct(a.shape, a.dtype) for a in arrays],
                       scratch=[pltpu.SemaphoreType.DMA((n,)), pltpu.SemaphoreType.DMA((n,))])(*arrays)


def join_halves(mine, other, half_axis, c_arr, *, name):
    a0, a1, a2 = mine.shape
    tile = _row_tile(a1, a2)

    def body(c_ref, m_ref, o_ref, out_ref):
        for half in range(2):
            @pl.when(c_ref[0] == half)
            def _(half=half):
                out_ref[half] = m_ref[...]
                out_ref[1 - half] = o_ref[...]

    blk = pl.BlockSpec((None, tile, a2), lambda i, j, c_ref: (i, j, 0))
    if half_axis == 0:
        out_shape, out_blk = (2, a0, a1, a2), pl.BlockSpec((2, None, tile, a2), lambda i, j, c_ref: (0, i, j, 0))
    else:
        out_shape, out_blk = (a0, 2, a1, a2), pl.BlockSpec((None, 2, tile, a2), lambda i, j, c_ref: (i, 0, j, 0))
    spec = pltpu.PrefetchScalarGridSpec(num_scalar_prefetch=1, grid=(a0, a1 // tile), in_specs=[blk, blk], out_specs=out_blk)
    out = pl.pallas_call(body, name=name, out_shape=jax.ShapeDtypeStruct(out_shape, mine.dtype), grid_spec=spec,
                         compiler_params=pltpu.CompilerParams(vmem_limit_bytes=VMEM_LIMIT_BYTES,
                                                              dimension_semantics=("parallel", "parallel")))(c_arr, mine, other)
    return out.reshape((2 * a0, a1, a2) if half_axis == 0 else (a0, 2 * a1, a2))


def _row_tile(rows, cols, itemsize=4, budget=2 * 1024 * 1024):
    for t in (1024, 512, 256, 128, 64, 32, 16):
        if rows % t == 0 and t * cols * itemsize <= budget:
            return t
    return rows


def add_own_half(full, recv, half_axis, c_arr, *, name):
    a0, a1, a2 = recv.shape
    tile = _row_tile(a1, a2)

    def body(c_ref, f_ref, r_ref, o_ref):
        o_ref[...] = (f_ref[...] + r_ref[...]).astype(o_ref.dtype)

    if half_axis == 0:
        f_spec = pl.BlockSpec((None, tile, a2), lambda i, j, c_ref: (c_ref[0] * a0 + i, j, 0))
    else:
        f_spec = pl.BlockSpec((None, tile, a2), lambda i, j, c_ref: (i, c_ref[0] * (a1 // tile) + j, 0))
    blk = pl.BlockSpec((None, tile, a2), lambda i, j, c_ref: (i, j, 0))
    spec = pltpu.PrefetchScalarGridSpec(num_scalar_prefetch=1, grid=(a0, a1 // tile), in_specs=[f_spec, blk], out_specs=blk)
    return pl.pallas_call(body, name=name, out_shape=jax.ShapeDtypeStruct(recv.shape, BF16), grid_spec=spec,
                          compiler_params=pltpu.CompilerParams(vmem_limit_bytes=VMEM_LIMIT_BYTES,
                                                               dimension_semantics=("parallel", "parallel")))(c_arr, full, recv)


def sum_slots(q, *, name):
    _, a0, a1, a2 = q.shape
    tile = _row_tile(a1, a2, budget=1024 * 1024)

    def body(q_ref, o_ref):
        total = q_ref[0].astype(F32)
        for k in range(1, N_CHIPS):
            total = total + q_ref[k].astype(F32)
        o_ref[...] = total

    return _pcall(body, name=name, out_shape=jax.ShapeDtypeStruct((a0, a1, a2), F32), grid=(a0, a1 // tile),
                  in_specs=[pl.BlockSpec((N_CHIPS, None, tile, a2), lambda i, j: (0, i, j, 0))],
                  out_specs=pl.BlockSpec((None, tile, a2), lambda i, j: (i, j, 0)), sem=("parallel", "parallel"))(q)


def gather_weights(W):
    shards = []
    for name, _, _ in EXCHANGED:
        w = W[name].astype(BF16)
        if name == "gdn_w_in":
            w = jnp.pad(w, ((0, 0), (0, 0), (0, GDN_IN_SLOT - GDN_IN_SHARD)))
        if name == "kv_w":
            w = jnp.pad(w, ((0, 0), (0, KV_PAD_WIDTH - _KV_WIDTH)))[None]
        shards.append(w)
    fulls = dict(zip([n for n, _, _ in EXCHANGED], gather_shards(shards, [(sa, ha) for _, sa, ha in EXCHANGED])))
    slots = fulls["gdn_w_in"]
    fulls["gdn_w_in"] = jnp.concatenate([slots[..., k * GDN_IN_SLOT:k * GDN_IN_SLOT + GDN_IN_SHARD] for k in range(N_CHIPS)], axis=-1)
    fulls["kv_w_pad"] = fulls.pop("kv_w").reshape(-1, KV_PAD_WIDTH)
    conv = pack([W["gdn_conv"]], F32, row_multiple=8)
    conv_all = all_gather8(conv, name="gather_conv").reshape(N_DEV, conv.shape[0], PACK_COLS)
    fulls["gdn_conv"] = jnp.concatenate([unpack(conv_all[2 * k], [W["gdn_conv"].shape])[0] for k in range(N_CHIPS)], axis=-1)
    return fulls


def reduce_gradients(G):
    layout = [(sa, ha) for _, sa, ha in EXCHANGED]
    c_arr = lax.axis_index("c").astype(jnp.int32).reshape(1)
    received = swap_other_halves(G, layout)
    pairs = [add_own_half(g, r, ha, c_arr, name="grad_pair_sum") for g, r, (_, ha) in zip(G, received, layout)]
    slots = scatter_to_chips(pairs, layout)
    halves = [sum_slots(q, name="grad_chip_sum") for q in slots]
    others = swap_with_sibling(halves)
    return [join_halves(h, o, ha, c_arr, name="grad_join_halves") for h, o, (_, ha) in zip(halves, others, layout)]


def allreduce_small(G, names):
    packed = pack([G[n] for n in names], F32, row_multiple=8)
    gathered = all_gather8(packed, name="gather_small_grads").reshape(N_DEV, packed.shape[0], PACK_COLS)
    total = sum_leading(gathered, name="sum_small_grads")
    return dict(zip(names, unpack(total, [G[n].shape for n in names])))


def kernel(x, mem, *rest):
    n_w = len(WEIGHT_NAMES)
    W = dict(zip(WEIGHT_NAMES, rest[:n_w]))
    target = rest[n_w]
    M = dict(zip(WEIGHT_NAMES, rest[n_w + 1:2 * n_w + 1]))
    V = dict(zip(WEIGHT_NAMES, rest[2 * n_w + 1:3 * n_w + 1]))

    full = gather_weights(W)
    P = {n: W[n] for n in REPLICATED}
    P.update({n: full[n] for n in ("ffn1_w_gate_up", "ffn1_w_down", "ffn2_w_gate_up", "ffn2_w_down", "fox_w_in", "w_out",
                                   "mem_w_kv", "kv_w_pad")})
    derived = derived_weights(full["gdn_w_in"], full["gdn_conv"])
    P.update(gdn_w_in_pad=derived["gdn_w_in_pad"], conv8=derived["conv8"])

    part, dx, G = forward_backward(x[0], mem[0], target[0], P)
    loss = lax.psum(0.5 / x.shape[-1] * jnp.sum(part), ("x", "y", "c"))

    ref = reference_layout(G)
    exchange = {n: ref[n] for n, _, _ in EXCHANGED}
    exchange["gdn_w_in"] = jnp.concatenate(
        [jnp.pad(ref["gdn_w_in"][..., k * GDN_IN_SHARD:(k + 1) * GDN_IN_SHARD], ((0, 0), (0, 0), (0, GDN_IN_SLOT - GDN_IN_SHARD)))
         for k in range(N_CHIPS)], axis=-1)
    exchange["kv_w"] = G["kv_w_pad"].reshape(N_CHIPS, -1, KV_PAD_WIDTH)
    shards = dict(zip([n for n, _, _ in EXCHANGED], reduce_gradients([exchange[n] for n, _, _ in EXCHANGED])))
    shards["gdn_w_in"] = shards["gdn_w_in"][..., :GDN_IN_SHARD]
    shards["kv_w"] = shards["kv_w"][0, :, :_KV_WIDTH]
    grads = allreduce_small(ref, REPLICATED + ["gdn_conv"])
    conv_cols = W["gdn_conv"].shape[-1]
    chip = 2 * lax.axis_index("x") + lax.axis_index("y")
    grads["gdn_conv"] = lax.dynamic_slice_in_dim(grads["gdn_conv"], chip * conv_cols, conv_cols, axis=2)
    grads.update(shards)

    outs = {n: adamw(W[n], grads[n], M[n], V[n], name="adamw_" + n) for n in WEIGHT_NAMES}
    return (loss, dx[None], *[grads[n] for n in WEIGHT_NAMES], *[outs[n][0] for n in WEIGHT_NAMES],
            *[outs[n][1] for n in WEIGHT_NAMES], *[outs[n][2] for n in WEIGHT_NAMES])
```

```python
import jax
import jax.numpy as jnp
from jax import lax
from jax.experimental import pallas as pl
from jax.experimental.pallas import tpu as pltpu

F32, BF16 = jnp.float32, jnp.bfloat16
HI = lax.Precision.HIGHEST
MESH = pl.DeviceIdType.MESH

VMEM_LIMIT_BYTES = 56 * 1024 * 1024
LANES = 128
EPS = 1e-6
NEG_INF = -1e30

D_MODEL = 1024
HEAD_DIM = 128
GDN_HEADS = 6
GDN_WIDTH = GDN_HEADS * HEAD_DIM
FOX_HEADS = 6
FOX_WIDTH = FOX_HEADS * HEAD_DIM
MEM_HEADS = 4
MEM_HEAD_DIM = 64
MEM_WIDTH = MEM_HEADS * MEM_HEAD_DIM
FFN_HIDDEN = 2816
CONV_WIDTH = 4
GDN_CHUNK = 128
N_CHIPS = 4
N_DEV = 8

ADAM_LR, ADAM_B1, ADAM_B2, ADAM_EPS, ADAM_WD, ADAM_STEP = 0.001, 0.9, 0.999, 1e-08, 0.01, 10


def _pcall(body, *, name, out_shape, grid=(), in_specs=None, out_specs=None, scratch=(), sem=None):
    params = dict(vmem_limit_bytes=VMEM_LIMIT_BYTES)
    if sem is not None:
        params["dimension_semantics"] = sem
    kw = dict(grid=grid, in_specs=in_specs, out_specs=out_specs) if grid else {}
    return pl.pallas_call(body, name=name, out_shape=out_shape, scratch_shapes=list(scratch),
                          compiler_params=pltpu.CompilerParams(**params), **kw)


def _pick(n, cands):
    for c in cands:
        if n % c == 0:
            return c
    return n


def _make_dot(dtype, precision):
    def raw(a, b, dims):
        return lax.dot_general(a.astype(dtype), b.astype(dtype), (dims, ((), ())),
                               precision=precision, preferred_element_type=F32)

    @jax.custom_vjp
    def dot(a, b):
        return raw(a, b, ((1,), (0,)))

    def fwd(a, b):
        return dot(a, b), (a, b)

    def bwd(resid, ct):
        a, b = resid
        return raw(ct, b, ((1,), (1,))).astype(a.dtype), raw(a, ct, ((0,), (0,))).astype(b.dtype)

    dot.defvjp(fwd, bwd)
    dot.nn = lambda a, b: raw(a, b, ((1,), (0,)))
    dot.nt = lambda a, b: raw(a, b, ((1,), (1,)))
    dot.tn = lambda a, b: raw(a, b, ((0,), (0,)))
    return dot


bdot = _make_dot(BF16, None)
fdot = _make_dot(F32, HI)
idot = _make_dot(F32, lax.Precision.HIGH)
sdot = idot


def _sigmoid(x):
    return 0.5 * jnp.tanh(0.5 * x) + 0.5


def _silu(x):
    return x * _sigmoid(x)


def _softplus(x):
    return jnp.maximum(x, 0.0) + jnp.log(1.0 + jnp.exp(-jnp.abs(x)))


def _log_sigmoid(x):
    return -_softplus(-x)


def _iota2(shape, dim):
    return lax.broadcasted_iota(jnp.int32, shape, dim)


def mm(a, b, *, ta=False, tb=False, a_split=False, b_split=False, out_dtype=F32, scale=1.0, res=None, name):
    assert not (a_split and ta) and not (b_split and tb)
    (K, M) = a.shape if ta else ((2 * a.shape[2], a.shape[1]) if a_split else a.shape[::-1])
    (N, Kb) = b.shape if tb else ((2 * b.shape[2], b.shape[1]) if b_split else b.shape[::-1])
    assert K == Kb, (a.shape, b.shape, ta, tb)
    tm = _pick(M, (1024, 1408, 512, 256, 128))
    tn = _pick(N, (1024, 1408, 1152, 1664, 768, 512, 384, 256, 128))
    deep = a.dtype == BF16 and b.dtype == BF16
    if ta:
        tk = _pick(K, (2048, 1024, 512, 256, 128) if deep else (1024, 512, 256, 128))
    else:
        tk = _pick(K, (1024, 2816, 1408, 1152, 1664, 512, 256, 128) if deep else (1024, 1408, 1152, 1664, 512, 256, 128))
    assert not a_split or (K // 2) % tk == 0
    assert not b_split or (N // 2) % tn == 0
    nk = K // tk
    dims = (((0 if ta else 1,), (1 if tb else 0,)), ((), ()))

    def body(a_ref, b_ref, *rest):
        o_ref, acc = rest[-2], rest[-1]
        k = pl.program_id(2)

        @pl.when(k == 0)
        def _():
            acc[...] = jnp.zeros_like(acc)

        acc[...] += lax.dot_general(a_ref[...].astype(BF16), b_ref[...].astype(BF16), dims,
                                    preferred_element_type=F32)

        @pl.when(k == nk - 1)
        def _():
            out = acc[...] * scale
            if res is not None:
                out = out + rest[0][...].astype(F32)
            o_ref[...] = out.astype(o_ref.dtype)

    a_spec = pl.BlockSpec((tk, tm), lambda i, j, k: (k, i)) if ta else pl.BlockSpec((tm, tk), lambda i, j, k: (i, k))
    b_spec = pl.BlockSpec((tn, tk), lambda i, j, k: (j, k)) if tb else pl.BlockSpec((tk, tn), lambda i, j, k: (k, j))
    if a_split:
        per_half = K // 2 // tk
        a_spec = pl.BlockSpec((None, tm, tk), lambda i, j, k: (k // per_half, i, k % per_half))
    if b_split:
        per_half = N // 2 // tn
        b_spec = pl.BlockSpec((None, tk, tn), lambda i, j, k: (j // per_half, k, j % per_half))
    o_spec = pl.BlockSpec((tm, tn), lambda i, j, k: (i, j))
    ins, specs = [a, b], [a_spec, b_spec]
    if res is not None:
        ins.append(res)
        specs.append(o_spec)
    return _pcall(body, name=name, out_shape=jax.ShapeDtypeStruct((M, N), out_dtype),
                  grid=(M // tm, N // tn, nk), in_specs=specs, out_specs=o_spec,
                  scratch=[pltpu.VMEM((tm, tn), F32)], sem=("parallel", "parallel", "arbitrary"))(*ins)


def mm_rms_bwd(a, b, x, gain_row, dres, *, a_split=False, name):
    (K, M) = (2 * a.shape[2], a.shape[1]) if a_split else a.shape[::-1]
    D = b.shape[0]
    assert b.shape[1] == K and x.shape == (M, D)
    tm = _pick(M, (1024, 512, 256, 128))
    rows = _pick(tm, (256, 128))
    tk = _pick(K, (1024, 2816, 1408, 1152, 1664, 512, 256, 128))
    assert not a_split or (K // 2) % tk == 0
    nk = K // tk

    def body(a_ref, b_ref, x_ref, g_ref, r_ref, o_ref, dg_ref, acc):
        i, k = pl.program_id(0), pl.program_id(1)

        @pl.when(k == 0)
        def _():
            acc[...] = jnp.zeros_like(acc)

        acc[...] += bdot.nt(a_ref[...], b_ref[...])

        @pl.when(k == nk - 1)
        def _():
            dg = jnp.zeros((1, D), F32)
            for s in range(tm // rows):
                sl = slice(s * rows, (s + 1) * rows)
                xv = x_ref[sl, :]
                rstd = lax.rsqrt(jnp.mean(xv * xv, axis=-1, keepdims=True) + EPS)
                xh = xv * rstd
                dn = acc[sl, :]
                dy = dn * g_ref[...]
                o_ref[sl, :] = (dy - xh * jnp.mean(dy * xh, axis=-1, keepdims=True)) * rstd + r_ref[sl, :]
                dg = dg + jnp.sum(dn * xh, axis=0, keepdims=True)

            @pl.when(i == 0)
            def _():
                dg_ref[...] = dg

            @pl.when(i > 0)
            def _():
                dg_ref[...] += dg

    a_spec = pl.BlockSpec((tm, tk), lambda i, k: (i, k))
    if a_split:
        per_half = K // 2 // tk
        a_spec = pl.BlockSpec((None, tm, tk), lambda i, k: (k // per_half, i, k % per_half))
    row = pl.BlockSpec((tm, D), lambda i, k: (i, 0))
    one = pl.BlockSpec((1, D), lambda i, k: (0, 0))
    return _pcall(body, name=name, out_shape=[jax.ShapeDtypeStruct((M, D), F32), jax.ShapeDtypeStruct((1, D), F32)],
                  grid=(M // tm, nk), in_specs=[a_spec, pl.BlockSpec((D, tk), lambda i, k: (0, k)), row, one, row],
                  out_specs=[row, one], scratch=[pltpu.VMEM((tm, D), F32)], sem=("arbitrary", "arbitrary"))(a, b, x, gain_row, dres)


def _row_spec(r, tile):
    if isinstance(r, tuple):
        arr, width, col = r
        return arr, pl.BlockSpec((tile, width), lambda i, col=col: (i, col))
    return r, pl.BlockSpec((tile, r.shape[1]), lambda i: (i, 0))


def _const_spec(c):
    return pl.BlockSpec(c.shape, lambda i: (0,) * c.ndim)


def rowwise(fn, rows, consts, outs, *, tile, name):
    arrs, specs = zip(*[_row_spec(r, tile) for r in rows])
    n_rows = arrs[0].shape[0]
    tile = min(tile, n_rows)
    n_in = len(rows) + len(consts)

    def body(*refs):
        res = fn(*[r[...] for r in refs[:n_in]])
        for o_ref, o in zip(refs[n_in:], res):
            o_ref[...] = o.astype(o_ref.dtype)

    arrs, specs = zip(*[_row_spec(r, tile) for r in rows])
    return _pcall(body, name=name,
                  out_shape=[jax.ShapeDtypeStruct((n_rows, w), dt) for w, dt in outs],
                  grid=(n_rows // tile,),
                  in_specs=list(specs) + [_const_spec(c) for c in consts],
                  out_specs=[pl.BlockSpec((tile, w), lambda i: (i, 0)) for w, _ in outs],
                  sem=("parallel",))(*arrs, *consts)


def rowwise_bwd(fn, rows, consts, cts, *, tile, name, row_grads, const_grads, add=None):
    arrs, _ = zip(*[_row_spec(r, tile) for r in rows])
    n_rows = arrs[0].shape[0]
    tile = min(tile, n_rows)
    arrs, specs = zip(*[_row_spec(r, tile) for r in rows])
    ct_arrs, ct_specs = zip(*[_row_spec(c, tile) for c in cts])
    add = add or {}
    add_idx = sorted(add)
    add_arrs, add_specs = (zip(*[_row_spec(add[i], tile) for i in add_idx]) if add_idx else ((), ()))
    nr, nc, nct, na = len(rows), len(consts), len(cts), len(add_idx)
    want_rows = [i for i, d in enumerate(row_grads) if d is not None]
    want_consts = [i for i, w in enumerate(const_grads) if w]

    def body(*refs):
        row_v = [r[...] for r in refs[:nr]]
        const_v = [r[...] for r in refs[nr:nr + nc]]
        ct_v = [r[...] for r in refs[nr + nc:nr + nc + nct]]
        add_v = {i: refs[nr + nc + nct + j][...] for j, i in enumerate(add_idx)}
        out_refs = refs[nr + nc + nct + na:]
        res, vjp = jax.vjp(fn, *row_v, *const_v)
        grads = vjp(tuple(c.astype(o.dtype) for c, o in zip(ct_v, res)))
        for o_ref, i in zip(out_refs, want_rows):
            g = grads[i].astype(F32)
            if i in add_v:
                g = g + add_v[i].astype(F32)
            o_ref[...] = g.astype(o_ref.dtype)
        first = pl.program_id(0) == 0
        for o_ref, i in zip(out_refs[len(want_rows):], want_consts):
            g = grads[nr + i].astype(F32)

            @pl.when(first)
            def _(o_ref=o_ref, g=g):
                o_ref[...] = g

            @pl.when(jnp.logical_not(first))
            def _(o_ref=o_ref, g=g):
                o_ref[...] += g

    def width(r):
        return r[1] if isinstance(r, tuple) else r.shape[1]

    out_shape = [jax.ShapeDtypeStruct((n_rows, width(rows[i])), row_grads[i]) for i in want_rows]
    out_shape += [jax.ShapeDtypeStruct(consts[i].shape, F32) for i in want_consts]
    out_specs = [pl.BlockSpec((tile, width(rows[i])), lambda i_: (i_, 0)) for i in want_rows]
    out_specs += [_const_spec(consts[i]) for i in want_consts]
    return _pcall(body, name=name, out_shape=out_shape, grid=(n_rows // tile,),
                  in_specs=list(specs) + [_const_spec(c) for c in consts] + list(ct_specs) + list(add_specs),
                  out_specs=out_specs, sem=("arbitrary",))(*arrs, *consts, *ct_arrs, *add_arrs)


def f_rmsnorm(x, g):
    x = x.astype(F32)
    return (x * lax.rsqrt(jnp.mean(x * x, axis=-1, keepdims=True) + EPS) * g,)


def _head_sel(first_lane):
    r, c = _iota2((LANES, GDN_WIDTH), 0), _iota2((LANES, GDN_WIDTH), 1)
    return (r == c // HEAD_DIM + first_lane).astype(F32)


def _tri(n, strict=False):
    r, c = _iota2((n, n), 0), _iota2((n, n), 1)
    return r > c if strict else r >= c


def f_gdn_pre(xc, ab, a_log, dt_bias):
    s = _silu(xc.astype(F32))
    qs, ks = [], []
    for h in range(GDN_HEADS):
        qh = s[:, h * HEAD_DIM:(h + 1) * HEAD_DIM]
        kh = s[:, GDN_WIDTH + h * HEAD_DIM:GDN_WIDTH + (h + 1) * HEAD_DIM]
        qs.append(qh * lax.rsqrt(jnp.sum(qh * qh, axis=-1, keepdims=True) + EPS) * (HEAD_DIM ** -0.5))
        ks.append(kh * lax.rsqrt(jnp.sum(kh * kh, axis=-1, keepdims=True) + EPS))
    q, k = jnp.concatenate(qs, axis=1), jnp.concatenate(ks, axis=1)
    v = s[:, 2 * GDN_WIDTH:]
    ab = ab.astype(F32)
    g = -jnp.exp(a_log) * _softplus(ab + dt_bias)
    gc = _head_broadcast(fdot(_tri(GDN_CHUNK).astype(F32), g), 0)
    beta = _head_broadcast(_sigmoid(ab), GDN_HEADS)
    return q, k, v, gc, beta


def _head_broadcast(x, first_lane):
    lane = _iota2(x.shape, 1)
    cols = [jnp.sum(jnp.where(lane == first_lane + h, x, 0.0), axis=1, keepdims=True) for h in range(GDN_HEADS)]
    return jnp.concatenate([jnp.broadcast_to(c, x.shape) for c in cols], axis=1)


def _unit_lower_inverses(neg_lowers):
    C = neg_lowers[0].shape[0]
    eye = (_iota2((C, C), 0) == _iota2((C, C), 1)).astype(F32)
    invs = [eye + n for n in neg_lowers]
    powers = list(neg_lowers)
    for _ in range(6):
        powers = [idot.nn(p, p) for p in powers]
        invs = [inv + idot.nn(p, inv) for p, inv in zip(powers, invs)]
    return invs


@jax.custom_vjp
def _solve_with_inverse(inv, neg_lower, rhs):
    return idot.nn(inv, rhs)


def _solve_fwd(inv, neg_lower, rhs):
    x = idot.nn(inv, rhs)
    return x, (inv, x)


def _solve_bwd(resid, ct):
    inv, x = resid
    d_rhs = idot.tn(inv, ct)
    return jnp.zeros_like(inv), idot.nt(d_rhs, x), d_rhs


_solve_with_inverse.defvjp(_solve_fwd, _solve_bwd)


def f_gdn_intra(q, k, v, gc, beta, inv=None):
    C = GDN_CHUNK
    causal, strict = _tri(C), _tri(C, strict=True)
    is_last = _iota2((C, HEAD_DIM), 0) == C - 1
    heads = range(GDN_HEADS)
    sls = [slice(h * HEAD_DIM, (h + 1) * HEAD_DIM) for h in heads]
    qs, ks, vs, gs, bs = ([a[:, sl] for sl in sls] for a in (q, k, v, gc, beta))
    decays = [jnp.where(causal, jnp.exp(jnp.where(causal, g - g.T, 0.0)), 0.0) for g in gs]
    kbs = [kh * bh for kh, bh in zip(ks, bs)]
    kts = [kh.T for kh in ks]
    neg_lowers = [jnp.where(strict, -(idot(kb, kt) * d), 0.0) for kb, kt, d in zip(kbs, kts, decays)]
    qks = [jnp.where(causal, idot(qh, kt) * d, 0.0) for qh, kt, d in zip(qs, kts, decays)]
    rhss = [jnp.concatenate([vh * bh, kb * jnp.exp(g)], axis=1) for vh, bh, kb, g in zip(vs, bs, kbs, gs)]
    if inv is None:
        invs = _unit_lower_inverses(neg_lowers)
        sols = [idot.nn(m, r) for m, r in zip(invs, rhss)]
    else:
        sols = [_solve_with_inverse(inv[:, sl], n, r) for sl, n, r in zip(sls, neg_lowers, rhss)]
    g_lasts = [jnp.sum(jnp.where(is_last, g, 0.0), axis=0, keepdims=True) for g in gs]
    outs = [[s[:, :HEAD_DIM] for s in sols], [s[:, HEAD_DIM:] for s in sols], qks,
            [kh * jnp.exp(gl - g) for kh, gl, g in zip(ks, g_lasts, gs)], [qh * jnp.exp(g) for qh, g in zip(qs, gs)]]
    if inv is None:
        outs.append(invs)
    return tuple(jnp.concatenate(o, axis=1) for o in outs)


def f_gdn_post(o, z, gain):
    z = z.astype(F32)
    parts = []
    for h in range(GDN_HEADS):
        oh = o[:, h * HEAD_DIM:(h + 1) * HEAD_DIM]
        parts.append(oh * lax.rsqrt(jnp.mean(oh * oh, axis=-1, keepdims=True) + EPS) * gain)
    return (jnp.concatenate(parts, axis=1) * _silu(z),)


def f_mem_attn(q, k, v):
    q = q.astype(F32)
    lane_head = _iota2((1, MEM_WIDTH), 1) // MEM_HEAD_DIM
    kt = k.astype(F32).T
    masks = [(lane_head == h).astype(F32) for h in range(MEM_HEADS)]
    logits = [bdot(q * mask, kt) * (MEM_HEAD_DIM ** -0.5) for mask in masks]
    ps = [jnp.exp(s - jnp.max(s, axis=-1, keepdims=True)) for s in logits]
    ps = [p / jnp.sum(p, axis=-1, keepdims=True) for p in ps]
    outs = [bdot(p, v) * mask for p, mask in zip(ps, masks)]
    return ((outs[0] + outs[1]) + (outs[2] + outs[3]),)


def f_loss(y, t):
    d = y - t
    return (d * d,)


def conv_fwd(proj, w8, *, width, tile=512):
    n_rows = proj.shape[0]
    tile = min(tile, n_rows)

    def body(x_ref, halo_ref, w_ref, o_ref):
        i = pl.program_id(0)
        halo = jnp.where(i > 0, halo_ref[...].astype(F32), 0.0)
        xs = jnp.concatenate([halo, x_ref[...].astype(F32)], axis=0)
        acc = xs[8:] * w_ref[3:4, :]
        for j in range(CONV_WIDTH - 1):
            acc = acc + pltpu.roll(xs, CONV_WIDTH - 1 - j, 0)[8:] * w_ref[j:j + 1, :]
        o_ref[...] = acc

    return _pcall(body, name="gdn_conv_fwd", out_shape=jax.ShapeDtypeStruct((n_rows, width), F32),
                  grid=(n_rows // tile,),
                  in_specs=[pl.BlockSpec((tile, width), lambda i: (i, 0)),
                            pl.BlockSpec((8, width), lambda i: (jnp.maximum(i * (tile // 8) - 1, 0), 0)),
                            pl.BlockSpec((8, width), lambda i: (0, 0))],
                  out_specs=pl.BlockSpec((tile, width), lambda i: (i, 0)), sem=("parallel",))(proj, proj, w8)


def conv_bwd(proj, w8, dy, *, width, tile=512):
    n_rows = proj.shape[0]
    tile = min(tile, n_rows)
    n = n_rows // tile

    def body(x_ref, xhalo_ref, w_ref, dy_ref, dyhalo_ref, dx_ref, dw_ref):
        i = pl.program_id(0)
        dy = dy_ref[...]
        after = jnp.where(i < n - 1, dyhalo_ref[...], 0.0)
        ds = jnp.concatenate([dy, after], axis=0)
        dx = dy * w_ref[3:4, :]
        for j in range(CONV_WIDTH - 1):
            shift = CONV_WIDTH - 1 - j
            dx = dx + pltpu.roll(ds, tile + 8 - shift, 0)[:tile] * w_ref[j:j + 1, :]
        dx_ref[...] = dx.astype(dx_ref.dtype)
        halo = jnp.where(i > 0, xhalo_ref[...].astype(F32), 0.0)
        xs = jnp.concatenate([halo, x_ref[...].astype(F32)], axis=0)
        rows = [jnp.sum(dy * pltpu.roll(xs, CONV_WIDTH - 1 - j, 0)[8:], axis=0, keepdims=True)
                for j in range(CONV_WIDTH - 1)]
        rows.append(jnp.sum(dy * xs[8:], axis=0, keepdims=True))
        dw = jnp.concatenate(rows + [jnp.zeros((8 - CONV_WIDTH, width), F32)], axis=0)

        @pl.when(i == 0)
        def _():
            dw_ref[...] = dw

        @pl.when(i > 0)
        def _():
            dw_ref[...] += dw

    t8 = tile // 8
    return _pcall(body, name="gdn_conv_bwd",
                  out_shape=[jax.ShapeDtypeStruct((n_rows, width), BF16), jax.ShapeDtypeStruct((8, width), F32)],
                  grid=(n,),
                  in_specs=[pl.BlockSpec((tile, width), lambda i: (i, 0)),
                            pl.BlockSpec((8, width), lambda i: (jnp.maximum(i * t8 - 1, 0), 0)),
                            pl.BlockSpec((8, width), lambda i: (0, 0)),
                            pl.BlockSpec((tile, width), lambda i: (i, 0)),
                            pl.BlockSpec((8, width), lambda i: (jnp.minimum((i + 1) * t8, n * t8 - 1), 0))],
                  out_specs=[pl.BlockSpec((tile, width), lambda i: (i, 0)), pl.BlockSpec((8, width), lambda i: (0, 0))],
                  sem=("arbitrary",))(proj, proj, w8, dy, dy)


def gdn_scan_fwd(u, w, qk, kt, qh, gc):
    n_rows = u.shape[0]
    C, n = GDN_CHUNK, u.shape[0] // GDN_CHUNK

    def body(u_ref, w_ref, qk_ref, kt_ref, qh_ref, gc_ref, o_ref, vn_ref, sin_ref, st):
        @pl.when(pl.program_id(0) == 0)
        def _():
            st[...] = jnp.zeros_like(st)

        sin_ref[0] = st[...]
        sls = [slice(h * HEAD_DIM, (h + 1) * HEAD_DIM) for h in range(GDN_HEADS)]
        states = [st[sl, :] for sl in sls]
        v_news = [u_ref[:, sl] - sdot(w_ref[:, sl], s) for sl, s in zip(sls, states)]
        from_state = [sdot(qh_ref[:, sl], s) for sl, s in zip(sls, states)]
        for sl, a, v_new in zip(sls, from_state, v_news):
            o_ref[:, sl] = a + sdot(qk_ref[:, sl], v_new)
            vn_ref[:, sl] = v_new
        for sl, s, v_new in zip(sls, states, v_news):
            st[sl, :] = s * jnp.exp(gc_ref[C - 1:C, sl]) + sdot.tn(kt_ref[:, sl], v_new)

    blk = pl.BlockSpec((C, GDN_WIDTH), lambda i: (i, 0))
    return _pcall(body, name="gdn_scan_fwd",
                  out_shape=[jax.ShapeDtypeStruct((n_rows, GDN_WIDTH), F32), jax.ShapeDtypeStruct((n_rows, GDN_WIDTH), F32),
                             jax.ShapeDtypeStruct((n, GDN_WIDTH, HEAD_DIM), F32)],
                  grid=(n,), in_specs=[blk] * 6,
                  out_specs=[blk, blk, pl.BlockSpec((1, GDN_WIDTH, HEAD_DIM), lambda i: (i, 0, 0))],
                  scratch=[pltpu.VMEM((GDN_WIDTH, HEAD_DIM), F32)], sem=("arbitrary",))(u, w, qk, kt, qh, gc)


def gdn_scan_bwd(do, w, qk, kt, qh, gc, vn, sin):
    n_rows = do.shape[0]
    C, n = GDN_CHUNK, do.shape[0] // GDN_CHUNK

    def body(do_ref, w_ref, qk_ref, kt_ref, qh_ref, gc_ref, vn_ref, sin_ref,
             du_ref, dw_ref, dqk_ref, dkt_ref, dqh_ref, dgl_ref, dst):
        @pl.when(pl.program_id(0) == 0)
        def _():
            dst[...] = jnp.zeros_like(dst)

        sls = [slice(h * HEAD_DIM, (h + 1) * HEAD_DIM) for h in range(GDN_HEADS)]
        dvns = [sdot.tn(qk_ref[:, sl], do_ref[:, sl]) + sdot(kt_ref[:, sl], dst[sl, :]) for sl in sls]
        for sl, dvn in zip(sls, dvns):
            du_ref[:, sl] = dvn
            dw_ref[:, sl] = -sdot.nt(dvn, sin_ref[0, sl, :])
        for sl in sls:
            dqk_ref[:, sl] = sdot.nt(do_ref[:, sl], vn_ref[:, sl])
            dkt_ref[:, sl] = sdot.nt(vn_ref[:, sl], dst[sl, :])
            dqh_ref[:, sl] = sdot.nt(do_ref[:, sl], sin_ref[0, sl, :])
        for sl, dvn in zip(sls, dvns):
            ds_out = dst[sl, :]
            e = jnp.exp(gc_ref[C - 1:C, sl])
            dgl = jnp.sum(ds_out * sin_ref[0, sl, :], axis=0, keepdims=True) * e
            dgl_ref[:, sl] = jnp.broadcast_to(dgl, (8, HEAD_DIM))
            dst[sl, :] = sdot.tn(qh_ref[:, sl], do_ref[:, sl]) + e * ds_out - sdot.tn(w_ref[:, sl], dvn)

    blk = pl.BlockSpec((C, GDN_WIDTH), lambda i: (n - 1 - i, 0))
    row = jax.ShapeDtypeStruct((n_rows, GDN_WIDTH), F32)
    return _pcall(body, name="gdn_scan_bwd",
                  out_shape=[row] * 5 + [jax.ShapeDtypeStruct((n * 8, GDN_WIDTH), F32)],
                  grid=(n,), in_specs=[blk] * 7 + [pl.BlockSpec((1, GDN_WIDTH, HEAD_DIM), lambda i: (n - 1 - i, 0, 0))],
                  out_specs=[blk] * 5 + [pl.BlockSpec((8, GDN_WIDTH), lambda i: (n - 1 - i, 0))],
                  scratch=[pltpu.VMEM((GDN_WIDTH, HEAD_DIM), F32)], sem=("arbitrary",))(do, w, qk, kt, qh, gc, vn, sin)


def gdn_intra_bwd(q, k, v, gc, beta, inv, cts, dgl):
    n_rows = q.shape[0]
    C = GDN_CHUNK

    def body(*refs):
        ins = [r[...] for r in refs[:5]]
        inv_v = refs[5][...]
        ct = tuple(r[...] for r in refs[6:11])
        dgl_v = refs[11][...]
        _, vjp = jax.vjp(lambda *a: f_gdn_intra(*a, inv=inv_v), *ins)
        grads = list(vjp(ct))
        last = _iota2((C, GDN_WIDTH), 0) == C - 1
        grads[3] = grads[3] + jnp.where(last, jnp.broadcast_to(dgl_v[0:1, :], (C, GDN_WIDTH)), 0.0)
        for o_ref, g in zip(refs[12:], grads):
            o_ref[...] = g

    blk = pl.BlockSpec((C, GDN_WIDTH), lambda i: (i, 0))
    return _pcall(body, name="gdn_intra_bwd", out_shape=[jax.ShapeDtypeStruct((n_rows, GDN_WIDTH), F32)] * 5,
                  grid=(n_rows // C,), in_specs=[blk] * 11 + [pl.BlockSpec((8, GDN_WIDTH), lambda i: (i, 0))],
                  out_specs=[blk] * 5, sem=("parallel",))(q, k, v, gc, beta, inv, *cts, dgl)


def fox_gate_fwd(f, b_f):
    n_rows = f.shape[0]
    T = LANES

    def body(f_ref, b_ref, cb_ref, carry):
        @pl.when(pl.program_id(0) == 0)
        def _():
            carry[...] = jnp.zeros_like(carry)

        c = fdot(_tri(T).astype(F32), _log_sigmoid(f_ref[...] + b_ref[...])) + carry[...]
        carry[...] = c[T - 1:T, :]
        cb_ref[...] = fdot(c, _head_sel(0))

    return _pcall(body, name="fox_gate_fwd", out_shape=jax.ShapeDtypeStruct((n_rows, FOX_WIDTH), F32),
                  grid=(n_rows // T,),
                  in_specs=[pl.BlockSpec((T, LANES), lambda i: (i, 0)), pl.BlockSpec((1, LANES), lambda i: (0, 0))],
                  out_specs=pl.BlockSpec((T, FOX_WIDTH), lambda i: (i, 0)),
                  scratch=[pltpu.VMEM((1, LANES), F32)], sem=("arbitrary",))(f, b_f)


def fox_gate_bwd(f, b_f, dcrow, dcb):
    n_rows = f.shape[0]
    T = LANES
    n = n_rows // T

    def body(f_ref, b_ref, dc_ref, dcb_ref, df_ref, db_ref, carry):
        i = pl.program_id(0)

        @pl.when(i == 0)
        def _():
            carry[...] = jnp.zeros_like(carry)

        rows = [dc_ref[h] for h in range(FOX_HEADS)] + [jnp.zeros((T - FOX_HEADS, T), F32)]
        first_lane = (_iota2((FOX_WIDTH, LANES), 0) == _iota2((FOX_WIDTH, LANES), 1) * HEAD_DIM).astype(F32)
        dc = jnp.concatenate(rows, axis=0).T + fdot(dcb_ref[...], first_lane)
        dlog = fdot.tn(_tri(T).astype(F32), dc) + carry[...]
        carry[...] = dlog[0:1, :]
        df = dlog * (1.0 - _sigmoid(f_ref[...] + b_ref[...]))
        df_ref[...] = df
        db = jnp.sum(df, axis=0, keepdims=True)

        @pl.when(i == 0)
        def _():
            db_ref[...] = db

        @pl.when(i > 0)
        def _():
            db_ref[...] += db

    return _pcall(body, name="fox_gate_bwd",
                  out_shape=[jax.ShapeDtypeStruct((n_rows, LANES), F32), jax.ShapeDtypeStruct((1, LANES), F32)],
                  grid=(n,),
                  in_specs=[pl.BlockSpec((T, LANES), lambda i: (n - 1 - i, 0)), pl.BlockSpec((1, LANES), lambda i: (0, 0)),
                            pl.BlockSpec((FOX_HEADS, 1, T), lambda i: (0, 0, n - 1 - i)),
                            pl.BlockSpec((T, FOX_WIDTH), lambda i: (n - 1 - i, 0))],
                  out_specs=[pl.BlockSpec((T, LANES), lambda i: (n - 1 - i, 0)), pl.BlockSpec((1, LANES), lambda i: (0, 0))],
                  scratch=[pltpu.VMEM((1, LANES), F32)], sem=("arbitrary",))(f, b_f, dcrow, dcb)


FOX_AUG = 2 * HEAD_DIM


def _fox_tiles(n_rows):
    return min(1024, n_rows), min(1024, n_rows)


def _fox_pairs(n_rows, query_major):
    tq, tk = _fox_tiles(n_rows)
    nq, r = n_rows // tq, tq // tk
    if query_major:
        pairs = [(i, j) for i in range(nq) for j in range(r * (i + 1))]
    else:
        pairs = [(i, j) for j in range(nq * r) for i in range(j // r, nq)]
    return jnp.asarray([p[0] for p in pairs], jnp.int32), jnp.asarray([p[1] for p in pairs], jnp.int32)


def fox_augment(x, cb, query_side):
    def fn(xt, ct):
        xt = xt.astype(F32)
        lane = _iota2((xt.shape[0], HEAD_DIM), 1)
        parts = []
        for h in range(FOX_HEADS):
            sl = slice(h * HEAD_DIM, (h + 1) * HEAD_DIM)
            c = ct[:, sl]
            hi = c.astype(BF16).astype(F32)
            mid = (c - hi).astype(BF16).astype(F32)
            lo = (c - hi - mid).astype(BF16).astype(F32)
            terms = jnp.where(lane % 3 == 0, hi, jnp.where(lane % 3 == 1, mid, lo))
            if query_side:
                extra = jnp.where(lane < 3, terms, jnp.where(lane < 6, 1.0, 0.0))
                parts += [xt[:, sl] * (HEAD_DIM ** -0.5), extra]
            else:
                extra = jnp.where(lane < 3, 1.0, jnp.where(lane < 6, -terms, 0.0))
                parts += [xt[:, sl], extra]
        return (jnp.concatenate(parts, axis=1),)

    return rowwise(fn, [(x, FOX_WIDTH, 0), cb], [], [(FOX_HEADS * FOX_AUG, BF16)], tile=ROW_TILE,
                   name="fox_augment_q" if query_side else "fox_augment_k")[0]


def _pcall_tables(body, *, name, out_shape, grid, tables, in_specs, out_specs, scratch, sem):
    spec = pltpu.PrefetchScalarGridSpec(num_scalar_prefetch=len(tables), grid=grid, in_specs=in_specs, out_specs=out_specs,
                                        scratch_shapes=list(scratch))
    return pl.pallas_call(body, name=name, out_shape=out_shape, grid_spec=spec,
                          compiler_params=pltpu.CompilerParams(vmem_limit_bytes=VMEM_LIMIT_BYTES, dimension_semantics=sem))


def _fox_logits(qa, ka, offset):
    s = bdot.nt(qa, ka)
    if offset is not None:
        s = jnp.where(_iota2(s.shape, 0) + offset >= _iota2(s.shape, 1), s, NEG_INF)
    return s


def _fox_p_ds(offset, qa_ref, ka_ref, v_ref, o_ref, lse_ref, do_ref):
    s = _fox_logits(qa_ref[...], ka_ref[...], offset)
    p = jnp.exp(s - jnp.tile(lse_ref[...], (1, s.shape[1] // LANES)))
    d_o = do_ref[...].astype(F32)
    delta = jnp.sum(d_o * o_ref[...].astype(F32), axis=-1, keepdims=True)
    return p, p * (bdot.nt(d_o, v_ref[...]) - delta), d_o


def _fox_on_diagonal(i, j, r, tk, step):
    @pl.when(j < r * i)
    def _():
        step(None)

    for m in range(r):
        @pl.when(j == r * i + m)
        def _(m=m):
            step(-m * tk)


def _fox_specs(tq, tk, do_col):
    qaspec = pl.BlockSpec((tq, FOX_AUG), lambda h, p, it, jt: (it[p], h))
    qspec = pl.BlockSpec((tq, HEAD_DIM), lambda h, p, it, jt: (it[p], h))
    dospec = pl.BlockSpec((tq, HEAD_DIM), lambda h, p, it, jt: (it[p], do_col + h))
    kaspec = pl.BlockSpec((tk, FOX_AUG), lambda h, p, it, jt: (jt[p], h))
    kspec = pl.BlockSpec((tk, HEAD_DIM), lambda h, p, it, jt: (jt[p], h))
    vspec = pl.BlockSpec((tk, HEAD_DIM), lambda h, p, it, jt: (jt[p], FOX_HEADS + h))
    cspec = pl.BlockSpec((1, 1, tk), lambda h, p, it, jt: (h, 0, jt[p]))
    return qaspec, qspec, dospec, kaspec, kspec, vspec, cspec


def fox_fwd(qa, kv, ka):
    n_rows = kv.shape[0]
    tq, tk = _fox_tiles(n_rows)
    r = tq // tk
    tables = _fox_pairs(n_rows, True)

    def body(it, jt, qa_ref, ka_ref, v_ref, o_ref, lse_ref, m_sc, l_sc, acc):
        i, j = it[pl.program_id(1)], jt[pl.program_id(1)]

        @pl.when(j == 0)
        def _():
            m_sc[...] = jnp.full(m_sc.shape, NEG_INF, F32)
            l_sc[...] = jnp.zeros_like(l_sc)
            acc[...] = jnp.zeros_like(acc)

        def step(offset):
            s = _fox_logits(qa_ref[...], ka_ref[...], offset)
            m_old = m_sc[...]
            m_new = jnp.maximum(m_old, jnp.max(s, axis=-1, keepdims=True))
            alpha = jnp.exp(m_old - m_new)
            p = jnp.exp(s - jnp.tile(m_new, (1, tk // LANES)))
            l_sc[...] = l_sc[...] * alpha + jnp.sum(p, axis=-1, keepdims=True)
            acc[...] = acc[...] * alpha + bdot(p, v_ref[...])
            m_sc[...] = m_new

        _fox_on_diagonal(i, j, r, tk, step)

        @pl.when(j == r * i + r - 1)
        def _():
            o_ref[...] = (acc[...] / l_sc[...]).astype(o_ref.dtype)
            lse_ref[...] = m_sc[...] + jnp.log(l_sc[...])

    qaspec, qspec, _, kaspec, _, vspec, _ = _fox_specs(tq, tk, 0)
    return _pcall_tables(body, name="fox_fwd",
                         out_shape=[jax.ShapeDtypeStruct((n_rows, FOX_WIDTH), BF16), jax.ShapeDtypeStruct((n_rows, FOX_WIDTH), F32)],
                         grid=(FOX_HEADS, tables[0].shape[0]), tables=tables,
                         in_specs=[qaspec, kaspec, vspec], out_specs=[qspec, qspec],
                         scratch=[pltpu.VMEM((tq, HEAD_DIM), F32)] * 3, sem=("parallel", "arbitrary"))(*tables, qa, ka, kv)


def fox_bwd_dq(qa, kv, ka, o, lse, do, prev=None):
    do, _, do_col = do
    do_col *= FOX_HEADS
    n_rows = kv.shape[0]
    tq, tk = _fox_tiles(n_rows)
    r = tq // tk
    n_prev = 0 if prev is None else 1
    tables = _fox_pairs(n_rows, True)

    def body(it, jt, qa_ref, ka_ref, k_ref, v_ref, o_ref, lse_ref, do_ref, *rest):
        dq_ref, drow_ref, acc, rows = rest[n_prev:]
        i, j = it[pl.program_id(1)], jt[pl.program_id(1)]

        @pl.when(j == 0)
        def _():
            acc[...] = jnp.zeros_like(acc)
            rows[...] = jnp.zeros_like(rows)

        def step(offset):
            _, ds, _ = _fox_p_ds(offset, qa_ref, ka_ref, v_ref, o_ref, lse_ref, do_ref)
            acc[...] += bdot(ds, k_ref[...])
            rows[...] += jnp.sum(ds, axis=-1, keepdims=True)

        _fox_on_diagonal(i, j, r, tk, step)

        @pl.when(j == r * i + r - 1)
        def _():
            dq_ref[...] = (acc[...] * (HEAD_DIM ** -0.5)).astype(dq_ref.dtype)
            drow_ref[...] = rows[...] + rest[0][...] if n_prev else rows[...]

    qaspec, qspec, dospec, kaspec, kspec, vspec, _ = _fox_specs(tq, tk, do_col)
    return _pcall_tables(body, name="fox_bwd_dq" + ("_acc" if n_prev else ""),
                         out_shape=[jax.ShapeDtypeStruct((n_rows, FOX_WIDTH), BF16), jax.ShapeDtypeStruct((n_rows, FOX_WIDTH), F32)],
                         grid=(FOX_HEADS, tables[0].shape[0]), tables=tables,
                         in_specs=[qaspec, kaspec, kspec, vspec, qspec, qspec, dospec] + [qspec] * n_prev,
                         out_specs=[qspec, qspec], scratch=[pltpu.VMEM((tq, HEAD_DIM), F32)] * 2,
                         sem=("parallel", "arbitrary"))(*tables, qa, ka, kv, kv, o, lse, do, *([prev] if n_prev else []))


def fox_bwd_dkv(qa, kv, ka, o, lse, do, prev=None):
    do, _, do_col = do
    do_col *= FOX_HEADS
    n_rows = kv.shape[0]
    tq, tk = _fox_tiles(n_rows)
    nq, r = n_rows // tq, tq // tk
    n_prev = 0 if prev is None else 3
    tables = _fox_pairs(n_rows, False)

    def body(it, jt, qa_ref, ka_ref, v_ref, o_ref, lse_ref, do_ref, *rest):
        prev_refs = rest[:n_prev]
        dk_ref, dv_ref, dc_ref, dk_acc, dv_acc, dc_acc = rest[n_prev:]
        i, j = it[pl.program_id(1)], jt[pl.program_id(1)]

        @pl.when(j >= r * i)
        def _():
            dk_acc[...] = jnp.zeros_like(dk_acc)
            dv_acc[...] = jnp.zeros_like(dv_acc)
            dc_acc[...] = jnp.zeros_like(dc_acc)

        def step(offset):
            p, ds, d_o = _fox_p_ds(offset, qa_ref, ka_ref, v_ref, o_ref, lse_ref, do_ref)
            dv_acc[...] += bdot.tn(p, d_o)
            dk_acc[...] += bdot.tn(ds, qa_ref[:, :HEAD_DIM])
            dc_acc[...] -= jnp.sum(ds, axis=0, keepdims=True)

        _fox_on_diagonal(i, j, r, tk, step)

        @pl.when(i == nq - 1)
        def _():
            dk, dv, dc = dk_acc[...], dv_acc[...], dc_acc[...]
            if n_prev:
                dk, dv, dc = dk + prev_refs[0][...], dv + prev_refs[1][...], dc + prev_refs[2][0]
            dk_ref[...] = dk
            dv_ref[...] = dv
            dc_ref[0] = dc

    qaspec, qspec, dospec, kaspec, kspec, vspec, cspec = _fox_specs(tq, tk, do_col)
    return _pcall_tables(body, name="fox_bwd_dkv" + ("_acc" if n_prev else ""),
                         out_shape=[jax.ShapeDtypeStruct((n_rows, FOX_WIDTH), F32), jax.ShapeDtypeStruct((n_rows, FOX_WIDTH), F32),
                                    jax.ShapeDtypeStruct((FOX_HEADS, 1, n_rows), F32)],
                         grid=(FOX_HEADS, tables[0].shape[0]), tables=tables,
                         in_specs=[qaspec, kaspec, vspec, qspec, qspec, dospec] + [kspec, kspec, cspec][:n_prev],
                         out_specs=[kspec, kspec, cspec],
                         scratch=[pltpu.VMEM((tk, HEAD_DIM), F32), pltpu.VMEM((tk, HEAD_DIM), F32), pltpu.VMEM((1, tk), F32)],
                         sem=("parallel", "arbitrary"))(*tables, qa, ka, kv, o, lse, do, *(prev or ()))


def loss_head(h, gain, target, *, tile=512):
    n_rows, d = h.shape
    tile = min(tile, n_rows)

    def body(h_ref, g_ref, t_ref, part_ref, dy_ref):
        (y,) = f_rmsnorm(h_ref[...], g_ref[...])
        diff = y - t_ref[...]
        dy_ref[...] = diff * (1.0 / d)
        part = jnp.sum(diff * diff, axis=0, keepdims=True)
        first = pl.program_id(0) == 0

        @pl.when(first)
        def _():
            part_ref[...] = part

        @pl.when(jnp.logical_not(first))
        def _():
            part_ref[...] += part

    blk = pl.BlockSpec((tile, d), lambda i: (i, 0))
    one = pl.BlockSpec((1, d), lambda i: (0, 0))
    return _pcall(body, name="loss_head",
                  out_shape=[jax.ShapeDtypeStruct((1, d), F32), jax.ShapeDtypeStruct((n_rows, d), F32)],
                  grid=(n_rows // tile,), in_specs=[blk, one, blk], out_specs=[one, blk], sem=("arbitrary",))(h, gain, target)


def adamw(w, g, m, v, *, name):
    shape = w.shape
    cols = shape[-1] if w.ndim >= 2 else w.size
    rows = w.size // cols
    tile = _pick(rows, (256, 128, 64, 32, 16, 8))
    as2d = lambda a: a.reshape(rows, cols)

    def body(w_ref, g_ref, m_ref, v_ref, d_ref, nm_ref, nv_ref):
        g_ = g_ref[...]
        m_ = ADAM_B1 * m_ref[...] + (1.0 - ADAM_B1) * g_
        v_ = ADAM_B2 * v_ref[...] + (1.0 - ADAM_B2) * (g_ * g_)
        m_hat = m_ / (1.0 - ADAM_B1 ** ADAM_STEP)
        v_hat = v_ / (1.0 - ADAM_B2 ** ADAM_STEP)
        d_ref[...] = -ADAM_LR * (m_hat / (jnp.sqrt(v_hat) + ADAM_EPS) + ADAM_WD * w_ref[...])
        nm_ref[...] = m_
        nv_ref[...] = v_

    blk = pl.BlockSpec((tile, cols), lambda i: (i, 0))
    outs = _pcall(body, name=name, out_shape=[jax.ShapeDtypeStruct((rows, cols), F32)] * 3, grid=(rows // tile,),
                  in_specs=[blk] * 4, out_specs=[blk] * 3, sem=("parallel",))(as2d(w), as2d(g), as2d(m), as2d(v))
    return tuple(o.reshape(shape) for o in outs)


def sum_leading(a, *, name):
    p, r, c = a.shape
    tile = _pick(r, (256, 128, 64, 32, 16, 8))

    def body(a_ref, o_ref):
        total = a_ref[0].astype(F32)
        for k in range(1, p):
            total = total + a_ref[k].astype(F32)
        o_ref[...] = total

    return _pcall(body, name=name, out_shape=jax.ShapeDtypeStruct((r, c), F32), grid=(r // tile,),
                  in_specs=[pl.BlockSpec((p, tile, c), lambda i: (0, i, 0))],
                  out_specs=pl.BlockSpec((tile, c), lambda i: (i, 0)), sem=("parallel",))(a)


_HBM = pl.BlockSpec(memory_space=pltpu.HBM)


def _comm_call(body, *, name, out_shape, n_in, scratch):
    return pl.pallas_call(body, name=name, out_shape=out_shape, in_specs=[_HBM] * n_in, out_specs=_HBM,
                          scratch_shapes=scratch,
                          compiler_params=pltpu.CompilerParams(has_side_effects=True))


def all_gather8(a, *, name):
    m_per, n = a.shape

    def body(x_ref, out_ref, send_sems, recv_sems, local_sem):
        x, y, c = lax.axis_index("x"), lax.axis_index("y"), lax.axis_index("c")
        me, sibling = (x, y, c), (x, y, 1 - c)
        chips = [(1 - x, y), (x, 1 - y), (1 - x, 1 - y)]

        def rows(px, py, pc):
            return out_ref.at[pl.ds((4 * px + 2 * py + pc) * m_per, m_per), :]

        def copy(k, block, to, src=None):
            return pltpu.make_async_remote_copy(
                src_ref=rows(*block) if src is None else src, dst_ref=rows(*block),
                send_sem=send_sems.at[k], recv_sem=recv_sems.at[k], device_id=to, device_id_type=MESH)

        mine = pltpu.make_async_copy(x_ref, rows(*me), local_sem)
        mine.start()
        first = [copy(0, me, sibling, src=x_ref)]
        first += [copy(1 + j, me, (*chip, c), src=x_ref) for j, chip in enumerate(chips)]
        for cp in first:
            cp.start()
        passed = [copy(4 + j, (*chip, c), sibling) for j, chip in enumerate(chips)]
        for j, chip in enumerate(chips):
            copy(1 + j, (*chip, c), me).wait_recv()
            passed[j].start()
        copy(0, sibling, me).wait_recv()
        for j, chip in enumerate(chips):
            copy(4 + j, (*chip, 1 - c), me).wait_recv()
        for cp in first + passed:
            cp.wait_send()
        mine.wait()

    return _comm_call(body, name=name, out_shape=jax.ShapeDtypeStruct((N_DEV * m_per, n), a.dtype), n_in=1,
                      scratch=[pltpu.SemaphoreType.DMA((7,)), pltpu.SemaphoreType.DMA((7,)), pltpu.SemaphoreType.DMA])(a)


PACK_COLS = 1024
PACK_ROW_MULTIPLE = 32

WEIGHT_NAMES = ["ffn1_norm", "ffn1_w_gate_up", "ffn1_w_down", "mix_norm", "ffn2_norm", "ffn2_w_gate_up", "ffn2_w_down",
                "gdn_w_in", "gdn_conv", "gdn_A_log", "gdn_dt_bias", "gdn_out_norm", "fox_w_in", "w_out", "mem_norm",
                "mem_w_kv", "kv_norm", "kv_w", "kv_b_f", "final_norm"]
SHARDED = [("ffn1_w_gate_up", 2), ("ffn1_w_down", 1), ("ffn2_w_gate_up", 2), ("ffn2_w_down", 1), ("gdn_w_in", 2),
           ("gdn_conv", 2), ("fox_w_in", 1), ("w_out", 1), ("mem_w_kv", 1), ("kv_w", 0)]
REPLICATED = [n for n in WEIGHT_NAMES if n not in dict(SHARDED)]


PACK_PIECE_ROWS = 16


def _rows_of(size):
    return -(-size // (PACK_COLS * PACK_PIECE_ROWS)) * PACK_PIECE_ROWS


def pack(pieces, dtype, row_multiple=PACK_ROW_MULTIPLE):
    bufs, total = [], 0
    for p in pieces:
        flat = p.astype(dtype).reshape(-1)
        rows = _rows_of(flat.size)
        bufs.append(jnp.pad(flat, (0, rows * PACK_COLS - flat.size)).reshape(rows, PACK_COLS))
        total += rows
    pad = -total % row_multiple
    if pad:
        bufs.append(jnp.zeros((pad, PACK_COLS), dtype))
    return jnp.concatenate(bufs, axis=0)


def unpack(buf, shapes):
    out, row = [], 0
    for shape in shapes:
        size = 1
        for s in shape:
            size *= s
        rows = _rows_of(size)
        out.append(buf[row:row + rows].reshape(-1)[:size].reshape(shape))
        row += rows
    return out


def _row(vec, width=None):
    vec = vec.astype(F32).reshape(1, -1)
    if width is not None and vec.shape[1] < width:
        vec = jnp.pad(vec, ((0, 0), (0, width - vec.shape[1])))
    return vec


ROW_TILE = 1024
GDN_PROJ_WIDTH = 4 * GDN_WIDTH + MEM_WIDTH + LANES
GDN_Z_COL, GDN_QMEM_COL, GDN_AB_COL = 3, 4 * GDN_WIDTH // MEM_WIDTH, (4 * GDN_WIDTH + MEM_WIDTH) // LANES
FOX_QMEM_COL = FOX_WIDTH // MEM_WIDTH
KV_PAD_WIDTH = 2 * FOX_WIDTH + LANES


def rms_fwd(x, gain_row, out_dtype=BF16):
    return rowwise(f_rmsnorm, [x], [gain_row], [(x.shape[1], out_dtype)], tile=ROW_TILE, name="rms_fwd")[0]


def rms_bwd(x, gain_row, dy, dres=None):
    return rowwise_bwd(f_rmsnorm, [x], [gain_row], [dy], tile=ROW_TILE, name="rms_bwd", row_grads=[F32],
                       const_grads=[True], add=None if dres is None else {0: dres})


def _ffn_tiles(n_rows):
    return _pick(n_rows, (512, 256, 128)), _pick(FFN_HIDDEN, (1408, 256, 128))


def ffn_up_act(n, wgu):
    n_rows, d = n.shape
    tm, tn = _ffn_tiles(n_rows)
    nj = FFN_HIDDEN // tn

    def body(n_ref, wg_ref, wu_ref, gu_ref, act_ref):
        x = n_ref[...].astype(BF16)
        g = bdot.nn(x, wg_ref[...])
        u = bdot.nn(x, wu_ref[...])
        gu_ref[0] = g.astype(gu_ref.dtype)
        gu_ref[1] = u.astype(gu_ref.dtype)
        act_ref[...] = (_silu(g) * u).astype(act_ref.dtype)

    return _pcall(body, name="ffn_up_act",
                  out_shape=[jax.ShapeDtypeStruct((2, n_rows, FFN_HIDDEN), BF16), jax.ShapeDtypeStruct((n_rows, FFN_HIDDEN), BF16)],
                  grid=(nj, n_rows // tm),
                  in_specs=[pl.BlockSpec((tm, d), lambda j, i: (i, 0)), pl.BlockSpec((d, tn), lambda j, i: (0, j)),
                            pl.BlockSpec((d, tn), lambda j, i: (0, nj + j))],
                  out_specs=[pl.BlockSpec((2, tm, tn), lambda j, i: (0, i, j)), pl.BlockSpec((tm, tn), lambda j, i: (i, j))],
                  sem=("parallel", "parallel"))(n, wgu, wgu)


def ffn_down_dx_act(dh, wd, gu):
    n_rows, d = dh.shape
    tm, tn = _ffn_tiles(n_rows)

    def body(dh_ref, wd_ref, gu_ref, dgu_ref):
        dact = 0.5 * bdot.nt(dh_ref[...], wd_ref[...])
        gate, up = gu_ref[0].astype(F32), gu_ref[1].astype(F32)
        sg = _sigmoid(gate)
        dgu_ref[0] = (dact * up * (sg * (1.0 + gate * (1.0 - sg)))).astype(dgu_ref.dtype)
        dgu_ref[1] = (dact * (gate * sg)).astype(dgu_ref.dtype)

    blk = pl.BlockSpec((2, tm, tn), lambda j, i: (0, i, j))
    return _pcall(body, name="ffn_down_dx_act", out_shape=jax.ShapeDtypeStruct((2, n_rows, FFN_HIDDEN), BF16),
                  grid=(FFN_HIDDEN // tn, n_rows // tm),
                  in_specs=[pl.BlockSpec((tm, d), lambda j, i: (i, 0)), pl.BlockSpec((tn, d), lambda j, i: (j, 0)), blk],
                  out_specs=blk, sem=("parallel", "parallel"))(dh, wd, gu)


def ffn_fwd(h, gain_row, wgu, wd):
    n = rms_fwd(h, gain_row)
    gu, act = ffn_up_act(n, wgu)
    return mm(act, wd, scale=0.5, res=h, name="ffn_down"), (h, n, gu, act)


def ffn_bwd(dh, saved, gain_row, wgu, wd):
    h, n, gu, act = saved
    dgu = ffn_down_dx_act(dh, wd, gu)
    dwd = mm(act, dh, ta=True, scale=0.5, name="ffn_down_dw")
    dwgu = mm(n, dgu, ta=True, b_split=True, name="ffn_up_dw")
    dh, dgain = mm_rms_bwd(dgu, wgu, h, gain_row, dh, a_split=True, name="ffn_up_dx")
    return dh, dwgu, dwd, dgain


def gdn_fwd(proj, w8, a_row, dt_row, onorm_row):
    wide = [(GDN_WIDTH, F32)] * 5
    xc = conv_fwd(proj, w8, width=3 * GDN_WIDTH)
    q, k, v, gc, beta = rowwise(f_gdn_pre, [xc, (proj, LANES, GDN_AB_COL)], [a_row, dt_row], wide, tile=GDN_CHUNK,
                                name="gdn_pre_fwd")
    u, w, qk, kt, qh, inv = rowwise(f_gdn_intra, [q, k, v, gc, beta], [], wide + wide[:1], tile=GDN_CHUNK,
                                    name="gdn_intra_fwd")
    o, vn, sin = gdn_scan_fwd(u, w, qk, kt, qh, gc)
    main = rowwise(f_gdn_post, [o, (proj, GDN_WIDTH, GDN_Z_COL)], [onorm_row], [(GDN_WIDTH, BF16)], tile=ROW_TILE,
                   name="gdn_post_fwd")[0]
    return main, (xc, q, k, v, gc, beta, inv, w, qk, kt, qh, vn, sin, o)


def gdn_bwd(dmain, proj, saved, w8, a_row, dt_row, onorm_row):
    xc, q, k, v, gc, beta, inv, w, qk, kt, qh, vn, sin, o = saved
    do, dz, donorm = rowwise_bwd(f_gdn_post, [o, (proj, GDN_WIDTH, GDN_Z_COL)], [onorm_row], [dmain], tile=ROW_TILE,
                                 name="gdn_post_bwd", row_grads=[F32, BF16], const_grads=[True])
    du, dw, dqk, dkt, dqh, dgl = gdn_scan_bwd(do, w, qk, kt, qh, gc, vn, sin)
    dq, dk, dv, dgc, dbeta = gdn_intra_bwd(q, k, v, gc, beta, inv, (du, dw, dqk, dkt, dqh), dgl)
    dxc, dab, da, ddt = rowwise_bwd(f_gdn_pre, [xc, (proj, LANES, GDN_AB_COL)], [a_row, dt_row], [dq, dk, dv, dgc, dbeta],
                                    tile=GDN_CHUNK, name="gdn_pre_bwd", row_grads=[F32, BF16], const_grads=[True, True])
    dqkv, dw8 = conv_bwd(proj, w8, dxc, width=3 * GDN_WIDTH)
    return dqkv, dz, dab, dw8, da, ddt, donorm


def mem_fwd(q, kmem, vmem):
    return rowwise(f_mem_attn, [q], [kmem, vmem], [(MEM_WIDTH, BF16)], tile=ROW_TILE, name="mem_attn_fwd")[0]


def mem_bwd(q, kmem, vmem, dout):
    return rowwise_bwd(f_mem_attn, [q], [kmem, vmem], [dout], tile=ROW_TILE, name="mem_attn_bwd", row_grads=[BF16],
                       const_grads=[True, True])


def forward_backward(xs, mems, target, P):
    depth, n_a = 4, 2
    G = {}
    mem_gain = _row(P["mem_norm"])
    mem_n = rms_fwd(mems, mem_gain)
    h = xs
    saved = []
    shared = None
    for l in range(depth):
        h0 = h
        h1, s1 = ffn_fwd(h0, _row(P["ffn1_norm"][l]), P["ffn1_w_gate_up"][l], P["ffn1_w_down"][l])
        u = rms_fwd(h1, _row(P["mix_norm"][l]))
        kvm = mm(mem_n, P["mem_w_kv"][l], name="mem_kv")
        kmem, vmem = kvm[:, :MEM_WIDTH], kvm[:, MEM_WIDTH:]
        if l < n_a:
            gp = (P["conv8"][l], _row(P["gdn_A_log"][l], LANES), _row(P["gdn_dt_bias"][l], LANES), _row(P["gdn_out_norm"][l]))
            proj = mm(u, P["gdn_w_in_pad"][l], name="gdn_in")
            main, sm = gdn_fwd(proj, *gp)
            qm = (proj, MEM_WIDTH, GDN_QMEM_COL)
        else:
            proj = mm(u, P["fox_w_in"][l - n_a], out_dtype=BF16, name="fox_in")
            kv, ka, cb = shared
            qa = fox_augment(proj, cb, True)
            main, lse = fox_fwd(qa, kv, ka)
            sm = (main, lse, qa)
            qm = (proj, MEM_WIDTH, FOX_QMEM_COL)
        mo = mem_fwd(qm, kmem, vmem)
        cat = jnp.concatenate([main, mo], axis=1)
        h2 = mm(cat, P["w_out"][l], res=h1, name="mix_out")
        h3, s2 = ffn_fwd(h2, _row(P["ffn2_norm"][l]), P["ffn2_w_gate_up"][l], P["ffn2_w_down"][l])
        saved.append((s1, h1, u, kmem, vmem, proj, sm, qm, cat, s2))
        h = h3
        if l == n_a - 1:
            nkv = rms_fwd(h, _row(P["kv_norm"]))
            kv = mm(nkv, P["kv_w_pad"][:, :2 * FOX_WIDTH], out_dtype=BF16, name="fox_kv")
            f = mm(nkv, P["kv_w_pad"][:, 2 * FOX_WIDTH:], name="fox_f")
            bf_row = _row(P["kv_b_f"], LANES)
            cb = fox_gate_fwd(f, bf_row)
            shared = (kv, fox_augment(kv, cb, False), cb)
            kv_saved = (h, nkv, f, bf_row)

    part, dy = loss_head(h, _row(P["final_norm"]), target)
    dh, G["final_norm"] = rms_bwd(h, _row(P["final_norm"]), dy)

    per_layer = {n: [None] * depth for n in ("ffn1_norm", "ffn1_w_gate_up", "ffn1_w_down", "mix_norm", "ffn2_norm",
                                             "ffn2_w_gate_up", "ffn2_w_down", "w_out", "mem_w_kv")}
    gdn_g = {n: [None] * n_a for n in ("gdn_w_in_pad", "conv8", "gdn_A_log", "gdn_dt_bias", "gdn_out_norm")}
    fox_g = [None] * (depth - n_a)
    dmem_n = None
    dkv_acc = dcb_acc = None
    for l in reversed(range(depth)):
        s1, h1, u, kmem, vmem, proj, sm, qm, cat, s2 = saved[l]
        if l == n_a - 1:
            hk, nkv, f, bf_row = kv_saved
            dk, dv, dcrow = dkv_acc
            df, dbf = fox_gate_bwd(f, bf_row, dcrow, dcb_acc)
            dp = jnp.concatenate([dk.astype(BF16), dv.astype(BF16), df.astype(BF16)], axis=1)
            G["kv_w_pad"] = mm(nkv, dp, ta=True, name="fox_kv_dw")
            G["kv_b_f"] = dbf
            dh, G["kv_norm"] = mm_rms_bwd(dp, P["kv_w_pad"], hk, _row(P["kv_norm"]), dh, name="fox_kv_dx")
        dh, per_layer["ffn2_w_gate_up"][l], per_layer["ffn2_w_down"][l], per_layer["ffn2_norm"][l] = ffn_bwd(
            dh, s2, _row(P["ffn2_norm"][l]), P["ffn2_w_gate_up"][l], P["ffn2_w_down"][l])
        dcat = mm(dh, P["w_out"][l], tb=True, out_dtype=BF16, name="mix_out_dx")
        per_layer["w_out"][l] = mm(cat, dh, ta=True, name="mix_out_dw")
        dqm, dkm, dvm = mem_bwd(qm, kmem, vmem, (dcat, MEM_WIDTH, FOX_QMEM_COL))
        dkvm = jnp.concatenate([dkm, dvm], axis=1)
        per_layer["mem_w_kv"][l] = mm(mem_n, dkvm, ta=True, name="mem_kv_dw")
        dmem_n = mm(dkvm, P["mem_w_kv"][l], tb=True, res=dmem_n, name="mem_kv_dx")
        dmain = (dcat, GDN_WIDTH, 0)
        if l < n_a:
            gp = (P["conv8"][l], _row(P["gdn_A_log"][l], LANES), _row(P["gdn_dt_bias"][l], LANES), _row(P["gdn_out_norm"][l]))
            dqkv, dz, dab, gdn_g["conv8"][l], gdn_g["gdn_A_log"][l], gdn_g["gdn_dt_bias"][l], gdn_g["gdn_out_norm"][l] = gdn_bwd(
                dmain, proj, sm, *gp)
            dproj = jnp.concatenate([dqkv, dz, dqm, dab], axis=1)
            gdn_g["gdn_w_in_pad"][l] = mm(u, dproj, ta=True, name="gdn_in_dw")
            w_in = P["gdn_w_in_pad"][l]
        else:
            o, lse, qa = sm
            kv, ka, _ = shared
            dq, dcb_acc = fox_bwd_dq(qa, kv, ka, o, lse, dmain, dcb_acc)
            dkv_acc = fox_bwd_dkv(qa, kv, ka, o, lse, dmain, dkv_acc)
            dproj = jnp.concatenate([dq, dqm], axis=1)
            fox_g[l - n_a] = mm(u, dproj, ta=True, name="fox_in_dw")
            w_in = P["fox_w_in"][l - n_a]
        dh, per_layer["mix_norm"][l] = mm_rms_bwd(dproj, w_in, h1, _row(P["mix_norm"][l]), dh, name="mix_in_dx")
        dh, per_layer["ffn1_w_gate_up"][l], per_layer["ffn1_w_down"][l], per_layer["ffn1_norm"][l] = ffn_bwd(
            dh, s1, _row(P["ffn1_norm"][l]), P["ffn1_w_gate_up"][l], P["ffn1_w_down"][l])

    (G["mem_norm"],) = rowwise_bwd(f_rmsnorm, [mems], [mem_gain], [dmem_n], tile=ROW_TILE, name="mem_norm_bwd",
                                   row_grads=[None], const_grads=[True])
    for n, v in per_layer.items():
        G[n] = jnp.stack(v)
    for n, v in gdn_g.items():
        G[n] = jnp.stack(v)
    G["fox_w_in"] = jnp.stack(fox_g)
    return part, dh, G


_GDN_O0 = 4 * GDN_WIDTH
_GDN_O1 = _GDN_O0 + 2 * GDN_HEADS
_KV_WIDTH = 2 * FOX_WIDTH + FOX_HEADS


def derived_weights(gdn_w_in, gdn_conv, kv_w=None):
    zeros = jnp.zeros(gdn_w_in.shape[:-1] + (LANES - 2 * GDN_HEADS,), gdn_w_in.dtype)
    out = dict(
        gdn_w_in_pad=jnp.concatenate([gdn_w_in[..., :_GDN_O0], gdn_w_in[..., _GDN_O1:], gdn_w_in[..., _GDN_O0:_GDN_O1], zeros], axis=-1),
        conv8=jnp.pad(gdn_conv.astype(F32), ((0, 0), (0, 8 - CONV_WIDTH), (0, 0))))
    if kv_w is not None:
        out["kv_w_pad"] = jnp.pad(kv_w, ((0, 0), (0, KV_PAD_WIDTH - _KV_WIDTH)))
    return out


def reference_layout(G):
    gp = G["gdn_w_in_pad"]
    out = dict(G)
    out["gdn_w_in"] = jnp.concatenate([gp[..., :_GDN_O0], gp[..., _GDN_O0 + MEM_WIDTH:_GDN_O0 + MEM_WIDTH + 2 * GDN_HEADS],
                                       gp[..., _GDN_O0:_GDN_O0 + MEM_WIDTH]], axis=-1)
    out["gdn_conv"] = G["conv8"][:, :CONV_WIDTH]
    out["kv_w"] = G["kv_w_pad"][:, :_KV_WIDTH]
    out["gdn_A_log"] = G["gdn_A_log"][:, 0, :GDN_HEADS]
    out["gdn_dt_bias"] = G["gdn_dt_bias"][:, 0, :GDN_HEADS]
    out["gdn_out_norm"] = G["gdn_out_norm"][:, 0, :]
    out["kv_b_f"] = G["kv_b_f"][0, :FOX_HEADS]
    for n in ("ffn1_norm", "mix_norm", "ffn2_norm"):
        out[n] = G[n][:, 0, :]
    for n in ("mem_norm", "kv_norm", "final_norm"):
        out[n] = G[n][0]
    return {n: out[n] for n in WEIGHT_NAMES}


EXCHANGED = [("ffn1_w_gate_up", 2, 0), ("ffn1_w_down", 1, 0), ("ffn2_w_gate_up", 2, 0), ("ffn2_w_down", 1, 0),
             ("gdn_w_in", 2, 0), ("fox_w_in", 1, 0), ("w_out", 1, 0), ("mem_w_kv", 1, 0), ("kv_w", 0, 1)]
GDN_IN_SHARD = (4 * GDN_WIDTH + 2 * GDN_HEADS + MEM_WIDTH) // N_CHIPS
GDN_IN_SLOT = 896


def _slab(ref, axis_slices):
    idx = [slice(None)] * len(ref.shape)
    for axis, (start, size) in axis_slices.items():
        idx[axis] = pl.ds(start, size)
    return ref.at[tuple(idx)]


def _comm_multi(body, *, name, n_in, out_shapes, scratch):
    return pl.pallas_call(body, name=name, out_shape=out_shapes, in_specs=[_HBM] * n_in, out_specs=[_HBM] * len(out_shapes),
                          scratch_shapes=scratch, compiler_params=pltpu.CompilerParams(has_side_effects=True))


def gather_shards(shards, layout):
    n = len(shards)
    fulls = [tuple(d * (N_CHIPS if a == sa else 1) for a, d in enumerate(s.shape)) for s, (sa, _) in zip(shards, layout)]
    n_sem = 9

    def pieces(w):
        sa, ha = layout[w]
        axis = 3 - sa - ha
        size = shards[w].shape[axis]
        unit = LANES if axis == 2 else 16
        first = -(-(size // 2) // unit) * unit
        return axis, [(0, first), (first, size - first)]

    def body(*refs):
        ins, outs = refs[:n], refs[n:2 * n]
        send_sems, recv_sems, local_sems = refs[2 * n:]
        x, y, c = lax.axis_index("x"), lax.axis_index("y"), lax.axis_index("c")
        me, sibling, x_nbr, y_nbr = (x, y, c), (x, y, 1 - c), (1 - x, y, c), (x, 1 - y, c)
        chip_x, chip_y, chip_d = (1 - x, y), (x, 1 - y), (1 - x, 1 - y)

        def region(w, chip, pc, piece=None):
            (sa, ha), shard = layout[w], shards[w].shape
            where = {sa: ((2 * chip[0] + chip[1]) * shard[sa], shard[sa]), ha: (pc * (shard[ha] // 2), shard[ha] // 2)}
            if piece is not None:
                axis, parts = pieces(w)
                where[axis] = parts[piece]
            return _slab(outs[w], where)

        def my_half(w):
            ha, shard = layout[w][1], shards[w].shape
            return _slab(ins[w], {ha: (c * (shard[ha] // 2), shard[ha] // 2)})

        def copy(w, k, where, to, src=None):
            return pltpu.make_async_remote_copy(
                src_ref=where if src is None else src, dst_ref=where, send_sem=send_sems.at[n_sem * w + k],
                recv_sem=recv_sems.at[n_sem * w + k], device_id=to, device_id_type=MESH)

        mine, sends = [], [[] for _ in range(n)]
        for w in range(n):
            mine.append(pltpu.make_async_copy(my_half(w), region(w, (x, y), c), local_sems.at[w]))
            mine[w].start()
            sends[w] = [copy(w, k, region(w, (x, y), c), to, src=my_half(w)) for k, to in enumerate((sibling, x_nbr, y_nbr))]
            for cp in sends[w]:
                cp.start()
        for w in range(n):
            copy(w, 1, region(w, chip_x, c), me).wait_recv()
            onward = [copy(w, 3, region(w, chip_x, c, 0), y_nbr), copy(w, 5, region(w, chip_x, c), sibling)]
            for cp in onward:
                cp.start()
            sends[w] += onward
            copy(w, 2, region(w, chip_y, c), me).wait_recv()
            onward = [copy(w, 4, region(w, chip_y, c, 1), x_nbr), copy(w, 6, region(w, chip_y, c), sibling)]
            for cp in onward:
                cp.start()
            sends[w] += onward
        for w in range(n):
            copy(w, 3, region(w, chip_d, c, 0), me).wait_recv()
            copy(w, 4, region(w, chip_d, c, 1), me).wait_recv()
            onward = [copy(w, 7, region(w, chip_d, c, 0), sibling), copy(w, 8, region(w, chip_d, c, 1), sibling)]
            for cp in onward:
                cp.start()
            sends[w] += onward
        for w in range(n):
            copy(w, 0, region(w, (x, y), 1 - c), me).wait_recv()
            copy(w, 5, region(w, chip_x, 1 - c), me).wait_recv()
            copy(w, 6, region(w, chip_y, 1 - c), me).wait_recv()
            copy(w, 7, region(w, chip_d, 1 - c, 0), me).wait_recv()
            copy(w, 8, region(w, chip_d, 1 - c, 1), me).wait_recv()
        for w in range(n):
            for cp in sends[w]:
                cp.wait_send()
            mine[w].wait()

    return _comm_multi(body, name="gather_shards", n_in=n,
                       out_shapes=[jax.ShapeDtypeStruct(f, s.dtype) for f, s in zip(fulls, shards)],
                       scratch=[pltpu.SemaphoreType.DMA((n_sem * n,)), pltpu.SemaphoreType.DMA((n_sem * n,)),
                                pltpu.SemaphoreType.DMA((n,))])(*shards)


def swap_other_halves(arrays, layout):
    n = len(arrays)
    halves = [tuple(d // 2 if a == ha else d for a, d in enumerate(g.shape)) for g, (_, ha) in zip(arrays, layout)]

    def body(*refs):
        ins, outs = refs[:n], refs[n:2 * n]
        send_sems, recv_sems = refs[2 * n:]
        x, y, c = lax.axis_index("x"), lax.axis_index("y"), lax.axis_index("c")
        copies = []
        for w in range(n):
            ha, size = layout[w][1], halves[w][layout[w][1]]
            copies.append(pltpu.make_async_remote_copy(
                src_ref=_slab(ins[w], {ha: ((1 - c) * size, size)}), dst_ref=outs[w], send_sem=send_sems.at[w],
                recv_sem=recv_sems.at[w], device_id=(x, y, 1 - c), device_id_type=MESH))
            copies[w].start()
        for cp in copies:
            cp.wait()

    return _comm_multi(body, name="grad_pair_swap", n_in=n,
                       out_shapes=[jax.ShapeDtypeStruct(h, g.dtype) for h, g in zip(halves, arrays)],
                       scratch=[pltpu.SemaphoreType.DMA((n,)), pltpu.SemaphoreType.DMA((n,))])(*arrays)


def scatter_to_chips(arrays, layout):
    n = len(arrays)
    slabs = [tuple(d // N_CHIPS if a == sa else d for a, d in enumerate(p.shape)) for p, (sa, _) in zip(arrays, layout)]

    def body(*refs):
        ins, outs = refs[:n], refs[n:2 * n]
        send_sems, recv_sems, local_sems = refs[2 * n:]
        x, y, c = lax.axis_index("x"), lax.axis_index("y"), lax.axis_index("c")
        me = 2 * x + y

        def slab(w, k):
            sa, size = layout[w][0], slabs[w][layout[w][0]]
            return _slab(ins[w], {sa: (k * size, size)})

        local, copies = [], []
        for w in range(n):
            local.append(pltpu.make_async_copy(slab(w, me), outs[w].at[me], local_sems.at[w]))
            local[w].start()
            for j, (px, py) in enumerate([(1 - x, y), (x, 1 - y), (1 - x, 1 - y)]):
                copies.append(pltpu.make_async_remote_copy(
                    src_ref=slab(w, 2 * px + py), dst_ref=outs[w].at[me], send_sem=send_sems.at[3 * w + j],
                    recv_sem=recv_sems.at[3 * w + j], device_id=(px, py, c), device_id_type=MESH))
                copies[-1].start()
        for cp in copies:
            cp.wait()
        for cp in local:
            cp.wait()

    return _comm_multi(body, name="grad_all_to_all", n_in=n,
                       out_shapes=[jax.ShapeDtypeStruct((N_CHIPS,) + s, p.dtype) for s, p in zip(slabs, arrays)],
                       scratch=[pltpu.SemaphoreType.DMA((3 * n,)), pltpu.SemaphoreType.DMA((3 * n,)),
                                pltpu.SemaphoreType.DMA((n,))])(*arrays)


def swap_with_sibling(arrays):
    n = len(arrays)

    def body(*refs):
        ins, outs = refs[:n], refs[n:2 * n]
        send_sems, recv_sems = refs[2 * n:]
        x, y, c = lax.axis_index("x"), lax.axis_index("y"), lax.axis_index("c")
        copies = [pltpu.make_async_remote_copy(src_ref=ins[w], dst_ref=outs[w], send_sem=send_sems.at[w], recv_sem=recv_sems.at[w],
                                               device_id=(x, y, 1 - c), device_id_type=MESH) for w in range(n)]
        for cp in copies:
            cp.start()
        for cp in copies:
            cp.wait()

    return _comm_multi(body, name="grad_half_swap", n_in=n, out_shapes=[jax.ShapeDtypeStruct(a.shape, a.dtype) for a in arrays],
                       scratch=[pltpu.SemaphoreType.DMA((n,)), pltpu.SemaphoreType.DMA((n,))])(*arrays)


def join_halves(mine, other, half_axis, c_arr, *, name):
    a0, a1, a2 = mine.shape
    tile = _row_tile(a1, a2)

    def body(c_ref, m_ref, o_ref, out_ref):
        for half in range(2):
            @pl.when(c_ref[0] == half)
            def _(half=half):
                out_ref[half] = m_ref[...]
                out_ref[1 - half] = o_ref[...]

    blk = pl.BlockSpec((None, tile, a2), lambda i, j, c_ref: (i, j, 0))
    if half_axis == 0:
        out_shape, out_blk = (2, a0, a1, a2), pl.BlockSpec((2, None, tile, a2), lambda i, j, c_ref: (0, i, j, 0))
    else:
        out_shape, out_blk = (a0, 2, a1, a2), pl.BlockSpec((None, 2, tile, a2), lambda i, j, c_ref: (i, 0, j, 0))
    spec = pltpu.PrefetchScalarGridSpec(num_scalar_prefetch=1, grid=(a0, a1 // tile), in_specs=[blk, blk], out_specs=out_blk)
    out = pl.pallas_call(body, name=name, out_shape=jax.ShapeDtypeStruct(out_shape, mine.dtype), grid_spec=spec,
                         compiler_params=pltpu.CompilerParams(vmem_limit_bytes=VMEM_LIMIT_BYTES,
                                                              dimension_semantics=("parallel", "parallel")))(c_arr, mine, other)
    return out.reshape((2 * a0, a1, a2) if half_axis == 0 else (a0, 2 * a1, a2))


def _row_tile(rows, cols, itemsize=4, budget=2 * 1024 * 1024):
    for t in (1024, 512, 256, 128, 64, 32, 16):
        if rows % t == 0 and t * cols * itemsize <= budget:
            return t
    return rows


def add_own_half(full, recv, half_axis, c_arr, *, name):
    a0, a1, a2 = recv.shape
    tile = _row_tile(a1, a2)

    def body(c_ref, f_ref, r_ref, o_ref):
        o_ref[...] = (f_ref[...] + r_ref[...]).astype(o_ref.dtype)

    if half_axis == 0:
        f_spec = pl.BlockSpec((None, tile, a2), lambda i, j, c_ref: (c_ref[0] * a0 + i, j, 0))
    else:
        f_spec = pl.BlockSpec((None, tile, a2), lambda i, j, c_ref: (i, c_ref[0] * (a1 // tile) + j, 0))
    blk = pl.BlockSpec((None, tile, a2), lambda i, j, c_ref: (i, j, 0))
    spec = pltpu.PrefetchScalarGridSpec(num_scalar_prefetch=1, grid=(a0, a1 // tile), in_specs=[f_spec, blk], out_specs=blk)
    return pl.pallas_call(body, name=name, out_shape=jax.ShapeDtypeStruct(recv.shape, BF16), grid_spec=spec,
                          compiler_params=pltpu.CompilerParams(vmem_limit_bytes=VMEM_LIMIT_BYTES,
                                                               dimension_semantics=("parallel", "parallel")))(c_arr, full, recv)


def sum_slots(q, *, name):
    _, a0, a1, a2 = q.shape
    tile = _row_tile(a1, a2, budget=1024 * 1024)

    def body(q_ref, o_ref):
        total = q_ref[0].astype(F32)
        for k in range(1, N_CHIPS):
            total = total + q_ref[k].astype(F32)
        o_ref[...] = total

    return _pcall(body, name=name, out_shape=jax.ShapeDtypeStruct((a0, a1, a2), F32), grid=(a0, a1 // tile),
                  in_specs=[pl.BlockSpec((N_CHIPS, None, tile, a2), lambda i, j: (0, i, j, 0))],
                  out_specs=pl.BlockSpec((None, tile, a2), lambda i, j: (i, j, 0)), sem=("parallel", "parallel"))(q)


def gather_weights(W):
    shards = []
    for name, _, _ in EXCHANGED:
        w = W[name].astype(BF16)
        if name == "gdn_w_in":
            w = jnp.pad(w, ((0, 0), (0, 0), (0, GDN_IN_SLOT - GDN_IN_SHARD)))
        if name == "kv_w":
            w = jnp.pad(w, ((0, 0), (0, KV_PAD_WIDTH - _KV_WIDTH)))[None]
        shards.append(w)
    fulls = dict(zip([n for n, _, _ in EXCHANGED], gather_shards(shards, [(sa, ha) for _, sa, ha in EXCHANGED])))
    slots = fulls["gdn_w_in"]
    fulls["gdn_w_in"] = jnp.concatenate([slots[..., k * GDN_IN_SLOT:k * GDN_IN_SLOT + GDN_IN_SHARD] for k in range(N_CHIPS)], axis=-1)
    fulls["kv_w_pad"] = fulls.pop("kv_w").reshape(-1, KV_PAD_WIDTH)
    conv = pack([W["gdn_conv"]], F32, row_multiple=8)
    conv_all = all_gather8(conv, name="gather_conv").reshape(N_DEV, conv.shape[0], PACK_COLS)
    fulls["gdn_conv"] = jnp.concatenate([unpack(conv_all[2 * k], [W["gdn_conv"].shape])[0] for k in range(N_CHIPS)], axis=-1)
    return fulls


def reduce_gradients(G):
    layout = [(sa, ha) for _, sa, ha in EXCHANGED]
    c_arr = lax.axis_index("c").astype(jnp.int32).reshape(1)
    received = swap_other_halves(G, layout)
    pairs = [add_own_half(g, r, ha, c_arr, name="grad_pair_sum") for g, r, (_, ha) in zip(G, received, layout)]
    slots = scatter_to_chips(pairs, layout)
    halves = [sum_slots(q, name="grad_chip_sum") for q in slots]
    others = swap_with_sibling(halves)
    return [join_halves(h, o, ha, c_arr, name="grad_join_halves") for h, o, (_, ha) in zip(halves, others, layout)]


def allreduce_small(G, names):
    packed = pack([G[n] for n in names], F32, row_multiple=8)
    gathered = all_gather8(packed, name="gather_small_grads").reshape(N_DEV, packed.shape[0], PACK_COLS)
    total = sum_leading(gathered, name="sum_small_grads")
    return dict(zip(names, unpack(total, [G[n].shape for n in names])))


def kernel(x, mem, *rest):
    n_w = len(WEIGHT_NAMES)
    W = dict(zip(WEIGHT_NAMES, rest[:n_w]))
    target = rest[n_w]
    M = dict(zip(WEIGHT_NAMES, rest[n_w + 1:2 * n_w + 1]))
    V = dict(zip(WEIGHT_NAMES, rest[2 * n_w + 1:3 * n_w + 1]))

    full = gather_weights(W)
    P = {n: W[n] for n in REPLICATED}
    P.update({n: full[n] for n in ("ffn1_w_gate_up", "ffn1_w_down", "ffn2_w_gate_up", "ffn2_w_down", "fox_w_in", "w_out",
                                   "mem_w_kv", "kv_w_pad")})
    derived = derived_weights(full["gdn_w_in"], full["gdn_conv"])
    P.update(gdn_w_in_pad=derived["gdn_w_in_pad"], conv8=derived["conv8"])

    part, dx, G = forward_backward(x[0], mem[0], target[0], P)
    loss = lax.psum(0.5 / x.shape[-1] * jnp.sum(part), ("x", "y", "c"))

    ref = reference_layout(G)
    exchange = {n: ref[n] for n, _, _ in EXCHANGED}
    exchange["gdn_w_in"] = jnp.concatenate(
        [jnp.pad(ref["gdn_w_in"][..., k * GDN_IN_SHARD:(k + 1) * GDN_IN_SHARD], ((0, 0), (0, 0), (0, GDN_IN_SLOT - GDN_IN_SHARD)))
         for k in range(N_CHIPS)], axis=-1)
    exchange["kv_w"] = G["kv_w_pad"].reshape(N_CHIPS, -1, KV_PAD_WIDTH)
    shards = dict(zip([n for n, _, _ in EXCHANGED], reduce_gradients([exchange[n] for n, _, _ in EXCHANGED])))
    shards["gdn_w_in"] = shards["gdn_w_in"][..., :GDN_IN_SHARD]
    shards["kv_w"] = shards["kv_w"][0, :, :_KV_WIDTH]
    grads = allreduce_small(ref, REPLICATED + ["gdn_conv"])
    conv_cols = W["gdn_conv"].shape[-1]
    chip = 2 * lax.axis_index("x") + lax.axis_index("y")
    grads["gdn_conv"] = lax.dynamic_slice_in_dim(grads["gdn_conv"], chip * conv_cols, conv_cols, axis=2)
    grads.update(shards)

    outs = {n: adamw(W[n], grads[n], M[n], V[n], name="adamw_" + n) for n in WEIGHT_NAMES}
    return (loss, dx[None], *[grads[n] for n in WEIGHT_NAMES], *[outs[n][0] for n in WEIGHT_NAMES],
            *[outs[n][1] for n in WEIGHT_NAMES], *[outs[n][2] for n in WEIGHT_NAMES])
```

```python
import jax
import jax.numpy as jnp
from jax import lax
from jax.experimental import pallas as pl
from jax.experimental.pallas import tpu as pltpu

F32, BF16 = jnp.float32, jnp.bfloat16
HI = lax.Precision.HIGHEST
MESH = pl.DeviceIdType.MESH

VMEM_LIMIT_BYTES = 56 * 1024 * 1024
LANES = 128
EPS = 1e-6
NEG_INF = -1e30

D_MODEL = 1024
HEAD_DIM = 128
GDN_HEADS = 6
GDN_WIDTH = GDN_HEADS * HEAD_DIM
FOX_HEADS = 6
FOX_WIDTH = FOX_HEADS * HEAD_DIM
MEM_HEADS = 4
MEM_HEAD_DIM = 64
MEM_WIDTH = MEM_HEADS * MEM_HEAD_DIM
FFN_HIDDEN = 2816
CONV_WIDTH = 4
GDN_CHUNK = 128
N_CHIPS = 4
N_DEV = 8

ADAM_LR, ADAM_B1, ADAM_B2, ADAM_EPS, ADAM_WD, ADAM_STEP = 0.001, 0.9, 0.999, 1e-08, 0.01, 10


def _pcall(body, *, name, out_shape, grid=(), in_specs=None, out_specs=None, scratch=(), sem=None):
    params = dict(vmem_limit_bytes=VMEM_LIMIT_BYTES)
    if sem is not None:
        params["dimension_semantics"] = sem
    kw = dict(grid=grid, in_specs=in_specs, out_specs=out_specs) if grid else {}
    return pl.pallas_call(body, name=name, out_shape=out_shape, scratch_shapes=list(scratch),
                          compiler_params=pltpu.CompilerParams(**params), **kw)


def _pick(n, cands):
    for c in cands:
        if n % c == 0:
            return c
    return n


def _make_dot(dtype, precision):
    def raw(a, b, dims):
        return lax.dot_general(a.astype(dtype), b.astype(dtype), (dims, ((), ())),
                               precision=precision, preferred_element_type=F32)

    @jax.custom_vjp
    def dot(a, b):
        return raw(a, b, ((1,), (0,)))

    def fwd(a, b):
        return dot(a, b), (a, b)

    def bwd(resid, ct):
        a, b = resid
        return raw(ct, b, ((1,), (1,))).astype(a.dtype), raw(a, ct, ((0,), (0,))).astype(b.dtype)

    dot.defvjp(fwd, bwd)
    dot.nn = lambda a, b: raw(a, b, ((1,), (0,)))
    dot.nt = lambda a, b: raw(a, b, ((1,), (1,)))
    dot.tn = lambda a, b: raw(a, b, ((0,), (0,)))
    return dot


bdot = _make_dot(BF16, None)
fdot = _make_dot(F32, HI)
idot = _make_dot(F32, lax.Precision.HIGH)
sdot = idot


def _sigmoid(x):
    return 0.5 * jnp.tanh(0.5 * x) + 0.5


def _silu(x):
    return x * _sigmoid(x)


def _softplus(x):
    return jnp.maximum(x, 0.0) + jnp.log(1.0 + jnp.exp(-jnp.abs(x)))


def _log_sigmoid(x):
    return -_softplus(-x)


def _iota2(shape, dim):
    return lax.broadcasted_iota(jnp.int32, shape, dim)


def mm(a, b, *, ta=False, tb=False, a_split=False, b_split=False, out_dtype=F32, scale=1.0, res=None, name):
    assert not (a_split and ta) and not (b_split and tb)
    (K, M) = a.shape if ta else ((2 * a.shape[2], a.shape[1]) if a_split else a.shape[::-1])
    (N, Kb) = b.shape if tb else ((2 * b.shape[2], b.shape[1]) if b_split else b.shape[::-1])
    assert K == Kb, (a.shape, b.shape, ta, tb)
    tm = _pick(M, (1024, 1408, 512, 256, 128))
    tn = _pick(N, (1024, 1408, 1152, 1664, 768, 512, 384, 256, 128))
    deep = a.dtype == BF16 and b.dtype == BF16
    if ta:
        tk = _pick(K, (2048, 1024, 512, 256, 128) if deep else (1024, 512, 256, 128))
    else:
        tk = _pick(K, (1024, 2816, 1408, 1152, 1664, 512, 256, 128) if deep else (1024, 1408, 1152, 1664, 512, 256, 128))
    assert not a_split or (K // 2) % tk == 0
    assert not b_split or (N // 2) % tn == 0
    nk = K // tk
    dims = (((0 if ta else 1,), (1 if tb else 0,)), ((), ()))

    def body(a_ref, b_ref, *rest):
        o_ref, acc = rest[-2], rest[-1]
        k = pl.program_id(2)

        @pl.when(k == 0)
        def _():
            acc[...] = jnp.zeros_like(acc)

        acc[...] += lax.dot_general(a_ref[...].astype(BF16), b_ref[...].astype(BF16), dims,
                                    preferred_element_type=F32)

        @pl.when(k == nk - 1)
        def _():
            out = acc[...] * scale
            if res is not None:
                out = out + rest[0][...].astype(F32)
            o_ref[...] = out.astype(o_ref.dtype)

    a_spec = pl.BlockSpec((tk, tm), lambda i, j, k: (k, i)) if ta else pl.BlockSpec((tm, tk), lambda i, j, k: (i, k))
    b_spec = pl.BlockSpec((tn, tk), lambda i, j, k: (j, k)) if tb else pl.BlockSpec((tk, tn), lambda i, j, k: (k, j))
    if a_split:
        per_half = K // 2 // tk
        a_spec = pl.BlockSpec((None, tm, tk), lambda i, j, k: (k // per_half, i, k % per_half))
    if b_split:
        per_half = N // 2 // tn
        b_spec = pl.BlockSpec((None, tk, tn), lambda i, j, k: (j // per_half, k, j % per_half))
    o_spec = pl.BlockSpec((tm, tn), lambda i, j, k: (i, j))
    ins, specs = [a, b], [a_spec, b_spec]
    if res is not None:
        ins.append(res)
        specs.append(o_spec)
    return _pcall(body, name=name, out_shape=jax.ShapeDtypeStruct((M, N), out_dtype),
                  grid=(M // tm, N // tn, nk), in_specs=specs, out_specs=o_spec,
                  scratch=[pltpu.VMEM((tm, tn), F32)], sem=("parallel", "parallel", "arbitrary"))(*ins)


def mm_rms_bwd(a, b, x, gain_row, dres, *, a_split=False, name):
    (K, M) = (2 * a.shape[2], a.shape[1]) if a_split else a.shape[::-1]
    D = b.shape[0]
    assert b.shape[1] == K and x.shape == (M, D)
    tm = _pick(M, (1024, 512, 256, 128))
    rows = _pick(tm, (256, 128))
    tk = _pick(K, (1024, 1408, 1152, 1664, 512, 256, 128))
    assert not a_split or (K // 2) % tk == 0
    nk = K // tk

    def body(a_ref, b_ref, x_ref, g_ref, r_ref, o_ref, dg_ref, o16_ref, acc):
        i, k = pl.program_id(0), pl.program_id(1)

        @pl.when(k == 0)
        def _():
            acc[...] = jnp.zeros_like(acc)

        acc[...] += bdot.nt(a_ref[...], b_ref[...])

        @pl.when(k == nk - 1)
        def _():
            dg = jnp.zeros((1, D), F32)
            for s in range(tm // rows):
                sl = slice(s * rows, (s + 1) * rows)
                xv = x_ref[sl, :]
                rstd = lax.rsqrt(jnp.mean(xv * xv, axis=-1, keepdims=True) + EPS)
                xh = xv * rstd
                dn = acc[sl, :]
                dy = dn * g_ref[...]
                total = (dy - xh * jnp.mean(dy * xh, axis=-1, keepdims=True)) * rstd + r_ref[sl, :]
                o_ref[sl, :] = total
                o16_ref[sl, :] = total.astype(BF16)
                dg = dg + jnp.sum(dn * xh, axis=0, keepdims=True)

            @pl.when(i == 0)
            def _():
                dg_ref[...] = dg

            @pl.when(i > 0)
            def _():
                dg_ref[...] += dg

    a_spec = pl.BlockSpec((tm, tk), lambda i, k: (i, k))
    if a_split:
        per_half = K // 2 // tk
        a_spec = pl.BlockSpec((None, tm, tk), lambda i, k: (k // per_half, i, k % per_half))
    row = pl.BlockSpec((tm, D), lambda i, k: (i, 0))
    one = pl.BlockSpec((1, D), lambda i, k: (0, 0))
    d32, dgain, d16 = _pcall(
        body, name=name,
        out_shape=[jax.ShapeDtypeStruct((M, D), F32), jax.ShapeDtypeStruct((1, D), F32), jax.ShapeDtypeStruct((M, D), BF16)],
        grid=(M // tm, nk), in_specs=[a_spec, pl.BlockSpec((D, tk), lambda i, k: (0, k)), row, one, row],
        out_specs=[row, one, row], scratch=[pltpu.VMEM((tm, D), F32)], sem=("arbitrary", "arbitrary"))(a, b, x, gain_row, dres)
    return (d32, d16), dgain


def _row_spec(r, tile):
    if isinstance(r, tuple):
        arr, width, col = r
        return arr, pl.BlockSpec((tile, width), lambda i, col=col: (i, col))
    return r, pl.BlockSpec((tile, r.shape[1]), lambda i: (i, 0))


def _const_spec(c):
    return pl.BlockSpec(c.shape, lambda i: (0,) * c.ndim)


def rowwise(fn, rows, consts, outs, *, tile, name):
    arrs, specs = zip(*[_row_spec(r, tile) for r in rows])
    n_rows = arrs[0].shape[0]
    tile = min(tile, n_rows)
    n_in = len(rows) + len(consts)

    def body(*refs):
        res = fn(*[r[...] for r in refs[:n_in]])
        for o_ref, o in zip(refs[n_in:], res):
            o_ref[...] = o.astype(o_ref.dtype)

    arrs, specs = zip(*[_row_spec(r, tile) for r in rows])
    return _pcall(body, name=name,
                  out_shape=[jax.ShapeDtypeStruct((n_rows, w), dt) for w, dt in outs],
                  grid=(n_rows // tile,),
                  in_specs=list(specs) + [_const_spec(c) for c in consts],
                  out_specs=[pl.BlockSpec((tile, w), lambda i: (i, 0)) for w, _ in outs],
                  sem=("parallel",))(*arrs, *consts)


def rowwise_bwd(fn, rows, consts, cts, *, tile, name, row_grads, const_grads, add=None):
    arrs, _ = zip(*[_row_spec(r, tile) for r in rows])
    n_rows = arrs[0].shape[0]
    tile = min(tile, n_rows)
    arrs, specs = zip(*[_row_spec(r, tile) for r in rows])
    ct_arrs, ct_specs = zip(*[_row_spec(c, tile) for c in cts])
    add = add or {}
    add_idx = sorted(add)
    add_arrs, add_specs = (zip(*[_row_spec(add[i], tile) for i in add_idx]) if add_idx else ((), ()))
    nr, nc, nct, na = len(rows), len(consts), len(cts), len(add_idx)
    want_rows = [i for i, d in enumerate(row_grads) if d is not None]
    want_consts = [i for i, w in enumerate(const_grads) if w]

    def body(*refs):
        row_v = [r[...] for r in refs[:nr]]
        const_v = [r[...] for r in refs[nr:nr + nc]]
        ct_v = [r[...] for r in refs[nr + nc:nr + nc + nct]]
        add_v = {i: refs[nr + nc + nct + j][...] for j, i in enumerate(add_idx)}
        out_refs = refs[nr + nc + nct + na:]
        res, vjp = jax.vjp(fn, *row_v, *const_v)
        grads = vjp(tuple(c.astype(o.dtype) for c, o in zip(ct_v, res)))
        for o_ref, i in zip(out_refs, want_rows):
            g = grads[i].astype(F32)
            if i in add_v:
                g = g + add_v[i].astype(F32)
            o_ref[...] = g.astype(o_ref.dtype)
        first = pl.program_id(0) == 0
        for o_ref, i in zip(out_refs[len(want_rows):], want_consts):
            g = grads[nr + i].astype(F32)

            @pl.when(first)
            def _(o_ref=o_ref, g=g):
                o_ref[...] = g

            @pl.when(jnp.logical_not(first))
            def _(o_ref=o_ref, g=g):
                o_ref[...] += g

    def width(r):
        return r[1] if isinstance(r, tuple) else r.shape[1]

    out_shape = [jax.ShapeDtypeStruct((n_rows, width(rows[i])), row_grads[i]) for i in want_rows]
    out_shape += [jax.ShapeDtypeStruct(consts[i].shape, F32) for i in want_consts]
    out_specs = [pl.BlockSpec((tile, width(rows[i])), lambda i_: (i_, 0)) for i in want_rows]
    out_specs += [_const_spec(consts[i]) for i in want_consts]
    return _pcall(body, name=name, out_shape=out_shape, grid=(n_rows // tile,),
                  in_specs=list(specs) + [_const_spec(c) for c in consts] + list(ct_specs) + list(add_specs),
                  out_specs=out_specs, sem=("arbitrary",))(*arrs, *consts, *ct_arrs, *add_arrs)


def f_rmsnorm(x, g):
    x = x.astype(F32)
    return (x * lax.rsqrt(jnp.mean(x * x, axis=-1, keepdims=True) + EPS) * g,)


def _head_sel(first_lane):
    r, c = _iota2((LANES, GDN_WIDTH), 0), _iota2((LANES, GDN_WIDTH), 1)
    return (r == c // HEAD_DIM + first_lane).astype(F32)


def _tri(n, strict=False):
    r, c = _iota2((n, n), 0), _iota2((n, n), 1)
    return r > c if strict else r >= c


def f_gdn_pre(xc, ab, a_log, dt_bias):
    s = _silu(xc.astype(F32))
    qs, ks = [], []
    for h in range(GDN_HEADS):
        qh = s[:, h * HEAD_DIM:(h + 1) * HEAD_DIM]
        kh = s[:, GDN_WIDTH + h * HEAD_DIM:GDN_WIDTH + (h + 1) * HEAD_DIM]
        qs.append(qh * lax.rsqrt(jnp.sum(qh * qh, axis=-1, keepdims=True) + EPS) * (HEAD_DIM ** -0.5))
        ks.append(kh * lax.rsqrt(jnp.sum(kh * kh, axis=-1, keepdims=True) + EPS))
    q, k = jnp.concatenate(qs, axis=1), jnp.concatenate(ks, axis=1)
    v = s[:, 2 * GDN_WIDTH:]
    ab = ab.astype(F32)
    g = -jnp.exp(a_log) * _softplus(ab + dt_bias)
    gc = _head_broadcast(fdot(_tri(GDN_CHUNK).astype(F32), g), 0)
    beta = _head_broadcast(_sigmoid(ab), GDN_HEADS)
    return q, k, v, gc, beta


def _head_broadcast(x, first_lane):
    lane = _iota2(x.shape, 1)
    cols = [jnp.sum(jnp.where(lane == first_lane + h, x, 0.0), axis=1, keepdims=True) for h in range(GDN_HEADS)]
    return jnp.concatenate([jnp.broadcast_to(c, x.shape) for c in cols], axis=1)


def _unit_lower_inverses(neg_lowers):
    C = neg_lowers[0].shape[0]
    eye = (_iota2((C, C), 0) == _iota2((C, C), 1)).astype(F32)
    invs = [eye + n for n in neg_lowers]
    powers = list(neg_lowers)
    for _ in range(6):
        powers = [idot.nn(p, p) for p in powers]
        invs = [inv + idot.nn(p, inv) for p, inv in zip(powers, invs)]
    return invs


@jax.custom_vjp
def _solve_with_inverse(inv, neg_lower, rhs):
    return idot.nn(inv, rhs)


def _solve_fwd(inv, neg_lower, rhs):
    x = idot.nn(inv, rhs)
    return x, (inv, x)


def _solve_bwd(resid, ct):
    inv, x = resid
    d_rhs = idot.tn(inv, ct)
    return jnp.zeros_like(inv), idot.nt(d_rhs, x), d_rhs


_solve_with_inverse.defvjp(_solve_fwd, _solve_bwd)


def f_gdn_intra(q, k, v, gc, beta, inv=None):
    C = GDN_CHUNK
    causal, strict = _tri(C), _tri(C, strict=True)
    is_last = _iota2((C, HEAD_DIM), 0) == C - 1
    heads = range(GDN_HEADS)
    sls = [slice(h * HEAD_DIM, (h + 1) * HEAD_DIM) for h in heads]
    qs, ks, vs, gs, bs = ([a[:, sl] for sl in sls] for a in (q, k, v, gc, beta))
    decays = [jnp.where(causal, jnp.exp(jnp.where(causal, g - g.T, 0.0)), 0.0) for g in gs]
    kbs = [kh * bh for kh, bh in zip(ks, bs)]
    kts = [kh.T for kh in ks]
    neg_lowers = [jnp.where(strict, -(idot(kb, kt) * d), 0.0) for kb, kt, d in zip(kbs, kts, decays)]
    qks = [jnp.where(causal, idot(qh, kt) * d, 0.0) for qh, kt, d in zip(qs, kts, decays)]
    rhss = [jnp.concatenate([vh * bh, kb * jnp.exp(g)], axis=1) for vh, bh, kb, g in zip(vs, bs, kbs, gs)]
    if inv is None:
        invs = _unit_lower_inverses(neg_lowers)
        sols = [idot.nn(m, r) for m, r in zip(invs, rhss)]
    else:
        sols = [_solve_with_inverse(inv[:, sl], n, r) for sl, n, r in zip(sls, neg_lowers, rhss)]
    g_lasts = [jnp.sum(jnp.where(is_last, g, 0.0), axis=0, keepdims=True) for g in gs]
    outs = [[s[:, :HEAD_DIM] for s in sols], [s[:, HEAD_DIM:] for s in sols], qks,
            [kh * jnp.exp(gl - g) for kh, gl, g in zip(ks, g_lasts, gs)], [qh * jnp.exp(g) for qh, g in zip(qs, gs)]]
    if inv is None:
        outs.append(invs)
    return tuple(jnp.concatenate(o, axis=1) for o in outs)


def f_gdn_post(o, z, gain):
    z = z.astype(F32)
    parts = []
    for h in range(GDN_HEADS):
        oh = o[:, h * HEAD_DIM:(h + 1) * HEAD_DIM]
        parts.append(oh * lax.rsqrt(jnp.mean(oh * oh, axis=-1, keepdims=True) + EPS) * gain)
    return (jnp.concatenate(parts, axis=1) * _silu(z),)


def f_mem_attn(q, k, v):
    q = q.astype(F32)
    lane_head = _iota2((1, MEM_WIDTH), 1) // MEM_HEAD_DIM
    kt = k.astype(F32).T
    masks = [(lane_head == h).astype(F32) for h in range(MEM_HEADS)]
    logits = [bdot(q * mask, kt) * (MEM_HEAD_DIM ** -0.5) for mask in masks]
    ps = [jnp.exp(s - jnp.max(s, axis=-1, keepdims=True)) for s in logits]
    ps = [p / jnp.sum(p, axis=-1, keepdims=True) for p in ps]
    outs = [bdot(p, v) * mask for p, mask in zip(ps, masks)]
    return ((outs[0] + outs[1]) + (outs[2] + outs[3]),)


def f_loss(y, t):
    d = y - t
    return (d * d,)


def conv_fwd(proj, w8, *, width, tile=512):
    n_rows = proj.shape[0]
    tile = min(tile, n_rows)

    def body(x_ref, halo_ref, w_ref, o_ref):
        i = pl.program_id(0)
        halo = jnp.where(i > 0, halo_ref[...].astype(F32), 0.0)
        xs = jnp.concatenate([halo, x_ref[...].astype(F32)], axis=0)
        acc = xs[8:] * w_ref[3:4, :]
        for j in range(CONV_WIDTH - 1):
            acc = acc + pltpu.roll(xs, CONV_WIDTH - 1 - j, 0)[8:] * w_ref[j:j + 1, :]
        o_ref[...] = acc

    return _pcall(body, name="gdn_conv_fwd", out_shape=jax.ShapeDtypeStruct((n_rows, width), F32),
                  grid=(n_rows // tile,),
                  in_specs=[pl.BlockSpec((tile, width), lambda i: (i, 0)),
                            pl.BlockSpec((8, width), lambda i: (jnp.maximum(i * (tile // 8) - 1, 0), 0)),
                            pl.BlockSpec((8, width), lambda i: (0, 0))],
                  out_specs=pl.BlockSpec((tile, width), lambda i: (i, 0)), sem=("parallel",))(proj, proj, w8)


def conv_bwd(proj, w8, dy, *, width, tile=512):
    n_rows = proj.shape[0]
    tile = min(tile, n_rows)
    n = n_rows // tile

    def body(x_ref, xhalo_ref, w_ref, dy_ref, dyhalo_ref, dx_ref, dw_ref):
        i = pl.program_id(0)
        dy = dy_ref[...]
        after = jnp.where(i < n - 1, dyhalo_ref[...], 0.0)
        ds = jnp.concatenate([dy, after], axis=0)
        dx = dy * w_ref[3:4, :]
        for j in range(CONV_WIDTH - 1):
            shift = CONV_WIDTH - 1 - j
            dx = dx + pltpu.roll(ds, tile + 8 - shift, 0)[:tile] * w_ref[j:j + 1, :]
        dx_ref[...] = dx.astype(dx_ref.dtype)
        halo = jnp.where(i > 0, xhalo_ref[...].astype(F32), 0.0)
        xs = jnp.concatenate([halo, x_ref[...].astype(F32)], axis=0)
        rows = [jnp.sum(dy * pltpu.roll(xs, CONV_WIDTH - 1 - j, 0)[8:], axis=0, keepdims=True)
                for j in range(CONV_WIDTH - 1)]
        rows.append(jnp.sum(dy * xs[8:], axis=0, keepdims=True))
        dw = jnp.concatenate(rows + [jnp.zeros((8 - CONV_WIDTH, width), F32)], axis=0)

        @pl.when(i == 0)
        def _():
            dw_ref[...] = dw

        @pl.when(i > 0)
        def _():
            dw_ref[...] += dw

    t8 = tile // 8
    return _pcall(body, name="gdn_conv_bwd",
                  out_shape=[jax.ShapeDtypeStruct((n_rows, width), BF16), jax.ShapeDtypeStruct((8, width), F32)],
                  grid=(n,),
                  in_specs=[pl.BlockSpec((tile, width), lambda i: (i, 0)),
                            pl.BlockSpec((8, width), lambda i: (jnp.maximum(i * t8 - 1, 0), 0)),
                            pl.BlockSpec((8, width), lambda i: (0, 0)),
                            pl.BlockSpec((tile, width), lambda i: (i, 0)),
                            pl.BlockSpec((8, width), lambda i: (jnp.minimum((i + 1) * t8, n * t8 - 1), 0))],
                  out_specs=[pl.BlockSpec((tile, width), lambda i: (i, 0)), pl.BlockSpec((8, width), lambda i: (0, 0))],
                  sem=("arbitrary",))(proj, proj, w8, dy, dy)


def gdn_scan_fwd(u, w, qk, kt, qh, gc):
    n_rows = u.shape[0]
    C, n = GDN_CHUNK, u.shape[0] // GDN_CHUNK

    def body(u_ref, w_ref, qk_ref, kt_ref, qh_ref, gc_ref, o_ref, vn_ref, sin_ref, st):
        @pl.when(pl.program_id(0) == 0)
        def _():
            st[...] = jnp.zeros_like(st)

        sin_ref[0] = st[...]
        sls = [slice(h * HEAD_DIM, (h + 1) * HEAD_DIM) for h in range(GDN_HEADS)]
        states = [st[sl, :] for sl in sls]
        v_news = [u_ref[:, sl] - sdot(w_ref[:, sl], s) for sl, s in zip(sls, states)]
        from_state = [sdot(qh_ref[:, sl], s) for sl, s in zip(sls, states)]
        for sl, a, v_new in zip(sls, from_state, v_news):
            o_ref[:, sl] = a + sdot(qk_ref[:, sl], v_new)
            vn_ref[:, sl] = v_new
        for sl, s, v_new in zip(sls, states, v_news):
            st[sl, :] = s * jnp.exp(gc_ref[C - 1:C, sl]) + sdot.tn(kt_ref[:, sl], v_new)

    blk = pl.BlockSpec((C, GDN_WIDTH), lambda i: (i, 0))
    return _pcall(body, name="gdn_scan_fwd",
                  out_shape=[jax.ShapeDtypeStruct((n_rows, GDN_WIDTH), F32), jax.ShapeDtypeStruct((n_rows, GDN_WIDTH), F32),
                             jax.ShapeDtypeStruct((n, GDN_WIDTH, HEAD_DIM), F32)],
                  grid=(n,), in_specs=[blk] * 6,
                  out_specs=[blk, blk, pl.BlockSpec((1, GDN_WIDTH, HEAD_DIM), lambda i: (i, 0, 0))],
                  scratch=[pltpu.VMEM((GDN_WIDTH, HEAD_DIM), F32)], sem=("arbitrary",))(u, w, qk, kt, qh, gc)


def gdn_scan_bwd(do, w, qk, kt, qh, gc, vn, sin):
    n_rows = do.shape[0]
    C, n = GDN_CHUNK, do.shape[0] // GDN_CHUNK

    def body(do_ref, w_ref, qk_ref, kt_ref, qh_ref, gc_ref, vn_ref, sin_ref,
             du_ref, dw_ref, dqk_ref, dkt_ref, dqh_ref, dgl_ref, dst):
        @pl.when(pl.program_id(0) == 0)
        def _():
            dst[...] = jnp.zeros_like(dst)

        sls = [slice(h * HEAD_DIM, (h + 1) * HEAD_DIM) for h in range(GDN_HEADS)]
        dvns = [sdot.tn(qk_ref[:, sl], do_ref[:, sl]) + sdot(kt_ref[:, sl], dst[sl, :]) for sl in sls]
        for sl, dvn in zip(sls, dvns):
            du_ref[:, sl] = dvn
            dw_ref[:, sl] = -sdot.nt(dvn, sin_ref[0, sl, :])
        for sl in sls:
            dqk_ref[:, sl] = sdot.nt(do_ref[:, sl], vn_ref[:, sl])
            dkt_ref[:, sl] = sdot.nt(vn_ref[:, sl], dst[sl, :])
            dqh_ref[:, sl] = sdot.nt(do_ref[:, sl], sin_ref[0, sl, :])
        for sl, dvn in zip(sls, dvns):
            ds_out = dst[sl, :]
            e = jnp.exp(gc_ref[C - 1:C, sl])
            dgl = jnp.sum(ds_out * sin_ref[0, sl, :], axis=0, keepdims=True) * e
            dgl_ref[:, sl] = jnp.broadcast_to(dgl, (8, HEAD_DIM))
            dst[sl, :] = sdot.tn(qh_ref[:, sl], do_ref[:, sl]) + e * ds_out - sdot.tn(w_ref[:, sl], dvn)

    blk = pl.BlockSpec((C, GDN_WIDTH), lambda i: (n - 1 - i, 0))
    row = jax.ShapeDtypeStruct((n_rows, GDN_WIDTH), F32)
    return _pcall(body, name="gdn_scan_bwd",
                  out_shape=[row] * 5 + [jax.ShapeDtypeStruct((n * 8, GDN_WIDTH), F32)],
                  grid=(n,), in_specs=[blk] * 7 + [pl.BlockSpec((1, GDN_WIDTH, HEAD_DIM), lambda i: (n - 1 - i, 0, 0))],
                  out_specs=[blk] * 5 + [pl.BlockSpec((8, GDN_WIDTH), lambda i: (n - 1 - i, 0))],
                  scratch=[pltpu.VMEM((GDN_WIDTH, HEAD_DIM), F32)], sem=("arbitrary",))(do, w, qk, kt, qh, gc, vn, sin)


def gdn_intra_bwd(q, k, v, gc, beta, inv, cts, dgl):
    n_rows = q.shape[0]
    C = GDN_CHUNK

    def body(*refs):
        ins = [r[...] for r in refs[:5]]
        inv_v = refs[5][...]
        ct = tuple(r[...] for r in refs[6:11])
        dgl_v = refs[11][...]
        _, vjp = jax.vjp(lambda *a: f_gdn_intra(*a, inv=inv_v), *ins)
        grads = list(vjp(ct))
        last = _iota2((C, GDN_WIDTH), 0) == C - 1
        grads[3] = grads[3] + jnp.where(last, jnp.broadcast_to(dgl_v[0:1, :], (C, GDN_WIDTH)), 0.0)
        for o_ref, g in zip(refs[12:], grads):
            o_ref[...] = g

    blk = pl.BlockSpec((C, GDN_WIDTH), lambda i: (i, 0))
    return _pcall(body, name="gdn_intra_bwd", out_shape=[jax.ShapeDtypeStruct((n_rows, GDN_WIDTH), F32)] * 5,
                  grid=(n_rows // C,), in_specs=[blk] * 11 + [pl.BlockSpec((8, GDN_WIDTH), lambda i: (i, 0))],
                  out_specs=[blk] * 5, sem=("parallel",))(q, k, v, gc, beta, inv, *cts, dgl)


def fox_gate_fwd(f, b_f):
    n_rows = f.shape[0]
    T = LANES

    def body(f_ref, b_ref, cb_ref, carry):
        @pl.when(pl.program_id(0) == 0)
        def _():
            carry[...] = jnp.zeros_like(carry)

        c = fdot(_tri(T).astype(F32), _log_sigmoid(f_ref[...] + b_ref[...])) + carry[...]
        carry[...] = c[T - 1:T, :]
        cb_ref[...] = fdot(c, _head_sel(0))

    return _pcall(body, name="fox_gate_fwd", out_shape=jax.ShapeDtypeStruct((n_rows, FOX_WIDTH), F32),
                  grid=(n_rows // T,),
                  in_specs=[pl.BlockSpec((T, LANES), lambda i: (i, 0)), pl.BlockSpec((1, LANES), lambda i: (0, 0))],
                  out_specs=pl.BlockSpec((T, FOX_WIDTH), lambda i: (i, 0)),
                  scratch=[pltpu.VMEM((1, LANES), F32)], sem=("arbitrary",))(f, b_f)


def fox_gate_bwd(f, b_f, dcrow, dcb):
    n_rows = f.shape[0]
    T = LANES
    n = n_rows // T

    def body(f_ref, b_ref, dc_ref, dcb_ref, df_ref, db_ref, carry):
        i = pl.program_id(0)

        @pl.when(i == 0)
        def _():
            carry[...] = jnp.zeros_like(carry)

        rows = [dc_ref[h] for h in range(FOX_HEADS)] + [jnp.zeros((T - FOX_HEADS, T), F32)]
        first_lane = (_iota2((FOX_WIDTH, LANES), 0) == _iota2((FOX_WIDTH, LANES), 1) * HEAD_DIM).astype(F32)
        dc = jnp.concatenate(rows, axis=0).T + fdot(dcb_ref[...], first_lane)
        dlog = fdot.tn(_tri(T).astype(F32), dc) + carry[...]
        carry[...] = dlog[0:1, :]
        df = dlog * (1.0 - _sigmoid(f_ref[...] + b_ref[...]))
        df_ref[...] = df
        db = jnp.sum(df, axis=0, keepdims=True)

        @pl.when(i == 0)
        def _():
            db_ref[...] = db

        @pl.when(i > 0)
        def _():
            db_ref[...] += db

    return _pcall(body, name="fox_gate_bwd",
                  out_shape=[jax.ShapeDtypeStruct((n_rows, LANES), F32), jax.ShapeDtypeStruct((1, LANES), F32)],
                  grid=(n,),
                  in_specs=[pl.BlockSpec((T, LANES), lambda i: (n - 1 - i, 0)), pl.BlockSpec((1, LANES), lambda i: (0, 0)),
                            pl.BlockSpec((FOX_HEADS, 1, T), lambda i: (0, 0, n - 1 - i)),
                            pl.BlockSpec((T, FOX_WIDTH), lambda i: (n - 1 - i, 0))],
                  out_specs=[pl.BlockSpec((T, LANES), lambda i: (n - 1 - i, 0)), pl.BlockSpec((1, LANES), lambda i: (0, 0))],
                  scratch=[pltpu.VMEM((1, LANES), F32)], sem=("arbitrary",))(f, b_f, dcrow, dcb)


FOX_AUG = 2 * HEAD_DIM


def _fox_tiles(n_rows):
    return min(1024, n_rows), min(1024, n_rows)


def _fox_pairs(n_rows, query_major):
    tq, tk = _fox_tiles(n_rows)
    nq, r = n_rows // tq, tq // tk
    if query_major:
        pairs = [(i, j) for i in range(nq) for j in range(r * (i + 1))]
    else:
        pairs = [(i, j) for j in range(nq * r) for i in range(j // r, nq)]
    return jnp.asarray([p[0] for p in pairs], jnp.int32), jnp.asarray([p[1] for p in pairs], jnp.int32)


def fox_augment(x, cb, query_side):
    def fn(xt, ct):
        xt = xt.astype(F32)
        lane = _iota2((xt.shape[0], HEAD_DIM), 1)
        parts = []
        for h in range(FOX_HEADS):
            sl = slice(h * HEAD_DIM, (h + 1) * HEAD_DIM)
            c = ct[:, sl]
            hi = c.astype(BF16).astype(F32)
            mid = (c - hi).astype(BF16).astype(F32)
            lo = (c - hi - mid).astype(BF16).astype(F32)
            terms = jnp.where(lane % 3 == 0, hi, jnp.where(lane % 3 == 1, mid, lo))
            if query_side:
                extra = jnp.where(lane < 3, terms, jnp.where(lane < 6, 1.0, 0.0))
                parts += [xt[:, sl] * (HEAD_DIM ** -0.5), extra]
            else:
                extra = jnp.where(lane < 3, 1.0, jnp.where(lane < 6, -terms, 0.0))
                parts += [xt[:, sl], extra]
        return (jnp.concatenate(parts, axis=1),)

    return rowwise(fn, [(x, FOX_WIDTH, 0), cb], [], [(FOX_HEADS * FOX_AUG, BF16)], tile=ROW_TILE,
                   name="fox_augment_q" if query_side else "fox_augment_k")[0]


def _pcall_tables(body, *, name, out_shape, grid, tables, in_specs, out_specs, scratch, sem):
    spec = pltpu.PrefetchScalarGridSpec(num_scalar_prefetch=len(tables), grid=grid, in_specs=in_specs, out_specs=out_specs,
                                        scratch_shapes=list(scratch))
    return pl.pallas_call(body, name=name, out_shape=out_shape, grid_spec=spec,
                          compiler_params=pltpu.CompilerParams(vmem_limit_bytes=VMEM_LIMIT_BYTES, dimension_semantics=sem))


def _fox_logits(qa, ka, offset):
    s = bdot.nt(qa, ka)
    if offset is not None:
        s = jnp.where(_iota2(s.shape, 0) + offset >= _iota2(s.shape, 1), s, NEG_INF)
    return s


def _fox_p_ds(offset, qa_ref, ka_ref, v_ref, o_ref, lse_ref, do_ref):
    s = _fox_logits(qa_ref[...], ka_ref[...], offset)
    p = jnp.exp(s - jnp.tile(lse_ref[...], (1, s.shape[1] // LANES)))
    d_o = do_ref[...].astype(F32)
    delta = jnp.sum(d_o * o_ref[...].astype(F32), axis=-1, keepdims=True)
    return p, p * (bdot.nt(d_o, v_ref[...]) - delta), d_o


def _fox_on_diagonal(i, j, r, tk, step):
    @pl.when(j < r * i)
    def _():
        step(None)

    for m in range(r):
        @pl.when(j == r * i + m)
        def _(m=m):
            step(-m * tk)


def _fox_specs(tq, tk, do_col):
    qaspec = pl.BlockSpec((tq, FOX_AUG), lambda h, p, it, jt: (it[p], h))
    qspec = pl.BlockSpec((tq, HEAD_DIM), lambda h, p, it, jt: (it[p], h))
    dospec = pl.BlockSpec((tq, HEAD_DIM), lambda h, p, it, jt: (it[p], do_col + h))
    kaspec = pl.BlockSpec((tk, FOX_AUG), lambda h, p, it, jt: (jt[p], h))
    kspec = pl.BlockSpec((tk, HEAD_DIM), lambda h, p, it, jt: (jt[p], h))
    vspec = pl.BlockSpec((tk, HEAD_DIM), lambda h, p, it, jt: (jt[p], FOX_HEADS + h))
    cspec = pl.BlockSpec((1, 1, tk), lambda h, p, it, jt: (h, 0, jt[p]))
    return qaspec, qspec, dospec, kaspec, kspec, vspec, cspec


def fox_fwd(qa, kv, ka):
    n_rows = kv.shape[0]
    tq, tk = _fox_tiles(n_rows)
    r = tq // tk
    tables = _fox_pairs(n_rows, True)

    def body(it, jt, qa_ref, ka_ref, v_ref, o_ref, lse_ref, m_sc, l_sc, acc):
        i, j = it[pl.program_id(1)], jt[pl.program_id(1)]

        @pl.when(j == 0)
        def _():
            m_sc[...] = jnp.full(m_sc.shape, NEG_INF, F32)
            l_sc[...] = jnp.zeros_like(l_sc)
            acc[...] = jnp.zeros_like(acc)

        def step(offset):
            s = _fox_logits(qa_ref[...], ka_ref[...], offset)
            m_old = m_sc[...]
            m_new = jnp.maximum(m_old, jnp.max(s, axis=-1, keepdims=True))
            alpha = jnp.exp(m_old - m_new)
            p = jnp.exp(s - jnp.tile(m_new, (1, tk // LANES)))
            l_sc[...] = l_sc[...] * alpha + jnp.sum(p, axis=-1, keepdims=True)
            acc[...] = acc[...] * alpha + bdot(p, v_ref[...])
            m_sc[...] = m_new

        _fox_on_diagonal(i, j, r, tk, step)

        @pl.when(j == r * i + r - 1)
        def _():
            o_ref[...] = (acc[...] / l_sc[...]).astype(o_ref.dtype)
            lse_ref[...] = m_sc[...] + jnp.log(l_sc[...])

    qaspec, qspec, _, kaspec, _, vspec, _ = _fox_specs(tq, tk, 0)
    return _pcall_tables(body, name="fox_fwd",
                         out_shape=[jax.ShapeDtypeStruct((n_rows, FOX_WIDTH), BF16), jax.ShapeDtypeStruct((n_rows, FOX_WIDTH), F32)],
                         grid=(FOX_HEADS, tables[0].shape[0]), tables=tables,
                         in_specs=[qaspec, kaspec, vspec], out_specs=[qspec, qspec],
                         scratch=[pltpu.VMEM((tq, HEAD_DIM), F32)] * 3, sem=("parallel", "arbitrary"))(*tables, qa, ka, kv)


def fox_bwd_dq(qa, kv, ka, o, lse, do, prev=None):
    do, _, do_col = do
    do_col *= FOX_HEADS
    n_rows = kv.shape[0]
    tq, tk = _fox_tiles(n_rows)
    r = tq // tk
    n_prev = 0 if prev is None else 1
    tables = _fox_pairs(n_rows, True)

    def body(it, jt, qa_ref, ka_ref, k_ref, v_ref, o_ref, lse_ref, do_ref, *rest):
        dq_ref, drow_ref, acc, rows = rest[n_prev:]
        i, j = it[pl.program_id(1)], jt[pl.program_id(1)]

        @pl.when(j == 0)
        def _():
            acc[...] = jnp.zeros_like(acc)
            rows[...] = jnp.zeros_like(rows)

        def step(offset):
            _, ds, _ = _fox_p_ds(offset, qa_ref, ka_ref, v_ref, o_ref, lse_ref, do_ref)
            acc[...] += bdot(ds, k_ref[...])
            rows[...] += jnp.sum(ds, axis=-1, keepdims=True)

        _fox_on_diagonal(i, j, r, tk, step)

        @pl.when(j == r * i + r - 1)
        def _():
            dq_ref[...] = (acc[...] * (HEAD_DIM ** -0.5)).astype(dq_ref.dtype)
            drow_ref[...] = rows[...] + rest[0][...] if n_prev else rows[...]

    qaspec, qspec, dospec, kaspec, kspec, vspec, _ = _fox_specs(tq, tk, do_col)
    return _pcall_tables(body, name="fox_bwd_dq" + ("_acc" if n_prev else ""),
                         out_shape=[jax.ShapeDtypeStruct((n_rows, FOX_WIDTH), BF16), jax.ShapeDtypeStruct((n_rows, FOX_WIDTH), F32)],
                         grid=(FOX_HEADS, tables[0].shape[0]), tables=tables,
                         in_specs=[qaspec, kaspec, kspec, vspec, qspec, qspec, dospec] + [qspec] * n_prev,
                         out_specs=[qspec, qspec], scratch=[pltpu.VMEM((tq, HEAD_DIM), F32)] * 2,
                         sem=("parallel", "arbitrary"))(*tables, qa, ka, kv, kv, o, lse, do, *([prev] if n_prev else []))


def fox_bwd_dkv(qa, kv, ka, o, lse, do, prev=None):
    do, _, do_col = do
    do_col *= FOX_HEADS
    n_rows = kv.shape[0]
    tq, tk = _fox_tiles(n_rows)
    nq, r = n_rows // tq, tq // tk
    n_prev = 0 if prev is None else 3
    tables = _fox_pairs(n_rows, False)

    def body(it, jt, qa_ref, ka_ref, v_ref, o_ref, lse_ref, do_ref, *rest):
        prev_refs = rest[:n_prev]
        dk_ref, dv_ref, dc_ref, dk_acc, dv_acc, dc_acc = rest[n_prev:]
        i, j = it[pl.program_id(1)], jt[pl.program_id(1)]

        @pl.when(j >= r * i)
        def _():
            dk_acc[...] = jnp.zeros_like(dk_acc)
            dv_acc[...] = jnp.zeros_like(dv_acc)
            dc_acc[...] = jnp.zeros_like(dc_acc)

        def step(offset):
            p, ds, d_o = _fox_p_ds(offset, qa_ref, ka_ref, v_ref, o_ref, lse_ref, do_ref)
            dv_acc[...] += bdot.tn(p, d_o)
            dk_acc[...] += bdot.tn(ds, qa_ref[:, :HEAD_DIM])
            dc_acc[...] -= jnp.sum(ds, axis=0, keepdims=True)

        _fox_on_diagonal(i, j, r, tk, step)

        @pl.when(i == nq - 1)
        def _():
            dk, dv, dc = dk_acc[...], dv_acc[...], dc_acc[...]
            if n_prev:
                dk, dv, dc = dk + prev_refs[0][...], dv + prev_refs[1][...], dc + prev_refs[2][0]
            dk_ref[...] = dk
            dv_ref[...] = dv
            dc_ref[0] = dc

    qaspec, qspec, dospec, kaspec, kspec, vspec, cspec = _fox_specs(tq, tk, do_col)
    return _pcall_tables(body, name="fox_bwd_dkv" + ("_acc" if n_prev else ""),
                         out_shape=[jax.ShapeDtypeStruct((n_rows, FOX_WIDTH), F32), jax.ShapeDtypeStruct((n_rows, FOX_WIDTH), F32),
                                    jax.ShapeDtypeStruct((FOX_HEADS, 1, n_rows), F32)],
                         grid=(FOX_HEADS, tables[0].shape[0]), tables=tables,
                         in_specs=[qaspec, kaspec, vspec, qspec, qspec, dospec] + [kspec, kspec, cspec][:n_prev],
                         out_specs=[kspec, kspec, cspec],
                         scratch=[pltpu.VMEM((tk, HEAD_DIM), F32), pltpu.VMEM((tk, HEAD_DIM), F32), pltpu.VMEM((1, tk), F32)],
                         sem=("parallel", "arbitrary"))(*tables, qa, ka, kv, o, lse, do, *(prev or ()))


def loss_head(h, gain, target, *, tile=512):
    n_rows, d = h.shape
    tile = min(tile, n_rows)

    def body(h_ref, g_ref, t_ref, part_ref, dy_ref):
        (y,) = f_rmsnorm(h_ref[...], g_ref[...])
        diff = y - t_ref[...]
        dy_ref[...] = diff * (1.0 / d)
        part = jnp.sum(diff * diff, axis=0, keepdims=True)
        first = pl.program_id(0) == 0

        @pl.when(first)
        def _():
            part_ref[...] = part

        @pl.when(jnp.logical_not(first))
        def _():
            part_ref[...] += part

    blk = pl.BlockSpec((tile, d), lambda i: (i, 0))
    one = pl.BlockSpec((1, d), lambda i: (0, 0))
    return _pcall(body, name="loss_head",
                  out_shape=[jax.ShapeDtypeStruct((1, d), F32), jax.ShapeDtypeStruct((n_rows, d), F32)],
                  grid=(n_rows // tile,), in_specs=[blk, one, blk], out_specs=[one, blk], sem=("arbitrary",))(h, gain, target)


def adamw(w, g, m, v, *, name):
    shape = w.shape
    cols = shape[-1] if w.ndim >= 2 else w.size
    rows = w.size // cols
    tile = _pick(rows, (256, 128, 64, 32, 16, 8))
    as2d = lambda a: a.reshape(rows, cols)

    def body(w_ref, g_ref, m_ref, v_ref, d_ref, nm_ref, nv_ref):
        g_ = g_ref[...]
        m_ = ADAM_B1 * m_ref[...] + (1.0 - ADAM_B1) * g_
        v_ = ADAM_B2 * v_ref[...] + (1.0 - ADAM_B2) * (g_ * g_)
        m_hat = m_ / (1.0 - ADAM_B1 ** ADAM_STEP)
        v_hat = v_ / (1.0 - ADAM_B2 ** ADAM_STEP)
        d_ref[...] = -ADAM_LR * (m_hat / (jnp.sqrt(v_hat) + ADAM_EPS) + ADAM_WD * w_ref[...])
        nm_ref[...] = m_
        nv_ref[...] = v_

    blk = pl.BlockSpec((tile, cols), lambda i: (i, 0))
    outs = _pcall(body, name=name, out_shape=[jax.ShapeDtypeStruct((rows, cols), F32)] * 3, grid=(rows // tile,),
                  in_specs=[blk] * 4, out_specs=[blk] * 3, sem=("parallel",))(as2d(w), as2d(g), as2d(m), as2d(v))
    return tuple(o.reshape(shape) for o in outs)


def sum_leading(a, *, name):
    p, r, c = a.shape
    tile = _pick(r, (256, 128, 64, 32, 16, 8))

    def body(a_ref, o_ref):
        total = a_ref[0].astype(F32)
        for k in range(1, p):
            total = total + a_ref[k].astype(F32)
        o_ref[...] = total

    return _pcall(body, name=name, out_shape=jax.ShapeDtypeStruct((r, c), F32), grid=(r // tile,),
                  in_specs=[pl.BlockSpec((p, tile, c), lambda i: (0, i, 0))],
                  out_specs=pl.BlockSpec((tile, c), lambda i: (i, 0)), sem=("parallel",))(a)


_HBM = pl.BlockSpec(memory_space=pltpu.HBM)


def _comm_call(body, *, name, out_shape, n_in, scratch):
    return pl.pallas_call(body, name=name, out_shape=out_shape, in_specs=[_HBM] * n_in, out_specs=_HBM,
                          scratch_shapes=scratch,
                          compiler_params=pltpu.CompilerParams(has_side_effects=True))


def all_gather8(a, *, name):
    m_per, n = a.shape

    def body(x_ref, out_ref, send_sems, recv_sems, local_sem):
        x, y, c = lax.axis_index("x"), lax.axis_index("y"), lax.axis_index("c")
        me, sibling = (x, y, c), (x, y, 1 - c)
        chips = [(1 - x, y), (x, 1 - y), (1 - x, 1 - y)]

        def rows(px, py, pc):
            return out_ref.at[pl.ds((4 * px + 2 * py + pc) * m_per, m_per), :]

        def copy(k, block, to, src=None):
            return pltpu.make_async_remote_copy(
                src_ref=rows(*block) if src is None else src, dst_ref=rows(*block),
                send_sem=send_sems.at[k], recv_sem=recv_sems.at[k], device_id=to, device_id_type=MESH)

        mine = pltpu.make_async_copy(x_ref, rows(*me), local_sem)
        mine.start()
        first = [copy(0, me, sibling, src=x_ref)]
        first += [copy(1 + j, me, (*chip, c), src=x_ref) for j, chip in enumerate(chips)]
        for cp in first:
            cp.start()
        passed = [copy(4 + j, (*chip, c), sibling) for j, chip in enumerate(chips)]
        for j, chip in enumerate(chips):
            copy(1 + j, (*chip, c), me).wait_recv()
            passed[j].start()
        copy(0, sibling, me).wait_recv()
        for j, chip in enumerate(chips):
            copy(4 + j, (*chip, 1 - c), me).wait_recv()
        for cp in first + passed:
            cp.wait_send()
        mine.wait()

    return _comm_call(body, name=name, out_shape=jax.ShapeDtypeStruct((N_DEV * m_per, n), a.dtype), n_in=1,
                      scratch=[pltpu.SemaphoreType.DMA((7,)), pltpu.SemaphoreType.DMA((7,)), pltpu.SemaphoreType.DMA])(a)


PACK_COLS = 1024
PACK_ROW_MULTIPLE = 32

WEIGHT_NAMES = ["ffn1_norm", "ffn1_w_gate_up", "ffn1_w_down", "mix_norm", "ffn2_norm", "ffn2_w_gate_up", "ffn2_w_down",
                "gdn_w_in", "gdn_conv", "gdn_A_log", "gdn_dt_bias", "gdn_out_norm", "fox_w_in", "w_out", "mem_norm",
                "mem_w_kv", "kv_norm", "kv_w", "kv_b_f", "final_norm"]
SHARDED = [("ffn1_w_gate_up", 2), ("ffn1_w_down", 1), ("ffn2_w_gate_up", 2), ("ffn2_w_down", 1), ("gdn_w_in", 2),
           ("gdn_conv", 2), ("fox_w_in", 1), ("w_out", 1), ("mem_w_kv", 1), ("kv_w", 0)]
REPLICATED = [n for n in WEIGHT_NAMES if n not in dict(SHARDED)]


PACK_PIECE_ROWS = 16


def _rows_of(size):
    return -(-size // (PACK_COLS * PACK_PIECE_ROWS)) * PACK_PIECE_ROWS


def pack(pieces, dtype, row_multiple=PACK_ROW_MULTIPLE):
    bufs, total = [], 0
    for p in pieces:
        flat = p.astype(dtype).reshape(-1)
        rows = _rows_of(flat.size)
        bufs.append(jnp.pad(flat, (0, rows * PACK_COLS - flat.size)).reshape(rows, PACK_COLS))
        total += rows
    pad = -total % row_multiple
    if pad:
        bufs.append(jnp.zeros((pad, PACK_COLS), dtype))
    return jnp.concatenate(bufs, axis=0)


def unpack(buf, shapes):
    out, row = [], 0
    for shape in shapes:
        size = 1
        for s in shape:
            size *= s
        rows = _rows_of(size)
        out.append(buf[row:row + rows].reshape(-1)[:size].reshape(shape))
        row += rows
    return out


def _row(vec, width=None):
    vec = vec.astype(F32).reshape(1, -1)
    if width is not None and vec.shape[1] < width:
        vec = jnp.pad(vec, ((0, 0), (0, width - vec.shape[1])))
    return vec


ROW_TILE = 512
GDN_PROJ_WIDTH = 4 * GDN_WIDTH + MEM_WIDTH + LANES
GDN_Z_COL, GDN_QMEM_COL, GDN_AB_COL = 3, 4 * GDN_WIDTH // MEM_WIDTH, (4 * GDN_WIDTH + MEM_WIDTH) // LANES
FOX_QMEM_COL = FOX_WIDTH // MEM_WIDTH
KV_PAD_WIDTH = 2 * FOX_WIDTH + LANES


def rms_fwd(x, gain_row, out_dtype=BF16):
    return rowwise(f_rmsnorm, [x], [gain_row], [(x.shape[1], out_dtype)], tile=ROW_TILE, name="rms_fwd")[0]


def rms_bwd(x, gain_row, dy, dres=None):
    return rowwise_bwd(f_rmsnorm, [x], [gain_row], [dy], tile=ROW_TILE, name="rms_bwd", row_grads=[F32],
                       const_grads=[True], add=None if dres is None else {0: dres})


def _ffn_tiles(n_rows):
    return _pick(n_rows, (512, 256, 128)), _pick(FFN_HIDDEN, (1408, 256, 128))


def ffn_up_act(n, wgu):
    n_rows, d = n.shape
    tm, tn = _ffn_tiles(n_rows)
    nj = FFN_HIDDEN // tn

    def body(n_ref, wg_ref, wu_ref, gu_ref, act_ref):
        x = n_ref[...].astype(BF16)
        g = bdot.nn(x, wg_ref[...])
        u = bdot.nn(x, wu_ref[...])
        gu_ref[0] = g.astype(gu_ref.dtype)
        gu_ref[1] = u.astype(gu_ref.dtype)
        act_ref[...] = (_silu(g) * u).astype(act_ref.dtype)

    return _pcall(body, name="ffn_up_act",
                  out_shape=[jax.ShapeDtypeStruct((2, n_rows, FFN_HIDDEN), BF16), jax.ShapeDtypeStruct((n_rows, FFN_HIDDEN), BF16)],
                  grid=(nj, n_rows // tm),
                  in_specs=[pl.BlockSpec((tm, d), lambda j, i: (i, 0)), pl.BlockSpec((d, tn), lambda j, i: (0, j)),
                            pl.BlockSpec((d, tn), lambda j, i: (0, nj + j))],
                  out_specs=[pl.BlockSpec((2, tm, tn), lambda j, i: (0, i, j)), pl.BlockSpec((tm, tn), lambda j, i: (i, j))],
                  sem=("parallel", "parallel"))(n, wgu, wgu)


def ffn_down_dx_act(dh, wd, gu):
    n_rows, d = dh.shape
    tm, tn = _ffn_tiles(n_rows)

    def body(dh_ref, wd_ref, gu_ref, dgu_ref):
        dact = 0.5 * bdot.nt(dh_ref[...], wd_ref[...])
        gate, up = gu_ref[0].astype(F32), gu_ref[1].astype(F32)
        sg = _sigmoid(gate)
        dgu_ref[0] = (dact * up * (sg * (1.0 + gate * (1.0 - sg)))).astype(dgu_ref.dtype)
        dgu_ref[1] = (dact * (gate * sg)).astype(dgu_ref.dtype)

    blk = pl.BlockSpec((2, tm, tn), lambda j, i: (0, i, j))
    return _pcall(body, name="ffn_down_dx_act", out_shape=jax.ShapeDtypeStruct((2, n_rows, FFN_HIDDEN), BF16),
                  grid=(FFN_HIDDEN // tn, n_rows // tm),
                  in_specs=[pl.BlockSpec((tm, d), lambda j, i: (i, 0)), pl.BlockSpec((tn, d), lambda j, i: (j, 0)), blk],
                  out_specs=blk, sem=("parallel", "parallel"))(dh, wd, gu)


def ffn_fwd(h, gain_row, wgu, wd):
    n = rms_fwd(h, gain_row)
    gu, act = ffn_up_act(n, wgu)
    return mm(act, wd, scale=0.5, res=h, name="ffn_down"), (h, n, gu, act)


def ffn_bwd(dh, saved, gain_row, wgu, wd):
    h, n, gu, act = saved
    dh32, dh16 = dh
    dgu = ffn_down_dx_act(dh16, wd, gu)
    dwd = mm(act, dh16, ta=True, scale=0.5, name="ffn_down_dw")
    dwgu = mm(n, dgu, ta=True, b_split=True, name="ffn_up_dw")
    dh, dgain = mm_rms_bwd(dgu, wgu, h, gain_row, dh32, a_split=True, name="ffn_up_dx")
    return dh, dwgu, dwd, dgain


def gdn_fwd(proj, w8, a_row, dt_row, onorm_row):
    wide = [(GDN_WIDTH, F32)] * 5
    xc = conv_fwd(proj, w8, width=3 * GDN_WIDTH)
    q, k, v, gc, beta = rowwise(f_gdn_pre, [xc, (proj, LANES, GDN_AB_COL)], [a_row, dt_row], wide, tile=GDN_CHUNK,
                                name="gdn_pre_fwd")
    u, w, qk, kt, qh, inv = rowwise(f_gdn_intra, [q, k, v, gc, beta], [], wide + wide[:1], tile=GDN_CHUNK,
                                    name="gdn_intra_fwd")
    o, vn, sin = gdn_scan_fwd(u, w, qk, kt, qh, gc)
    main = rowwise(f_gdn_post, [o, (proj, GDN_WIDTH, GDN_Z_COL)], [onorm_row], [(GDN_WIDTH, BF16)], tile=ROW_TILE,
                   name="gdn_post_fwd")[0]
    return main, (xc, q, k, v, gc, beta, inv, w, qk, kt, qh, vn, sin, o)


def gdn_bwd(dmain, proj, saved, w8, a_row, dt_row, onorm_row):
    xc, q, k, v, gc, beta, inv, w, qk, kt, qh, vn, sin, o = saved
    do, dz, donorm = rowwise_bwd(f_gdn_post, [o, (proj, GDN_WIDTH, GDN_Z_COL)], [onorm_row], [dmain], tile=ROW_TILE,
                                 name="gdn_post_bwd", row_grads=[F32, BF16], const_grads=[True])
    du, dw, dqk, dkt, dqh, dgl = gdn_scan_bwd(do, w, qk, kt, qh, gc, vn, sin)
    dq, dk, dv, dgc, dbeta = gdn_intra_bwd(q, k, v, gc, beta, inv, (du, dw, dqk, dkt, dqh), dgl)
    dxc, dab, da, ddt = rowwise_bwd(f_gdn_pre, [xc, (proj, LANES, GDN_AB_COL)], [a_row, dt_row], [dq, dk, dv, dgc, dbeta],
                                    tile=GDN_CHUNK, name="gdn_pre_bwd", row_grads=[F32, BF16], const_grads=[True, True])
    dqkv, dw8 = conv_bwd(proj, w8, dxc, width=3 * GDN_WIDTH)
    return dqkv, dz, dab, dw8, da, ddt, donorm


def mem_fwd(q, kmem, vmem):
    return rowwise(f_mem_attn, [q], [kmem, vmem], [(MEM_WIDTH, BF16)], tile=ROW_TILE, name="mem_attn_fwd")[0]


def mem_bwd(q, kmem, vmem, dout):
    return rowwise_bwd(f_mem_attn, [q], [kmem, vmem], [dout], tile=ROW_TILE, name="mem_attn_bwd", row_grads=[BF16],
                       const_grads=[True, True])


def forward_backward(xs, mems, target, P):
    depth, n_a = 4, 2
    G = {}
    mem_gain = _row(P["mem_norm"])
    mem_n = rms_fwd(mems, mem_gain)
    h = xs
    saved = []
    shared = None
    for l in range(depth):
        h0 = h
        h1, s1 = ffn_fwd(h0, _row(P["ffn1_norm"][l]), P["ffn1_w_gate_up"][l], P["ffn1_w_down"][l])
        u = rms_fwd(h1, _row(P["mix_norm"][l]))
        kvm = mm(mem_n, P["mem_w_kv"][l], name="mem_kv")
        kmem, vmem = kvm[:, :MEM_WIDTH], kvm[:, MEM_WIDTH:]
        if l < n_a:
            gp = (P["conv8"][l], _row(P["gdn_A_log"][l], LANES), _row(P["gdn_dt_bias"][l], LANES), _row(P["gdn_out_norm"][l]))
            proj = mm(u, P["gdn_w_in_pad"][l], name="gdn_in")
            main, sm = gdn_fwd(proj, *gp)
            qm = (proj, MEM_WIDTH, GDN_QMEM_COL)
        else:
            proj = mm(u, P["fox_w_in"][l - n_a], out_dtype=BF16, name="fox_in")
            kv, ka, cb = shared
            qa = fox_augment(proj, cb, True)
            main, lse = fox_fwd(qa, kv, ka)
            sm = (main, lse, qa)
            qm = (proj, MEM_WIDTH, FOX_QMEM_COL)
        mo = mem_fwd(qm, kmem, vmem)
        cat = jnp.concatenate([main, mo], axis=1)
        h2 = mm(cat, P["w_out"][l], res=h1, name="mix_out")
        h3, s2 = ffn_fwd(h2, _row(P["ffn2_norm"][l]), P["ffn2_w_gate_up"][l], P["ffn2_w_down"][l])
        saved.append((s1, h1, u, kmem, vmem, proj, sm, qm, cat, s2))
        h = h3
        if l == n_a - 1:
            nkv = rms_fwd(h, _row(P["kv_norm"]))
            kv = mm(nkv, P["kv_w_pad"][:, :2 * FOX_WIDTH], out_dtype=BF16, name="fox_kv")
            f = mm(nkv, P["kv_w_pad"][:, 2 * FOX_WIDTH:], name="fox_f")
            bf_row = _row(P["kv_b_f"], LANES)
            cb = fox_gate_fwd(f, bf_row)
            shared = (kv, fox_augment(kv, cb, False), cb)
            kv_saved = (h, nkv, f, bf_row)

    part, dy = loss_head(h, _row(P["final_norm"]), target)
    dh, G["final_norm"] = rms_bwd(h, _row(P["final_norm"]), dy)
    dh = (dh, dh.astype(BF16))

    per_layer = {n: [None] * depth for n in ("ffn1_norm", "ffn1_w_gate_up", "ffn1_w_down", "mix_norm", "ffn2_norm",
                                             "ffn2_w_gate_up", "ffn2_w_down", "w_out", "mem_w_kv")}
    gdn_g = {n: [None] * n_a for n in ("gdn_w_in_pad", "conv8", "gdn_A_log", "gdn_dt_bias", "gdn_out_norm")}
    fox_g = [None] * (depth - n_a)
    dmem_n = None
    dkv_acc = dcb_acc = None
    for l in reversed(range(depth)):
        s1, h1, u, kmem, vmem, proj, sm, qm, cat, s2 = saved[l]
        if l == n_a - 1:
            hk, nkv, f, bf_row = kv_saved
            dk, dv, dcrow = dkv_acc
            df, dbf = fox_gate_bwd(f, bf_row, dcrow, dcb_acc)
            dp = jnp.concatenate([dk.astype(BF16), dv.astype(BF16), df.astype(BF16)], axis=1)
            G["kv_w_pad"] = mm(nkv, dp, ta=True, name="fox_kv_dw")
            G["kv_b_f"] = dbf
            dh, G["kv_norm"] = mm_rms_bwd(dp, P["kv_w_pad"], hk, _row(P["kv_norm"]), dh[0], name="fox_kv_dx")
        dh, per_layer["ffn2_w_gate_up"][l], per_layer["ffn2_w_down"][l], per_layer["ffn2_norm"][l] = ffn_bwd(
            dh, s2, _row(P["ffn2_norm"][l]), P["ffn2_w_gate_up"][l], P["ffn2_w_down"][l])
        dcat = mm(dh[1], P["w_out"][l], tb=True, out_dtype=BF16, name="mix_out_dx")
        per_layer["w_out"][l] = mm(cat, dh[1], ta=True, name="mix_out_dw")
        dqm, dkm, dvm = mem_bwd(qm, kmem, vmem, (dcat, MEM_WIDTH, FOX_QMEM_COL))
        dkvm = jnp.concatenate([dkm, dvm], axis=1)
        per_layer["mem_w_kv"][l] = mm(mem_n, dkvm, ta=True, name="mem_kv_dw")
        dmem_n = mm(dkvm, P["mem_w_kv"][l], tb=True, res=dmem_n, name="mem_kv_dx")
        dmain = (dcat, GDN_WIDTH, 0)
        if l < n_a:
            gp = (P["conv8"][l], _row(P["gdn_A_log"][l], LANES), _row(P["gdn_dt_bias"][l], LANES), _row(P["gdn_out_norm"][l]))
            dqkv, dz, dab, gdn_g["conv8"][l], gdn_g["gdn_A_log"][l], gdn_g["gdn_dt_bias"][l], gdn_g["gdn_out_norm"][l] = gdn_bwd(
                dmain, proj, sm, *gp)
            dproj = jnp.concatenate([dqkv, dz, dqm, dab], axis=1)
            gdn_g["gdn_w_in_pad"][l] = mm(u, dproj, ta=True, name="gdn_in_dw")
            w_in = P["gdn_w_in_pad"][l]
        else:
            o, lse, qa = sm
            kv, ka, _ = shared
            dq, dcb_acc = fox_bwd_dq(qa, kv, ka, o, lse, dmain, dcb_acc)
            dkv_acc = fox_bwd_dkv(qa, kv, ka, o, lse, dmain, dkv_acc)
            dproj = jnp.concatenate([dq, dqm], axis=1)
            fox_g[l - n_a] = mm(u, dproj, ta=True, name="fox_in_dw")
            w_in = P["fox_w_in"][l - n_a]
        dh, per_layer["mix_norm"][l] = mm_rms_bwd(dproj, w_in, h1, _row(P["mix_norm"][l]), dh[0], name="mix_in_dx")
        dh, per_layer["ffn1_w_gate_up"][l], per_layer["ffn1_w_down"][l], per_layer["ffn1_norm"][l] = ffn_bwd(
            dh, s1, _row(P["ffn1_norm"][l]), P["ffn1_w_gate_up"][l], P["ffn1_w_down"][l])

    (G["mem_norm"],) = rowwise_bwd(f_rmsnorm, [mems], [mem_gain], [dmem_n], tile=ROW_TILE, name="mem_norm_bwd",
                                   row_grads=[None], const_grads=[True])
    for n, v in per_layer.items():
        G[n] = jnp.stack(v)
    for n, v in gdn_g.items():
        G[n] = jnp.stack(v)
    G["fox_w_in"] = jnp.stack(fox_g)
    return part, dh[0], G


_GDN_O0 = 4 * GDN_WIDTH
_GDN_O1 = _GDN_O0 + 2 * GDN_HEADS
_KV_WIDTH = 2 * FOX_WIDTH + FOX_HEADS


def derived_weights(gdn_w_in, gdn_conv, kv_w=None):
    zeros = jnp.zeros(gdn_w_in.shape[:-1] + (LANES - 2 * GDN_HEADS,), gdn_w_in.dtype)
    out = dict(
        gdn_w_in_pad=jnp.concatenate([gdn_w_in[..., :_GDN_O0], gdn_w_in[..., _GDN_O1:], gdn_w_in[..., _GDN_O0:_GDN_O1], zeros], axis=-1),
        conv8=jnp.pad(gdn_conv.astype(F32), ((0, 0), (0, 8 - CONV_WIDTH), (0, 0))))
    if kv_w is not None:
        out["kv_w_pad"] = jnp.pad(kv_w, ((0, 0), (0, KV_PAD_WIDTH - _KV_WIDTH)))
    return out


def reference_layout(G):
    gp = G["gdn_w_in_pad"]
    out = dict(G)
    out["gdn_w_in"] = jnp.concatenate([gp[..., :_GDN_O0], gp[..., _GDN_O0 + MEM_WIDTH:_GDN_O0 + MEM_WIDTH + 2 * GDN_HEADS],
                                       gp[..., _GDN_O0:_GDN_O0 + MEM_WIDTH]], axis=-1)
    out["gdn_conv"] = G["conv8"][:, :CONV_WIDTH]
    out["kv_w"] = G["kv_w_pad"][:, :_KV_WIDTH]
    out["gdn_A_log"] = G["gdn_A_log"][:, 0, :GDN_HEADS]
    out["gdn_dt_bias"] = G["gdn_dt_bias"][:, 0, :GDN_HEADS]
    out["gdn_out_norm"] = G["gdn_out_norm"][:, 0, :]
    out["kv_b_f"] = G["kv_b_f"][0, :FOX_HEADS]
    for n in ("ffn1_norm", "mix_norm", "ffn2_norm"):
        out[n] = G[n][:, 0, :]
    for n in ("mem_norm", "kv_norm", "final_norm"):
        out[n] = G[n][0]
    return {n: out[n] for n in WEIGHT_NAMES}


EXCHANGED = [("ffn1_w_gate_up", 2, 0), ("ffn1_w_down", 1, 0), ("ffn2_w_gate_up", 2, 0), ("ffn2_w_down", 1, 0),
             ("gdn_w_in", 2, 0), ("fox_w_in", 1, 0), ("w_out", 1, 0), ("mem_w_kv", 1, 0), ("kv_w", 0, 1)]
GDN_IN_SHARD = (4 * GDN_WIDTH + 2 * GDN_HEADS + MEM_WIDTH) // N_CHIPS
GDN_IN_SLOT = 896


def _slab(ref, axis_slices):
    idx = [slice(None)] * len(ref.shape)
    for axis, (start, size) in axis_slices.items():
        idx[axis] = pl.ds(start, size)
    return ref.at[tuple(idx)]


def _comm_multi(body, *, name, n_in, out_shapes, scratch):
    return pl.pallas_call(body, name=name, out_shape=out_shapes, in_specs=[_HBM] * n_in, out_specs=[_HBM] * len(out_shapes),
                          scratch_shapes=scratch, compiler_params=pltpu.CompilerParams(has_side_effects=True))


def gather_shards(shards, layout):
    n = len(shards)
    fulls = [tuple(d * (N_CHIPS if a == sa else 1) for a, d in enumerate(s.shape)) for s, (sa, _) in zip(shards, layout)]
    n_sem = 9

    def pieces(w):
        sa, ha = layout[w]
        axis = 3 - sa - ha
        size = shards[w].shape[axis]
        unit = LANES if axis == 2 else 16
        first = -(-(size // 2) // unit) * unit
        return axis, [(0, first), (first, size - first)]

    def body(*refs):
        ins, outs = refs[:n], refs[n:2 * n]
        send_sems, recv_sems, local_sems = refs[2 * n:]
        x, y, c = lax.axis_index("x"), lax.axis_index("y"), lax.axis_index("c")
        me, sibling, x_nbr, y_nbr = (x, y, c), (x, y, 1 - c), (1 - x, y, c), (x, 1 - y, c)
        chip_x, chip_y, chip_d = (1 - x, y), (x, 1 - y), (1 - x, 1 - y)

        def region(w, chip, pc, piece=None):
            (sa, ha), shard = layout[w], shards[w].shape
            where = {sa: ((2 * chip[0] + chip[1]) * shard[sa], shard[sa]), ha: (pc * (shard[ha] // 2), shard[ha] // 2)}
            if piece is not None:
                axis, parts = pieces(w)
                where[axis] = parts[piece]
            return _slab(outs[w], where)

        def my_half(w):
            ha, shard = layout[w][1], shards[w].shape
            return _slab(ins[w], {ha: (c * (shard[ha] // 2), shard[ha] // 2)})

        def copy(w, k, where, to, src=None):
            return pltpu.make_async_remote_copy(
                src_ref=where if src is None else src, dst_ref=where, send_sem=send_sems.at[n_sem * w + k],
                recv_sem=recv_sems.at[n_sem * w + k], device_id=to, device_id_type=MESH)

        mine, sends = [], [[] for _ in range(n)]
        for w in range(n):
            mine.append(pltpu.make_async_copy(my_half(w), region(w, (x, y), c), local_sems.at[w]))
            mine[w].start()
            sends[w] = [copy(w, k, region(w, (x, y), c), to, src=my_half(w)) for k, to in enumerate((sibling, x_nbr, y_nbr))]
            for cp in sends[w]:
                cp.start()
        for w in range(n):
            copy(w, 1, region(w, chip_x, c), me).wait_recv()
            onward = [copy(w, 3, region(w, chip_x, c, 0), y_nbr), copy(w, 5, region(w, chip_x, c), sibling)]
            for cp in onward:
                cp.start()
            sends[w] += onward
            copy(w, 2, region(w, chip_y, c), me).wait_recv()
            onward = [copy(w, 4, region(w, chip_y, c, 1), x_nbr), copy(w, 6, region(w, chip_y, c), sibling)]
            for cp in onward:
                cp.start()
            sends[w] += onward
        for w in range(n):
            copy(w, 3, region(w, chip_d, c, 0), me).wait_recv()
            copy(w, 4, region(w, chip_d, c, 1), me).wait_recv()
            onward = [copy(w, 7, region(w, chip_d, c, 0), sibling), copy(w, 8, region(w, chip_d, c, 1), sibling)]
            for cp in onward:
                cp.start()
            sends[w] += onward
        for w in range(n):
            copy(w, 0, region(w, (x, y), 1 - c), me).wait_recv()
            copy(w, 5, region(w, chip_x, 1 - c), me).wait_recv()
            copy(w, 6, region(w, chip_y, 1 - c), me).wait_recv()
            copy(w, 7, region(w, chip_d, 1 - c, 0), me).wait_recv()
            copy(w, 8, region(w, chip_d, 1 - c, 1), me).wait_recv()
        for w in range(n):
            for cp in sends[w]:
                cp.wait_send()
            mine[w].wait()

    return _comm_multi(body, name="gather_shards", n_in=n,
                       out_shapes=[jax.ShapeDtypeStruct(f, s.dtype) for f, s in zip(fulls, shards)],
                       scratch=[pltpu.SemaphoreType.DMA((n_sem * n,)), pltpu.SemaphoreType.DMA((n_sem * n,)),
                                pltpu.SemaphoreType.DMA((n,))])(*shards)


def swap_other_halves(arrays, layout):
    n = len(arrays)
    halves = [tuple(d // 2 if a == ha else d for a, d in enumerate(g.shape)) for g, (_, ha) in zip(arrays, layout)]

    def body(*refs):
        ins, outs = refs[:n], refs[n:2 * n]
        send_sems, recv_sems = refs[2 * n:]
        x, y, c = lax.axis_index("x"), lax.axis_index("y"), lax.axis_index("c")
        copies = []
        for w in range(n):
            ha, size = layout[w][1], halves[w][layout[w][1]]
            copies.append(pltpu.make_async_remote_copy(
                src_ref=_slab(ins[w], {ha: ((1 - c) * size, size)}), dst_ref=outs[w], send_sem=send_sems.at[w],
                recv_sem=recv_sems.at[w], device_id=(x, y, 1 - c), device_id_type=MESH))
            copies[w].start()
        for cp in copies:
            cp.wait()

    return _comm_multi(body, name="grad_pair_swap", n_in=n,
                       out_shapes=[jax.ShapeDtypeStruct(h, g.dtype) for h, g in zip(halves, arrays)],
                       scratch=[pltpu.SemaphoreType.DMA((n,)), pltpu.SemaphoreType.DMA((n,))])(*arrays)


def scatter_to_chips(arrays, layout):
    n = len(arrays)
    slabs = [tuple(d // N_CHIPS if a == sa else d for a, d in enumerate(p.shape)) for p, (sa, _) in zip(arrays, layout)]

    def body(*refs):
        ins, outs = refs[:n], refs[n:2 * n]
        send_sems, recv_sems, local_sems = refs[2 * n:]
        x, y, c = lax.axis_index("x"), lax.axis_index("y"), lax.axis_index("c")
        me = 2 * x + y

        def slab(w, k):
            sa, size = layout[w][0], slabs[w][layout[w][0]]
            return _slab(ins[w], {sa: (k * size, size)})

        local, copies = [], []
        for w in range(n):
            local.append(pltpu.make_async_copy(slab(w, me), outs[w].at[me], local_sems.at[w]))
            local[w].start()
            for j, (px, py) in enumerate([(1 - x, y), (x, 1 - y), (1 - x, 1 - y)]):
                copies.append(pltpu.make_async_remote_copy(
                    src_ref=slab(w, 2 * px + py), dst_ref=outs[w].at[me], send_sem=send_sems.at[3 * w + j],
                    recv_sem=recv_sems.at[3 * w + j], device_id=(px, py, c), device_id_type=MESH))
                copies[-1].start()
        for cp in copies:
            cp.wait()
        for cp in local:
            cp.wait()

    return _comm_multi(body, name="grad_all_to_all", n_in=n,
                       out_shapes=[jax.ShapeDtypeStruct((N_CHIPS,) + s, p.dtype) for s, p in zip(slabs, arrays)],
                       scratch=[pltpu.SemaphoreType.DMA((3 * n,)), pltpu.SemaphoreType.DMA((3 * n,)),
                                pltpu.SemaphoreType.DMA((n,))])(*arrays)


def swap_with_sibling(arrays):
    n = len(arrays)

    def body(*refs):
        ins, outs = refs[:n], refs[n:2 * n]
        send_sems, recv_sems = refs[2 * n:]
        x, y, c = lax.axis_index("x"), lax.axis_index("y"), lax.axis_index("c")
        copies = [pltpu.make_async_remote_copy(src_ref=ins[w], dst_ref=outs[w], send_sem=send_sems.at[w], recv_sem=recv_sems.at[w],
                                               device_id=(x, y, 1 - c), device_id_type=MESH) for w in range(n)]
        for cp in copies:
            cp.start()
        for cp in copies:
            cp.wait()

    return _comm_multi(body, name="grad_half_swap", n_in=n, out_shapes=[jax.ShapeDtypeStruct(a.shape, a.dtype) for a in arrays],
                       scratch=[pltpu.SemaphoreType.DMA((n,)), pltpu.SemaphoreType.DMA((n,))])(*arrays)


def join_halves(mine, other, half_axis, c_arr, *, name):
    a0, a1, a2 = mine.shape
    tile = _row_tile(a1, a2)

    def body(c_ref, m_ref, o_ref, out_ref):
        for half in range(2):
            @pl.when(c_ref[0] == half)
            def _(half=half):
                out_ref[half] = m_ref[...]
                out_ref[1 - half] = o_ref[...]

    blk = pl.BlockSpec((None, tile, a2), lambda i, j, c_ref: (i, j, 0))
    if half_axis == 0:
        out_shape, out_blk = (2, a0, a1, a2), pl.BlockSpec((2, None, tile, a2), lambda i, j, c_ref: (0, i, j, 0))
    else:
        out_shape, out_blk = (a0, 2, a1, a2), pl.BlockSpec((None, 2, tile, a2), lambda i, j, c_ref: (i, 0, j, 0))
    spec = pltpu.PrefetchScalarGridSpec(num_scalar_prefetch=1, grid=(a0, a1 // tile), in_specs=[blk, blk], out_specs=out_blk)
    out = pl.pallas_call(body, name=name, out_shape=jax.ShapeDtypeStruct(out_shape, mine.dtype), grid_spec=spec,
                         compiler_params=pltpu.CompilerParams(vmem_limit_bytes=VMEM_LIMIT_BYTES,
                                                              dimension_semantics=("parallel", "parallel")))(c_arr, mine, other)
    return out.reshape((2 * a0, a1, a2) if half_axis == 0 else (a0, 2 * a1, a2))


def _row_tile(rows, cols, itemsize=4, budget=2 * 1024 * 1024):
    for t in (1024, 512, 256, 128, 64, 32, 16):
        if rows % t == 0 and t * cols * itemsize <= budget:
            return t
    return rows


def add_own_half(full, recv, half_axis, c_arr, *, name):
    a0, a1, a2 = recv.shape
    tile = _row_tile(a1, a2)

    def body(c_ref, f_ref, r_ref, o_ref):
        o_ref[...] = (f_ref[...] + r_ref[...]).astype(o_ref.dtype)

    if half_axis == 0:
        f_spec = pl.BlockSpec((None, tile, a2), lambda i, j, c_ref: (c_ref[0] * a0 + i, j, 0))
    else:
        f_spec = pl.BlockSpec((None, tile, a2), lambda i, j, c_ref: (i, c_ref[0] * (a1 // tile) + j, 0))
    blk = pl.BlockSpec((None, tile, a2), lambda i, j, c_ref: (i, j, 0))
    spec = pltpu.PrefetchScalarGridSpec(num_scalar_prefetch=1, grid=(a0, a1 // tile), in_specs=[f_spec, blk], out_specs=blk)
    return pl.pallas_call(body, name=name, out_shape=jax.ShapeDtypeStruct(recv.shape, BF16), grid_spec=spec,
                          compiler_params=pltpu.CompilerParams(vmem_limit_bytes=VMEM_LIMIT_BYTES,
                                                               dimension_semantics=("parallel", "parallel")))(c_arr, full, recv)


def sum_slots(q, *, name):
    _, a0, a1, a2 = q.shape
    tile = _row_tile(a1, a2, budget=1024 * 1024)

    def body(q_ref, o_ref):
        total = q_ref[0].astype(F32)
        for k in range(1, N_CHIPS):
            total = total + q_ref[k].astype(F32)
        o_ref[...] = total

    return _pcall(body, name=name, out_shape=jax.ShapeDtypeStruct((a0, a1, a2), F32), grid=(a0, a1 // tile),
                  in_specs=[pl.BlockSpec((N_CHIPS, None, tile, a2), lambda i, j: (0, i, j, 0))],
                  out_specs=pl.BlockSpec((None, tile, a2), lambda i, j: (i, j, 0)), sem=("parallel", "parallel"))(q)


def gather_weights(W):
    shards = []
    for name, _, _ in EXCHANGED:
        w = W[name].astype(BF16)
        if name == "gdn_w_in":
            w = jnp.pad(w, ((0, 0), (0, 0), (0, GDN_IN_SLOT - GDN_IN_SHARD)))
        if name == "kv_w":
            w = jnp.pad(w, ((0, 0), (0, KV_PAD_WIDTH - _KV_WIDTH)))[None]
        shards.append(w)
    fulls = dict(zip([n for n, _, _ in EXCHANGED], gather_shards(shards, [(sa, ha) for _, sa, ha in EXCHANGED])))
    slots = fulls["gdn_w_in"]
    fulls["gdn_w_in"] = jnp.concatenate([slots[..., k * GDN_IN_SLOT:k * GDN_IN_SLOT + GDN_IN_SHARD] for k in range(N_CHIPS)], axis=-1)
    fulls["kv_w_pad"] = fulls.pop("kv_w").reshape(-1, KV_PAD_WIDTH)
    conv = pack([W["gdn_conv"]], F32, row_multiple=8)
    conv_all = all_gather8(conv, name="gather_conv").reshape(N_DEV, conv.shape[0], PACK_COLS)
    fulls["gdn_conv"] = jnp.concatenate([unpack(conv_all[2 * k], [W["gdn_conv"].shape])[0] for k in range(N_CHIPS)], axis=-1)
    return fulls


def reduce_gradients(G):
    layout = [(sa, ha) for _, sa, ha in EXCHANGED]
    c_arr = lax.axis_index("c").astype(jnp.int32).reshape(1)
    received = swap_other_halves(G, layout)
    pairs = [add_own_half(g, r, ha, c_arr, name="grad_pair_sum") for g, r, (_, ha) in zip(G, received, layout)]
    slots = scatter_to_chips(pairs, layout)
    halves = [sum_slots(q, name="grad_chip_sum") for q in slots]
    others = swap_with_sibling(halves)
    return [join_halves(h, o, ha, c_arr, name="grad_join_halves") for h, o, (_, ha) in zip(halves, others, layout)]


def allreduce_small(G, names):
    packed = pack([G[n] for n in names], F32, row_multiple=8)
    gathered = all_gather8(packed, name="gather_small_grads").reshape(N_DEV, packed.shape[0], PACK_COLS)
    total = sum_leading(gathered, name="sum_small_grads")
    return dict(zip(names, unpack(total, [G[n].shape for n in names])))


def kernel(x, mem, *rest):
    n_w = len(WEIGHT_NAMES)
    W = dict(zip(WEIGHT_NAMES, rest[:n_w]))
    target = rest[n_w]
    M = dict(zip(WEIGHT_NAMES, rest[n_w + 1:2 * n_w + 1]))
    V = dict(zip(WEIGHT_NAMES, rest[2 * n_w + 1:3 * n_w + 1]))

    full = gather_weights(W)
    P = {n: W[n] for n in REPLICATED}
    P.update({n: full[n] for n in ("ffn1_w_gate_up", "ffn1_w_down", "ffn2_w_gate_up", "ffn2_w_down", "fox_w_in", "w_out",
                                   "mem_w_kv", "kv_w_pad")})
    derived = derived_weights(full["gdn_w_in"], full["gdn_conv"])
    P.update(gdn_w_in_pad=derived["gdn_w_in_pad"], conv8=derived["conv8"])

    part, dx, G = forward_backward(x[0], mem[0], target[0], P)
    loss = lax.psum(0.5 / x.shape[-1] * jnp.sum(part), ("x", "y", "c"))

    ref = reference_layout(G)
    exchange = {n: ref[n] for n, _, _ in EXCHANGED}
    exchange["gdn_w_in"] = jnp.concatenate(
        [jnp.pad(ref["gdn_w_in"][..., k * GDN_IN_SHARD:(k + 1) * GDN_IN_SHARD], ((0, 0), (0, 0), (0, GDN_IN_SLOT - GDN_IN_SHARD)))
         for k in range(N_CHIPS)], axis=-1)
    exchange["kv_w"] = G["kv_w_pad"].reshape(N_CHIPS, -1, KV_PAD_WIDTH)
    shards = dict(zip([n for n, _, _ in EXCHANGED], reduce_gradients([exchange[n] for n, _, _ in EXCHANGED])))
    shards["gdn_w_in"] = shards["gdn_w_in"][..., :GDN_IN_SHARD]
    shards["kv_w"] = shards["kv_w"][0, :, :_KV_WIDTH]
    grads = allreduce_small(ref, REPLICATED + ["gdn_conv"])
    conv_cols = W["gdn_conv"].shape[-1]
    chip = 2 * lax.axis_index("x") + lax.axis_index("y")
    grads["gdn_conv"] = lax.dynamic_slice_in_dim(grads["gdn_conv"], chip * conv_cols, conv_cols, axis=2)
    grads.update(shards)

    outs = {n: adamw(W[n], grads[n], M[n], V[n], name="adamw_" + n) for n in WEIGHT_NAMES}
    return (loss, dx[None], *[grads[n] for n in WEIGHT_NAMES], *[outs[n][0] for n in WEIGHT_NAMES],
            *[outs[n][1] for n in WEIGHT_NAMES], *[outs[n][2] for n in WEIGHT_NAMES])
```
